```python
import jax
import jax.numpy as jnp
from jax import lax
import numpy as np

D_MODEL = 1024
BATCH = 8
SEQ = 16384
DEPTH = 2

GRID_W = 64
CTX_LEN = 256
N_MOD = 9
D_FF = 2816
NA_HEADS = 8
NA_HEAD_DIM = 64
NA_WIDTH = NA_HEADS * NA_HEAD_DIM
NA_KH = 8
NA_KW = 16
POOL_WINDOWS = (2, 4, 8, 16)
POOL_GROUPS = len(POOL_WINDOWS)
POOL_WIDTH = D_MODEL - NA_WIDTH
POOL_GROUP_DIM = POOL_WIDTH // POOL_GROUPS
EVEN_IN_WIDTH = 3 * NA_WIDTH + POOL_WIDTH
EVEN_MIX_WIDTH = NA_WIDTH + POOL_WIDTH
CONV_WIDTH = D_MODEL
CONV_K = 3
N_EVEN = (DEPTH + 1) // 2
N_ODD = DEPTH // 2
RMS_EPS = 1e-6
NEG_INF = -1e30

kernel_name = "hybrid_natten_pool_shortconv_dit"


def rms_norm(x, g):
    x32 = x.astype(jnp.float32)
    y = x32 * lax.rsqrt(jnp.mean(x32 * x32, axis=-1, keepdims=True) + RMS_EPS)
    return (y * g.astype(jnp.float32)).astype(x.dtype)


def modulate(h, shift, scale):
    return h * (1 + scale) + shift


def adaln(cond, w, b):
    m = jax.nn.silu(cond) @ w + b
    return m.reshape(m.shape[:-1] + (N_MOD, D_MODEL))


def swiglu(h, w13, w2):
    a, b = jnp.split(h @ w13, 2, axis=-1)
    return (jax.nn.silu(a) * b) @ w2


def ffn_sublayer(h, m, g, w13, w2, base):
    hn = modulate(rms_norm(h, g), m[:, :, base], m[:, :, base + 1])
    return h + 0.5 * m[:, :, base + 2] * swiglu(hn, w13, w2)


def split_heads(t):
    return t.reshape(t.shape[:2] + (NA_HEADS, NA_HEAD_DIM))


def neighbourhood_attention(q, k, v, k_ctx, v_ctx, rpb):
    B, L, H, dh = q.shape
    R = L // GRID_W
    kh = min(NA_KH, R)
    kw = NA_KW
    scale = dh ** -0.5
    qg = q.reshape(B, R, GRID_W, H, dh)
    kg = k.reshape(B, R, GRID_W, H, dh)
    vg = v.reshape(B, R, GRID_W, H, dh)
    r = jnp.arange(R)
    row_start = jnp.clip(r - kh // 2, 0, R - kh)
    ridx = row_start[:, None] + jnp.arange(kh)[None, :]
    k_blk = kg[:, ridx]
    v_blk = vg[:, ridx]
    col = jnp.arange(GRID_W)
    col_start = jnp.clip(col - kw // 2, 0, GRID_W - kw)
    col_ok = (col[None, :] >= col_start[:, None]) & (col[None, :] < col_start[:, None] + kw)
    ri = ridx - r[:, None] + (NA_KH - 1)
    ci = jnp.clip(col[None, :] - col[:, None] + (NA_KW - 1), 0, 2 * NA_KW - 2)
    bias = rpb[:, ri[:, None, :, None], ci[None, :, None, :]].astype(jnp.float32)
    s_win = jnp.einsum('brqhd,brkwhd->bhrqkw', qg, k_blk).astype(jnp.float32) * scale + bias
    s_win = jnp.where(col_ok[:, None, :], s_win, NEG_INF)
    n_win = kh * GRID_W
    s_win = s_win.reshape(B, H, R, GRID_W, n_win)
    s_ctx = jnp.einsum('brqhd,bchd->bhrqc', qg, k_ctx).astype(jnp.float32) * scale
    p = jax.nn.softmax(jnp.concatenate([s_win, s_ctx], axis=-1), axis=-1).astype(v.dtype)
    p_win = p[..., :n_win].reshape(B, H, R, GRID_W, kh, GRID_W)
    o = (jnp.einsum('bhrqkw,brkwhd->brqhd', p_win, v_blk)
         + jnp.einsum('bhrqc,bchd->brqhd', p[..., n_win:], v_ctx))
    return o.reshape(B, L, H * dh)


def context_attention(q, k, v):
    B, C, H, dh = q.shape
    s = jnp.einsum('bqhd,bkhd->bhqk', q, k).astype(jnp.float32) * dh ** -0.5
    p = jax.nn.softmax(s, axis=-1).astype(v.dtype)
    return jnp.einsum('bhqk,bkhd->bqhd', p, v).reshape(B, C, H * dh)


def multiscale_pool(u, pool_w, pool_scale):
    B, L, _ = u.shape
    t = jnp.arange(L)
    ug = u.reshape(B, L, POOL_GROUPS, POOL_GROUP_DIM)
    outs = []
    for g, w in enumerate(POOL_WINDOWS):
        xg = ug[:, :, g].astype(jnp.float32)
        cs = jnp.pad(jnp.cumsum(xg, axis=1), ((0, 0), (1, 0), (0, 0)))
        lo = jnp.clip(t - w // 2, 0, L)
        hi = jnp.clip(t - w // 2 + w, 0, L)
        cnt = (hi - lo).astype(jnp.float32)[None, :, None]
        mean = (jnp.take(cs, hi, axis=1) - jnp.take(cs, lo, axis=1)) / cnt
        outs.append((mean - xg).astype(u.dtype) @ pool_w[g])
    return jnp.concatenate(outs, axis=-1) * pool_scale


def even_mixer(hl, hc, w_in, w_out, rpb, pool_w, pool_scale, ctx_out):
    q, k, v, u = jnp.split(hl @ w_in, [NA_WIDTH, 2 * NA_WIDTH, 3 * NA_WIDTH], axis=-1)
    k_c, v_c = jnp.split(hc @ w_in[:, NA_WIDTH:3 * NA_WIDTH], 2, axis=-1)
    k_c, v_c = split_heads(k_c), split_heads(v_c)
    att = neighbourhood_attention(split_heads(q), split_heads(k), split_heads(v), k_c, v_c, rpb)
    pool = multiscale_pool(u, pool_w, pool_scale)
    y_lat = jnp.concatenate([att, pool], axis=-1) @ w_out
    y_ctx = None
    if ctx_out:
        q_c = split_heads(hc @ w_in[:, :NA_WIDTH])
        u_c = hc @ w_in[:, 3 * NA_WIDTH:]
        att_c = context_attention(q_c, k_c, v_c)
        pool_c = multiscale_pool(u_c, pool_w, pool_scale)
        y_ctx = jnp.concatenate([att_c, pool_c], axis=-1) @ w_out
    return y_lat, y_ctx


def short_conv_mixer(h, w_in, conv_w, w_out):
    bg, cg, xin = jnp.split(h @ w_in, 3, axis=-1)
    z = cg * xin
    L = h.shape[1]
    zp = jnp.pad(z, ((0, 0), (1, 1), (0, 0)))
    y = zp[:, 0:L] * conv_w[0] + zp[:, 1:L + 1] * conv_w[1] + zp[:, 2:L + 2] * conv_w[2]
    return (bg * y) @ w_out


def _fwd_setup_inputs(seed: int = 0) -> dict:
    key = jax.random.key(seed)
    ks = jax.random.split(key, 18)
    D = D_MODEL

    def nrm(k, shape, s):
        return jax.random.normal(k, shape, jnp.float32) * s

    return {
        "x": nrm(ks[0], (BATCH, SEQ, D), 1.0),
        "c": nrm(ks[1], (BATCH, D), 1.0),
        "ctx": nrm(ks[2], (BATCH, CTX_LEN, D), 1.0),
        "c_ctx": nrm(ks[3], (D,), 1.0),
        "mod_w": nrm(ks[4], (DEPTH, D, N_MOD * D), 0.5 * D ** -0.5),
        "mod_b": nrm(ks[5], (DEPTH, N_MOD * D), 0.01),
        "norm_g": 1.0 + nrm(ks[6], (DEPTH, 3, D), 0.02),
        "ffn_w13": nrm(ks[7], (DEPTH, 2, D, 2 * D_FF), D ** -0.5),
        "ffn_w2": nrm(ks[8], (DEPTH, 2, D_FF, D), D_FF ** -0.5),
        "even_w_in": nrm(ks[9], (N_EVEN, D, EVEN_IN_WIDTH), D ** -0.5),
        "even_w_out": nrm(ks[10], (N_EVEN, EVEN_MIX_WIDTH, D), EVEN_MIX_WIDTH ** -0.5),
        "na_rpb": nrm(ks[11], (N_EVEN, NA_HEADS, 2 * NA_KH - 1, 2 * NA_KW - 1), 0.1),
        "pool_w": nrm(ks[12], (N_EVEN, POOL_GROUPS, POOL_GROUP_DIM, POOL_GROUP_DIM), POOL_GROUP_DIM ** -0.5),
        "pool_scale": 1.0 + nrm(ks[13], (N_EVEN, POOL_WIDTH), 0.1),
        "conv_w_in": nrm(ks[14], (N_ODD, D, 3 * CONV_WIDTH), D ** -0.5),
        "conv_w": nrm(ks[15], (N_ODD, CONV_K, CONV_WIDTH), CONV_K ** -0.5),
        "conv_w_out": nrm(ks[16], (N_ODD, CONV_WIDTH, D), CONV_WIDTH ** -0.5),
        "final_g": 1.0 + nrm(ks[17], (D,), 0.02),
    }


def _fwd_reference(x, c, ctx, c_ctx, mod_w, mod_b, norm_g, ffn_w13, ffn_w2, even_w_in, even_w_out,
              na_rpb, pool_w, pool_scale, conv_w_in, conv_w, conv_w_out, final_g):
    for i in range(DEPTH):
        even = (i % 2 == 0)
        ctx_out = any(j % 2 == 0 for j in range(i + 1, DEPTH))
        ctx_here = even or ctx_out
        m_l = adaln(c[:, None, :], mod_w[i], mod_b[i])
        m_c = adaln(c_ctx[None, None, :], mod_w[i], mod_b[i])

        x = ffn_sublayer(x, m_l, norm_g[i, 0], ffn_w13[i, 0], ffn_w2[i, 0], 0)
        if ctx_here:
            ctx = ffn_sublayer(ctx, m_c, norm_g[i, 0], ffn_w13[i, 0], ffn_w2[i, 0], 0)

        xn = modulate(rms_norm(x, norm_g[i, 1]), m_l[:, :, 3], m_l[:, :, 4])
        y_c = None
        if even:
            e = i // 2
            cn = modulate(rms_norm(ctx, norm_g[i, 1]), m_c[:, :, 3], m_c[:, :, 4])
            y_l, y_c = even_mixer(xn, cn, even_w_in[e], even_w_out[e], na_rpb[e],
                                  pool_w[e], pool_scale[e], ctx_out)
        else:
            o = i // 2
            y_l = short_conv_mixer(xn, conv_w_in[o], conv_w[o], conv_w_out[o])
            if ctx_out:
                cn = modulate(rms_norm(ctx, norm_g[i, 1]), m_c[:, :, 3], m_c[:, :, 4])
                y_c = short_conv_mixer(cn, conv_w_in[o], conv_w[o], conv_w_out[o])
        x = x + m_l[:, :, 5] * y_l

        x = ffn_sublayer(x, m_l, norm_g[i, 2], ffn_w13[i, 1], ffn_w2[i, 1], 6)
        if ctx_out:
            ctx = ctx + m_c[:, :, 5] * y_c
            ctx = ffn_sublayer(ctx, m_c, norm_g[i, 2], ffn_w13[i, 1], ffn_w2[i, 1], 6)

    return rms_norm(x, final_g)


import jax as _jax
import jax.numpy as _jnp

TWIN_FORMAT = 'train_step'
FWD_PARAMS = ['x', 'c', 'ctx', 'c_ctx', 'mod_w', 'mod_b', 'norm_g', 'ffn_w13', 'ffn_w2', 'even_w_in', 'even_w_out', 'na_rpb', 'pool_w', 'pool_scale', 'conv_w_in', 'conv_w', 'conv_w_out', 'final_g']
TWIN_WEIGHTS = ['c_ctx', 'mod_w', 'mod_b', 'norm_g', 'ffn_w13', 'ffn_w2', 'even_w_in', 'even_w_out', 'na_rpb', 'pool_w', 'pool_scale', 'conv_w_in', 'conv_w', 'conv_w_out', 'final_g']
TWIN_DIFF_INPUT = 'x'
TWIN_INPUTS = ['x', 'c', 'ctx', 'c_ctx', 'mod_w', 'mod_b', 'norm_g', 'ffn_w13', 'ffn_w2', 'even_w_in', 'even_w_out', 'na_rpb', 'pool_w', 'pool_scale', 'conv_w_in', 'conv_w', 'conv_w_out', 'final_g', 'loss_target', 'm_c_ctx', 'm_mod_w', 'm_mod_b', 'm_norm_g', 'm_ffn_w13', 'm_ffn_w2', 'm_even_w_in', 'm_even_w_out', 'm_na_rpb', 'm_pool_w', 'm_pool_scale', 'm_conv_w_in', 'm_conv_w', 'm_conv_w_out', 'm_final_g', 'v_c_ctx', 'v_mod_w', 'v_mod_b', 'v_norm_g', 'v_ffn_w13', 'v_ffn_w2', 'v_even_w_in', 'v_even_w_out', 'v_na_rpb', 'v_pool_w', 'v_pool_scale', 'v_conv_w_in', 'v_conv_w', 'v_conv_w_out', 'v_final_g']
TWIN_OUTPUTS = ['loss', 'grad_x', 'grad_c_ctx', 'grad_mod_w', 'grad_mod_b', 'grad_norm_g', 'grad_ffn_w13', 'grad_ffn_w2', 'grad_even_w_in', 'grad_even_w_out', 'grad_na_rpb', 'grad_pool_w', 'grad_pool_scale', 'grad_conv_w_in', 'grad_conv_w', 'grad_conv_w_out', 'grad_final_g', 'delta_c_ctx', 'delta_mod_w', 'delta_mod_b', 'delta_norm_g', 'delta_ffn_w13', 'delta_ffn_w2', 'delta_even_w_in', 'delta_even_w_out', 'delta_na_rpb', 'delta_pool_w', 'delta_pool_scale', 'delta_conv_w_in', 'delta_conv_w', 'delta_conv_w_out', 'delta_final_g', 'new_m_c_ctx', 'new_m_mod_w', 'new_m_mod_b', 'new_m_norm_g', 'new_m_ffn_w13', 'new_m_ffn_w2', 'new_m_even_w_in', 'new_m_even_w_out', 'new_m_na_rpb', 'new_m_pool_w', 'new_m_pool_scale', 'new_m_conv_w_in', 'new_m_conv_w', 'new_m_conv_w_out', 'new_m_final_g', 'new_v_c_ctx', 'new_v_mod_w', 'new_v_mod_b', 'new_v_norm_g', 'new_v_ffn_w13', 'new_v_ffn_w2', 'new_v_even_w_in', 'new_v_even_w_out', 'new_v_na_rpb', 'new_v_pool_w', 'new_v_pool_scale', 'new_v_conv_w_in', 'new_v_conv_w', 'new_v_conv_w_out', 'new_v_final_g']
TWIN_LEAF_KINDS = {'loss': 'loss', 'grad_x': 'grad_x', 'grad_c_ctx': 'grad_w', 'grad_mod_w': 'grad_w', 'grad_mod_b': 'grad_w', 'grad_norm_g': 'grad_w', 'grad_ffn_w13': 'grad_w', 'grad_ffn_w2': 'grad_w', 'grad_even_w_in': 'grad_w', 'grad_even_w_out': 'grad_w', 'grad_na_rpb': 'grad_w', 'grad_pool_w': 'grad_w', 'grad_pool_scale': 'grad_w', 'grad_conv_w_in': 'grad_w', 'grad_conv_w': 'grad_w', 'grad_conv_w_out': 'grad_w', 'grad_final_g': 'grad_w', 'delta_c_ctx': 'delta_w', 'delta_mod_w': 'delta_w', 'delta_mod_b': 'delta_w', 'delta_norm_g': 'delta_w', 'delta_ffn_w13': 'delta_w', 'delta_ffn_w2': 'delta_w', 'delta_even_w_in': 'delta_w', 'delta_even_w_out': 'delta_w', 'delta_na_rpb': 'delta_w', 'delta_pool_w': 'delta_w', 'delta_pool_scale': 'delta_w', 'delta_conv_w_in': 'delta_w', 'delta_conv_w': 'delta_w', 'delta_conv_w_out': 'delta_w', 'delta_final_g': 'delta_w', 'new_m_c_ctx': 'new_m', 'new_m_mod_w': 'new_m', 'new_m_mod_b': 'new_m', 'new_m_norm_g': 'new_m', 'new_m_ffn_w13': 'new_m', 'new_m_ffn_w2': 'new_m', 'new_m_even_w_in': 'new_m', 'new_m_even_w_out': 'new_m', 'new_m_na_rpb': 'new_m', 'new_m_pool_w': 'new_m', 'new_m_pool_scale': 'new_m', 'new_m_conv_w_in': 'new_m', 'new_m_conv_w': 'new_m', 'new_m_conv_w_out': 'new_m', 'new_m_final_g': 'new_m', 'new_v_c_ctx': 'new_v', 'new_v_mod_w': 'new_v', 'new_v_mod_b': 'new_v', 'new_v_norm_g': 'new_v', 'new_v_ffn_w13': 'new_v', 'new_v_ffn_w2': 'new_v', 'new_v_even_w_in': 'new_v', 'new_v_even_w_out': 'new_v', 'new_v_na_rpb': 'new_v', 'new_v_pool_w': 'new_v', 'new_v_pool_scale': 'new_v', 'new_v_conv_w_in': 'new_v', 'new_v_conv_w': 'new_v', 'new_v_conv_w_out': 'new_v', 'new_v_final_g': 'new_v'}


def _forward(args):
    return _fwd_reference(*[args[k] for k in FWD_PARAMS])


def _output_shape():
    def fwd():
        inp = _fwd_setup_inputs(0)
        return _fwd_reference(*[inp[k] for k in FWD_PARAMS])
    out = _jax.eval_shape(fwd)
    return out.shape, out.dtype

N_MICROBATCH = 1
ADAM_LR = 0.001
ADAM_B1 = 0.9
ADAM_B2 = 0.999
ADAM_EPS = 1e-08
ADAM_WD = 0.01
ADAM_STEP = 10
PER_EXAMPLE_BATCH_AXIS = {'x': 0, 'c': 0, 'ctx': 0, 'loss_target': 0}
SHARED_INPUTS = []
_WEIGHT_DTYPES = {'c_ctx': _jnp.float32, 'mod_w': _jnp.float32, 'mod_b': _jnp.float32, 'norm_g': _jnp.float32, 'ffn_w13': _jnp.float32, 'ffn_w2': _jnp.float32, 'even_w_in': _jnp.float32, 'even_w_out': _jnp.float32, 'na_rpb': _jnp.float32, 'pool_w': _jnp.float32, 'pool_scale': _jnp.float32, 'conv_w_in': _jnp.float32, 'conv_w': _jnp.float32, 'conv_w_out': _jnp.float32, 'final_g': _jnp.float32}
MOMENT_SCALE = {'c_ctx': 1.862481e-02, 'mod_w': 1.038329e-01, 'mod_b': 1.846279e-01, 'norm_g': 1.101148e-01, 'ffn_w13': 2.464504e-02, 'ffn_w2': 4.011548e-02, 'even_w_in': 5.638513e-02, 'even_w_out': 7.853291e-02, 'na_rpb': 3.411441e-03, 'pool_w': 1.060419e-01, 'pool_scale': 1.075788e-01, 'conv_w_in': 1.246121e-01, 'conv_w': 1.289208e-01, 'conv_w_out': 1.246318e-01, 'final_g': 1.284136e+02}


def _to_microbatches(a, axis):
    t = _jnp.moveaxis(a, axis, 0)
    t = t.reshape((N_MICROBATCH, t.shape[0] // N_MICROBATCH) + t.shape[1:])
    return _jnp.moveaxis(t, 1, axis + 1)


def setup_inputs(seed: int = 0) -> dict:
    inp = _fwd_setup_inputs(seed)
    key = _jax.random.fold_in(_jax.random.key(seed), 7919)
    shape, _ = _output_shape()
    out = dict(inp)
    out["loss_target"] = _jax.random.normal(_jax.random.fold_in(key, 0), shape, _jnp.float32)
    for i, name in enumerate(TWIN_WEIGHTS):
        w = inp[name].astype(_jnp.float32)
        if MOMENT_SCALE is None:
            s = _jnp.sqrt(_jnp.mean(_jnp.square(w)) + 1e-30)
        else:
            s = MOMENT_SCALE[name]
        km, kv = _jax.random.split(_jax.random.fold_in(key, i + 1))
        out[name] = w
        out["m_" + name] = s * _jax.random.normal(km, w.shape, _jnp.float32)
        out["v_" + name] = (s * s) * _jax.random.uniform(kv, w.shape, _jnp.float32, 0.5, 1.5)
    if N_MICROBATCH > 1:
        for name, axis in PER_EXAMPLE_BATCH_AXIS.items():
            out[name] = _to_microbatches(out[name], axis)
    return {'x': out['x'], 'c': out['c'], 'ctx': out['ctx'], 'c_ctx': out['c_ctx'], 'mod_w': out['mod_w'], 'mod_b': out['mod_b'], 'norm_g': out['norm_g'], 'ffn_w13': out['ffn_w13'], 'ffn_w2': out['ffn_w2'], 'even_w_in': out['even_w_in'], 'even_w_out': out['even_w_out'], 'na_rpb': out['na_rpb'], 'pool_w': out['pool_w'], 'pool_scale': out['pool_scale'], 'conv_w_in': out['conv_w_in'], 'conv_w': out['conv_w'], 'conv_w_out': out['conv_w_out'], 'final_g': out['final_g'], 'loss_target': out['loss_target'], 'm_c_ctx': out['m_c_ctx'], 'm_mod_w': out['m_mod_w'], 'm_mod_b': out['m_mod_b'], 'm_norm_g': out['m_norm_g'], 'm_ffn_w13': out['m_ffn_w13'], 'm_ffn_w2': out['m_ffn_w2'], 'm_even_w_in': out['m_even_w_in'], 'm_even_w_out': out['m_even_w_out'], 'm_na_rpb': out['m_na_rpb'], 'm_pool_w': out['m_pool_w'], 'm_pool_scale': out['m_pool_scale'], 'm_conv_w_in': out['m_conv_w_in'], 'm_conv_w': out['m_conv_w'], 'm_conv_w_out': out['m_conv_w_out'], 'm_final_g': out['m_final_g'], 'v_c_ctx': out['v_c_ctx'], 'v_mod_w': out['v_mod_w'], 'v_mod_b': out['v_mod_b'], 'v_norm_g': out['v_norm_g'], 'v_ffn_w13': out['v_ffn_w13'], 'v_ffn_w2': out['v_ffn_w2'], 'v_even_w_in': out['v_even_w_in'], 'v_even_w_out': out['v_even_w_out'], 'v_na_rpb': out['v_na_rpb'], 'v_pool_w': out['v_pool_w'], 'v_pool_scale': out['v_pool_scale'], 'v_conv_w_in': out['v_conv_w_in'], 'v_conv_w': out['v_conv_w'], 'v_conv_w_out': out['v_conv_w_out'], 'v_final_g': out['v_final_g']}


def _loss(weights, diff, rest, loss_target):
    with _jax.named_scope("forward"):
        args = {**rest, TWIN_DIFF_INPUT: diff, **{k: w.astype(_WEIGHT_DTYPES[k]) for k, w in weights.items()}}
        y = _forward(args)
    with _jax.named_scope("loss_head"):
        err = _jnp.square(y.astype(_jnp.float32) - loss_target)
        return 0.5 * _jnp.sum(_jnp.mean(err, axis=-1)) if err.ndim else 0.5 * err


def _adamw(w, g, m, v):
    m = ADAM_B1 * m + (1.0 - ADAM_B1) * g
    v = ADAM_B2 * v + (1.0 - ADAM_B2) * _jnp.square(g)
    m_hat = m / (1.0 - ADAM_B1 ** ADAM_STEP)
    v_hat = v / (1.0 - ADAM_B2 ** ADAM_STEP)
    delta = -ADAM_LR * (m_hat / (_jnp.sqrt(v_hat) + ADAM_EPS) + ADAM_WD * w)
    return delta, m, v


def reference(x, c, ctx, c_ctx, mod_w, mod_b, norm_g, ffn_w13, ffn_w2, even_w_in, even_w_out, na_rpb, pool_w, pool_scale, conv_w_in, conv_w, conv_w_out, final_g, loss_target, m_c_ctx, m_mod_w, m_mod_b, m_norm_g, m_ffn_w13, m_ffn_w2, m_even_w_in, m_even_w_out, m_na_rpb, m_pool_w, m_pool_scale, m_conv_w_in, m_conv_w, m_conv_w_out, m_final_g, v_c_ctx, v_mod_w, v_mod_b, v_norm_g, v_ffn_w13, v_ffn_w2, v_even_w_in, v_even_w_out, v_na_rpb, v_pool_w, v_pool_scale, v_conv_w_in, v_conv_w, v_conv_w_out, v_final_g):
    given = dict(x=x, c=c, ctx=ctx, c_ctx=c_ctx, mod_w=mod_w, mod_b=mod_b, norm_g=norm_g, ffn_w13=ffn_w13, ffn_w2=ffn_w2, even_w_in=even_w_in, even_w_out=even_w_out, na_rpb=na_rpb, pool_w=pool_w, pool_scale=pool_scale, conv_w_in=conv_w_in, conv_w=conv_w, conv_w_out=conv_w_out, final_g=final_g, loss_target=loss_target, m_c_ctx=m_c_ctx, m_mod_w=m_mod_w, m_mod_b=m_mod_b, m_norm_g=m_norm_g, m_ffn_w13=m_ffn_w13, m_ffn_w2=m_ffn_w2, m_even_w_in=m_even_w_in, m_even_w_out=m_even_w_out, m_na_rpb=m_na_rpb, m_pool_w=m_pool_w, m_pool_scale=m_pool_scale, m_conv_w_in=m_conv_w_in, m_conv_w=m_conv_w, m_conv_w_out=m_conv_w_out, m_final_g=m_final_g, v_c_ctx=v_c_ctx, v_mod_w=v_mod_w, v_mod_b=v_mod_b, v_norm_g=v_norm_g, v_ffn_w13=v_ffn_w13, v_ffn_w2=v_ffn_w2, v_even_w_in=v_even_w_in, v_even_w_out=v_even_w_out, v_na_rpb=v_na_rpb, v_pool_w=v_pool_w, v_pool_scale=v_pool_scale, v_conv_w_in=v_conv_w_in, v_conv_w=v_conv_w, v_conv_w_out=v_conv_w_out, v_final_g=v_final_g)
    weights = {n: given[n] for n in TWIN_WEIGHTS}
    shared = {n: given[n] for n in SHARED_INPUTS}
    per_example = {n: given[n] for n in ['x', 'c', 'ctx']}
    grad_fn = _jax.value_and_grad(_loss, argnums=(0, 1))

    def one_microbatch(ex, loss_target):
        ex = dict(ex)
        diff = ex.pop(TWIN_DIFF_INPUT)
        return grad_fn(weights, diff, {**shared, **ex}, loss_target)

    if N_MICROBATCH == 1:
        loss, (grad_w, grad_x) = one_microbatch(per_example, given["loss_target"])
    else:
        def body(carry, xs):
            loss_sum, grad_sum = carry
            l_k, (gw_k, gx_k) = one_microbatch(xs[0], xs[1])
            with _jax.named_scope("update"):
                return (loss_sum + l_k, _jax.tree.map(_jnp.add, grad_sum, gw_k)), gx_k

        init = (_jnp.zeros((), _jnp.float32), _jax.tree.map(_jnp.zeros_like, weights))
        (loss, grad_w), grad_x = _jax.lax.scan(body, init, (per_example, given["loss_target"]))
    with _jax.named_scope("update"):
        delta_w, new_m, new_v = {}, {}, {}
        for n in TWIN_WEIGHTS:
            delta_w[n], new_m[n], new_v[n] = _adamw(weights[n], grad_w[n], given["m_" + n], given["v_" + n])
    return (loss, grad_x, *[grad_w[n] for n in TWIN_WEIGHTS], *[delta_w[n] for n in TWIN_WEIGHTS],
            *[new_m[n] for n in TWIN_WEIGHTS], *[new_v[n] for n in TWIN_WEIGHTS])
```

```python
import functools

import jax
import jax.numpy as jnp
from jax import lax
from jax.experimental import pallas as pl
from jax.experimental.pallas import tpu as pltpu

F32 = jnp.float32
BF16 = jnp.bfloat16
MESH = pl.DeviceIdType.MESH

GRID_W = 64
NA_HEADS = 8
NA_HEAD_DIM = 64
NA_KH = 8
NA_KW = 16
NA_WIDTH = NA_HEADS * NA_HEAD_DIM
POOL_WINDOWS = (2, 4, 8, 16)
POOL_GROUP_DIM = 128
N_MOD = 9
RMS_EPS = 1e-6
NEG_INF = -1e30
ADAM_LR, ADAM_B1, ADAM_B2, ADAM_EPS, ADAM_WD, ADAM_STEP = 0.001, 0.9, 0.999, 1e-08, 0.01, 10

HALO = 16
VMEM_LIMIT = 56 * 1024 * 1024
N_CHIPS = 4
N_DEV = 8


def _dot(a, b):
    return jnp.dot(a, b, preferred_element_type=F32)


def _dot_nt(a, b):
    return lax.dot_general(a, b, (((1,), (1,)), ((), ())), preferred_element_type=F32)


def _dot_tn(a, b):
    return lax.dot_general(a, b, (((0,), (0,)), ((), ())), preferred_element_type=F32)


def _sigmoid(a):
    return 1.0 / (1.0 + jnp.exp(-a))


def _sum0(v):
    return jnp.sum(v, axis=0, keepdims=True)


def _nm(x, g, shift, scale):
    r = lax.rsqrt(jnp.mean(x * x, axis=-1, keepdims=True) + RMS_EPS)
    xhat = x * r
    nrm = xhat * g
    return nrm * (1.0 + scale) + shift, xhat, r, nrm


def _nm_bwd(dhn, xhat, r, nrm, g, scale):
    dshift = _sum0(dhn)
    dscale = _sum0(dhn * nrm)
    dnrm = dhn * (1.0 + scale)
    dgn = _sum0(dnrm * xhat)
    dxh = dnrm * g
    dx = r * (dxh - xhat * jnp.mean(dxh * xhat, axis=-1, keepdims=True))
    return dx, dshift, dscale, dgn


def _acc_rows(acc_ref, first, rows):
    @pl.when(first)
    def _():
        acc_ref[...] = jnp.zeros(acc_ref.shape, acc_ref.dtype)
    for k, row in enumerate(rows):
        if row is not None:
            acc_ref[k:k + 1, :] += row


def _shift_rows(v, k):
    n = v.shape[0]
    k = k % n
    return v if k == 0 else pltpu.roll(v, k, 0)


def _tile(tm, w):
    return pl.BlockSpec((tm, w), lambda i: (i, 0))


def _full(shape):
    nd = len(shape)
    return pl.BlockSpec(shape, lambda i: (0,) * nd)


def _resident(block, imap):
    return pl.BlockSpec(block, imap, pipeline_mode=pl.Buffered(1))


def _halo_prev(tm, w):
    return pl.BlockSpec((HALO, w), lambda i: (jnp.maximum(i * (tm // HALO) - 1, 0), 0))


def _halo_next(tm, w, L):
    return pl.BlockSpec((HALO, w), lambda i: (jnp.minimum((i + 1) * (tm // HALO), L // HALO - 1), 0))


def _params(vmem=VMEM_LIMIT):
    return pltpu.CompilerParams(vmem_limit_bytes=vmem)


def _pick_rows(rows, cols, itemsize=4, target=1 << 20):
    best = None
    for t in range(8, rows + 1, 8):
        if rows % t == 0 and t * cols * itemsize <= target:
            best = t
    return best if best is not None else rows


def _ext(prev, cur, nxt, i, nt):
    prev = jnp.where(i > 0, prev, jnp.zeros_like(prev))
    nxt = jnp.where(i < nt - 1, nxt, jnp.zeros_like(nxt))
    return jnp.concatenate([prev, cur, nxt], axis=0)


def _cast_bf16(a2d, name):
    rows, cols = a2d.shape
    tr = _pick_rows(rows, cols)

    def body(a_ref, o_ref):
        o_ref[...] = a_ref[...].astype(BF16)

    return pl.pallas_call(
        body, name=name, grid=(rows // tr,), in_specs=[_tile(tr, cols)], out_specs=_tile(tr, cols),
        out_shape=jax.ShapeDtypeStruct((rows, cols), BF16))(a2d)


def _add_pair(a2d, b2d, name):
    rows, cols = a2d.shape
    tr = _pick_rows(rows, cols)

    def body(a_ref, b_ref, o_ref, ob_ref):
        s = a_ref[...] + b_ref[...]
        o_ref[...] = s
        ob_ref[...] = s.astype(BF16)

    return pl.pallas_call(
        body, name=name, grid=(rows // tr,), in_specs=[_tile(tr, cols)] * 2, out_specs=[_tile(tr, cols)] * 2,
        out_shape=[jax.ShapeDtypeStruct((rows, cols), F32), jax.ShapeDtypeStruct((rows, cols), BF16)])(a2d, b2d)


def _add_four(mine, got, name):
    rows, cols = mine.shape
    tr = _pick_rows(rows, cols)

    def body(m_ref, g_ref, o_ref):
        s = m_ref[...]
        for j in range(3):
            s = s + g_ref[j].astype(F32)
        o_ref[...] = s

    return pl.pallas_call(
        body, name=name, grid=(rows // tr,),
        in_specs=[_tile(tr, cols), pl.BlockSpec((3, tr, cols), lambda i: (0, i, 0))], out_specs=_tile(tr, cols),
        out_shape=jax.ShapeDtypeStruct((rows, cols), F32))(mine, got)


def _sum_devices(g, name):
    n, rows, cols = g.shape
    tr = _pick_rows(rows, cols, target=1 << 18)

    def body(g_ref, o_ref):
        s = g_ref[0]
        for d in range(1, n):
            s = s + g_ref[d]
        o_ref[...] = s

    return pl.pallas_call(
        body, name=name, grid=(rows // tr,), in_specs=[pl.BlockSpec((n, tr, cols), lambda i: (0, i, 0))],
        out_specs=_tile(tr, cols), out_shape=jax.ShapeDtypeStruct((rows, cols), F32))(g)


def _adamw(w, g, m, v, name):
    shape = w.shape
    cols = shape[-1]
    rows = w.size // cols
    w2, g2, m2, v2 = (t.reshape(rows, cols) for t in (w, g, m, v))
    tr = _pick_rows(rows, cols)
    c1 = 1.0 - ADAM_B1 ** ADAM_STEP
    c2 = 1.0 - ADAM_B2 ** ADAM_STEP

    def body(w_ref, g_ref, m_ref, v_ref, d_ref, mo_ref, vo_ref):
        gg = g_ref[...]
        mn = ADAM_B1 * m_ref[...] + (1.0 - ADAM_B1) * gg
        vn = ADAM_B2 * v_ref[...] + (1.0 - ADAM_B2) * (gg * gg)
        d_ref[...] = -ADAM_LR * ((mn / c1) / (jnp.sqrt(vn / c2) + ADAM_EPS) + ADAM_WD * w_ref[...])
        mo_ref[...] = mn
        vo_ref[...] = vn

    outs = pl.pallas_call(
        body, name=name, grid=(rows // tr,), in_specs=[_tile(tr, cols)] * 4, out_specs=[_tile(tr, cols)] * 3,
        out_shape=[jax.ShapeDtypeStruct((rows, cols), F32)] * 3)(w2, g2, m2, v2)
    return tuple(o.reshape(shape) for o in outs)


def _mesh_pos():
    x, y, c = lax.axis_index("x"), lax.axis_index("y"), lax.axis_index("c")
    chips = [(1 - x, y), (x, 1 - y), (1 - x, 1 - y)]
    return x, y, c, chips


def _hbm_specs(n):
    return [pl.BlockSpec(memory_space=pltpu.HBM)] * n


def _small_all_gather(v, name):
    rows, w = v.shape

    def body(x_ref, out_ref, send_sems, recv_sems, local_sem):
        x, y, c, chips = _mesh_pos()
        me, sibling = (x, y, c), (x, y, 1 - c)

        def blk(px, py, pc):
            return out_ref.at[4 * px + 2 * py + pc]

        def copy(k, block, to, src=None):
            return pltpu.make_async_remote_copy(
                src_ref=blk(*block) if src is None else src, dst_ref=blk(*block),
                send_sem=send_sems.at[k], recv_sem=recv_sems.at[k], device_id=to, device_id_type=MESH)

        mine = pltpu.make_async_copy(x_ref, blk(*me), local_sem)
        mine.start()
        first = [copy(0, me, sibling, src=x_ref)]
        first += [copy(1 + j, me, (*chip, c), src=x_ref) for j, chip in enumerate(chips)]
        for cp in first:
            cp.start()
        passed = [copy(4 + j, (*chip, c), sibling) for j, chip in enumerate(chips)]
        for j, chip in enumerate(chips):
            copy(1 + j, (*chip, c), me).wait_recv()
            passed[j].start()
        copy(0, sibling, me).wait_recv()
        for j, chip in enumerate(chips):
            copy(4 + j, (*chip, 1 - c), me).wait_recv()
        for cp in first + passed:
            cp.wait_send()
        mine.wait()

    return pl.pallas_call(
        body, name=name, out_shape=jax.ShapeDtypeStruct((N_DEV, rows, w), v.dtype),
        in_specs=[pl.BlockSpec(memory_space=pltpu.VMEM)], out_specs=pl.BlockSpec(memory_space=pltpu.VMEM),
        scratch_shapes=[pltpu.SemaphoreType.DMA((7,)), pltpu.SemaphoreType.DMA((7,)), pltpu.SemaphoreType.DMA],
    )(v)


def _gather_shards(shards, name):
    n = len(shards)

    def body(*refs):
        ins, outs = refs[:n], refs[n:2 * n]
        send_sems, recv_sems, local_sems = refs[2 * n:]
        x, y, c, chips = _mesh_pos()
        k = 2 * x + y
        sibling = (x, y, 1 - c)

        def window(t, chip_k, half):
            r = ins[t].shape[1]
            return outs[t].at[:, pl.ds(chip_k * r + half * (r // 2), r // 2), :]

        def copy(t, j, chip_k, half, to, src=None):
            return pltpu.make_async_remote_copy(
                src_ref=window(t, chip_k, half) if src is None else src, dst_ref=window(t, chip_k, half),
                send_sem=send_sems.at[6 * t + j], recv_sem=recv_sems.at[6 * t + j], device_id=to, device_id_type=MESH)

        started, local = [], []
        for t in range(n):
            r = ins[t].shape[1]
            lc = pltpu.make_async_copy(ins[t], outs[t].at[:, pl.ds(k * r, r), :], local_sems.at[t])
            lc.start()
            local.append(lc)
            src = ins[t].at[:, pl.ds(c * (r // 2), r // 2), :]
            for j, chip in enumerate(chips):
                cp = copy(t, j, k, c, (*chip, c), src=src)
                cp.start()
                started.append(cp)
        for t in range(n):
            for j, chip in enumerate(chips):
                kj = 2 * chip[0] + chip[1]
                copy(t, j, kj, c, sibling).wait_recv()
                cp = copy(t, 3 + j, kj, c, sibling)
                cp.start()
                started.append(cp)
        for t in range(n):
            for j, chip in enumerate(chips):
                kj = 2 * chip[0] + chip[1]
                copy(t, 3 + j, kj, 1 - c, sibling).wait_recv()
        for cp in started:
            cp.wait_send()
        for lc in local:
            lc.wait()

    out_shape = [jax.ShapeDtypeStruct((s.shape[0], N_CHIPS * s.shape[1], s.shape[2]), s.dtype) for s in shards]
    return pl.pallas_call(
        body, name=name, out_shape=out_shape, in_specs=_hbm_specs(n), out_specs=_hbm_specs(n),
        scratch_shapes=[pltpu.SemaphoreType.DMA((6 * n,)), pltpu.SemaphoreType.DMA((6 * n,)),
                        pltpu.SemaphoreType.DMA((n,))],
    )(*shards)


def _swap_halves(parts, name):
    n = len(parts)

    def body(*refs):
        ins, own, got = refs[:n], refs[n:2 * n], refs[2 * n:3 * n]
        send_sems, recv_sems, local_sems = refs[3 * n:]
        x, y, c, _ = _mesh_pos()
        sibling = (x, y, 1 - c)
        started, local = [], []
        for t in range(n):
            h = ins[t].shape[1] // 2
            lc = pltpu.make_async_copy(ins[t].at[:, pl.ds(c * h, h), :], own[t], local_sems.at[t])
            lc.start()
            local.append(lc)
            cp = pltpu.make_async_remote_copy(
                src_ref=ins[t].at[:, pl.ds((1 - c) * h, h), :], dst_ref=got[t],
                send_sem=send_sems.at[t], recv_sem=recv_sems.at[t], device_id=sibling, device_id_type=MESH)
            cp.start()
            started.append(cp)
        for cp in started:
            cp.wait_recv()
        for cp in started:
            cp.wait_send()
        for lc in local:
            lc.wait()

    half = [jax.ShapeDtypeStruct((4, p.shape[1] // 2, p.shape[2]), p.dtype) for p in parts]
    outs = pl.pallas_call(
        body, name=name, out_shape=half + half, in_specs=_hbm_specs(n), out_specs=_hbm_specs(2 * n),
        scratch_shapes=[pltpu.SemaphoreType.DMA((n,)), pltpu.SemaphoreType.DMA((n,)), pltpu.SemaphoreType.DMA((n,))],
    )(*parts)
    return outs[:n], outs[n:]


def _scatter_to_chips(sums_f32, sums_bf16, name):
    n = len(sums_f32)

    def body(*refs):
        f32s, b16s = refs[:n], refs[n:2 * n]
        mine, got = refs[2 * n:3 * n], refs[3 * n:4 * n]
        send_sems, recv_sems, local_sems = refs[4 * n:]
        x, y, c, chips = _mesh_pos()
        k = 2 * x + y
        started, local = [], []
        for t in range(n):
            lc = pltpu.make_async_copy(f32s[t].at[k], mine[t], local_sems.at[t])
            lc.start()
            local.append(lc)
            for j, chip in enumerate(chips):
                cp = pltpu.make_async_remote_copy(
                    src_ref=b16s[t].at[2 * chip[0] + chip[1]], dst_ref=got[t].at[j],
                    send_sem=send_sems.at[3 * t + j], recv_sem=recv_sems.at[3 * t + j],
                    device_id=(*chip, c), device_id_type=MESH)
                cp.start()
                started.append(cp)
        for cp in started:
            cp.wait_recv()
        for cp in started:
            cp.wait_send()
        for lc in local:
            lc.wait()

    mine_shape = [jax.ShapeDtypeStruct(s.shape[1:], F32) for s in sums_f32]
    got_shape = [jax.ShapeDtypeStruct((3,) + s.shape[1:], BF16) for s in sums_f32]
    outs = pl.pallas_call(
        body, name=name, out_shape=mine_shape + got_shape, in_specs=_hbm_specs(2 * n), out_specs=_hbm_specs(2 * n),
        scratch_shapes=[pltpu.SemaphoreType.DMA((3 * n,)), pltpu.SemaphoreType.DMA((3 * n,)),
                        pltpu.SemaphoreType.DMA((n,))],
    )(*sums_f32, *sums_bf16)
    return outs[:n], outs[n:]


def _join_halves(halves, name):
    n = len(halves)

    def body(*refs):
        ins, outs = refs[:n], refs[n:2 * n]
        send_sems, recv_sems, local_sems = refs[2 * n:]
        x, y, c, _ = _mesh_pos()
        sibling = (x, y, 1 - c)
        started, local = [], []
        for t in range(n):
            h = ins[t].shape[0]
            lc = pltpu.make_async_copy(ins[t], outs[t].at[pl.ds(c * h, h), :], local_sems.at[t])
            lc.start()
            local.append(lc)
            cp = pltpu.make_async_remote_copy(
                src_ref=ins[t], dst_ref=outs[t].at[pl.ds(c * h, h), :],
                send_sem=send_sems.at[t], recv_sem=recv_sems.at[t], device_id=sibling, device_id_type=MESH)
            cp.start()
            started.append(cp)
        for cp in started:
            cp.wait_recv()
        for cp in started:
            cp.wait_send()
        for lc in local:
            lc.wait()

    out_shape = [jax.ShapeDtypeStruct((2 * s.shape[0], s.shape[1]), s.dtype) for s in halves]
    return pl.pallas_call(
        body, name=name, out_shape=out_shape, in_specs=_hbm_specs(n), out_specs=_hbm_specs(n),
        scratch_shapes=[pltpu.SemaphoreType.DMA((n,)), pltpu.SemaphoreType.DMA((n,)), pltpu.SemaphoreType.DMA((n,))],
    )(*halves)


def _reduce_scatter(parts, tag):
    parts = [p.reshape(4, p.shape[0] // 4, p.shape[1]) for p in parts]
    own, got = _swap_halves(parts, f"rs_swap_{tag}")
    sums, sums_b = [], []
    for t, (a, b) in enumerate(zip(own, got)):
        _, h, w = a.shape
        s, sb = _add_pair(a.reshape(4 * h, w), b.reshape(4 * h, w), f"rs_add2_{tag}_{t}")
        sums.append(s.reshape(4, h, w))
        sums_b.append(sb.reshape(4, h, w))
    mine, recv = _scatter_to_chips(sums, sums_b, f"rs_scatter_{tag}")
    halves = [_add_four(m, r, f"rs_add4_{tag}_{t}") for t, (m, r) in enumerate(zip(mine, recv))]
    return _join_halves(halves, f"rs_join_{tag}")


def _ffn_fwd(x, prm, w13g, w2g, t, name, tm=512):
    L, D = x.shape
    Fh = w13g.shape[-1]
    tm = min(tm, L)

    def body(x_ref, p_ref, w13_ref, w2_ref, xo_ref, ab_ref, y_ref):
        xv = x_ref[...]
        hn, _, _, _ = _nm(xv, p_ref[3:4, :], p_ref[0:1, :], p_ref[1:2, :])
        hb = hn.astype(BF16)
        acc = jnp.zeros((tm, D), F32)
        for j in range(2):
            a = _dot(hb, w13_ref[j])
            b = _dot(hb, w13_ref[2 + j])
            ab_ref[:, j * Fh:(j + 1) * Fh] = a.astype(BF16)
            ab_ref[:, (2 + j) * Fh:(3 + j) * Fh] = b.astype(BF16)
            g = (a * _sigmoid(a) * b).astype(BF16)
            acc = acc + _dot(g, w2_ref[j * Fh:(j + 1) * Fh, :])
        y_ref[...] = acc.astype(BF16)
        xo_ref[...] = xv + (0.5 * p_ref[2:3, :]) * acc

    return pl.pallas_call(
        body, name=name, grid=(L // tm,),
        in_specs=[_tile(tm, D), _full((8, D)),
                  _resident((None, 4, D, Fh), lambda i: (t, 0, 0, 0)),
                  _resident((None, 2 * Fh, D), lambda i: (t, 0, 0))],
        out_specs=[_tile(tm, D), _tile(tm, 4 * Fh), _tile(tm, D)],
        out_shape=[jax.ShapeDtypeStruct((L, D), F32), jax.ShapeDtypeStruct((L, 4 * Fh), BF16),
                   jax.ShapeDtypeStruct((L, D), BF16)],
        compiler_params=_params())(x, prm, w13g, w2g)


def _ffn_bwd(dout, x, ab, y, prm, w13g, w2g, t, name, tm=256):
    L, D = x.shape
    Fh = w13g.shape[-1]
    tm = min(tm, L)

    def body(do_ref, x_ref, ab_ref, y_ref, p_ref, w13_ref, w2_ref, dx_ref, dab_ref, g_ref, dy_ref, hn_ref, acc_ref):
        i = pl.program_id(0)
        do = do_ref[...]
        gain, shift, scale, gate = p_ref[3:4, :], p_ref[0:1, :], p_ref[1:2, :], p_ref[2:3, :]
        hn, xhat, r, nrm = _nm(x_ref[...], gain, shift, scale)
        hn_ref[...] = hn.astype(BF16)
        dgate = 0.5 * _sum0(do * y_ref[...].astype(F32))
        dyb = ((0.5 * gate) * do).astype(BF16)
        dy_ref[...] = dyb
        dhn = jnp.zeros((tm, D), F32)
        for j in range(2):
            dg = _dot_nt(dyb, w2_ref[j * Fh:(j + 1) * Fh, :])
            a = ab_ref[:, j * Fh:(j + 1) * Fh].astype(F32)
            b = ab_ref[:, (2 + j) * Fh:(3 + j) * Fh].astype(F32)
            sg = _sigmoid(a)
            sa = a * sg
            g_ref[:, j * Fh:(j + 1) * Fh] = (sa * b).astype(BF16)
            da = (dg * b * (sg * (1.0 + a * (1.0 - sg)))).astype(BF16)
            db = (dg * sa).astype(BF16)
            dab_ref[:, j * Fh:(j + 1) * Fh] = da
            dab_ref[:, (2 + j) * Fh:(3 + j) * Fh] = db
            dhn = dhn + _dot_nt(da, w13_ref[j]) + _dot_nt(db, w13_ref[2 + j])
        dx, dshift, dscale, dgn = _nm_bwd(dhn, xhat, r, nrm, gain, scale)
        dx_ref[...] = do + dx
        _acc_rows(acc_ref, i == 0, [dshift, dscale, dgate, dgn])

    return pl.pallas_call(
        body, name=name, grid=(L // tm,),
        in_specs=[_tile(tm, D), _tile(tm, D), _tile(tm, 4 * Fh), _tile(tm, D), _full((8, D)),
                  _resident((None, 4, D, Fh), lambda i: (t, 0, 0, 0)),
                  _resident((None, 2 * Fh, D), lambda i: (t, 0, 0))],
        out_specs=[_tile(tm, D), _tile(tm, 4 * Fh), _tile(tm, 2 * Fh), _tile(tm, D), _tile(tm, D), _full((8, D))],
        out_shape=[jax.ShapeDtypeStruct((L, D), F32), jax.ShapeDtypeStruct((L, 4 * Fh), BF16),
                   jax.ShapeDtypeStruct((L, 2 * Fh), BF16), jax.ShapeDtypeStruct((L, D), BF16),
                   jax.ShapeDtypeStruct((L, D), BF16), jax.ShapeDtypeStruct((8, D), F32)],
        compiler_params=_params())(dout, x, ab, y, prm, w13g, w2g)


def _mm_tn(a, b, slabs, a_slabbed, name, init=None, tl=512):
    L = a.shape[0]
    ka = a.shape[1] // slabs if a_slabbed else a.shape[1]
    nb = b.shape[1] if a_slabbed else b.shape[1] // slabs
    tl = min(tl, L)
    has_init = init is not None

    def body(a_ref, b_ref, *rest):
        o_ref = rest[-1]
        step = pl.program_id(1)

        @pl.when(step == 0)
        def _():
            o_ref[...] = rest[0][...] if has_init else jnp.zeros((ka, nb), F32)

        o_ref[...] += _dot_tn(a_ref[...], b_ref[...])

    in_specs = [pl.BlockSpec((tl, ka), (lambda s, l: (l, s)) if a_slabbed else (lambda s, l: (l, 0))),
                pl.BlockSpec((tl, nb), (lambda s, l: (l, 0)) if a_slabbed else (lambda s, l: (l, s)))]
    args = [a, b]
    if has_init:
        in_specs.append(pl.BlockSpec((ka, nb), lambda s, l: (s, 0)))
        args.append(init)
    return pl.pallas_call(
        body, name=name, grid=(slabs, L // tl), in_specs=in_specs,
        out_specs=pl.BlockSpec((ka, nb), lambda s, l: (s, 0)),
        out_shape=jax.ShapeDtypeStruct((slabs * ka, nb), F32), compiler_params=_params())(*args)


def _even_in_fwd(x, prm, wing, name, tm=512):
    L, D = x.shape
    W = wing.shape[-1]
    tm = min(tm, L)

    def body(x_ref, p_ref, w_ref, q_ref, k_ref, v_ref, u_ref, hn_ref):
        hn, _, _, _ = _nm(x_ref[...], p_ref[3:4, :], p_ref[0:1, :], p_ref[1:2, :])
        hb = hn.astype(BF16)
        hn_ref[...] = hb
        q_ref[...] = _dot(hb, w_ref[0]).astype(BF16)
        k_ref[...] = _dot(hb, w_ref[1]).astype(BF16)
        v_ref[...] = _dot(hb, w_ref[2]).astype(BF16)
        u_ref[...] = _dot(hb, w_ref[3])

    return pl.pallas_call(
        body, name=name, grid=(L // tm,),
        in_specs=[_tile(tm, D), _full((8, D)), _resident((None, 4, D, W), lambda i: (0, 0, 0, 0))],
        out_specs=[_tile(tm, W)] * 4 + [_tile(tm, D)],
        out_shape=[jax.ShapeDtypeStruct((L, W), BF16)] * 3 + [jax.ShapeDtypeStruct((L, W), F32),
                                                              jax.ShapeDtypeStruct((L, D), BF16)],
        compiler_params=_params())(x, prm, wing)


def _even_in_bwd(dout, x, dq, dk, dv, du, prm, wing, name, tm=256):
    L, D = x.shape
    W = wing.shape[-1]
    tm = min(tm, L)

    def body(do_ref, x_ref, dq_ref, dk_ref, dv_ref, du_ref, p_ref, w_ref, dx_ref, ds_ref, acc_ref):
        i = pl.program_id(0)
        gain, shift, scale = p_ref[3:4, :], p_ref[0:1, :], p_ref[1:2, :]
        _, xhat, r, nrm = _nm(x_ref[...], gain, shift, scale)
        dhn = jnp.zeros((tm, D), F32)
        for s, ref in enumerate((dq_ref, dk_ref, dv_ref, du_ref)):
            d = ref[...].astype(BF16)
            ds_ref[:, s * W:(s + 1) * W] = d
            dhn = dhn + _dot_nt(d, w_ref[s])
        dx, dshift, dscale, dgn = _nm_bwd(dhn, xhat, r, nrm, gain, scale)
        dx_ref[...] = do_ref[...] + dx
        _acc_rows(acc_ref, i == 0, [dshift, dscale, None, dgn])

    return pl.pallas_call(
        body, name=name, grid=(L // tm,),
        in_specs=[_tile(tm, D), _tile(tm, D)] + [_tile(tm, W)] * 4 +
                 [_full((8, D)), _resident((None, 4, D, W), lambda i: (0, 0, 0, 0))],
        out_specs=[_tile(tm, D), _tile(tm, 4 * W), _full((8, D))],
        out_shape=[jax.ShapeDtypeStruct((L, D), F32), jax.ShapeDtypeStruct((L, 4 * W), BF16),
                   jax.ShapeDtypeStruct((8, D), F32)],
        compiler_params=_params())(dout, x, dq, dk, dv, du, prm, wing)


def _even_out_fwd(x, mix, prm, woutg, name, tm=512):
    L, D = x.shape
    tm = min(tm, L)

    def body(x_ref, m_ref, p_ref, w_ref, xo_ref, y_ref):
        yv = _dot(m_ref[...], w_ref[...])
        y_ref[...] = yv.astype(BF16)
        xo_ref[...] = x_ref[...] + p_ref[2:3, :] * yv

    return pl.pallas_call(
        body, name=name, grid=(L // tm,),
        in_specs=[_tile(tm, D), _tile(tm, D), _full((8, D)), _resident((None, D, D), lambda i: (0, 0, 0))],
        out_specs=[_tile(tm, D), _tile(tm, D)],
        out_shape=[jax.ShapeDtypeStruct((L, D), F32), jax.ShapeDtypeStruct((L, D), BF16)],
        compiler_params=_params())(x, mix, prm, woutg)


def _even_out_bwd(dout, y, prm, woutg, name, tm=512):
    L, D = dout.shape
    W = D // 2
    tm = min(tm, L)

    def body(do_ref, y_ref, p_ref, w_ref, dy_ref, da_ref, dp_ref, acc_ref):
        i = pl.program_id(0)
        do = do_ref[...]
        dgate = _sum0(do * y_ref[...].astype(F32))
        dyb = (p_ref[2:3, :] * do).astype(BF16)
        dy_ref[...] = dyb
        da_ref[...] = _dot_nt(dyb, w_ref[0:W, :]).astype(BF16)
        dp_ref[...] = _dot_nt(dyb, w_ref[W:2 * W, :])
        _acc_rows(acc_ref, i == 0, [None, None, dgate])

    return pl.pallas_call(
        body, name=name, grid=(L // tm,),
        in_specs=[_tile(tm, D), _tile(tm, D), _full((8, D)), _resident((None, D, D), lambda i: (0, 0, 0))],
        out_specs=[_tile(tm, D), _tile(tm, W), _tile(tm, W), _full((8, D))],
        out_shape=[jax.ShapeDtypeStruct((L, D), BF16), jax.ShapeDtypeStruct((L, W), BF16),
                   jax.ShapeDtypeStruct((L, W), F32), jax.ShapeDtypeStruct((8, D), F32)],
        compiler_params=_params())(dout, y, prm, woutg)


def _bias_table(rpb):
    var = jnp.arange(NA_KH)[:, None]
    kr = jnp.arange(NA_KH)[None, :]
    ri = kr - var + (NA_KH - 1)
    col = jnp.arange(GRID_W)
    cs = jnp.clip(col - NA_KW // 2, 0, GRID_W - NA_KW)
    ok = (col[None, :] >= cs[:, None]) & (col[None, :] < cs[:, None] + NA_KW)
    ci = jnp.clip(col[None, :] - col[:, None] + (NA_KW - 1), 0, 2 * NA_KW - 2)
    b = rpb[:, ri[:, None, :, None], ci[None, :, None, :]]
    b = jnp.where(ok[None, None, :, None, :], b, NEG_INF)
    return b.reshape(rpb.shape[0], NA_KH, GRID_W, NA_KH * GRID_W).astype(F32)


def _attn_probs(q, kw, kc, bias, scale):
    s_w = _dot_nt(q, kw) * scale + bias
    s_c = _dot_nt(q, kc) * scale
    m = jnp.maximum(jnp.max(s_w, axis=-1, keepdims=True), jnp.max(s_c, axis=-1, keepdims=True))
    e_w = jnp.exp(s_w - m)
    e_c = jnp.exp(s_c - m)
    inv = 1.0 / (jnp.sum(e_w, axis=-1, keepdims=True) + jnp.sum(e_c, axis=-1, keepdims=True))
    return e_w * inv, e_c * inv


def _attn_fwd(qh, kh, vh, kch, vch, bias, name):
    H, L, dh = qh.shape
    C = kch.shape[1]
    R = L // GRID_W
    nwin = NA_KH * GRID_W
    scale = dh ** -0.5

    def body(q_ref, k_ref, v_ref, kc_ref, vc_ref, b_ref, o_ref):
        kc = kc_ref[...]
        vc = vc_ref[...]

        def row(r, carry):
            rs = jnp.clip(r - NA_KH // 2, 0, R - NA_KH)
            q0 = pl.multiple_of(r * GRID_W, GRID_W)
            k0 = pl.multiple_of(rs * GRID_W, GRID_W)
            q = q_ref[pl.ds(q0, GRID_W), :]
            p_w, p_c = _attn_probs(q, k_ref[pl.ds(k0, nwin), :], kc, b_ref[r - rs], scale)
            o = _dot(p_w.astype(BF16), v_ref[pl.ds(k0, nwin), :]) + _dot(p_c.astype(BF16), vc)
            o_ref[pl.ds(q0, GRID_W), :] = o.astype(BF16)
            return carry

        lax.fori_loop(0, R, row, 0)

    head = lambda n: pl.BlockSpec((None, n, dh), lambda h: (h, 0, 0))
    return pl.pallas_call(
        body, name=name, grid=(H,),
        in_specs=[head(L), head(L), head(L), head(C), head(C),
                  pl.BlockSpec((None, NA_KH, GRID_W, nwin), lambda h: (h, 0, 0, 0))],
        out_specs=head(L), out_shape=jax.ShapeDtypeStruct((H, L, dh), BF16),
        compiler_params=_params())(qh, kh, vh, kch, vch, bias)


def _attn_bwd(qh, kh, vh, kch, vch, bias, doh, name):
    H, L, dh = qh.shape
    C = kch.shape[1]
    R = L // GRID_W
    nwin = NA_KH * GRID_W
    scale = dh ** -0.5

    def body(q_ref, k_ref, v_ref, kc_ref, vc_ref, b_ref, do_ref, dq_ref, dk_ref, dv_ref, dkc_ref, dvc_ref, db_ref):
        kc = kc_ref[...]
        vc = vc_ref[...]
        dk_ref[...] = jnp.zeros((L, dh), F32)
        dv_ref[...] = jnp.zeros((L, dh), F32)
        dkc_ref[...] = jnp.zeros((C, dh), F32)
        dvc_ref[...] = jnp.zeros((C, dh), F32)
        db_ref[...] = jnp.zeros((NA_KH, GRID_W, nwin), F32)

        def row(r, carry):
            rs = jnp.clip(r - NA_KH // 2, 0, R - NA_KH)
            q0 = pl.multiple_of(r * GRID_W, GRID_W)
            k0 = pl.multiple_of(rs * GRID_W, GRID_W)
            q = q_ref[pl.ds(q0, GRID_W), :]
            kw = k_ref[pl.ds(k0, nwin), :]
            vw = v_ref[pl.ds(k0, nwin), :]
            do = do_ref[pl.ds(q0, GRID_W), :]
            p_w, p_c = _attn_probs(q, kw, kc, b_ref[r - rs], scale)
            dp_w = _dot_nt(do, vw)
            dp_c = _dot_nt(do, vc)
            delta = jnp.sum(p_w * dp_w, axis=-1, keepdims=True) + jnp.sum(p_c * dp_c, axis=-1, keepdims=True)
            ds_w = p_w * (dp_w - delta)
            ds_c = p_c * (dp_c - delta)
            db_ref[r - rs] += ds_w
            dsw = (ds_w * scale).astype(BF16)
            dsc = (ds_c * scale).astype(BF16)
            dq_ref[pl.ds(q0, GRID_W), :] = _dot(dsw, kw) + _dot(dsc, kc)
            dk_ref[pl.ds(k0, nwin), :] += _dot_tn(dsw, q)
            dv_ref[pl.ds(k0, nwin), :] += _dot_tn(p_w.astype(BF16), do)
            dkc_ref[...] += _dot_tn(dsc, q)
            dvc_ref[...] += _dot_tn(p_c.astype(BF16), do)
            return carry

        lax.fori_loop(0, R, row, 0)

    head = lambda n: _resident((None, n, dh), lambda h: (h, 0, 0))
    bspec = pl.BlockSpec((None, NA_KH, GRID_W, nwin), lambda h: (h, 0, 0, 0))
    return pl.pallas_call(
        body, name=name, grid=(H,),
        in_specs=[head(L), head(L), head(L), head(C), head(C), bspec, head(L)],
        out_specs=[head(L), head(L), head(L), head(C), head(C), bspec],
        out_shape=[jax.ShapeDtypeStruct((H, L, dh), F32)] * 3 + [jax.ShapeDtypeStruct((H, C, dh), F32)] * 2 +
                  [jax.ShapeDtypeStruct((H, NA_KH, GRID_W, nwin), F32)],
        compiler_params=_params())(qh, kh, vh, kch, vch, bias, doh)


def _rpb_grad(dbias, name):
    H = dbias.shape[0]
    nri, nci = 2 * NA_KH - 1, 2 * NA_KW - 1
    d5 = dbias.reshape(H, NA_KH, GRID_W, NA_KH, GRID_W).transpose(0, 1, 3, 2, 4)
    col = jnp.arange(GRID_W)
    onehot = (col[None, None, :] - col[None, :, None] + (NA_KW - 1) == jnp.arange(32)[:, None, None]).astype(F32)

    def body(d_ref, m_ref, o_ref, t_ref):
        t_ref[...] = jnp.zeros((32, GRID_W), F32)
        o_ref[...] = jnp.zeros((16, 32, 128), F32)
        for ri in range(nri):
            a = None
            for var in range(NA_KH):
                kr = ri + var - (NA_KH - 1)
                if 0 <= kr < NA_KH:
                    blk = d_ref[var, kr]
                    a = blk if a is None else a + blk
            for ci in range(nci):
                t_ref[ci:ci + 1, :] = _sum0(a * m_ref[ci])
            o_ref[ri] = jnp.broadcast_to(jnp.sum(t_ref[...], axis=1, keepdims=True), (32, 128))

    out = pl.pallas_call(
        body, name=name, grid=(H,),
        in_specs=[pl.BlockSpec((None, NA_KH, NA_KH, GRID_W, GRID_W), lambda h: (h, 0, 0, 0, 0)),
                  pl.BlockSpec((32, GRID_W, GRID_W), lambda h: (0, 0, 0))],
        out_specs=pl.BlockSpec((None, 16, 32, 128), lambda h: (h, 0, 0, 0)),
        out_shape=jax.ShapeDtypeStruct((H, 16, 32, 128), F32),
        scratch_shapes=[pltpu.VMEM((32, GRID_W), F32)])(d5, onehot)
    return out[:, :nri, :nci, 0]


def _window_count(t, w, L):
    lo = jnp.clip(t - w // 2, 0, L)
    hi = jnp.clip(t - w // 2 + w, 0, L)
    return jnp.maximum(hi - lo, 1).astype(F32)


def _running_sum(v, w):
    k = 1
    while k < w:
        v = v + _shift_rows(v, k)
        k *= 2
    return v


def _pool_fwd(u, poolw, pscale, name, tm=512):
    L, W = u.shape
    G = POOL_GROUP_DIM
    tm = min(tm, L)
    nt = L // tm

    def body(c_ref, p_ref, n_ref, w_ref, s_ref, o_ref, dm_ref):
        i = pl.program_id(0)
        ext = _ext(p_ref[...], c_ref[...], n_ref[...], i, nt)
        t = i * tm + lax.broadcasted_iota(jnp.int32, (tm, 1), 0)
        for g, w in enumerate(POOL_WINDOWS):
            e = ext[:, g * G:(g + 1) * G]
            win = _shift_rows(_running_sum(e, w), -(w // 2 - 1))[HALO:HALO + tm]
            dmx = (win / _window_count(t, w, L) - e[HALO:HALO + tm]).astype(BF16)
            dm_ref[:, g * G:(g + 1) * G] = dmx
            o_ref[:, g * G:(g + 1) * G] = (_dot(dmx, w_ref[g]) * s_ref[:, g * G:(g + 1) * G]).astype(BF16)

    return pl.pallas_call(
        body, name=name, grid=(nt,),
        in_specs=[_tile(tm, W), _halo_prev(tm, W), _halo_next(tm, W, L), _full((4, G, G)), _full((1, W))],
        out_specs=[_tile(tm, W), _tile(tm, W)],
        out_shape=[jax.ShapeDtypeStruct((L, W), BF16)] * 2, compiler_params=_params())(u, u, u, poolw, pscale)


def _pool_bwd(dpool, dmx, poolw, pscale, name, tm=512):
    L, W = dpool.shape
    G = POOL_GROUP_DIM
    tm = min(tm, L)
    nt = L // tm

    def body(c_ref, p_ref, n_ref, dm_ref, w_ref, s_ref, du_ref, dw_ref, acc_ref):
        i = pl.program_id(0)
        ext = _ext(p_ref[...], c_ref[...], n_ref[...], i, nt)
        te = i * tm - HALO + lax.broadcasted_iota(jnp.int32, (tm + 2 * HALO, 1), 0)

        @pl.when(i == 0)
        def _():
            dw_ref[...] = jnp.zeros((4 * G, G), F32)

        rows = []
        for g, w in enumerate(POOL_WINDOWS):
            sc = s_ref[:, g * G:(g + 1) * G]
            dpre = (ext[:, g * G:(g + 1) * G] * sc).astype(BF16)
            dd = _dot_nt(dpre, w_ref[g])
            spread = _shift_rows(_running_sum(dd / _window_count(te, w, L), w), -(w // 2))
            du_ref[:, g * G:(g + 1) * G] = (spread - dd)[HALO:HALO + tm]
            dmx_g = dm_ref[:, g * G:(g + 1) * G]
            rows.append(_sum0(c_ref[:, g * G:(g + 1) * G] * _dot(dmx_g, w_ref[g])))
            dw_ref[g * G:(g + 1) * G, :] += _dot_tn(dmx_g, dpre[HALO:HALO + tm])
        _acc_rows(acc_ref, i == 0, [jnp.concatenate(rows, axis=1)])

    return pl.pallas_call(
        body, name=name, grid=(nt,),
        in_specs=[_tile(tm, W), _halo_prev(tm, W), _halo_next(tm, W, L), _tile(tm, W), _full((4, G, G)),
                  _full((1, W))],
        out_specs=[_tile(tm, W), _full((4 * G, G)), _full((8, W))],
        out_shape=[jax.ShapeDtypeStruct((L, W), F32), jax.ShapeDtypeStruct((4 * G, G), F32),
                   jax.ShapeDtypeStruct((8, W), F32)],
        compiler_params=_params())(dpool, dpool, dpool, dmx, poolw, pscale)


def _conv3(z, cw):
    return _shift_rows(z, 1) * cw[0] + z * cw[1] + _shift_rows(z, -1) * cw[2]


def _conv_fwd(x, prm, wing, woutg, name, tm=512):
    L, D = x.shape
    Ws = wing.shape[-1]
    tm = min(tm, L)
    nt = L // tm
    te = tm + 2 * HALO

    def body(c_ref, p_ref, n_ref, prm_ref, wi_ref, wo_ref, xo_ref, y_ref, b_ref):
        i = pl.program_id(0)
        xe = jnp.concatenate([p_ref[...], c_ref[...], n_ref[...]], axis=0)
        hn, _, _, _ = _nm(xe, prm_ref[3:4, :], prm_ref[0:1, :], prm_ref[1:2, :])
        hb = hn.astype(BF16)
        proj = jnp.concatenate([_dot(hb, wi_ref[s]) for s in range(4)], axis=1)
        bg, cg, xin = proj[:, :D], proj[:, D:2 * D], proj[:, 2 * D:]
        tpos = i * tm - HALO + lax.broadcasted_iota(jnp.int32, (te, 1), 0)
        valid = ((tpos >= 0) & (tpos < L)).astype(F32)
        yc = _conv3(cg * xin * valid, [prm_ref[4 + k:5 + k, :] for k in range(3)])
        h2 = (bg * yc)[HALO:HALO + tm].astype(BF16)
        yv = _dot(h2, wo_ref[...])
        y_ref[...] = yv.astype(BF16)
        xo_ref[...] = c_ref[...] + prm_ref[2:3, :] * yv
        b_ref[...] = proj[HALO:HALO + tm].astype(BF16)

    return pl.pallas_call(
        body, name=name, grid=(nt,),
        in_specs=[_tile(tm, D), _halo_prev(tm, D), _halo_next(tm, D, L), _full((8, D)),
                  _resident((None, 4, D, Ws), lambda i: (0, 0, 0, 0)),
                  _resident((None, D, D), lambda i: (0, 0, 0))],
        out_specs=[_tile(tm, D), _tile(tm, D), _tile(tm, 3 * D)],
        out_shape=[jax.ShapeDtypeStruct((L, D), F32), jax.ShapeDtypeStruct((L, D), BF16),
                   jax.ShapeDtypeStruct((L, 3 * D), BF16)],
        compiler_params=_params())(x, x, x, prm, wing, woutg)


def _conv_bwd(dout, x, y, bcx, prm, wing, woutg, name, tm=256):
    L, D = x.shape
    Ws = wing.shape[-1]
    tm = min(tm, L)
    nt = L // tm
    te = tm + 2 * HALO

    def body(dc_ref, dp_ref, dn_ref, x_ref, y_ref, bc_ref, bp_ref, bn_ref, prm_ref, wi_ref, wo_ref,
             dx_ref, dpr_ref, h2_ref, dy_ref, acc_ref):
        i = pl.program_id(0)
        gain, shift, scale, gate = prm_ref[3:4, :], prm_ref[0:1, :], prm_ref[1:2, :], prm_ref[2:3, :]
        taps = [prm_ref[4 + k:5 + k, :] for k in range(3)]
        do = dc_ref[...]
        doe = _ext(dp_ref[...], do, dn_ref[...], i, nt)
        dye = (gate * doe).astype(BF16)
        dy_ref[...] = dye[HALO:HALO + tm]
        dh2 = _dot_nt(dye, wo_ref[...])
        be = jnp.concatenate([bp_ref[...], bc_ref[...], bn_ref[...]], axis=0).astype(F32)
        bg, cg, xin = be[:, :D], be[:, D:2 * D], be[:, 2 * D:]
        tpos = i * tm - HALO + lax.broadcasted_iota(jnp.int32, (te, 1), 0)
        valid = ((tpos >= 0) & (tpos < L)).astype(F32)
        z = cg * xin * valid
        yc = _conv3(z, taps)
        dyc = dh2 * bg
        h2_ref[...] = (bg * yc)[HALO:HALO + tm].astype(BF16)
        dz = _conv3(dyc, taps[::-1]) * valid
        dproj = jnp.concatenate([dh2 * yc, dz * xin, dz * cg], axis=1)[HALO:HALO + tm].astype(BF16)
        dpr_ref[...] = dproj
        dhn = jnp.zeros((tm, D), F32)
        for s in range(4):
            dhn = dhn + _dot_nt(dproj[:, s * Ws:(s + 1) * Ws], wi_ref[s])
        _, xhat, r, nrm = _nm(x_ref[...], gain, shift, scale)
        dx, dshift, dscale, dgn = _nm_bwd(dhn, xhat, r, nrm, gain, scale)
        dx_ref[...] = do + dx
        dgate = _sum0(do * y_ref[...].astype(F32))
        dtaps = [_sum0((dyc * _shift_rows(z, 1 - k))[HALO:HALO + tm]) for k in range(3)]
        _acc_rows(acc_ref, i == 0, [dshift, dscale, dgate, dgn] + dtaps)

    return pl.pallas_call(
        body, name=name, grid=(nt,),
        in_specs=[_tile(tm, D), _halo_prev(tm, D), _halo_next(tm, D, L), _tile(tm, D), _tile(tm, D),
                  _tile(tm, 3 * D), _halo_prev(tm, 3 * D), _halo_next(tm, 3 * D, L), _full((8, D)),
                  _resident((None, 4, D, Ws), lambda i: (0, 0, 0, 0)),
                  _resident((None, D, D), lambda i: (0, 0, 0))],
        out_specs=[_tile(tm, D), _tile(tm, 3 * D), _tile(tm, D), _tile(tm, D), _full((8, D))],
        out_shape=[jax.ShapeDtypeStruct((L, D), F32), jax.ShapeDtypeStruct((L, 3 * D), BF16),
                   jax.ShapeDtypeStruct((L, D), BF16), jax.ShapeDtypeStruct((L, D), BF16),
                   jax.ShapeDtypeStruct((8, D), F32)],
        compiler_params=_params())(dout, dout, dout, x, y, bcx, bcx, bcx, prm, wing, woutg)


def _loss_head(x, tgt, fg, name, tm=512):
    L, D = x.shape
    tm = min(tm, L)

    def body(x_ref, t_ref, g_ref, dx_ref, acc_ref):
        i = pl.program_id(0)
        xv = x_ref[...]
        g = g_ref[...]
        r = lax.rsqrt(jnp.mean(xv * xv, axis=-1, keepdims=True) + RMS_EPS)
        xhat = xv * r
        err = xhat * g - t_ref[...]
        part = 0.5 * jnp.sum(jnp.mean(err * err, axis=-1, keepdims=True), axis=0, keepdims=True)
        dy = err * (1.0 / D)
        dxh = dy * g
        dx_ref[...] = r * (dxh - xhat * jnp.mean(dxh * xhat, axis=-1, keepdims=True))
        _acc_rows(acc_ref, i == 0, [_sum0(dy * xhat), jnp.broadcast_to(part, (1, D))])

    return pl.pallas_call(
        body, name=name, grid=(L // tm,), in_specs=[_tile(tm, D), _tile(tm, D), _full((1, D))],
        out_specs=[_tile(tm, D), _full((8, D))],
        out_shape=[jax.ShapeDtypeStruct((L, D), F32), jax.ShapeDtypeStruct((8, D), F32)],
        compiler_params=_params())(x, tgt, fg)


def _mod_fwd(cond, mod_w, mod_b, name, tn=768):
    nl, D, N = mod_w.shape
    tn = min(tn, N)

    def body(c_ref, w_ref, b_ref, o_ref):
        cv = c_ref[...]
        s = (cv * _sigmoid(cv)).astype(BF16)
        o_ref[...] = _dot(s, w_ref[...].astype(BF16)) + b_ref[...]

    return pl.pallas_call(
        body, name=name, grid=(nl, N // tn),
        in_specs=[pl.BlockSpec((16, D), lambda l, j: (0, 0)), pl.BlockSpec((None, D, tn), lambda l, j: (l, 0, j)),
                  pl.BlockSpec((None, 1, tn), lambda l, j: (l, 0, j))],
        out_specs=pl.BlockSpec((None, 16, tn), lambda l, j: (l, 0, j)),
        out_shape=jax.ShapeDtypeStruct((nl, 16, N), F32), compiler_params=_params())(cond, mod_w, mod_b)


def _mod_bwd(cond, dm, mod_w, name, tn=768):
    nl, D, N = mod_w.shape
    tn = min(tn, N)

    def body(c_ref, d_ref, w_ref, dw_ref, dc_ref):
        first = (pl.program_id(0) == 0) & (pl.program_id(1) == 0)
        cv = c_ref[...]
        s = (cv * _sigmoid(cv)).astype(BF16)
        d = d_ref[...].astype(BF16)
        dw_ref[...] = _dot_tn(s, d)

        @pl.when(first)
        def _():
            dc_ref[...] = jnp.zeros((16, D), F32)

        dc_ref[...] += _dot_nt(d, w_ref[...].astype(BF16))

    return pl.pallas_call(
        body, name=name, grid=(nl, N // tn),
        in_specs=[pl.BlockSpec((16, D), lambda l, j: (0, 0)), pl.BlockSpec((None, 16, tn), lambda l, j: (l, 0, j)),
                  pl.BlockSpec((None, D, tn), lambda l, j: (l, 0, j))],
        out_specs=[pl.BlockSpec((None, D, tn), lambda l, j: (l, 0, j)), pl.BlockSpec((16, D), lambda l, j: (0, 0))],
        out_shape=[jax.ShapeDtypeStruct((nl, D, N), F32), jax.ShapeDtypeStruct((16, D), F32)],
        compiler_params=_params())(cond, dm, mod_w)


def _mod_small_grads(dm_all, cond, dsilu_parts, name):
    nl, _, N = dm_all.shape
    D = cond.shape[1]

    def body(d_ref, c_ref, p_ref, db_ref, dc_ref):
        for l in range(nl):
            db_ref[l] = _sum0(d_ref[l])
        tot = p_ref[0, 8:9, :]
        for k in range(1, N_CHIPS):
            tot = tot + p_ref[2 * k, 8:9, :]
        cv = c_ref[8:9, :]
        sg = _sigmoid(cv)
        dc_ref[...] = tot * (sg * (1.0 + cv * (1.0 - sg)))

    return pl.pallas_call(
        body, name=name, out_shape=[jax.ShapeDtypeStruct((nl, 1, N), F32), jax.ShapeDtypeStruct((1, D), F32)],
    )(dm_all, cond, dsilu_parts)


def _prm(rows, D):
    rows = [r.reshape(1, D) for r in rows]
    return jnp.concatenate(rows + [jnp.zeros((8 - len(rows), D), F32)], axis=0)


def _heads(a):
    L = a.shape[0]
    return a.reshape(L, NA_HEADS, NA_HEAD_DIM).transpose(1, 0, 2)


def _unheads(a):
    return a.transpose(1, 0, 2).reshape(a.shape[1], NA_HEADS * NA_HEAD_DIM)


def kernel(x, c, ctx, c_ctx, mod_w, mod_b, norm_g, ffn_w13, ffn_w2, even_w_in, even_w_out, na_rpb, pool_w, pool_scale, conv_w_in, conv_w, conv_w_out, final_g, loss_target, m_c_ctx, m_mod_w, m_mod_b, m_norm_g, m_ffn_w13, m_ffn_w2, m_even_w_in, m_even_w_out, m_na_rpb, m_pool_w, m_pool_scale, m_conv_w_in, m_conv_w, m_conv_w_out, m_final_g, v_c_ctx, v_mod_w, v_mod_b, v_norm_g, v_ffn_w13, v_ffn_w2, v_even_w_in, v_even_w_out, v_na_rpb, v_pool_w, v_pool_scale, v_conv_w_in, v_conv_w, v_conv_w_out, v_final_g):
    xi, yi, ci = lax.axis_index("x"), lax.axis_index("y"), lax.axis_index("c")
    chip = 2 * xi + yi
    dev = 4 * xi + 2 * yi + ci
    _, L, D = x.shape
    C = ctx.shape[1]
    Ds = D // N_CHIPS
    Nm = mod_w.shape[-1]
    Fh = ffn_w13.shape[-1]
    Fq = ffn_w2.shape[2]
    assert ffn_w13.shape[:2] == (2, 2) and Fh == 2 * Fq and L % GRID_W == 0 and L // GRID_W >= NA_KH
    x0, ctx0, tgt = x[0], ctx[0], loss_target[0]

    pad = lambda a: jnp.pad(a, ((0, 0), (0, D - a.shape[1])))
    pack1 = jnp.concatenate([c, pad(norm_g.reshape(6, Ds)), pad(conv_w.reshape(3, Ds)), jnp.zeros((6, D), F32)], axis=0)
    g1 = _small_all_gather(pack1, "ag_cond")
    cond = jnp.concatenate([g1[:, 0], c_ctx[None], jnp.zeros((7, D), F32)], axis=0)
    norm_full = jnp.concatenate([g1[2 * k, 1:7, :Ds] for k in range(N_CHIPS)], axis=1).reshape(2, 3, D)
    convw_full = jnp.concatenate([g1[2 * k, 7:10, :Ds] for k in range(N_CHIPS)], axis=1)

    mod_b_loc = lax.dynamic_slice_in_dim(mod_b, chip * Nm, Nm, axis=1).reshape(2, 1, Nm)
    m_loc = _mod_fwd(cond, mod_w, mod_b_loc, "mod_fwd")
    g2 = _small_all_gather(m_loc.reshape(32, Nm), "ag_mod")
    m_all = jnp.concatenate([g2[2 * k] for k in range(N_CHIPS)], axis=1).reshape(2, 16, N_MOD, D)
    m_lat = lax.dynamic_index_in_dim(m_all, dev, axis=1, keepdims=False)
    m_ctx = m_all[:, 8]

    def prm(mods, layer, base, gain_idx, extra=()):
        return _prm([mods[layer, base], mods[layer, base + 1], mods[layer, base + 2], norm_full[layer, gain_idx],
                     *extra], D)

    def shard_bf16(w, name):
        return _cast_bf16(w.reshape(-1, w.shape[-1]), name).reshape(-1, *w.shape[-2:])

    w13g, w2g, eing, eoutg, cing, coutg = _gather_shards(
        [shard_bf16(ffn_w13, "cast_w13"), shard_bf16(ffn_w2, "cast_w2"), shard_bf16(even_w_in, "cast_ein"),
         shard_bf16(even_w_out, "cast_eout"), shard_bf16(conv_w_in, "cast_cin"), shard_bf16(conv_w_out, "cast_cout")],
        "ag_weights")
    w13g = w13g.reshape(4, 4, D, Fh)
    eing = eing.reshape(1, 4, D, NA_WIDTH)
    cing = cing.reshape(1, 4, D, conv_w_in.shape[-1])

    p_f1 = prm(m_lat, 0, 0, 0)
    p_mx = prm(m_lat, 0, 3, 1)
    p_f2 = prm(m_lat, 0, 6, 2)
    p_g1 = prm(m_lat, 1, 0, 0)
    p_cv = prm(m_lat, 1, 3, 1, extra=(convw_full[0], convw_full[1], convw_full[2]))
    p_g2 = prm(m_lat, 1, 6, 2)
    pc_f1 = prm(m_ctx, 0, 0, 0)
    pc_mx = prm(m_ctx, 0, 3, 1)

    x1, ab1, y1 = _ffn_fwd(x0, p_f1, w13g, w2g, 0, "ffn_fwd_l0a")
    ctx1, abc, yc = _ffn_fwd(ctx0, pc_f1, w13g, w2g, 0, "ffn_fwd_ctx")
    q, k, v, u, hn_mx = _even_in_fwd(x1, p_mx, eing, "even_in_fwd")
    _, k_c, v_c, _, hn_cx = _even_in_fwd(ctx1, pc_mx, eing, "even_in_ctx")
    bias = _bias_table(na_rpb[0])
    qh, kh, vh, kch, vch = _heads(q), _heads(k), _heads(v), _heads(k_c), _heads(v_c)
    att = _unheads(_attn_fwd(qh, kh, vh, kch, vch, bias, "attn_fwd"))
    pw_b = _cast_bf16(pool_w.reshape(-1, POOL_GROUP_DIM), "cast_poolw").reshape(4, POOL_GROUP_DIM, POOL_GROUP_DIM)
    pool, dmx = _pool_fwd(u, pw_b, pool_scale, "pool_fwd")
    mix = jnp.concatenate([att, pool], axis=1)
    x2, ymx = _even_out_fwd(x1, mix, p_mx, eoutg, "even_out_fwd")
    x3, ab2, y2 = _ffn_fwd(x2, p_f2, w13g, w2g, 1, "ffn_fwd_l0b")
    x4, ab3, y3 = _ffn_fwd(x3, p_g1, w13g, w2g, 2, "ffn_fwd_l1a")
    x5, ycv, bcx = _conv_fwd(x4, p_cv, cing, coutg, "conv_fwd")
    x6, ab4, y4 = _ffn_fwd(x5, p_g2, w13g, w2g, 3, "ffn_fwd_l1b")
    dx6, acc_head = _loss_head(x6, tgt, final_g.reshape(1, D), "loss_head")
    loss = lax.psum(acc_head[1, 0], ("x", "y", "c"))

    def ffn_back(dout, xin, ab, yy, p, t, tag, init13=None, init2=None):
        dx, dab, gact, dy, hn, acc = _ffn_bwd(dout, xin, ab, yy, p, w13g, w2g, t, f"ffn_bwd_{tag}")
        dw13 = _mm_tn(hn, dab, 4, False, f"dw13_{tag}", init=init13)
        dw2 = _mm_tn(gact, dy, 2, True, f"dw2_{tag}", init=init2)
        return dx, acc, dw13, dw2

    dx5, acc_g2, dw13_3, dw2_3 = ffn_back(dx6, x5, ab4, y4, p_g2, 3, "l1b")
    dx4, dproj, h2, dycv, acc_cv = _conv_bwd(dx5, x4, ycv, bcx, p_cv, cing, coutg, "conv_bwd")
    hn_cv = _even_hn(x4, p_cv, "conv_hn")
    dcin = _mm_tn(hn_cv, dproj, 4, False, "dw_cin")
    dcout = _mm_tn(h2, dycv, 1, False, "dw_cout")
    dx3, acc_g1, dw13_2, dw2_2 = ffn_back(dx4, x3, ab3, y3, p_g1, 2, "l1a")
    dx2, acc_f2, dw13_1, dw2_1 = ffn_back(dx3, x2, ab2, y2, p_f2, 1, "l0b")

    dymx, datt, dpool, acc_mxo = _even_out_bwd(dx2, ymx, p_mx, eoutg, "even_out_bwd")
    deout = _mm_tn(mix, dymx, 1, False, "dw_eout")
    du, dpoolw, acc_pool = _pool_bwd(dpool, dmx, pw_b, pool_scale, "pool_bwd")
    doh = _heads(datt)
    dqh, dkh, dvh, dkch, dvch, dbias = _attn_bwd(qh, kh, vh, kch, vch, bias, doh, "attn_bwd")
    drpb = _rpb_grad(dbias, "rpb_grad")
    dx1, dstack, acc_mxi = _even_in_bwd(dx2, x1, _unheads(dqh), _unheads(dkh), _unheads(dvh), du, p_mx, eing,
                                        "even_in_bwd")
    zc = jnp.zeros((C, NA_WIDTH), F32)
    dctx1, dstack_c, accc_mx = _even_in_bwd(jnp.zeros((C, D), F32), ctx1, zc, _unheads(dkch), _unheads(dvch), zc,
                                            pc_mx, eing, "even_in_bwd_ctx")
    dein_c = _mm_tn(hn_cx, dstack_c, 4, False, "dw_ein_ctx")
    dein = _mm_tn(hn_mx, dstack, 4, False, "dw_ein", init=dein_c)
    _, accc_f1, dw13_c, dw2_c = ffn_back(dctx1, ctx0, abc, yc, pc_f1, 0, "ctx")
    dx0, acc_f1, dw13_0, dw2_0 = ffn_back(dx1, x0, ab1, y1, p_f1, 0, "l0a", init13=dw13_c, init2=dw2_c)

    z1 = jnp.zeros((1, D), F32)
    dm_lat = jnp.concatenate([acc_f1[0:3], acc_mxi[0:2], acc_mxo[2:3], acc_f2[0:3],
                              acc_g1[0:3], acc_cv[0:3], acc_g2[0:3]], axis=0)
    dm_ctx = jnp.concatenate([accc_f1[0:3], accc_mx[0:2]] + [z1] * 13, axis=0)
    dnorm = jnp.concatenate([acc_f1[3:4] + accc_f1[3:4], acc_mxi[3:4] + accc_mx[3:4], acc_f2[3:4],
                             acc_g1[3:4], acc_cv[3:4], acc_g2[3:4]], axis=0)
    rpb_flat = jnp.pad(drpb.reshape(-1), (0, 4 * D - drpb.size)).reshape(4, D)
    pack3 = jnp.concatenate([dm_lat, dm_ctx, dnorm, acc_cv[4:7], acc_head[0:1], pad(acc_pool[0:1]), z1,
                             dpoolw.reshape(-1, D), rpb_flat, jnp.zeros((4, D), F32)], axis=0)
    g3 = _small_all_gather(pack3, "ag_small")
    tot = _sum_devices(g3, "sum_small")
    dm_all = jnp.concatenate([g3[:, 0:18].reshape(8, 2, N_MOD * D).transpose(1, 0, 2),
                              tot[18:36].reshape(2, 1, N_MOD * D), jnp.zeros((2, 7, N_MOD * D), F32)], axis=1)
    dm_loc = lax.dynamic_slice_in_dim(dm_all, chip * Nm, Nm, axis=2)
    g_mod_w, dsilu = _mod_bwd(cond, dm_loc, mod_w, "mod_bwd")
    g4 = _small_all_gather(dsilu, "ag_dsilu")
    g_mod_b, g_c_ctx = _mod_small_grads(dm_all, cond, g4, "mod_small")
    g_mod_b = g_mod_b.reshape(2, N_MOD * D)
    g_c_ctx = g_c_ctx.reshape(D)
    g_norm_full = tot[36:42].reshape(2, 3, D)
    g_norm = lax.dynamic_slice_in_dim(g_norm_full, chip * Ds, Ds, axis=2)
    g_conv_w = lax.dynamic_slice_in_dim(tot[42:45], chip * Ds, Ds, axis=1).reshape(1, 3, Ds)
    g_final = tot[45]
    g_pscale = tot[46:47, :pool_scale.shape[1]]
    g_poolw = tot[48:112].reshape(pool_w.shape)
    g_rpb = tot[112:116].reshape(-1)[:na_rpb.size].reshape(na_rpb.shape)

    red = _reduce_scatter([dw13_0, dw13_1, dw13_2, dw13_3, dw2_0, dw2_1, dw2_2, dw2_3, dein, deout, dcin, dcout], "w")
    g_w13 = jnp.stack(red[0:4]).reshape(ffn_w13.shape)
    g_w2 = jnp.stack(red[4:8]).reshape(ffn_w2.shape)
    g_ein, g_eout, g_cin, g_cout = (red[8][None], red[9][None], red[10][None], red[11][None])

    grads = [g_c_ctx, g_mod_w, g_mod_b, g_norm, g_w13, g_w2, g_ein, g_eout, g_rpb, g_poolw, g_pscale, g_cin,
             g_conv_w, g_cout, g_final]
    weights = [c_ctx, mod_w, mod_b, norm_g, ffn_w13, ffn_w2, even_w_in, even_w_out, na_rpb, pool_w, pool_scale,
               conv_w_in, conv_w, conv_w_out, final_g]
    ms = [m_c_ctx, m_mod_w, m_mod_b, m_norm_g, m_ffn_w13, m_ffn_w2, m_even_w_in, m_even_w_out, m_na_rpb, m_pool_w,
          m_pool_scale, m_conv_w_in, m_conv_w, m_conv_w_out, m_final_g]
    vs = [v_c_ctx, v_mod_w, v_mod_b, v_norm_g, v_ffn_w13, v_ffn_w2, v_even_w_in, v_even_w_out, v_na_rpb, v_pool_w,
          v_pool_scale, v_conv_w_in, v_conv_w, v_conv_w_out, v_final_g]
    names = ["c_ctx", "mod_w", "mod_b", "norm_g", "ffn_w13", "ffn_w2", "even_w_in", "even_w_out", "na_rpb", "pool_w",
             "pool_scale", "conv_w_in", "conv_w", "conv_w_out", "final_g"]
    deltas, new_m, new_v = [], [], []
    for n, w, g, m, vv in zip(names, weights, grads, ms, vs):
        g = g.reshape(w.shape)
        if w.ndim == 1:
            d, mn, vn = (t.reshape(w.shape) for t in _adamw(w[None], g[None], m[None], vv[None], f"adamw_{n}"))
        else:
            d, mn, vn = _adamw(w, g, m, vv, f"adamw_{n}")
        deltas.append(d)
        new_m.append(mn)
        new_v.append(vn)
    grads = [g.reshape(w.shape) for g, w in zip(grads, weights)]
    return (loss, dx0[None], *grads, *deltas, *new_m, *new_v)


def _even_hn(x, prm, name, tm=512):
    L, D = x.shape
    tm = min(tm, L)

    def body(x_ref, p_ref, o_ref):
        hn, _, _, _ = _nm(x_ref[...], p_ref[3:4, :], p_ref[0:1, :], p_ref[1:2, :])
        o_ref[...] = hn.astype(BF16)

    return pl.pallas_call(
        body, name=name, grid=(L // tm,), in_specs=[_tile(tm, D), _full((8, D))], out_specs=_tile(tm, D),
        out_shape=jax.ShapeDtypeStruct((L, D), BF16))(x, prm)
```

```python
import jax
import jax.numpy as jnp
from jax import lax
from jax.experimental import pallas as pl
from jax.experimental.pallas import tpu as pltpu

F32 = jnp.float32
BF16 = jnp.bfloat16
MESH = pl.DeviceIdType.MESH

GRID_W = 64
NA_HEADS = 8
NA_HEAD_DIM = 64
NA_KH = 8
NA_KW = 16
NA_WIDTH = NA_HEADS * NA_HEAD_DIM
POOL_WINDOWS = (2, 4, 8, 16)
POOL_GROUP_DIM = 128
N_MOD = 9
RMS_EPS = 1e-6
NEG_INF = -1e30
ADAM_LR, ADAM_B1, ADAM_B2, ADAM_EPS, ADAM_WD, ADAM_STEP = 0.001, 0.9, 0.999, 1e-08, 0.01, 10

HALO = 16
VMEM_LIMIT = 56 * 1024 * 1024
N_CHIPS = 4
N_DEV = 8


def _dot(a, b):
    return jnp.dot(a, b, preferred_element_type=F32)


def _dot_nt(a, b):
    return lax.dot_general(a, b, (((1,), (1,)), ((), ())), preferred_element_type=F32)


def _dot_tn(a, b):
    return lax.dot_general(a, b, (((0,), (0,)), ((), ())), preferred_element_type=F32)


def _sigmoid(a):
    return 1.0 / (1.0 + jnp.exp(-a))


def _sum0(v):
    return jnp.sum(v, axis=0, keepdims=True)


def _nm(x, g, shift, scale):
    r = lax.rsqrt(jnp.mean(x * x, axis=-1, keepdims=True) + RMS_EPS)
    xhat = x * r
    nrm = xhat * g
    return nrm * (1.0 + scale) + shift, xhat, r, nrm


def _nm_bwd(dhn, xhat, r, nrm, g, scale):
    dshift = _sum0(dhn)
    dscale = _sum0(dhn * nrm)
    dnrm = dhn * (1.0 + scale)
    dgn = _sum0(dnrm * xhat)
    dxh = dnrm * g
    dx = r * (dxh - xhat * jnp.mean(dxh * xhat, axis=-1, keepdims=True))
    return dx, dshift, dscale, dgn


def _acc_rows(acc_ref, first, rows):
    @pl.when(first)
    def _():
        acc_ref[...] = jnp.zeros(acc_ref.shape, acc_ref.dtype)
    for k, row in enumerate(rows):
        if row is not None:
            acc_ref[k:k + 1, :] += row


def _shift_rows(v, k):
    n = v.shape[0]
    k = k % n
    return v if k == 0 else pltpu.roll(v, k, 0)


def _tile(tm, w):
    return pl.BlockSpec((tm, w), lambda i: (i, 0))


def _full(shape):
    nd = len(shape)
    return pl.BlockSpec(shape, lambda i: (0,) * nd)


def _resident(block, imap):
    return pl.BlockSpec(block, imap, pipeline_mode=pl.Buffered(1))


def _halo_prev(tm, w):
    return pl.BlockSpec((HALO, w), lambda i: (jnp.maximum(i * (tm // HALO) - 1, 0), 0))


def _halo_next(tm, w, L):
    return pl.BlockSpec((HALO, w), lambda i: (jnp.minimum((i + 1) * (tm // HALO), L // HALO - 1), 0))


def _params(vmem=VMEM_LIMIT):
    return pltpu.CompilerParams(vmem_limit_bytes=vmem)


def _pick_rows(rows, cols, itemsize=4, target=1 << 20):
    best = None
    for t in range(8, rows + 1, 8):
        if rows % t == 0 and t * cols * itemsize <= target:
            best = t
    return best if best is not None else rows


def _ext(prev, cur, nxt, i, nt):
    prev = jnp.where(i > 0, prev, jnp.zeros_like(prev))
    nxt = jnp.where(i < nt - 1, nxt, jnp.zeros_like(nxt))
    return jnp.concatenate([prev, cur, nxt], axis=0)


def _cast_bf16(a2d, name):
    rows, cols = a2d.shape
    tr = _pick_rows(rows, cols)

    def body(a_ref, o_ref):
        o_ref[...] = a_ref[...].astype(BF16)

    return pl.pallas_call(
        body, name=name, grid=(rows // tr,), in_specs=[_tile(tr, cols)], out_specs=_tile(tr, cols),
        out_shape=jax.ShapeDtypeStruct((rows, cols), BF16))(a2d)


def _sum_devices(g, name):
    n, rows, cols = g.shape
    tr = _pick_rows(rows, cols, target=1 << 18)

    def body(g_ref, o_ref):
        s = g_ref[0]
        for d in range(1, n):
            s = s + g_ref[d]
        o_ref[...] = s

    return pl.pallas_call(
        body, name=name, grid=(rows // tr,), in_specs=[pl.BlockSpec((n, tr, cols), lambda i: (0, i, 0))],
        out_specs=_tile(tr, cols), out_shape=jax.ShapeDtypeStruct((rows, cols), F32))(g)


def _adamw(w, g, m, v, name):
    shape = w.shape
    cols = shape[-1]
    rows = w.size // cols
    w2, g2, m2, v2 = (t.reshape(rows, cols) for t in (w, g, m, v))
    tr = _pick_rows(rows, cols)
    c1 = 1.0 - ADAM_B1 ** ADAM_STEP
    c2 = 1.0 - ADAM_B2 ** ADAM_STEP

    def body(w_ref, g_ref, m_ref, v_ref, d_ref, mo_ref, vo_ref):
        gg = g_ref[...]
        mn = ADAM_B1 * m_ref[...] + (1.0 - ADAM_B1) * gg
        vn = ADAM_B2 * v_ref[...] + (1.0 - ADAM_B2) * (gg * gg)
        d_ref[...] = -ADAM_LR * ((mn / c1) / (jnp.sqrt(vn / c2) + ADAM_EPS) + ADAM_WD * w_ref[...])
        mo_ref[...] = mn
        vo_ref[...] = vn

    outs = pl.pallas_call(
        body, name=name, grid=(rows // tr,), in_specs=[_tile(tr, cols)] * 4, out_specs=[_tile(tr, cols)] * 3,
        out_shape=[jax.ShapeDtypeStruct((rows, cols), F32)] * 3)(w2, g2, m2, v2)
    return tuple(o.reshape(shape) for o in outs)


def _mesh_pos():
    x, y, c = lax.axis_index("x"), lax.axis_index("y"), lax.axis_index("c")
    chips = [(1 - x, y), (x, 1 - y), (1 - x, 1 - y)]
    return x, y, c, chips


def _hbm_specs(n):
    return [pl.BlockSpec(memory_space=pltpu.HBM)] * n


def _small_all_gather(v, name):
    rows, w = v.shape

    def body(x_ref, out_ref, send_sems, recv_sems, local_sem):
        x, y, c, chips = _mesh_pos()
        me, sibling = (x, y, c), (x, y, 1 - c)

        def blk(px, py, pc):
            return out_ref.at[4 * px + 2 * py + pc]

        def copy(k, block, to, src=None):
            return pltpu.make_async_remote_copy(
                src_ref=blk(*block) if src is None else src, dst_ref=blk(*block),
                send_sem=send_sems.at[k], recv_sem=recv_sems.at[k], device_id=to, device_id_type=MESH)

        mine = pltpu.make_async_copy(x_ref, blk(*me), local_sem)
        mine.start()
        first = [copy(0, me, sibling, src=x_ref)]
        first += [copy(1 + j, me, (*chip, c), src=x_ref) for j, chip in enumerate(chips)]
        for cp in first:
            cp.start()
        passed = [copy(4 + j, (*chip, c), sibling) for j, chip in enumerate(chips)]
        for j, chip in enumerate(chips):
            copy(1 + j, (*chip, c), me).wait_recv()
            passed[j].start()
        copy(0, sibling, me).wait_recv()
        for j, chip in enumerate(chips):
            copy(4 + j, (*chip, 1 - c), me).wait_recv()
        for cp in first + passed:
            cp.wait_send()
        mine.wait()

    return pl.pallas_call(
        body, name=name, out_shape=jax.ShapeDtypeStruct((N_DEV, rows, w), v.dtype),
        in_specs=[pl.BlockSpec(memory_space=pltpu.VMEM)], out_specs=pl.BlockSpec(memory_space=pltpu.VMEM),
        scratch_shapes=[pltpu.SemaphoreType.DMA((7,)), pltpu.SemaphoreType.DMA((7,)), pltpu.SemaphoreType.DMA],
    )(v)


def _gather_shards(shards, name):
    n = len(shards)

    def body(*refs):
        ins, outs = refs[:n], refs[n:2 * n]
        send_sems, recv_sems, local_sems = refs[2 * n:]
        x, y, c, chips = _mesh_pos()
        k = 2 * x + y
        sibling = (x, y, 1 - c)

        def window(t, chip_k, half):
            r = ins[t].shape[1]
            return outs[t].at[:, pl.ds(chip_k * r + half * (r // 2), r // 2), :]

        def copy(t, j, chip_k, half, to, src=None):
            return pltpu.make_async_remote_copy(
                src_ref=window(t, chip_k, half) if src is None else src, dst_ref=window(t, chip_k, half),
                send_sem=send_sems.at[6 * t + j], recv_sem=recv_sems.at[6 * t + j], device_id=to, device_id_type=MESH)

        started, local = [], []
        for t in range(n):
            r = ins[t].shape[1]
            lc = pltpu.make_async_copy(ins[t], outs[t].at[:, pl.ds(k * r, r), :], local_sems.at[t])
            lc.start()
            local.append(lc)
            src = ins[t].at[:, pl.ds(c * (r // 2), r // 2), :]
            for j, chip in enumerate(chips):
                cp = copy(t, j, k, c, (*chip, c), src=src)
                cp.start()
                started.append(cp)
        for t in range(n):
            for j, chip in enumerate(chips):
                kj = 2 * chip[0] + chip[1]
                copy(t, j, kj, c, sibling).wait_recv()
                cp = copy(t, 3 + j, kj, c, sibling)
                cp.start()
                started.append(cp)
        for t in range(n):
            for j, chip in enumerate(chips):
                kj = 2 * chip[0] + chip[1]
                copy(t, 3 + j, kj, 1 - c, sibling).wait_recv()
        for cp in started:
            cp.wait_send()
        for lc in local:
            lc.wait()

    out_shape = [jax.ShapeDtypeStruct((s.shape[0], N_CHIPS * s.shape[1], s.shape[2]), s.dtype) for s in shards]
    return pl.pallas_call(
        body, name=name, out_shape=out_shape, in_specs=_hbm_specs(n), out_specs=_hbm_specs(n),
        scratch_shapes=[pltpu.SemaphoreType.DMA((6 * n,)), pltpu.SemaphoreType.DMA((6 * n,)),
                        pltpu.SemaphoreType.DMA((n,))],
    )(*shards)


def _chunk_rows(h, w):
    best = 16
    for t in range(16, h + 1, 16):
        if h % t == 0 and t * w * 4 <= (1 << 20):
            best = t
    return best


def _pair_sum(part, pos, name):
    _, h, w = part.shape
    cr = _chunk_rows(h, w)
    nc = h // cr

    def body(pos_ref, own_ref, send_ref, s_ref, sb_ref, rbuf, send_sems, recv_sems):
        x, y, c, _ = _mesh_pos()
        slot = pl.program_id(0) % 2
        cp = pltpu.make_async_remote_copy(
            src_ref=send_ref, dst_ref=rbuf.at[slot], send_sem=send_sems.at[slot], recv_sem=recv_sems.at[slot],
            device_id=(x, y, 1 - c), device_id_type=MESH)
        cp.start()
        cp.wait_recv()
        s = own_ref[...] + rbuf[slot]
        s_ref[...] = s
        sb_ref[...] = s.astype(BF16)
        cp.wait_send()

    grid_spec = pltpu.PrefetchScalarGridSpec(
        num_scalar_prefetch=1, grid=(4 * nc,),
        in_specs=[pl.BlockSpec((cr, w), lambda k, p: ((2 * (k // nc) + p[1]) * nc + k % nc, 0)),
                  pl.BlockSpec((cr, w), lambda k, p: ((2 * (k // nc) + 1 - p[1]) * nc + k % nc, 0))],
        out_specs=[pl.BlockSpec((cr, w), lambda k, p: (k, 0))] * 2,
        scratch_shapes=[pltpu.VMEM((2, cr, w), F32), pltpu.SemaphoreType.DMA((2,)), pltpu.SemaphoreType.DMA((2,))])
    part2 = part.reshape(8 * h, w)
    s, sb = pl.pallas_call(
        body, name=name, grid_spec=grid_spec,
        out_shape=[jax.ShapeDtypeStruct((4 * h, w), F32), jax.ShapeDtypeStruct((4 * h, w), BF16)],
    )(pos, part2, part2)
    return s.reshape(4, h, w), sb.reshape(4, h, w)


def _scatter_to_chips(sums_bf16, name):
    n = len(sums_bf16)

    def body(*refs):
        b16s, got = refs[:n], refs[n:2 * n]
        send_sems, recv_sems = refs[2 * n:]
        x, y, c, chips = _mesh_pos()
        started = []
        for t in range(n):
            for j, chip in enumerate(chips):
                cp = pltpu.make_async_remote_copy(
                    src_ref=b16s[t].at[2 * chip[0] + chip[1]], dst_ref=got[t].at[j],
                    send_sem=send_sems.at[3 * t + j], recv_sem=recv_sems.at[3 * t + j],
                    device_id=(*chip, c), device_id_type=MESH)
                cp.start()
                started.append(cp)
        for cp in started:
            cp.wait_recv()
        for cp in started:
            cp.wait_send()

    got_shape = [jax.ShapeDtypeStruct((3,) + s.shape[1:], BF16) for s in sums_bf16]
    return pl.pallas_call(
        body, name=name, out_shape=got_shape, in_specs=_hbm_specs(n), out_specs=_hbm_specs(n),
        scratch_shapes=[pltpu.SemaphoreType.DMA((3 * n,)), pltpu.SemaphoreType.DMA((3 * n,))],
    )(*sums_bf16)


def _sum_and_join(sums, got, pos, name):
    _, h, w = sums.shape
    cr = _chunk_rows(h, w)

    def body(pos_ref, mine_ref, got_ref, o_ref, ebuf, rbuf, send_sems, recv_sems):
        x, y, c, _ = _mesh_pos()
        slot = pl.program_id(0) % 2
        e = mine_ref[...]
        for j in range(3):
            e = e + got_ref[j].astype(F32)
        ebuf[slot] = e
        cp = pltpu.make_async_remote_copy(
            src_ref=ebuf.at[slot], dst_ref=rbuf.at[slot], send_sem=send_sems.at[slot], recv_sem=recv_sems.at[slot],
            device_id=(x, y, 1 - c), device_id_type=MESH)
        cp.start()
        o_ref[pos_ref[1]] = e
        cp.wait_recv()
        o_ref[1 - pos_ref[1]] = rbuf[slot]
        cp.wait_send()

    grid_spec = pltpu.PrefetchScalarGridSpec(
        num_scalar_prefetch=1, grid=(h // cr,),
        in_specs=[pl.BlockSpec((None, cr, w), lambda i, p: (p[0], i, 0)),
                  pl.BlockSpec((3, cr, w), lambda i, p: (0, i, 0))],
        out_specs=pl.BlockSpec((2, cr, w), lambda i, p: (0, i, 0)),
        scratch_shapes=[pltpu.VMEM((2, cr, w), F32), pltpu.VMEM((2, cr, w), F32),
                        pltpu.SemaphoreType.DMA((2,)), pltpu.SemaphoreType.DMA((2,))])
    return pl.pallas_call(
        body, name=name, grid_spec=grid_spec, out_shape=jax.ShapeDtypeStruct((2, h, w), F32),
    )(pos, sums, got)


def _reduce_scatter(parts, pos, tag):
    sums, sums_b = [], []
    for t, p in enumerate(parts):
        h = p.shape[0] // 8
        s, sb = _pair_sum(p.reshape(8, h, p.shape[1]), pos, f"rs_pair_{tag}_{t}")
        sums.append(s)
        sums_b.append(sb)
    recv = _scatter_to_chips(sums_b, f"rs_scatter_{tag}")
    out = []
    for t, (s, r) in enumerate(zip(sums, recv)):
        full = _sum_and_join(s, r, pos, f"rs_join_{tag}_{t}")
        out.append(full.reshape(2 * full.shape[1], full.shape[2]))
    return out


def _ffn_fwd(x, prm, w13g, w2g, t, name, tm=512):
    L, D = x.shape
    Fh = w13g.shape[-1]
    tm = min(tm, L)

    def body(x_ref, p_ref, w13_ref, w2_ref, xo_ref, ab_ref, y_ref):
        xv = x_ref[...]
        hn, _, _, _ = _nm(xv, p_ref[3:4, :], p_ref[0:1, :], p_ref[1:2, :])
        hb = hn.astype(BF16)
        acc = jnp.zeros((tm, D), F32)
        for j in range(2):
            a = _dot(hb, w13_ref[j])
            b = _dot(hb, w13_ref[2 + j])
            ab_ref[:, j * Fh:(j + 1) * Fh] = a.astype(BF16)
            ab_ref[:, (2 + j) * Fh:(3 + j) * Fh] = b.astype(BF16)
            g = (a * _sigmoid(a) * b).astype(BF16)
            acc = acc + _dot(g, w2_ref[j * Fh:(j + 1) * Fh, :])
        y_ref[...] = acc.astype(BF16)
        xo_ref[...] = xv + (0.5 * p_ref[2:3, :]) * acc

    return pl.pallas_call(
        body, name=name, grid=(L // tm,),
        in_specs=[_tile(tm, D), _full((8, D)),
                  _resident((None, 4, D, Fh), lambda i: (t, 0, 0, 0)),
                  _resident((None, 2 * Fh, D), lambda i: (t, 0, 0))],
        out_specs=[_tile(tm, D), _tile(tm, 4 * Fh), _tile(tm, D)],
        out_shape=[jax.ShapeDtypeStruct((L, D), F32), jax.ShapeDtypeStruct((L, 4 * Fh), BF16),
                   jax.ShapeDtypeStruct((L, D), BF16)],
        compiler_params=_params())(x, prm, w13g, w2g)


def _ffn_bwd(dout, x, ab, y, prm, w13g, w2g, t, name, tm=256):
    L, D = x.shape
    Fh = w13g.shape[-1]
    tm = min(tm, L)

    def body(do_ref, x_ref, ab_ref, y_ref, p_ref, w13_ref, w2_ref, dx_ref, dab_ref, g_ref, dy_ref, hn_ref, acc_ref):
        i = pl.program_id(0)
        do = do_ref[...]
        gain, shift, scale, gate = p_ref[3:4, :], p_ref[0:1, :], p_ref[1:2, :], p_ref[2:3, :]
        hn, xhat, r, nrm = _nm(x_ref[...], gain, shift, scale)
        hn_ref[...] = hn.astype(BF16)
        dgate = 0.5 * _sum0(do * y_ref[...].astype(F32))
        dyb = ((0.5 * gate) * do).astype(BF16)
        dy_ref[...] = dyb
        dhn = jnp.zeros((tm, D), F32)
        for j in range(2):
            dg = _dot_nt(dyb, w2_ref[j * Fh:(j + 1) * Fh, :])
            a = ab_ref[:, j * Fh:(j + 1) * Fh].astype(F32)
            b = ab_ref[:, (2 + j) * Fh:(3 + j) * Fh].astype(F32)
            sg = _sigmoid(a)
            sa = a * sg
            g_ref[:, j * Fh:(j + 1) * Fh] = (sa * b).astype(BF16)
            da = (dg * b * (sg * (1.0 + a * (1.0 - sg)))).astype(BF16)
            db = (dg * sa).astype(BF16)
            dab_ref[:, j * Fh:(j + 1) * Fh] = da
            dab_ref[:, (2 + j) * Fh:(3 + j) * Fh] = db
            dhn = dhn + _dot_nt(da, w13_ref[j]) + _dot_nt(db, w13_ref[2 + j])
        dx, dshift, dscale, dgn = _nm_bwd(dhn, xhat, r, nrm, gain, scale)
        dx_ref[...] = do + dx
        _acc_rows(acc_ref, i == 0, [dshift, dscale, dgate, dgn])

    return pl.pallas_call(
        body, name=name, grid=(L // tm,),
        in_specs=[_tile(tm, D), _tile(tm, D), _tile(tm, 4 * Fh), _tile(tm, D), _full((8, D)),
                  _resident((None, 4, D, Fh), lambda i: (t, 0, 0, 0)),
                  _resident((None, 2 * Fh, D), lambda i: (t, 0, 0))],
        out_specs=[_tile(tm, D), _tile(tm, 4 * Fh), _tile(tm, 2 * Fh), _tile(tm, D), _tile(tm, D), _full((8, D))],
        out_shape=[jax.ShapeDtypeStruct((L, D), F32), jax.ShapeDtypeStruct((L, 4 * Fh), BF16),
                   jax.ShapeDtypeStruct((L, 2 * Fh), BF16), jax.ShapeDtypeStruct((L, D), BF16),
                   jax.ShapeDtypeStruct((L, D), BF16), jax.ShapeDtypeStruct((8, D), F32)],
        compiler_params=_params())(dout, x, ab, y, prm, w13g, w2g)


def _mm_tn(a, b, slabs, a_slabbed, name, init=None, tl=512):
    L = a.shape[0]
    ka = a.shape[1] // slabs if a_slabbed else a.shape[1]
    nb = b.shape[1] if a_slabbed else b.shape[1] // slabs
    tl = min(tl, L)
    has_init = init is not None

    def body(a_ref, b_ref, *rest):
        o_ref = rest[-1]
        step = pl.program_id(1)

        @pl.when(step == 0)
        def _():
            o_ref[...] = rest[0][...] if has_init else jnp.zeros((ka, nb), F32)

        o_ref[...] += _dot_tn(a_ref[...], b_ref[...])

    in_specs = [pl.BlockSpec((tl, ka), (lambda s, l: (l, s)) if a_slabbed else (lambda s, l: (l, 0))),
                pl.BlockSpec((tl, nb), (lambda s, l: (l, 0)) if a_slabbed else (lambda s, l: (l, s)))]
    args = [a, b]
    if has_init:
        in_specs.append(pl.BlockSpec((ka, nb), lambda s, l: (s, 0)))
        args.append(init)
    return pl.pallas_call(
        body, name=name, grid=(slabs, L // tl), in_specs=in_specs,
        out_specs=pl.BlockSpec((ka, nb), lambda s, l: (s, 0)),
        out_shape=jax.ShapeDtypeStruct((slabs * ka, nb), F32), compiler_params=_params())(*args)


def _even_in_fwd(x, prm, wing, name, tm=512):
    L, D = x.shape
    W = wing.shape[-1]
    tm = min(tm, L)

    def body(x_ref, p_ref, w_ref, q_ref, k_ref, v_ref, u_ref, hn_ref):
        hn, _, _, _ = _nm(x_ref[...], p_ref[3:4, :], p_ref[0:1, :], p_ref[1:2, :])
        hb = hn.astype(BF16)
        hn_ref[...] = hb
        q_ref[...] = _dot(hb, w_ref[0]).astype(BF16)
        k_ref[...] = _dot(hb, w_ref[1]).astype(BF16)
        v_ref[...] = _dot(hb, w_ref[2]).astype(BF16)
        u_ref[...] = _dot(hb, w_ref[3])

    return pl.pallas_call(
        body, name=name, grid=(L // tm,),
        in_specs=[_tile(tm, D), _full((8, D)), _resident((None, 4, D, W), lambda i: (0, 0, 0, 0))],
        out_specs=[_tile(tm, W)] * 4 + [_tile(tm, D)],
        out_shape=[jax.ShapeDtypeStruct((L, W), BF16)] * 3 + [jax.ShapeDtypeStruct((L, W), F32),
                                                              jax.ShapeDtypeStruct((L, D), BF16)],
        compiler_params=_params())(x, prm, wing)


def _even_in_bwd(dout, x, dq, dk, dv, du, prm, wing, name, tm=256):
    L, D = x.shape
    W = wing.shape[-1]
    tm = min(tm, L)

    def body(do_ref, x_ref, dq_ref, dk_ref, dv_ref, du_ref, p_ref, w_ref, dx_ref, ds_ref, acc_ref):
        i = pl.program_id(0)
        gain, shift, scale = p_ref[3:4, :], p_ref[0:1, :], p_ref[1:2, :]
        _, xhat, r, nrm = _nm(x_ref[...], gain, shift, scale)
        dhn = jnp.zeros((tm, D), F32)
        for s, ref in enumerate((dq_ref, dk_ref, dv_ref, du_ref)):
            d = ref[...].astype(BF16)
            ds_ref[:, s * W:(s + 1) * W] = d
            dhn = dhn + _dot_nt(d, w_ref[s])
        dx, dshift, dscale, dgn = _nm_bwd(dhn, xhat, r, nrm, gain, scale)
        dx_ref[...] = do_ref[...] + dx
        _acc_rows(acc_ref, i == 0, [dshift, dscale, None, dgn])

    return pl.pallas_call(
        body, name=name, grid=(L // tm,),
        in_specs=[_tile(tm, D), _tile(tm, D)] + [_tile(tm, W)] * 4 +
                 [_full((8, D)), _resident((None, 4, D, W), lambda i: (0, 0, 0, 0))],
        out_specs=[_tile(tm, D), _tile(tm, 4 * W), _full((8, D))],
        out_shape=[jax.ShapeDtypeStruct((L, D), F32), jax.ShapeDtypeStruct((L, 4 * W), BF16),
                   jax.ShapeDtypeStruct((8, D), F32)],
        compiler_params=_params())(dout, x, dq, dk, dv, du, prm, wing)


def _even_out_fwd(x, mix, prm, woutg, name, tm=512):
    L, D = x.shape
    tm = min(tm, L)

    def body(x_ref, m_ref, p_ref, w_ref, xo_ref, y_ref):
        yv = _dot(m_ref[...], w_ref[...])
        y_ref[...] = yv.astype(BF16)
        xo_ref[...] = x_ref[...] + p_ref[2:3, :] * yv

    return pl.pallas_call(
        body, name=name, grid=(L // tm,),
        in_specs=[_tile(tm, D), _tile(tm, D), _full((8, D)), _resident((None, D, D), lambda i: (0, 0, 0))],
        out_specs=[_tile(tm, D), _tile(tm, D)],
        out_shape=[jax.ShapeDtypeStruct((L, D), F32), jax.ShapeDtypeStruct((L, D), BF16)],
        compiler_params=_params())(x, mix, prm, woutg)


def _even_out_bwd(dout, y, prm, woutg, name, tm=512):
    L, D = dout.shape
    W = D // 2
    tm = min(tm, L)

    def body(do_ref, y_ref, p_ref, w_ref, dy_ref, da_ref, dp_ref, acc_ref):
        i = pl.program_id(0)
        do = do_ref[...]
        dgate = _sum0(do * y_ref[...].astype(F32))
        dyb = (p_ref[2:3, :] * do).astype(BF16)
        dy_ref[...] = dyb
        da_ref[...] = _dot_nt(dyb, w_ref[0:W, :]).astype(BF16)
        dp_ref[...] = _dot_nt(dyb, w_ref[W:2 * W, :])
        _acc_rows(acc_ref, i == 0, [None, None, dgate])

    return pl.pallas_call(
        body, name=name, grid=(L // tm,),
        in_specs=[_tile(tm, D), _tile(tm, D), _full((8, D)), _resident((None, D, D), lambda i: (0, 0, 0))],
        out_specs=[_tile(tm, D), _tile(tm, W), _tile(tm, W), _full((8, D))],
        out_shape=[jax.ShapeDtypeStruct((L, D), BF16), jax.ShapeDtypeStruct((L, W), BF16),
                   jax.ShapeDtypeStruct((L, W), F32), jax.ShapeDtypeStruct((8, D), F32)],
        compiler_params=_params())(dout, y, prm, woutg)


def _bias_table(rpb, name):
    H = rpb.shape[0]
    nri, nci = 2 * NA_KH - 1, 2 * NA_KW - 1
    col = jnp.arange(GRID_W)
    rel = (col[None, :] - col[:, None] + (NA_KW - 1)).reshape(1, -1)
    onehot = (rel == jnp.arange(32)[:, None]).astype(F32)
    cs = jnp.clip(col - NA_KW // 2, 0, GRID_W - NA_KW)
    ok = ((col[None, :] >= cs[:, None]) & (col[None, :] < cs[:, None] + NA_KW)).astype(F32).reshape(1, -1)
    rpb2 = jnp.pad(rpb.reshape(H * nri, nci), ((0, 0), (0, 32 - nci)))

    def body(r_ref, e_ref, m_ref, o_ref):
        t = jnp.dot(r_ref[...], e_ref[...], preferred_element_type=F32, precision=lax.Precision.HIGHEST)
        o_ref[...] = jnp.where(m_ref[...] > 0.0, t, NEG_INF)

    tab = pl.pallas_call(body, name=name, out_shape=jax.ShapeDtypeStruct((H * nri, GRID_W * GRID_W), F32))(
        rpb2, onehot, ok)
    tab = tab.reshape(H, nri, GRID_W, GRID_W)
    per_var = [tab[:, NA_KH - 1 - var:2 * NA_KH - 1 - var] for var in range(NA_KH)]
    return jnp.stack(per_var, axis=1).transpose(0, 1, 3, 2, 4).reshape(H, NA_KH, GRID_W, NA_KH * GRID_W)


def _attn_probs(q, kw, kc, bias, scale):
    s_w = _dot_nt(q, kw) * scale + bias
    s_c = _dot_nt(q, kc) * scale
    m = jnp.maximum(jnp.max(s_w, axis=-1, keepdims=True), jnp.max(s_c, axis=-1, keepdims=True))
    e_w = jnp.exp(s_w - m)
    e_c = jnp.exp(s_c - m)
    inv = 1.0 / (jnp.sum(e_w, axis=-1, keepdims=True) + jnp.sum(e_c, axis=-1, keepdims=True))
    return e_w * inv, e_c * inv


def _attn_fwd(qh, kh, vh, kch, vch, bias, name):
    H, L, dh = qh.shape
    C = kch.shape[1]
    R = L // GRID_W
    nwin = NA_KH * GRID_W
    scale = dh ** -0.5

    def body(q_ref, k_ref, v_ref, kc_ref, vc_ref, b_ref, o_ref):
        kc = kc_ref[...]
        vc = vc_ref[...]

        def row(r, carry):
            rs = jnp.clip(r - NA_KH // 2, 0, R - NA_KH)
            q0 = pl.multiple_of(r * GRID_W, GRID_W)
            k0 = pl.multiple_of(rs * GRID_W, GRID_W)
            q = q_ref[pl.ds(q0, GRID_W), :]
            p_w, p_c = _attn_probs(q, k_ref[pl.ds(k0, nwin), :], kc, b_ref[r - rs], scale)
            o = _dot(p_w.astype(BF16), v_ref[pl.ds(k0, nwin), :]) + _dot(p_c.astype(BF16), vc)
            o_ref[pl.ds(q0, GRID_W), :] = o.astype(BF16)
            return carry

        lax.fori_loop(0, R, row, 0, unroll=2)

    head = lambda n: pl.BlockSpec((None, n, dh), lambda h: (h, 0, 0))
    return pl.pallas_call(
        body, name=name, grid=(H,),
        in_specs=[head(L), head(L), head(L), head(C), head(C),
                  pl.BlockSpec((None, NA_KH, GRID_W, nwin), lambda h: (h, 0, 0, 0))],
        out_specs=head(L), out_shape=jax.ShapeDtypeStruct((H, L, dh), BF16),
        compiler_params=_params())(qh, kh, vh, kch, vch, bias)


def _attn_bwd(qh, kh, vh, kch, vch, bias, doh, name):
    H, L, dh = qh.shape
    C = kch.shape[1]
    R = L // GRID_W
    nwin = NA_KH * GRID_W
    scale = dh ** -0.5

    def body(q_ref, k_ref, v_ref, kc_ref, vc_ref, b_ref, do_ref, dq_ref, dk_ref, dv_ref, dkc_ref, dvc_ref, db_ref):
        kc = kc_ref[...]
        vc = vc_ref[...]
        dk_ref[...] = jnp.zeros((L, dh), F32)
        dv_ref[...] = jnp.zeros((L, dh), F32)
        dkc_ref[...] = jnp.zeros((C, dh), F32)
        dvc_ref[...] = jnp.zeros((C, dh), F32)
        db_ref[...] = jnp.zeros((NA_KH, GRID_W, nwin), F32)

        def row(r, carry):
            rs = jnp.clip(r - NA_KH // 2, 0, R - NA_KH)
            q0 = pl.multiple_of(r * GRID_W, GRID_W)
            k0 = pl.multiple_of(rs * GRID_W, GRID_W)
            q = q_ref[pl.ds(q0, GRID_W), :]
            kw = k_ref[pl.ds(k0, nwin), :]
            vw = v_ref[pl.ds(k0, nwin), :]
            do = do_ref[pl.ds(q0, GRID_W), :]
            p_w, p_c = _attn_probs(q, kw, kc, b_ref[r - rs], scale)
            dp_w = _dot_nt(do, vw)
            dp_c = _dot_nt(do, vc)
            delta = jnp.sum(p_w * dp_w, axis=-1, keepdims=True) + jnp.sum(p_c * dp_c, axis=-1, keepdims=True)
            ds_w = p_w * (dp_w - delta)
            ds_c = p_c * (dp_c - delta)
            db_ref[r - rs] += ds_w
            dsw = (ds_w * scale).astype(BF16)
            dsc = (ds_c * scale).astype(BF16)
            dq_ref[pl.ds(q0, GRID_W), :] = _dot(dsw, kw) + _dot(dsc, kc)
            dk_ref[pl.ds(k0, nwin), :] += _dot_tn(dsw, q)
            dv_ref[pl.ds(k0, nwin), :] += _dot_tn(p_w.astype(BF16), do)
            dkc_ref[...] += _dot_tn(dsc, q)
            dvc_ref[...] += _dot_tn(p_c.astype(BF16), do)
            return carry

        lax.fori_loop(0, R, row, 0, unroll=2)

    head = lambda n: _resident((None, n, dh), lambda h: (h, 0, 0))
    bspec = pl.BlockSpec((None, NA_KH, GRID_W, nwin), lambda h: (h, 0, 0, 0))
    return pl.pallas_call(
        body, name=name, grid=(H,),
        in_specs=[head(L), head(L), head(L), head(C), head(C), bspec, head(L)],
        out_specs=[head(L), head(L), head(L), head(C), head(C), bspec],
        out_shape=[jax.ShapeDtypeStruct((H, L, dh), F32)] * 3 + [jax.ShapeDtypeStruct((H, C, dh), F32)] * 2 +
                  [jax.ShapeDtypeStruct((H, NA_KH, GRID_W, nwin), F32)],
        compiler_params=_params())(qh, kh, vh, kch, vch, bias, doh)


def _rpb_grad(dbias, name):
    H = dbias.shape[0]
    nri, nci = 2 * NA_KH - 1, 2 * NA_KW - 1
    d5 = dbias.reshape(H, NA_KH, GRID_W, NA_KH, GRID_W).transpose(0, 1, 3, 2, 4)
    col = jnp.arange(GRID_W)
    onehot = (col[None, None, :] - col[None, :, None] + (NA_KW - 1) == jnp.arange(32)[:, None, None]).astype(F32)

    def body(d_ref, m_ref, o_ref, t_ref):
        t_ref[...] = jnp.zeros((32, GRID_W), F32)
        o_ref[...] = jnp.zeros((16, 32, 128), F32)
        for ri in range(nri):
            a = None
            for var in range(NA_KH):
                kr = ri + var - (NA_KH - 1)
                if 0 <= kr < NA_KH:
                    blk = d_ref[var, kr]
                    a = blk if a is None else a + blk
            for ci in range(nci):
                t_ref[ci:ci + 1, :] = _sum0(a * m_ref[ci])
            o_ref[ri] = jnp.broadcast_to(jnp.sum(t_ref[...], axis=1, keepdims=True), (32, 128))

    out = pl.pallas_call(
        body, name=name, grid=(H,),
        in_specs=[pl.BlockSpec((None, NA_KH, NA_KH, GRID_W, GRID_W), lambda h: (h, 0, 0, 0, 0)),
                  pl.BlockSpec((32, GRID_W, GRID_W), lambda h: (0, 0, 0))],
        out_specs=pl.BlockSpec((None, 16, 32, 128), lambda h: (h, 0, 0, 0)),
        out_shape=jax.ShapeDtypeStruct((H, 16, 32, 128), F32),
        scratch_shapes=[pltpu.VMEM((32, GRID_W), F32)])(d5, onehot)
    return out[:, :nri, :nci, 0]


def _window_count(t, w, L):
    lo = jnp.clip(t - w // 2, 0, L)
    hi = jnp.clip(t - w // 2 + w, 0, L)
    return jnp.maximum(hi - lo, 1).astype(F32)


def _running_sum(v, w):
    k = 1
    while k < w:
        v = v + _shift_rows(v, k)
        k *= 2
    return v


def _pool_fwd(u, poolw, pscale, name, tm=512):
    L, W = u.shape
    G = POOL_GROUP_DIM
    tm = min(tm, L)
    nt = L // tm

    def body(c_ref, p_ref, n_ref, w_ref, s_ref, o_ref, dm_ref):
        i = pl.program_id(0)
        ext = _ext(p_ref[...], c_ref[...], n_ref[...], i, nt)
        t = i * tm + lax.broadcasted_iota(jnp.int32, (tm, 1), 0)
        for g, w in enumerate(POOL_WINDOWS):
            e = ext[:, g * G:(g + 1) * G]
            win = _shift_rows(_running_sum(e, w), -(w // 2 - 1))[HALO:HALO + tm]
            dmx = (win / _window_count(t, w, L) - e[HALO:HALO + tm]).astype(BF16)
            dm_ref[:, g * G:(g + 1) * G] = dmx
            o_ref[:, g * G:(g + 1) * G] = (_dot(dmx, w_ref[g]) * s_ref[:, g * G:(g + 1) * G]).astype(BF16)

    return pl.pallas_call(
        body, name=name, grid=(nt,),
        in_specs=[_tile(tm, W), _halo_prev(tm, W), _halo_next(tm, W, L), _full((4, G, G)), _full((1, W))],
        out_specs=[_tile(tm, W), _tile(tm, W)],
        out_shape=[jax.ShapeDtypeStruct((L, W), BF16)] * 2, compiler_params=_params())(u, u, u, poolw, pscale)


def _pool_bwd(dpool, dmx, poolw, pscale, name, tm=512):
    L, W = dpool.shape
    G = POOL_GROUP_DIM
    tm = min(tm, L)
    nt = L // tm

    def body(c_ref, p_ref, n_ref, dm_ref, w_ref, s_ref, du_ref, dw_ref, acc_ref):
        i = pl.program_id(0)
        ext = _ext(p_ref[...], c_ref[...], n_ref[...], i, nt)
        te = i * tm - HALO + lax.broadcasted_iota(jnp.int32, (tm + 2 * HALO, 1), 0)

        @pl.when(i == 0)
        def _():
            dw_ref[...] = jnp.zeros((4 * G, G), F32)

        rows = []
        for g, w in enumerate(POOL_WINDOWS):
            sc = s_ref[:, g * G:(g + 1) * G]
            dpre = (ext[:, g * G:(g + 1) * G] * sc).astype(BF16)
            dd = _dot_nt(dpre, w_ref[g])
            spread = _shift_rows(_running_sum(dd / _window_count(te, w, L), w), -(w // 2))
            du_ref[:, g * G:(g + 1) * G] = (spread - dd)[HALO:HALO + tm]
            dmx_g = dm_ref[:, g * G:(g + 1) * G]
            rows.append(_sum0(c_ref[:, g * G:(g + 1) * G] * _dot(dmx_g, w_ref[g])))
            dw_ref[g * G:(g + 1) * G, :] += _dot_tn(dmx_g, dpre[HALO:HALO + tm])
        _acc_rows(acc_ref, i == 0, [jnp.concatenate(rows, axis=1)])

    return pl.pallas_call(
        body, name=name, grid=(nt,),
        in_specs=[_tile(tm, W), _halo_prev(tm, W), _halo_next(tm, W, L), _tile(tm, W), _full((4, G, G)),
                  _full((1, W))],
        out_specs=[_tile(tm, W), _full((4 * G, G)), _full((8, W))],
        out_shape=[jax.ShapeDtypeStruct((L, W), F32), jax.ShapeDtypeStruct((4 * G, G), F32),
                   jax.ShapeDtypeStruct((8, W), F32)],
        compiler_params=_params())(dpool, dpool, dpool, dmx, poolw, pscale)


def _conv3(z, cw):
    return _shift_rows(z, 1) * cw[0] + z * cw[1] + _shift_rows(z, -1) * cw[2]


def _conv_fwd(x, prm, wing, woutg, name, tm=512):
    L, D = x.shape
    Ws = wing.shape[-1]
    tm = min(tm, L)
    nt = L // tm
    te = tm + 2 * HALO

    def body(c_ref, p_ref, n_ref, prm_ref, wi_ref, wo_ref, xo_ref, y_ref, b_ref):
        i = pl.program_id(0)
        xe = jnp.concatenate([p_ref[...], c_ref[...], n_ref[...]], axis=0)
        hn, _, _, _ = _nm(xe, prm_ref[3:4, :], prm_ref[0:1, :], prm_ref[1:2, :])
        hb = hn.astype(BF16)
        proj = jnp.concatenate([_dot(hb, wi_ref[s]) for s in range(4)], axis=1)
        bg, cg, xin = proj[:, :D], proj[:, D:2 * D], proj[:, 2 * D:]
        tpos = i * tm - HALO + lax.broadcasted_iota(jnp.int32, (te, 1), 0)
        valid = ((tpos >= 0) & (tpos < L)).astype(F32)
        yc = _conv3(cg * xin * valid, [prm_ref[4 + k:5 + k, :] for k in range(3)])
        h2 = (bg * yc)[HALO:HALO + tm].astype(BF16)
        yv = _dot(h2, wo_ref[...])
        y_ref[...] = yv.astype(BF16)
        xo_ref[...] = c_ref[...] + prm_ref[2:3, :] * yv
        b_ref[...] = proj[HALO:HALO + tm].astype(BF16)

    return pl.pallas_call(
        body, name=name, grid=(nt,),
        in_specs=[_tile(tm, D), _halo_prev(tm, D), _halo_next(tm, D, L), _full((8, D)),
                  _resident((None, 4, D, Ws), lambda i: (0, 0, 0, 0)),
                  _resident((None, D, D), lambda i: (0, 0, 0))],
        out_specs=[_tile(tm, D), _tile(tm, D), _tile(tm, 3 * D)],
        out_shape=[jax.ShapeDtypeStruct((L, D), F32), jax.ShapeDtypeStruct((L, D), BF16),
                   jax.ShapeDtypeStruct((L, 3 * D), BF16)],
        compiler_params=_params())(x, x, x, prm, wing, woutg)


def _conv_bwd(dout, x, y, bcx, prm, wing, woutg, name, tm=256):
    L, D = x.shape
    Ws = wing.shape[-1]
    tm = min(tm, L)
    nt = L // tm
    te = tm + 2 * HALO

    def body(dc_ref, dp_ref, dn_ref, x_ref, y_ref, bc_ref, bp_ref, bn_ref, prm_ref, wi_ref, wo_ref,
             dx_ref, dpr_ref, h2_ref, dy_ref, acc_ref):
        i = pl.program_id(0)
        gain, shift, scale, gate = prm_ref[3:4, :], prm_ref[0:1, :], prm_ref[1:2, :], prm_ref[2:3, :]
        taps = [prm_ref[4 + k:5 + k, :] for k in range(3)]
        do = dc_ref[...]
        doe = _ext(dp_ref[...], do, dn_ref[...], i, nt)
        dye = (gate * doe).astype(BF16)
        dy_ref[...] = dye[HALO:HALO + tm]
        dh2 = _dot_nt(dye, wo_ref[...])
        be = jnp.concatenate([bp_ref[...], bc_ref[...], bn_ref[...]], axis=0).astype(F32)
        bg, cg, xin = be[:, :D], be[:, D:2 * D], be[:, 2 * D:]
        tpos = i * tm - HALO + lax.broadcasted_iota(jnp.int32, (te, 1), 0)
        valid = ((tpos >= 0) & (tpos < L)).astype(F32)
        z = cg * xin * valid
        yc = _conv3(z, taps)
        dyc = dh2 * bg
        h2_ref[...] = (bg * yc)[HALO:HALO + tm].astype(BF16)
        dz = _conv3(dyc, taps[::-1]) * valid
        dproj = jnp.concatenate([dh2 * yc, dz * xin, dz * cg], axis=1)[HALO:HALO + tm].astype(BF16)
        dpr_ref[...] = dproj
        dhn = jnp.zeros((tm, D), F32)
        for s in range(4):
            dhn = dhn + _dot_nt(dproj[:, s * Ws:(s + 1) * Ws], wi_ref[s])
        _, xhat, r, nrm = _nm(x_ref[...], gain, shift, scale)
        dx, dshift, dscale, dgn = _nm_bwd(dhn, xhat, r, nrm, gain, scale)
        dx_ref[...] = do + dx
        dgate = _sum0(do * y_ref[...].astype(F32))
        dtaps = [_sum0((dyc * _shift_rows(z, 1 - k))[HALO:HALO + tm]) for k in range(3)]
        _acc_rows(acc_ref, i == 0, [dshift, dscale, dgate, dgn] + dtaps)

    return pl.pallas_call(
        body, name=name, grid=(nt,),
        in_specs=[_tile(tm, D), _halo_prev(tm, D), _halo_next(tm, D, L), _tile(tm, D), _tile(tm, D),
                  _tile(tm, 3 * D), _halo_prev(tm, 3 * D), _halo_next(tm, 3 * D, L), _full((8, D)),
                  _resident((None, 4, D, Ws), lambda i: (0, 0, 0, 0)),
                  _resident((None, D, D), lambda i: (0, 0, 0))],
        out_specs=[_tile(tm, D), _tile(tm, 3 * D), _tile(tm, D), _tile(tm, D), _full((8, D))],
        out_shape=[jax.ShapeDtypeStruct((L, D), F32), jax.ShapeDtypeStruct((L, 3 * D), BF16),
                   jax.ShapeDtypeStruct((L, D), BF16), jax.ShapeDtypeStruct((L, D), BF16),
                   jax.ShapeDtypeStruct((8, D), F32)],
        compiler_params=_params())(dout, dout, dout, x, y, bcx, bcx, bcx, prm, wing, woutg)


def _loss_head(x, tgt, fg, name, tm=512):
    L, D = x.shape
    tm = min(tm, L)

    def body(x_ref, t_ref, g_ref, dx_ref, acc_ref):
        i = pl.program_id(0)
        xv = x_ref[...]
        g = g_ref[...]
        r = lax.rsqrt(jnp.mean(xv * xv, axis=-1, keepdims=True) + RMS_EPS)
        xhat = xv * r
        err = xhat * g - t_ref[...]
        part = 0.5 * jnp.sum(jnp.mean(err * err, axis=-1, keepdims=True), axis=0, keepdims=True)
        dy = err * (1.0 / D)
        dxh = dy * g
        dx_ref[...] = r * (dxh - xhat * jnp.mean(dxh * xhat, axis=-1, keepdims=True))
        _acc_rows(acc_ref, i == 0, [_sum0(dy * xhat), jnp.broadcast_to(part, (1, D))])

    return pl.pallas_call(
        body, name=name, grid=(L // tm,), in_specs=[_tile(tm, D), _tile(tm, D), _full((1, D))],
        out_specs=[_tile(tm, D), _full((8, D))],
        out_shape=[jax.ShapeDtypeStruct((L, D), F32), jax.ShapeDtypeStruct((8, D), F32)],
        compiler_params=_params())(x, tgt, fg)


def _mod_fwd(cond, mod_w, mod_b, name, tn=768):
    nl, D, N = mod_w.shape
    tn = min(tn, N)

    def body(c_ref, w_ref, b_ref, o_ref):
        cv = c_ref[...]
        s = (cv * _sigmoid(cv)).astype(BF16)
        o_ref[...] = _dot(s, w_ref[...].astype(BF16)) + b_ref[...]

    return pl.pallas_call(
        body, name=name, grid=(nl, N // tn),
        in_specs=[pl.BlockSpec((16, D), lambda l, j: (0, 0)), pl.BlockSpec((None, D, tn), lambda l, j: (l, 0, j)),
                  pl.BlockSpec((None, 1, tn), lambda l, j: (l, 0, j))],
        out_specs=pl.BlockSpec((None, 16, tn), lambda l, j: (l, 0, j)),
        out_shape=jax.ShapeDtypeStruct((nl, 16, N), F32), compiler_params=_params())(cond, mod_w, mod_b)


def _mod_bwd(cond, dm, mod_w, name, tn=768):
    nl, D, N = mod_w.shape
    tn = min(tn, N)

    def body(c_ref, d_ref, w_ref, dw_ref, dc_ref):
        first = (pl.program_id(0) == 0) & (pl.program_id(1) == 0)
        cv = c_ref[...]
        s = (cv * _sigmoid(cv)).astype(BF16)
        d = d_ref[...].astype(BF16)
        dw_ref[...] = _dot_tn(s, d)

        @pl.when(first)
        def _():
            dc_ref[...] = jnp.zeros((16, D), F32)

        dc_ref[...] += _dot_nt(d, w_ref[...].astype(BF16))

    return pl.pallas_call(
        body, name=name, grid=(nl, N // tn),
        in_specs=[pl.BlockSpec((16, D), lambda l, j: (0, 0)), pl.BlockSpec((None, 16, tn), lambda l, j: (l, 0, j)),
                  pl.BlockSpec((None, D, tn), lambda l, j: (l, 0, j))],
        out_specs=[pl.BlockSpec((None, D, tn), lambda l, j: (l, 0, j)), pl.BlockSpec((16, D), lambda l, j: (0, 0))],
        out_shape=[jax.ShapeDtypeStruct((nl, D, N), F32), jax.ShapeDtypeStruct((16, D), F32)],
        compiler_params=_params())(cond, dm, mod_w)


def _mod_small_grads(dm_all, cond, dsilu_parts, name):
    nl, _, N = dm_all.shape
    D = cond.shape[1]

    def body(d_ref, c_ref, p_ref, db_ref, dc_ref):
        for l in range(nl):
            db_ref[l] = _sum0(d_ref[l])
        tot = p_ref[0, 8:9, :]
        for k in range(1, N_CHIPS):
            tot = tot + p_ref[2 * k, 8:9, :]
        cv = c_ref[8:9, :]
        sg = _sigmoid(cv)
        dc_ref[...] = tot * (sg * (1.0 + cv * (1.0 - sg)))

    return pl.pallas_call(
        body, name=name, out_shape=[jax.ShapeDtypeStruct((nl, 1, N), F32), jax.ShapeDtypeStruct((1, D), F32)],
    )(dm_all, cond, dsilu_parts)


def _prm(rows, D):
    rows = [r.reshape(1, D) for r in rows]
    return jnp.concatenate(rows + [jnp.zeros((8 - len(rows), D), F32)], axis=0)


def _heads(a):
    L = a.shape[0]
    return a.reshape(L, NA_HEADS, NA_HEAD_DIM).transpose(1, 0, 2)


def _unheads(a):
    return a.transpose(1, 0, 2).reshape(a.shape[1], NA_HEADS * NA_HEAD_DIM)


def kernel(x, c, ctx, c_ctx, mod_w, mod_b, norm_g, ffn_w13, ffn_w2, even_w_in, even_w_out, na_rpb, pool_w, pool_scale, conv_w_in, conv_w, conv_w_out, final_g, loss_target, m_c_ctx, m_mod_w, m_mod_b, m_norm_g, m_ffn_w13, m_ffn_w2, m_even_w_in, m_even_w_out, m_na_rpb, m_pool_w, m_pool_scale, m_conv_w_in, m_conv_w, m_conv_w_out, m_final_g, v_c_ctx, v_mod_w, v_mod_b, v_norm_g, v_ffn_w13, v_ffn_w2, v_even_w_in, v_even_w_out, v_na_rpb, v_pool_w, v_pool_scale, v_conv_w_in, v_conv_w, v_conv_w_out, v_final_g):
    xi, yi, ci = lax.axis_index("x"), lax.axis_index("y"), lax.axis_index("c")
    chip = 2 * xi + yi
    dev = 4 * xi + 2 * yi + ci
    _, L, D = x.shape
    C = ctx.shape[1]
    Ds = D // N_CHIPS
    Nm = mod_w.shape[-1]
    Fh = ffn_w13.shape[-1]
    Fq = ffn_w2.shape[2]
    assert ffn_w13.shape[:2] == (2, 2) and Fh == 2 * Fq and L % GRID_W == 0 and L // GRID_W >= NA_KH
    x0, ctx0, tgt = x[0], ctx[0], loss_target[0]

    pad = lambda a: jnp.pad(a, ((0, 0), (0, D - a.shape[1])))
    pack1 = jnp.concatenate([c, pad(norm_g.reshape(6, Ds)), pad(conv_w.reshape(3, Ds)), jnp.zeros((6, D), F32)], axis=0)
    g1 = _small_all_gather(pack1, "ag_cond")
    cond = jnp.concatenate([g1[:, 0], c_ctx[None], jnp.zeros((7, D), F32)], axis=0)
    norm_full = jnp.concatenate([g1[2 * k, 1:7, :Ds] for k in range(N_CHIPS)], axis=1).reshape(2, 3, D)
    convw_full = jnp.concatenate([g1[2 * k, 7:10, :Ds] for k in range(N_CHIPS)], axis=1)

    mod_b_loc = lax.dynamic_slice_in_dim(mod_b, chip * Nm, Nm, axis=1).reshape(2, 1, Nm)
    m_loc = _mod_fwd(cond, mod_w, mod_b_loc, "mod_fwd")
    g2 = _small_all_gather(m_loc.reshape(32, Nm), "ag_mod")
    m_all = jnp.concatenate([g2[2 * k] for k in range(N_CHIPS)], axis=1).reshape(2, 16, N_MOD, D)
    m_lat = lax.dynamic_index_in_dim(m_all, dev, axis=1, keepdims=False)
    m_ctx = m_all[:, 8]

    def prm(mods, layer, base, gain_idx, extra=()):
        return _prm([mods[layer, base], mods[layer, base + 1], mods[layer, base + 2], norm_full[layer, gain_idx],
                     *extra], D)

    def shard_bf16(w, name):
        return _cast_bf16(w.reshape(-1, w.shape[-1]), name).reshape(-1, *w.shape[-2:])

    w13g, w2g, eing, eoutg, cing, coutg = _gather_shards(
        [shard_bf16(ffn_w13, "cast_w13"), shard_bf16(ffn_w2, "cast_w2"), shard_bf16(even_w_in, "cast_ein"),
         shard_bf16(even_w_out, "cast_eout"), shard_bf16(conv_w_in, "cast_cin"), shard_bf16(conv_w_out, "cast_cout")],
        "ag_weights")
    w13g = w13g.reshape(4, 4, D, Fh)
    eing = eing.reshape(1, 4, D, NA_WIDTH)
    cing = cing.reshape(1, 4, D, conv_w_in.shape[-1])

    p_f1 = prm(m_lat, 0, 0, 0)
    p_mx = prm(m_lat, 0, 3, 1)
    p_f2 = prm(m_lat, 0, 6, 2)
    p_g1 = prm(m_lat, 1, 0, 0)
    p_cv = prm(m_lat, 1, 3, 1, extra=(convw_full[0], convw_full[1], convw_full[2]))
    p_g2 = prm(m_lat, 1, 6, 2)
    pc_f1 = prm(m_ctx, 0, 0, 0)
    pc_mx = prm(m_ctx, 0, 3, 1)

    x1, ab1, y1 = _ffn_fwd(x0, p_f1, w13g, w2g, 0, "ffn_fwd_l0a")
    ctx1, abc, yc = _ffn_fwd(ctx0, pc_f1, w13g, w2g, 0, "ffn_fwd_ctx")
    q, k, v, u, hn_mx = _even_in_fwd(x1, p_mx, eing, "even_in_fwd")
    _, k_c, v_c, _, hn_cx = _even_in_fwd(ctx1, pc_mx, eing, "even_in_ctx")
    bias = _bias_table(na_rpb[0], "bias_table")
    qh, kh, vh, kch, vch = _heads(q), _heads(k), _heads(v), _heads(k_c), _heads(v_c)
    att = _unheads(_attn_fwd(qh, kh, vh, kch, vch, bias, "attn_fwd"))
    pw_b = _cast_bf16(pool_w.reshape(-1, POOL_GROUP_DIM), "cast_poolw").reshape(4, POOL_GROUP_DIM, POOL_GROUP_DIM)
    pool, dmx = _pool_fwd(u, pw_b, pool_scale, "pool_fwd")
    mix = jnp.concatenate([att, pool], axis=1)
    x2, ymx = _even_out_fwd(x1, mix, p_mx, eoutg, "even_out_fwd")
    x3, ab2, y2 = _ffn_fwd(x2, p_f2, w13g, w2g, 1, "ffn_fwd_l0b")
    x4, ab3, y3 = _ffn_fwd(x3, p_g1, w13g, w2g, 2, "ffn_fwd_l1a")
    x5, ycv, bcx = _conv_fwd(x4, p_cv, cing, coutg, "conv_fwd")
    x6, ab4, y4 = _ffn_fwd(x5, p_g2, w13g, w2g, 3, "ffn_fwd_l1b")
    dx6, acc_head = _loss_head(x6, tgt, final_g.reshape(1, D), "loss_head")
    loss = lax.psum(acc_head[1, 0], ("x", "y", "c"))

    def ffn_back(dout, xin, ab, yy, p, t, tag, init13=None, init2=None):
        dx, dab, gact, dy, hn, acc = _ffn_bwd(dout, xin, ab, yy, p, w13g, w2g, t, f"ffn_bwd_{tag}")
        dw13 = _mm_tn(hn, dab, 4, False, f"dw13_{tag}", init=init13)
        dw2 = _mm_tn(gact, dy, 2, True, f"dw2_{tag}", init=init2)
        return dx, acc, dw13, dw2

    dx5, acc_g2, dw13_3, dw2_3 = ffn_back(dx6, x5, ab4, y4, p_g2, 3, "l1b")
    dx4, dproj, h2, dycv, acc_cv = _conv_bwd(dx5, x4, ycv, bcx, p_cv, cing, coutg, "conv_bwd")
    hn_cv = _even_hn(x4, p_cv, "conv_hn")
    dcin = _mm_tn(hn_cv, dproj, 4, False, "dw_cin")
    dcout = _mm_tn(h2, dycv, 1, False, "dw_cout")
    dx3, acc_g1, dw13_2, dw2_2 = ffn_back(dx4, x3, ab3, y3, p_g1, 2, "l1a")
    dx2, acc_f2, dw13_1, dw2_1 = ffn_back(dx3, x2, ab2, y2, p_f2, 1, "l0b")

    dymx, datt, dpool, acc_mxo = _even_out_bwd(dx2, ymx, p_mx, eoutg, "even_out_bwd")
    deout = _mm_tn(mix, dymx, 1, False, "dw_eout")
    du, dpoolw, acc_pool = _pool_bwd(dpool, dmx, pw_b, pool_scale, "pool_bwd")
    doh = _heads(datt)
    dqh, dkh, dvh, dkch, dvch, dbias = _attn_bwd(qh, kh, vh, kch, vch, bias, doh, "attn_bwd")
    drpb = _rpb_grad(dbias, "rpb_grad")
    dx1, dstack, acc_mxi = _even_in_bwd(dx2, x1, _unheads(dqh), _unheads(dkh), _unheads(dvh), du, p_mx, eing,
                                        "even_in_bwd")
    zc = jnp.zeros((C, NA_WIDTH), F32)
    dctx1, dstack_c, accc_mx = _even_in_bwd(jnp.zeros((C, D), F32), ctx1, zc, _unheads(dkch), _unheads(dvch), zc,
                                            pc_mx, eing, "even_in_bwd_ctx")
    dein_c = _mm_tn(hn_cx, dstack_c, 4, False, "dw_ein_ctx")
    dein = _mm_tn(hn_mx, dstack, 4, False, "dw_ein", init=dein_c)
    _, accc_f1, dw13_c, dw2_c = ffn_back(dctx1, ctx0, abc, yc, pc_f1, 0, "ctx")
    dx0, acc_f1, dw13_0, dw2_0 = ffn_back(dx1, x0, ab1, y1, p_f1, 0, "l0a", init13=dw13_c, init2=dw2_c)

    z1 = jnp.zeros((1, D), F32)
    dm_lat = jnp.concatenate([acc_f1[0:3], acc_mxi[0:2], acc_mxo[2:3], acc_f2[0:3],
                              acc_g1[0:3], acc_cv[0:3], acc_g2[0:3]], axis=0)
    dm_ctx = jnp.concatenate([accc_f1[0:3], accc_mx[0:2]] + [z1] * 13, axis=0)
    dnorm = jnp.concatenate([acc_f1[3:4] + accc_f1[3:4], acc_mxi[3:4] + accc_mx[3:4], acc_f2[3:4],
                             acc_g1[3:4], acc_cv[3:4], acc_g2[3:4]], axis=0)
    rpb_flat = jnp.pad(drpb.reshape(-1), (0, 4 * D - drpb.size)).reshape(4, D)
    pack3 = jnp.concatenate([dm_lat, dm_ctx, dnorm, acc_cv[4:7], acc_head[0:1], pad(acc_pool[0:1]), z1,
                             dpoolw.reshape(-1, D), rpb_flat, jnp.zeros((4, D), F32)], axis=0)
    g3 = _small_all_gather(pack3, "ag_small")
    tot = _sum_devices(g3, "sum_small")
    dm_all = jnp.concatenate([g3[:, 0:18].reshape(8, 2, N_MOD * D).transpose(1, 0, 2),
                              tot[18:36].reshape(2, 1, N_MOD * D), jnp.zeros((2, 7, N_MOD * D), F32)], axis=1)
    dm_loc = lax.dynamic_slice_in_dim(dm_all, chip * Nm, Nm, axis=2)
    g_mod_w, dsilu = _mod_bwd(cond, dm_loc, mod_w, "mod_bwd")
    g4 = _small_all_gather(dsilu, "ag_dsilu")
    g_mod_b, g_c_ctx = _mod_small_grads(dm_all, cond, g4, "mod_small")
    g_mod_b = g_mod_b.reshape(2, N_MOD * D)
    g_c_ctx = g_c_ctx.reshape(D)
    g_norm_full = tot[36:42].reshape(2, 3, D)
    g_norm = lax.dynamic_slice_in_dim(g_norm_full, chip * Ds, Ds, axis=2)
    g_conv_w = lax.dynamic_slice_in_dim(tot[42:45], chip * Ds, Ds, axis=1).reshape(1, 3, Ds)
    g_final = tot[45]
    g_pscale = tot[46:47, :pool_scale.shape[1]]
    g_poolw = tot[48:112].reshape(pool_w.shape)
    g_rpb = tot[112:116].reshape(-1)[:na_rpb.size].reshape(na_rpb.shape)

    pos = jnp.stack([chip, ci]).astype(jnp.int32)
    red = _reduce_scatter([dw13_0, dw13_1, dw13_2, dw13_3, dw2_0, dw2_1, dw2_2, dw2_3, dein, deout, dcin, dcout],
                          pos, "w")
    g_w13 = jnp.stack(red[0:4]).reshape(ffn_w13.shape)
    g_w2 = jnp.stack(red[4:8]).reshape(ffn_w2.shape)
    g_ein, g_eout, g_cin, g_cout = (red[8][None], red[9][None], red[10][None], red[11][None])

    grads = [g_c_ctx, g_mod_w, g_mod_b, g_norm, g_w13, g_w2, g_ein, g_eout, g_rpb, g_poolw, g_pscale, g_cin,
             g_conv_w, g_cout, g_final]
    weights = [c_ctx, mod_w, mod_b, norm_g, ffn_w13, ffn_w2, even_w_in, even_w_out, na_rpb, pool_w, pool_scale,
               conv_w_in, conv_w, conv_w_out, final_g]
    ms = [m_c_ctx, m_mod_w, m_mod_b, m_norm_g, m_ffn_w13, m_ffn_w2, m_even_w_in, m_even_w_out, m_na_rpb, m_pool_w,
          m_pool_scale, m_conv_w_in, m_conv_w, m_conv_w_out, m_final_g]
    vs = [v_c_ctx, v_mod_w, v_mod_b, v_norm_g, v_ffn_w13, v_ffn_w2, v_even_w_in, v_even_w_out, v_na_rpb, v_pool_w,
          v_pool_scale, v_conv_w_in, v_conv_w, v_conv_w_out, v_final_g]
    names = ["c_ctx", "mod_w", "mod_b", "norm_g", "ffn_w13", "ffn_w2", "even_w_in", "even_w_out", "na_rpb", "pool_w",
             "pool_scale", "conv_w_in", "conv_w", "conv_w_out", "final_g"]
    deltas, new_m, new_v = [], [], []
    for n, w, g, m, vv in zip(names, weights, grads, ms, vs):
        g = g.reshape(w.shape)
        if w.ndim == 1:
            d, mn, vn = (t.reshape(w.shape) for t in _adamw(w[None], g[None], m[None], vv[None], f"adamw_{n}"))
        else:
            d, mn, vn = _adamw(w, g, m, vv, f"adamw_{n}")
        deltas.append(d)
        new_m.append(mn)
        new_v.append(vn)
    grads = [g.reshape(w.shape) for g, w in zip(grads, weights)]
    return (loss, dx0[None], *grads, *deltas, *new_m, *new_v)


def _even_hn(x, prm, name, tm=512):
    L, D = x.shape
    tm = min(tm, L)

    def body(x_ref, p_ref, o_ref):
        hn, _, _, _ = _nm(x_ref[...], p_ref[3:4, :], p_ref[0:1, :], p_ref[1:2, :])
        o_ref[...] = hn.astype(BF16)

    return pl.pallas_call(
        body, name=name, grid=(L // tm,), in_specs=[_tile(tm, D), _full((8, D))], out_specs=_tile(tm, D),
        out_shape=jax.ShapeDtypeStruct((L, D), BF16))(x, prm)
```

```python
import jax
import jax.numpy as jnp
from jax import lax
from jax.experimental import pallas as pl
from jax.experimental.pallas import tpu as pltpu

F32 = jnp.float32
BF16 = jnp.bfloat16
MESH = pl.DeviceIdType.MESH

GRID_W = 64
NA_HEADS = 8
NA_HEAD_DIM = 64
NA_KH = 8
NA_KW = 16
GQ = 4
GK = GQ + NA_KH
NA_WIDTH = NA_HEADS * NA_HEAD_DIM
POOL_WINDOWS = (2, 4, 8, 16)
POOL_GROUP_DIM = 128
N_MOD = 9
RMS_EPS = 1e-6
NEG_INF = -1e30
ADAM_LR, ADAM_B1, ADAM_B2, ADAM_EPS, ADAM_WD, ADAM_STEP = 0.001, 0.9, 0.999, 1e-08, 0.01, 10

HALO = 16
VMEM_LIMIT = 56 * 1024 * 1024
N_CHIPS = 4
N_DEV = 8


def _dot(a, b):
    return jnp.dot(a, b, preferred_element_type=F32)


def _dot_nt(a, b):
    return lax.dot_general(a, b, (((1,), (1,)), ((), ())), preferred_element_type=F32)


def _dot_tn(a, b):
    return lax.dot_general(a, b, (((0,), (0,)), ((), ())), preferred_element_type=F32)


def _sigmoid(a):
    return 1.0 / (1.0 + jnp.exp(-a))


def _sum0(v):
    return jnp.sum(v, axis=0, keepdims=True)


def _nm(x, g, shift, scale):
    r = lax.rsqrt(jnp.mean(x * x, axis=-1, keepdims=True) + RMS_EPS)
    xhat = x * r
    nrm = xhat * g
    return nrm * (1.0 + scale) + shift, xhat, r, nrm


def _nm_bwd(dhn, xhat, r, nrm, g, scale):
    dshift = _sum0(dhn)
    dscale = _sum0(dhn * nrm)
    dnrm = dhn * (1.0 + scale)
    dgn = _sum0(dnrm * xhat)
    dxh = dnrm * g
    dx = r * (dxh - xhat * jnp.mean(dxh * xhat, axis=-1, keepdims=True))
    return dx, dshift, dscale, dgn


def _acc_rows(acc_ref, first, rows):
    @pl.when(first)
    def _():
        acc_ref[...] = jnp.zeros(acc_ref.shape, acc_ref.dtype)
    for k, row in enumerate(rows):
        if row is not None:
            acc_ref[k:k + 1, :] += row


def _shift_rows(v, k):
    n = v.shape[0]
    k = k % n
    return v if k == 0 else pltpu.roll(v, k, 0)


def _tile(tm, w):
    return pl.BlockSpec((tm, w), lambda i: (i, 0))


def _full(shape):
    nd = len(shape)
    return pl.BlockSpec(shape, lambda i: (0,) * nd)


def _resident(block, imap):
    return pl.BlockSpec(block, imap, pipeline_mode=pl.Buffered(1))


def _halo_prev(tm, w):
    return pl.BlockSpec((HALO, w), lambda i: (jnp.maximum(i * (tm // HALO) - 1, 0), 0))


def _halo_next(tm, w, L):
    return pl.BlockSpec((HALO, w), lambda i: (jnp.minimum((i + 1) * (tm // HALO), L // HALO - 1), 0))


def _params(vmem=VMEM_LIMIT):
    return pltpu.CompilerParams(vmem_limit_bytes=vmem)


def _pick_rows(rows, cols, itemsize=4, target=1 << 20):
    best = None
    for t in range(8, rows + 1, 8):
        if rows % t == 0 and t * cols * itemsize <= target:
            best = t
    return best if best is not None else rows


def _ext(prev, cur, nxt, i, nt):
    prev = jnp.where(i > 0, prev, jnp.zeros_like(prev))
    nxt = jnp.where(i < nt - 1, nxt, jnp.zeros_like(nxt))
    return jnp.concatenate([prev, cur, nxt], axis=0)


def _cast_bf16(a2d, name):
    rows, cols = a2d.shape
    tr = _pick_rows(rows, cols)

    def body(a_ref, o_ref):
        o_ref[...] = a_ref[...].astype(BF16)

    return pl.pallas_call(
        body, name=name, grid=(rows // tr,), in_specs=[_tile(tr, cols)], out_specs=_tile(tr, cols),
        out_shape=jax.ShapeDtypeStruct((rows, cols), BF16))(a2d)


def _sum_devices(g, name):
    n, rows, cols = g.shape
    tr = _pick_rows(rows, cols, target=1 << 18)

    def body(g_ref, o_ref):
        s = g_ref[0]
        for d in range(1, n):
            s = s + g_ref[d]
        o_ref[...] = s

    return pl.pallas_call(
        body, name=name, grid=(rows // tr,), in_specs=[pl.BlockSpec((n, tr, cols), lambda i: (0, i, 0))],
        out_specs=_tile(tr, cols), out_shape=jax.ShapeDtypeStruct((rows, cols), F32))(g)


def _adamw(w, g, m, v, name):
    shape = w.shape
    cols = shape[-1]
    rows = w.size // cols
    w2, g2, m2, v2 = (t.reshape(rows, cols) for t in (w, g, m, v))
    tr = _pick_rows(rows, cols)
    c1 = 1.0 - ADAM_B1 ** ADAM_STEP
    c2 = 1.0 - ADAM_B2 ** ADAM_STEP

    def body(w_ref, g_ref, m_ref, v_ref, d_ref, mo_ref, vo_ref):
        gg = g_ref[...]
        mn = ADAM_B1 * m_ref[...] + (1.0 - ADAM_B1) * gg
        vn = ADAM_B2 * v_ref[...] + (1.0 - ADAM_B2) * (gg * gg)
        d_ref[...] = -ADAM_LR * ((mn / c1) / (jnp.sqrt(vn / c2) + ADAM_EPS) + ADAM_WD * w_ref[...])
        mo_ref[...] = mn
        vo_ref[...] = vn

    outs = pl.pallas_call(
        body, name=name, grid=(rows // tr,), in_specs=[_tile(tr, cols)] * 4, out_specs=[_tile(tr, cols)] * 3,
        out_shape=[jax.ShapeDtypeStruct((rows, cols), F32)] * 3)(w2, g2, m2, v2)
    return tuple(o.reshape(shape) for o in outs)


def _mesh_pos():
    x, y, c = lax.axis_index("x"), lax.axis_index("y"), lax.axis_index("c")
    chips = [(1 - x, y), (x, 1 - y), (1 - x, 1 - y)]
    return x, y, c, chips


def _hbm_specs(n):
    return [pl.BlockSpec(memory_space=pltpu.HBM)] * n


def _small_all_gather(v, name):
    rows, w = v.shape

    def body(x_ref, out_ref, send_sems, recv_sems, local_sem):
        x, y, c, chips = _mesh_pos()
        me, sibling = (x, y, c), (x, y, 1 - c)

        def blk(px, py, pc):
            return out_ref.at[4 * px + 2 * py + pc]

        def copy(k, block, to, src=None):
            return pltpu.make_async_remote_copy(
                src_ref=blk(*block) if src is None else src, dst_ref=blk(*block),
                send_sem=send_sems.at[k], recv_sem=recv_sems.at[k], device_id=to, device_id_type=MESH)

        mine = pltpu.make_async_copy(x_ref, blk(*me), local_sem)
        mine.start()
        first = [copy(0, me, sibling, src=x_ref)]
        first += [copy(1 + j, me, (*chip, c), src=x_ref) for j, chip in enumerate(chips)]
        for cp in first:
            cp.start()
        passed = [copy(4 + j, (*chip, c), sibling) for j, chip in enumerate(chips)]
        for j, chip in enumerate(chips):
            copy(1 + j, (*chip, c), me).wait_recv()
            passed[j].start()
        copy(0, sibling, me).wait_recv()
        for j, chip in enumerate(chips):
            copy(4 + j, (*chip, 1 - c), me).wait_recv()
        for cp in first + passed:
            cp.wait_send()
        mine.wait()

    return pl.pallas_call(
        body, name=name, out_shape=jax.ShapeDtypeStruct((N_DEV, rows, w), v.dtype),
        in_specs=[pl.BlockSpec(memory_space=pltpu.VMEM)], out_specs=pl.BlockSpec(memory_space=pltpu.VMEM),
        scratch_shapes=[pltpu.SemaphoreType.DMA((7,)), pltpu.SemaphoreType.DMA((7,)), pltpu.SemaphoreType.DMA],
    )(v)


def _gather_shards(shards, name):
    n = len(shards)

    def body(*refs):
        ins, outs = refs[:n], refs[n:2 * n]
        send_sems, recv_sems, local_sems = refs[2 * n:]
        x, y, c, chips = _mesh_pos()
        k = 2 * x + y
        sibling = (x, y, 1 - c)

        def window(t, chip_k, half):
            r = ins[t].shape[1]
            return outs[t].at[:, pl.ds(chip_k * r + half * (r // 2), r // 2), :]

        def copy(t, j, chip_k, half, to, src=None):
            return pltpu.make_async_remote_copy(
                src_ref=window(t, chip_k, half) if src is None else src, dst_ref=window(t, chip_k, half),
                send_sem=send_sems.at[6 * t + j], recv_sem=recv_sems.at[6 * t + j], device_id=to, device_id_type=MESH)

        started, local = [], []
        for t in range(n):
            r = ins[t].shape[1]
            lc = pltpu.make_async_copy(ins[t], outs[t].at[:, pl.ds(k * r, r), :], local_sems.at[t])
            lc.start()
            local.append(lc)
            src = ins[t].at[:, pl.ds(c * (r // 2), r // 2), :]
            for j, chip in enumerate(chips):
                cp = copy(t, j, k, c, (*chip, c), src=src)
                cp.start()
                started.append(cp)
        for t in range(n):
            for j, chip in enumerate(chips):
                kj = 2 * chip[0] + chip[1]
                copy(t, j, kj, c, sibling).wait_recv()
                cp = copy(t, 3 + j, kj, c, sibling)
                cp.start()
                started.append(cp)
        for t in range(n):
            for j, chip in enumerate(chips):
                kj = 2 * chip[0] + chip[1]
                copy(t, 3 + j, kj, 1 - c, sibling).wait_recv()
        for cp in started:
            cp.wait_send()
        for lc in local:
            lc.wait()

    out_shape = [jax.ShapeDtypeStruct((s.shape[0], N_CHIPS * s.shape[1], s.shape[2]), s.dtype) for s in shards]
    return pl.pallas_call(
        body, name=name, out_shape=out_shape, in_specs=_hbm_specs(n), out_specs=_hbm_specs(n),
        scratch_shapes=[pltpu.SemaphoreType.DMA((6 * n,)), pltpu.SemaphoreType.DMA((6 * n,)),
                        pltpu.SemaphoreType.DMA((n,))],
    )(*shards)


def _chunk_rows(h, w):
    best = 16
    for t in range(16, h + 1, 16):
        if h % t == 0 and t * w * 4 <= (1 << 20):
            best = t
    return best


def _pair_sum(part, pos, name):
    _, h, w = part.shape
    cr = _chunk_rows(h, w)
    nc = h // cr

    def body(pos_ref, own_ref, send_ref, s_ref, sb_ref, rbuf, send_sems, recv_sems):
        x, y, c, _ = _mesh_pos()
        slot = pl.program_id(0) % 2
        cp = pltpu.make_async_remote_copy(
            src_ref=send_ref, dst_ref=rbuf.at[slot], send_sem=send_sems.at[slot], recv_sem=recv_sems.at[slot],
            device_id=(x, y, 1 - c), device_id_type=MESH)
        cp.start()
        cp.wait_recv()
        s = own_ref[...] + rbuf[slot]
        s_ref[...] = s
        sb_ref[...] = s.astype(BF16)
        cp.wait_send()

    grid_spec = pltpu.PrefetchScalarGridSpec(
        num_scalar_prefetch=1, grid=(4 * nc,),
        in_specs=[pl.BlockSpec((cr, w), lambda k, p: ((2 * (k // nc) + p[1]) * nc + k % nc, 0)),
                  pl.BlockSpec((cr, w), lambda k, p: ((2 * (k // nc) + 1 - p[1]) * nc + k % nc, 0))],
        out_specs=[pl.BlockSpec((cr, w), lambda k, p: (k, 0))] * 2,
        scratch_shapes=[pltpu.VMEM((2, cr, w), F32), pltpu.SemaphoreType.DMA((2,)), pltpu.SemaphoreType.DMA((2,))])
    part2 = part.reshape(8 * h, w)
    s, sb = pl.pallas_call(
        body, name=name, grid_spec=grid_spec,
        out_shape=[jax.ShapeDtypeStruct((4 * h, w), F32), jax.ShapeDtypeStruct((4 * h, w), BF16)],
    )(pos, part2, part2)
    return s.reshape(4, h, w), sb.reshape(4, h, w)


def _scatter_to_chips(sums_bf16, name):
    n = len(sums_bf16)

    def body(*refs):
        b16s, got = refs[:n], refs[n:2 * n]
        send_sems, recv_sems = refs[2 * n:]
        x, y, c, chips = _mesh_pos()
        started = []
        for t in range(n):
            for j, chip in enumerate(chips):
                cp = pltpu.make_async_remote_copy(
                    src_ref=b16s[t].at[2 * chip[0] + chip[1]], dst_ref=got[t].at[j],
                    send_sem=send_sems.at[3 * t + j], recv_sem=recv_sems.at[3 * t + j],
                    device_id=(*chip, c), device_id_type=MESH)
                cp.start()
                started.append(cp)
        for cp in started:
            cp.wait_recv()
        for cp in started:
            cp.wait_send()

    got_shape = [jax.ShapeDtypeStruct((3,) + s.shape[1:], BF16) for s in sums_bf16]
    return pl.pallas_call(
        body, name=name, out_shape=got_shape, in_specs=_hbm_specs(n), out_specs=_hbm_specs(n),
        scratch_shapes=[pltpu.SemaphoreType.DMA((3 * n,)), pltpu.SemaphoreType.DMA((3 * n,))],
    )(*sums_bf16)


def _sum_and_join(sums, got, pos, name):
    _, h, w = sums.shape
    cr = _chunk_rows(h, w)

    def body(pos_ref, mine_ref, got_ref, o_ref, ebuf, rbuf, send_sems, recv_sems):
        x, y, c, _ = _mesh_pos()
        slot = pl.program_id(0) % 2
        e = mine_ref[...]
        for j in range(3):
            e = e + got_ref[j].astype(F32)
        ebuf[slot] = e
        cp = pltpu.make_async_remote_copy(
            src_ref=ebuf.at[slot], dst_ref=rbuf.at[slot], send_sem=send_sems.at[slot], recv_sem=recv_sems.at[slot],
            device_id=(x, y, 1 - c), device_id_type=MESH)
        cp.start()
        o_ref[pos_ref[1]] = e
        cp.wait_recv()
        o_ref[1 - pos_ref[1]] = rbuf[slot]
        cp.wait_send()

    grid_spec = pltpu.PrefetchScalarGridSpec(
        num_scalar_prefetch=1, grid=(h // cr,),
        in_specs=[pl.BlockSpec((None, cr, w), lambda i, p: (p[0], i, 0)),
                  pl.BlockSpec((3, cr, w), lambda i, p: (0, i, 0))],
        out_specs=pl.BlockSpec((2, cr, w), lambda i, p: (0, i, 0)),
        scratch_shapes=[pltpu.VMEM((2, cr, w), F32), pltpu.VMEM((2, cr, w), F32),
                        pltpu.SemaphoreType.DMA((2,)), pltpu.SemaphoreType.DMA((2,))])
    return pl.pallas_call(
        body, name=name, grid_spec=grid_spec, out_shape=jax.ShapeDtypeStruct((2, h, w), F32),
    )(pos, sums, got)


def _reduce_scatter(parts, pos, tag):
    sums, sums_b = [], []
    for t, p in enumerate(parts):
        h = p.shape[0] // 8
        s, sb = _pair_sum(p.reshape(8, h, p.shape[1]), pos, f"rs_pair_{tag}_{t}")
        sums.append(s)
        sums_b.append(sb)
    recv = _scatter_to_chips(sums_b, f"rs_scatter_{tag}")
    out = []
    for t, (s, r) in enumerate(zip(sums, recv)):
        full = _sum_and_join(s, r, pos, f"rs_join_{tag}_{t}")
        out.append(full.reshape(2 * full.shape[1], full.shape[2]))
    return out


def _ffn_fwd(x, prm, w13g, w2g, t, name, tm=512):
    L, D = x.shape
    Fh = w13g.shape[-1]
    tm = min(tm, L)

    def body(x_ref, p_ref, w13_ref, w2_ref, xo_ref, ab_ref, y_ref):
        xv = x_ref[...]
        hn, _, _, _ = _nm(xv, p_ref[3:4, :], p_ref[0:1, :], p_ref[1:2, :])
        hb = hn.astype(BF16)
        acc = jnp.zeros((tm, D), F32)
        for j in range(2):
            a = _dot(hb, w13_ref[j])
            b = _dot(hb, w13_ref[2 + j])
            ab_ref[:, j * Fh:(j + 1) * Fh] = a.astype(BF16)
            ab_ref[:, (2 + j) * Fh:(3 + j) * Fh] = b.astype(BF16)
            g = (a * _sigmoid(a) * b).astype(BF16)
            acc = acc + _dot(g, w2_ref[j * Fh:(j + 1) * Fh, :])
        y_ref[...] = acc.astype(BF16)
        xo_ref[...] = xv + (0.5 * p_ref[2:3, :]) * acc

    return pl.pallas_call(
        body, name=name, grid=(L // tm,),
        in_specs=[_tile(tm, D), _full((8, D)),
                  _resident((None, 4, D, Fh), lambda i: (t, 0, 0, 0)),
                  _resident((None, 2 * Fh, D), lambda i: (t, 0, 0))],
        out_specs=[_tile(tm, D), _tile(tm, 4 * Fh), _tile(tm, D)],
        out_shape=[jax.ShapeDtypeStruct((L, D), F32), jax.ShapeDtypeStruct((L, 4 * Fh), BF16),
                   jax.ShapeDtypeStruct((L, D), BF16)],
        compiler_params=_params())(x, prm, w13g, w2g)


def _ffn_bwd(dout, x, ab, y, prm, w13g, w2g, t, name, tm=256):
    L, D = x.shape
    Fh = w13g.shape[-1]
    tm = min(tm, L)

    def body(do_ref, x_ref, ab_ref, y_ref, p_ref, w13_ref, w2_ref, dx_ref, dab_ref, g_ref, dy_ref, hn_ref, acc_ref):
        i = pl.program_id(0)
        do = do_ref[...]
        gain, shift, scale, gate = p_ref[3:4, :], p_ref[0:1, :], p_ref[1:2, :], p_ref[2:3, :]
        hn, xhat, r, nrm = _nm(x_ref[...], gain, shift, scale)
        hn_ref[...] = hn.astype(BF16)
        dgate = 0.5 * _sum0(do * y_ref[...].astype(F32))
        dyb = ((0.5 * gate) * do).astype(BF16)
        dy_ref[...] = dyb
        dhn = jnp.zeros((tm, D), F32)
        for j in range(2):
            dg = _dot_nt(dyb, w2_ref[j * Fh:(j + 1) * Fh, :])
            a = ab_ref[:, j * Fh:(j + 1) * Fh].astype(F32)
            b = ab_ref[:, (2 + j) * Fh:(3 + j) * Fh].astype(F32)
            sg = _sigmoid(a)
            sa = a * sg
            g_ref[:, j * Fh:(j + 1) * Fh] = (sa * b).astype(BF16)
            da = (dg * b * (sg * (1.0 + a * (1.0 - sg)))).astype(BF16)
            db = (dg * sa).astype(BF16)
            dab_ref[:, j * Fh:(j + 1) * Fh] = da
            dab_ref[:, (2 + j) * Fh:(3 + j) * Fh] = db
            dhn = dhn + _dot_nt(da, w13_ref[j]) + _dot_nt(db, w13_ref[2 + j])
        dx, dshift, dscale, dgn = _nm_bwd(dhn, xhat, r, nrm, gain, scale)
        dx_ref[...] = do + dx
        _acc_rows(acc_ref, i == 0, [dshift, dscale, dgate, dgn])

    return pl.pallas_call(
        body, name=name, grid=(L // tm,),
        in_specs=[_tile(tm, D), _tile(tm, D), _tile(tm, 4 * Fh), _tile(tm, D), _full((8, D)),
                  _resident((None, 4, D, Fh), lambda i: (t, 0, 0, 0)),
                  _resident((None, 2 * Fh, D), lambda i: (t, 0, 0))],
        out_specs=[_tile(tm, D), _tile(tm, 4 * Fh), _tile(tm, 2 * Fh), _tile(tm, D), _tile(tm, D), _full((8, D))],
        out_shape=[jax.ShapeDtypeStruct((L, D), F32), jax.ShapeDtypeStruct((L, 4 * Fh), BF16),
                   jax.ShapeDtypeStruct((L, 2 * Fh), BF16), jax.ShapeDtypeStruct((L, D), BF16),
                   jax.ShapeDtypeStruct((L, D), BF16), jax.ShapeDtypeStruct((8, D), F32)],
        compiler_params=_params())(dout, x, ab, y, prm, w13g, w2g)


def _mm_tn(a, b, slabs, a_slabbed, name, init=None, tl=512):
    L = a.shape[0]
    ka = a.shape[1] // slabs if a_slabbed else a.shape[1]
    nb = b.shape[1] if a_slabbed else b.shape[1] // slabs
    tl = min(tl, L)
    has_init = init is not None

    def body(a_ref, b_ref, *rest):
        o_ref = rest[-1]
        step = pl.program_id(1)

        @pl.when(step == 0)
        def _():
            o_ref[...] = rest[0][...] if has_init else jnp.zeros((ka, nb), F32)

        o_ref[...] += _dot_tn(a_ref[...], b_ref[...])

    in_specs = [pl.BlockSpec((tl, ka), (lambda s, l: (l, s)) if a_slabbed else (lambda s, l: (l, 0))),
                pl.BlockSpec((tl, nb), (lambda s, l: (l, 0)) if a_slabbed else (lambda s, l: (l, s)))]
    args = [a, b]
    if has_init:
        in_specs.append(pl.BlockSpec((ka, nb), lambda s, l: (s, 0)))
        args.append(init)
    return pl.pallas_call(
        body, name=name, grid=(slabs, L // tl), in_specs=in_specs,
        out_specs=pl.BlockSpec((ka, nb), lambda s, l: (s, 0)),
        out_shape=jax.ShapeDtypeStruct((slabs * ka, nb), F32), compiler_params=_params())(*args)


def _even_in_fwd(x, prm, wing, name, tm=512):
    L, D = x.shape
    W = wing.shape[-1]
    tm = min(tm, L)

    def body(x_ref, p_ref, w_ref, q_ref, k_ref, v_ref, u_ref, hn_ref):
        hn, _, _, _ = _nm(x_ref[...], p_ref[3:4, :], p_ref[0:1, :], p_ref[1:2, :])
        hb = hn.astype(BF16)
        hn_ref[...] = hb
        q_ref[...] = _dot(hb, w_ref[0]).astype(BF16)
        k_ref[...] = _dot(hb, w_ref[1]).astype(BF16)
        v_ref[...] = _dot(hb, w_ref[2]).astype(BF16)
        u_ref[...] = _dot(hb, w_ref[3])

    return pl.pallas_call(
        body, name=name, grid=(L // tm,),
        in_specs=[_tile(tm, D), _full((8, D)), _resident((None, 4, D, W), lambda i: (0, 0, 0, 0))],
        out_specs=[_tile(tm, W)] * 4 + [_tile(tm, D)],
        out_shape=[jax.ShapeDtypeStruct((L, W), BF16)] * 3 + [jax.ShapeDtypeStruct((L, W), F32),
                                                              jax.ShapeDtypeStruct((L, D), BF16)],
        compiler_params=_params())(x, prm, wing)


def _even_in_bwd(dout, x, dq, dk, dv, du, prm, wing, name, tm=256):
    L, D = x.shape
    W = wing.shape[-1]
    tm = min(tm, L)

    def body(do_ref, x_ref, dq_ref, dk_ref, dv_ref, du_ref, p_ref, w_ref, dx_ref, ds_ref, acc_ref):
        i = pl.program_id(0)
        gain, shift, scale = p_ref[3:4, :], p_ref[0:1, :], p_ref[1:2, :]
        _, xhat, r, nrm = _nm(x_ref[...], gain, shift, scale)
        dhn = jnp.zeros((tm, D), F32)
        for s, ref in enumerate((dq_ref, dk_ref, dv_ref, du_ref)):
            d = ref[...].astype(BF16)
            ds_ref[:, s * W:(s + 1) * W] = d
            dhn = dhn + _dot_nt(d, w_ref[s])
        dx, dshift, dscale, dgn = _nm_bwd(dhn, xhat, r, nrm, gain, scale)
        dx_ref[...] = do_ref[...] + dx
        _acc_rows(acc_ref, i == 0, [dshift, dscale, None, dgn])

    return pl.pallas_call(
        body, name=name, grid=(L // tm,),
        in_specs=[_tile(tm, D), _tile(tm, D)] + [_tile(tm, W)] * 4 +
                 [_full((8, D)), _resident((None, 4, D, W), lambda i: (0, 0, 0, 0))],
        out_specs=[_tile(tm, D), _tile(tm, 4 * W), _full((8, D))],
        out_shape=[jax.ShapeDtypeStruct((L, D), F32), jax.ShapeDtypeStruct((L, 4 * W), BF16),
                   jax.ShapeDtypeStruct((8, D), F32)],
        compiler_params=_params())(dout, x, dq, dk, dv, du, prm, wing)


def _even_out_fwd(x, mix, prm, woutg, name, tm=512):
    L, D = x.shape
    tm = min(tm, L)

    def body(x_ref, m_ref, p_ref, w_ref, xo_ref, y_ref):
        yv = _dot(m_ref[...], w_ref[...])
        y_ref[...] = yv.astype(BF16)
        xo_ref[...] = x_ref[...] + p_ref[2:3, :] * yv

    return pl.pallas_call(
        body, name=name, grid=(L // tm,),
        in_specs=[_tile(tm, D), _tile(tm, D), _full((8, D)), _resident((None, D, D), lambda i: (0, 0, 0))],
        out_specs=[_tile(tm, D), _tile(tm, D)],
        out_shape=[jax.ShapeDtypeStruct((L, D), F32), jax.ShapeDtypeStruct((L, D), BF16)],
        compiler_params=_params())(x, mix, prm, woutg)


def _even_out_bwd(dout, y, prm, woutg, name, tm=512):
    L, D = dout.shape
    W = D // 2
    tm = min(tm, L)

    def body(do_ref, y_ref, p_ref, w_ref, dy_ref, da_ref, dp_ref, acc_ref):
        i = pl.program_id(0)
        do = do_ref[...]
        dgate = _sum0(do * y_ref[...].astype(F32))
        dyb = (p_ref[2:3, :] * do).astype(BF16)
        dy_ref[...] = dyb
        da_ref[...] = _dot_nt(dyb, w_ref[0:W, :]).astype(BF16)
        dp_ref[...] = _dot_nt(dyb, w_ref[W:2 * W, :])
        _acc_rows(acc_ref, i == 0, [None, None, dgate])

    return pl.pallas_call(
        body, name=name, grid=(L // tm,),
        in_specs=[_tile(tm, D), _tile(tm, D), _full((8, D)), _resident((None, D, D), lambda i: (0, 0, 0))],
        out_specs=[_tile(tm, D), _tile(tm, W), _tile(tm, W), _full((8, D))],
        out_shape=[jax.ShapeDtypeStruct((L, D), BF16), jax.ShapeDtypeStruct((L, W), BF16),
                   jax.ShapeDtypeStruct((L, W), F32), jax.ShapeDtypeStruct((8, D), F32)],
        compiler_params=_params())(dout, y, prm, woutg)


def _group_ri(variant, qr, kr):
    first_key = (0, qr, GK - NA_KH)[variant]
    if not first_key <= kr < first_key + NA_KH:
        return None
    return kr - qr + (NA_KH - 1, NA_KH - 1 - NA_KH // 2, NA_KH - 1 - (GK - GQ))[variant]


def _bias_table(rpb, name):
    H = rpb.shape[0]
    nri, nci = 2 * NA_KH - 1, 2 * NA_KW - 1
    col = jnp.arange(GRID_W)
    rel = (col[None, :] - col[:, None] + (NA_KW - 1)).reshape(1, -1)
    onehot = (rel == jnp.arange(32)[:, None]).astype(F32)
    cs = jnp.clip(col - NA_KW // 2, 0, GRID_W - NA_KW)
    ok = ((col[None, :] >= cs[:, None]) & (col[None, :] < cs[:, None] + NA_KW)).astype(F32).reshape(1, -1)
    rpb2 = jnp.pad(rpb.reshape(H * nri, nci), ((0, 0), (0, 32 - nci)))

    def body(r_ref, e_ref, m_ref, o_ref):
        t = jnp.dot(r_ref[...], e_ref[...], preferred_element_type=F32, precision=lax.Precision.HIGHEST)
        o_ref[...] = jnp.where(m_ref[...] > 0.0, t, NEG_INF)

    tab = pl.pallas_call(body, name=name, out_shape=jax.ShapeDtypeStruct((H * nri, GRID_W * GRID_W), F32))(
        rpb2, onehot, ok)
    tab = tab.reshape(H, nri, GRID_W, GRID_W)
    outside = jnp.full((H, GRID_W, GRID_W), NEG_INF, F32)
    variants = []
    for variant in range(3):
        rows = []
        for qr in range(GQ):
            ris = [_group_ri(variant, qr, kr) for kr in range(GK)]
            rows.append(jnp.concatenate([outside if ri is None else tab[:, ri] for ri in ris], axis=2))
        variants.append(jnp.concatenate(rows, axis=1))
    return jnp.stack(variants, axis=1)


def _attn_probs(q, kw, kc, bias, scale):
    s_w = _dot_nt(q, kw) * scale + bias
    s_c = _dot_nt(q, kc) * scale
    m = jnp.maximum(jnp.max(s_w, axis=-1, keepdims=True), jnp.max(s_c, axis=-1, keepdims=True))
    e_w = jnp.exp(s_w - m)
    e_c = jnp.exp(s_c - m)
    inv = 1.0 / (jnp.sum(e_w, axis=-1, keepdims=True) + jnp.sum(e_c, axis=-1, keepdims=True))
    return e_w * inv, e_c * inv


def _group_place(g, R):
    G = R // GQ
    kb = jnp.clip(g * GQ - NA_KH // 2, 0, R - GK)
    variant = jnp.where(g == 0, 0, jnp.where(g == G - 1, 2, 1))
    return pl.multiple_of(g * (GQ * GRID_W), GQ * GRID_W), pl.multiple_of(kb * GRID_W, GRID_W), variant


def _attn_fwd(qh, kh, vh, kch, vch, bias, name):
    H, L, dh = qh.shape
    C = kch.shape[1]
    R = L // GRID_W
    nq, nk = GQ * GRID_W, GK * GRID_W
    scale = dh ** -0.5

    def body(q_ref, k_ref, v_ref, kc_ref, vc_ref, b_ref, o_ref):
        kc = kc_ref[...]
        vc = vc_ref[...]

        def group(g, carry):
            q0, k0, variant = _group_place(g, R)
            p_w, p_c = _attn_probs(q_ref[pl.ds(q0, nq), :], k_ref[pl.ds(k0, nk), :], kc, b_ref[variant], scale)
            o = _dot(p_w.astype(BF16), v_ref[pl.ds(k0, nk), :]) + _dot(p_c.astype(BF16), vc)
            o_ref[pl.ds(q0, nq), :] = o.astype(BF16)
            return carry

        lax.fori_loop(0, R // GQ, group, 0)

    head = lambda n: pl.BlockSpec((None, n, dh), lambda h: (h, 0, 0))
    return pl.pallas_call(
        body, name=name, grid=(H,),
        in_specs=[head(L), head(L), head(L), head(C), head(C),
                  pl.BlockSpec((None, 3, nq, nk), lambda h: (h, 0, 0, 0))],
        out_specs=head(L), out_shape=jax.ShapeDtypeStruct((H, L, dh), BF16),
        compiler_params=_params())(qh, kh, vh, kch, vch, bias)


def _attn_bwd(qh, kh, vh, kch, vch, bias, doh, name):
    H, L, dh = qh.shape
    C = kch.shape[1]
    R = L // GRID_W
    nq, nk = GQ * GRID_W, GK * GRID_W
    scale = dh ** -0.5

    def body(q_ref, k_ref, v_ref, kc_ref, vc_ref, b_ref, do_ref, dq_ref, dk_ref, dv_ref, dkc_ref, dvc_ref, db_ref):
        kc = kc_ref[...]
        vc = vc_ref[...]
        dk_ref[...] = jnp.zeros((L, dh), F32)
        dv_ref[...] = jnp.zeros((L, dh), F32)
        dkc_ref[...] = jnp.zeros((C, dh), F32)
        dvc_ref[...] = jnp.zeros((C, dh), F32)
        db_ref[...] = jnp.zeros((3, nq, nk), F32)

        def group(g, carry):
            q0, k0, variant = _group_place(g, R)
            q = q_ref[pl.ds(q0, nq), :]
            kw = k_ref[pl.ds(k0, nk), :]
            vw = v_ref[pl.ds(k0, nk), :]
            do = do_ref[pl.ds(q0, nq), :]
            p_w, p_c = _attn_probs(q, kw, kc, b_ref[variant], scale)
            dp_w = _dot_nt(do, vw)
            dp_c = _dot_nt(do, vc)
            delta = jnp.sum(p_w * dp_w, axis=-1, keepdims=True) + jnp.sum(p_c * dp_c, axis=-1, keepdims=True)
            ds_w = p_w * (dp_w - delta)
            ds_c = p_c * (dp_c - delta)
            db_ref[variant] += ds_w
            dsw = (ds_w * scale).astype(BF16)
            dsc = (ds_c * scale).astype(BF16)
            dq_ref[pl.ds(q0, nq), :] = (_dot(dsw, kw) + _dot(dsc, kc)).astype(BF16)
            dk_ref[pl.ds(k0, nk), :] += _dot_tn(dsw, q)
            dv_ref[pl.ds(k0, nk), :] += _dot_tn(p_w.astype(BF16), do)
            dkc_ref[...] += _dot_tn(dsc, q)
            dvc_ref[...] += _dot_tn(p_c.astype(BF16), do)
            return carry

        lax.fori_loop(0, R // GQ, group, 0)

    head = lambda n: _resident((None, n, dh), lambda h: (h, 0, 0))
    bspec = _resident((None, 3, nq, nk), lambda h: (h, 0, 0, 0))
    return pl.pallas_call(
        body, name=name, grid=(H,),
        in_specs=[head(L), head(L), head(L), head(C), head(C), bspec, head(L)],
        out_specs=[head(L), head(L), head(L), head(C), head(C), bspec],
        out_shape=[jax.ShapeDtypeStruct((H, L, dh), BF16)] + [jax.ShapeDtypeStruct((H, L, dh), F32)] * 2 +
                  [jax.ShapeDtypeStruct((H, C, dh), F32)] * 2 + [jax.ShapeDtypeStruct((H, 3, nq, nk), F32)],
        compiler_params=_params())(qh, kh, vh, kch, vch, bias, doh)


def _rpb_grad(dbias, name):
    H = dbias.shape[0]
    nri, nci = 2 * NA_KH - 1, 2 * NA_KW - 1
    d6 = dbias.reshape(H, 3, GQ, GRID_W, GK, GRID_W).transpose(0, 1, 2, 4, 3, 5)
    col = jnp.arange(GRID_W)
    onehot = (col[None, None, :] - col[None, :, None] + (NA_KW - 1) == jnp.arange(32)[:, None, None]).astype(F32)
    places = [(v, qr, kr) for v in range(3) for qr in range(GQ) for kr in range(GK)]

    def body(d_ref, m_ref, o_ref, t_ref):
        t_ref[...] = jnp.zeros((32, GRID_W), F32)
        o_ref[...] = jnp.zeros((16, 32, 128), F32)
        for ri in range(nri):
            a = None
            for place in places:
                if _group_ri(*place) == ri:
                    blk = d_ref[place]
                    a = blk if a is None else a + blk
            for ci in range(nci):
                t_ref[ci:ci + 1, :] = _sum0(a * m_ref[ci])
            o_ref[ri] = jnp.broadcast_to(jnp.sum(t_ref[...], axis=1, keepdims=True), (32, 128))

    out = pl.pallas_call(
        body, name=name, grid=(H,),
        in_specs=[pl.BlockSpec((None, 3, GQ, GK, GRID_W, GRID_W), lambda h: (h, 0, 0, 0, 0, 0)),
                  pl.BlockSpec((32, GRID_W, GRID_W), lambda h: (0, 0, 0))],
        out_specs=pl.BlockSpec((None, 16, 32, 128), lambda h: (h, 0, 0, 0)),
        out_shape=jax.ShapeDtypeStruct((H, 16, 32, 128), F32),
        scratch_shapes=[pltpu.VMEM((32, GRID_W), F32)])(d6, onehot)
    return out[:, :nri, :nci, 0]


def _window_count(t, w, L):
    lo = jnp.clip(t - w // 2, 0, L)
    hi = jnp.clip(t - w // 2 + w, 0, L)
    return jnp.maximum(hi - lo, 1).astype(F32)


def _running_sum(v, w):
    k = 1
    while k < w:
        v = v + _shift_rows(v, k)
        k *= 2
    return v


def _pool_fwd(u, poolw, pscale, name, tm=512):
    L, W = u.shape
    G = POOL_GROUP_DIM
    tm = min(tm, L)
    nt = L // tm

    def body(c_ref, p_ref, n_ref, w_ref, s_ref, o_ref, dm_ref):
        i = pl.program_id(0)
        ext = _ext(p_ref[...], c_ref[...], n_ref[...], i, nt)
        t = i * tm + lax.broadcasted_iota(jnp.int32, (tm, 1), 0)
        for g, w in enumerate(POOL_WINDOWS):
            e = ext[:, g * G:(g + 1) * G]
            win = _shift_rows(_running_sum(e, w), -(w // 2 - 1))[HALO:HALO + tm]
            dmx = (win / _window_count(t, w, L) - e[HALO:HALO + tm]).astype(BF16)
            dm_ref[:, g * G:(g + 1) * G] = dmx
            o_ref[:, g * G:(g + 1) * G] = (_dot(dmx, w_ref[g]) * s_ref[:, g * G:(g + 1) * G]).astype(BF16)

    return pl.pallas_call(
        body, name=name, grid=(nt,),
        in_specs=[_tile(tm, W), _halo_prev(tm, W), _halo_next(tm, W, L), _full((4, G, G)), _full((1, W))],
        out_specs=[_tile(tm, W), _tile(tm, W)],
        out_shape=[jax.ShapeDtypeStruct((L, W), BF16)] * 2, compiler_params=_params())(u, u, u, poolw, pscale)


def _pool_bwd(dpool, dmx, poolw, pscale, name, tm=512):
    L, W = dpool.shape
    G = POOL_GROUP_DIM
    tm = min(tm, L)
    nt = L // tm

    def body(c_ref, p_ref, n_ref, dm_ref, w_ref, s_ref, du_ref, dw_ref, acc_ref):
        i = pl.program_id(0)
        ext = _ext(p_ref[...], c_ref[...], n_ref[...], i, nt)
        te = i * tm - HALO + lax.broadcasted_iota(jnp.int32, (tm + 2 * HALO, 1), 0)

        @pl.when(i == 0)
        def _():
            dw_ref[...] = jnp.zeros((4 * G, G), F32)

        rows = []
        for g, w in enumerate(POOL_WINDOWS):
            sc = s_ref[:, g * G:(g + 1) * G]
            dpre = (ext[:, g * G:(g + 1) * G] * sc).astype(BF16)
            dd = _dot_nt(dpre, w_ref[g])
            spread = _shift_rows(_running_sum(dd / _window_count(te, w, L), w), -(w // 2))
            du_ref[:, g * G:(g + 1) * G] = (spread - dd)[HALO:HALO + tm]
            dmx_g = dm_ref[:, g * G:(g + 1) * G]
            rows.append(_sum0(c_ref[:, g * G:(g + 1) * G] * _dot(dmx_g, w_ref[g])))
            dw_ref[g * G:(g + 1) * G, :] += _dot_tn(dmx_g, dpre[HALO:HALO + tm])
        _acc_rows(acc_ref, i == 0, [jnp.concatenate(rows, axis=1)])

    return pl.pallas_call(
        body, name=name, grid=(nt,),
        in_specs=[_tile(tm, W), _halo_prev(tm, W), _halo_next(tm, W, L), _tile(tm, W), _full((4, G, G)),
                  _full((1, W))],
        out_specs=[_tile(tm, W), _full((4 * G, G)), _full((8, W))],
        out_shape=[jax.ShapeDtypeStruct((L, W), F32), jax.ShapeDtypeStruct((4 * G, G), F32),
                   jax.ShapeDtypeStruct((8, W), F32)],
        compiler_params=_params())(dpool, dpool, dpool, dmx, poolw, pscale)


def _conv3(z, cw):
    return _shift_rows(z, 1) * cw[0] + z * cw[1] + _shift_rows(z, -1) * cw[2]


def _conv_fwd(x, prm, wing, woutg, name, tm=512):
    L, D = x.shape
    Ws = wing.shape[-1]
    tm = min(tm, L)
    nt = L // tm
    te = tm + 2 * HALO

    def body(c_ref, p_ref, n_ref, prm_ref, wi_ref, wo_ref, xo_ref, y_ref, b_ref):
        i = pl.program_id(0)
        xe = jnp.concatenate([p_ref[...], c_ref[...], n_ref[...]], axis=0)
        hn, _, _, _ = _nm(xe, prm_ref[3:4, :], prm_ref[0:1, :], prm_ref[1:2, :])
        hb = hn.astype(BF16)
        proj = jnp.concatenate([_dot(hb, wi_ref[s]) for s in range(4)], axis=1)
        bg, cg, xin = proj[:, :D], proj[:, D:2 * D], proj[:, 2 * D:]
        tpos = i * tm - HALO + lax.broadcasted_iota(jnp.int32, (te, 1), 0)
        valid = ((tpos >= 0) & (tpos < L)).astype(F32)
        yc = _conv3(cg * xin * valid, [prm_ref[4 + k:5 + k, :] for k in range(3)])
        h2 = (bg * yc)[HALO:HALO + tm].astype(BF16)
        yv = _dot(h2, wo_ref[...])
        y_ref[...] = yv.astype(BF16)
        xo_ref[...] = c_ref[...] + prm_ref[2:3, :] * yv
        b_ref[...] = proj[HALO:HALO + tm].astype(BF16)

    return pl.pallas_call(
        body, name=name, grid=(nt,),
        in_specs=[_tile(tm, D), _halo_prev(tm, D), _halo_next(tm, D, L), _full((8, D)),
                  _resident((None, 4, D, Ws), lambda i: (0, 0, 0, 0)),
                  _resident((None, D, D), lambda i: (0, 0, 0))],
        out_specs=[_tile(tm, D), _tile(tm, D), _tile(tm, 3 * D)],
        out_shape=[jax.ShapeDtypeStruct((L, D), F32), jax.ShapeDtypeStruct((L, D), BF16),
                   jax.ShapeDtypeStruct((L, 3 * D), BF16)],
        compiler_params=_params())(x, x, x, prm, wing, woutg)


def _conv_bwd(dout, x, y, bcx, prm, wing, woutg, name, tm=256):
    L, D = x.shape
    Ws = wing.shape[-1]
    tm = min(tm, L)
    nt = L // tm
    te = tm + 2 * HALO

    def body(dc_ref, dp_ref, dn_ref, x_ref, y_ref, bc_ref, bp_ref, bn_ref, prm_ref, wi_ref, wo_ref,
             dx_ref, dpr_ref, h2_ref, dy_ref, acc_ref):
        i = pl.program_id(0)
        gain, shift, scale, gate = prm_ref[3:4, :], prm_ref[0:1, :], prm_ref[1:2, :], prm_ref[2:3, :]
        taps = [prm_ref[4 + k:5 + k, :] for k in range(3)]
        do = dc_ref[...]
        doe = _ext(dp_ref[...], do, dn_ref[...], i, nt)
        dye = (gate * doe).astype(BF16)
        dy_ref[...] = dye[HALO:HALO + tm]
        dh2 = _dot_nt(dye, wo_ref[...])
        be = jnp.concatenate([bp_ref[...], bc_ref[...], bn_ref[...]], axis=0).astype(F32)
        bg, cg, xin = be[:, :D], be[:, D:2 * D], be[:, 2 * D:]
        tpos = i * tm - HALO + lax.broadcasted_iota(jnp.int32, (te, 1), 0)
        valid = ((tpos >= 0) & (tpos < L)).astype(F32)
        z = cg * xin * valid
        yc = _conv3(z, taps)
        dyc = dh2 * bg
        h2_ref[...] = (bg * yc)[HALO:HALO + tm].astype(BF16)
        dz = _conv3(dyc, taps[::-1]) * valid
        dproj = jnp.concatenate([dh2 * yc, dz * xin, dz * cg], axis=1)[HALO:HALO + tm].astype(BF16)
        dpr_ref[...] = dproj
        dhn = jnp.zeros((tm, D), F32)
        for s in range(4):
            dhn = dhn + _dot_nt(dproj[:, s * Ws:(s + 1) * Ws], wi_ref[s])
        _, xhat, r, nrm = _nm(x_ref[...], gain, shift, scale)
        dx, dshift, dscale, dgn = _nm_bwd(dhn, xhat, r, nrm, gain, scale)
        dx_ref[...] = do + dx
        dgate = _sum0(do * y_ref[...].astype(F32))
        dtaps = [_sum0((dyc * _shift_rows(z, 1 - k))[HALO:HALO + tm]) for k in range(3)]
        _acc_rows(acc_ref, i == 0, [dshift, dscale, dgate, dgn] + dtaps)

    return pl.pallas_call(
        body, name=name, grid=(nt,),
        in_specs=[_tile(tm, D), _halo_prev(tm, D), _halo_next(tm, D, L), _tile(tm, D), _tile(tm, D),
                  _tile(tm, 3 * D), _halo_prev(tm, 3 * D), _halo_next(tm, 3 * D, L), _full((8, D)),
                  _resident((None, 4, D, Ws), lambda i: (0, 0, 0, 0)),
                  _resident((None, D, D), lambda i: (0, 0, 0))],
        out_specs=[_tile(tm, D), _tile(tm, 3 * D), _tile(tm, D), _tile(tm, D), _full((8, D))],
        out_shape=[jax.ShapeDtypeStruct((L, D), F32), jax.ShapeDtypeStruct((L, 3 * D), BF16),
                   jax.ShapeDtypeStruct((L, D), BF16), jax.ShapeDtypeStruct((L, D), BF16),
                   jax.ShapeDtypeStruct((8, D), F32)],
        compiler_params=_params())(dout, dout, dout, x, y, bcx, bcx, bcx, prm, wing, woutg)


def _loss_head(x, tgt, fg, name, tm=512):
    L, D = x.shape
    tm = min(tm, L)

    def body(x_ref, t_ref, g_ref, dx_ref, acc_ref):
        i = pl.program_id(0)
        xv = x_ref[...]
        g = g_ref[...]
        r = lax.rsqrt(jnp.mean(xv * xv, axis=-1, keepdims=True) + RMS_EPS)
        xhat = xv * r
        err = xhat * g - t_ref[...]
        part = 0.5 * jnp.sum(jnp.mean(err * err, axis=-1, keepdims=True), axis=0, keepdims=True)
        dy = err * (1.0 / D)
        dxh = dy * g
        dx_ref[...] = r * (dxh - xhat * jnp.mean(dxh * xhat, axis=-1, keepdims=True))
        _acc_rows(acc_ref, i == 0, [_sum0(dy * xhat), jnp.broadcast_to(part, (1, D))])

    return pl.pallas_call(
        body, name=name, grid=(L // tm,), in_specs=[_tile(tm, D), _tile(tm, D), _full((1, D))],
        out_specs=[_tile(tm, D), _full((8, D))],
        out_shape=[jax.ShapeDtypeStruct((L, D), F32), jax.ShapeDtypeStruct((8, D), F32)],
        compiler_params=_params())(x, tgt, fg)


def _mod_fwd(cond, mod_w, mod_b, name, tn=768):
    nl, D, N = mod_w.shape
    tn = min(tn, N)

    def body(c_ref, w_ref, b_ref, o_ref):
        cv = c_ref[...]
        s = (cv * _sigmoid(cv)).astype(BF16)
        o_ref[...] = _dot(s, w_ref[...].astype(BF16)) + b_ref[...]

    return pl.pallas_call(
        body, name=name, grid=(nl, N // tn),
        in_specs=[pl.BlockSpec((16, D), lambda l, j: (0, 0)), pl.BlockSpec((None, D, tn), lambda l, j: (l, 0, j)),
                  pl.BlockSpec((None, 1, tn), lambda l, j: (l, 0, j))],
        out_specs=pl.BlockSpec((None, 16, tn), lambda l, j: (l, 0, j)),
        out_shape=jax.ShapeDtypeStruct((nl, 16, N), F32), compiler_params=_params())(cond, mod_w, mod_b)


def _mod_bwd(cond, dm, mod_w, name, tn=768):
    nl, D, N = mod_w.shape
    tn = min(tn, N)

    def body(c_ref, d_ref, w_ref, dw_ref, dc_ref):
        first = (pl.program_id(0) == 0) & (pl.program_id(1) == 0)
        cv = c_ref[...]
        s = (cv * _sigmoid(cv)).astype(BF16)
        d = d_ref[...].astype(BF16)
        dw_ref[...] = _dot_tn(s, d)

        @pl.when(first)
        def _():
            dc_ref[...] = jnp.zeros((16, D), F32)

        dc_ref[...] += _dot_nt(d, w_ref[...].astype(BF16))

    return pl.pallas_call(
        body, name=name, grid=(nl, N // tn),
        in_specs=[pl.BlockSpec((16, D), lambda l, j: (0, 0)), pl.BlockSpec((None, 16, tn), lambda l, j: (l, 0, j)),
                  pl.BlockSpec((None, D, tn), lambda l, j: (l, 0, j))],
        out_specs=[pl.BlockSpec((None, D, tn), lambda l, j: (l, 0, j)), pl.BlockSpec((16, D), lambda l, j: (0, 0))],
        out_shape=[jax.ShapeDtypeStruct((nl, D, N), F32), jax.ShapeDtypeStruct((16, D), F32)],
        compiler_params=_params())(cond, dm, mod_w)


def _mod_small_grads(dm_all, cond, dsilu_parts, name):
    nl, _, N = dm_all.shape
    D = cond.shape[1]

    def body(d_ref, c_ref, p_ref, db_ref, dc_ref):
        for l in range(nl):
            db_ref[l] = _sum0(d_ref[l])
        tot = p_ref[0, 8:9, :]
        for k in range(1, N_CHIPS):
            tot = tot + p_ref[2 * k, 8:9, :]
        cv = c_ref[8:9, :]
        sg = _sigmoid(cv)
        dc_ref[...] = tot * (sg * (1.0 + cv * (1.0 - sg)))

    return pl.pallas_call(
        body, name=name, out_shape=[jax.ShapeDtypeStruct((nl, 1, N), F32), jax.ShapeDtypeStruct((1, D), F32)],
    )(dm_all, cond, dsilu_parts)


def _prm(rows, D):
    rows = [r.reshape(1, D) for r in rows]
    return jnp.concatenate(rows + [jnp.zeros((8 - len(rows), D), F32)], axis=0)


def _heads(a):
    L = a.shape[0]
    return a.reshape(L, NA_HEADS, NA_HEAD_DIM).transpose(1, 0, 2)


def _unheads(a):
    return a.transpose(1, 0, 2).reshape(a.shape[1], NA_HEADS * NA_HEAD_DIM)


def kernel(x, c, ctx, c_ctx, mod_w, mod_b, norm_g, ffn_w13, ffn_w2, even_w_in, even_w_out, na_rpb, pool_w, pool_scale, conv_w_in, conv_w, conv_w_out, final_g, loss_target, m_c_ctx, m_mod_w, m_mod_b, m_norm_g, m_ffn_w13, m_ffn_w2, m_even_w_in, m_even_w_out, m_na_rpb, m_pool_w, m_pool_scale, m_conv_w_in, m_conv_w, m_conv_w_out, m_final_g, v_c_ctx, v_mod_w, v_mod_b, v_norm_g, v_ffn_w13, v_ffn_w2, v_even_w_in, v_even_w_out, v_na_rpb, v_pool_w, v_pool_scale, v_conv_w_in, v_conv_w, v_conv_w_out, v_final_g):
    xi, yi, ci = lax.axis_index("x"), lax.axis_index("y"), lax.axis_index("c")
    chip = 2 * xi + yi
    dev = 4 * xi + 2 * yi + ci
    _, L, D = x.shape
    C = ctx.shape[1]
    Ds = D // N_CHIPS
    Nm = mod_w.shape[-1]
    Fh = ffn_w13.shape[-1]
    Fq = ffn_w2.shape[2]
    assert ffn_w13.shape[:2] == (2, 2) and Fh == 2 * Fq and L % (GQ * GRID_W) == 0 and L // GRID_W >= GK and GQ == NA_KH // 2
    x0, ctx0, tgt = x[0], ctx[0], loss_target[0]

    pad = lambda a: jnp.pad(a, ((0, 0), (0, D - a.shape[1])))
    pack1 = jnp.concatenate([c, pad(norm_g.reshape(6, Ds)), pad(conv_w.reshape(3, Ds)), jnp.zeros((6, D), F32)], axis=0)
    g1 = _small_all_gather(pack1, "ag_cond")
    cond = jnp.concatenate([g1[:, 0], c_ctx[None], jnp.zeros((7, D), F32)], axis=0)
    norm_full = jnp.concatenate([g1[2 * k, 1:7, :Ds] for k in range(N_CHIPS)], axis=1).reshape(2, 3, D)
    convw_full = jnp.concatenate([g1[2 * k, 7:10, :Ds] for k in range(N_CHIPS)], axis=1)

    mod_b_loc = lax.dynamic_slice_in_dim(mod_b, chip * Nm, Nm, axis=1).reshape(2, 1, Nm)
    m_loc = _mod_fwd(cond, mod_w, mod_b_loc, "mod_fwd")
    g2 = _small_all_gather(m_loc.reshape(32, Nm), "ag_mod")
    m_all = jnp.concatenate([g2[2 * k] for k in range(N_CHIPS)], axis=1).reshape(2, 16, N_MOD, D)
    m_lat = lax.dynamic_index_in_dim(m_all, dev, axis=1, keepdims=False)
    m_ctx = m_all[:, 8]

    def prm(mods, layer, base, gain_idx, extra=()):
        return _prm([mods[layer, base], mods[layer, base + 1], mods[layer, base + 2], norm_full[layer, gain_idx],
                     *extra], D)

    def shard_bf16(w, name):
        return _cast_bf16(w.reshape(-1, w.shape[-1]), name).reshape(-1, *w.shape[-2:])

    w13g, w2g, eing, eoutg, cing, coutg = _gather_shards(
        [shard_bf16(ffn_w13, "cast_w13"), shard_bf16(ffn_w2, "cast_w2"), shard_bf16(even_w_in, "cast_ein"),
         shard_bf16(even_w_out, "cast_eout"), shard_bf16(conv_w_in, "cast_cin"), shard_bf16(conv_w_out, "cast_cout")],
        "ag_weights")
    w13g = w13g.reshape(4, 4, D, Fh)
    eing = eing.reshape(1, 4, D, NA_WIDTH)
    cing = cing.reshape(1, 4, D, conv_w_in.shape[-1])

    p_f1 = prm(m_lat, 0, 0, 0)
    p_mx = prm(m_lat, 0, 3, 1)
    p_f2 = prm(m_lat, 0, 6, 2)
    p_g1 = prm(m_lat, 1, 0, 0)
    p_cv = prm(m_lat, 1, 3, 1, extra=(convw_full[0], convw_full[1], convw_full[2]))
    p_g2 = prm(m_lat, 1, 6, 2)
    pc_f1 = prm(m_ctx, 0, 0, 0)
    pc_mx = prm(m_ctx, 0, 3, 1)

    x1, ab1, y1 = _ffn_fwd(x0, p_f1, w13g, w2g, 0, "ffn_fwd_l0a")
    ctx1, abc, yc = _ffn_fwd(ctx0, pc_f1, w13g, w2g, 0, "ffn_fwd_ctx")
    q, k, v, u, hn_mx = _even_in_fwd(x1, p_mx, eing, "even_in_fwd")
    _, k_c, v_c, _, hn_cx = _even_in_fwd(ctx1, pc_mx, eing, "even_in_ctx")
    bias = _bias_table(na_rpb[0], "bias_table")
    qh, kh, vh, kch, vch = _heads(q), _heads(k), _heads(v), _heads(k_c), _heads(v_c)
    att = _unheads(_attn_fwd(qh, kh, vh, kch, vch, bias, "attn_fwd"))
    pw_b = _cast_bf16(pool_w.reshape(-1, POOL_GROUP_DIM), "cast_poolw").reshape(4, POOL_GROUP_DIM, POOL_GROUP_DIM)
    pool, dmx = _pool_fwd(u, pw_b, pool_scale, "pool_fwd")
    mix = jnp.concatenate([att, pool], axis=1)
    x2, ymx = _even_out_fwd(x1, mix, p_mx, eoutg, "even_out_fwd")
    x3, ab2, y2 = _ffn_fwd(x2, p_f2, w13g, w2g, 1, "ffn_fwd_l0b")
    x4, ab3, y3 = _ffn_fwd(x3, p_g1, w13g, w2g, 2, "ffn_fwd_l1a")
    x5, ycv, bcx = _conv_fwd(x4, p_cv, cing, coutg, "conv_fwd")
    x6, ab4, y4 = _ffn_fwd(x5, p_g2, w13g, w2g, 3, "ffn_fwd_l1b")
    dx6, acc_head = _loss_head(x6, tgt, final_g.reshape(1, D), "loss_head")
    loss = lax.psum(acc_head[1, 0], ("x", "y", "c"))

    def ffn_back(dout, xin, ab, yy, p, t, tag, init13=None, init2=None):
        dx, dab, gact, dy, hn, acc = _ffn_bwd(dout, xin, ab, yy, p, w13g, w2g, t, f"ffn_bwd_{tag}")
        dw13 = _mm_tn(hn, dab, 4, False, f"dw13_{tag}", init=init13)
        dw2 = _mm_tn(gact, dy, 2, True, f"dw2_{tag}", init=init2)
        return dx, acc, dw13, dw2

    dx5, acc_g2, dw13_3, dw2_3 = ffn_back(dx6, x5, ab4, y4, p_g2, 3, "l1b")
    dx4, dproj, h2, dycv, acc_cv = _conv_bwd(dx5, x4, ycv, bcx, p_cv, cing, coutg, "conv_bwd")
    hn_cv = _even_hn(x4, p_cv, "conv_hn")
    dcin = _mm_tn(hn_cv, dproj, 4, False, "dw_cin")
    dcout = _mm_tn(h2, dycv, 1, False, "dw_cout")
    dx3, acc_g1, dw13_2, dw2_2 = ffn_back(dx4, x3, ab3, y3, p_g1, 2, "l1a")
    dx2, acc_f2, dw13_1, dw2_1 = ffn_back(dx3, x2, ab2, y2, p_f2, 1, "l0b")

    dymx, datt, dpool, acc_mxo = _even_out_bwd(dx2, ymx, p_mx, eoutg, "even_out_bwd")
    deout = _mm_tn(mix, dymx, 1, False, "dw_eout")
    du, dpoolw, acc_pool = _pool_bwd(dpool, dmx, pw_b, pool_scale, "pool_bwd")
    doh = _heads(datt)
    dqh, dkh, dvh, dkch, dvch, dbias = _attn_bwd(qh, kh, vh, kch, vch, bias, doh, "attn_bwd")
    drpb = _rpb_grad(dbias, "rpb_grad")
    dx1, dstack, acc_mxi = _even_in_bwd(dx2, x1, _unheads(dqh), _unheads(dkh), _unheads(dvh), du, p_mx, eing,
                                        "even_in_bwd")
    zc = jnp.zeros((C, NA_WIDTH), F32)
    dctx1, dstack_c, accc_mx = _even_in_bwd(jnp.zeros((C, D), F32), ctx1, zc, _unheads(dkch), _unheads(dvch), zc,
                                            pc_mx, eing, "even_in_bwd_ctx")
    dein_c = _mm_tn(hn_cx, dstack_c, 4, False, "dw_ein_ctx")
    dein = _mm_tn(hn_mx, dstack, 4, False, "dw_ein", init=dein_c)
    _, accc_f1, dw13_c, dw2_c = ffn_back(dctx1, ctx0, abc, yc, pc_f1, 0, "ctx")
    dx0, acc_f1, dw13_0, dw2_0 = ffn_back(dx1, x0, ab1, y1, p_f1, 0, "l0a", init13=dw13_c, init2=dw2_c)

    z1 = jnp.zeros((1, D), F32)
    dm_lat = jnp.concatenate([acc_f1[0:3], acc_mxi[0:2], acc_mxo[2:3], acc_f2[0:3],
                              acc_g1[0:3], acc_cv[0:3], acc_g2[0:3]], axis=0)
    dm_ctx = jnp.concatenate([accc_f1[0:3], accc_mx[0:2]] + [z1] * 13, axis=0)
    dnorm = jnp.concatenate([acc_f1[3:4] + accc_f1[3:4], acc_mxi[3:4] + accc_mx[3:4], acc_f2[3:4],
                             acc_g1[3:4], acc_cv[3:4], acc_g2[3:4]], axis=0)
    rpb_flat = jnp.pad(drpb.reshape(-1), (0, 4 * D - drpb.size)).reshape(4, D)
    pack3 = jnp.concatenate([dm_lat, dm_ctx, dnorm, acc_cv[4:7], acc_head[0:1], pad(acc_pool[0:1]), z1,
                             dpoolw.reshape(-1, D), rpb_flat, jnp.zeros((4, D), F32)], axis=0)
    g3 = _small_all_gather(pack3, "ag_small")
    tot = _sum_devices(g3, "sum_small")
    dm_all = jnp.concatenate([g3[:, 0:18].reshape(8, 2, N_MOD * D).transpose(1, 0, 2),
                              tot[18:36].reshape(2, 1, N_MOD * D), jnp.zeros((2, 7, N_MOD * D), F32)], axis=1)
    dm_loc = lax.dynamic_slice_in_dim(dm_all, chip * Nm, Nm, axis=2)
    g_mod_w, dsilu = _mod_bwd(cond, dm_loc, mod_w, "mod_bwd")
    g4 = _small_all_gather(dsilu, "ag_dsilu")
    g_mod_b, g_c_ctx = _mod_small_grads(dm_all, cond, g4, "mod_small")
    g_mod_b = g_mod_b.reshape(2, N_MOD * D)
    g_c_ctx = g_c_ctx.reshape(D)
    g_norm_full = tot[36:42].reshape(2, 3, D)
    g_norm = lax.dynamic_slice_in_dim(g_norm_full, chip * Ds, Ds, axis=2)
    g_conv_w = lax.dynamic_slice_in_dim(tot[42:45], chip * Ds, Ds, axis=1).reshape(1, 3, Ds)
    g_final = tot[45]
    g_pscale = tot[46:47, :pool_scale.shape[1]]
    g_poolw = tot[48:112].reshape(pool_w.shape)
    g_rpb = tot[112:116].reshape(-1)[:na_rpb.size].reshape(na_rpb.shape)

    pos = jnp.stack([chip, ci]).astype(jnp.int32)
    red = _reduce_scatter([dw13_0, dw13_1, dw13_2, dw13_3, dw2_0, dw2_1, dw2_2, dw2_3, dein, deout, dcin, dcout],
                          pos, "w")
    g_w13 = jnp.stack(red[0:4]).reshape(ffn_w13.shape)
    g_w2 = jnp.stack(red[4:8]).reshape(ffn_w2.shape)
    g_ein, g_eout, g_cin, g_cout = (red[8][None], red[9][None], red[10][None], red[11][None])

    grads = [g_c_ctx, g_mod_w, g_mod_b, g_norm, g_w13, g_w2, g_ein, g_eout, g_rpb, g_poolw, g_pscale, g_cin,
             g_conv_w, g_cout, g_final]
    weights = [c_ctx, mod_w, mod_b, norm_g, ffn_w13, ffn_w2, even_w_in, even_w_out, na_rpb, pool_w, pool_scale,
               conv_w_in, conv_w, conv_w_out, final_g]
    ms = [m_c_ctx, m_mod_w, m_mod_b, m_norm_g, m_ffn_w13, m_ffn_w2, m_even_w_in, m_even_w_out, m_na_rpb, m_pool_w,
          m_pool_scale, m_conv_w_in, m_conv_w, m_conv_w_out, m_final_g]
    vs = [v_c_ctx, v_mod_w, v_mod_b, v_norm_g, v_ffn_w13, v_ffn_w2, v_even_w_in, v_even_w_out, v_na_rpb, v_pool_w,
          v_pool_scale, v_conv_w_in, v_conv_w, v_conv_w_out, v_final_g]
    names = ["c_ctx", "mod_w", "mod_b", "norm_g", "ffn_w13", "ffn_w2", "even_w_in", "even_w_out", "na_rpb", "pool_w",
             "pool_scale", "conv_w_in", "conv_w", "conv_w_out", "final_g"]
    deltas, new_m, new_v = [], [], []
    for n, w, g, m, vv in zip(names, weights, grads, ms, vs):
        g = g.reshape(w.shape)
        if w.ndim == 1:
            d, mn, vn = (t.reshape(w.shape) for t in _adamw(w[None], g[None], m[None], vv[None], f"adamw_{n}"))
        else:
            d, mn, vn = _adamw(w, g, m, vv, f"adamw_{n}")
        deltas.append(d)
        new_m.append(mn)
        new_v.append(vn)
    grads = [g.reshape(w.shape) for g, w in zip(grads, weights)]
    return (loss, dx0[None], *grads, *deltas, *new_m, *new_v)


def _even_hn(x, prm, name, tm=512):
    L, D = x.shape
    tm = min(tm, L)

    def body(x_ref, p_ref, o_ref):
        hn, _, _, _ = _nm(x_ref[...], p_ref[3:4, :], p_ref[0:1, :], p_ref[1:2, :])
        o_ref[...] = hn.astype(BF16)

    return pl.pallas_call(
        body, name=name, grid=(L // tm,), in_specs=[_tile(tm, D), _full((8, D))], out_specs=_tile(tm, D),
        out_shape=jax.ShapeDtypeStruct((L, D), BF16))(x, prm)
```

```python
import jax
import jax.numpy as jnp
from jax import lax
from jax.experimental import pallas as pl
from jax.experimental.pallas import tpu as pltpu

F32 = jnp.float32
BF16 = jnp.bfloat16
MESH = pl.DeviceIdType.MESH

GRID_W = 64
NA_HEADS = 8
NA_HEAD_DIM = 64
NA_KH = 8
NA_KW = 16
GQ = 4
GK = GQ + NA_KH
NA_WIDTH = NA_HEADS * NA_HEAD_DIM
POOL_WINDOWS = (2, 4, 8, 16)
POOL_GROUP_DIM = 128
N_MOD = 9
RMS_EPS = 1e-6
NEG_INF = -1e30
ADAM_LR, ADAM_B1, ADAM_B2, ADAM_EPS, ADAM_WD, ADAM_STEP = 0.001, 0.9, 0.999, 1e-08, 0.01, 10

HALO = 16
VMEM_LIMIT = 56 * 1024 * 1024
N_CHIPS = 4
N_DEV = 8


def _dot(a, b):
    return jnp.dot(a, b, preferred_element_type=F32)


def _dot_nt(a, b):
    return lax.dot_general(a, b, (((1,), (1,)), ((), ())), preferred_element_type=F32)


def _dot_tn(a, b):
    return lax.dot_general(a, b, (((0,), (0,)), ((), ())), preferred_element_type=F32)


def _sigmoid(a):
    return 1.0 / (1.0 + jnp.exp(-a))


def _sum0(v):
    return jnp.sum(v, axis=0, keepdims=True)


def _nm(x, g, shift, scale):
    r = lax.rsqrt(jnp.mean(x * x, axis=-1, keepdims=True) + RMS_EPS)
    xhat = x * r
    nrm = xhat * g
    return nrm * (1.0 + scale) + shift, xhat, r, nrm


def _nm_bwd(dhn, xhat, r, nrm, g, scale):
    dshift = _sum0(dhn)
    dscale = _sum0(dhn * nrm)
    dnrm = dhn * (1.0 + scale)
    dgn = _sum0(dnrm * xhat)
    dxh = dnrm * g
    dx = r * (dxh - xhat * jnp.mean(dxh * xhat, axis=-1, keepdims=True))
    return dx, dshift, dscale, dgn


def _acc_rows(acc_ref, first, rows):
    @pl.when(first)
    def _():
        acc_ref[...] = jnp.zeros(acc_ref.shape, acc_ref.dtype)
    for k, row in enumerate(rows):
        if row is not None:
            acc_ref[k:k + 1, :] += row


def _shift_rows(v, k):
    n = v.shape[0]
    k = k % n
    return v if k == 0 else pltpu.roll(v, k, 0)


def _tile(tm, w):
    return pl.BlockSpec((tm, w), lambda i: (i, 0))


def _full(shape):
    nd = len(shape)
    return pl.BlockSpec(shape, lambda i: (0,) * nd)


def _resident(block, imap):
    return pl.BlockSpec(block, imap, pipeline_mode=pl.Buffered(1))


def _halo_prev(tm, w):
    return pl.BlockSpec((HALO, w), lambda i: (jnp.maximum(i * (tm // HALO) - 1, 0), 0))


def _halo_next(tm, w, L):
    return pl.BlockSpec((HALO, w), lambda i: (jnp.minimum((i + 1) * (tm // HALO), L // HALO - 1), 0))


def _params(vmem=VMEM_LIMIT):
    return pltpu.CompilerParams(vmem_limit_bytes=vmem)


def _pick_rows(rows, cols, itemsize=4, target=1 << 20):
    best = None
    for t in range(8, rows + 1, 8):
        if rows % t == 0 and t * cols * itemsize <= target:
            best = t
    return best if best is not None else rows


def _ext(prev, cur, nxt, i, nt):
    prev = jnp.where(i > 0, prev, jnp.zeros_like(prev))
    nxt = jnp.where(i < nt - 1, nxt, jnp.zeros_like(nxt))
    return jnp.concatenate([prev, cur, nxt], axis=0)


def _cast_bf16(a2d, name):
    rows, cols = a2d.shape
    tr = _pick_rows(rows, cols)

    def body(a_ref, o_ref):
        o_ref[...] = a_ref[...].astype(BF16)

    return pl.pallas_call(
        body, name=name, grid=(rows // tr,), in_specs=[_tile(tr, cols)], out_specs=_tile(tr, cols),
        out_shape=jax.ShapeDtypeStruct((rows, cols), BF16))(a2d)


def _sum_devices(g, name):
    n, rows, cols = g.shape
    tr = _pick_rows(rows, cols, target=1 << 18)

    def body(g_ref, o_ref):
        s = g_ref[0]
        for d in range(1, n):
            s = s + g_ref[d]
        o_ref[...] = s

    return pl.pallas_call(
        body, name=name, grid=(rows // tr,), in_specs=[pl.BlockSpec((n, tr, cols), lambda i: (0, i, 0))],
        out_specs=_tile(tr, cols), out_shape=jax.ShapeDtypeStruct((rows, cols), F32))(g)


def _adamw(w, g, m, v, name):
    shape = w.shape
    cols = shape[-1]
    rows = w.size // cols
    w2, g2, m2, v2 = (t.reshape(rows, cols) for t in (w, g, m, v))
    tr = _pick_rows(rows, cols)
    c1 = 1.0 - ADAM_B1 ** ADAM_STEP
    c2 = 1.0 - ADAM_B2 ** ADAM_STEP

    def body(w_ref, g_ref, m_ref, v_ref, d_ref, mo_ref, vo_ref):
        gg = g_ref[...]
        mn = ADAM_B1 * m_ref[...] + (1.0 - ADAM_B1) * gg
        vn = ADAM_B2 * v_ref[...] + (1.0 - ADAM_B2) * (gg * gg)
        d_ref[...] = -ADAM_LR * ((mn / c1) / (jnp.sqrt(vn / c2) + ADAM_EPS) + ADAM_WD * w_ref[...])
        mo_ref[...] = mn
        vo_ref[...] = vn

    outs = pl.pallas_call(
        body, name=name, grid=(rows // tr,), in_specs=[_tile(tr, cols)] * 4, out_specs=[_tile(tr, cols)] * 3,
        out_shape=[jax.ShapeDtypeStruct((rows, cols), F32)] * 3)(w2, g2, m2, v2)
    return tuple(o.reshape(shape) for o in outs)


def _mesh_pos():
    x, y, c = lax.axis_index("x"), lax.axis_index("y"), lax.axis_index("c")
    chips = [(1 - x, y), (x, 1 - y), (1 - x, 1 - y)]
    return x, y, c, chips


def _hbm_specs(n):
    return [pl.BlockSpec(memory_space=pltpu.HBM)] * n


def _small_all_gather(v, name):
    rows, w = v.shape

    def body(x_ref, out_ref, send_sems, recv_sems, local_sem):
        x, y, c, chips = _mesh_pos()
        me, sibling = (x, y, c), (x, y, 1 - c)

        def blk(px, py, pc):
            return out_ref.at[4 * px + 2 * py + pc]

        def copy(k, block, to, src=None):
            return pltpu.make_async_remote_copy(
                src_ref=blk(*block) if src is None else src, dst_ref=blk(*block),
                send_sem=send_sems.at[k], recv_sem=recv_sems.at[k], device_id=to, device_id_type=MESH)

        mine = pltpu.make_async_copy(x_ref, blk(*me), local_sem)
        mine.start()
        first = [copy(0, me, sibling, src=x_ref)]
        first += [copy(1 + j, me, (*chip, c), src=x_ref) for j, chip in enumerate(chips)]
        for cp in first:
            cp.start()
        passed = [copy(4 + j, (*chip, c), sibling) for j, chip in enumerate(chips)]
        for j, chip in enumerate(chips):
            copy(1 + j, (*chip, c), me).wait_recv()
            passed[j].start()
        copy(0, sibling, me).wait_recv()
        for j, chip in enumerate(chips):
            copy(4 + j, (*chip, 1 - c), me).wait_recv()
        for cp in first + passed:
            cp.wait_send()
        mine.wait()

    return pl.pallas_call(
        body, name=name, out_shape=jax.ShapeDtypeStruct((N_DEV, rows, w), v.dtype),
        in_specs=[pl.BlockSpec(memory_space=pltpu.VMEM)], out_specs=pl.BlockSpec(memory_space=pltpu.VMEM),
        scratch_shapes=[pltpu.SemaphoreType.DMA((7,)), pltpu.SemaphoreType.DMA((7,)), pltpu.SemaphoreType.DMA],
    )(v)


def _gather_shards(shards, name):
    n = len(shards)

    def body(*refs):
        ins, outs = refs[:n], refs[n:2 * n]
        send_sems, recv_sems, local_sems = refs[2 * n:]
        x, y, c, chips = _mesh_pos()
        k = 2 * x + y
        sibling = (x, y, 1 - c)

        def window(t, chip_k, half):
            r = ins[t].shape[1]
            return outs[t].at[:, pl.ds(chip_k * r + half * (r // 2), r // 2), :]

        def copy(t, j, chip_k, half, to, src=None):
            return pltpu.make_async_remote_copy(
                src_ref=window(t, chip_k, half) if src is None else src, dst_ref=window(t, chip_k, half),
                send_sem=send_sems.at[6 * t + j], recv_sem=recv_sems.at[6 * t + j], device_id=to, device_id_type=MESH)

        started, local = [], []
        for t in range(n):
            r = ins[t].shape[1]
            lc = pltpu.make_async_copy(ins[t], outs[t].at[:, pl.ds(k * r, r), :], local_sems.at[t])
            lc.start()
            local.append(lc)
            src = ins[t].at[:, pl.ds(c * (r // 2), r // 2), :]
            for j, chip in enumerate(chips):
                cp = copy(t, j, k, c, (*chip, c), src=src)
                cp.start()
                started.append(cp)
        for t in range(n):
            for j, chip in enumerate(chips):
                kj = 2 * chip[0] + chip[1]
                copy(t, j, kj, c, sibling).wait_recv()
                cp = copy(t, 3 + j, kj, c, sibling)
                cp.start()
                started.append(cp)
        for t in range(n):
            for j, chip in enumerate(chips):
                kj = 2 * chip[0] + chip[1]
                copy(t, 3 + j, kj, 1 - c, sibling).wait_recv()
        for cp in started:
            cp.wait_send()
        for lc in local:
            lc.wait()

    out_shape = [jax.ShapeDtypeStruct((s.shape[0], N_CHIPS * s.shape[1], s.shape[2]), s.dtype) for s in shards]
    return pl.pallas_call(
        body, name=name, out_shape=out_shape, in_specs=_hbm_specs(n), out_specs=_hbm_specs(n),
        scratch_shapes=[pltpu.SemaphoreType.DMA((6 * n,)), pltpu.SemaphoreType.DMA((6 * n,)),
                        pltpu.SemaphoreType.DMA((n,))],
    )(*shards)


def _chunk_rows(h, w):
    best = 16
    for t in range(16, h + 1, 16):
        if h % t == 0 and t * w * 4 <= (1 << 20):
            best = t
    return best


def _pair_sum(part, pos, name):
    _, h, w = part.shape
    cr = _chunk_rows(h, w)
    nc = h // cr

    def body(pos_ref, own_ref, send_ref, s_ref, sb_ref, rbuf, send_sems, recv_sems):
        x, y, c, _ = _mesh_pos()
        slot = pl.program_id(0) % 2
        cp = pltpu.make_async_remote_copy(
            src_ref=send_ref, dst_ref=rbuf.at[slot], send_sem=send_sems.at[slot], recv_sem=recv_sems.at[slot],
            device_id=(x, y, 1 - c), device_id_type=MESH)
        cp.start()
        cp.wait_recv()
        s = own_ref[...] + rbuf[slot]
        s_ref[...] = s
        sb_ref[...] = s.astype(BF16)
        cp.wait_send()

    grid_spec = pltpu.PrefetchScalarGridSpec(
        num_scalar_prefetch=1, grid=(4 * nc,),
        in_specs=[pl.BlockSpec((cr, w), lambda k, p: ((2 * (k // nc) + p[1]) * nc + k % nc, 0)),
                  pl.BlockSpec((cr, w), lambda k, p: ((2 * (k // nc) + 1 - p[1]) * nc + k % nc, 0))],
        out_specs=[pl.BlockSpec((cr, w), lambda k, p: (k, 0))] * 2,
        scratch_shapes=[pltpu.VMEM((2, cr, w), F32), pltpu.SemaphoreType.DMA((2,)), pltpu.SemaphoreType.DMA((2,))])
    part2 = part.reshape(8 * h, w)
    s, sb = pl.pallas_call(
        body, name=name, grid_spec=grid_spec,
        out_shape=[jax.ShapeDtypeStruct((4 * h, w), F32), jax.ShapeDtypeStruct((4 * h, w), BF16)],
    )(pos, part2, part2)
    return s.reshape(4, h, w), sb.reshape(4, h, w)


def _scatter_to_chips(sums_bf16, name):
    n = len(sums_bf16)

    def body(*refs):
        b16s, got = refs[:n], refs[n:2 * n]
        send_sems, recv_sems = refs[2 * n:]
        x, y, c, chips = _mesh_pos()
        started = []
        for t in range(n):
            for j, chip in enumerate(chips):
                cp = pltpu.make_async_remote_copy(
                    src_ref=b16s[t].at[2 * chip[0] + chip[1]], dst_ref=got[t].at[j],
                    send_sem=send_sems.at[3 * t + j], recv_sem=recv_sems.at[3 * t + j],
                    device_id=(*chip, c), device_id_type=MESH)
                cp.start()
                started.append(cp)
        for cp in started:
            cp.wait_recv()
        for cp in started:
            cp.wait_send()

    got_shape = [jax.ShapeDtypeStruct((3,) + s.shape[1:], BF16) for s in sums_bf16]
    return pl.pallas_call(
        body, name=name, out_shape=got_shape, in_specs=_hbm_specs(n), out_specs=_hbm_specs(n),
        scratch_shapes=[pltpu.SemaphoreType.DMA((3 * n,)), pltpu.SemaphoreType.DMA((3 * n,))],
    )(*sums_bf16)


def _sum_and_join(sums, got, pos, name):
    _, h, w = sums.shape
    cr = _chunk_rows(h, w)

    def body(pos_ref, mine_ref, got_ref, o_ref, ebuf, rbuf, send_sems, recv_sems):
        x, y, c, _ = _mesh_pos()
        slot = pl.program_id(0) % 2
        e = mine_ref[...]
        for j in range(3):
            e = e + got_ref[j].astype(F32)
        ebuf[slot] = e
        cp = pltpu.make_async_remote_copy(
            src_ref=ebuf.at[slot], dst_ref=rbuf.at[slot], send_sem=send_sems.at[slot], recv_sem=recv_sems.at[slot],
            device_id=(x, y, 1 - c), device_id_type=MESH)
        cp.start()
        o_ref[pos_ref[1]] = e
        cp.wait_recv()
        o_ref[1 - pos_ref[1]] = rbuf[slot]
        cp.wait_send()

    grid_spec = pltpu.PrefetchScalarGridSpec(
        num_scalar_prefetch=1, grid=(h // cr,),
        in_specs=[pl.BlockSpec((None, cr, w), lambda i, p: (p[0], i, 0)),
                  pl.BlockSpec((3, cr, w), lambda i, p: (0, i, 0))],
        out_specs=pl.BlockSpec((2, cr, w), lambda i, p: (0, i, 0)),
        scratch_shapes=[pltpu.VMEM((2, cr, w), F32), pltpu.VMEM((2, cr, w), F32),
                        pltpu.SemaphoreType.DMA((2,)), pltpu.SemaphoreType.DMA((2,))])
    return pl.pallas_call(
        body, name=name, grid_spec=grid_spec, out_shape=jax.ShapeDtypeStruct((2, h, w), F32),
    )(pos, sums, got)


def _reduce_scatter(parts, pos, tag):
    sums, sums_b = [], []
    for t, p in enumerate(parts):
        h = p.shape[0] // 8
        s, sb = _pair_sum(p.reshape(8, h, p.shape[1]), pos, f"rs_pair_{tag}_{t}")
        sums.append(s)
        sums_b.append(sb)
    recv = _scatter_to_chips(sums_b, f"rs_scatter_{tag}")
    out = []
    for t, (s, r) in enumerate(zip(sums, recv)):
        full = _sum_and_join(s, r, pos, f"rs_join_{tag}_{t}")
        out.append(full.reshape(2 * full.shape[1], full.shape[2]))
    return out


def _ffn_fwd(x, prm, w13g, w2g, t, name, tm=512):
    L, D = x.shape
    Fh = w13g.shape[-1]
    tm = min(tm, L)

    def body(x_ref, p_ref, w13_ref, w2_ref, xo_ref, ab_ref, y_ref):
        xv = x_ref[...]
        hn, _, _, _ = _nm(xv, p_ref[3:4, :], p_ref[0:1, :], p_ref[1:2, :])
        hb = hn.astype(BF16)
        acc = jnp.zeros((tm, D), F32)
        for j in range(2):
            a = _dot(hb, w13_ref[j])
            b = _dot(hb, w13_ref[2 + j])
            ab_ref[:, j * Fh:(j + 1) * Fh] = a.astype(BF16)
            ab_ref[:, (2 + j) * Fh:(3 + j) * Fh] = b.astype(BF16)
            g = (a * _sigmoid(a) * b).astype(BF16)
            acc = acc + _dot(g, w2_ref[j * Fh:(j + 1) * Fh, :])
        y_ref[...] = acc.astype(BF16)
        xo_ref[...] = xv + (0.5 * p_ref[2:3, :]) * acc

    return pl.pallas_call(
        body, name=name, grid=(L // tm,),
        in_specs=[_tile(tm, D), _full((8, D)),
                  _resident((None, 4, D, Fh), lambda i: (t, 0, 0, 0)),
                  _resident((None, 2 * Fh, D), lambda i: (t, 0, 0))],
        out_specs=[_tile(tm, D), _tile(tm, 4 * Fh), _tile(tm, D)],
        out_shape=[jax.ShapeDtypeStruct((L, D), F32), jax.ShapeDtypeStruct((L, 4 * Fh), BF16),
                   jax.ShapeDtypeStruct((L, D), BF16)],
        compiler_params=_params())(x, prm, w13g, w2g)


def _ffn_bwd(dout, x, ab, y, prm, w13g, w2g, t, name, tm=256):
    L, D = x.shape
    Fh = w13g.shape[-1]
    tm = min(tm, L)

    def body(do_ref, x_ref, ab_ref, y_ref, p_ref, w13_ref, w2_ref, dx_ref, dab_ref, g_ref, dy_ref, hn_ref, acc_ref):
        i = pl.program_id(0)
        do = do_ref[...]
        gain, shift, scale, gate = p_ref[3:4, :], p_ref[0:1, :], p_ref[1:2, :], p_ref[2:3, :]
        hn, xhat, r, nrm = _nm(x_ref[...], gain, shift, scale)
        hn_ref[...] = hn.astype(BF16)
        dgate = 0.5 * _sum0(do * y_ref[...].astype(F32))
        dyb = ((0.5 * gate) * do).astype(BF16)
        dy_ref[...] = dyb
        dhn = jnp.zeros((tm, D), F32)
        for j in range(2):
            dg = _dot_nt(dyb, w2_ref[j * Fh:(j + 1) * Fh, :])
            a = ab_ref[:, j * Fh:(j + 1) * Fh].astype(F32)
            b = ab_ref[:, (2 + j) * Fh:(3 + j) * Fh].astype(F32)
            sg = _sigmoid(a)
            sa = a * sg
            g_ref[:, j * Fh:(j + 1) * Fh] = (sa * b).astype(BF16)
            da = (dg * b * (sg * (1.0 + a * (1.0 - sg)))).astype(BF16)
            db = (dg * sa).astype(BF16)
            dab_ref[:, j * Fh:(j + 1) * Fh] = da
            dab_ref[:, (2 + j) * Fh:(3 + j) * Fh] = db
            dhn = dhn + _dot_nt(da, w13_ref[j]) + _dot_nt(db, w13_ref[2 + j])
        dx, dshift, dscale, dgn = _nm_bwd(dhn, xhat, r, nrm, gain, scale)
        dx_ref[...] = do + dx
        _acc_rows(acc_ref, i == 0, [dshift, dscale, dgate, dgn])

    return pl.pallas_call(
        body, name=name, grid=(L // tm,),
        in_specs=[_tile(tm, D), _tile(tm, D), _tile(tm, 4 * Fh), _tile(tm, D), _full((8, D)),
                  _resident((None, 4, D, Fh), lambda i: (t, 0, 0, 0)),
                  _resident((None, 2 * Fh, D), lambda i: (t, 0, 0))],
        out_specs=[_tile(tm, D), _tile(tm, 4 * Fh), _tile(tm, 2 * Fh), _tile(tm, D), _tile(tm, D), _full((8, D))],
        out_shape=[jax.ShapeDtypeStruct((L, D), F32), jax.ShapeDtypeStruct((L, 4 * Fh), BF16),
                   jax.ShapeDtypeStruct((L, 2 * Fh), BF16), jax.ShapeDtypeStruct((L, D), BF16),
                   jax.ShapeDtypeStruct((L, D), BF16), jax.ShapeDtypeStruct((8, D), F32)],
        compiler_params=_params())(dout, x, ab, y, prm, w13g, w2g)


def _mm_tn(a, b, slabs, a_slabbed, name, init=None, tl=512):
    L = a.shape[0]
    ka = a.shape[1] // slabs if a_slabbed else a.shape[1]
    nb = b.shape[1] if a_slabbed else b.shape[1] // slabs
    tl = min(tl, L)
    has_init = init is not None

    def body(a_ref, b_ref, *rest):
        o_ref = rest[-1]
        step = pl.program_id(1)

        @pl.when(step == 0)
        def _():
            o_ref[...] = rest[0][...] if has_init else jnp.zeros((ka, nb), F32)

        o_ref[...] += _dot_tn(a_ref[...], b_ref[...])

    in_specs = [pl.BlockSpec((tl, ka), (lambda s, l: (l, s)) if a_slabbed else (lambda s, l: (l, 0))),
                pl.BlockSpec((tl, nb), (lambda s, l: (l, 0)) if a_slabbed else (lambda s, l: (l, s)))]
    args = [a, b]
    if has_init:
        in_specs.append(pl.BlockSpec((ka, nb), lambda s, l: (s, 0)))
        args.append(init)
    return pl.pallas_call(
        body, name=name, grid=(slabs, L // tl), in_specs=in_specs,
        out_specs=pl.BlockSpec((ka, nb), lambda s, l: (s, 0)),
        out_shape=jax.ShapeDtypeStruct((slabs * ka, nb), F32), compiler_params=_params())(*args)


def _even_in_fwd(x, prm, wing, name, tm=512):
    L, D = x.shape
    W = wing.shape[-1]
    tm = min(tm, L)

    def body(x_ref, p_ref, w_ref, q_ref, k_ref, v_ref, u_ref, hn_ref):
        hn, _, _, _ = _nm(x_ref[...], p_ref[3:4, :], p_ref[0:1, :], p_ref[1:2, :])
        hb = hn.astype(BF16)
        hn_ref[...] = hb
        q_ref[...] = _dot(hb, w_ref[0]).astype(BF16)
        k_ref[...] = _dot(hb, w_ref[1]).astype(BF16)
        v_ref[...] = _dot(hb, w_ref[2]).astype(BF16)
        u_ref[...] = _dot(hb, w_ref[3])

    return pl.pallas_call(
        body, name=name, grid=(L // tm,),
        in_specs=[_tile(tm, D), _full((8, D)), _resident((None, 4, D, W), lambda i: (0, 0, 0, 0))],
        out_specs=[_tile(tm, W)] * 4 + [_tile(tm, D)],
        out_shape=[jax.ShapeDtypeStruct((L, W), BF16)] * 3 + [jax.ShapeDtypeStruct((L, W), F32),
                                                              jax.ShapeDtypeStruct((L, D), BF16)],
        compiler_params=_params())(x, prm, wing)


def _even_in_bwd(dout, x, dq, dk, dv, du, prm, wing, name, tm=256):
    L, D = x.shape
    W = wing.shape[-1]
    tm = min(tm, L)

    def body(do_ref, x_ref, dq_ref, dk_ref, dv_ref, du_ref, p_ref, w_ref, dx_ref, ds_ref, acc_ref):
        i = pl.program_id(0)
        gain, shift, scale = p_ref[3:4, :], p_ref[0:1, :], p_ref[1:2, :]
        _, xhat, r, nrm = _nm(x_ref[...], gain, shift, scale)
        dhn = jnp.zeros((tm, D), F32)
        for s, ref in enumerate((dq_ref, dk_ref, dv_ref, du_ref)):
            d = ref[...].astype(BF16)
            ds_ref[:, s * W:(s + 1) * W] = d
            dhn = dhn + _dot_nt(d, w_ref[s])
        dx, dshift, dscale, dgn = _nm_bwd(dhn, xhat, r, nrm, gain, scale)
        dx_ref[...] = do_ref[...] + dx
        _acc_rows(acc_ref, i == 0, [dshift, dscale, None, dgn])

    return pl.pallas_call(
        body, name=name, grid=(L // tm,),
        in_specs=[_tile(tm, D), _tile(tm, D)] + [_tile(tm, W)] * 4 +
                 [_full((8, D)), _resident((None, 4, D, W), lambda i: (0, 0, 0, 0))],
        out_specs=[_tile(tm, D), _tile(tm, 4 * W), _full((8, D))],
        out_shape=[jax.ShapeDtypeStruct((L, D), F32), jax.ShapeDtypeStruct((L, 4 * W), BF16),
                   jax.ShapeDtypeStruct((8, D), F32)],
        compiler_params=_params())(dout, x, dq, dk, dv, du, prm, wing)


def _even_out_fwd(x, att, pool, prm, woutg, name, tm=512):
    L, D = x.shape
    W = D // 2
    tm = min(tm, L)

    def body(x_ref, a_ref, p_ref, prm_ref, w_ref, xo_ref, y_ref):
        yv = _dot(a_ref[...], w_ref[0:W, :]) + _dot(p_ref[...], w_ref[W:2 * W, :])
        y_ref[...] = yv.astype(BF16)
        xo_ref[...] = x_ref[...] + prm_ref[2:3, :] * yv

    return pl.pallas_call(
        body, name=name, grid=(L // tm,),
        in_specs=[_tile(tm, D), _tile(tm, W), _tile(tm, W), _full((8, D)),
                  _resident((None, D, D), lambda i: (0, 0, 0))],
        out_specs=[_tile(tm, D), _tile(tm, D)],
        out_shape=[jax.ShapeDtypeStruct((L, D), F32), jax.ShapeDtypeStruct((L, D), BF16)],
        compiler_params=_params())(x, att, pool, prm, woutg)


def _even_out_bwd(dout, y, prm, woutg, name, tm=512):
    L, D = dout.shape
    W = D // 2
    tm = min(tm, L)

    def body(do_ref, y_ref, p_ref, w_ref, dy_ref, da_ref, dp_ref, acc_ref):
        i = pl.program_id(0)
        do = do_ref[...]
        dgate = _sum0(do * y_ref[...].astype(F32))
        dyb = (p_ref[2:3, :] * do).astype(BF16)
        dy_ref[...] = dyb
        da_ref[...] = _dot_nt(dyb, w_ref[0:W, :]).astype(BF16)
        dp_ref[...] = _dot_nt(dyb, w_ref[W:2 * W, :])
        _acc_rows(acc_ref, i == 0, [None, None, dgate])

    return pl.pallas_call(
        body, name=name, grid=(L // tm,),
        in_specs=[_tile(tm, D), _tile(tm, D), _full((8, D)), _resident((None, D, D), lambda i: (0, 0, 0))],
        out_specs=[_tile(tm, D), _tile(tm, W), _tile(tm, W), _full((8, D))],
        out_shape=[jax.ShapeDtypeStruct((L, D), BF16), jax.ShapeDtypeStruct((L, W), BF16),
                   jax.ShapeDtypeStruct((L, W), F32), jax.ShapeDtypeStruct((8, D), F32)],
        compiler_params=_params())(dout, y, prm, woutg)


def _group_ri(variant, qr, kr):
    first_key = (0, qr, GK - NA_KH)[variant]
    if not first_key <= kr < first_key + NA_KH:
        return None
    return kr - qr + (NA_KH - 1, NA_KH - 1 - NA_KH // 2, NA_KH - 1 - (GK - GQ))[variant]


def _bias_table(rpb, name):
    H = rpb.shape[0]
    nri, nci = 2 * NA_KH - 1, 2 * NA_KW - 1
    col = jnp.arange(GRID_W)
    rel = (col[None, :] - col[:, None] + (NA_KW - 1)).reshape(1, -1)
    onehot = (rel == jnp.arange(32)[:, None]).astype(F32)
    cs = jnp.clip(col - NA_KW // 2, 0, GRID_W - NA_KW)
    ok = ((col[None, :] >= cs[:, None]) & (col[None, :] < cs[:, None] + NA_KW)).astype(F32).reshape(1, -1)
    rpb2 = jnp.pad(rpb.reshape(H * nri, nci), ((0, 0), (0, 32 - nci)))

    def body(r_ref, e_ref, m_ref, o_ref):
        t = jnp.dot(r_ref[...], e_ref[...], preferred_element_type=F32, precision=lax.Precision.HIGHEST)
        o_ref[...] = jnp.where(m_ref[...] > 0.0, t, NEG_INF)

    tab = pl.pallas_call(body, name=name, out_shape=jax.ShapeDtypeStruct((H * nri, GRID_W * GRID_W), F32))(
        rpb2, onehot, ok)
    tab = tab.reshape(H, nri, GRID_W, GRID_W)
    outside = jnp.full((H, GRID_W, GRID_W), NEG_INF, F32)
    variants = []
    for variant in range(3):
        rows = []
        for qr in range(GQ):
            ris = [_group_ri(variant, qr, kr) for kr in range(GK)]
            rows.append(jnp.concatenate([outside if ri is None else tab[:, ri] for ri in ris], axis=2))
        variants.append(jnp.concatenate(rows, axis=1))
    return jnp.stack(variants, axis=1)


def _attn_probs(q, kw, kc, bias, scale):
    s_w = _dot_nt(q, kw) * scale + bias
    s_c = _dot_nt(q, kc) * scale
    m = jnp.maximum(jnp.max(s_w, axis=-1, keepdims=True), jnp.max(s_c, axis=-1, keepdims=True))
    e_w = jnp.exp(s_w - m)
    e_c = jnp.exp(s_c - m)
    inv = 1.0 / (jnp.sum(e_w, axis=-1, keepdims=True) + jnp.sum(e_c, axis=-1, keepdims=True))
    return e_w * inv, e_c * inv


def _group_place(g, R):
    G = R // GQ
    kb = jnp.clip(g * GQ - NA_KH // 2, 0, R - GK)
    variant = jnp.where(g == 0, 0, jnp.where(g == G - 1, 2, 1))
    return pl.multiple_of(g * (GQ * GRID_W), GQ * GRID_W), pl.multiple_of(kb * GRID_W, GRID_W), variant


def _lane_masks(width, dh):
    lane = lax.broadcasted_iota(jnp.int32, (1, width), 1)
    return [(lane >= h * dh) & (lane < (h + 1) * dh) for h in range(width // dh)]


def _only(mask, a):
    return jnp.where(mask, a, jnp.zeros_like(a))


def _attn_fwd(q, k, v, kc, vc, bias, name):
    L, width = q.shape
    C = kc.shape[0]
    dh = NA_HEAD_DIM
    lanes = 128
    hpb = lanes // dh
    R = L // GRID_W
    nq, nk = GQ * GRID_W, GK * GRID_W
    scale = dh ** -0.5

    def body(q_ref, k_ref, v_ref, kc_ref, vc_ref, b_ref, o_ref):
        masks = _lane_masks(lanes, dh)
        kc2 = kc_ref[...]
        vcs = [_only(m, vc_ref[...]) for m in masks]

        def group(g, carry):
            q0, k0, variant = _group_place(g, R)
            q2 = q_ref[pl.ds(q0, nq), :]
            k2 = k_ref[pl.ds(k0, nk), :]
            v2 = v_ref[pl.ds(k0, nk), :]
            o2 = jnp.zeros((nq, lanes), F32)
            for h, m in enumerate(masks):
                p_w, p_c = _attn_probs(_only(m, q2), k2, kc2, b_ref[h, variant], scale)
                o2 = o2 + _dot(p_w.astype(BF16), _only(m, v2)) + _dot(p_c.astype(BF16), vcs[h])
            o_ref[pl.ds(q0, nq), :] = o2.astype(BF16)
            return carry

        lax.fori_loop(0, R // GQ, group, 0)

    cols = lambda n: pl.BlockSpec((n, lanes), lambda p: (0, p))
    return pl.pallas_call(
        body, name=name, grid=(width // lanes,),
        in_specs=[cols(L), cols(L), cols(L), cols(C), cols(C),
                  pl.BlockSpec((hpb, 3, nq, nk), lambda p: (p, 0, 0, 0))],
        out_specs=cols(L), out_shape=jax.ShapeDtypeStruct((L, width), BF16),
        compiler_params=_params())(q, k, v, kc, vc, bias)


def _attn_bwd(q, k, v, kc, vc, bias, do, name):
    L, width = q.shape
    C = kc.shape[0]
    dh = NA_HEAD_DIM
    lanes = 128
    hpb = lanes // dh
    R = L // GRID_W
    nq, nk = GQ * GRID_W, GK * GRID_W
    scale = dh ** -0.5

    def body(q_ref, k_ref, v_ref, kc_ref, vc_ref, b_ref, do_ref, dq_ref, dk_ref, dv_ref, dkc_ref, dvc_ref, db_ref):
        masks = _lane_masks(lanes, dh)
        kc2 = kc_ref[...]
        vc2 = vc_ref[...]
        kcs = [_only(m, kc2) for m in masks]
        dk_ref[...] = jnp.zeros((L, lanes), F32)
        dv_ref[...] = jnp.zeros((L, lanes), F32)
        dkc_ref[...] = jnp.zeros((C, lanes), F32)
        dvc_ref[...] = jnp.zeros((C, lanes), F32)
        db_ref[...] = jnp.zeros((hpb, 3, nq, nk), F32)

        def group(g, carry):
            q0, k0, variant = _group_place(g, R)
            q2 = q_ref[pl.ds(q0, nq), :]
            k2 = k_ref[pl.ds(k0, nk), :]
            v2 = v_ref[pl.ds(k0, nk), :]
            do2 = do_ref[pl.ds(q0, nq), :]
            dq2 = jnp.zeros((nq, lanes), F32)
            dk2 = jnp.zeros((nk, lanes), F32)
            dv2 = jnp.zeros((nk, lanes), F32)
            for h, m in enumerate(masks):
                qh = _only(m, q2)
                doh = _only(m, do2)
                p_w, p_c = _attn_probs(qh, k2, kc2, b_ref[h, variant], scale)
                dp_w = _dot_nt(doh, v2)
                dp_c = _dot_nt(doh, vc2)
                delta = jnp.sum(p_w * dp_w, axis=-1, keepdims=True) + jnp.sum(p_c * dp_c, axis=-1, keepdims=True)
                ds_w = p_w * (dp_w - delta)
                ds_c = p_c * (dp_c - delta)
                db_ref[h, variant] += ds_w
                dsw = (ds_w * scale).astype(BF16)
                dsc = (ds_c * scale).astype(BF16)
                dq2 = dq2 + _dot(dsw, _only(m, k2)) + _dot(dsc, kcs[h])
                dk2 = dk2 + _dot_tn(dsw, qh)
                dv2 = dv2 + _dot_tn(p_w.astype(BF16), doh)
                dkc_ref[...] += _dot_tn(dsc, qh)
                dvc_ref[...] += _dot_tn(p_c.astype(BF16), doh)
            dq_ref[pl.ds(q0, nq), :] = dq2.astype(BF16)
            dk_ref[pl.ds(k0, nk), :] += dk2
            dv_ref[pl.ds(k0, nk), :] += dv2
            return carry

        lax.fori_loop(0, R // GQ, group, 0)

    cols = lambda n: _resident((n, lanes), lambda p: (0, p))
    bspec = _resident((hpb, 3, nq, nk), lambda p: (p, 0, 0, 0))
    return pl.pallas_call(
        body, name=name, grid=(width // lanes,),
        in_specs=[cols(L), cols(L), cols(L), cols(C), cols(C), bspec, cols(L)],
        out_specs=[cols(L), cols(L), cols(L), cols(C), cols(C), bspec],
        out_shape=[jax.ShapeDtypeStruct((L, width), BF16)] + [jax.ShapeDtypeStruct((L, width), F32)] * 2 +
                  [jax.ShapeDtypeStruct((C, width), F32)] * 2 +
                  [jax.ShapeDtypeStruct((width // dh, 3, nq, nk), F32)],
        compiler_params=_params())(q, k, v, kc, vc, bias, do)


def _rpb_grad(dbias, name):
    H = dbias.shape[0]
    nri, nci = 2 * NA_KH - 1, 2 * NA_KW - 1
    d6 = dbias.reshape(H, 3, GQ, GRID_W, GK, GRID_W).transpose(0, 1, 2, 4, 3, 5)
    col = jnp.arange(GRID_W)
    onehot = (col[None, None, :] - col[None, :, None] + (NA_KW - 1) == jnp.arange(32)[:, None, None]).astype(F32)
    places = [(v, qr, kr) for v in range(3) for qr in range(GQ) for kr in range(GK)]

    def body(d_ref, m_ref, o_ref, t_ref):
        t_ref[...] = jnp.zeros((32, GRID_W), F32)
        o_ref[...] = jnp.zeros((16, 32, 128), F32)
        for ri in range(nri):
            a = None
            for place in places:
                if _group_ri(*place) == ri:
                    blk = d_ref[place]
                    a = blk if a is None else a + blk
            for ci in range(nci):
                t_ref[ci:ci + 1, :] = _sum0(a * m_ref[ci])
            o_ref[ri] = jnp.broadcast_to(jnp.sum(t_ref[...], axis=1, keepdims=True), (32, 128))

    out = pl.pallas_call(
        body, name=name, grid=(H,),
        in_specs=[pl.BlockSpec((None, 3, GQ, GK, GRID_W, GRID_W), lambda h: (h, 0, 0, 0, 0, 0)),
                  pl.BlockSpec((32, GRID_W, GRID_W), lambda h: (0, 0, 0))],
        out_specs=pl.BlockSpec((None, 16, 32, 128), lambda h: (h, 0, 0, 0)),
        out_shape=jax.ShapeDtypeStruct((H, 16, 32, 128), F32),
        scratch_shapes=[pltpu.VMEM((32, GRID_W), F32)])(d6, onehot)
    return out[:, :nri, :nci, 0]


def _window_count(t, w, L):
    lo = jnp.clip(t - w // 2, 0, L)
    hi = jnp.clip(t - w // 2 + w, 0, L)
    return jnp.maximum(hi - lo, 1).astype(F32)


def _running_sum(v, w):
    k = 1
    while k < w:
        v = v + _shift_rows(v, k)
        k *= 2
    return v


def _pool_fwd(u, poolw, pscale, name, tm=512):
    L, W = u.shape
    G = POOL_GROUP_DIM
    tm = min(tm, L)
    nt = L // tm

    def body(c_ref, p_ref, n_ref, w_ref, s_ref, o_ref, dm_ref):
        i = pl.program_id(0)
        ext = _ext(p_ref[...], c_ref[...], n_ref[...], i, nt)
        t = i * tm + lax.broadcasted_iota(jnp.int32, (tm, 1), 0)
        for g, w in enumerate(POOL_WINDOWS):
            e = ext[:, g * G:(g + 1) * G]
            win = _shift_rows(_running_sum(e, w), -(w // 2 - 1))[HALO:HALO + tm]
            dmx = (win / _window_count(t, w, L) - e[HALO:HALO + tm]).astype(BF16)
            dm_ref[:, g * G:(g + 1) * G] = dmx
            o_ref[:, g * G:(g + 1) * G] = (_dot(dmx, w_ref[g]) * s_ref[:, g * G:(g + 1) * G]).astype(BF16)

    return pl.pallas_call(
        body, name=name, grid=(nt,),
        in_specs=[_tile(tm, W), _halo_prev(tm, W), _halo_next(tm, W, L), _full((4, G, G)), _full((1, W))],
        out_specs=[_tile(tm, W), _tile(tm, W)],
        out_shape=[jax.ShapeDtypeStruct((L, W), BF16)] * 2, compiler_params=_params())(u, u, u, poolw, pscale)


def _pool_bwd(dpool, dmx, poolw, pscale, name, tm=512):
    L, W = dpool.shape
    G = POOL_GROUP_DIM
    tm = min(tm, L)
    nt = L // tm

    def body(c_ref, p_ref, n_ref, dm_ref, w_ref, s_ref, du_ref, dw_ref, acc_ref):
        i = pl.program_id(0)
        ext = _ext(p_ref[...], c_ref[...], n_ref[...], i, nt)
        te = i * tm - HALO + lax.broadcasted_iota(jnp.int32, (tm + 2 * HALO, 1), 0)

        @pl.when(i == 0)
        def _():
            dw_ref[...] = jnp.zeros((4 * G, G), F32)

        rows = []
        for g, w in enumerate(POOL_WINDOWS):
            sc = s_ref[:, g * G:(g + 1) * G]
            dpre = (ext[:, g * G:(g + 1) * G] * sc).astype(BF16)
            dd = _dot_nt(dpre, w_ref[g])
            spread = _shift_rows(_running_sum(dd / _window_count(te, w, L), w), -(w // 2))
            du_ref[:, g * G:(g + 1) * G] = (spread - dd)[HALO:HALO + tm]
            dmx_g = dm_ref[:, g * G:(g + 1) * G]
            rows.append(_sum0(c_ref[:, g * G:(g + 1) * G] * _dot(dmx_g, w_ref[g])))
            dw_ref[g * G:(g + 1) * G, :] += _dot_tn(dmx_g, dpre[HALO:HALO + tm])
        _acc_rows(acc_ref, i == 0, [jnp.concatenate(rows, axis=1)])

    return pl.pallas_call(
        body, name=name, grid=(nt,),
        in_specs=[_tile(tm, W), _halo_prev(tm, W), _halo_next(tm, W, L), _tile(tm, W), _full((4, G, G)),
                  _full((1, W))],
        out_specs=[_tile(tm, W), _full((4 * G, G)), _full((8, W))],
        out_shape=[jax.ShapeDtypeStruct((L, W), F32), jax.ShapeDtypeStruct((4 * G, G), F32),
                   jax.ShapeDtypeStruct((8, W), F32)],
        compiler_params=_params())(dpool, dpool, dpool, dmx, poolw, pscale)


def _conv3(z, cw):
    return _shift_rows(z, 1) * cw[0] + z * cw[1] + _shift_rows(z, -1) * cw[2]


def _conv_fwd(x, prm, wing, woutg, name, tm=512):
    L, D = x.shape
    Ws = wing.shape[-1]
    tm = min(tm, L)
    nt = L // tm
    te = tm + 2 * HALO

    def body(c_ref, p_ref, n_ref, prm_ref, wi_ref, wo_ref, xo_ref, y_ref, b_ref):
        i = pl.program_id(0)
        xe = jnp.concatenate([p_ref[...], c_ref[...], n_ref[...]], axis=0)
        hn, _, _, _ = _nm(xe, prm_ref[3:4, :], prm_ref[0:1, :], prm_ref[1:2, :])
        hb = hn.astype(BF16)
        proj = jnp.concatenate([_dot(hb, wi_ref[s]) for s in range(4)], axis=1)
        bg, cg, xin = proj[:, :D], proj[:, D:2 * D], proj[:, 2 * D:]
        tpos = i * tm - HALO + lax.broadcasted_iota(jnp.int32, (te, 1), 0)
        valid = ((tpos >= 0) & (tpos < L)).astype(F32)
        yc = _conv3(cg * xin * valid, [prm_ref[4 + k:5 + k, :] for k in range(3)])
        h2 = (bg * yc)[HALO:HALO + tm].astype(BF16)
        yv = _dot(h2, wo_ref[...])
        y_ref[...] = yv.astype(BF16)
        xo_ref[...] = c_ref[...] + prm_ref[2:3, :] * yv
        b_ref[...] = proj[HALO:HALO + tm].astype(BF16)

    return pl.pallas_call(
        body, name=name, grid=(nt,),
        in_specs=[_tile(tm, D), _halo_prev(tm, D), _halo_next(tm, D, L), _full((8, D)),
                  _resident((None, 4, D, Ws), lambda i: (0, 0, 0, 0)),
                  _resident((None, D, D), lambda i: (0, 0, 0))],
        out_specs=[_tile(tm, D), _tile(tm, D), _tile(tm, 3 * D)],
        out_shape=[jax.ShapeDtypeStruct((L, D), F32), jax.ShapeDtypeStruct((L, D), BF16),
                   jax.ShapeDtypeStruct((L, 3 * D), BF16)],
        compiler_params=_params())(x, x, x, prm, wing, woutg)


def _conv_bwd(dout, x, y, bcx, prm, wing, woutg, name, tm=256):
    L, D = x.shape
    Ws = wing.shape[-1]
    tm = min(tm, L)
    nt = L // tm
    te = tm + 2 * HALO

    def body(dc_ref, dp_ref, dn_ref, x_ref, y_ref, bc_ref, bp_ref, bn_ref, prm_ref, wi_ref, wo_ref,
             dx_ref, dpr_ref, h2_ref, dy_ref, hn_ref, acc_ref):
        i = pl.program_id(0)
        gain, shift, scale, gate = prm_ref[3:4, :], prm_ref[0:1, :], prm_ref[1:2, :], prm_ref[2:3, :]
        taps = [prm_ref[4 + k:5 + k, :] for k in range(3)]
        do = dc_ref[...]
        doe = _ext(dp_ref[...], do, dn_ref[...], i, nt)
        dye = (gate * doe).astype(BF16)
        dy_ref[...] = dye[HALO:HALO + tm]
        dh2 = _dot_nt(dye, wo_ref[...])
        be = jnp.concatenate([bp_ref[...], bc_ref[...], bn_ref[...]], axis=0).astype(F32)
        bg, cg, xin = be[:, :D], be[:, D:2 * D], be[:, 2 * D:]
        tpos = i * tm - HALO + lax.broadcasted_iota(jnp.int32, (te, 1), 0)
        valid = ((tpos >= 0) & (tpos < L)).astype(F32)
        z = cg * xin * valid
        yc = _conv3(z, taps)
        dyc = dh2 * bg
        h2_ref[...] = (bg * yc)[HALO:HALO + tm].astype(BF16)
        dz = _conv3(dyc, taps[::-1]) * valid
        dproj = jnp.concatenate([dh2 * yc, dz * xin, dz * cg], axis=1)[HALO:HALO + tm].astype(BF16)
        dpr_ref[...] = dproj
        dhn = jnp.zeros((tm, D), F32)
        for s in range(4):
            dhn = dhn + _dot_nt(dproj[:, s * Ws:(s + 1) * Ws], wi_ref[s])
        hn, xhat, r, nrm = _nm(x_ref[...], gain, shift, scale)
        hn_ref[...] = hn.astype(BF16)
        dx, dshift, dscale, dgn = _nm_bwd(dhn, xhat, r, nrm, gain, scale)
        dx_ref[...] = do + dx
        dgate = _sum0(do * y_ref[...].astype(F32))
        dtaps = [_sum0((dyc * _shift_rows(z, 1 - k))[HALO:HALO + tm]) for k in range(3)]
        _acc_rows(acc_ref, i == 0, [dshift, dscale, dgate, dgn] + dtaps)

    return pl.pallas_call(
        body, name=name, grid=(nt,),
        in_specs=[_tile(tm, D), _halo_prev(tm, D), _halo_next(tm, D, L), _tile(tm, D), _tile(tm, D),
                  _tile(tm, 3 * D), _halo_prev(tm, 3 * D), _halo_next(tm, 3 * D, L), _full((8, D)),
                  _resident((None, 4, D, Ws), lambda i: (0, 0, 0, 0)),
                  _resident((None, D, D), lambda i: (0, 0, 0))],
        out_specs=[_tile(tm, D), _tile(tm, 3 * D), _tile(tm, D), _tile(tm, D), _tile(tm, D), _full((8, D))],
        out_shape=[jax.ShapeDtypeStruct((L, D), F32), jax.ShapeDtypeStruct((L, 3 * D), BF16),
                   jax.ShapeDtypeStruct((L, D), BF16), jax.ShapeDtypeStruct((L, D), BF16),
                   jax.ShapeDtypeStruct((L, D), BF16), jax.ShapeDtypeStruct((8, D), F32)],
        compiler_params=_params())(dout, dout, dout, x, y, bcx, bcx, bcx, prm, wing, woutg)


def _loss_head(x, tgt, fg, name, tm=512):
    L, D = x.shape
    tm = min(tm, L)

    def body(x_ref, t_ref, g_ref, dx_ref, acc_ref):
        i = pl.program_id(0)
        xv = x_ref[...]
        g = g_ref[...]
        r = lax.rsqrt(jnp.mean(xv * xv, axis=-1, keepdims=True) + RMS_EPS)
        xhat = xv * r
        err = xhat * g - t_ref[...]
        part = 0.5 * jnp.sum(jnp.mean(err * err, axis=-1, keepdims=True), axis=0, keepdims=True)
        dy = err * (1.0 / D)
        dxh = dy * g
        dx_ref[...] = r * (dxh - xhat * jnp.mean(dxh * xhat, axis=-1, keepdims=True))
        _acc_rows(acc_ref, i == 0, [_sum0(dy * xhat), jnp.broadcast_to(part, (1, D))])

    return pl.pallas_call(
        body, name=name, grid=(L // tm,), in_specs=[_tile(tm, D), _tile(tm, D), _full((1, D))],
        out_specs=[_tile(tm, D), _full((8, D))],
        out_shape=[jax.ShapeDtypeStruct((L, D), F32), jax.ShapeDtypeStruct((8, D), F32)],
        compiler_params=_params())(x, tgt, fg)


def _mod_fwd(cond, mod_w, mod_b, name, tn=768):
    nl, D, N = mod_w.shape
    tn = min(tn, N)

    def body(c_ref, w_ref, b_ref, o_ref):
        cv = c_ref[...]
        s = (cv * _sigmoid(cv)).astype(BF16)
        o_ref[...] = _dot(s, w_ref[...].astype(BF16)) + b_ref[...]

    return pl.pallas_call(
        body, name=name, grid=(nl, N // tn),
        in_specs=[pl.BlockSpec((16, D), lambda l, j: (0, 0)), pl.BlockSpec((None, D, tn), lambda l, j: (l, 0, j)),
                  pl.BlockSpec((None, 1, tn), lambda l, j: (l, 0, j))],
        out_specs=pl.BlockSpec((None, 16, tn), lambda l, j: (l, 0, j)),
        out_shape=jax.ShapeDtypeStruct((nl, 16, N), F32), compiler_params=_params())(cond, mod_w, mod_b)


def _mod_bwd(cond, dm, mod_w, name, tn=768):
    nl, D, N = mod_w.shape
    tn = min(tn, N)

    def body(c_ref, d_ref, w_ref, dw_ref, dc_ref):
        first = (pl.program_id(0) == 0) & (pl.program_id(1) == 0)
        cv = c_ref[...]
        s = (cv * _sigmoid(cv)).astype(BF16)
        d = d_ref[...].astype(BF16)
        dw_ref[...] = _dot_tn(s, d)

        @pl.when(first)
        def _():
            dc_ref[...] = jnp.zeros((16, D), F32)

        dc_ref[...] += _dot_nt(d, w_ref[...].astype(BF16))

    return pl.pallas_call(
        body, name=name, grid=(nl, N // tn),
        in_specs=[pl.BlockSpec((16, D), lambda l, j: (0, 0)), pl.BlockSpec((None, 16, tn), lambda l, j: (l, 0, j)),
                  pl.BlockSpec((None, D, tn), lambda l, j: (l, 0, j))],
        out_specs=[pl.BlockSpec((None, D, tn), lambda l, j: (l, 0, j)), pl.BlockSpec((16, D), lambda l, j: (0, 0))],
        out_shape=[jax.ShapeDtypeStruct((nl, D, N), F32), jax.ShapeDtypeStruct((16, D), F32)],
        compiler_params=_params())(cond, dm, mod_w)


def _mod_small_grads(dm_all, cond, dsilu_parts, name):
    nl, _, N = dm_all.shape
    D = cond.shape[1]

    def body(d_ref, c_ref, p_ref, db_ref, dc_ref):
        for l in range(nl):
            db_ref[l] = _sum0(d_ref[l])
        tot = p_ref[0, 8:9, :]
        for k in range(1, N_CHIPS):
            tot = tot + p_ref[2 * k, 8:9, :]
        cv = c_ref[8:9, :]
        sg = _sigmoid(cv)
        dc_ref[...] = tot * (sg * (1.0 + cv * (1.0 - sg)))

    return pl.pallas_call(
        body, name=name, out_shape=[jax.ShapeDtypeStruct((nl, 1, N), F32), jax.ShapeDtypeStruct((1, D), F32)],
    )(dm_all, cond, dsilu_parts)


def _prm(rows, D):
    rows = [r.reshape(1, D) for r in rows]
    return jnp.concatenate(rows + [jnp.zeros((8 - len(rows), D), F32)], axis=0)


def kernel(x, c, ctx, c_ctx, mod_w, mod_b, norm_g, ffn_w13, ffn_w2, even_w_in, even_w_out, na_rpb, pool_w, pool_scale, conv_w_in, conv_w, conv_w_out, final_g, loss_target, m_c_ctx, m_mod_w, m_mod_b, m_norm_g, m_ffn_w13, m_ffn_w2, m_even_w_in, m_even_w_out, m_na_rpb, m_pool_w, m_pool_scale, m_conv_w_in, m_conv_w, m_conv_w_out, m_final_g, v_c_ctx, v_mod_w, v_mod_b, v_norm_g, v_ffn_w13, v_ffn_w2, v_even_w_in, v_even_w_out, v_na_rpb, v_pool_w, v_pool_scale, v_conv_w_in, v_conv_w, v_conv_w_out, v_final_g):
    xi, yi, ci = lax.axis_index("x"), lax.axis_index("y"), lax.axis_index("c")
    chip = 2 * xi + yi
    dev = 4 * xi + 2 * yi + ci
    _, L, D = x.shape
    C = ctx.shape[1]
    Ds = D // N_CHIPS
    Nm = mod_w.shape[-1]
    Fh = ffn_w13.shape[-1]
    Fq = ffn_w2.shape[2]
    assert ffn_w13.shape[:2] == (2, 2) and Fh == 2 * Fq and L % (GQ * GRID_W) == 0 and L // GRID_W >= GK and GQ == NA_KH // 2
    x0, ctx0, tgt = x[0], ctx[0], loss_target[0]

    pad = lambda a: jnp.pad(a, ((0, 0), (0, D - a.shape[1])))
    pack1 = jnp.concatenate([c, pad(norm_g.reshape(6, Ds)), pad(conv_w.reshape(3, Ds)), jnp.zeros((6, D), F32)], axis=0)
    g1 = _small_all_gather(pack1, "ag_cond")
    cond = jnp.concatenate([g1[:, 0], c_ctx[None], jnp.zeros((7, D), F32)], axis=0)
    norm_full = jnp.concatenate([g1[2 * k, 1:7, :Ds] for k in range(N_CHIPS)], axis=1).reshape(2, 3, D)
    convw_full = jnp.concatenate([g1[2 * k, 7:10, :Ds] for k in range(N_CHIPS)], axis=1)

    mod_b_loc = lax.dynamic_slice_in_dim(mod_b, chip * Nm, Nm, axis=1).reshape(2, 1, Nm)
    m_loc = _mod_fwd(cond, mod_w, mod_b_loc, "mod_fwd")
    g2 = _small_all_gather(m_loc.reshape(32, Nm), "ag_mod")
    m_all = jnp.concatenate([g2[2 * k] for k in range(N_CHIPS)], axis=1).reshape(2, 16, N_MOD, D)
    m_lat = lax.dynamic_index_in_dim(m_all, dev, axis=1, keepdims=False)
    m_ctx = m_all[:, 8]

    def prm(mods, layer, base, gain_idx, extra=()):
        return _prm([mods[layer, base], mods[layer, base + 1], mods[layer, base + 2], norm_full[layer, gain_idx],
                     *extra], D)

    def shard_bf16(w, name):
        return _cast_bf16(w.reshape(-1, w.shape[-1]), name).reshape(-1, *w.shape[-2:])

    w13g, w2g, eing, eoutg, cing, coutg = _gather_shards(
        [shard_bf16(ffn_w13, "cast_w13"), shard_bf16(ffn_w2, "cast_w2"), shard_bf16(even_w_in, "cast_ein"),
         shard_bf16(even_w_out, "cast_eout"), shard_bf16(conv_w_in, "cast_cin"), shard_bf16(conv_w_out, "cast_cout")],
        "ag_weights")
    w13g = w13g.reshape(4, 4, D, Fh)
    eing = eing.reshape(1, 4, D, NA_WIDTH)
    cing = cing.reshape(1, 4, D, conv_w_in.shape[-1])

    p_f1 = prm(m_lat, 0, 0, 0)
    p_mx = prm(m_lat, 0, 3, 1)
    p_f2 = prm(m_lat, 0, 6, 2)
    p_g1 = prm(m_lat, 1, 0, 0)
    p_cv = prm(m_lat, 1, 3, 1, extra=(convw_full[0], convw_full[1], convw_full[2]))
    p_g2 = prm(m_lat, 1, 6, 2)
    pc_f1 = prm(m_ctx, 0, 0, 0)
    pc_mx = prm(m_ctx, 0, 3, 1)

    x1, ab1, y1 = _ffn_fwd(x0, p_f1, w13g, w2g, 0, "ffn_fwd_l0a")
    ctx1, abc, yc = _ffn_fwd(ctx0, pc_f1, w13g, w2g, 0, "ffn_fwd_ctx")
    q, k, v, u, hn_mx = _even_in_fwd(x1, p_mx, eing, "even_in_fwd")
    _, k_c, v_c, _, hn_cx = _even_in_fwd(ctx1, pc_mx, eing, "even_in_ctx")
    bias = _bias_table(na_rpb[0], "bias_table")
    att = _attn_fwd(q, k, v, k_c, v_c, bias, "attn_fwd")
    pw_b = _cast_bf16(pool_w.reshape(-1, POOL_GROUP_DIM), "cast_poolw").reshape(4, POOL_GROUP_DIM, POOL_GROUP_DIM)
    pool, dmx = _pool_fwd(u, pw_b, pool_scale, "pool_fwd")
    x2, ymx = _even_out_fwd(x1, att, pool, p_mx, eoutg, "even_out_fwd")
    x3, ab2, y2 = _ffn_fwd(x2, p_f2, w13g, w2g, 1, "ffn_fwd_l0b")
    x4, ab3, y3 = _ffn_fwd(x3, p_g1, w13g, w2g, 2, "ffn_fwd_l1a")
    x5, ycv, bcx = _conv_fwd(x4, p_cv, cing, coutg, "conv_fwd")
    x6, ab4, y4 = _ffn_fwd(x5, p_g2, w13g, w2g, 3, "ffn_fwd_l1b")
    dx6, acc_head = _loss_head(x6, tgt, final_g.reshape(1, D), "loss_head")
    loss = lax.psum(acc_head[1, 0], ("x", "y", "c"))

    def ffn_back(dout, xin, ab, yy, p, t, tag, init13=None, init2=None):
        dx, dab, gact, dy, hn, acc = _ffn_bwd(dout, xin, ab, yy, p, w13g, w2g, t, f"ffn_bwd_{tag}")
        dw13 = _mm_tn(hn, dab, 4, False, f"dw13_{tag}", init=init13)
        dw2 = _mm_tn(gact, dy, 2, True, f"dw2_{tag}", init=init2)
        return dx, acc, dw13, dw2

    dx5, acc_g2, dw13_3, dw2_3 = ffn_back(dx6, x5, ab4, y4, p_g2, 3, "l1b")
    dx4, dproj, h2, dycv, hn_cv, acc_cv = _conv_bwd(dx5, x4, ycv, bcx, p_cv, cing, coutg, "conv_bwd")
    dcin = _mm_tn(hn_cv, dproj, 4, False, "dw_cin")
    dcout = _mm_tn(h2, dycv, 1, False, "dw_cout")
    dx3, acc_g1, dw13_2, dw2_2 = ffn_back(dx4, x3, ab3, y3, p_g1, 2, "l1a")
    dx2, acc_f2, dw13_1, dw2_1 = ffn_back(dx3, x2, ab2, y2, p_f2, 1, "l0b")

    dymx, datt, dpool, acc_mxo = _even_out_bwd(dx2, ymx, p_mx, eoutg, "even_out_bwd")
    deout = jnp.concatenate([_mm_tn(att, dymx, 1, False, "dw_eout_att"),
                             _mm_tn(pool, dymx, 1, False, "dw_eout_pool")], axis=0)
    du, dpoolw, acc_pool = _pool_bwd(dpool, dmx, pw_b, pool_scale, "pool_bwd")
    dq, dk, dv, dkc, dvc, dbias = _attn_bwd(q, k, v, k_c, v_c, bias, datt, "attn_bwd")
    drpb = _rpb_grad(dbias, "rpb_grad")
    dx1, dstack, acc_mxi = _even_in_bwd(dx2, x1, dq, dk, dv, du, p_mx, eing,
                                        "even_in_bwd")
    zc = jnp.zeros((C, NA_WIDTH), F32)
    dctx1, dstack_c, accc_mx = _even_in_bwd(jnp.zeros((C, D), F32), ctx1, zc, dkc, dvc, zc,
                                            pc_mx, eing, "even_in_bwd_ctx")
    dein_c = _mm_tn(hn_cx, dstack_c, 4, False, "dw_ein_ctx")
    dein = _mm_tn(hn_mx, dstack, 4, False, "dw_ein", init=dein_c)
    _, accc_f1, dw13_c, dw2_c = ffn_back(dctx1, ctx0, abc, yc, pc_f1, 0, "ctx")
    dx0, acc_f1, dw13_0, dw2_0 = ffn_back(dx1, x0, ab1, y1, p_f1, 0, "l0a", init13=dw13_c, init2=dw2_c)

    z1 = jnp.zeros((1, D), F32)
    dm_lat = jnp.concatenate([acc_f1[0:3], acc_mxi[0:2], acc_mxo[2:3], acc_f2[0:3],
                              acc_g1[0:3], acc_cv[0:3], acc_g2[0:3]], axis=0)
    dm_ctx = jnp.concatenate([accc_f1[0:3], accc_mx[0:2]] + [z1] * 13, axis=0)
    dnorm = jnp.concatenate([acc_f1[3:4] + accc_f1[3:4], acc_mxi[3:4] + accc_mx[3:4], acc_f2[3:4],
                             acc_g1[3:4], acc_cv[3:4], acc_g2[3:4]], axis=0)
    rpb_flat = jnp.pad(drpb.reshape(-1), (0, 4 * D - drpb.size)).reshape(4, D)
    pack3 = jnp.concatenate([dm_lat, dm_ctx, dnorm, acc_cv[4:7], acc_head[0:1], pad(acc_pool[0:1]), z1,
                             dpoolw.reshape(-1, D), rpb_flat, jnp.zeros((4, D), F32)], axis=0)
    g3 = _small_all_gather(pack3, "ag_small")
    tot = _sum_devices(g3, "sum_small")
    dm_all = jnp.concatenate([g3[:, 0:18].reshape(8, 2, N_MOD * D).transpose(1, 0, 2),
                              tot[18:36].reshape(2, 1, N_MOD * D), jnp.zeros((2, 7, N_MOD * D), F32)], axis=1)
    dm_loc = lax.dynamic_slice_in_dim(dm_all, chip * Nm, Nm, axis=2)
    g_mod_w, dsilu = _mod_bwd(cond, dm_loc, mod_w, "mod_bwd")
    g4 = _small_all_gather(dsilu, "ag_dsilu")
    g_mod_b, g_c_ctx = _mod_small_grads(dm_all, cond, g4, "mod_small")
    g_mod_b = g_mod_b.reshape(2, N_MOD * D)
    g_c_ctx = g_c_ctx.reshape(D)
    g_norm_full = tot[36:42].reshape(2, 3, D)
    g_norm = lax.dynamic_slice_in_dim(g_norm_full, chip * Ds, Ds, axis=2)
    g_conv_w = lax.dynamic_slice_in_dim(tot[42:45], chip * Ds, Ds, axis=1).reshape(1, 3, Ds)
    g_final = tot[45]
    g_pscale = tot[46:47, :pool_scale.shape[1]]
    g_poolw = tot[48:112].reshape(pool_w.shape)
    g_rpb = tot[112:116].reshape(-1)[:na_rpb.size].reshape(na_rpb.shape)

    pos = jnp.stack([chip, ci]).astype(jnp.int32)
    red = _reduce_scatter([dw13_0, dw13_1, dw13_2, dw13_3, dw2_0, dw2_1, dw2_2, dw2_3, dein, deout, dcin, dcout],
                          pos, "w")
    g_w13 = jnp.stack(red[0:4]).reshape(ffn_w13.shape)
    g_w2 = jnp.stack(red[4:8]).reshape(ffn_w2.shape)
    g_ein, g_eout, g_cin, g_cout = (red[8][None], red[9][None], red[10][None], red[11][None])

    grads = [g_c_ctx, g_mod_w, g_mod_b, g_norm, g_w13, g_w2, g_ein, g_eout, g_rpb, g_poolw, g_pscale, g_cin,
             g_conv_w, g_cout, g_final]
    weights = [c_ctx, mod_w, mod_b, norm_g, ffn_w13, ffn_w2, even_w_in, even_w_out, na_rpb, pool_w, pool_scale,
               conv_w_in, conv_w, conv_w_out, final_g]
    ms = [m_c_ctx, m_mod_w, m_mod_b, m_norm_g, m_ffn_w13, m_ffn_w2, m_even_w_in, m_even_w_out, m_na_rpb, m_pool_w,
          m_pool_scale, m_conv_w_in, m_conv_w, m_conv_w_out, m_final_g]
    vs = [v_c_ctx, v_mod_w, v_mod_b, v_norm_g, v_ffn_w13, v_ffn_w2, v_even_w_in, v_even_w_out, v_na_rpb, v_pool_w,
          v_pool_scale, v_conv_w_in, v_conv_w, v_conv_w_out, v_final_g]
    names = ["c_ctx", "mod_w", "mod_b", "norm_g", "ffn_w13", "ffn_w2", "even_w_in", "even_w_out", "na_rpb", "pool_w",
             "pool_scale", "conv_w_in", "conv_w", "conv_w_out", "final_g"]
    deltas, new_m, new_v = [], [], []
    for n, w, g, m, vv in zip(names, weights, grads, ms, vs):
        g = g.reshape(w.shape)
        if w.ndim == 1:
            d, mn, vn = (t.reshape(w.shape) for t in _adamw(w[None], g[None], m[None], vv[None], f"adamw_{n}"))
        else:
            d, mn, vn = _adamw(w, g, m, vv, f"adamw_{n}")
        deltas.append(d)
        new_m.append(mn)
        new_v.append(vn)
    grads = [g.reshape(w.shape) for g, w in zip(grads, weights)]
    return (loss, dx0[None], *grads, *deltas, *new_m, *new_v)
```

```python
import jax
import jax.numpy as jnp
from jax import lax
from jax.experimental import pallas as pl
from jax.experimental.pallas import tpu as pltpu

F32 = jnp.float32
BF16 = jnp.bfloat16
MESH = pl.DeviceIdType.MESH

GRID_W = 64
NA_HEADS = 8
NA_HEAD_DIM = 64
NA_KH = 8
NA_KW = 16
GQ = 4
GK = GQ + NA_KH
NA_WIDTH = NA_HEADS * NA_HEAD_DIM
POOL_WINDOWS = (2, 4, 8, 16)
POOL_GROUP_DIM = 128
N_MOD = 9
RMS_EPS = 1e-6
NEG_INF = -1e30
ADAM_LR, ADAM_B1, ADAM_B2, ADAM_EPS, ADAM_WD, ADAM_STEP = 0.001, 0.9, 0.999, 1e-08, 0.01, 10

HALO = 16
VMEM_LIMIT = 56 * 1024 * 1024
N_CHIPS = 4
N_DEV = 8


def _dot(a, b):
    return jnp.dot(a, b, preferred_element_type=F32)


def _dot_nt(a, b):
    return lax.dot_general(a, b, (((1,), (1,)), ((), ())), preferred_element_type=F32)


def _dot_tn(a, b):
    return lax.dot_general(a, b, (((0,), (0,)), ((), ())), preferred_element_type=F32)


def _sigmoid(a):
    return 1.0 / (1.0 + jnp.exp(-a))


def _sum0(v):
    return jnp.sum(v, axis=0, keepdims=True)


def _nm(x, g, shift, scale):
    r = lax.rsqrt(jnp.mean(x * x, axis=-1, keepdims=True) + RMS_EPS)
    xhat = x * r
    nrm = xhat * g
    return nrm * (1.0 + scale) + shift, xhat, r, nrm


def _nm_bwd(dhn, xhat, r, nrm, g, scale):
    dshift = _sum0(dhn)
    dscale = _sum0(dhn * nrm)
    dnrm = dhn * (1.0 + scale)
    dgn = _sum0(dnrm * xhat)
    dxh = dnrm * g
    dx = r * (dxh - xhat * jnp.mean(dxh * xhat, axis=-1, keepdims=True))
    return dx, dshift, dscale, dgn


def _acc_rows(acc_ref, first, rows):
    @pl.when(first)
    def _():
        acc_ref[...] = jnp.zeros(acc_ref.shape, acc_ref.dtype)
    for k, row in enumerate(rows):
        if row is not None:
            acc_ref[k:k + 1, :] += row


def _shift_rows(v, k):
    n = v.shape[0]
    k = k % n
    return v if k == 0 else pltpu.roll(v, k, 0)


def _tile(tm, w):
    return pl.BlockSpec((tm, w), lambda i: (i, 0))


def _full(shape):
    nd = len(shape)
    return pl.BlockSpec(shape, lambda i: (0,) * nd)


def _resident(block, imap):
    return pl.BlockSpec(block, imap, pipeline_mode=pl.Buffered(1))


def _halo_prev(tm, w):
    return pl.BlockSpec((HALO, w), lambda i: (jnp.maximum(i * (tm // HALO) - 1, 0), 0))


def _halo_next(tm, w, L):
    return pl.BlockSpec((HALO, w), lambda i: (jnp.minimum((i + 1) * (tm // HALO), L // HALO - 1), 0))


def _params(vmem=VMEM_LIMIT):
    return pltpu.CompilerParams(vmem_limit_bytes=vmem)


def _pick_rows(rows, cols, itemsize=4, target=1 << 20):
    best = None
    for t in range(8, rows + 1, 8):
        if rows % t == 0 and t * cols * itemsize <= target:
            best = t
    return best if best is not None else rows


def _ext(prev, cur, nxt, i, nt):
    prev = jnp.where(i > 0, prev, jnp.zeros_like(prev))
    nxt = jnp.where(i < nt - 1, nxt, jnp.zeros_like(nxt))
    return jnp.concatenate([prev, cur, nxt], axis=0)


def _cast_bf16(a2d, name):
    rows, cols = a2d.shape
    tr = _pick_rows(rows, cols)

    def body(a_ref, o_ref):
        o_ref[...] = a_ref[...].astype(BF16)

    return pl.pallas_call(
        body, name=name, grid=(rows // tr,), in_specs=[_tile(tr, cols)], out_specs=_tile(tr, cols),
        out_shape=jax.ShapeDtypeStruct((rows, cols), BF16))(a2d)


def _sum_devices(g, name):
    n, rows, cols = g.shape
    tr = _pick_rows(rows, cols, target=1 << 18)

    def body(g_ref, o_ref):
        s = g_ref[0]
        for d in range(1, n):
            s = s + g_ref[d]
        o_ref[...] = s

    return pl.pallas_call(
        body, name=name, grid=(rows // tr,), in_specs=[pl.BlockSpec((n, tr, cols), lambda i: (0, i, 0))],
        out_specs=_tile(tr, cols), out_shape=jax.ShapeDtypeStruct((rows, cols), F32))(g)


def _adamw(w, g, m, v, name):
    shape = w.shape
    cols = shape[-1]
    rows = w.size // cols
    w2, g2, m2, v2 = (t.reshape(rows, cols) for t in (w, g, m, v))
    tr = _pick_rows(rows, cols)
    c1 = 1.0 - ADAM_B1 ** ADAM_STEP
    c2 = 1.0 - ADAM_B2 ** ADAM_STEP

    def body(w_ref, g_ref, m_ref, v_ref, d_ref, mo_ref, vo_ref):
        gg = g_ref[...]
        mn = ADAM_B1 * m_ref[...] + (1.0 - ADAM_B1) * gg
        vn = ADAM_B2 * v_ref[...] + (1.0 - ADAM_B2) * (gg * gg)
        d_ref[...] = -ADAM_LR * ((mn / c1) / (jnp.sqrt(vn / c2) + ADAM_EPS) + ADAM_WD * w_ref[...])
        mo_ref[...] = mn
        vo_ref[...] = vn

    outs = pl.pallas_call(
        body, name=name, grid=(rows // tr,), in_specs=[_tile(tr, cols)] * 4, out_specs=[_tile(tr, cols)] * 3,
        out_shape=[jax.ShapeDtypeStruct((rows, cols), F32)] * 3)(w2, g2, m2, v2)
    return tuple(o.reshape(shape) for o in outs)


def _mesh_pos():
    x, y, c = lax.axis_index("x"), lax.axis_index("y"), lax.axis_index("c")
    chips = [(1 - x, y), (x, 1 - y), (1 - x, 1 - y)]
    return x, y, c, chips


def _hbm_specs(n):
    return [pl.BlockSpec(memory_space=pltpu.HBM)] * n


def _small_all_gather(v, name):
    rows, w = v.shape

    def body(x_ref, out_ref, send_sems, recv_sems, local_sem):
        x, y, c, chips = _mesh_pos()
        me, sibling = (x, y, c), (x, y, 1 - c)

        def blk(px, py, pc):
            return out_ref.at[4 * px + 2 * py + pc]

        def copy(k, block, to, src=None):
            return pltpu.make_async_remote_copy(
                src_ref=blk(*block) if src is None else src, dst_ref=blk(*block),
                send_sem=send_sems.at[k], recv_sem=recv_sems.at[k], device_id=to, device_id_type=MESH)

        mine = pltpu.make_async_copy(x_ref, blk(*me), local_sem)
        mine.start()
        first = [copy(0, me, sibling, src=x_ref)]
        first += [copy(1 + j, me, (*chip, c), src=x_ref) for j, chip in enumerate(chips)]
        for cp in first:
            cp.start()
        passed = [copy(4 + j, (*chip, c), sibling) for j, chip in enumerate(chips)]
        for j, chip in enumerate(chips):
            copy(1 + j, (*chip, c), me).wait_recv()
            passed[j].start()
        copy(0, sibling, me).wait_recv()
        for j, chip in enumerate(chips):
            copy(4 + j, (*chip, 1 - c), me).wait_recv()
        for cp in first + passed:
            cp.wait_send()
        mine.wait()

    return pl.pallas_call(
        body, name=name, out_shape=jax.ShapeDtypeStruct((N_DEV, rows, w), v.dtype),
        in_specs=[pl.BlockSpec(memory_space=pltpu.VMEM)], out_specs=pl.BlockSpec(memory_space=pltpu.VMEM),
        scratch_shapes=[pltpu.SemaphoreType.DMA((7,)), pltpu.SemaphoreType.DMA((7,)), pltpu.SemaphoreType.DMA],
    )(v)


def _gather_shards(shards, name):
    n = len(shards)

    def body(*refs):
        ins, outs = refs[:n], refs[n:2 * n]
        send_sems, recv_sems, local_sems = refs[2 * n:]
        x, y, c, chips = _mesh_pos()
        k = 2 * x + y
        sibling = (x, y, 1 - c)

        def window(t, chip_k, half):
            r = ins[t].shape[1]
            return outs[t].at[:, pl.ds(chip_k * r + half * (r // 2), r // 2), :]

        def copy(t, j, chip_k, half, to, src=None):
            return pltpu.make_async_remote_copy(
                src_ref=window(t, chip_k, half) if src is None else src, dst_ref=window(t, chip_k, half),
                send_sem=send_sems.at[6 * t + j], recv_sem=recv_sems.at[6 * t + j], device_id=to, device_id_type=MESH)

        started, local = [], []
        for t in range(n):
            r = ins[t].shape[1]
            lc = pltpu.make_async_copy(ins[t], outs[t].at[:, pl.ds(k * r, r), :], local_sems.at[t])
            lc.start()
            local.append(lc)
            src = ins[t].at[:, pl.ds(c * (r // 2), r // 2), :]
            for j, chip in enumerate(chips):
                cp = copy(t, j, k, c, (*chip, c), src=src)
                cp.start()
                started.append(cp)
        for t in range(n):
            for j, chip in enumerate(chips):
                kj = 2 * chip[0] + chip[1]
                copy(t, j, kj, c, sibling).wait_recv()
                cp = copy(t, 3 + j, kj, c, sibling)
                cp.start()
                started.append(cp)
        for t in range(n):
            for j, chip in enumerate(chips):
                kj = 2 * chip[0] + chip[1]
                copy(t, 3 + j, kj, 1 - c, sibling).wait_recv()
        for cp in started:
            cp.wait_send()
        for lc in local:
            lc.wait()

    out_shape = [jax.ShapeDtypeStruct((s.shape[0], N_CHIPS * s.shape[1], s.shape[2]), s.dtype) for s in shards]
    return pl.pallas_call(
        body, name=name, out_shape=out_shape, in_specs=_hbm_specs(n), out_specs=_hbm_specs(n),
        scratch_shapes=[pltpu.SemaphoreType.DMA((6 * n,)), pltpu.SemaphoreType.DMA((6 * n,)),
                        pltpu.SemaphoreType.DMA((n,))],
    )(*shards)


def _chunk_rows(h, w):
    best = 16
    for t in range(16, h + 1, 16):
        if h % t == 0 and t * w * 4 <= (1 << 20):
            best = t
    return best


def _pair_sum(part, pos, name):
    _, h, w = part.shape
    cr = _chunk_rows(h, w)
    nc = h // cr

    def body(pos_ref, own_ref, send_ref, s_ref, sb_ref, rbuf, send_sems, recv_sems):
        x, y, c, _ = _mesh_pos()
        slot = pl.program_id(0) % 2
        cp = pltpu.make_async_remote_copy(
            src_ref=send_ref, dst_ref=rbuf.at[slot], send_sem=send_sems.at[slot], recv_sem=recv_sems.at[slot],
            device_id=(x, y, 1 - c), device_id_type=MESH)
        cp.start()
        cp.wait_recv()
        s = own_ref[...] + rbuf[slot]
        s_ref[...] = s
        sb_ref[...] = s.astype(BF16)
        cp.wait_send()

    grid_spec = pltpu.PrefetchScalarGridSpec(
        num_scalar_prefetch=1, grid=(4 * nc,),
        in_specs=[pl.BlockSpec((cr, w), lambda k, p: ((2 * (k // nc) + p[1]) * nc + k % nc, 0)),
                  pl.BlockSpec((cr, w), lambda k, p: ((2 * (k // nc) + 1 - p[1]) * nc + k % nc, 0))],
        out_specs=[pl.BlockSpec((cr, w), lambda k, p: (k, 0))] * 2,
        scratch_shapes=[pltpu.VMEM((2, cr, w), F32), pltpu.SemaphoreType.DMA((2,)), pltpu.SemaphoreType.DMA((2,))])
    part2 = part.reshape(8 * h, w)
    s, sb = pl.pallas_call(
        body, name=name, grid_spec=grid_spec,
        out_shape=[jax.ShapeDtypeStruct((4 * h, w), F32), jax.ShapeDtypeStruct((4 * h, w), BF16)],
    )(pos, part2, part2)
    return s.reshape(4, h, w), sb.reshape(4, h, w)


class _Ride:
    def __init__(self, ins, out_shape, sems, copies):
        self.ins, self.out_shape, self.sems, self.copies = list(ins), list(out_shape), list(sems), copies

    def start(self, ins, outs, sems):
        sends, _, _, local = self.copies(ins, outs, sems)
        for cp in local + sends:
            cp.start()

    def finish(self, ins, outs, sems):
        _, recvs, sends, local = self.copies(ins, outs, sems)
        for cp in recvs:
            cp.wait_recv()
        for cp in sends:
            cp.wait_send()
        for cp in local:
            cp.wait()


def _scatter_ride(sums_bf16):
    n = len(sums_bf16)

    def copies(ins, outs, sems):
        send_sems, recv_sems = sems
        x, y, c, chips = _mesh_pos()
        cps = [pltpu.make_async_remote_copy(
            src_ref=ins[t].at[2 * chip[0] + chip[1]], dst_ref=outs[t].at[j],
            send_sem=send_sems.at[3 * t + j], recv_sem=recv_sems.at[3 * t + j],
            device_id=(*chip, c), device_id_type=MESH) for t in range(n) for j, chip in enumerate(chips)]
        return cps, cps, cps, []

    return _Ride(sums_bf16, [jax.ShapeDtypeStruct((3,) + s.shape[1:], BF16) for s in sums_bf16],
                 [pltpu.SemaphoreType.DMA((3 * n,)), pltpu.SemaphoreType.DMA((3 * n,))], copies)


def _broadcast_ride(shards):
    n = len(shards)

    def copies(ins, outs, sems):
        send_sems, recv_sems, local_sems = sems
        x, y, c, chips = _mesh_pos()
        k = 2 * x + y
        sends, recvs, local = [], [], []
        for t in range(n):
            r = ins[t].shape[1]
            h = r // 2
            local.append(pltpu.make_async_copy(ins[t], outs[t].at[:, pl.ds(k * r, r), :], local_sems.at[t]))
            src = ins[t].at[:, pl.ds(c * h, h), :]
            mine = outs[t].at[:, pl.ds(k * r + c * h, h), :]
            for j, chip in enumerate(chips):
                kj = 2 * chip[0] + chip[1]
                for d in range(2):
                    sends.append(pltpu.make_async_remote_copy(
                        src_ref=src, dst_ref=mine, send_sem=send_sems.at[6 * t + 2 * j + d],
                        recv_sem=recv_sems.at[6 * t + 2 * j + c], device_id=(*chip, d), device_id_type=MESH))
                    theirs = outs[t].at[:, pl.ds(kj * r + d * h, h), :]
                    recvs.append(pltpu.make_async_remote_copy(
                        src_ref=theirs, dst_ref=theirs, send_sem=send_sems.at[6 * t + 2 * j + d],
                        recv_sem=recv_sems.at[6 * t + 2 * j + d], device_id=(*chip, d), device_id_type=MESH))
        return sends, recvs, sends, local

    return _Ride(shards, [jax.ShapeDtypeStruct((s.shape[0], N_CHIPS * s.shape[1], s.shape[2]), s.dtype) for s in shards],
                 [pltpu.SemaphoreType.DMA((6 * n,)), pltpu.SemaphoreType.DMA((6 * n,)), pltpu.SemaphoreType.DMA((n,))],
                 copies)


def _run_ride(ride, name):
    ni, no = len(ride.ins), len(ride.out_shape)

    def body(*refs):
        ride.start(refs[:ni], refs[ni:ni + no], refs[ni + no:])
        ride.finish(refs[:ni], refs[ni:ni + no], refs[ni + no:])

    return pl.pallas_call(body, name=name, out_shape=ride.out_shape, in_specs=_hbm_specs(ni), out_specs=_hbm_specs(no),
                          scratch_shapes=ride.sems)(*ride.ins)


def _ride_call(body, ride, args, *, name, grid, in_specs, out_specs, out_shape, compiler_params=None):
    in_specs, out_specs, out_shape = list(in_specs), list(out_specs), list(out_shape)
    if ride is None:
        res = pl.pallas_call(body, name=name, grid=grid, in_specs=in_specs, out_specs=out_specs, out_shape=out_shape,
                             compiler_params=compiler_params)(*args)
        return list(res), []
    ni, no, ri, ro = len(in_specs), len(out_specs), len(ride.ins), len(ride.out_shape)
    last = grid[0] - 1

    def carried(*refs):
        ins, rins = refs[:ni], refs[ni:ni + ri]
        outs, routs = refs[ni + ri:ni + ri + no], refs[ni + ri + no:ni + ri + no + ro]
        sems = refs[ni + ri + no + ro:]

        @pl.when(pl.program_id(0) == 0)
        def _():
            ride.start(rins, routs, sems)

        body(*ins, *outs)

        @pl.when(pl.program_id(0) == last)
        def _():
            ride.finish(rins, routs, sems)

    res = pl.pallas_call(
        carried, name=name, grid=grid, in_specs=in_specs + _hbm_specs(ri), out_specs=out_specs + _hbm_specs(ro),
        out_shape=out_shape + ride.out_shape, scratch_shapes=ride.sems, compiler_params=compiler_params,
    )(*args, *ride.ins)
    return list(res[:no]), list(res[no:])


def _sum_and_join(sums, got, pos, name):
    _, h, w = sums.shape
    cr = _chunk_rows(h, w)

    def body(pos_ref, mine_ref, got_ref, o_ref, ebuf, rbuf, send_sems, recv_sems):
        x, y, c, _ = _mesh_pos()
        slot = pl.program_id(0) % 2
        e = mine_ref[...]
        for j in range(3):
            e = e + got_ref[j].astype(F32)
        ebuf[slot] = e
        cp = pltpu.make_async_remote_copy(
            src_ref=ebuf.at[slot], dst_ref=rbuf.at[slot], send_sem=send_sems.at[slot], recv_sem=recv_sems.at[slot],
            device_id=(x, y, 1 - c), device_id_type=MESH)
        cp.start()
        o_ref[pos_ref[1]] = e
        cp.wait_recv()
        o_ref[1 - pos_ref[1]] = rbuf[slot]
        cp.wait_send()

    grid_spec = pltpu.PrefetchScalarGridSpec(
        num_scalar_prefetch=1, grid=(h // cr,),
        in_specs=[pl.BlockSpec((None, cr, w), lambda i, p: (p[0], i, 0)),
                  pl.BlockSpec((3, cr, w), lambda i, p: (0, i, 0))],
        out_specs=pl.BlockSpec((2, cr, w), lambda i, p: (0, i, 0)),
        scratch_shapes=[pltpu.VMEM((2, cr, w), F32), pltpu.VMEM((2, cr, w), F32),
                        pltpu.SemaphoreType.DMA((2,)), pltpu.SemaphoreType.DMA((2,))])
    return pl.pallas_call(
        body, name=name, grid_spec=grid_spec, out_shape=jax.ShapeDtypeStruct((2, h, w), F32),
    )(pos, sums, got)


def _pair_sums(parts, pos, tag):
    pairs = [_pair_sum(p.reshape(8, p.shape[0] // 8, p.shape[1]), pos, f"rs_pair_{tag}_{t}")
             for t, p in enumerate(parts)]
    return [s for s, _ in pairs], [sb for _, sb in pairs]


def _joins(sums, got, pos, tag):
    out = []
    for t, (s, r) in enumerate(zip(sums, got)):
        full = _sum_and_join(s, r, pos, f"rs_join_{tag}_{t}")
        out.append(full.reshape(2 * full.shape[1], full.shape[2]))
    return out


def _ffn_fwd(x, prm, w13g, w2g, t, name, tm=512, ride=None):
    L, D = x.shape
    Fh = w13g.shape[-1]
    tm = min(tm, L)

    def body(x_ref, p_ref, w13_ref, w2_ref, xo_ref, ab_ref, y_ref):
        xv = x_ref[...]
        hn, _, _, _ = _nm(xv, p_ref[3:4, :], p_ref[0:1, :], p_ref[1:2, :])
        hb = hn.astype(BF16)
        acc = jnp.zeros((tm, D), F32)
        for j in range(2):
            a = _dot(hb, w13_ref[j])
            b = _dot(hb, w13_ref[2 + j])
            ab_ref[:, j * Fh:(j + 1) * Fh] = a.astype(BF16)
            ab_ref[:, (2 + j) * Fh:(3 + j) * Fh] = b.astype(BF16)
            g = (a * _sigmoid(a) * b).astype(BF16)
            acc = acc + _dot(g, w2_ref[j * Fh:(j + 1) * Fh, :])
        y_ref[...] = acc.astype(BF16)
        xo_ref[...] = xv + (0.5 * p_ref[2:3, :]) * acc

    res, carried = _ride_call(
        body, ride, (x, prm, w13g, w2g), name=name, grid=(L // tm,),
        in_specs=[_tile(tm, D), _full((8, D)),
                  _resident((None, 4, D, Fh), lambda i: (t, 0, 0, 0)),
                  _resident((None, 2 * Fh, D), lambda i: (t, 0, 0))],
        out_specs=[_tile(tm, D), _tile(tm, 4 * Fh), _tile(tm, D)],
        out_shape=[jax.ShapeDtypeStruct((L, D), F32), jax.ShapeDtypeStruct((L, 4 * Fh), BF16),
                   jax.ShapeDtypeStruct((L, D), BF16)],
        compiler_params=_params())
    return (*res, carried)


def _ffn_bwd(dout, x, ab, y, prm, w13g, w2g, t, name, tm=256, ride=None):
    L, D = x.shape
    Fh = w13g.shape[-1]
    tm = min(tm, L)

    def body(do_ref, x_ref, ab_ref, y_ref, p_ref, w13_ref, w2_ref, dx_ref, dab_ref, g_ref, dy_ref, hn_ref, acc_ref):
        i = pl.program_id(0)
        do = do_ref[...]
        gain, shift, scale, gate = p_ref[3:4, :], p_ref[0:1, :], p_ref[1:2, :], p_ref[2:3, :]
        hn, xhat, r, nrm = _nm(x_ref[...], gain, shift, scale)
        hn_ref[...] = hn.astype(BF16)
        dgate = 0.5 * _sum0(do * y_ref[...].astype(F32))
        dyb = ((0.5 * gate) * do).astype(BF16)
        dy_ref[...] = dyb
        dhn = jnp.zeros((tm, D), F32)
        for j in range(2):
            dg = _dot_nt(dyb, w2_ref[j * Fh:(j + 1) * Fh, :])
            a = ab_ref[:, j * Fh:(j + 1) * Fh].astype(F32)
            b = ab_ref[:, (2 + j) * Fh:(3 + j) * Fh].astype(F32)
            sg = _sigmoid(a)
            sa = a * sg
            g_ref[:, j * Fh:(j + 1) * Fh] = (sa * b).astype(BF16)
            da = (dg * b * (sg * (1.0 + a * (1.0 - sg)))).astype(BF16)
            db = (dg * sa).astype(BF16)
            dab_ref[:, j * Fh:(j + 1) * Fh] = da
            dab_ref[:, (2 + j) * Fh:(3 + j) * Fh] = db
            dhn = dhn + _dot_nt(da, w13_ref[j]) + _dot_nt(db, w13_ref[2 + j])
        dx, dshift, dscale, dgn = _nm_bwd(dhn, xhat, r, nrm, gain, scale)
        dx_ref[...] = do + dx
        _acc_rows(acc_ref, i == 0, [dshift, dscale, dgate, dgn])

    res, carried = _ride_call(
        body, ride, (dout, x, ab, y, prm, w13g, w2g), name=name, grid=(L // tm,),
        in_specs=[_tile(tm, D), _tile(tm, D), _tile(tm, 4 * Fh), _tile(tm, D), _full((8, D)),
                  _resident((None, 4, D, Fh), lambda i: (t, 0, 0, 0)),
                  _resident((None, 2 * Fh, D), lambda i: (t, 0, 0))],
        out_specs=[_tile(tm, D), _tile(tm, 4 * Fh), _tile(tm, 2 * Fh), _tile(tm, D), _tile(tm, D), _full((8, D))],
        out_shape=[jax.ShapeDtypeStruct((L, D), F32), jax.ShapeDtypeStruct((L, 4 * Fh), BF16),
                   jax.ShapeDtypeStruct((L, 2 * Fh), BF16), jax.ShapeDtypeStruct((L, D), BF16),
                   jax.ShapeDtypeStruct((L, D), BF16), jax.ShapeDtypeStruct((8, D), F32)],
        compiler_params=_params())
    return (*res, carried)


def _mm_tn(a, b, slabs, a_slabbed, name, init=None, tl=512):
    L = a.shape[0]
    ka = a.shape[1] // slabs if a_slabbed else a.shape[1]
    nb = b.shape[1] if a_slabbed else b.shape[1] // slabs
    tl = min(tl, L)
    has_init = init is not None

    def body(a_ref, b_ref, *rest):
        o_ref = rest[-1]
        step = pl.program_id(1)

        @pl.when(step == 0)
        def _():
            o_ref[...] = rest[0][...] if has_init else jnp.zeros((ka, nb), F32)

        o_ref[...] += _dot_tn(a_ref[...], b_ref[...])

    in_specs = [pl.BlockSpec((tl, ka), (lambda s, l: (l, s)) if a_slabbed else (lambda s, l: (l, 0))),
                pl.BlockSpec((tl, nb), (lambda s, l: (l, 0)) if a_slabbed else (lambda s, l: (l, s)))]
    args = [a, b]
    if has_init:
        in_specs.append(pl.BlockSpec((ka, nb), lambda s, l: (s, 0)))
        args.append(init)
    return pl.pallas_call(
        body, name=name, grid=(slabs, L // tl), in_specs=in_specs,
        out_specs=pl.BlockSpec((ka, nb), lambda s, l: (s, 0)),
        out_shape=jax.ShapeDtypeStruct((slabs * ka, nb), F32), compiler_params=_params())(*args)


def _even_in_fwd(x, prm, wing, name, tm=512):
    L, D = x.shape
    W = wing.shape[-1]
    tm = min(tm, L)

    def body(x_ref, p_ref, w_ref, q_ref, k_ref, v_ref, u_ref, hn_ref):
        hn, _, _, _ = _nm(x_ref[...], p_ref[3:4, :], p_ref[0:1, :], p_ref[1:2, :])
        hb = hn.astype(BF16)
        hn_ref[...] = hb
        q_ref[...] = _dot(hb, w_ref[0]).astype(BF16)
        k_ref[...] = _dot(hb, w_ref[1]).astype(BF16)
        v_ref[...] = _dot(hb, w_ref[2]).astype(BF16)
        u_ref[...] = _dot(hb, w_ref[3])

    return pl.pallas_call(
        body, name=name, grid=(L // tm,),
        in_specs=[_tile(tm, D), _full((8, D)), _resident((None, 4, D, W), lambda i: (0, 0, 0, 0))],
        out_specs=[_tile(tm, W)] * 4 + [_tile(tm, D)],
        out_shape=[jax.ShapeDtypeStruct((L, W), BF16)] * 3 + [jax.ShapeDtypeStruct((L, W), F32),
                                                              jax.ShapeDtypeStruct((L, D), BF16)],
        compiler_params=_params())(x, prm, wing)


def _even_in_bwd(dout, x, dq, dk, dv, du, prm, wing, name, tm=256):
    L, D = x.shape
    W = wing.shape[-1]
    tm = min(tm, L)

    def body(do_ref, x_ref, dq_ref, dk_ref, dv_ref, du_ref, p_ref, w_ref, dx_ref, ds_ref, acc_ref):
        i = pl.program_id(0)
        gain, shift, scale = p_ref[3:4, :], p_ref[0:1, :], p_ref[1:2, :]
        _, xhat, r, nrm = _nm(x_ref[...], gain, shift, scale)
        dhn = jnp.zeros((tm, D), F32)
        for s, ref in enumerate((dq_ref, dk_ref, dv_ref, du_ref)):
            d = ref[...].astype(BF16)
            ds_ref[:, s * W:(s + 1) * W] = d
            dhn = dhn + _dot_nt(d, w_ref[s])
        dx, dshift, dscale, dgn = _nm_bwd(dhn, xhat, r, nrm, gain, scale)
        dx_ref[...] = do_ref[...] + dx
        _acc_rows(acc_ref, i == 0, [dshift, dscale, None, dgn])

    return pl.pallas_call(
        body, name=name, grid=(L // tm,),
        in_specs=[_tile(tm, D), _tile(tm, D)] + [_tile(tm, W)] * 4 +
                 [_full((8, D)), _resident((None, 4, D, W), lambda i: (0, 0, 0, 0))],
        out_specs=[_tile(tm, D), _tile(tm, 4 * W), _full((8, D))],
        out_shape=[jax.ShapeDtypeStruct((L, D), F32), jax.ShapeDtypeStruct((L, 4 * W), BF16),
                   jax.ShapeDtypeStruct((8, D), F32)],
        compiler_params=_params())(dout, x, dq, dk, dv, du, prm, wing)


def _even_out_fwd(x, att, pool, prm, woutg, name, tm=512):
    L, D = x.shape
    W = D // 2
    tm = min(tm, L)

    def body(x_ref, a_ref, p_ref, prm_ref, w_ref, xo_ref, y_ref):
        yv = _dot(a_ref[...], w_ref[0:W, :]) + _dot(p_ref[...], w_ref[W:2 * W, :])
        y_ref[...] = yv.astype(BF16)
        xo_ref[...] = x_ref[...] + prm_ref[2:3, :] * yv

    return pl.pallas_call(
        body, name=name, grid=(L // tm,),
        in_specs=[_tile(tm, D), _tile(tm, W), _tile(tm, W), _full((8, D)),
                  _resident((None, D, D), lambda i: (0, 0, 0))],
        out_specs=[_tile(tm, D), _tile(tm, D)],
        out_shape=[jax.ShapeDtypeStruct((L, D), F32), jax.ShapeDtypeStruct((L, D), BF16)],
        compiler_params=_params())(x, att, pool, prm, woutg)


def _even_out_bwd(dout, y, prm, woutg, name, tm=512):
    L, D = dout.shape
    W = D // 2
    tm = min(tm, L)

    def body(do_ref, y_ref, p_ref, w_ref, dy_ref, da_ref, dp_ref, acc_ref):
        i = pl.program_id(0)
        do = do_ref[...]
        dgate = _sum0(do * y_ref[...].astype(F32))
        dyb = (p_ref[2:3, :] * do).astype(BF16)
        dy_ref[...] = dyb
        da_ref[...] = _dot_nt(dyb, w_ref[0:W, :]).astype(BF16)
        dp_ref[...] = _dot_nt(dyb, w_ref[W:2 * W, :])
        _acc_rows(acc_ref, i == 0, [None, None, dgate])

    return pl.pallas_call(
        body, name=name, grid=(L // tm,),
        in_specs=[_tile(tm, D), _tile(tm, D), _full((8, D)), _resident((None, D, D), lambda i: (0, 0, 0))],
        out_specs=[_tile(tm, D), _tile(tm, W), _tile(tm, W), _full((8, D))],
        out_shape=[jax.ShapeDtypeStruct((L, D), BF16), jax.ShapeDtypeStruct((L, W), BF16),
                   jax.ShapeDtypeStruct((L, W), F32), jax.ShapeDtypeStruct((8, D), F32)],
        compiler_params=_params())(dout, y, prm, woutg)


def _group_ri(variant, qr, kr):
    first_key = (0, qr, GK - NA_KH)[variant]
    if not first_key <= kr < first_key + NA_KH:
        return None
    return kr - qr + (NA_KH - 1, NA_KH - 1 - NA_KH // 2, NA_KH - 1 - (GK - GQ))[variant]


def _bias_table(rpb, name):
    H = rpb.shape[0]
    nri, nci = 2 * NA_KH - 1, 2 * NA_KW - 1
    col = jnp.arange(GRID_W)
    rel = (col[None, :] - col[:, None] + (NA_KW - 1)).reshape(1, -1)
    onehot = (rel == jnp.arange(32)[:, None]).astype(F32)
    cs = jnp.clip(col - NA_KW // 2, 0, GRID_W - NA_KW)
    ok = ((col[None, :] >= cs[:, None]) & (col[None, :] < cs[:, None] + NA_KW)).astype(F32).reshape(1, -1)
    rpb2 = jnp.pad(rpb.reshape(H * nri, nci), ((0, 0), (0, 32 - nci)))

    def body(r_ref, e_ref, m_ref, o_ref):
        t = jnp.dot(r_ref[...], e_ref[...], preferred_element_type=F32, precision=lax.Precision.HIGHEST)
        o_ref[...] = jnp.where(m_ref[...] > 0.0, t, NEG_INF)

    tab = pl.pallas_call(body, name=name, out_shape=jax.ShapeDtypeStruct((H * nri, GRID_W * GRID_W), F32))(
        rpb2, onehot, ok)
    tab = tab.reshape(H, nri, GRID_W, GRID_W)
    outside = jnp.full((H, GRID_W, GRID_W), NEG_INF, F32)
    variants = []
    for variant in range(3):
        rows = []
        for qr in range(GQ):
            ris = [_group_ri(variant, qr, kr) for kr in range(GK)]
            rows.append(jnp.concatenate([outside if ri is None else tab[:, ri] for ri in ris], axis=2))
        variants.append(jnp.concatenate(rows, axis=1))
    return jnp.stack(variants, axis=1)


def _attn_probs(q, kw, kc, bias, scale):
    s_w = _dot_nt(q, kw) * scale + bias
    s_c = _dot_nt(q, kc) * scale
    m = jnp.maximum(jnp.max(s_w, axis=-1, keepdims=True), jnp.max(s_c, axis=-1, keepdims=True))
    e_w = jnp.exp(s_w - m)
    e_c = jnp.exp(s_c - m)
    inv = 1.0 / (jnp.sum(e_w, axis=-1, keepdims=True) + jnp.sum(e_c, axis=-1, keepdims=True))
    return e_w * inv, e_c * inv


def _group_place(g, R):
    G = R // GQ
    kb = jnp.clip(g * GQ - NA_KH // 2, 0, R - GK)
    variant = jnp.where(g == 0, 0, jnp.where(g == G - 1, 2, 1))
    return pl.multiple_of(g * (GQ * GRID_W), GQ * GRID_W), pl.multiple_of(kb * GRID_W, GRID_W), variant


def _lane_masks(width, dh):
    lane = lax.broadcasted_iota(jnp.int32, (1, width), 1)
    return [(lane >= h * dh) & (lane < (h + 1) * dh) for h in range(width // dh)]


def _only(mask, a):
    return jnp.where(mask, a, jnp.zeros_like(a))


def _attn_fwd(q, k, v, kc, vc, bias, name, ride=None):
    L, width = q.shape
    C = kc.shape[0]
    dh = NA_HEAD_DIM
    lanes = 128
    hpb = lanes // dh
    R = L // GRID_W
    nq, nk = GQ * GRID_W, GK * GRID_W
    scale = dh ** -0.5

    def body(q_ref, k_ref, v_ref, kc_ref, vc_ref, b_ref, o_ref):
        masks = _lane_masks(lanes, dh)
        kc2 = kc_ref[...]
        vcs = [_only(m, vc_ref[...]) for m in masks]

        def group(g, carry):
            q0, k0, variant = _group_place(g, R)
            q2 = q_ref[pl.ds(q0, nq), :]
            k2 = k_ref[pl.ds(k0, nk), :]
            v2 = v_ref[pl.ds(k0, nk), :]
            o2 = jnp.zeros((nq, lanes), F32)
            for h, m in enumerate(masks):
                p_w, p_c = _attn_probs(_only(m, q2), k2, kc2, b_ref[h, variant], scale)
                o2 = o2 + _dot(p_w.astype(BF16), _only(m, v2)) + _dot(p_c.astype(BF16), vcs[h])
            o_ref[pl.ds(q0, nq), :] = o2.astype(BF16)
            return carry

        lax.fori_loop(0, R // GQ, group, 0)

    cols = lambda n: pl.BlockSpec((n, lanes), lambda p: (0, p))
    res, carried = _ride_call(
        body, ride, (q, k, v, kc, vc, bias), name=name, grid=(width // lanes,),
        in_specs=[cols(L), cols(L), cols(L), cols(C), cols(C),
                  pl.BlockSpec((hpb, 3, nq, nk), lambda p: (p, 0, 0, 0))],
        out_specs=[cols(L)], out_shape=[jax.ShapeDtypeStruct((L, width), BF16)],
        compiler_params=_params())
    return res[0], carried


def _attn_bwd(q, k, v, kc, vc, bias, do, name, ride=None):
    L, width = q.shape
    C = kc.shape[0]
    dh = NA_HEAD_DIM
    lanes = 128
    hpb = lanes // dh
    R = L // GRID_W
    nq, nk = GQ * GRID_W, GK * GRID_W
    scale = dh ** -0.5

    def body(q_ref, k_ref, v_ref, kc_ref, vc_ref, b_ref, do_ref, dq_ref, dk_ref, dv_ref, dkc_ref, dvc_ref, db_ref):
        masks = _lane_masks(lanes, dh)
        kc2 = kc_ref[...]
        vc2 = vc_ref[...]
        kcs = [_only(m, kc2) for m in masks]
        dk_ref[...] = jnp.zeros((L, lanes), F32)
        dv_ref[...] = jnp.zeros((L, lanes), F32)
        dkc_ref[...] = jnp.zeros((C, lanes), F32)
        dvc_ref[...] = jnp.zeros((C, lanes), F32)
        db_ref[...] = jnp.zeros((hpb, 3, nq, nk), F32)

        def group(g, carry):
            q0, k0, variant = _group_place(g, R)
            q2 = q_ref[pl.ds(q0, nq), :]
            k2 = k_ref[pl.ds(k0, nk), :]
            v2 = v_ref[pl.ds(k0, nk), :]
            do2 = do_ref[pl.ds(q0, nq), :]
            dq2 = jnp.zeros((nq, lanes), F32)
            dk2 = jnp.zeros((nk, lanes), F32)
            dv2 = jnp.zeros((nk, lanes), F32)
            for h, m in enumerate(masks):
                qh = _only(m, q2)
                doh = _only(m, do2)
                p_w, p_c = _attn_probs(qh, k2, kc2, b_ref[h, variant], scale)
                dp_w = _dot_nt(doh, v2)
                dp_c = _dot_nt(doh, vc2)
                delta = jnp.sum(p_w * dp_w, axis=-1, keepdims=True) + jnp.sum(p_c * dp_c, axis=-1, keepdims=True)
                ds_w = p_w * (dp_w - delta)
                ds_c = p_c * (dp_c - delta)
                db_ref[h, variant] += ds_w
                dsw = (ds_w * scale).astype(BF16)
                dsc = (ds_c * scale).astype(BF16)
                dq2 = dq2 + _dot(dsw, _only(m, k2)) + _dot(dsc, kcs[h])
                dk2 = dk2 + _dot_tn(dsw, qh)
                dv2 = dv2 + _dot_tn(p_w.astype(BF16), doh)
                dkc_ref[...] += _dot_tn(dsc, qh)
                dvc_ref[...] += _dot_tn(p_c.astype(BF16), doh)
            dq_ref[pl.ds(q0, nq), :] = dq2.astype(BF16)
            dk_ref[pl.ds(k0, nk), :] += dk2
            dv_ref[pl.ds(k0, nk), :] += dv2
            return carry

        lax.fori_loop(0, R // GQ, group, 0)

    cols = lambda n: _resident((n, lanes), lambda p: (0, p))
    bspec = _resident((hpb, 3, nq, nk), lambda p: (p, 0, 0, 0))
    res, carried = _ride_call(
        body, ride, (q, k, v, kc, vc, bias, do), name=name, grid=(width // lanes,),
        in_specs=[cols(L), cols(L), cols(L), cols(C), cols(C), bspec, cols(L)],
        out_specs=[cols(L), cols(L), cols(L), cols(C), cols(C), bspec],
        out_shape=[jax.ShapeDtypeStruct((L, width), BF16)] + [jax.ShapeDtypeStruct((L, width), F32)] * 2 +
                  [jax.ShapeDtypeStruct((C, width), F32)] * 2 +
                  [jax.ShapeDtypeStruct((width // dh, 3, nq, nk), F32)],
        compiler_params=_params())
    return (*res, carried)


def _rpb_grad(dbias, name):
    H = dbias.shape[0]
    nri, nci = 2 * NA_KH - 1, 2 * NA_KW - 1
    d6 = dbias.reshape(H, 3, GQ, GRID_W, GK, GRID_W).transpose(0, 1, 2, 4, 3, 5)
    col = jnp.arange(GRID_W)
    onehot = (col[None, None, :] - col[None, :, None] + (NA_KW - 1) == jnp.arange(32)[:, None, None]).astype(F32)
    places = [(v, qr, kr) for v in range(3) for qr in range(GQ) for kr in range(GK)]

    def body(d_ref, m_ref, o_ref, t_ref):
        t_ref[...] = jnp.zeros((32, GRID_W), F32)
        o_ref[...] = jnp.zeros((16, 32, 128), F32)
        for ri in range(nri):
            a = None
            for place in places:
                if _group_ri(*place) == ri:
                    blk = d_ref[place]
                    a = blk if a is None else a + blk
            for ci in range(nci):
                t_ref[ci:ci + 1, :] = _sum0(a * m_ref[ci])
            o_ref[ri] = jnp.broadcast_to(jnp.sum(t_ref[...], axis=1, keepdims=True), (32, 128))

    out = pl.pallas_call(
        body, name=name, grid=(H,),
        in_specs=[pl.BlockSpec((None, 3, GQ, GK, GRID_W, GRID_W), lambda h: (h, 0, 0, 0, 0, 0)),
                  pl.BlockSpec((32, GRID_W, GRID_W), lambda h: (0, 0, 0))],
        out_specs=pl.BlockSpec((None, 16, 32, 128), lambda h: (h, 0, 0, 0)),
        out_shape=jax.ShapeDtypeStruct((H, 16, 32, 128), F32),
        scratch_shapes=[pltpu.VMEM((32, GRID_W), F32)])(d6, onehot)
    return out[:, :nri, :nci, 0]


def _window_count(t, w, L):
    lo = jnp.clip(t - w // 2, 0, L)
    hi = jnp.clip(t - w // 2 + w, 0, L)
    return jnp.maximum(hi - lo, 1).astype(F32)


def _running_sum(v, w):
    k = 1
    while k < w:
        v = v + _shift_rows(v, k)
        k *= 2
    return v


def _pool_fwd(u, poolw, pscale, name, tm=512):
    L, W = u.shape
    G = POOL_GROUP_DIM
    tm = min(tm, L)
    nt = L // tm

    def body(c_ref, p_ref, n_ref, w_ref, s_ref, o_ref, dm_ref):
        i = pl.program_id(0)
        ext = _ext(p_ref[...], c_ref[...], n_ref[...], i, nt)
        t = i * tm + lax.broadcasted_iota(jnp.int32, (tm, 1), 0)
        for g, w in enumerate(POOL_WINDOWS):
            e = ext[:, g * G:(g + 1) * G]
            win = _shift_rows(_running_sum(e, w), -(w // 2 - 1))[HALO:HALO + tm]
            dmx = (win / _window_count(t, w, L) - e[HALO:HALO + tm]).astype(BF16)
            dm_ref[:, g * G:(g + 1) * G] = dmx
            o_ref[:, g * G:(g + 1) * G] = (_dot(dmx, w_ref[g]) * s_ref[:, g * G:(g + 1) * G]).astype(BF16)

    return pl.pallas_call(
        body, name=name, grid=(nt,),
        in_specs=[_tile(tm, W), _halo_prev(tm, W), _halo_next(tm, W, L), _full((4, G, G)), _full((1, W))],
        out_specs=[_tile(tm, W), _tile(tm, W)],
        out_shape=[jax.ShapeDtypeStruct((L, W), BF16)] * 2, compiler_params=_params())(u, u, u, poolw, pscale)


def _pool_bwd(dpool, dmx, poolw, pscale, name, tm=512):
    L, W = dpool.shape
    G = POOL_GROUP_DIM
    tm = min(tm, L)
    nt = L // tm

    def body(c_ref, p_ref, n_ref, dm_ref, w_ref, s_ref, du_ref, dw_ref, acc_ref):
        i = pl.program_id(0)
        ext = _ext(p_ref[...], c_ref[...], n_ref[...], i, nt)
        te = i * tm - HALO + lax.broadcasted_iota(jnp.int32, (tm + 2 * HALO, 1), 0)

        @pl.when(i == 0)
        def _():
            dw_ref[...] = jnp.zeros((4 * G, G), F32)

        rows = []
        for g, w in enumerate(POOL_WINDOWS):
            sc = s_ref[:, g * G:(g + 1) * G]
            dpre = (ext[:, g * G:(g + 1) * G] * sc).astype(BF16)
            dd = _dot_nt(dpre, w_ref[g])
            spread = _shift_rows(_running_sum(dd / _window_count(te, w, L), w), -(w // 2))
            du_ref[:, g * G:(g + 1) * G] = (spread - dd)[HALO:HALO + tm]
            dmx_g = dm_ref[:, g * G:(g + 1) * G]
            rows.append(_sum0(c_ref[:, g * G:(g + 1) * G] * _dot(dmx_g, w_ref[g])))
            dw_ref[g * G:(g + 1) * G, :] += _dot_tn(dmx_g, dpre[HALO:HALO + tm])
        _acc_rows(acc_ref, i == 0, [jnp.concatenate(rows, axis=1)])

    return pl.pallas_call(
        body, name=name, grid=(nt,),
        in_specs=[_tile(tm, W), _halo_prev(tm, W), _halo_next(tm, W, L), _tile(tm, W), _full((4, G, G)),
                  _full((1, W))],
        out_specs=[_tile(tm, W), _full((4 * G, G)), _full((8, W))],
        out_shape=[jax.ShapeDtypeStruct((L, W), F32), jax.ShapeDtypeStruct((4 * G, G), F32),
                   jax.ShapeDtypeStruct((8, W), F32)],
        compiler_params=_params())(dpool, dpool, dpool, dmx, poolw, pscale)


def _conv3(z, cw):
    return _shift_rows(z, 1) * cw[0] + z * cw[1] + _shift_rows(z, -1) * cw[2]


def _conv_fwd(x, prm, wing, woutg, name, tm=512):
    L, D = x.shape
    Ws = wing.shape[-1]
    tm = min(tm, L)
    nt = L // tm
    te = tm + 2 * HALO

    def body(c_ref, p_ref, n_ref, prm_ref, wi_ref, wo_ref, xo_ref, y_ref, b_ref):
        i = pl.program_id(0)
        xe = jnp.concatenate([p_ref[...], c_ref[...], n_ref[...]], axis=0)
        hn, _, _, _ = _nm(xe, prm_ref[3:4, :], prm_ref[0:1, :], prm_ref[1:2, :])
        hb = hn.astype(BF16)
        proj = jnp.concatenate([_dot(hb, wi_ref[s]) for s in range(4)], axis=1)
        bg, cg, xin = proj[:, :D], proj[:, D:2 * D], proj[:, 2 * D:]
        tpos = i * tm - HALO + lax.broadcasted_iota(jnp.int32, (te, 1), 0)
        valid = ((tpos >= 0) & (tpos < L)).astype(F32)
        yc = _conv3(cg * xin * valid, [prm_ref[4 + k:5 + k, :] for k in range(3)])
        h2 = (bg * yc)[HALO:HALO + tm].astype(BF16)
        yv = _dot(h2, wo_ref[...])
        y_ref[...] = yv.astype(BF16)
        xo_ref[...] = c_ref[...] + prm_ref[2:3, :] * yv
        b_ref[...] = proj[HALO:HALO + tm].astype(BF16)

    return pl.pallas_call(
        body, name=name, grid=(nt,),
        in_specs=[_tile(tm, D), _halo_prev(tm, D), _halo_next(tm, D, L), _full((8, D)),
                  _resident((None, 4, D, Ws), lambda i: (0, 0, 0, 0)),
                  _resident((None, D, D), lambda i: (0, 0, 0))],
        out_specs=[_tile(tm, D), _tile(tm, D), _tile(tm, 3 * D)],
        out_shape=[jax.ShapeDtypeStruct((L, D), F32), jax.ShapeDtypeStruct((L, D), BF16),
                   jax.ShapeDtypeStruct((L, 3 * D), BF16)],
        compiler_params=_params())(x, x, x, prm, wing, woutg)


def _conv_bwd(dout, x, y, bcx, prm, wing, woutg, name, tm=256, ride=None):
    L, D = x.shape
    Ws = wing.shape[-1]
    tm = min(tm, L)
    nt = L // tm
    te = tm + 2 * HALO

    def body(dc_ref, dp_ref, dn_ref, x_ref, y_ref, bc_ref, bp_ref, bn_ref, prm_ref, wi_ref, wo_ref,
             dx_ref, dpr_ref, h2_ref, dy_ref, hn_ref, acc_ref):
        i = pl.program_id(0)
        gain, shift, scale, gate = prm_ref[3:4, :], prm_ref[0:1, :], prm_ref[1:2, :], prm_ref[2:3, :]
        taps = [prm_ref[4 + k:5 + k, :] for k in range(3)]
        do = dc_ref[...]
        doe = _ext(dp_ref[...], do, dn_ref[...], i, nt)
        dye = (gate * doe).astype(BF16)
        dy_ref[...] = dye[HALO:HALO + tm]
        dh2 = _dot_nt(dye, wo_ref[...])
        be = jnp.concatenate([bp_ref[...], bc_ref[...], bn_ref[...]], axis=0).astype(F32)
        bg, cg, xin = be[:, :D], be[:, D:2 * D], be[:, 2 * D:]
        tpos = i * tm - HALO + lax.broadcasted_iota(jnp.int32, (te, 1), 0)
        valid = ((tpos >= 0) & (tpos < L)).astype(F32)
        z = cg * xin * valid
        yc = _conv3(z, taps)
        dyc = dh2 * bg
        h2_ref[...] = (bg * yc)[HALO:HALO + tm].astype(BF16)
        dz = _conv3(dyc, taps[::-1]) * valid
        dproj = jnp.concatenate([dh2 * yc, dz * xin, dz * cg], axis=1)[HALO:HALO + tm].astype(BF16)
        dpr_ref[...] = dproj
        dhn = jnp.zeros((tm, D), F32)
        for s in range(4):
            dhn = dhn + _dot_nt(dproj[:, s * Ws:(s + 1) * Ws], wi_ref[s])
        hn, xhat, r, nrm = _nm(x_ref[...], gain, shift, scale)
        hn_ref[...] = hn.astype(BF16)
        dx, dshift, dscale, dgn = _nm_bwd(dhn, xhat, r, nrm, gain, scale)
        dx_ref[...] = do + dx
        dgate = _sum0(do * y_ref[...].astype(F32))
        dtaps = [_sum0((dyc * _shift_rows(z, 1 - k))[HALO:HALO + tm]) for k in range(3)]
        _acc_rows(acc_ref, i == 0, [dshift, dscale, dgate, dgn] + dtaps)

    res, carried = _ride_call(
        body, ride, (dout, dout, dout, x, y, bcx, bcx, bcx, prm, wing, woutg), name=name, grid=(nt,),
        in_specs=[_tile(tm, D), _halo_prev(tm, D), _halo_next(tm, D, L), _tile(tm, D), _tile(tm, D),
                  _tile(tm, 3 * D), _halo_prev(tm, 3 * D), _halo_next(tm, 3 * D, L), _full((8, D)),
                  _resident((None, 4, D, Ws), lambda i: (0, 0, 0, 0)),
                  _resident((None, D, D), lambda i: (0, 0, 0))],
        out_specs=[_tile(tm, D), _tile(tm, 3 * D), _tile(tm, D), _tile(tm, D), _tile(tm, D), _full((8, D))],
        out_shape=[jax.ShapeDtypeStruct((L, D), F32), jax.ShapeDtypeStruct((L, 3 * D), BF16),
                   jax.ShapeDtypeStruct((L, D), BF16), jax.ShapeDtypeStruct((L, D), BF16),
                   jax.ShapeDtypeStruct((L, D), BF16), jax.ShapeDtypeStruct((8, D), F32)],
        compiler_params=_params())
    return (*res, carried)


def _loss_head(x, tgt, fg, name, tm=512):
    L, D = x.shape
    tm = min(tm, L)

    def body(x_ref, t_ref, g_ref, dx_ref, acc_ref):
        i = pl.program_id(0)
        xv = x_ref[...]
        g = g_ref[...]
        r = lax.rsqrt(jnp.mean(xv * xv, axis=-1, keepdims=True) + RMS_EPS)
        xhat = xv * r
        err = xhat * g - t_ref[...]
        part = 0.5 * jnp.sum(jnp.mean(err * err, axis=-1, keepdims=True), axis=0, keepdims=True)
        dy = err * (1.0 / D)
        dxh = dy * g
        dx_ref[...] = r * (dxh - xhat * jnp.mean(dxh * xhat, axis=-1, keepdims=True))
        _acc_rows(acc_ref, i == 0, [_sum0(dy * xhat), jnp.broadcast_to(part, (1, D))])

    return pl.pallas_call(
        body, name=name, grid=(L // tm,), in_specs=[_tile(tm, D), _tile(tm, D), _full((1, D))],
        out_specs=[_tile(tm, D), _full((8, D))],
        out_shape=[jax.ShapeDtypeStruct((L, D), F32), jax.ShapeDtypeStruct((8, D), F32)],
        compiler_params=_params())(x, tgt, fg)


def _mod_fwd(cond, mod_w, mod_b, name, tn=768):
    nl, D, N = mod_w.shape
    tn = min(tn, N)

    def body(c_ref, w_ref, b_ref, o_ref):
        cv = c_ref[...]
        s = (cv * _sigmoid(cv)).astype(BF16)
        o_ref[...] = _dot(s, w_ref[...].astype(BF16)) + b_ref[...]

    return pl.pallas_call(
        body, name=name, grid=(nl, N // tn),
        in_specs=[pl.BlockSpec((16, D), lambda l, j: (0, 0)), pl.BlockSpec((None, D, tn), lambda l, j: (l, 0, j)),
                  pl.BlockSpec((None, 1, tn), lambda l, j: (l, 0, j))],
        out_specs=pl.BlockSpec((None, 16, tn), lambda l, j: (l, 0, j)),
        out_shape=jax.ShapeDtypeStruct((nl, 16, N), F32), compiler_params=_params())(cond, mod_w, mod_b)


def _mod_bwd(cond, dm, mod_w, name, tn=768):
    nl, D, N = mod_w.shape
    tn = min(tn, N)

    def body(c_ref, d_ref, w_ref, dw_ref, dc_ref):
        first = (pl.program_id(0) == 0) & (pl.program_id(1) == 0)
        cv = c_ref[...]
        s = (cv * _sigmoid(cv)).astype(BF16)
        d = d_ref[...].astype(BF16)
        dw_ref[...] = _dot_tn(s, d)

        @pl.when(first)
        def _():
            dc_ref[...] = jnp.zeros((16, D), F32)

        dc_ref[...] += _dot_nt(d, w_ref[...].astype(BF16))

    return pl.pallas_call(
        body, name=name, grid=(nl, N // tn),
        in_specs=[pl.BlockSpec((16, D), lambda l, j: (0, 0)), pl.BlockSpec((None, 16, tn), lambda l, j: (l, 0, j)),
                  pl.BlockSpec((None, D, tn), lambda l, j: (l, 0, j))],
        out_specs=[pl.BlockSpec((None, D, tn), lambda l, j: (l, 0, j)), pl.BlockSpec((16, D), lambda l, j: (0, 0))],
        out_shape=[jax.ShapeDtypeStruct((nl, D, N), F32), jax.ShapeDtypeStruct((16, D), F32)],
        compiler_params=_params())(cond, dm, mod_w)


def _mod_small_grads(dm_all, cond, dsilu_parts, name):
    nl, _, N = dm_all.shape
    D = cond.shape[1]

    def body(d_ref, c_ref, p_ref, db_ref, dc_ref):
        for l in range(nl):
            db_ref[l] = _sum0(d_ref[l])
        tot = p_ref[0, 8:9, :]
        for k in range(1, N_CHIPS):
            tot = tot + p_ref[2 * k, 8:9, :]
        cv = c_ref[8:9, :]
        sg = _sigmoid(cv)
        dc_ref[...] = tot * (sg * (1.0 + cv * (1.0 - sg)))

    return pl.pallas_call(
        body, name=name, out_shape=[jax.ShapeDtypeStruct((nl, 1, N), F32), jax.ShapeDtypeStruct((1, D), F32)],
    )(dm_all, cond, dsilu_parts)


def _prm(rows, D):
    rows = [r.reshape(1, D) for r in rows]
    return jnp.concatenate(rows + [jnp.zeros((8 - len(rows), D), F32)], axis=0)


def kernel(x, c, ctx, c_ctx, mod_w, mod_b, norm_g, ffn_w13, ffn_w2, even_w_in, even_w_out, na_rpb, pool_w, pool_scale, conv_w_in, conv_w, conv_w_out, final_g, loss_target, m_c_ctx, m_mod_w, m_mod_b, m_norm_g, m_ffn_w13, m_ffn_w2, m_even_w_in, m_even_w_out, m_na_rpb, m_pool_w, m_pool_scale, m_conv_w_in, m_conv_w, m_conv_w_out, m_final_g, v_c_ctx, v_mod_w, v_mod_b, v_norm_g, v_ffn_w13, v_ffn_w2, v_even_w_in, v_even_w_out, v_na_rpb, v_pool_w, v_pool_scale, v_conv_w_in, v_conv_w, v_conv_w_out, v_final_g):
    xi, yi, ci = lax.axis_index("x"), lax.axis_index("y"), lax.axis_index("c")
    chip = 2 * xi + yi
    dev = 4 * xi + 2 * yi + ci
    _, L, D = x.shape
    C = ctx.shape[1]
    Ds = D // N_CHIPS
    Nm = mod_w.shape[-1]
    Fh = ffn_w13.shape[-1]
    Fq = ffn_w2.shape[2]
    assert ffn_w13.shape[:2] == (2, 2) and Fh == 2 * Fq and L % (GQ * GRID_W) == 0 and L // GRID_W >= GK and GQ == NA_KH // 2
    x0, ctx0, tgt = x[0], ctx[0], loss_target[0]

    pad = lambda a: jnp.pad(a, ((0, 0), (0, D - a.shape[1])))
    pack1 = jnp.concatenate([c, pad(norm_g.reshape(6, Ds)), pad(conv_w.reshape(3, Ds)), jnp.zeros((6, D), F32)], axis=0)
    g1 = _small_all_gather(pack1, "ag_cond")
    cond = jnp.concatenate([g1[:, 0], c_ctx[None], jnp.zeros((7, D), F32)], axis=0)
    norm_full = jnp.concatenate([g1[2 * k, 1:7, :Ds] for k in range(N_CHIPS)], axis=1).reshape(2, 3, D)
    convw_full = jnp.concatenate([g1[2 * k, 7:10, :Ds] for k in range(N_CHIPS)], axis=1)

    mod_b_loc = lax.dynamic_slice_in_dim(mod_b, chip * Nm, Nm, axis=1).reshape(2, 1, Nm)
    m_loc = _mod_fwd(cond, mod_w, mod_b_loc, "mod_fwd")
    g2 = _small_all_gather(m_loc.reshape(32, Nm), "ag_mod")
    m_all = jnp.concatenate([g2[2 * k] for k in range(N_CHIPS)], axis=1).reshape(2, 16, N_MOD, D)
    m_lat = lax.dynamic_index_in_dim(m_all, dev, axis=1, keepdims=False)
    m_ctx = m_all[:, 8]

    def prm(mods, layer, base, gain_idx, extra=()):
        return _prm([mods[layer, base], mods[layer, base + 1], mods[layer, base + 2], norm_full[layer, gain_idx],
                     *extra], D)

    def shard_bf16(w, name):
        return _cast_bf16(w.reshape(-1, w.shape[-1]), name).reshape(-1, *w.shape[-2:])

    w13s, w2s = shard_bf16(ffn_w13, "cast_w13"), shard_bf16(ffn_w2, "cast_w2")
    eins, eouts = shard_bf16(even_w_in, "cast_ein"), shard_bf16(even_w_out, "cast_eout")
    cins, couts = shard_bf16(conv_w_in, "cast_cin"), shard_bf16(conv_w_out, "cast_cout")
    ffn_shards = [[w13s[t:t + 1], w2s[t:t + 1]] for t in range(4)]

    def ffn_weights(w13g, w2g):
        return w13g.reshape(1, 4, D, Fh), w2g

    wf = [ffn_weights(*_gather_shards(ffn_shards[0], "ag_ffn0")), None, None, None]
    pos = jnp.stack([chip, ci]).astype(jnp.int32)

    p_f1 = prm(m_lat, 0, 0, 0)
    p_mx = prm(m_lat, 0, 3, 1)
    p_f2 = prm(m_lat, 0, 6, 2)
    p_g1 = prm(m_lat, 1, 0, 0)
    p_cv = prm(m_lat, 1, 3, 1, extra=(convw_full[0], convw_full[1], convw_full[2]))
    p_g2 = prm(m_lat, 1, 6, 2)
    pc_f1 = prm(m_ctx, 0, 0, 0)
    pc_mx = prm(m_ctx, 0, 3, 1)

    x1, ab1, y1, (eing, eoutg) = _ffn_fwd(x0, p_f1, *wf[0], 0, "ffn_fwd_l0a", ride=_broadcast_ride([eins, eouts]))
    eing = eing.reshape(1, 4, D, NA_WIDTH)
    ctx1, abc, yc, _ = _ffn_fwd(ctx0, pc_f1, *wf[0], 0, "ffn_fwd_ctx")
    q, k, v, u, hn_mx = _even_in_fwd(x1, p_mx, eing, "even_in_fwd")
    _, k_c, v_c, _, hn_cx = _even_in_fwd(ctx1, pc_mx, eing, "even_in_ctx")
    bias = _bias_table(na_rpb[0], "bias_table")
    att, gathered = _attn_fwd(q, k, v, k_c, v_c, bias, "attn_fwd", ride=_broadcast_ride(ffn_shards[1]))
    wf[1] = ffn_weights(*gathered)
    pw_b = _cast_bf16(pool_w.reshape(-1, POOL_GROUP_DIM), "cast_poolw").reshape(4, POOL_GROUP_DIM, POOL_GROUP_DIM)
    pool, dmx = _pool_fwd(u, pw_b, pool_scale, "pool_fwd")
    x2, ymx = _even_out_fwd(x1, att, pool, p_mx, eoutg, "even_out_fwd")
    x3, ab2, y2, gathered = _ffn_fwd(x2, p_f2, *wf[1], 0, "ffn_fwd_l0b",
                                     ride=_broadcast_ride(ffn_shards[2] + [cins, couts]))
    wf[2] = ffn_weights(*gathered[:2])
    cing, coutg = gathered[2].reshape(1, 4, D, conv_w_in.shape[-1]), gathered[3]
    x4, ab3, y3, gathered = _ffn_fwd(x3, p_g1, *wf[2], 0, "ffn_fwd_l1a", ride=_broadcast_ride(ffn_shards[3]))
    wf[3] = ffn_weights(*gathered)
    x5, ycv, bcx = _conv_fwd(x4, p_cv, cing, coutg, "conv_fwd")
    x6, ab4, y4, _ = _ffn_fwd(x5, p_g2, *wf[3], 0, "ffn_fwd_l1b")
    dx6, acc_head = _loss_head(x6, tgt, final_g.reshape(1, D), "loss_head")
    loss = lax.psum(acc_head[1, 0], ("x", "y", "c"))

    def ffn_back(dout, xin, ab, yy, p, t, tag, init13=None, init2=None, ride=None):
        dx, dab, gact, dy, hn, acc, carried = _ffn_bwd(dout, xin, ab, yy, p, *wf[t], 0, f"ffn_bwd_{tag}", ride=ride)
        dw13 = _mm_tn(hn, dab, 4, False, f"dw13_{tag}", init=init13)
        dw2 = _mm_tn(gact, dy, 2, True, f"dw2_{tag}", init=init2)
        return dx, acc, dw13, dw2, carried

    dx5, acc_g2, dw13_3, dw2_3, _ = ffn_back(dx6, x5, ab4, y4, p_g2, 3, "l1b")
    s_a, sb_a = _pair_sums([dw13_3, dw2_3], pos, "l1b")
    dx4, dproj, h2, dycv, hn_cv, acc_cv, got_a = _conv_bwd(dx5, x4, ycv, bcx, p_cv, cing, coutg, "conv_bwd",
                                                           ride=_scatter_ride(sb_a))
    dcin = _mm_tn(hn_cv, dproj, 4, False, "dw_cin")
    dcout = _mm_tn(h2, dycv, 1, False, "dw_cout")
    s_b, sb_b = _pair_sums([dcin, dcout], pos, "conv")
    dx3, acc_g1, dw13_2, dw2_2, got_b = ffn_back(dx4, x3, ab3, y3, p_g1, 2, "l1a", ride=_scatter_ride(sb_b))
    s_c, sb_c = _pair_sums([dw13_2, dw2_2], pos, "l1a")
    dx2, acc_f2, dw13_1, dw2_1, got_c = ffn_back(dx3, x2, ab2, y2, p_f2, 1, "l0b", ride=_scatter_ride(sb_c))

    dymx, datt, dpool, acc_mxo = _even_out_bwd(dx2, ymx, p_mx, eoutg, "even_out_bwd")
    deout = jnp.concatenate([_mm_tn(att, dymx, 1, False, "dw_eout_att"),
                             _mm_tn(pool, dymx, 1, False, "dw_eout_pool")], axis=0)
    s_d, sb_d = _pair_sums([dw13_1, dw2_1, deout], pos, "l0b")
    du, dpoolw, acc_pool = _pool_bwd(dpool, dmx, pw_b, pool_scale, "pool_bwd")
    dq, dk, dv, dkc, dvc, dbias, got_d = _attn_bwd(q, k, v, k_c, v_c, bias, datt, "attn_bwd",
                                                   ride=_scatter_ride(sb_d))
    drpb = _rpb_grad(dbias, "rpb_grad")
    dx1, dstack, acc_mxi = _even_in_bwd(dx2, x1, dq, dk, dv, du, p_mx, eing,
                                        "even_in_bwd")
    zc = jnp.zeros((C, NA_WIDTH), F32)
    dctx1, dstack_c, accc_mx = _even_in_bwd(jnp.zeros((C, D), F32), ctx1, zc, dkc, dvc, zc,
                                            pc_mx, eing, "even_in_bwd_ctx")
    dein_c = _mm_tn(hn_cx, dstack_c, 4, False, "dw_ein_ctx")
    dein = _mm_tn(hn_mx, dstack, 4, False, "dw_ein", init=dein_c)
    s_e, sb_e = _pair_sums([dein], pos, "ein")
    _, accc_f1, dw13_c, dw2_c, _ = ffn_back(dctx1, ctx0, abc, yc, pc_f1, 0, "ctx")
    dx0, acc_f1, dw13_0, dw2_0, got_e = ffn_back(dx1, x0, ab1, y1, p_f1, 0, "l0a", init13=dw13_c, init2=dw2_c,
                                                 ride=_scatter_ride(sb_e))
    s_f, sb_f = _pair_sums([dw13_0, dw2_0], pos, "l0a")
    got_f = _run_ride(_scatter_ride(sb_f), "rs_scatter_l0a")

    z1 = jnp.zeros((1, D), F32)
    dm_lat = jnp.concatenate([acc_f1[0:3], acc_mxi[0:2], acc_mxo[2:3], acc_f2[0:3],
                              acc_g1[0:3], acc_cv[0:3], acc_g2[0:3]], axis=0)
    dm_ctx = jnp.concatenate([accc_f1[0:3], accc_mx[0:2]] + [z1] * 13, axis=0)
    dnorm = jnp.concatenate([acc_f1[3:4] + accc_f1[3:4], acc_mxi[3:4] + accc_mx[3:4], acc_f2[3:4],
                             acc_g1[3:4], acc_cv[3:4], acc_g2[3:4]], axis=0)
    rpb_flat = jnp.pad(drpb.reshape(-1), (0, 4 * D - drpb.size)).reshape(4, D)
    pack3 = jnp.concatenate([dm_lat, dm_ctx, dnorm, acc_cv[4:7], acc_head[0:1], pad(acc_pool[0:1]), z1,
                             dpoolw.reshape(-1, D), rpb_flat, jnp.zeros((4, D), F32)], axis=0)
    g3 = _small_all_gather(pack3, "ag_small")
    tot = _sum_devices(g3, "sum_small")
    dm_all = jnp.concatenate([g3[:, 0:18].reshape(8, 2, N_MOD * D).transpose(1, 0, 2),
                              tot[18:36].reshape(2, 1, N_MOD * D), jnp.zeros((2, 7, N_MOD * D), F32)], axis=1)
    dm_loc = lax.dynamic_slice_in_dim(dm_all, chip * Nm, Nm, axis=2)
    g_mod_w, dsilu = _mod_bwd(cond, dm_loc, mod_w, "mod_bwd")
    g4 = _small_all_gather(dsilu, "ag_dsilu")
    g_mod_b, g_c_ctx = _mod_small_grads(dm_all, cond, g4, "mod_small")
    g_mod_b = g_mod_b.reshape(2, N_MOD * D)
    g_c_ctx = g_c_ctx.reshape(D)
    g_norm_full = tot[36:42].reshape(2, 3, D)
    g_norm = lax.dynamic_slice_in_dim(g_norm_full, chip * Ds, Ds, axis=2)
    g_conv_w = lax.dynamic_slice_in_dim(tot[42:45], chip * Ds, Ds, axis=1).reshape(1, 3, Ds)
    g_final = tot[45]
    g_pscale = tot[46:47, :pool_scale.shape[1]]
    g_poolw = tot[48:112].reshape(pool_w.shape)
    g_rpb = tot[112:116].reshape(-1)[:na_rpb.size].reshape(na_rpb.shape)

    r13_3, r2_3 = _joins(s_a, got_a, pos, "l1b")
    r_cin, r_cout = _joins(s_b, got_b, pos, "conv")
    r13_2, r2_2 = _joins(s_c, got_c, pos, "l1a")
    r13_1, r2_1, r_eout = _joins(s_d, got_d, pos, "l0b")
    (r_ein,) = _joins(s_e, got_e, pos, "ein")
    r13_0, r2_0 = _joins(s_f, got_f, pos, "l0a")
    g_w13 = jnp.stack([r13_0, r13_1, r13_2, r13_3]).reshape(ffn_w13.shape)
    g_w2 = jnp.stack([r2_0, r2_1, r2_2, r2_3]).reshape(ffn_w2.shape)
    g_ein, g_eout, g_cin, g_cout = r_ein[None], r_eout[None], r_cin[None], r_cout[None]

    grads = [g_c_ctx, g_mod_w, g_mod_b, g_norm, g_w13, g_w2, g_ein, g_eout, g_rpb, g_poolw, g_pscale, g_cin,
             g_conv_w, g_cout, g_final]
    weights = [c_ctx, mod_w, mod_b, norm_g, ffn_w13, ffn_w2, even_w_in, even_w_out, na_rpb, pool_w, pool_scale,
               conv_w_in, conv_w, conv_w_out, final_g]
    ms = [m_c_ctx, m_mod_w, m_mod_b, m_norm_g, m_ffn_w13, m_ffn_w2, m_even_w_in, m_even_w_out, m_na_rpb, m_pool_w,
          m_pool_scale, m_conv_w_in, m_conv_w, m_conv_w_out, m_final_g]
    vs = [v_c_ctx, v_mod_w, v_mod_b, v_norm_g, v_ffn_w13, v_ffn_w2, v_even_w_in, v_even_w_out, v_na_rpb, v_pool_w,
          v_pool_scale, v_conv_w_in, v_conv_w, v_conv_w_out, v_final_g]
    names = ["c_ctx", "mod_w", "mod_b", "norm_g", "ffn_w13", "ffn_w2", "even_w_in", "even_w_out", "na_rpb", "pool_w",
             "pool_scale", "conv_w_in", "conv_w", "conv_w_out", "final_g"]
    deltas, new_m, new_v = [], [], []
    for n, w, g, m, vv in zip(names, weights, grads, ms, vs):
        g = g.reshape(w.shape)
        if w.ndim == 1:
            d, mn, vn = (t.reshape(w.shape) for t in _adamw(w[None], g[None], m[None], vv[None], f"adamw_{n}"))
        else:
            d, mn, vn = _adamw(w, g, m, vv, f"adamw_{n}")
        deltas.append(d)
        new_m.append(mn)
        new_v.append(vn)
    grads = [g.reshape(w.shape) for g, w in zip(grads, weights)]
    return (loss, dx0[None], *grads, *deltas, *new_m, *new_v)
```

```python
import jax
import jax.numpy as jnp
from jax import lax
from jax.experimental import pallas as pl
from jax.experimental.pallas import tpu as pltpu

F32 = jnp.float32
BF16 = jnp.bfloat16
MESH = pl.DeviceIdType.MESH

GRID_W = 64
NA_HEADS = 8
NA_HEAD_DIM = 64
NA_KH = 8
NA_KW = 16
GQ = 4
GK = GQ + NA_KH
NA_WIDTH = NA_HEADS * NA_HEAD_DIM
POOL_WINDOWS = (2, 4, 8, 16)
POOL_GROUP_DIM = 128
N_MOD = 9
RMS_EPS = 1e-6
NEG_INF = -1e30
ADAM_LR, ADAM_B1, ADAM_B2, ADAM_EPS, ADAM_WD, ADAM_STEP = 0.001, 0.9, 0.999, 1e-08, 0.01, 10

HALO = 16
VMEM_LIMIT = 56 * 1024 * 1024
N_CHIPS = 4
N_DEV = 8


def _dot(a, b):
    return jnp.dot(a, b, preferred_element_type=F32)


def _dot_nt(a, b):
    return lax.dot_general(a, b, (((1,), (1,)), ((), ())), preferred_element_type=F32)


def _dot_tn(a, b):
    return lax.dot_general(a, b, (((0,), (0,)), ((), ())), preferred_element_type=F32)


def _sigmoid(a):
    return 1.0 / (1.0 + jnp.exp(-a))


def _sum0(v):
    return jnp.sum(v, axis=0, keepdims=True)


def _nm(x, g, shift, scale):
    r = lax.rsqrt(jnp.mean(x * x, axis=-1, keepdims=True) + RMS_EPS)
    xhat = x * r
    nrm = xhat * g
    return nrm * (1.0 + scale) + shift, xhat, r, nrm


def _nm_bwd(dhn, xhat, r, nrm, g, scale):
    dshift = _sum0(dhn)
    dscale = _sum0(dhn * nrm)
    dnrm = dhn * (1.0 + scale)
    dgn = _sum0(dnrm * xhat)
    dxh = dnrm * g
    dx = r * (dxh - xhat * jnp.mean(dxh * xhat, axis=-1, keepdims=True))
    return dx, dshift, dscale, dgn


def _acc_rows(acc_ref, first, rows):
    @pl.when(first)
    def _():
        acc_ref[...] = jnp.zeros(acc_ref.shape, acc_ref.dtype)
    for k, row in enumerate(rows):
        if row is not None:
            acc_ref[k:k + 1, :] += row


def _shift_rows(v, k):
    n = v.shape[0]
    k = k % n
    return v if k == 0 else pltpu.roll(v, k, 0)


def _tile(tm, w):
    return pl.BlockSpec((tm, w), lambda i: (i, 0))


def _full(shape):
    nd = len(shape)
    return pl.BlockSpec(shape, lambda i: (0,) * nd)


def _resident(block, imap):
    return pl.BlockSpec(block, imap, pipeline_mode=pl.Buffered(1))


def _halo_prev(tm, w):
    return pl.BlockSpec((HALO, w), lambda i: (jnp.maximum(i * (tm // HALO) - 1, 0), 0))


def _halo_next(tm, w, L):
    return pl.BlockSpec((HALO, w), lambda i: (jnp.minimum((i + 1) * (tm // HALO), L // HALO - 1), 0))


def _params(vmem=VMEM_LIMIT):
    return pltpu.CompilerParams(vmem_limit_bytes=vmem)


def _pick_rows(rows, cols, itemsize=4, target=1 << 20):
    best = None
    for t in range(8, rows + 1, 8):
        if rows % t == 0 and t * cols * itemsize <= target:
            best = t
    return best if best is not None else rows


def _ext(prev, cur, nxt, i, nt):
    prev = jnp.where(i > 0, prev, jnp.zeros_like(prev))
    nxt = jnp.where(i < nt - 1, nxt, jnp.zeros_like(nxt))
    return jnp.concatenate([prev, cur, nxt], axis=0)


def _cast_bf16(a2d, name):
    rows, cols = a2d.shape
    tr = _pick_rows(rows, cols)

    def body(a_ref, o_ref):
        o_ref[...] = a_ref[...].astype(BF16)

    return pl.pallas_call(
        body, name=name, grid=(rows // tr,), in_specs=[_tile(tr, cols)], out_specs=_tile(tr, cols),
        out_shape=jax.ShapeDtypeStruct((rows, cols), BF16))(a2d)


def _sum_devices(g, name):
    n, rows, cols = g.shape
    tr = _pick_rows(rows, cols, target=1 << 18)

    def body(g_ref, o_ref):
        s = g_ref[0]
        for d in range(1, n):
            s = s + g_ref[d]
        o_ref[...] = s

    return pl.pallas_call(
        body, name=name, grid=(rows // tr,), in_specs=[pl.BlockSpec((n, tr, cols), lambda i: (0, i, 0))],
        out_specs=_tile(tr, cols), out_shape=jax.ShapeDtypeStruct((rows, cols), F32))(g)


def _adamw(w, g, m, v, name):
    shape = w.shape
    cols = shape[-1]
    rows = w.size // cols
    w2, g2, m2, v2 = (t.reshape(rows, cols) for t in (w, g, m, v))
    tr = _pick_rows(rows, cols)
    c1 = 1.0 - ADAM_B1 ** ADAM_STEP
    c2 = 1.0 - ADAM_B2 ** ADAM_STEP

    def body(w_ref, g_ref, m_ref, v_ref, d_ref, mo_ref, vo_ref):
        gg = g_ref[...]
        mn = ADAM_B1 * m_ref[...] + (1.0 - ADAM_B1) * gg
        vn = ADAM_B2 * v_ref[...] + (1.0 - ADAM_B2) * (gg * gg)
        d_ref[...] = -ADAM_LR * ((mn / c1) / (jnp.sqrt(vn / c2) + ADAM_EPS) + ADAM_WD * w_ref[...])
        mo_ref[...] = mn
        vo_ref[...] = vn

    outs = pl.pallas_call(
        body, name=name, grid=(rows // tr,), in_specs=[_tile(tr, cols)] * 4, out_specs=[_tile(tr, cols)] * 3,
        out_shape=[jax.ShapeDtypeStruct((rows, cols), F32)] * 3)(w2, g2, m2, v2)
    return tuple(o.reshape(shape) for o in outs)


def _mesh_pos():
    x, y, c = lax.axis_index("x"), lax.axis_index("y"), lax.axis_index("c")
    chips = [(1 - x, y), (x, 1 - y), (1 - x, 1 - y)]
    return x, y, c, chips


def _hbm_specs(n):
    return [pl.BlockSpec(memory_space=pltpu.HBM)] * n


def _small_all_gather(v, name):
    rows, w = v.shape

    def body(x_ref, out_ref, send_sems, recv_sems, local_sem):
        x, y, c, chips = _mesh_pos()
        me, sibling = (x, y, c), (x, y, 1 - c)

        def blk(px, py, pc):
            return out_ref.at[4 * px + 2 * py + pc]

        def copy(k, block, to, src=None):
            return pltpu.make_async_remote_copy(
                src_ref=blk(*block) if src is None else src, dst_ref=blk(*block),
                send_sem=send_sems.at[k], recv_sem=recv_sems.at[k], device_id=to, device_id_type=MESH)

        mine = pltpu.make_async_copy(x_ref, blk(*me), local_sem)
        mine.start()
        first = [copy(0, me, sibling, src=x_ref)]
        first += [copy(1 + j, me, (*chip, c), src=x_ref) for j, chip in enumerate(chips)]
        for cp in first:
            cp.start()
        passed = [copy(4 + j, (*chip, c), sibling) for j, chip in enumerate(chips)]
        for j, chip in enumerate(chips):
            copy(1 + j, (*chip, c), me).wait_recv()
            passed[j].start()
        copy(0, sibling, me).wait_recv()
        for j, chip in enumerate(chips):
            copy(4 + j, (*chip, 1 - c), me).wait_recv()
        for cp in first + passed:
            cp.wait_send()
        mine.wait()

    return pl.pallas_call(
        body, name=name, out_shape=jax.ShapeDtypeStruct((N_DEV, rows, w), v.dtype),
        in_specs=[pl.BlockSpec(memory_space=pltpu.VMEM)], out_specs=pl.BlockSpec(memory_space=pltpu.VMEM),
        scratch_shapes=[pltpu.SemaphoreType.DMA((7,)), pltpu.SemaphoreType.DMA((7,)), pltpu.SemaphoreType.DMA],
    )(v)


def _gather_shards(shards, name):
    n = len(shards)

    def body(*refs):
        ins, outs = refs[:n], refs[n:2 * n]
        send_sems, recv_sems, local_sems = refs[2 * n:]
        x, y, c, chips = _mesh_pos()
        k = 2 * x + y
        sibling = (x, y, 1 - c)

        def window(t, chip_k, half):
            r = ins[t].shape[1]
            return outs[t].at[:, pl.ds(chip_k * r + half * (r // 2), r // 2), :]

        def copy(t, j, chip_k, half, to, src=None):
            return pltpu.make_async_remote_copy(
                src_ref=window(t, chip_k, half) if src is None else src, dst_ref=window(t, chip_k, half),
                send_sem=send_sems.at[6 * t + j], recv_sem=recv_sems.at[6 * t + j], device_id=to, device_id_type=MESH)

        started, local = [], []
        for t in range(n):
            r = ins[t].shape[1]
            lc = pltpu.make_async_copy(ins[t], outs[t].at[:, pl.ds(k * r, r), :], local_sems.at[t])
            lc.start()
            local.append(lc)
            src = ins[t].at[:, pl.ds(c * (r // 2), r // 2), :]
            for j, chip in enumerate(chips):
                cp = copy(t, j, k, c, (*chip, c), src=src)
                cp.start()
                started.append(cp)
        for t in range(n):
            for j, chip in enumerate(chips):
                kj = 2 * chip[0] + chip[1]
                copy(t, j, kj, c, sibling).wait_recv()
                cp = copy(t, 3 + j, kj, c, sibling)
                cp.start()
                started.append(cp)
        for t in range(n):
            for j, chip in enumerate(chips):
                kj = 2 * chip[0] + chip[1]
                copy(t, 3 + j, kj, 1 - c, sibling).wait_recv()
        for cp in started:
            cp.wait_send()
        for lc in local:
            lc.wait()

    out_shape = [jax.ShapeDtypeStruct((s.shape[0], N_CHIPS * s.shape[1], s.shape[2]), s.dtype) for s in shards]
    return pl.pallas_call(
        body, name=name, out_shape=out_shape, in_specs=_hbm_specs(n), out_specs=_hbm_specs(n),
        scratch_shapes=[pltpu.SemaphoreType.DMA((6 * n,)), pltpu.SemaphoreType.DMA((6 * n,)),
                        pltpu.SemaphoreType.DMA((n,))],
    )(*shards)


def _chunk_rows(h, w):
    best = 16
    for t in range(16, h + 1, 16):
        if h % t == 0 and t * w * 4 <= (1 << 20):
            best = t
    return best


def _pair_sum(part, pos, name):
    _, h, w = part.shape
    cr = _chunk_rows(h, w)
    nc = h // cr
    n = 4 * nc
    slots = 4

    def body(pos_ref, own_ref, send_ref, s_ref, sb_ref, rbuf, send_sems, recv_sems):
        x, y, c, _ = _mesh_pos()
        k = pl.program_id(0)

        def copy(slot):
            return pltpu.make_async_remote_copy(
                src_ref=send_ref, dst_ref=rbuf.at[slot], send_sem=send_sems.at[slot], recv_sem=recv_sems.at[slot],
                device_id=(x, y, 1 - c), device_id_type=MESH)

        @pl.when(k < n)
        def _():
            copy(k % slots).start()

        @pl.when(k > 0)
        def _():
            before = (k + slots - 1) % slots
            copy(before).wait_recv()
            s = own_ref[...] + rbuf[before]
            s_ref[...] = s
            sb_ref[...] = s.astype(BF16)

        @pl.when(k < n)
        def _():
            copy(k % slots).wait_send()

    def own(k, p):
        j = jnp.maximum(k - 1, 0)
        return ((2 * (j // nc) + p[1]) * nc + j % nc, 0)

    def send(k, p):
        j = jnp.minimum(k, n - 1)
        return ((2 * (j // nc) + 1 - p[1]) * nc + j % nc, 0)

    grid_spec = pltpu.PrefetchScalarGridSpec(
        num_scalar_prefetch=1, grid=(n + 1,),
        in_specs=[pl.BlockSpec((cr, w), own), pl.BlockSpec((cr, w), send)],
        out_specs=[pl.BlockSpec((cr, w), lambda k, p: (jnp.maximum(k - 1, 0), 0))] * 2,
        scratch_shapes=[pltpu.VMEM((slots, cr, w), F32), pltpu.SemaphoreType.DMA((slots,)),
                        pltpu.SemaphoreType.DMA((slots,))])
    part2 = part.reshape(8 * h, w)
    s, sb = pl.pallas_call(
        body, name=name, grid_spec=grid_spec,
        out_shape=[jax.ShapeDtypeStruct((4 * h, w), F32), jax.ShapeDtypeStruct((4 * h, w), BF16)],
    )(pos, part2, part2)
    return s.reshape(4, h, w), sb.reshape(4, h, w)


class _Ride:
    def __init__(self, ins, out_shape, sems, copies):
        self.ins, self.out_shape, self.sems, self.copies = list(ins), list(out_shape), list(sems), copies

    def start(self, ins, outs, sems):
        sends, _, _, local = self.copies(ins, outs, sems)
        for cp in local + sends:
            cp.start()

    def finish(self, ins, outs, sems):
        _, recvs, sends, local = self.copies(ins, outs, sems)
        for cp in recvs:
            cp.wait_recv()
        for cp in sends:
            cp.wait_send()
        for cp in local:
            cp.wait()


def _scatter_ride(sums_bf16):
    n = len(sums_bf16)

    def copies(ins, outs, sems):
        send_sems, recv_sems = sems
        x, y, c, chips = _mesh_pos()
        cps = [pltpu.make_async_remote_copy(
            src_ref=ins[t].at[2 * chip[0] + chip[1]], dst_ref=outs[t].at[j],
            send_sem=send_sems.at[3 * t + j], recv_sem=recv_sems.at[3 * t + j],
            device_id=(*chip, c), device_id_type=MESH) for t in range(n) for j, chip in enumerate(chips)]
        return cps, cps, cps, []

    return _Ride(sums_bf16, [jax.ShapeDtypeStruct((3,) + s.shape[1:], BF16) for s in sums_bf16],
                 [pltpu.SemaphoreType.DMA((3 * n,)), pltpu.SemaphoreType.DMA((3 * n,))], copies)


def _broadcast_ride(shards):
    n = len(shards)

    def copies(ins, outs, sems):
        send_sems, recv_sems, local_sems = sems
        x, y, c, chips = _mesh_pos()
        k = 2 * x + y
        sends, recvs, local = [], [], []
        for t in range(n):
            r = ins[t].shape[1]
            h = r // 2
            local.append(pltpu.make_async_copy(ins[t], outs[t].at[:, pl.ds(k * r, r), :], local_sems.at[t]))
            src = ins[t].at[:, pl.ds(c * h, h), :]
            mine = outs[t].at[:, pl.ds(k * r + c * h, h), :]
            for j, chip in enumerate(chips):
                kj = 2 * chip[0] + chip[1]
                for d in range(2):
                    sends.append(pltpu.make_async_remote_copy(
                        src_ref=src, dst_ref=mine, send_sem=send_sems.at[6 * t + 2 * j + d],
                        recv_sem=recv_sems.at[6 * t + 2 * j + c], device_id=(*chip, d), device_id_type=MESH))
                    theirs = outs[t].at[:, pl.ds(kj * r + d * h, h), :]
                    recvs.append(pltpu.make_async_remote_copy(
                        src_ref=theirs, dst_ref=theirs, send_sem=send_sems.at[6 * t + 2 * j + d],
                        recv_sem=recv_sems.at[6 * t + 2 * j + d], device_id=(*chip, d), device_id_type=MESH))
        return sends, recvs, sends, local

    return _Ride(shards, [jax.ShapeDtypeStruct((s.shape[0], N_CHIPS * s.shape[1], s.shape[2]), s.dtype) for s in shards],
                 [pltpu.SemaphoreType.DMA((6 * n,)), pltpu.SemaphoreType.DMA((6 * n,)), pltpu.SemaphoreType.DMA((n,))],
                 copies)


def _run_ride(ride, name):
    ni, no = len(ride.ins), len(ride.out_shape)

    def body(*refs):
        ride.start(refs[:ni], refs[ni:ni + no], refs[ni + no:])
        ride.finish(refs[:ni], refs[ni:ni + no], refs[ni + no:])

    return pl.pallas_call(body, name=name, out_shape=ride.out_shape, in_specs=_hbm_specs(ni), out_specs=_hbm_specs(no),
                          scratch_shapes=ride.sems)(*ride.ins)


def _ride_call(body, ride, args, *, name, grid, in_specs, out_specs, out_shape, compiler_params=None):
    in_specs, out_specs, out_shape = list(in_specs), list(out_specs), list(out_shape)
    if ride is None:
        res = pl.pallas_call(body, name=name, grid=grid, in_specs=in_specs, out_specs=out_specs, out_shape=out_shape,
                             compiler_params=compiler_params)(*args)
        return list(res), []
    ni, no, ri, ro = len(in_specs), len(out_specs), len(ride.ins), len(ride.out_shape)
    last = grid[0] - 1

    def carried(*refs):
        ins, rins = refs[:ni], refs[ni:ni + ri]
        outs, routs = refs[ni + ri:ni + ri + no], refs[ni + ri + no:ni + ri + no + ro]
        sems = refs[ni + ri + no + ro:]

        @pl.when(pl.program_id(0) == 0)
        def _():
            ride.start(rins, routs, sems)

        body(*ins, *outs)

        @pl.when(pl.program_id(0) == last)
        def _():
            ride.finish(rins, routs, sems)

    res = pl.pallas_call(
        carried, name=name, grid=grid, in_specs=in_specs + _hbm_specs(ri), out_specs=out_specs + _hbm_specs(ro),
        out_shape=out_shape + ride.out_shape, scratch_shapes=ride.sems, compiler_params=compiler_params,
    )(*args, *ride.ins)
    return list(res[:no]), list(res[no:])


def _sum_and_join(sums, got, pos, name):
    _, h, w = sums.shape
    cr = _chunk_rows(h, w)

    def body(pos_ref, mine_ref, got_ref, o_ref, ebuf, rbuf, send_sems, recv_sems):
        x, y, c, _ = _mesh_pos()
        slot = pl.program_id(0) % 2
        e = mine_ref[...]
        for j in range(3):
            e = e + got_ref[j].astype(F32)
        ebuf[slot] = e
        cp = pltpu.make_async_remote_copy(
            src_ref=ebuf.at[slot], dst_ref=rbuf.at[slot], send_sem=send_sems.at[slot], recv_sem=recv_sems.at[slot],
            device_id=(x, y, 1 - c), device_id_type=MESH)
        cp.start()
        o_ref[pos_ref[1]] = e
        cp.wait_recv()
        o_ref[1 - pos_ref[1]] = rbuf[slot]
        cp.wait_send()

    grid_spec = pltpu.PrefetchScalarGridSpec(
        num_scalar_prefetch=1, grid=(h // cr,),
        in_specs=[pl.BlockSpec((None, cr, w), lambda i, p: (p[0], i, 0)),
                  pl.BlockSpec((3, cr, w), lambda i, p: (0, i, 0))],
        out_specs=pl.BlockSpec((2, cr, w), lambda i, p: (0, i, 0)),
        scratch_shapes=[pltpu.VMEM((2, cr, w), F32), pltpu.VMEM((2, cr, w), F32),
                        pltpu.SemaphoreType.DMA((2,)), pltpu.SemaphoreType.DMA((2,))])
    return pl.pallas_call(
        body, name=name, grid_spec=grid_spec, out_shape=jax.ShapeDtypeStruct((2, h, w), F32),
    )(pos, sums, got)


def _pair_sums(parts, pos, tag):
    pairs = [_pair_sum(p.reshape(8, p.shape[0] // 8, p.shape[1]), pos, f"rs_pair_{tag}_{t}")
             for t, p in enumerate(parts)]
    return [s for s, _ in pairs], [sb for _, sb in pairs]


def _joins(sums, got, pos, tag):
    out = []
    for t, (s, r) in enumerate(zip(sums, got)):
        full = _sum_and_join(s, r, pos, f"rs_join_{tag}_{t}")
        out.append(full.reshape(2 * full.shape[1], full.shape[2]))
    return out


def _ffn_fwd(x, prm, w13g, w2g, t, name, tm=512, ride=None):
    L, D = x.shape
    Fh = w13g.shape[-1]
    tm = min(tm, L)

    def body(x_ref, p_ref, w13_ref, w2_ref, xo_ref, ab_ref, y_ref):
        xv = x_ref[...]
        hn, _, _, _ = _nm(xv, p_ref[3:4, :], p_ref[0:1, :], p_ref[1:2, :])
        hb = hn.astype(BF16)
        acc = jnp.zeros((tm, D), F32)
        for j in range(2):
            a = _dot(hb, w13_ref[j])
            b = _dot(hb, w13_ref[2 + j])
            ab_ref[:, j * Fh:(j + 1) * Fh] = a.astype(BF16)
            ab_ref[:, (2 + j) * Fh:(3 + j) * Fh] = b.astype(BF16)
            g = (a * _sigmoid(a) * b).astype(BF16)
            acc = acc + _dot(g, w2_ref[j * Fh:(j + 1) * Fh, :])
        y_ref[...] = acc.astype(BF16)
        xo_ref[...] = xv + (0.5 * p_ref[2:3, :]) * acc

    res, carried = _ride_call(
        body, ride, (x, prm, w13g, w2g), name=name, grid=(L // tm,),
        in_specs=[_tile(tm, D), _full((8, D)),
                  _resident((None, 4, D, Fh), lambda i: (t, 0, 0, 0)),
                  _resident((None, 2 * Fh, D), lambda i: (t, 0, 0))],
        out_specs=[_tile(tm, D), _tile(tm, 4 * Fh), _tile(tm, D)],
        out_shape=[jax.ShapeDtypeStruct((L, D), F32), jax.ShapeDtypeStruct((L, 4 * Fh), BF16),
                   jax.ShapeDtypeStruct((L, D), BF16)],
        compiler_params=_params())
    return (*res, carried)


def _ffn_bwd(dout, x, ab, y, prm, w13g, w2g, t, name, tm=256, ride=None):
    L, D = x.shape
    Fh = w13g.shape[-1]
    tm = min(tm, L)

    def body(do_ref, x_ref, ab_ref, y_ref, p_ref, w13_ref, w2_ref, dx_ref, dab_ref, g_ref, dy_ref, hn_ref, acc_ref):
        i = pl.program_id(0)
        do = do_ref[...]
        gain, shift, scale, gate = p_ref[3:4, :], p_ref[0:1, :], p_ref[1:2, :], p_ref[2:3, :]
        hn, xhat, r, nrm = _nm(x_ref[...], gain, shift, scale)
        hn_ref[...] = hn.astype(BF16)
        dgate = 0.5 * _sum0(do * y_ref[...].astype(F32))
        dyb = ((0.5 * gate) * do).astype(BF16)
        dy_ref[...] = dyb
        dhn = jnp.zeros((tm, D), F32)
        for j in range(2):
            dg = _dot_nt(dyb, w2_ref[j * Fh:(j + 1) * Fh, :])
            a = ab_ref[:, j * Fh:(j + 1) * Fh].astype(F32)
            b = ab_ref[:, (2 + j) * Fh:(3 + j) * Fh].astype(F32)
            sg = _sigmoid(a)
            sa = a * sg
            g_ref[:, j * Fh:(j + 1) * Fh] = (sa * b).astype(BF16)
            da = (dg * b * (sg * (1.0 + a * (1.0 - sg)))).astype(BF16)
            db = (dg * sa).astype(BF16)
            dab_ref[:, j * Fh:(j + 1) * Fh] = da
            dab_ref[:, (2 + j) * Fh:(3 + j) * Fh] = db
            dhn = dhn + _dot_nt(da, w13_ref[j]) + _dot_nt(db, w13_ref[2 + j])
        dx, dshift, dscale, dgn = _nm_bwd(dhn, xhat, r, nrm, gain, scale)
        dx_ref[...] = do + dx
        _acc_rows(acc_ref, i == 0, [dshift, dscale, dgate, dgn])

    res, carried = _ride_call(
        body, ride, (dout, x, ab, y, prm, w13g, w2g), name=name, grid=(L // tm,),
        in_specs=[_tile(tm, D), _tile(tm, D), _tile(tm, 4 * Fh), _tile(tm, D), _full((8, D)),
                  _resident((None, 4, D, Fh), lambda i: (t, 0, 0, 0)),
                  _resident((None, 2 * Fh, D), lambda i: (t, 0, 0))],
        out_specs=[_tile(tm, D), _tile(tm, 4 * Fh), _tile(tm, 2 * Fh), _tile(tm, D), _tile(tm, D), _full((8, D))],
        out_shape=[jax.ShapeDtypeStruct((L, D), F32), jax.ShapeDtypeStruct((L, 4 * Fh), BF16),
                   jax.ShapeDtypeStruct((L, 2 * Fh), BF16), jax.ShapeDtypeStruct((L, D), BF16),
                   jax.ShapeDtypeStruct((L, D), BF16), jax.ShapeDtypeStruct((8, D), F32)],
        compiler_params=_params())
    return (*res, carried)


def _mm_tn(a, b, slabs, a_slabbed, name, init=None, tl=1024):
    L = a.shape[0]
    ka = a.shape[1] // slabs if a_slabbed else a.shape[1]
    nb = b.shape[1] if a_slabbed else b.shape[1] // slabs
    tl = min(tl, L)
    has_init = init is not None

    def body(a_ref, b_ref, *rest):
        o_ref = rest[-1]
        step = pl.program_id(1)

        @pl.when(step == 0)
        def _():
            o_ref[...] = rest[0][...] if has_init else jnp.zeros((ka, nb), F32)

        o_ref[...] += _dot_tn(a_ref[...], b_ref[...])

    in_specs = [pl.BlockSpec((tl, ka), (lambda s, l: (l, s)) if a_slabbed else (lambda s, l: (l, 0))),
                pl.BlockSpec((tl, nb), (lambda s, l: (l, 0)) if a_slabbed else (lambda s, l: (l, s)))]
    args = [a, b]
    if has_init:
        in_specs.append(pl.BlockSpec((ka, nb), lambda s, l: (s, 0)))
        args.append(init)
    return pl.pallas_call(
        body, name=name, grid=(slabs, L // tl), in_specs=in_specs,
        out_specs=pl.BlockSpec((ka, nb), lambda s, l: (s, 0)),
        out_shape=jax.ShapeDtypeStruct((slabs * ka, nb), F32), compiler_params=_params())(*args)


def _even_in_fwd(x, prm, wing, name, tm=512):
    L, D = x.shape
    W = wing.shape[-1]
    tm = min(tm, L)

    def body(x_ref, p_ref, w_ref, q_ref, k_ref, v_ref, u_ref, hn_ref):
        hn, _, _, _ = _nm(x_ref[...], p_ref[3:4, :], p_ref[0:1, :], p_ref[1:2, :])
        hb = hn.astype(BF16)
        hn_ref[...] = hb
        q_ref[...] = _dot(hb, w_ref[0]).astype(BF16)
        k_ref[...] = _dot(hb, w_ref[1]).astype(BF16)
        v_ref[...] = _dot(hb, w_ref[2]).astype(BF16)
        u_ref[...] = _dot(hb, w_ref[3])

    return pl.pallas_call(
        body, name=name, grid=(L // tm,),
        in_specs=[_tile(tm, D), _full((8, D)), _resident((None, 4, D, W), lambda i: (0, 0, 0, 0))],
        out_specs=[_tile(tm, W)] * 4 + [_tile(tm, D)],
        out_shape=[jax.ShapeDtypeStruct((L, W), BF16)] * 3 + [jax.ShapeDtypeStruct((L, W), F32),
                                                              jax.ShapeDtypeStruct((L, D), BF16)],
        compiler_params=_params())(x, prm, wing)


def _even_in_bwd(dout, x, dq, dk, dv, du, prm, wing, name, tm=512):
    L, D = x.shape
    W = wing.shape[-1]
    tm = min(tm, L)

    def body(do_ref, x_ref, dq_ref, dk_ref, dv_ref, du_ref, p_ref, w_ref, dx_ref, ds_ref, acc_ref):
        i = pl.program_id(0)
        gain, shift, scale = p_ref[3:4, :], p_ref[0:1, :], p_ref[1:2, :]
        _, xhat, r, nrm = _nm(x_ref[...], gain, shift, scale)
        dhn = jnp.zeros((tm, D), F32)
        for s, ref in enumerate((dq_ref, dk_ref, dv_ref, du_ref)):
            d = ref[...].astype(BF16)
            ds_ref[:, s * W:(s + 1) * W] = d
            dhn = dhn + _dot_nt(d, w_ref[s])
        dx, dshift, dscale, dgn = _nm_bwd(dhn, xhat, r, nrm, gain, scale)
        dx_ref[...] = do_ref[...] + dx
        _acc_rows(acc_ref, i == 0, [dshift, dscale, None, dgn])

    return pl.pallas_call(
        body, name=name, grid=(L // tm,),
        in_specs=[_tile(tm, D), _tile(tm, D)] + [_tile(tm, W)] * 4 +
                 [_full((8, D)), _resident((None, 4, D, W), lambda i: (0, 0, 0, 0))],
        out_specs=[_tile(tm, D), _tile(tm, 4 * W), _full((8, D))],
        out_shape=[jax.ShapeDtypeStruct((L, D), F32), jax.ShapeDtypeStruct((L, 4 * W), BF16),
                   jax.ShapeDtypeStruct((8, D), F32)],
        compiler_params=_params())(dout, x, dq, dk, dv, du, prm, wing)


def _even_out_fwd(x, att, pool, prm, woutg, name, tm=512):
    L, D = x.shape
    W = D // 2
    tm = min(tm, L)

    def body(x_ref, a_ref, p_ref, prm_ref, w_ref, xo_ref, y_ref):
        yv = _dot(a_ref[...], w_ref[0:W, :]) + _dot(p_ref[...], w_ref[W:2 * W, :])
        y_ref[...] = yv.astype(BF16)
        xo_ref[...] = x_ref[...] + prm_ref[2:3, :] * yv

    return pl.pallas_call(
        body, name=name, grid=(L // tm,),
        in_specs=[_tile(tm, D), _tile(tm, W), _tile(tm, W), _full((8, D)),
                  _resident((None, D, D), lambda i: (0, 0, 0))],
        out_specs=[_tile(tm, D), _tile(tm, D)],
        out_shape=[jax.ShapeDtypeStruct((L, D), F32), jax.ShapeDtypeStruct((L, D), BF16)],
        compiler_params=_params())(x, att, pool, prm, woutg)


def _even_out_bwd(dout, y, prm, woutg, name, tm=512):
    L, D = dout.shape
    W = D // 2
    tm = min(tm, L)

    def body(do_ref, y_ref, p_ref, w_ref, dy_ref, da_ref, dp_ref, acc_ref):
        i = pl.program_id(0)
        do = do_ref[...]
        dgate = _sum0(do * y_ref[...].astype(F32))
        dyb = (p_ref[2:3, :] * do).astype(BF16)
        dy_ref[...] = dyb
        da_ref[...] = _dot_nt(dyb, w_ref[0:W, :]).astype(BF16)
        dp_ref[...] = _dot_nt(dyb, w_ref[W:2 * W, :])
        _acc_rows(acc_ref, i == 0, [None, None, dgate])

    return pl.pallas_call(
        body, name=name, grid=(L // tm,),
        in_specs=[_tile(tm, D), _tile(tm, D), _full((8, D)), _resident((None, D, D), lambda i: (0, 0, 0))],
        out_specs=[_tile(tm, D), _tile(tm, W), _tile(tm, W), _full((8, D))],
        out_shape=[jax.ShapeDtypeStruct((L, D), BF16), jax.ShapeDtypeStruct((L, W), BF16),
                   jax.ShapeDtypeStruct((L, W), F32), jax.ShapeDtypeStruct((8, D), F32)],
        compiler_params=_params())(dout, y, prm, woutg)


def _group_ri(variant, qr, kr):
    first_key = (0, qr, GK - NA_KH)[variant]
    if not first_key <= kr < first_key + NA_KH:
        return None
    return kr - qr + (NA_KH - 1, NA_KH - 1 - NA_KH // 2, NA_KH - 1 - (GK - GQ))[variant]


def _bias_table(rpb, name):
    H = rpb.shape[0]
    nri, nci = 2 * NA_KH - 1, 2 * NA_KW - 1
    col = jnp.arange(GRID_W)
    rel = (col[None, :] - col[:, None] + (NA_KW - 1)).reshape(1, -1)
    onehot = (rel == jnp.arange(32)[:, None]).astype(F32)
    cs = jnp.clip(col - NA_KW // 2, 0, GRID_W - NA_KW)
    ok = ((col[None, :] >= cs[:, None]) & (col[None, :] < cs[:, None] + NA_KW)).astype(F32).reshape(1, -1)
    rpb2 = jnp.pad(rpb.reshape(H * nri, nci), ((0, 0), (0, 32 - nci)))

    def body(r_ref, e_ref, m_ref, o_ref):
        t = jnp.dot(r_ref[...], e_ref[...], preferred_element_type=F32, precision=lax.Precision.HIGHEST)
        o_ref[...] = jnp.where(m_ref[...] > 0.0, t, NEG_INF)

    tab = pl.pallas_call(body, name=name, out_shape=jax.ShapeDtypeStruct((H * nri, GRID_W * GRID_W), F32))(
        rpb2, onehot, ok)
    tab = tab.reshape(H, nri, GRID_W, GRID_W)
    outside = jnp.full((H, GRID_W, GRID_W), NEG_INF, F32)
    variants = []
    for variant in range(3):
        rows = []
        for qr in range(GQ):
            ris = [_group_ri(variant, qr, kr) for kr in range(GK)]
            rows.append(jnp.concatenate([outside if ri is None else tab[:, ri] for ri in ris], axis=2))
        variants.append(jnp.concatenate(rows, axis=1))
    return jnp.stack(variants, axis=1)


def _attn_probs(q, kw, kc, bias, scale):
    s_w = _dot_nt(q, kw) * scale + bias
    s_c = _dot_nt(q, kc) * scale
    m = jnp.maximum(jnp.max(s_w, axis=-1, keepdims=True), jnp.max(s_c, axis=-1, keepdims=True))
    e_w = jnp.exp(s_w - m)
    e_c = jnp.exp(s_c - m)
    inv = 1.0 / (jnp.sum(e_w, axis=-1, keepdims=True) + jnp.sum(e_c, axis=-1, keepdims=True))
    return e_w * inv, e_c * inv


def _group_place(g, R):
    G = R // GQ
    kb = jnp.clip(g * GQ - NA_KH // 2, 0, R - GK)
    variant = jnp.where(g == 0, 0, jnp.where(g == G - 1, 2, 1))
    return pl.multiple_of(g * (GQ * GRID_W), GQ * GRID_W), pl.multiple_of(kb * GRID_W, GRID_W), variant


def _lane_masks(width, dh):
    lane = lax.broadcasted_iota(jnp.int32, (1, width), 1)
    return [(lane >= h * dh) & (lane < (h + 1) * dh) for h in range(width // dh)]


def _only(mask, a):
    return jnp.where(mask, a, jnp.zeros_like(a))


def _attn_fwd(q, k, v, kc, vc, bias, name, ride=None):
    L, width = q.shape
    C = kc.shape[0]
    dh = NA_HEAD_DIM
    lanes = 128
    hpb = lanes // dh
    R = L // GRID_W
    nq, nk = GQ * GRID_W, GK * GRID_W
    scale = dh ** -0.5

    def body(q_ref, k_ref, v_ref, kc_ref, vc_ref, b_ref, o_ref):
        masks = _lane_masks(lanes, dh)
        kc2 = kc_ref[...]
        vcs = [_only(m, vc_ref[...]) for m in masks]

        def group(g, carry):
            q0, k0, variant = _group_place(g, R)
            q2 = q_ref[pl.ds(q0, nq), :]
            k2 = k_ref[pl.ds(k0, nk), :]
            v2 = v_ref[pl.ds(k0, nk), :]
            o2 = jnp.zeros((nq, lanes), F32)
            for h, m in enumerate(masks):
                p_w, p_c = _attn_probs(_only(m, q2), k2, kc2, b_ref[h, variant], scale)
                o2 = o2 + _dot(p_w.astype(BF16), _only(m, v2)) + _dot(p_c.astype(BF16), vcs[h])
            o_ref[pl.ds(q0, nq), :] = o2.astype(BF16)
            return carry

        lax.fori_loop(0, R // GQ, group, 0)

    cols = lambda n: pl.BlockSpec((n, lanes), lambda p: (0, p))
    res, carried = _ride_call(
        body, ride, (q, k, v, kc, vc, bias), name=name, grid=(width // lanes,),
        in_specs=[cols(L), cols(L), cols(L), cols(C), cols(C),
                  pl.BlockSpec((hpb, 3, nq, nk), lambda p: (p, 0, 0, 0))],
        out_specs=[cols(L)], out_shape=[jax.ShapeDtypeStruct((L, width), BF16)],
        compiler_params=_params())
    return res[0], carried


def _attn_bwd(q, k, v, kc, vc, bias, do, name, ride=None):
    L, width = q.shape
    C = kc.shape[0]
    dh = NA_HEAD_DIM
    lanes = 128
    hpb = lanes // dh
    R = L // GRID_W
    nq, nk = GQ * GRID_W, GK * GRID_W
    scale = dh ** -0.5

    def body(q_ref, k_ref, v_ref, kc_ref, vc_ref, b_ref, do_ref, dq_ref, dk_ref, dv_ref, dkc_ref, dvc_ref, db_ref):
        masks = _lane_masks(lanes, dh)
        kc2 = kc_ref[...]
        vc2 = vc_ref[...]
        kcs = [_only(m, kc2) for m in masks]
        dk_ref[...] = jnp.zeros((L, lanes), F32)
        dv_ref[...] = jnp.zeros((L, lanes), F32)
        dkc_ref[...] = jnp.zeros((C, lanes), F32)
        dvc_ref[...] = jnp.zeros((C, lanes), F32)
        db_ref[...] = jnp.zeros((hpb, 3, nq, nk), F32)

        def group(g, carry):
            q0, k0, variant = _group_place(g, R)
            q2 = q_ref[pl.ds(q0, nq), :]
            k2 = k_ref[pl.ds(k0, nk), :]
            v2 = v_ref[pl.ds(k0, nk), :]
            do2 = do_ref[pl.ds(q0, nq), :]
            dq2 = jnp.zeros((nq, lanes), F32)
            dk2 = jnp.zeros((nk, lanes), F32)
            dv2 = jnp.zeros((nk, lanes), F32)
            for h, m in enumerate(masks):
                qh = _only(m, q2)
                doh = _only(m, do2)
                p_w, p_c = _attn_probs(qh, k2, kc2, b_ref[h, variant], scale)
                dp_w = _dot_nt(doh, v2)
                dp_c = _dot_nt(doh, vc2)
                delta = jnp.sum(p_w * dp_w, axis=-1, keepdims=True) + jnp.sum(p_c * dp_c, axis=-1, keepdims=True)
                ds_w = p_w * (dp_w - delta)
                ds_c = p_c * (dp_c - delta)
                db_ref[h, variant] += ds_w
                dsw = (ds_w * scale).astype(BF16)
                dsc = (ds_c * scale).astype(BF16)
                dq2 = dq2 + _dot(dsw, _only(m, k2)) + _dot(dsc, kcs[h])
                dk2 = dk2 + _dot_tn(dsw, qh)
                dv2 = dv2 + _dot_tn(p_w.astype(BF16), doh)
                dkc_ref[...] += _dot_tn(dsc, qh)
                dvc_ref[...] += _dot_tn(p_c.astype(BF16), doh)
            dq_ref[pl.ds(q0, nq), :] = dq2.astype(BF16)
            dk_ref[pl.ds(k0, nk), :] += dk2
            dv_ref[pl.ds(k0, nk), :] += dv2
            return carry

        lax.fori_loop(0, R // GQ, group, 0)

    cols = lambda n: _resident((n, lanes), lambda p: (0, p))
    bspec = _resident((hpb, 3, nq, nk), lambda p: (p, 0, 0, 0))
    res, carried = _ride_call(
        body, ride, (q, k, v, kc, vc, bias, do), name=name, grid=(width // lanes,),
        in_specs=[cols(L), cols(L), cols(L), cols(C), cols(C), bspec, cols(L)],
        out_specs=[cols(L), cols(L), cols(L), cols(C), cols(C), bspec],
        out_shape=[jax.ShapeDtypeStruct((L, width), BF16)] + [jax.ShapeDtypeStruct((L, width), F32)] * 2 +
                  [jax.ShapeDtypeStruct((C, width), F32)] * 2 +
                  [jax.ShapeDtypeStruct((width // dh, 3, nq, nk), F32)],
        compiler_params=_params())
    return (*res, carried)


def _rpb_grad(dbias, name):
    H = dbias.shape[0]
    nri, nci = 2 * NA_KH - 1, 2 * NA_KW - 1
    d6 = dbias.reshape(H, 3, GQ, GRID_W, GK, GRID_W).transpose(0, 1, 2, 4, 3, 5)
    col = jnp.arange(GRID_W)
    onehot = (col[None, None, :] - col[None, :, None] + (NA_KW - 1) == jnp.arange(32)[:, None, None]).astype(F32)
    places = [(v, qr, kr) for v in range(3) for qr in range(GQ) for kr in range(GK)]

    def body(d_ref, m_ref, o_ref, t_ref):
        t_ref[...] = jnp.zeros((32, GRID_W), F32)
        o_ref[...] = jnp.zeros((16, 32, 128), F32)
        for ri in range(nri):
            a = None
            for place in places:
                if _group_ri(*place) == ri:
                    blk = d_ref[place]
                    a = blk if a is None else a + blk
            for ci in range(nci):
                t_ref[ci:ci + 1, :] = _sum0(a * m_ref[ci])
            o_ref[ri] = jnp.broadcast_to(jnp.sum(t_ref[...], axis=1, keepdims=True), (32, 128))

    out = pl.pallas_call(
        body, name=name, grid=(H,),
        in_specs=[pl.BlockSpec((None, 3, GQ, GK, GRID_W, GRID_W), lambda h: (h, 0, 0, 0, 0, 0)),
                  pl.BlockSpec((32, GRID_W, GRID_W), lambda h: (0, 0, 0))],
        out_specs=pl.BlockSpec((None, 16, 32, 128), lambda h: (h, 0, 0, 0)),
        out_shape=jax.ShapeDtypeStruct((H, 16, 32, 128), F32),
        scratch_shapes=[pltpu.VMEM((32, GRID_W), F32)])(d6, onehot)
    return out[:, :nri, :nci, 0]


def _window_count(t, w, L):
    lo = jnp.clip(t - w // 2, 0, L)
    hi = jnp.clip(t - w // 2 + w, 0, L)
    return jnp.maximum(hi - lo, 1).astype(F32)


def _running_sum(v, w):
    k = 1
    while k < w:
        v = v + _shift_rows(v, k)
        k *= 2
    return v


def _pool_fwd(u, poolw, pscale, name, tm=512):
    L, W = u.shape
    G = POOL_GROUP_DIM
    tm = min(tm, L)
    nt = L // tm

    def body(c_ref, p_ref, n_ref, w_ref, s_ref, o_ref, dm_ref):
        i = pl.program_id(0)
        ext = _ext(p_ref[...], c_ref[...], n_ref[...], i, nt)
        t = i * tm + lax.broadcasted_iota(jnp.int32, (tm, 1), 0)
        for g, w in enumerate(POOL_WINDOWS):
            e = ext[:, g * G:(g + 1) * G]
            win = _shift_rows(_running_sum(e, w), -(w // 2 - 1))[HALO:HALO + tm]
            dmx = (win / _window_count(t, w, L) - e[HALO:HALO + tm]).astype(BF16)
            dm_ref[:, g * G:(g + 1) * G] = dmx
            o_ref[:, g * G:(g + 1) * G] = (_dot(dmx, w_ref[g]) * s_ref[:, g * G:(g + 1) * G]).astype(BF16)

    return pl.pallas_call(
        body, name=name, grid=(nt,),
        in_specs=[_tile(tm, W), _halo_prev(tm, W), _halo_next(tm, W, L), _full((4, G, G)), _full((1, W))],
        out_specs=[_tile(tm, W), _tile(tm, W)],
        out_shape=[jax.ShapeDtypeStruct((L, W), BF16)] * 2, compiler_params=_params())(u, u, u, poolw, pscale)


def _pool_bwd(dpool, dmx, poolw, pscale, name, tm=512):
    L, W = dpool.shape
    G = POOL_GROUP_DIM
    tm = min(tm, L)
    nt = L // tm

    def body(c_ref, p_ref, n_ref, dm_ref, w_ref, s_ref, du_ref, dw_ref, acc_ref):
        i = pl.program_id(0)
        ext = _ext(p_ref[...], c_ref[...], n_ref[...], i, nt)
        te = i * tm - HALO + lax.broadcasted_iota(jnp.int32, (tm + 2 * HALO, 1), 0)

        @pl.when(i == 0)
        def _():
            dw_ref[...] = jnp.zeros((4 * G, G), F32)

        rows = []
        for g, w in enumerate(POOL_WINDOWS):
            sc = s_ref[:, g * G:(g + 1) * G]
            dpre = (ext[:, g * G:(g + 1) * G] * sc).astype(BF16)
            dd = _dot_nt(dpre, w_ref[g])
            spread = _shift_rows(_running_sum(dd / _window_count(te, w, L), w), -(w // 2))
            du_ref[:, g * G:(g + 1) * G] = (spread - dd)[HALO:HALO + tm]
            dmx_g = dm_ref[:, g * G:(g + 1) * G]
            rows.append(_sum0(c_ref[:, g * G:(g + 1) * G] * _dot(dmx_g, w_ref[g])))
            dw_ref[g * G:(g + 1) * G, :] += _dot_tn(dmx_g, dpre[HALO:HALO + tm])
        _acc_rows(acc_ref, i == 0, [jnp.concatenate(rows, axis=1)])

    return pl.pallas_call(
        body, name=name, grid=(nt,),
        in_specs=[_tile(tm, W), _halo_prev(tm, W), _halo_next(tm, W, L), _tile(tm, W), _full((4, G, G)),
                  _full((1, W))],
        out_specs=[_tile(tm, W), _full((4 * G, G)), _full((8, W))],
        out_shape=[jax.ShapeDtypeStruct((L, W), F32), jax.ShapeDtypeStruct((4 * G, G), F32),
                   jax.ShapeDtypeStruct((8, W), F32)],
        compiler_params=_params())(dpool, dpool, dpool, dmx, poolw, pscale)


def _conv3(z, cw):
    return _shift_rows(z, 1) * cw[0] + z * cw[1] + _shift_rows(z, -1) * cw[2]


def _conv_fwd(x, prm, wing, woutg, name, tm=512):
    L, D = x.shape
    Ws = wing.shape[-1]
    tm = min(tm, L)
    nt = L // tm
    te = tm + 2 * HALO

    def body(c_ref, p_ref, n_ref, prm_ref, wi_ref, wo_ref, xo_ref, y_ref, b_ref):
        i = pl.program_id(0)
        xe = jnp.concatenate([p_ref[...], c_ref[...], n_ref[...]], axis=0)
        hn, _, _, _ = _nm(xe, prm_ref[3:4, :], prm_ref[0:1, :], prm_ref[1:2, :])
        hb = hn.astype(BF16)
        proj = jnp.concatenate([_dot(hb, wi_ref[s]) for s in range(4)], axis=1)
        bg, cg, xin = proj[:, :D], proj[:, D:2 * D], proj[:, 2 * D:]
        tpos = i * tm - HALO + lax.broadcasted_iota(jnp.int32, (te, 1), 0)
        valid = ((tpos >= 0) & (tpos < L)).astype(F32)
        yc = _conv3(cg * xin * valid, [prm_ref[4 + k:5 + k, :] for k in range(3)])
        h2 = (bg * yc)[HALO:HALO + tm].astype(BF16)
        yv = _dot(h2, wo_ref[...])
        y_ref[...] = yv.astype(BF16)
        xo_ref[...] = c_ref[...] + prm_ref[2:3, :] * yv
        b_ref[...] = proj[HALO:HALO + tm].astype(BF16)

    return pl.pallas_call(
        body, name=name, grid=(nt,),
        in_specs=[_tile(tm, D), _halo_prev(tm, D), _halo_next(tm, D, L), _full((8, D)),
                  _resident((None, 4, D, Ws), lambda i: (0, 0, 0, 0)),
                  _resident((None, D, D), lambda i: (0, 0, 0))],
        out_specs=[_tile(tm, D), _tile(tm, D), _tile(tm, 3 * D)],
        out_shape=[jax.ShapeDtypeStruct((L, D), F32), jax.ShapeDtypeStruct((L, D), BF16),
                   jax.ShapeDtypeStruct((L, 3 * D), BF16)],
        compiler_params=_params())(x, x, x, prm, wing, woutg)


def _conv_bwd(dout, x, y, bcx, prm, wing, woutg, name, tm=256, ride=None):
    L, D = x.shape
    Ws = wing.shape[-1]
    tm = min(tm, L)
    nt = L // tm
    te = tm + 2 * HALO

    def body(dc_ref, dp_ref, dn_ref, x_ref, y_ref, bc_ref, bp_ref, bn_ref, prm_ref, wi_ref, wo_ref,
             dx_ref, dpr_ref, h2_ref, dy_ref, hn_ref, acc_ref):
        i = pl.program_id(0)
        gain, shift, scale, gate = prm_ref[3:4, :], prm_ref[0:1, :], prm_ref[1:2, :], prm_ref[2:3, :]
        taps = [prm_ref[4 + k:5 + k, :] for k in range(3)]
        do = dc_ref[...]
        doe = _ext(dp_ref[...], do, dn_ref[...], i, nt)
        dye = (gate * doe).astype(BF16)
        dy_ref[...] = dye[HALO:HALO + tm]
        dh2 = _dot_nt(dye, wo_ref[...])
        be = jnp.concatenate([bp_ref[...], bc_ref[...], bn_ref[...]], axis=0).astype(F32)
        bg, cg, xin = be[:, :D], be[:, D:2 * D], be[:, 2 * D:]
        tpos = i * tm - HALO + lax.broadcasted_iota(jnp.int32, (te, 1), 0)
        valid = ((tpos >= 0) & (tpos < L)).astype(F32)
        z = cg * xin * valid
        yc = _conv3(z, taps)
        dyc = dh2 * bg
        h2_ref[...] = (bg * yc)[HALO:HALO + tm].astype(BF16)
        dz = _conv3(dyc, taps[::-1]) * valid
        dproj = jnp.concatenate([dh2 * yc, dz * xin, dz * cg], axis=1)[HALO:HALO + tm].astype(BF16)
        dpr_ref[...] = dproj
        dhn = jnp.zeros((tm, D), F32)
        for s in range(4):
            dhn = dhn + _dot_nt(dproj[:, s * Ws:(s + 1) * Ws], wi_ref[s])
        hn, xhat, r, nrm = _nm(x_ref[...], gain, shift, scale)
        hn_ref[...] = hn.astype(BF16)
        dx, dshift, dscale, dgn = _nm_bwd(dhn, xhat, r, nrm, gain, scale)
        dx_ref[...] = do + dx
        dgate = _sum0(do * y_ref[...].astype(F32))
        dtaps = [_sum0((dyc * _shift_rows(z, 1 - k))[HALO:HALO + tm]) for k in range(3)]
        _acc_rows(acc_ref, i == 0, [dshift, dscale, dgate, dgn] + dtaps)

    res, carried = _ride_call(
        body, ride, (dout, dout, dout, x, y, bcx, bcx, bcx, prm, wing, woutg), name=name, grid=(nt,),
        in_specs=[_tile(tm, D), _halo_prev(tm, D), _halo_next(tm, D, L), _tile(tm, D), _tile(tm, D),
                  _tile(tm, 3 * D), _halo_prev(tm, 3 * D), _halo_next(tm, 3 * D, L), _full((8, D)),
                  _resident((None, 4, D, Ws), lambda i: (0, 0, 0, 0)),
                  _resident((None, D, D), lambda i: (0, 0, 0))],
        out_specs=[_tile(tm, D), _tile(tm, 3 * D), _tile(tm, D), _tile(tm, D), _tile(tm, D), _full((8, D))],
        out_shape=[jax.ShapeDtypeStruct((L, D), F32), jax.ShapeDtypeStruct((L, 3 * D), BF16),
                   jax.ShapeDtypeStruct((L, D), BF16), jax.ShapeDtypeStruct((L, D), BF16),
                   jax.ShapeDtypeStruct((L, D), BF16), jax.ShapeDtypeStruct((8, D), F32)],
        compiler_params=_params())
    return (*res, carried)


def _loss_head(x, tgt, fg, name, tm=512):
    L, D = x.shape
    tm = min(tm, L)

    def body(x_ref, t_ref, g_ref, dx_ref, acc_ref):
        i = pl.program_id(0)
        xv = x_ref[...]
        g = g_ref[...]
        r = lax.rsqrt(jnp.mean(xv * xv, axis=-1, keepdims=True) + RMS_EPS)
        xhat = xv * r
        err = xhat * g - t_ref[...]
        part = 0.5 * jnp.sum(jnp.mean(err * err, axis=-1, keepdims=True), axis=0, keepdims=True)
        dy = err * (1.0 / D)
        dxh = dy * g
        dx_ref[...] = r * (dxh - xhat * jnp.mean(dxh * xhat, axis=-1, keepdims=True))
        _acc_rows(acc_ref, i == 0, [_sum0(dy * xhat), jnp.broadcast_to(part, (1, D))])

    return pl.pallas_call(
        body, name=name, grid=(L // tm,), in_specs=[_tile(tm, D), _tile(tm, D), _full((1, D))],
        out_specs=[_tile(tm, D), _full((8, D))],
        out_shape=[jax.ShapeDtypeStruct((L, D), F32), jax.ShapeDtypeStruct((8, D), F32)],
        compiler_params=_params())(x, tgt, fg)


def _mod_fwd(cond, mod_w, mod_b, name, tn=768):
    nl, D, N = mod_w.shape
    tn = min(tn, N)

    def body(c_ref, w_ref, b_ref, o_ref):
        cv = c_ref[...]
        s = (cv * _sigmoid(cv)).astype(BF16)
        o_ref[...] = _dot(s, w_ref[...].astype(BF16)) + b_ref[...]

    return pl.pallas_call(
        body, name=name, grid=(nl, N // tn),
        in_specs=[pl.BlockSpec((16, D), lambda l, j: (0, 0)), pl.BlockSpec((None, D, tn), lambda l, j: (l, 0, j)),
                  pl.BlockSpec((None, 1, tn), lambda l, j: (l, 0, j))],
        out_specs=pl.BlockSpec((None, 16, tn), lambda l, j: (l, 0, j)),
        out_shape=jax.ShapeDtypeStruct((nl, 16, N), F32), compiler_params=_params())(cond, mod_w, mod_b)


def _mod_bwd(cond, dm, mod_w, name, tn=768):
    nl, D, N = mod_w.shape
    tn = min(tn, N)

    def body(c_ref, d_ref, w_ref, dw_ref, dc_ref):
        first = (pl.program_id(0) == 0) & (pl.program_id(1) == 0)
        cv = c_ref[...]
        s = (cv * _sigmoid(cv)).astype(BF16)
        d = d_ref[...].astype(BF16)
        dw_ref[...] = _dot_tn(s, d)

        @pl.when(first)
        def _():
            dc_ref[...] = jnp.zeros((16, D), F32)

        dc_ref[...] += _dot_nt(d, w_ref[...].astype(BF16))

    return pl.pallas_call(
        body, name=name, grid=(nl, N // tn),
        in_specs=[pl.BlockSpec((16, D), lambda l, j: (0, 0)), pl.BlockSpec((None, 16, tn), lambda l, j: (l, 0, j)),
                  pl.BlockSpec((None, D, tn), lambda l, j: (l, 0, j))],
        out_specs=[pl.BlockSpec((None, D, tn), lambda l, j: (l, 0, j)), pl.BlockSpec((16, D), lambda l, j: (0, 0))],
        out_shape=[jax.ShapeDtypeStruct((nl, D, N), F32), jax.ShapeDtypeStruct((16, D), F32)],
        compiler_params=_params())(cond, dm, mod_w)


def _mod_small_grads(dm_all, cond, dsilu_parts, name):
    nl, _, N = dm_all.shape
    D = cond.shape[1]

    def body(d_ref, c_ref, p_ref, db_ref, dc_ref):
        for l in range(nl):
            db_ref[l] = _sum0(d_ref[l])
        tot = p_ref[0, 8:9, :]
        for k in range(1, N_CHIPS):
            tot = tot + p_ref[2 * k, 8:9, :]
        cv = c_ref[8:9, :]
        sg = _sigmoid(cv)
        dc_ref[...] = tot * (sg * (1.0 + cv * (1.0 - sg)))

    return pl.pallas_call(
        body, name=name, out_shape=[jax.ShapeDtypeStruct((nl, 1, N), F32), jax.ShapeDtypeStruct((1, D), F32)],
    )(dm_all, cond, dsilu_parts)


def _prm(rows, D):
    rows = [r.reshape(1, D) for r in rows]
    return jnp.concatenate(rows + [jnp.zeros((8 - len(rows), D), F32)], axis=0)


def kernel(x, c, ctx, c_ctx, mod_w, mod_b, norm_g, ffn_w13, ffn_w2, even_w_in, even_w_out, na_rpb, pool_w, pool_scale, conv_w_in, conv_w, conv_w_out, final_g, loss_target, m_c_ctx, m_mod_w, m_mod_b, m_norm_g, m_ffn_w13, m_ffn_w2, m_even_w_in, m_even_w_out, m_na_rpb, m_pool_w, m_pool_scale, m_conv_w_in, m_conv_w, m_conv_w_out, m_final_g, v_c_ctx, v_mod_w, v_mod_b, v_norm_g, v_ffn_w13, v_ffn_w2, v_even_w_in, v_even_w_out, v_na_rpb, v_pool_w, v_pool_scale, v_conv_w_in, v_conv_w, v_conv_w_out, v_final_g):
    xi, yi, ci = lax.axis_index("x"), lax.axis_index("y"), lax.axis_index("c")
    chip = 2 * xi + yi
    dev = 4 * xi + 2 * yi + ci
    _, L, D = x.shape
    C = ctx.shape[1]
    Ds = D // N_CHIPS
    Nm = mod_w.shape[-1]
    Fh = ffn_w13.shape[-1]
    Fq = ffn_w2.shape[2]
    assert ffn_w13.shape[:2] == (2, 2) and Fh == 2 * Fq and L % (GQ * GRID_W) == 0 and L // GRID_W >= GK and GQ == NA_KH // 2
    x0, ctx0, tgt = x[0], ctx[0], loss_target[0]

    pad = lambda a: jnp.pad(a, ((0, 0), (0, D - a.shape[1])))
    pack1 = jnp.concatenate([c, pad(norm_g.reshape(6, Ds)), pad(conv_w.reshape(3, Ds)), jnp.zeros((6, D), F32)], axis=0)
    g1 = _small_all_gather(pack1, "ag_cond")
    cond = jnp.concatenate([g1[:, 0], c_ctx[None], jnp.zeros((7, D), F32)], axis=0)
    norm_full = jnp.concatenate([g1[2 * k, 1:7, :Ds] for k in range(N_CHIPS)], axis=1).reshape(2, 3, D)
    convw_full = jnp.concatenate([g1[2 * k, 7:10, :Ds] for k in range(N_CHIPS)], axis=1)

    mod_b_loc = lax.dynamic_slice_in_dim(mod_b, chip * Nm, Nm, axis=1).reshape(2, 1, Nm)
    m_loc = _mod_fwd(cond, mod_w, mod_b_loc, "mod_fwd")
    g2 = _small_all_gather(m_loc.reshape(32, Nm), "ag_mod")
    m_all = jnp.concatenate([g2[2 * k] for k in range(N_CHIPS)], axis=1).reshape(2, 16, N_MOD, D)
    m_lat = lax.dynamic_index_in_dim(m_all, dev, axis=1, keepdims=False)
    m_ctx = m_all[:, 8]

    def prm(mods, layer, base, gain_idx, extra=()):
        return _prm([mods[layer, base], mods[layer, base + 1], mods[layer, base + 2], norm_full[layer, gain_idx],
                     *extra], D)

    def shard_bf16(w, name):
        return _cast_bf16(w.reshape(-1, w.shape[-1]), name).reshape(-1, *w.shape[-2:])

    w13s, w2s = shard_bf16(ffn_w13, "cast_w13"), shard_bf16(ffn_w2, "cast_w2")
    eins, eouts = shard_bf16(even_w_in, "cast_ein"), shard_bf16(even_w_out, "cast_eout")
    cins, couts = shard_bf16(conv_w_in, "cast_cin"), shard_bf16(conv_w_out, "cast_cout")
    ffn_shards = [[w13s[t:t + 1], w2s[t:t + 1]] for t in range(4)]

    def ffn_weights(w13g, w2g):
        return w13g.reshape(1, 4, D, Fh), w2g

    wf = [ffn_weights(*_gather_shards(ffn_shards[0], "ag_ffn0")), None, None, None]
    pos = jnp.stack([chip, ci]).astype(jnp.int32)

    p_f1 = prm(m_lat, 0, 0, 0)
    p_mx = prm(m_lat, 0, 3, 1)
    p_f2 = prm(m_lat, 0, 6, 2)
    p_g1 = prm(m_lat, 1, 0, 0)
    p_cv = prm(m_lat, 1, 3, 1, extra=(convw_full[0], convw_full[1], convw_full[2]))
    p_g2 = prm(m_lat, 1, 6, 2)
    pc_f1 = prm(m_ctx, 0, 0, 0)
    pc_mx = prm(m_ctx, 0, 3, 1)

    x1, ab1, y1, (eing, eoutg) = _ffn_fwd(x0, p_f1, *wf[0], 0, "ffn_fwd_l0a", ride=_broadcast_ride([eins, eouts]))
    eing = eing.reshape(1, 4, D, NA_WIDTH)
    ctx1, abc, yc, _ = _ffn_fwd(ctx0, pc_f1, *wf[0], 0, "ffn_fwd_ctx")
    q, k, v, u, hn_mx = _even_in_fwd(x1, p_mx, eing, "even_in_fwd")
    _, k_c, v_c, _, hn_cx = _even_in_fwd(ctx1, pc_mx, eing, "even_in_ctx")
    bias = _bias_table(na_rpb[0], "bias_table")
    att, gathered = _attn_fwd(q, k, v, k_c, v_c, bias, "attn_fwd", ride=_broadcast_ride(ffn_shards[1]))
    wf[1] = ffn_weights(*gathered)
    pw_b = _cast_bf16(pool_w.reshape(-1, POOL_GROUP_DIM), "cast_poolw").reshape(4, POOL_GROUP_DIM, POOL_GROUP_DIM)
    pool, dmx = _pool_fwd(u, pw_b, pool_scale, "pool_fwd")
    x2, ymx = _even_out_fwd(x1, att, pool, p_mx, eoutg, "even_out_fwd")
    x3, ab2, y2, gathered = _ffn_fwd(x2, p_f2, *wf[1], 0, "ffn_fwd_l0b",
                                     ride=_broadcast_ride(ffn_shards[2] + [cins, couts]))
    wf[2] = ffn_weights(*gathered[:2])
    cing, coutg = gathered[2].reshape(1, 4, D, conv_w_in.shape[-1]), gathered[3]
    x4, ab3, y3, gathered = _ffn_fwd(x3, p_g1, *wf[2], 0, "ffn_fwd_l1a", ride=_broadcast_ride(ffn_shards[3]))
    wf[3] = ffn_weights(*gathered)
    x5, ycv, bcx = _conv_fwd(x4, p_cv, cing, coutg, "conv_fwd")
    x6, ab4, y4, _ = _ffn_fwd(x5, p_g2, *wf[3], 0, "ffn_fwd_l1b")
    dx6, acc_head = _loss_head(x6, tgt, final_g.reshape(1, D), "loss_head")
    loss = lax.psum(acc_head[1, 0], ("x", "y", "c"))

    def ffn_back(dout, xin, ab, yy, p, t, tag, init13=None, init2=None, ride=None):
        dx, dab, gact, dy, hn, acc, carried = _ffn_bwd(dout, xin, ab, yy, p, *wf[t], 0, f"ffn_bwd_{tag}", ride=ride)
        dw13 = _mm_tn(hn, dab, 4, False, f"dw13_{tag}", init=init13)
        dw2 = _mm_tn(gact, dy, 2, True, f"dw2_{tag}", init=init2)
        return dx, acc, dw13, dw2, carried

    dx5, acc_g2, dw13_3, dw2_3, _ = ffn_back(dx6, x5, ab4, y4, p_g2, 3, "l1b")
    s_a, sb_a = _pair_sums([dw13_3, dw2_3], pos, "l1b")
    dx4, dproj, h2, dycv, hn_cv, acc_cv, got_a = _conv_bwd(dx5, x4, ycv, bcx, p_cv, cing, coutg, "conv_bwd",
                                                           ride=_scatter_ride(sb_a))
    dcin = _mm_tn(hn_cv, dproj, 4, False, "dw_cin")
    dcout = _mm_tn(h2, dycv, 1, False, "dw_cout")
    s_b, sb_b = _pair_sums([dcin, dcout], pos, "conv")
    dx3, acc_g1, dw13_2, dw2_2, got_b = ffn_back(dx4, x3, ab3, y3, p_g1, 2, "l1a", ride=_scatter_ride(sb_b))
    s_c, sb_c = _pair_sums([dw13_2, dw2_2], pos, "l1a")
    dx2, acc_f2, dw13_1, dw2_1, got_c = ffn_back(dx3, x2, ab2, y2, p_f2, 1, "l0b", ride=_scatter_ride(sb_c))

    dymx, datt, dpool, acc_mxo = _even_out_bwd(dx2, ymx, p_mx, eoutg, "even_out_bwd")
    deout = jnp.concatenate([_mm_tn(att, dymx, 1, False, "dw_eout_att"),
                             _mm_tn(pool, dymx, 1, False, "dw_eout_pool")], axis=0)
    s_d, sb_d = _pair_sums([dw13_1, dw2_1, deout], pos, "l0b")
    du, dpoolw, acc_pool = _pool_bwd(dpool, dmx, pw_b, pool_scale, "pool_bwd")
    dq, dk, dv, dkc, dvc, dbias, got_d = _attn_bwd(q, k, v, k_c, v_c, bias, datt, "attn_bwd",
                                                   ride=_scatter_ride(sb_d))
    drpb = _rpb_grad(dbias, "rpb_grad")
    dx1, dstack, acc_mxi = _even_in_bwd(dx2, x1, dq, dk, dv, du, p_mx, eing,
                                        "even_in_bwd")
    zc = jnp.zeros((C, NA_WIDTH), F32)
    dctx1, dstack_c, accc_mx = _even_in_bwd(jnp.zeros((C, D), F32), ctx1, zc, dkc, dvc, zc,
                                            pc_mx, eing, "even_in_bwd_ctx")
    dein_c = _mm_tn(hn_cx, dstack_c, 4, False, "dw_ein_ctx")
    dein = _mm_tn(hn_mx, dstack, 4, False, "dw_ein", init=dein_c)
    s_e, sb_e = _pair_sums([dein], pos, "ein")
    _, accc_f1, dw13_c, dw2_c, _ = ffn_back(dctx1, ctx0, abc, yc, pc_f1, 0, "ctx")
    dx0, acc_f1, dw13_0, dw2_0, got_e = ffn_back(dx1, x0, ab1, y1, p_f1, 0, "l0a", init13=dw13_c, init2=dw2_c,
                                                 ride=_scatter_ride(sb_e))
    s_f, sb_f = _pair_sums([dw13_0, dw2_0], pos, "l0a")
    got_f = _run_ride(_scatter_ride(sb_f), "rs_scatter_l0a")

    z1 = jnp.zeros((1, D), F32)
    dm_lat = jnp.concatenate([acc_f1[0:3], acc_mxi[0:2], acc_mxo[2:3], acc_f2[0:3],
                              acc_g1[0:3], acc_cv[0:3], acc_g2[0:3]], axis=0)
    dm_ctx = jnp.concatenate([accc_f1[0:3], accc_mx[0:2]] + [z1] * 13, axis=0)
    dnorm = jnp.concatenate([acc_f1[3:4] + accc_f1[3:4], acc_mxi[3:4] + accc_mx[3:4], acc_f2[3:4],
                             acc_g1[3:4], acc_cv[3:4], acc_g2[3:4]], axis=0)
    rpb_flat = jnp.pad(drpb.reshape(-1), (0, 4 * D - drpb.size)).reshape(4, D)
    pack3 = jnp.concatenate([dm_lat, dm_ctx, dnorm, acc_cv[4:7], acc_head[0:1], pad(acc_pool[0:1]), z1,
                             dpoolw.reshape(-1, D), rpb_flat, jnp.zeros((4, D), F32)], axis=0)
    g3 = _small_all_gather(pack3, "ag_small")
    tot = _sum_devices(g3, "sum_small")
    dm_all = jnp.concatenate([g3[:, 0:18].reshape(8, 2, N_MOD * D).transpose(1, 0, 2),
                              tot[18:36].reshape(2, 1, N_MOD * D), jnp.zeros((2, 7, N_MOD * D), F32)], axis=1)
    dm_loc = lax.dynamic_slice_in_dim(dm_all, chip * Nm, Nm, axis=2)
    g_mod_w, dsilu = _mod_bwd(cond, dm_loc, mod_w, "mod_bwd")
    g4 = _small_all_gather(dsilu, "ag_dsilu")
    g_mod_b, g_c_ctx = _mod_small_grads(dm_all, cond, g4, "mod_small")
    g_mod_b = g_mod_b.reshape(2, N_MOD * D)
    g_c_ctx = g_c_ctx.reshape(D)
    g_norm_full = tot[36:42].reshape(2, 3, D)
    g_norm = lax.dynamic_slice_in_dim(g_norm_full, chip * Ds, Ds, axis=2)
    g_conv_w = lax.dynamic_slice_in_dim(tot[42:45], chip * Ds, Ds, axis=1).reshape(1, 3, Ds)
    g_final = tot[45]
    g_pscale = tot[46:47, :pool_scale.shape[1]]
    g_poolw = tot[48:112].reshape(pool_w.shape)
    g_rpb = tot[112:116].reshape(-1)[:na_rpb.size].reshape(na_rpb.shape)

    r13_3, r2_3 = _joins(s_a, got_a, pos, "l1b")
    r_cin, r_cout = _joins(s_b, got_b, pos, "conv")
    r13_2, r2_2 = _joins(s_c, got_c, pos, "l1a")
    r13_1, r2_1, r_eout = _joins(s_d, got_d, pos, "l0b")
    (r_ein,) = _joins(s_e, got_e, pos, "ein")
    r13_0, r2_0 = _joins(s_f, got_f, pos, "l0a")
    g_w13 = jnp.stack([r13_0, r13_1, r13_2, r13_3]).reshape(ffn_w13.shape)
    g_w2 = jnp.stack([r2_0, r2_1, r2_2, r2_3]).reshape(ffn_w2.shape)
    g_ein, g_eout, g_cin, g_cout = r_ein[None], r_eout[None], r_cin[None], r_cout[None]

    grads = [g_c_ctx, g_mod_w, g_mod_b, g_norm, g_w13, g_w2, g_ein, g_eout, g_rpb, g_poolw, g_pscale, g_cin,
             g_conv_w, g_cout, g_final]
    weights = [c_ctx, mod_w, mod_b, norm_g, ffn_w13, ffn_w2, even_w_in, even_w_out, na_rpb, pool_w, pool_scale,
               conv_w_in, conv_w, conv_w_out, final_g]
    ms = [m_c_ctx, m_mod_w, m_mod_b, m_norm_g, m_ffn_w13, m_ffn_w2, m_even_w_in, m_even_w_out, m_na_rpb, m_pool_w,
          m_pool_scale, m_conv_w_in, m_conv_w, m_conv_w_out, m_final_g]
    vs = [v_c_ctx, v_mod_w, v_mod_b, v_norm_g, v_ffn_w13, v_ffn_w2, v_even_w_in, v_even_w_out, v_na_rpb, v_pool_w,
          v_pool_scale, v_conv_w_in, v_conv_w, v_conv_w_out, v_final_g]
    names = ["c_ctx", "mod_w", "mod_b", "norm_g", "ffn_w13", "ffn_w2", "even_w_in", "even_w_out", "na_rpb", "pool_w",
             "pool_scale", "conv_w_in", "conv_w", "conv_w_out", "final_g"]
    deltas, new_m, new_v = [], [], []
    for n, w, g, m, vv in zip(names, weights, grads, ms, vs):
        g = g.reshape(w.shape)
        if w.ndim == 1:
            d, mn, vn = (t.reshape(w.shape) for t in _adamw(w[None], g[None], m[None], vv[None], f"adamw_{n}"))
        else:
            d, mn, vn = _adamw(w, g, m, vv, f"adamw_{n}")
        deltas.append(d)
        new_m.append(mn)
        new_v.append(vn)
    grads = [g.reshape(w.shape) for g, w in zip(grads, weights)]
    return (loss, dx0[None], *grads, *deltas, *new_m, *new_v)
```

```python
import jax
import jax.numpy as jnp
from jax import lax
from jax.experimental import pallas as pl
from jax.experimental.pallas import tpu as pltpu

F32 = jnp.float32
BF16 = jnp.bfloat16
MESH = pl.DeviceIdType.MESH

GRID_W = 64
NA_HEADS = 8
NA_HEAD_DIM = 64
NA_KH = 8
NA_KW = 16
GQ = 4
GK = GQ + NA_KH
NA_WIDTH = NA_HEADS * NA_HEAD_DIM
POOL_WINDOWS = (2, 4, 8, 16)
POOL_GROUP_DIM = 128
N_MOD = 9
RMS_EPS = 1e-6
NEG_INF = -1e30
ADAM_LR, ADAM_B1, ADAM_B2, ADAM_EPS, ADAM_WD, ADAM_STEP = 0.001, 0.9, 0.999, 1e-08, 0.01, 10

HALO = 16
VMEM_LIMIT = 56 * 1024 * 1024
N_CHIPS = 4
N_DEV = 8


def _dot(a, b):
    return jnp.dot(a, b, preferred_element_type=F32)


def _dot_nt(a, b):
    return lax.dot_general(a, b, (((1,), (1,)), ((), ())), preferred_element_type=F32)


def _dot_tn(a, b):
    return lax.dot_general(a, b, (((0,), (0,)), ((), ())), preferred_element_type=F32)


def _sigmoid(a):
    return 1.0 / (1.0 + jnp.exp(-a))


def _sum0(v):
    return jnp.sum(v, axis=0, keepdims=True)


def _nm(x, g, shift, scale):
    r = lax.rsqrt(jnp.mean(x * x, axis=-1, keepdims=True) + RMS_EPS)
    xhat = x * r
    nrm = xhat * g
    return nrm * (1.0 + scale) + shift, xhat, r, nrm


def _nm_bwd(dhn, xhat, r, nrm, g, scale):
    dshift = _sum0(dhn)
    dscale = _sum0(dhn * nrm)
    dnrm = dhn * (1.0 + scale)
    dgn = _sum0(dnrm * xhat)
    dxh = dnrm * g
    dx = r * (dxh - xhat * jnp.mean(dxh * xhat, axis=-1, keepdims=True))
    return dx, dshift, dscale, dgn


def _acc_rows(acc_ref, first, rows):
    @pl.when(first)
    def _():
        acc_ref[...] = jnp.zeros(acc_ref.shape, acc_ref.dtype)
    for k, row in enumerate(rows):
        if row is not None:
            acc_ref[k:k + 1, :] += row


def _shift_rows(v, k):
    n = v.shape[0]
    k = k % n
    return v if k == 0 else pltpu.roll(v, k, 0)


def _tile(tm, w):
    return pl.BlockSpec((tm, w), lambda i: (i, 0))


def _full(shape):
    nd = len(shape)
    return pl.BlockSpec(shape, lambda i: (0,) * nd)


def _resident(block, imap):
    return pl.BlockSpec(block, imap, pipeline_mode=pl.Buffered(1))


def _halo_prev(tm, w):
    return pl.BlockSpec((HALO, w), lambda i: (jnp.maximum(i * (tm // HALO) - 1, 0), 0))


def _halo_next(tm, w, L):
    return pl.BlockSpec((HALO, w), lambda i: (jnp.minimum((i + 1) * (tm // HALO), L // HALO - 1), 0))


def _params(vmem=VMEM_LIMIT):
    return pltpu.CompilerParams(vmem_limit_bytes=vmem)


def _pick_rows(rows, cols, itemsize=4, target=1 << 20):
    best = None
    for t in range(8, rows + 1, 8):
        if rows % t == 0 and t * cols * itemsize <= target:
            best = t
    return best if best is not None else rows


def _ext(prev, cur, nxt, i, nt):
    prev = jnp.where(i > 0, prev, jnp.zeros_like(prev))
    nxt = jnp.where(i < nt - 1, nxt, jnp.zeros_like(nxt))
    return jnp.concatenate([prev, cur, nxt], axis=0)


def _cast_bf16(a2d, name):
    rows, cols = a2d.shape
    tr = _pick_rows(rows, cols)

    def body(a_ref, o_ref):
        o_ref[...] = a_ref[...].astype(BF16)

    return pl.pallas_call(
        body, name=name, grid=(rows // tr,), in_specs=[_tile(tr, cols)], out_specs=_tile(tr, cols),
        out_shape=jax.ShapeDtypeStruct((rows, cols), BF16))(a2d)


def _sum_devices(g, name):
    n, rows, cols = g.shape
    tr = _pick_rows(rows, cols, target=1 << 18)

    def body(g_ref, o_ref):
        s = g_ref[0]
        for d in range(1, n):
            s = s + g_ref[d]
        o_ref[...] = s

    return pl.pallas_call(
        body, name=name, grid=(rows // tr,), in_specs=[pl.BlockSpec((n, tr, cols), lambda i: (0, i, 0))],
        out_specs=_tile(tr, cols), out_shape=jax.ShapeDtypeStruct((rows, cols), F32))(g)


def _adamw(w, g, m, v, name):
    shape = w.shape
    cols = shape[-1]
    rows = w.size // cols
    w2, g2, m2, v2 = (t.reshape(rows, cols) for t in (w, g, m, v))
    tr = _pick_rows(rows, cols)
    c1 = 1.0 - ADAM_B1 ** ADAM_STEP
    c2 = 1.0 - ADAM_B2 ** ADAM_STEP

    def body(w_ref, g_ref, m_ref, v_ref, d_ref, mo_ref, vo_ref):
        gg = g_ref[...]
        mn = ADAM_B1 * m_ref[...] + (1.0 - ADAM_B1) * gg
        vn = ADAM_B2 * v_ref[...] + (1.0 - ADAM_B2) * (gg * gg)
        d_ref[...] = -ADAM_LR * ((mn / c1) / (jnp.sqrt(vn / c2) + ADAM_EPS) + ADAM_WD * w_ref[...])
        mo_ref[...] = mn
        vo_ref[...] = vn

    outs = pl.pallas_call(
        body, name=name, grid=(rows // tr,), in_specs=[_tile(tr, cols)] * 4, out_specs=[_tile(tr, cols)] * 3,
        out_shape=[jax.ShapeDtypeStruct((rows, cols), F32)] * 3)(w2, g2, m2, v2)
    return tuple(o.reshape(shape) for o in outs)


def _mesh_pos():
    x, y, c = lax.axis_index("x"), lax.axis_index("y"), lax.axis_index("c")
    chips = [(1 - x, y), (x, 1 - y), (1 - x, 1 - y)]
    return x, y, c, chips


def _hbm_specs(n):
    return [pl.BlockSpec(memory_space=pltpu.HBM)] * n


def _small_all_gather(v, name):
    rows, w = v.shape

    def body(x_ref, out_ref, send_sems, recv_sems, local_sem):
        x, y, c, chips = _mesh_pos()
        me, sibling = (x, y, c), (x, y, 1 - c)

        def blk(px, py, pc):
            return out_ref.at[4 * px + 2 * py + pc]

        def copy(k, block, to, src=None):
            return pltpu.make_async_remote_copy(
                src_ref=blk(*block) if src is None else src, dst_ref=blk(*block),
                send_sem=send_sems.at[k], recv_sem=recv_sems.at[k], device_id=to, device_id_type=MESH)

        mine = pltpu.make_async_copy(x_ref, blk(*me), local_sem)
        mine.start()
        first = [copy(0, me, sibling, src=x_ref)]
        first += [copy(1 + j, me, (*chip, c), src=x_ref) for j, chip in enumerate(chips)]
        for cp in first:
            cp.start()
        passed = [copy(4 + j, (*chip, c), sibling) for j, chip in enumerate(chips)]
        for j, chip in enumerate(chips):
            copy(1 + j, (*chip, c), me).wait_recv()
            passed[j].start()
        copy(0, sibling, me).wait_recv()
        for j, chip in enumerate(chips):
            copy(4 + j, (*chip, 1 - c), me).wait_recv()
        for cp in first + passed:
            cp.wait_send()
        mine.wait()

    return pl.pallas_call(
        body, name=name, out_shape=jax.ShapeDtypeStruct((N_DEV, rows, w), v.dtype),
        in_specs=[pl.BlockSpec(memory_space=pltpu.VMEM)], out_specs=pl.BlockSpec(memory_space=pltpu.VMEM),
        scratch_shapes=[pltpu.SemaphoreType.DMA((7,)), pltpu.SemaphoreType.DMA((7,)), pltpu.SemaphoreType.DMA],
    )(v)


def _gather_shards(shards, name):
    n = len(shards)

    def body(*refs):
        ins, outs = refs[:n], refs[n:2 * n]
        send_sems, recv_sems, local_sems = refs[2 * n:]
        x, y, c, chips = _mesh_pos()
        k = 2 * x + y
        sibling = (x, y, 1 - c)

        def window(t, chip_k, half):
            r = ins[t].shape[1]
            return outs[t].at[:, pl.ds(chip_k * r + half * (r // 2), r // 2), :]

        def copy(t, j, chip_k, half, to, src=None):
            return pltpu.make_async_remote_copy(
                src_ref=window(t, chip_k, half) if src is None else src, dst_ref=window(t, chip_k, half),
                send_sem=send_sems.at[6 * t + j], recv_sem=recv_sems.at[6 * t + j], device_id=to, device_id_type=MESH)

        started, local = [], []
        for t in range(n):
            r = ins[t].shape[1]
            lc = pltpu.make_async_copy(ins[t], outs[t].at[:, pl.ds(k * r, r), :], local_sems.at[t])
            lc.start()
            local.append(lc)
            src = ins[t].at[:, pl.ds(c * (r // 2), r // 2), :]
            for j, chip in enumerate(chips):
                cp = copy(t, j, k, c, (*chip, c), src=src)
                cp.start()
                started.append(cp)
        for t in range(n):
            for j, chip in enumerate(chips):
                kj = 2 * chip[0] + chip[1]
                copy(t, j, kj, c, sibling).wait_recv()
                cp = copy(t, 3 + j, kj, c, sibling)
                cp.start()
                started.append(cp)
        for t in range(n):
            for j, chip in enumerate(chips):
                kj = 2 * chip[0] + chip[1]
                copy(t, 3 + j, kj, 1 - c, sibling).wait_recv()
        for cp in started:
            cp.wait_send()
        for lc in local:
            lc.wait()

    out_shape = [jax.ShapeDtypeStruct((s.shape[0], N_CHIPS * s.shape[1], s.shape[2]), s.dtype) for s in shards]
    return pl.pallas_call(
        body, name=name, out_shape=out_shape, in_specs=_hbm_specs(n), out_specs=_hbm_specs(n),
        scratch_shapes=[pltpu.SemaphoreType.DMA((6 * n,)), pltpu.SemaphoreType.DMA((6 * n,)),
                        pltpu.SemaphoreType.DMA((n,))],
    )(*shards)


def _chunk_rows(h, w):
    best = 16
    for t in range(16, h + 1, 16):
        if h % t == 0 and t * w * 4 <= (2 << 20):
            best = t
    return best


def _pair_sum(part, pos, name):
    _, h, w = part.shape
    cr = _chunk_rows(h, w)
    nc = h // cr
    n = 4 * nc
    slots = 4

    def body(pos_ref, own_ref, send_ref, s_ref, sb_ref, rbuf, send_sems, recv_sems):
        x, y, c, _ = _mesh_pos()
        k = pl.program_id(0)

        def copy(slot):
            return pltpu.make_async_remote_copy(
                src_ref=send_ref, dst_ref=rbuf.at[slot], send_sem=send_sems.at[slot], recv_sem=recv_sems.at[slot],
                device_id=(x, y, 1 - c), device_id_type=MESH)

        @pl.when(k < n)
        def _():
            copy(k % slots).start()

        @pl.when(k > 0)
        def _():
            before = (k + slots - 1) % slots
            copy(before).wait_recv()
            s = own_ref[...] + rbuf[before]
            s_ref[...] = s
            sb_ref[...] = s.astype(BF16)

        @pl.when(k < n)
        def _():
            copy(k % slots).wait_send()

    def own(k, p):
        j = jnp.maximum(k - 1, 0)
        return ((2 * (j // nc) + p[1]) * nc + j % nc, 0)

    def send(k, p):
        j = jnp.minimum(k, n - 1)
        return ((2 * (j // nc) + 1 - p[1]) * nc + j % nc, 0)

    grid_spec = pltpu.PrefetchScalarGridSpec(
        num_scalar_prefetch=1, grid=(n + 1,),
        in_specs=[pl.BlockSpec((cr, w), own), pl.BlockSpec((cr, w), send)],
        out_specs=[pl.BlockSpec((cr, w), lambda k, p: (jnp.maximum(k - 1, 0), 0))] * 2,
        scratch_shapes=[pltpu.VMEM((slots, cr, w), F32), pltpu.SemaphoreType.DMA((slots,)),
                        pltpu.SemaphoreType.DMA((slots,))])
    part2 = part.reshape(8 * h, w)
    s, sb = pl.pallas_call(
        body, name=name, grid_spec=grid_spec, compiler_params=_params(),
        out_shape=[jax.ShapeDtypeStruct((4 * h, w), F32), jax.ShapeDtypeStruct((4 * h, w), BF16)],
    )(pos, part2, part2)
    return s.reshape(4, h, w), sb.reshape(4, h, w)


class _Ride:
    def __init__(self, ins, out_shape, sems, copies):
        self.ins, self.out_shape, self.sems, self.copies = list(ins), list(out_shape), list(sems), copies

    def start(self, ins, outs, sems):
        sends, _, _, local = self.copies(ins, outs, sems)
        for cp in local + sends:
            cp.start()

    def finish(self, ins, outs, sems):
        _, recvs, sends, local = self.copies(ins, outs, sems)
        for cp in recvs:
            cp.wait_recv()
        for cp in sends:
            cp.wait_send()
        for cp in local:
            cp.wait()


def _scatter_ride(sums_bf16):
    n = len(sums_bf16)

    def copies(ins, outs, sems):
        send_sems, recv_sems = sems
        x, y, c, chips = _mesh_pos()
        cps = [pltpu.make_async_remote_copy(
            src_ref=ins[t].at[2 * chip[0] + chip[1]], dst_ref=outs[t].at[j],
            send_sem=send_sems.at[3 * t + j], recv_sem=recv_sems.at[3 * t + j],
            device_id=(*chip, c), device_id_type=MESH) for t in range(n) for j, chip in enumerate(chips)]
        return cps, cps, cps, []

    return _Ride(sums_bf16, [jax.ShapeDtypeStruct((3,) + s.shape[1:], BF16) for s in sums_bf16],
                 [pltpu.SemaphoreType.DMA((3 * n,)), pltpu.SemaphoreType.DMA((3 * n,))], copies)


def _broadcast_ride(shards):
    n = len(shards)

    def copies(ins, outs, sems):
        send_sems, recv_sems, local_sems = sems
        x, y, c, chips = _mesh_pos()
        k = 2 * x + y
        sends, recvs, local = [], [], []
        for t in range(n):
            r = ins[t].shape[1]
            h = r // 2
            local.append(pltpu.make_async_copy(ins[t], outs[t].at[:, pl.ds(k * r, r), :], local_sems.at[t]))
            src = ins[t].at[:, pl.ds(c * h, h), :]
            mine = outs[t].at[:, pl.ds(k * r + c * h, h), :]
            for j, chip in enumerate(chips):
                kj = 2 * chip[0] + chip[1]
                for d in range(2):
                    sends.append(pltpu.make_async_remote_copy(
                        src_ref=src, dst_ref=mine, send_sem=send_sems.at[6 * t + 2 * j + d],
                        recv_sem=recv_sems.at[6 * t + 2 * j + c], device_id=(*chip, d), device_id_type=MESH))
                    theirs = outs[t].at[:, pl.ds(kj * r + d * h, h), :]
                    recvs.append(pltpu.make_async_remote_copy(
                        src_ref=theirs, dst_ref=theirs, send_sem=send_sems.at[6 * t + 2 * j + d],
                        recv_sem=recv_sems.at[6 * t + 2 * j + d], device_id=(*chip, d), device_id_type=MESH))
        return sends, recvs, sends, local

    return _Ride(shards, [jax.ShapeDtypeStruct((s.shape[0], N_CHIPS * s.shape[1], s.shape[2]), s.dtype) for s in shards],
                 [pltpu.SemaphoreType.DMA((6 * n,)), pltpu.SemaphoreType.DMA((6 * n,)), pltpu.SemaphoreType.DMA((n,))],
                 copies)


def _run_ride(ride, name):
    ni, no = len(ride.ins), len(ride.out_shape)

    def body(*refs):
        ride.start(refs[:ni], refs[ni:ni + no], refs[ni + no:])
        ride.finish(refs[:ni], refs[ni:ni + no], refs[ni + no:])

    return pl.pallas_call(body, name=name, out_shape=ride.out_shape, in_specs=_hbm_specs(ni), out_specs=_hbm_specs(no),
                          scratch_shapes=ride.sems)(*ride.ins)


def _ride_call(body, ride, args, *, name, grid, in_specs, out_specs, out_shape, compiler_params=None):
    in_specs, out_specs, out_shape = list(in_specs), list(out_specs), list(out_shape)
    if ride is None:
        res = pl.pallas_call(body, name=name, grid=grid, in_specs=in_specs, out_specs=out_specs, out_shape=out_shape,
                             compiler_params=compiler_params)(*args)
        return list(res), []
    ni, no, ri, ro = len(in_specs), len(out_specs), len(ride.ins), len(ride.out_shape)
    last = grid[0] - 1

    def carried(*refs):
        ins, rins = refs[:ni], refs[ni:ni + ri]
        outs, routs = refs[ni + ri:ni + ri + no], refs[ni + ri + no:ni + ri + no + ro]
        sems = refs[ni + ri + no + ro:]

        @pl.when(pl.program_id(0) == 0)
        def _():
            ride.start(rins, routs, sems)

        body(*ins, *outs)

        @pl.when(pl.program_id(0) == last)
        def _():
            ride.finish(rins, routs, sems)

    res = pl.pallas_call(
        carried, name=name, grid=grid, in_specs=in_specs + _hbm_specs(ri), out_specs=out_specs + _hbm_specs(ro),
        out_shape=out_shape + ride.out_shape, scratch_shapes=ride.sems, compiler_params=compiler_params,
    )(*args, *ride.ins)
    return list(res[:no]), list(res[no:])


def _sum_and_join(sums, got, pos, name):
    _, h, w = sums.shape
    cr = _chunk_rows(h, w)

    def body(pos_ref, mine_ref, got_ref, o_ref, ebuf, rbuf, send_sems, recv_sems):
        x, y, c, _ = _mesh_pos()
        slot = pl.program_id(0) % 2
        e = mine_ref[...]
        for j in range(3):
            e = e + got_ref[j].astype(F32)
        ebuf[slot] = e
        cp = pltpu.make_async_remote_copy(
            src_ref=ebuf.at[slot], dst_ref=rbuf.at[slot], send_sem=send_sems.at[slot], recv_sem=recv_sems.at[slot],
            device_id=(x, y, 1 - c), device_id_type=MESH)
        cp.start()
        o_ref[pos_ref[1]] = e
        cp.wait_recv()
        o_ref[1 - pos_ref[1]] = rbuf[slot]
        cp.wait_send()

    grid_spec = pltpu.PrefetchScalarGridSpec(
        num_scalar_prefetch=1, grid=(h // cr,),
        in_specs=[pl.BlockSpec((None, cr, w), lambda i, p: (p[0], i, 0)),
                  pl.BlockSpec((3, cr, w), lambda i, p: (0, i, 0))],
        out_specs=pl.BlockSpec((2, cr, w), lambda i, p: (0, i, 0)),
        scratch_shapes=[pltpu.VMEM((2, cr, w), F32), pltpu.VMEM((2, cr, w), F32),
                        pltpu.SemaphoreType.DMA((2,)), pltpu.SemaphoreType.DMA((2,))])
    return pl.pallas_call(
        body, name=name, grid_spec=grid_spec, compiler_params=_params(),
        out_shape=jax.ShapeDtypeStruct((2, h, w), F32),
    )(pos, sums, got)


def _pair_sums(parts, pos, tag):
    pairs = [_pair_sum(p.reshape(8, p.shape[0] // 8, p.shape[1]), pos, f"rs_pair_{tag}_{t}")
             for t, p in enumerate(parts)]
    return [s for s, _ in pairs], [sb for _, sb in pairs]


def _joins(sums, got, pos, tag):
    out = []
    for t, (s, r) in enumerate(zip(sums, got)):
        full = _sum_and_join(s, r, pos, f"rs_join_{tag}_{t}")
        out.append(full.reshape(2 * full.shape[1], full.shape[2]))
    return out


def _ffn_fwd(x, prm, w13g, w2g, t, name, tm=512, ride=None):
    L, D = x.shape
    Fh = w13g.shape[-1]
    tm = min(tm, L)

    def body(x_ref, p_ref, w13_ref, w2_ref, xo_ref, sv_ref, g_ref, y_ref):
        xv = x_ref[...]
        hn, _, _, _ = _nm(xv, p_ref[3:4, :], p_ref[0:1, :], p_ref[1:2, :])
        hb = hn.astype(BF16)
        acc = jnp.zeros((tm, D), F32)
        for j in range(2):
            a = _dot(hb, w13_ref[j])
            b = _dot(hb, w13_ref[2 + j])
            sg = _sigmoid(a)
            sa = a * sg
            sv_ref[:, j * Fh:(j + 1) * Fh] = sa.astype(BF16)
            sv_ref[:, (2 + j) * Fh:(3 + j) * Fh] = (b * (sg * (1.0 + a * (1.0 - sg)))).astype(BF16)
            g = (sa * b).astype(BF16)
            g_ref[:, j * Fh:(j + 1) * Fh] = g
            acc = acc + _dot(g, w2_ref[j * Fh:(j + 1) * Fh, :])
        y_ref[...] = acc.astype(BF16)
        xo_ref[...] = xv + (0.5 * p_ref[2:3, :]) * acc

    res, carried = _ride_call(
        body, ride, (x, prm, w13g, w2g), name=name, grid=(L // tm,),
        in_specs=[_tile(tm, D), _full((8, D)),
                  _resident((None, 4, D, Fh), lambda i: (t, 0, 0, 0)),
                  _resident((None, 2 * Fh, D), lambda i: (t, 0, 0))],
        out_specs=[_tile(tm, D), _tile(tm, 4 * Fh), _tile(tm, 2 * Fh), _tile(tm, D)],
        out_shape=[jax.ShapeDtypeStruct((L, D), F32), jax.ShapeDtypeStruct((L, 4 * Fh), BF16),
                   jax.ShapeDtypeStruct((L, 2 * Fh), BF16), jax.ShapeDtypeStruct((L, D), BF16)],
        compiler_params=_params())
    xo, sv, g, y = res
    return xo, (sv, g), y, carried


def _ffn_bwd(dout, x, sv, y, prm, w13g, w2g, t, name, tm=256, ride=None):
    L, D = x.shape
    Fh = w13g.shape[-1]
    tm = min(tm, L)

    def body(do_ref, x_ref, sv_ref, y_ref, p_ref, w13_ref, w2_ref, dx_ref, dab_ref, dy_ref, hn_ref, acc_ref):
        i = pl.program_id(0)
        do = do_ref[...]
        gain, shift, scale, gate = p_ref[3:4, :], p_ref[0:1, :], p_ref[1:2, :], p_ref[2:3, :]
        hn, xhat, r, nrm = _nm(x_ref[...], gain, shift, scale)
        hn_ref[...] = hn.astype(BF16)
        dgate = 0.5 * _sum0(do * y_ref[...].astype(F32))
        dyb = ((0.5 * gate) * do).astype(BF16)
        dy_ref[...] = dyb
        dhn = jnp.zeros((tm, D), F32)
        for j in range(2):
            dg = _dot_nt(dyb, w2_ref[j * Fh:(j + 1) * Fh, :])
            da = (dg * sv_ref[:, (2 + j) * Fh:(3 + j) * Fh].astype(F32)).astype(BF16)
            db = (dg * sv_ref[:, j * Fh:(j + 1) * Fh].astype(F32)).astype(BF16)
            dab_ref[:, j * Fh:(j + 1) * Fh] = da
            dab_ref[:, (2 + j) * Fh:(3 + j) * Fh] = db
            dhn = dhn + _dot_nt(da, w13_ref[j]) + _dot_nt(db, w13_ref[2 + j])
        dx, dshift, dscale, dgn = _nm_bwd(dhn, xhat, r, nrm, gain, scale)
        dx_ref[...] = do + dx
        _acc_rows(acc_ref, i == 0, [dshift, dscale, dgate, dgn])

    res, carried = _ride_call(
        body, ride, (dout, x, sv, y, prm, w13g, w2g), name=name, grid=(L // tm,),
        in_specs=[_tile(tm, D), _tile(tm, D), _tile(tm, 4 * Fh), _tile(tm, D), _full((8, D)),
                  _resident((None, 4, D, Fh), lambda i: (t, 0, 0, 0)),
                  _resident((None, 2 * Fh, D), lambda i: (t, 0, 0))],
        out_specs=[_tile(tm, D), _tile(tm, 4 * Fh), _tile(tm, D), _tile(tm, D), _full((8, D))],
        out_shape=[jax.ShapeDtypeStruct((L, D), F32), jax.ShapeDtypeStruct((L, 4 * Fh), BF16),
                   jax.ShapeDtypeStruct((L, D), BF16), jax.ShapeDtypeStruct((L, D), BF16),
                   jax.ShapeDtypeStruct((8, D), F32)],
        compiler_params=_params())
    return (*res, carried)


def _mm_tn(a, b, slabs, a_slabbed, name, init=None, tl=1024):
    L = a.shape[0]
    ka = a.shape[1] // slabs if a_slabbed else a.shape[1]
    nb = b.shape[1] if a_slabbed else b.shape[1] // slabs
    tl = min(tl, L)
    has_init = init is not None

    def body(a_ref, b_ref, *rest):
        o_ref = rest[-1]
        step = pl.program_id(1)

        @pl.when(step == 0)
        def _():
            o_ref[...] = rest[0][...] if has_init else jnp.zeros((ka, nb), F32)

        o_ref[...] += _dot_tn(a_ref[...], b_ref[...])

    in_specs = [pl.BlockSpec((tl, ka), (lambda s, l: (l, s)) if a_slabbed else (lambda s, l: (l, 0))),
                pl.BlockSpec((tl, nb), (lambda s, l: (l, 0)) if a_slabbed else (lambda s, l: (l, s)))]
    args = [a, b]
    if has_init:
        in_specs.append(pl.BlockSpec((ka, nb), lambda s, l: (s, 0)))
        args.append(init)
    return pl.pallas_call(
        body, name=name, grid=(slabs, L // tl), in_specs=in_specs,
        out_specs=pl.BlockSpec((ka, nb), lambda s, l: (s, 0)),
        out_shape=jax.ShapeDtypeStruct((slabs * ka, nb), F32), compiler_params=_params())(*args)


def _even_in_fwd(x, prm, wing, name, tm=512):
    L, D = x.shape
    W = wing.shape[-1]
    tm = min(tm, L)

    def body(x_ref, p_ref, w_ref, q_ref, k_ref, v_ref, u_ref, hn_ref):
        hn, _, _, _ = _nm(x_ref[...], p_ref[3:4, :], p_ref[0:1, :], p_ref[1:2, :])
        hb = hn.astype(BF16)
        hn_ref[...] = hb
        q_ref[...] = _dot(hb, w_ref[0]).astype(BF16)
        k_ref[...] = _dot(hb, w_ref[1]).astype(BF16)
        v_ref[...] = _dot(hb, w_ref[2]).astype(BF16)
        u_ref[...] = _dot(hb, w_ref[3])

    return pl.pallas_call(
        body, name=name, grid=(L // tm,),
        in_specs=[_tile(tm, D), _full((8, D)), _resident((None, 4, D, W), lambda i: (0, 0, 0, 0))],
        out_specs=[_tile(tm, W)] * 4 + [_tile(tm, D)],
        out_shape=[jax.ShapeDtypeStruct((L, W), BF16)] * 3 + [jax.ShapeDtypeStruct((L, W), F32),
                                                              jax.ShapeDtypeStruct((L, D), BF16)],
        compiler_params=_params())(x, prm, wing)


def _even_in_bwd(dout, x, dq, dk, dv, du, prm, wing, name, tm=512):
    L, D = x.shape
    W = wing.shape[-1]
    tm = min(tm, L)

    def body(do_ref, x_ref, dq_ref, dk_ref, dv_ref, du_ref, p_ref, w_ref, dx_ref, ds_ref, acc_ref):
        i = pl.program_id(0)
        gain, shift, scale = p_ref[3:4, :], p_ref[0:1, :], p_ref[1:2, :]
        _, xhat, r, nrm = _nm(x_ref[...], gain, shift, scale)
        dhn = jnp.zeros((tm, D), F32)
        for s, ref in enumerate((dq_ref, dk_ref, dv_ref, du_ref)):
            d = ref[...].astype(BF16)
            ds_ref[:, s * W:(s + 1) * W] = d
            dhn = dhn + _dot_nt(d, w_ref[s])
        dx, dshift, dscale, dgn = _nm_bwd(dhn, xhat, r, nrm, gain, scale)
        dx_ref[...] = do_ref[...] + dx
        _acc_rows(acc_ref, i == 0, [dshift, dscale, None, dgn])

    return pl.pallas_call(
        body, name=name, grid=(L // tm,),
        in_specs=[_tile(tm, D), _tile(tm, D)] + [_tile(tm, W)] * 4 +
                 [_full((8, D)), _resident((None, 4, D, W), lambda i: (0, 0, 0, 0))],
        out_specs=[_tile(tm, D), _tile(tm, 4 * W), _full((8, D))],
        out_shape=[jax.ShapeDtypeStruct((L, D), F32), jax.ShapeDtypeStruct((L, 4 * W), BF16),
                   jax.ShapeDtypeStruct((8, D), F32)],
        compiler_params=_params())(dout, x, dq, dk, dv, du, prm, wing)


def _even_out_fwd(x, att, pool, prm, woutg, name, tm=512):
    L, D = x.shape
    W = D // 2
    tm = min(tm, L)

    def body(x_ref, a_ref, p_ref, prm_ref, w_ref, xo_ref, y_ref):
        yv = _dot(a_ref[...], w_ref[0:W, :]) + _dot(p_ref[...], w_ref[W:2 * W, :])
        y_ref[...] = yv.astype(BF16)
        xo_ref[...] = x_ref[...] + prm_ref[2:3, :] * yv

    return pl.pallas_call(
        body, name=name, grid=(L // tm,),
        in_specs=[_tile(tm, D), _tile(tm, W), _tile(tm, W), _full((8, D)),
                  _resident((None, D, D), lambda i: (0, 0, 0))],
        out_specs=[_tile(tm, D), _tile(tm, D)],
        out_shape=[jax.ShapeDtypeStruct((L, D), F32), jax.ShapeDtypeStruct((L, D), BF16)],
        compiler_params=_params())(x, att, pool, prm, woutg)


def _even_out_bwd(dout, y, prm, woutg, name, tm=512):
    L, D = dout.shape
    W = D // 2
    tm = min(tm, L)

    def body(do_ref, y_ref, p_ref, w_ref, dy_ref, da_ref, dp_ref, acc_ref):
        i = pl.program_id(0)
        do = do_ref[...]
        dgate = _sum0(do * y_ref[...].astype(F32))
        dyb = (p_ref[2:3, :] * do).astype(BF16)
        dy_ref[...] = dyb
        da_ref[...] = _dot_nt(dyb, w_ref[0:W, :]).astype(BF16)
        dp_ref[...] = _dot_nt(dyb, w_ref[W:2 * W, :])
        _acc_rows(acc_ref, i == 0, [None, None, dgate])

    return pl.pallas_call(
        body, name=name, grid=(L // tm,),
        in_specs=[_tile(tm, D), _tile(tm, D), _full((8, D)), _resident((None, D, D), lambda i: (0, 0, 0))],
        out_specs=[_tile(tm, D), _tile(tm, W), _tile(tm, W), _full((8, D))],
        out_shape=[jax.ShapeDtypeStruct((L, D), BF16), jax.ShapeDtypeStruct((L, W), BF16),
                   jax.ShapeDtypeStruct((L, W), F32), jax.ShapeDtypeStruct((8, D), F32)],
        compiler_params=_params())(dout, y, prm, woutg)


def _group_ri(variant, qr, kr):
    first_key = (0, qr, GK - NA_KH)[variant]
    if not first_key <= kr < first_key + NA_KH:
        return None
    return kr - qr + (NA_KH - 1, NA_KH - 1 - NA_KH // 2, NA_KH - 1 - (GK - GQ))[variant]


def _bias_table(rpb, name):
    H = rpb.shape[0]
    nri, nci = 2 * NA_KH - 1, 2 * NA_KW - 1
    col = jnp.arange(GRID_W)
    rel = (col[None, :] - col[:, None] + (NA_KW - 1)).reshape(1, -1)
    onehot = (rel == jnp.arange(32)[:, None]).astype(F32)
    cs = jnp.clip(col - NA_KW // 2, 0, GRID_W - NA_KW)
    ok = ((col[None, :] >= cs[:, None]) & (col[None, :] < cs[:, None] + NA_KW)).astype(F32).reshape(1, -1)
    rpb2 = jnp.pad(rpb.reshape(H * nri, nci), ((0, 0), (0, 32 - nci)))

    def body(r_ref, e_ref, m_ref, o_ref):
        t = jnp.dot(r_ref[...], e_ref[...], preferred_element_type=F32, precision=lax.Precision.HIGHEST)
        o_ref[...] = jnp.where(m_ref[...] > 0.0, t, NEG_INF)

    tab = pl.pallas_call(body, name=name, out_shape=jax.ShapeDtypeStruct((H * nri, GRID_W * GRID_W), F32))(
        rpb2, onehot, ok)
    tab = tab.reshape(H, nri, GRID_W, GRID_W)
    outside = jnp.full((H, GRID_W, GRID_W), NEG_INF, F32)
    variants = []
    for variant in range(3):
        rows = []
        for qr in range(GQ):
            ris = [_group_ri(variant, qr, kr) for kr in range(GK)]
            rows.append(jnp.concatenate([outside if ri is None else tab[:, ri] for ri in ris], axis=2))
        variants.append(jnp.concatenate(rows, axis=1))
    return jnp.stack(variants, axis=1)


def _attn_probs(q, kw, kc, bias, scale):
    s_w = _dot_nt(q, kw) * scale + bias
    s_c = _dot_nt(q, kc) * scale
    m = jnp.maximum(jnp.max(s_w, axis=-1, keepdims=True), jnp.max(s_c, axis=-1, keepdims=True))
    e_w = jnp.exp(s_w - m)
    e_c = jnp.exp(s_c - m)
    inv = 1.0 / (jnp.sum(e_w, axis=-1, keepdims=True) + jnp.sum(e_c, axis=-1, keepdims=True))
    return e_w * inv, e_c * inv


def _group_place(g, R):
    G = R // GQ
    kb = jnp.clip(g * GQ - NA_KH // 2, 0, R - GK)
    variant = jnp.where(g == 0, 0, jnp.where(g == G - 1, 2, 1))
    return pl.multiple_of(g * (GQ * GRID_W), GQ * GRID_W), pl.multiple_of(kb * GRID_W, GRID_W), variant


def _lane_masks(width, dh):
    lane = lax.broadcasted_iota(jnp.int32, (1, width), 1)
    return [(lane >= h * dh) & (lane < (h + 1) * dh) for h in range(width // dh)]


def _only(mask, a):
    return jnp.where(mask, a, jnp.zeros_like(a))


def _stack_heads(bias, hpb):
    H, nv, nq, nk = bias.shape
    return bias.reshape(H // hpb, hpb, nv, nq, nk).transpose(0, 2, 1, 3, 4).reshape(H // hpb, nv, hpb * nq, nk)


def _unstack_heads(stacked, hpb):
    P, nv, rows, nk = stacked.shape
    return stacked.reshape(P, nv, hpb, rows // hpb, nk).transpose(0, 2, 1, 3, 4).reshape(P * hpb, nv, rows // hpb, nk)


def _attn_fwd(q, k, v, kc, vc, bias, name, ride=None):
    L, width = q.shape
    C = kc.shape[0]
    dh = NA_HEAD_DIM
    lanes = 128
    hpb = lanes // dh
    R = L // GRID_W
    nq, nk = GQ * GRID_W, GK * GRID_W
    scale = dh ** -0.5

    def body(q_ref, k_ref, v_ref, kc_ref, vc_ref, b_ref, o_ref):
        masks = _lane_masks(lanes, dh)
        kc2 = kc_ref[...]
        vcs = [_only(m, vc_ref[...]) for m in masks]

        def group(g, carry):
            q0, k0, variant = _group_place(g, R)
            q2 = q_ref[pl.ds(q0, nq), :]
            k2 = k_ref[pl.ds(k0, nk), :]
            v2 = v_ref[pl.ds(k0, nk), :]
            qs = jnp.concatenate([_only(m, q2) for m in masks], axis=0)
            p_w, p_c = _attn_probs(qs, k2, kc2, b_ref[variant], scale)
            p_w, p_c = p_w.astype(BF16), p_c.astype(BF16)
            o2 = jnp.zeros((nq, lanes), F32)
            for h, m in enumerate(masks):
                rows = slice(h * nq, (h + 1) * nq)
                o2 = o2 + _dot(p_w[rows], _only(m, v2)) + _dot(p_c[rows], vcs[h])
            o_ref[pl.ds(q0, nq), :] = o2.astype(BF16)
            return carry

        lax.fori_loop(0, R // GQ, group, 0)

    cols = lambda n: pl.BlockSpec((n, lanes), lambda p: (0, p))
    res, carried = _ride_call(
        body, ride, (q, k, v, kc, vc, _stack_heads(bias, hpb)), name=name, grid=(width // lanes,),
        in_specs=[cols(L), cols(L), cols(L), cols(C), cols(C),
                  pl.BlockSpec((None, 3, hpb * nq, nk), lambda p: (p, 0, 0, 0))],
        out_specs=[cols(L)], out_shape=[jax.ShapeDtypeStruct((L, width), BF16)],
        compiler_params=_params())
    return res[0], carried


def _attn_bwd(q, k, v, kc, vc, bias, do, name, ride=None):
    L, width = q.shape
    C = kc.shape[0]
    dh = NA_HEAD_DIM
    lanes = 128
    hpb = lanes // dh
    R = L // GRID_W
    nq, nk = GQ * GRID_W, GK * GRID_W
    scale = dh ** -0.5

    def body(q_ref, k_ref, v_ref, kc_ref, vc_ref, b_ref, do_ref, dq_ref, dk_ref, dv_ref, dkc_ref, dvc_ref, db_ref):
        masks = _lane_masks(lanes, dh)
        kc2 = kc_ref[...]
        vc2 = vc_ref[...]
        kcs = [_only(m, kc2) for m in masks]
        dk_ref[...] = jnp.zeros((L, lanes), F32)
        dv_ref[...] = jnp.zeros((L, lanes), F32)
        dkc_ref[...] = jnp.zeros((C, lanes), F32)
        dvc_ref[...] = jnp.zeros((C, lanes), F32)
        db_ref[...] = jnp.zeros((3, hpb * nq, nk), F32)

        def group(g, carry):
            q0, k0, variant = _group_place(g, R)
            q2 = q_ref[pl.ds(q0, nq), :]
            k2 = k_ref[pl.ds(k0, nk), :]
            v2 = v_ref[pl.ds(k0, nk), :]
            do2 = do_ref[pl.ds(q0, nq), :]
            qs = jnp.concatenate([_only(m, q2) for m in masks], axis=0)
            dos = jnp.concatenate([_only(m, do2) for m in masks], axis=0)
            p_w, p_c = _attn_probs(qs, k2, kc2, b_ref[variant], scale)
            dp_w = _dot_nt(dos, v2)
            dp_c = _dot_nt(dos, vc2)
            delta = jnp.sum(p_w * dp_w, axis=-1, keepdims=True) + jnp.sum(p_c * dp_c, axis=-1, keepdims=True)
            ds_w = p_w * (dp_w - delta)
            ds_c = p_c * (dp_c - delta)
            db_ref[variant] += ds_w
            dsw = (ds_w * scale).astype(BF16)
            dsc = (ds_c * scale).astype(BF16)
            dq2 = jnp.zeros((nq, lanes), F32)
            for h, m in enumerate(masks):
                rows = slice(h * nq, (h + 1) * nq)
                dq2 = dq2 + _dot(dsw[rows], _only(m, k2)) + _dot(dsc[rows], kcs[h])
            dq_ref[pl.ds(q0, nq), :] = dq2.astype(BF16)
            dk_ref[pl.ds(k0, nk), :] += _dot_tn(dsw, qs)
            dv_ref[pl.ds(k0, nk), :] += _dot_tn(p_w.astype(BF16), dos)
            dkc_ref[...] += _dot_tn(dsc, qs)
            dvc_ref[...] += _dot_tn(p_c.astype(BF16), dos)
            return carry

        lax.fori_loop(0, R // GQ, group, 0)

    cols = lambda n: _resident((n, lanes), lambda p: (0, p))
    bspec = _resident((None, 3, hpb * nq, nk), lambda p: (p, 0, 0, 0))
    res, carried = _ride_call(
        body, ride, (q, k, v, kc, vc, _stack_heads(bias, hpb), do), name=name, grid=(width // lanes,),
        in_specs=[cols(L), cols(L), cols(L), cols(C), cols(C), bspec, cols(L)],
        out_specs=[cols(L), cols(L), cols(L), cols(C), cols(C), bspec],
        out_shape=[jax.ShapeDtypeStruct((L, width), BF16)] + [jax.ShapeDtypeStruct((L, width), F32)] * 2 +
                  [jax.ShapeDtypeStruct((C, width), F32)] * 2 +
                  [jax.ShapeDtypeStruct((width // lanes, 3, hpb * nq, nk), F32)],
        compiler_params=_params())
    res[5] = _unstack_heads(res[5], hpb)
    return (*res, carried)


def _rpb_grad(dbias, name):
    H = dbias.shape[0]
    nri, nci = 2 * NA_KH - 1, 2 * NA_KW - 1
    d6 = dbias.reshape(H, 3, GQ, GRID_W, GK, GRID_W).transpose(0, 1, 2, 4, 3, 5)
    col = jnp.arange(GRID_W)
    onehot = (col[None, None, :] - col[None, :, None] + (NA_KW - 1) == jnp.arange(32)[:, None, None]).astype(F32)
    places = [(v, qr, kr) for v in range(3) for qr in range(GQ) for kr in range(GK)]

    def body(d_ref, m_ref, o_ref, t_ref):
        t_ref[...] = jnp.zeros((32, GRID_W), F32)
        o_ref[...] = jnp.zeros((16, 32, 128), F32)
        for ri in range(nri):
            a = None
            for place in places:
                if _group_ri(*place) == ri:
                    blk = d_ref[place]
                    a = blk if a is None else a + blk
            for ci in range(nci):
                t_ref[ci:ci + 1, :] = _sum0(a * m_ref[ci])
            o_ref[ri] = jnp.broadcast_to(jnp.sum(t_ref[...], axis=1, keepdims=True), (32, 128))

    out = pl.pallas_call(
        body, name=name, grid=(H,),
        in_specs=[pl.BlockSpec((None, 3, GQ, GK, GRID_W, GRID_W), lambda h: (h, 0, 0, 0, 0, 0)),
                  pl.BlockSpec((32, GRID_W, GRID_W), lambda h: (0, 0, 0))],
        out_specs=pl.BlockSpec((None, 16, 32, 128), lambda h: (h, 0, 0, 0)),
        out_shape=jax.ShapeDtypeStruct((H, 16, 32, 128), F32),
        scratch_shapes=[pltpu.VMEM((32, GRID_W), F32)])(d6, onehot)
    return out[:, :nri, :nci, 0]


def _window_count(t, w, L):
    lo = jnp.clip(t - w // 2, 0, L)
    hi = jnp.clip(t - w // 2 + w, 0, L)
    return jnp.maximum(hi - lo, 1).astype(F32)


def _running_sum(v, w):
    k = 1
    while k < w:
        v = v + _shift_rows(v, k)
        k *= 2
    return v


def _pool_fwd(u, poolw, pscale, name, tm=512):
    L, W = u.shape
    G = POOL_GROUP_DIM
    tm = min(tm, L)
    nt = L // tm

    def body(c_ref, p_ref, n_ref, w_ref, s_ref, o_ref, dm_ref):
        i = pl.program_id(0)
        ext = _ext(p_ref[...], c_ref[...], n_ref[...], i, nt)
        t = i * tm + lax.broadcasted_iota(jnp.int32, (tm, 1), 0)
        for g, w in enumerate(POOL_WINDOWS):
            e = ext[:, g * G:(g + 1) * G]
            win = _shift_rows(_running_sum(e, w), -(w // 2 - 1))[HALO:HALO + tm]
            dmx = (win / _window_count(t, w, L) - e[HALO:HALO + tm]).astype(BF16)
            dm_ref[:, g * G:(g + 1) * G] = dmx
            o_ref[:, g * G:(g + 1) * G] = (_dot(dmx, w_ref[g]) * s_ref[:, g * G:(g + 1) * G]).astype(BF16)

    return pl.pallas_call(
        body, name=name, grid=(nt,),
        in_specs=[_tile(tm, W), _halo_prev(tm, W), _halo_next(tm, W, L), _full((4, G, G)), _full((1, W))],
        out_specs=[_tile(tm, W), _tile(tm, W)],
        out_shape=[jax.ShapeDtypeStruct((L, W), BF16)] * 2, compiler_params=_params())(u, u, u, poolw, pscale)


def _pool_bwd(dpool, dmx, poolw, pscale, name, tm=512):
    L, W = dpool.shape
    G = POOL_GROUP_DIM
    tm = min(tm, L)
    nt = L // tm

    def body(c_ref, p_ref, n_ref, dm_ref, w_ref, s_ref, du_ref, dw_ref, acc_ref):
        i = pl.program_id(0)
        ext = _ext(p_ref[...], c_ref[...], n_ref[...], i, nt)
        te = i * tm - HALO + lax.broadcasted_iota(jnp.int32, (tm + 2 * HALO, 1), 0)

        @pl.when(i == 0)
        def _():
            dw_ref[...] = jnp.zeros((4 * G, G), F32)

        rows = []
        for g, w in enumerate(POOL_WINDOWS):
            sc = s_ref[:, g * G:(g + 1) * G]
            dpre = (ext[:, g * G:(g + 1) * G] * sc).astype(BF16)
            dd = _dot_nt(dpre, w_ref[g])
            spread = _shift_rows(_running_sum(dd / _window_count(te, w, L), w), -(w // 2))
            du_ref[:, g * G:(g + 1) * G] = (spread - dd)[HALO:HALO + tm]
            dmx_g = dm_ref[:, g * G:(g + 1) * G]
            rows.append(_sum0(c_ref[:, g * G:(g + 1) * G] * _dot(dmx_g, w_ref[g])))
            dw_ref[g * G:(g + 1) * G, :] += _dot_tn(dmx_g, dpre[HALO:HALO + tm])
        _acc_rows(acc_ref, i == 0, [jnp.concatenate(rows, axis=1)])

    return pl.pallas_call(
        body, name=name, grid=(nt,),
        in_specs=[_tile(tm, W), _halo_prev(tm, W), _halo_next(tm, W, L), _tile(tm, W), _full((4, G, G)),
                  _full((1, W))],
        out_specs=[_tile(tm, W), _full((4 * G, G)), _full((8, W))],
        out_shape=[jax.ShapeDtypeStruct((L, W), F32), jax.ShapeDtypeStruct((4 * G, G), F32),
                   jax.ShapeDtypeStruct((8, W), F32)],
        compiler_params=_params())(dpool, dpool, dpool, dmx, poolw, pscale)


def _conv3(z, cw):
    return _shift_rows(z, 1) * cw[0] + z * cw[1] + _shift_rows(z, -1) * cw[2]


def _conv_fwd(x, prm, wing, woutg, name, tm=512):
    L, D = x.shape
    Ws = wing.shape[-1]
    tm = min(tm, L)
    nt = L // tm
    te = tm + 2 * HALO

    def body(c_ref, p_ref, n_ref, prm_ref, wi_ref, wo_ref, xo_ref, y_ref, b_ref):
        i = pl.program_id(0)
        xe = jnp.concatenate([p_ref[...], c_ref[...], n_ref[...]], axis=0)
        hn, _, _, _ = _nm(xe, prm_ref[3:4, :], prm_ref[0:1, :], prm_ref[1:2, :])
        hb = hn.astype(BF16)
        proj = jnp.concatenate([_dot(hb, wi_ref[s]) for s in range(4)], axis=1)
        bg, cg, xin = proj[:, :D], proj[:, D:2 * D], proj[:, 2 * D:]
        tpos = i * tm - HALO + lax.broadcasted_iota(jnp.int32, (te, 1), 0)
        valid = ((tpos >= 0) & (tpos < L)).astype(F32)
        yc = _conv3(cg * xin * valid, [prm_ref[4 + k:5 + k, :] for k in range(3)])
        h2 = (bg * yc)[HALO:HALO + tm].astype(BF16)
        yv = _dot(h2, wo_ref[...])
        y_ref[...] = yv.astype(BF16)
        xo_ref[...] = c_ref[...] + prm_ref[2:3, :] * yv
        b_ref[...] = proj[HALO:HALO + tm].astype(BF16)

    return pl.pallas_call(
        body, name=name, grid=(nt,),
        in_specs=[_tile(tm, D), _halo_prev(tm, D), _halo_next(tm, D, L), _full((8, D)),
                  _resident((None, 4, D, Ws), lambda i: (0, 0, 0, 0)),
                  _resident((None, D, D), lambda i: (0, 0, 0))],
        out_specs=[_tile(tm, D), _tile(tm, D), _tile(tm, 3 * D)],
        out_shape=[jax.ShapeDtypeStruct((L, D), F32), jax.ShapeDtypeStruct((L, D), BF16),
                   jax.ShapeDtypeStruct((L, 3 * D), BF16)],
        compiler_params=_params())(x, x, x, prm, wing, woutg)


def _conv_bwd(dout, x, y, bcx, prm, wing, woutg, name, tm=256, ride=None):
    L, D = x.shape
    Ws = wing.shape[-1]
    tm = min(tm, L)
    nt = L // tm
    te = tm + 2 * HALO

    def body(dc_ref, dp_ref, dn_ref, x_ref, y_ref, bc_ref, bp_ref, bn_ref, prm_ref, wi_ref, wo_ref,
             dx_ref, dpr_ref, h2_ref, dy_ref, hn_ref, acc_ref):
        i = pl.program_id(0)
        gain, shift, scale, gate = prm_ref[3:4, :], prm_ref[0:1, :], prm_ref[1:2, :], prm_ref[2:3, :]
        taps = [prm_ref[4 + k:5 + k, :] for k in range(3)]
        do = dc_ref[...]
        doe = _ext(dp_ref[...], do, dn_ref[...], i, nt)
        dye = (gate * doe).astype(BF16)
        dy_ref[...] = dye[HALO:HALO + tm]
        dh2 = _dot_nt(dye, wo_ref[...])
        be = jnp.concatenate([bp_ref[...], bc_ref[...], bn_ref[...]], axis=0).astype(F32)
        bg, cg, xin = be[:, :D], be[:, D:2 * D], be[:, 2 * D:]
        tpos = i * tm - HALO + lax.broadcasted_iota(jnp.int32, (te, 1), 0)
        valid = ((tpos >= 0) & (tpos < L)).astype(F32)
        z = cg * xin * valid
        yc = _conv3(z, taps)
        dyc = dh2 * bg
        h2_ref[...] = (bg * yc)[HALO:HALO + tm].astype(BF16)
        dz = _conv3(dyc, taps[::-1]) * valid
        dproj = jnp.concatenate([dh2 * yc, dz * xin, dz * cg], axis=1)[HALO:HALO + tm].astype(BF16)
        dpr_ref[...] = dproj
        dhn = jnp.zeros((tm, D), F32)
        for s in range(4):
            dhn = dhn + _dot_nt(dproj[:, s * Ws:(s + 1) * Ws], wi_ref[s])
        hn, xhat, r, nrm = _nm(x_ref[...], gain, shift, scale)
        hn_ref[...] = hn.astype(BF16)
        dx, dshift, dscale, dgn = _nm_bwd(dhn, xhat, r, nrm, gain, scale)
        dx_ref[...] = do + dx
        dgate = _sum0(do * y_ref[...].astype(F32))
        dtaps = [_sum0((dyc * _shift_rows(z, 1 - k))[HALO:HALO + tm]) for k in range(3)]
        _acc_rows(acc_ref, i == 0, [dshift, dscale, dgate, dgn] + dtaps)

    res, carried = _ride_call(
        body, ride, (dout, dout, dout, x, y, bcx, bcx, bcx, prm, wing, woutg), name=name, grid=(nt,),
        in_specs=[_tile(tm, D), _halo_prev(tm, D), _halo_next(tm, D, L), _tile(tm, D), _tile(tm, D),
                  _tile(tm, 3 * D), _halo_prev(tm, 3 * D), _halo_next(tm, 3 * D, L), _full((8, D)),
                  _resident((None, 4, D, Ws), lambda i: (0, 0, 0, 0)),
                  _resident((None, D, D), lambda i: (0, 0, 0))],
        out_specs=[_tile(tm, D), _tile(tm, 3 * D), _tile(tm, D), _tile(tm, D), _tile(tm, D), _full((8, D))],
        out_shape=[jax.ShapeDtypeStruct((L, D), F32), jax.ShapeDtypeStruct((L, 3 * D), BF16),
                   jax.ShapeDtypeStruct((L, D), BF16), jax.ShapeDtypeStruct((L, D), BF16),
                   jax.ShapeDtypeStruct((L, D), BF16), jax.ShapeDtypeStruct((8, D), F32)],
        compiler_params=_params())
    return (*res, carried)


def _loss_head(x, tgt, fg, name, tm=512):
    L, D = x.shape
    tm = min(tm, L)

    def body(x_ref, t_ref, g_ref, dx_ref, acc_ref):
        i = pl.program_id(0)
        xv = x_ref[...]
        g = g_ref[...]
        r = lax.rsqrt(jnp.mean(xv * xv, axis=-1, keepdims=True) + RMS_EPS)
        xhat = xv * r
        err = xhat * g - t_ref[...]
        part = 0.5 * jnp.sum(jnp.mean(err * err, axis=-1, keepdims=True), axis=0, keepdims=True)
        dy = err * (1.0 / D)
        dxh = dy * g
        dx_ref[...] = r * (dxh - xhat * jnp.mean(dxh * xhat, axis=-1, keepdims=True))
        _acc_rows(acc_ref, i == 0, [_sum0(dy * xhat), jnp.broadcast_to(part, (1, D))])

    return pl.pallas_call(
        body, name=name, grid=(L // tm,), in_specs=[_tile(tm, D), _tile(tm, D), _full((1, D))],
        out_specs=[_tile(tm, D), _full((8, D))],
        out_shape=[jax.ShapeDtypeStruct((L, D), F32), jax.ShapeDtypeStruct((8, D), F32)],
        compiler_params=_params())(x, tgt, fg)


def _mod_fwd(cond, mod_w, mod_b, name, tn=768):
    nl, D, N = mod_w.shape
    tn = min(tn, N)

    def body(c_ref, w_ref, b_ref, o_ref):
        cv = c_ref[...]
        s = (cv * _sigmoid(cv)).astype(BF16)
        o_ref[...] = _dot(s, w_ref[...].astype(BF16)) + b_ref[...]

    return pl.pallas_call(
        body, name=name, grid=(nl, N // tn),
        in_specs=[pl.BlockSpec((16, D), lambda l, j: (0, 0)), pl.BlockSpec((None, D, tn), lambda l, j: (l, 0, j)),
                  pl.BlockSpec((None, 1, tn), lambda l, j: (l, 0, j))],
        out_specs=pl.BlockSpec((None, 16, tn), lambda l, j: (l, 0, j)),
        out_shape=jax.ShapeDtypeStruct((nl, 16, N), F32), compiler_params=_params())(cond, mod_w, mod_b)


def _mod_bwd(cond, dm, mod_w, name, tn=768):
    nl, D, N = mod_w.shape
    tn = min(tn, N)

    def body(c_ref, d_ref, w_ref, dw_ref, dc_ref):
        first = (pl.program_id(0) == 0) & (pl.program_id(1) == 0)
        cv = c_ref[...]
        s = (cv * _sigmoid(cv)).astype(BF16)
        d = d_ref[...].astype(BF16)
        dw_ref[...] = _dot_tn(s, d)

        @pl.when(first)
        def _():
            dc_ref[...] = jnp.zeros((16, D), F32)

        dc_ref[...] += _dot_nt(d, w_ref[...].astype(BF16))

    return pl.pallas_call(
        body, name=name, grid=(nl, N // tn),
        in_specs=[pl.BlockSpec((16, D), lambda l, j: (0, 0)), pl.BlockSpec((None, 16, tn), lambda l, j: (l, 0, j)),
                  pl.BlockSpec((None, D, tn), lambda l, j: (l, 0, j))],
        out_specs=[pl.BlockSpec((None, D, tn), lambda l, j: (l, 0, j)), pl.BlockSpec((16, D), lambda l, j: (0, 0))],
        out_shape=[jax.ShapeDtypeStruct((nl, D, N), F32), jax.ShapeDtypeStruct((16, D), F32)],
        compiler_params=_params())(cond, dm, mod_w)


def _mod_small_grads(dm_all, cond, dsilu_parts, name):
    nl, _, N = dm_all.shape
    D = cond.shape[1]

    def body(d_ref, c_ref, p_ref, db_ref, dc_ref):
        for l in range(nl):
            db_ref[l] = _sum0(d_ref[l])
        tot = p_ref[0, 8:9, :]
        for k in range(1, N_CHIPS):
            tot = tot + p_ref[2 * k, 8:9, :]
        cv = c_ref[8:9, :]
        sg = _sigmoid(cv)
        dc_ref[...] = tot * (sg * (1.0 + cv * (1.0 - sg)))

    return pl.pallas_call(
        body, name=name, out_shape=[jax.ShapeDtypeStruct((nl, 1, N), F32), jax.ShapeDtypeStruct((1, D), F32)],
    )(dm_all, cond, dsilu_parts)


def _prm(rows, D):
    rows = [r.reshape(1, D) for r in rows]
    return jnp.concatenate(rows + [jnp.zeros((8 - len(rows), D), F32)], axis=0)


def kernel(x, c, ctx, c_ctx, mod_w, mod_b, norm_g, ffn_w13, ffn_w2, even_w_in, even_w_out, na_rpb, pool_w, pool_scale, conv_w_in, conv_w, conv_w_out, final_g, loss_target, m_c_ctx, m_mod_w, m_mod_b, m_norm_g, m_ffn_w13, m_ffn_w2, m_even_w_in, m_even_w_out, m_na_rpb, m_pool_w, m_pool_scale, m_conv_w_in, m_conv_w, m_conv_w_out, m_final_g, v_c_ctx, v_mod_w, v_mod_b, v_norm_g, v_ffn_w13, v_ffn_w2, v_even_w_in, v_even_w_out, v_na_rpb, v_pool_w, v_pool_scale, v_conv_w_in, v_conv_w, v_conv_w_out, v_final_g):
    xi, yi, ci = lax.axis_index("x"), lax.axis_index("y"), lax.axis_index("c")
    chip = 2 * xi + yi
    dev = 4 * xi + 2 * yi + ci
    _, L, D = x.shape
    C = ctx.shape[1]
    Ds = D // N_CHIPS
    Nm = mod_w.shape[-1]
    Fh = ffn_w13.shape[-1]
    Fq = ffn_w2.shape[2]
    assert ffn_w13.shape[:2] == (2, 2) and Fh == 2 * Fq and L % (GQ * GRID_W) == 0 and L // GRID_W >= GK and GQ == NA_KH // 2
    x0, ctx0, tgt = x[0], ctx[0], loss_target[0]

    pad = lambda a: jnp.pad(a, ((0, 0), (0, D - a.shape[1])))
    pack1 = jnp.concatenate([c, pad(norm_g.reshape(6, Ds)), pad(conv_w.reshape(3, Ds)), jnp.zeros((6, D), F32)], axis=0)
    g1 = _small_all_gather(pack1, "ag_cond")
    cond = jnp.concatenate([g1[:, 0], c_ctx[None], jnp.zeros((7, D), F32)], axis=0)
    norm_full = jnp.concatenate([g1[2 * k, 1:7, :Ds] for k in range(N_CHIPS)], axis=1).reshape(2, 3, D)
    convw_full = jnp.concatenate([g1[2 * k, 7:10, :Ds] for k in range(N_CHIPS)], axis=1)

    mod_b_loc = lax.dynamic_slice_in_dim(mod_b, chip * Nm, Nm, axis=1).reshape(2, 1, Nm)
    m_loc = _mod_fwd(cond, mod_w, mod_b_loc, "mod_fwd")
    g2 = _small_all_gather(m_loc.reshape(32, Nm), "ag_mod")
    m_all = jnp.concatenate([g2[2 * k] for k in range(N_CHIPS)], axis=1).reshape(2, 16, N_MOD, D)
    m_lat = lax.dynamic_index_in_dim(m_all, dev, axis=1, keepdims=False)
    m_ctx = m_all[:, 8]

    def prm(mods, layer, base, gain_idx, extra=()):
        return _prm([mods[layer, base], mods[layer, base + 1], mods[layer, base + 2], norm_full[layer, gain_idx],
                     *extra], D)

    def shard_bf16(w, name):
        return _cast_bf16(w.reshape(-1, w.shape[-1]), name).reshape(-1, *w.shape[-2:])

    w13s, w2s = shard_bf16(ffn_w13, "cast_w13"), shard_bf16(ffn_w2, "cast_w2")
    eins, eouts = shard_bf16(even_w_in, "cast_ein"), shard_bf16(even_w_out, "cast_eout")
    cins, couts = shard_bf16(conv_w_in, "cast_cin"), shard_bf16(conv_w_out, "cast_cout")
    ffn_shards = [[w13s[t:t + 1], w2s[t:t + 1]] for t in range(4)]

    def ffn_weights(w13g, w2g):
        return w13g.reshape(1, 4, D, Fh), w2g

    wf = [ffn_weights(*_gather_shards(ffn_shards[0], "ag_ffn0")), None, None, None]
    pos = jnp.stack([chip, ci]).astype(jnp.int32)

    p_f1 = prm(m_lat, 0, 0, 0)
    p_mx = prm(m_lat, 0, 3, 1)
    p_f2 = prm(m_lat, 0, 6, 2)
    p_g1 = prm(m_lat, 1, 0, 0)
    p_cv = prm(m_lat, 1, 3, 1, extra=(convw_full[0], convw_full[1], convw_full[2]))
    p_g2 = prm(m_lat, 1, 6, 2)
    pc_f1 = prm(m_ctx, 0, 0, 0)
    pc_mx = prm(m_ctx, 0, 3, 1)

    x1, ab1, y1, (eing, eoutg) = _ffn_fwd(x0, p_f1, *wf[0], 0, "ffn_fwd_l0a", ride=_broadcast_ride([eins, eouts]))
    eing = eing.reshape(1, 4, D, NA_WIDTH)
    ctx1, abc, yc, _ = _ffn_fwd(ctx0, pc_f1, *wf[0], 0, "ffn_fwd_ctx")
    q, k, v, u, hn_mx = _even_in_fwd(x1, p_mx, eing, "even_in_fwd")
    _, k_c, v_c, _, hn_cx = _even_in_fwd(ctx1, pc_mx, eing, "even_in_ctx")
    bias = _bias_table(na_rpb[0], "bias_table")
    att, gathered = _attn_fwd(q, k, v, k_c, v_c, bias, "attn_fwd", ride=_broadcast_ride(ffn_shards[1]))
    wf[1] = ffn_weights(*gathered)
    pw_b = _cast_bf16(pool_w.reshape(-1, POOL_GROUP_DIM), "cast_poolw").reshape(4, POOL_GROUP_DIM, POOL_GROUP_DIM)
    pool, dmx = _pool_fwd(u, pw_b, pool_scale, "pool_fwd")
    x2, ymx = _even_out_fwd(x1, att, pool, p_mx, eoutg, "even_out_fwd")
    x3, ab2, y2, gathered = _ffn_fwd(x2, p_f2, *wf[1], 0, "ffn_fwd_l0b",
                                     ride=_broadcast_ride(ffn_shards[2] + [cins, couts]))
    wf[2] = ffn_weights(*gathered[:2])
    cing, coutg = gathered[2].reshape(1, 4, D, conv_w_in.shape[-1]), gathered[3]
    x4, ab3, y3, gathered = _ffn_fwd(x3, p_g1, *wf[2], 0, "ffn_fwd_l1a", ride=_broadcast_ride(ffn_shards[3]))
    wf[3] = ffn_weights(*gathered)
    x5, ycv, bcx = _conv_fwd(x4, p_cv, cing, coutg, "conv_fwd")
    x6, ab4, y4, _ = _ffn_fwd(x5, p_g2, *wf[3], 0, "ffn_fwd_l1b")
    dx6, acc_head = _loss_head(x6, tgt, final_g.reshape(1, D), "loss_head")
    loss = lax.psum(acc_head[1, 0], ("x", "y", "c"))

    def ffn_back(dout, xin, ab, yy, p, t, tag, init13=None, init2=None, ride=None):
        sv, gact = ab
        dx, dab, dy, hn, acc, carried = _ffn_bwd(dout, xin, sv, yy, p, *wf[t], 0, f"ffn_bwd_{tag}", ride=ride)
        dw13 = _mm_tn(hn, dab, 4, False, f"dw13_{tag}", init=init13)
        dw2 = _mm_tn(gact, dy, 2, True, f"dw2_{tag}", init=init2)
        return dx, acc, dw13, dw2, carried

    dx5, acc_g2, dw13_3, dw2_3, _ = ffn_back(dx6, x5, ab4, y4, p_g2, 3, "l1b")
    s_a, sb_a = _pair_sums([dw13_3, dw2_3], pos, "l1b")
    dx4, dproj, h2, dycv, hn_cv, acc_cv, got_a = _conv_bwd(dx5, x4, ycv, bcx, p_cv, cing, coutg, "conv_bwd",
                                                           ride=_scatter_ride(sb_a))
    dcin = _mm_tn(hn_cv, dproj, 4, False, "dw_cin")
    dcout = _mm_tn(h2, dycv, 1, False, "dw_cout")
    s_b, sb_b = _pair_sums([dcin, dcout], pos, "conv")
    dx3, acc_g1, dw13_2, dw2_2, got_b = ffn_back(dx4, x3, ab3, y3, p_g1, 2, "l1a", ride=_scatter_ride(sb_b))
    s_c, sb_c = _pair_sums([dw13_2, dw2_2], pos, "l1a")
    dx2, acc_f2, dw13_1, dw2_1, got_c = ffn_back(dx3, x2, ab2, y2, p_f2, 1, "l0b", ride=_scatter_ride(sb_c))

    dymx, datt, dpool, acc_mxo = _even_out_bwd(dx2, ymx, p_mx, eoutg, "even_out_bwd")
    deout = jnp.concatenate([_mm_tn(att, dymx, 1, False, "dw_eout_att"),
                             _mm_tn(pool, dymx, 1, False, "dw_eout_pool")], axis=0)
    s_d, sb_d = _pair_sums([dw13_1, dw2_1, deout], pos, "l0b")
    du, dpoolw, acc_pool = _pool_bwd(dpool, dmx, pw_b, pool_scale, "pool_bwd")
    dq, dk, dv, dkc, dvc, dbias, got_d = _attn_bwd(q, k, v, k_c, v_c, bias, datt, "attn_bwd",
                                                   ride=_scatter_ride(sb_d))
    drpb = _rpb_grad(dbias, "rpb_grad")
    dx1, dstack, acc_mxi = _even_in_bwd(dx2, x1, dq, dk, dv, du, p_mx, eing,
                                        "even_in_bwd")
    zc = jnp.zeros((C, NA_WIDTH), F32)
    dctx1, dstack_c, accc_mx = _even_in_bwd(jnp.zeros((C, D), F32), ctx1, zc, dkc, dvc, zc,
                                            pc_mx, eing, "even_in_bwd_ctx")
    dein_c = _mm_tn(hn_cx, dstack_c, 4, False, "dw_ein_ctx")
    dein = _mm_tn(hn_mx, dstack, 4, False, "dw_ein", init=dein_c)
    s_e, sb_e = _pair_sums([dein], pos, "ein")
    _, accc_f1, dw13_c, dw2_c, _ = ffn_back(dctx1, ctx0, abc, yc, pc_f1, 0, "ctx")
    dx0, acc_f1, dw13_0, dw2_0, got_e = ffn_back(dx1, x0, ab1, y1, p_f1, 0, "l0a", init13=dw13_c, init2=dw2_c,
                                                 ride=_scatter_ride(sb_e))
    s_f, sb_f = _pair_sums([dw13_0, dw2_0], pos, "l0a")
    got_f = _run_ride(_scatter_ride(sb_f), "rs_scatter_l0a")

    z1 = jnp.zeros((1, D), F32)
    dm_lat = jnp.concatenate([acc_f1[0:3], acc_mxi[0:2], acc_mxo[2:3], acc_f2[0:3],
                              acc_g1[0:3], acc_cv[0:3], acc_g2[0:3]], axis=0)
    dm_ctx = jnp.concatenate([accc_f1[0:3], accc_mx[0:2]] + [z1] * 13, axis=0)
    dnorm = jnp.concatenate([acc_f1[3:4] + accc_f1[3:4], acc_mxi[3:4] + accc_mx[3:4], acc_f2[3:4],
                             acc_g1[3:4], acc_cv[3:4], acc_g2[3:4]], axis=0)
    rpb_flat = jnp.pad(drpb.reshape(-1), (0, 4 * D - drpb.size)).reshape(4, D)
    pack3 = jnp.concatenate([dm_lat, dm_ctx, dnorm, acc_cv[4:7], acc_head[0:1], pad(acc_pool[0:1]), z1,
                             dpoolw.reshape(-1, D), rpb_flat, jnp.zeros((4, D), F32)], axis=0)
    g3 = _small_all_gather(pack3, "ag_small")
    tot = _sum_devices(g3, "sum_small")
    dm_all = jnp.concatenate([g3[:, 0:18].reshape(8, 2, N_MOD * D).transpose(1, 0, 2),
                              tot[18:36].reshape(2, 1, N_MOD * D), jnp.zeros((2, 7, N_MOD * D), F32)], axis=1)
    dm_loc = lax.dynamic_slice_in_dim(dm_all, chip * Nm, Nm, axis=2)
    g_mod_w, dsilu = _mod_bwd(cond, dm_loc, mod_w, "mod_bwd")
    g4 = _small_all_gather(dsilu, "ag_dsilu")
    g_mod_b, g_c_ctx = _mod_small_grads(dm_all, cond, g4, "mod_small")
    g_mod_b = g_mod_b.reshape(2, N_MOD * D)
    g_c_ctx = g_c_ctx.reshape(D)
    g_norm_full = tot[36:42].reshape(2, 3, D)
    g_norm = lax.dynamic_slice_in_dim(g_norm_full, chip * Ds, Ds, axis=2)
    g_conv_w = lax.dynamic_slice_in_dim(tot[42:45], chip * Ds, Ds, axis=1).reshape(1, 3, Ds)
    g_final = tot[45]
    g_pscale = tot[46:47, :pool_scale.shape[1]]
    g_poolw = tot[48:112].reshape(pool_w.shape)
    g_rpb = tot[112:116].reshape(-1)[:na_rpb.size].reshape(na_rpb.shape)

    r13_3, r2_3 = _joins(s_a, got_a, pos, "l1b")
    r_cin, r_cout = _joins(s_b, got_b, pos, "conv")
    r13_2, r2_2 = _joins(s_c, got_c, pos, "l1a")
    r13_1, r2_1, r_eout = _joins(s_d, got_d, pos, "l0b")
    (r_ein,) = _joins(s_e, got_e, pos, "ein")
    r13_0, r2_0 = _joins(s_f, got_f, pos, "l0a")
    g_w13 = jnp.stack([r13_0, r13_1, r13_2, r13_3]).reshape(ffn_w13.shape)
    g_w2 = jnp.stack([r2_0, r2_1, r2_2, r2_3]).reshape(ffn_w2.shape)
    g_ein, g_eout, g_cin, g_cout = r_ein[None], r_eout[None], r_cin[None], r_cout[None]

    grads = [g_c_ctx, g_mod_w, g_mod_b, g_norm, g_w13, g_w2, g_ein, g_eout, g_rpb, g_poolw, g_pscale, g_cin,
             g_conv_w, g_cout, g_final]
    weights = [c_ctx, mod_w, mod_b, norm_g, ffn_w13, ffn_w2, even_w_in, even_w_out, na_rpb, pool_w, pool_scale,
               conv_w_in, conv_w, conv_w_out, final_g]
    ms = [m_c_ctx, m_mod_w, m_mod_b, m_norm_g, m_ffn_w13, m_ffn_w2, m_even_w_in, m_even_w_out, m_na_rpb, m_pool_w,
          m_pool_scale, m_conv_w_in, m_conv_w, m_conv_w_out, m_final_g]
    vs = [v_c_ctx, v_mod_w, v_mod_b, v_norm_g, v_ffn_w13, v_ffn_w2, v_even_w_in, v_even_w_out, v_na_rpb, v_pool_w,
          v_pool_scale, v_conv_w_in, v_conv_w, v_conv_w_out, v_final_g]
    names = ["c_ctx", "mod_w", "mod_b", "norm_g", "ffn_w13", "ffn_w2", "even_w_in", "even_w_out", "na_rpb", "pool_w",
             "pool_scale", "conv_w_in", "conv_w", "conv_w_out", "final_g"]
    deltas, new_m, new_v = [], [], []
    for n, w, g, m, vv in zip(names, weights, grads, ms, vs):
        g = g.reshape(w.shape)
        if w.ndim == 1:
            d, mn, vn = (t.reshape(w.shape) for t in _adamw(w[None], g[None], m[None], vv[None], f"adamw_{n}"))
        else:
            d, mn, vn = _adamw(w, g, m, vv, f"adamw_{n}")
        deltas.append(d)
        new_m.append(mn)
        new_v.append(vn)
    grads = [g.reshape(w.shape) for g, w in zip(grads, weights)]
    return (loss, dx0[None], *grads, *deltas, *new_m, *new_v)
```

```python
import jax
import jax.numpy as jnp
from jax import lax
from jax.experimental import pallas as pl
from jax.experimental.pallas import tpu as pltpu

F32 = jnp.float32
BF16 = jnp.bfloat16
MESH = pl.DeviceIdType.MESH

GRID_W = 64
NA_HEADS = 8
NA_HEAD_DIM = 64
NA_KH = 8
NA_KW = 16
GQ = 4
GK = GQ + NA_KH
NA_WIDTH = NA_HEADS * NA_HEAD_DIM
POOL_WINDOWS = (2, 4, 8, 16)
POOL_GROUP_DIM = 128
N_MOD = 9
RMS_EPS = 1e-6
NEG_INF = -1e30
ADAM_LR, ADAM_B1, ADAM_B2, ADAM_EPS, ADAM_WD, ADAM_STEP = 0.001, 0.9, 0.999, 1e-08, 0.01, 10

HALO = 16
VMEM_LIMIT = 56 * 1024 * 1024
N_CHIPS = 4
N_DEV = 8


def _dot(a, b):
    return jnp.dot(a, b, preferred_element_type=F32)


def _dot_nt(a, b):
    return lax.dot_general(a, b, (((1,), (1,)), ((), ())), preferred_element_type=F32)


def _dot_tn(a, b):
    return lax.dot_general(a, b, (((0,), (0,)), ((), ())), preferred_element_type=F32)


def _sigmoid(a):
    return 1.0 / (1.0 + jnp.exp(-a))


def _sum0(v):
    return jnp.sum(v, axis=0, keepdims=True)


def _nm(x, g, shift, scale):
    r = lax.rsqrt(jnp.mean(x * x, axis=-1, keepdims=True) + RMS_EPS)
    xhat = x * r
    nrm = xhat * g
    return nrm * (1.0 + scale) + shift, xhat, r, nrm


def _nm_bwd(dhn, xhat, r, nrm, g, scale):
    dshift = _sum0(dhn)
    dscale = _sum0(dhn * nrm)
    dnrm = dhn * (1.0 + scale)
    dgn = _sum0(dnrm * xhat)
    dxh = dnrm * g
    dx = r * (dxh - xhat * jnp.mean(dxh * xhat, axis=-1, keepdims=True))
    return dx, dshift, dscale, dgn


def _acc_rows(acc_ref, first, rows):
    @pl.when(first)
    def _():
        acc_ref[...] = jnp.zeros(acc_ref.shape, acc_ref.dtype)
    for k, row in enumerate(rows):
        if row is not None:
            acc_ref[k:k + 1, :] += row


def _shift_rows(v, k):
    n = v.shape[0]
    k = k % n
    return v if k == 0 else pltpu.roll(v, k, 0)


def _tile(tm, w):
    return pl.BlockSpec((tm, w), lambda i: (i, 0))


def _full(shape):
    nd = len(shape)
    return pl.BlockSpec(shape, lambda i: (0,) * nd)


def _resident(block, imap):
    return pl.BlockSpec(block, imap, pipeline_mode=pl.Buffered(1))


def _halo_prev(tm, w):
    return pl.BlockSpec((HALO, w), lambda i: (jnp.maximum(i * (tm // HALO) - 1, 0), 0))


def _halo_next(tm, w, L):
    return pl.BlockSpec((HALO, w), lambda i: (jnp.minimum((i + 1) * (tm // HALO), L // HALO - 1), 0))


def _params(vmem=VMEM_LIMIT):
    return pltpu.CompilerParams(vmem_limit_bytes=vmem)


def _pick_rows(rows, cols, itemsize=4, target=1 << 20):
    best = None
    for t in range(8, rows + 1, 8):
        if rows % t == 0 and t * cols * itemsize <= target:
            best = t
    return best if best is not None else rows


def _ext(prev, cur, nxt, i, nt):
    prev = jnp.where(i > 0, prev, jnp.zeros_like(prev))
    nxt = jnp.where(i < nt - 1, nxt, jnp.zeros_like(nxt))
    return jnp.concatenate([prev, cur, nxt], axis=0)


def _cast_bf16(a2d, name):
    rows, cols = a2d.shape
    tr = _pick_rows(rows, cols)

    def body(a_ref, o_ref):
        o_ref[...] = a_ref[...].astype(BF16)

    return pl.pallas_call(
        body, name=name, grid=(rows // tr,), in_specs=[_tile(tr, cols)], out_specs=_tile(tr, cols),
        out_shape=jax.ShapeDtypeStruct((rows, cols), BF16))(a2d)


def _sum_devices(g, name):
    n, rows, cols = g.shape
    tr = _pick_rows(rows, cols, target=1 << 18)

    def body(g_ref, o_ref):
        s = g_ref[0]
        for d in range(1, n):
            s = s + g_ref[d]
        o_ref[...] = s

    return pl.pallas_call(
        body, name=name, grid=(rows // tr,), in_specs=[pl.BlockSpec((n, tr, cols), lambda i: (0, i, 0))],
        out_specs=_tile(tr, cols), out_shape=jax.ShapeDtypeStruct((rows, cols), F32))(g)


def _adamw(w, g, m, v, name, ride=None):
    shape = w.shape
    cols = shape[-1]
    rows = w.size // cols
    w2, g2, m2, v2 = (t.reshape(rows, cols) for t in (w, g, m, v))
    tr = _pick_rows(rows, cols)
    c1 = 1.0 - ADAM_B1 ** ADAM_STEP
    c2 = 1.0 - ADAM_B2 ** ADAM_STEP

    def body(w_ref, g_ref, m_ref, v_ref, d_ref, mo_ref, vo_ref):
        gg = g_ref[...]
        mn = ADAM_B1 * m_ref[...] + (1.0 - ADAM_B1) * gg
        vn = ADAM_B2 * v_ref[...] + (1.0 - ADAM_B2) * (gg * gg)
        d_ref[...] = -ADAM_LR * ((mn / c1) / (jnp.sqrt(vn / c2) + ADAM_EPS) + ADAM_WD * w_ref[...])
        mo_ref[...] = mn
        vo_ref[...] = vn

    outs, carried = _ride_call(
        body, ride, (w2, g2, m2, v2), name=name, grid=(rows // tr,), in_specs=[_tile(tr, cols)] * 4,
        out_specs=[_tile(tr, cols)] * 3, out_shape=[jax.ShapeDtypeStruct((rows, cols), F32)] * 3)
    outs = tuple(o.reshape(shape) for o in outs)
    return outs if ride is None else (outs, carried)


def _mesh_pos():
    x, y, c = lax.axis_index("x"), lax.axis_index("y"), lax.axis_index("c")
    chips = [(1 - x, y), (x, 1 - y), (1 - x, 1 - y)]
    return x, y, c, chips


def _hbm_specs(n):
    return [pl.BlockSpec(memory_space=pltpu.HBM)] * n


def _small_all_gather(v, name):
    rows, w = v.shape

    def body(x_ref, out_ref, send_sems, recv_sems, local_sem):
        x, y, c, chips = _mesh_pos()
        me, sibling = (x, y, c), (x, y, 1 - c)

        def blk(px, py, pc):
            return out_ref.at[4 * px + 2 * py + pc]

        def copy(k, block, to, src=None):
            return pltpu.make_async_remote_copy(
                src_ref=blk(*block) if src is None else src, dst_ref=blk(*block),
                send_sem=send_sems.at[k], recv_sem=recv_sems.at[k], device_id=to, device_id_type=MESH)

        mine = pltpu.make_async_copy(x_ref, blk(*me), local_sem)
        mine.start()
        first = [copy(0, me, sibling, src=x_ref)]
        first += [copy(1 + j, me, (*chip, c), src=x_ref) for j, chip in enumerate(chips)]
        for cp in first:
            cp.start()
        passed = [copy(4 + j, (*chip, c), sibling) for j, chip in enumerate(chips)]
        for j, chip in enumerate(chips):
            copy(1 + j, (*chip, c), me).wait_recv()
            passed[j].start()
        copy(0, sibling, me).wait_recv()
        for j, chip in enumerate(chips):
            copy(4 + j, (*chip, 1 - c), me).wait_recv()
        for cp in first + passed:
            cp.wait_send()
        mine.wait()

    return pl.pallas_call(
        body, name=name, out_shape=jax.ShapeDtypeStruct((N_DEV, rows, w), v.dtype),
        in_specs=[pl.BlockSpec(memory_space=pltpu.VMEM)], out_specs=pl.BlockSpec(memory_space=pltpu.VMEM),
        scratch_shapes=[pltpu.SemaphoreType.DMA((7,)), pltpu.SemaphoreType.DMA((7,)), pltpu.SemaphoreType.DMA],
    )(v)


def _gather_shards(shards, name):
    n = len(shards)

    def body(*refs):
        ins, outs = refs[:n], refs[n:2 * n]
        send_sems, recv_sems, local_sems = refs[2 * n:]
        x, y, c, chips = _mesh_pos()
        k = 2 * x + y
        sibling = (x, y, 1 - c)

        def window(t, chip_k, half):
            r = ins[t].shape[1]
            return outs[t].at[:, pl.ds(chip_k * r + half * (r // 2), r // 2), :]

        def copy(t, j, chip_k, half, to, src=None):
            return pltpu.make_async_remote_copy(
                src_ref=window(t, chip_k, half) if src is None else src, dst_ref=window(t, chip_k, half),
                send_sem=send_sems.at[6 * t + j], recv_sem=recv_sems.at[6 * t + j], device_id=to, device_id_type=MESH)

        started, local = [], []
        for t in range(n):
            r = ins[t].shape[1]
            lc = pltpu.make_async_copy(ins[t], outs[t].at[:, pl.ds(k * r, r), :], local_sems.at[t])
            lc.start()
            local.append(lc)
            src = ins[t].at[:, pl.ds(c * (r // 2), r // 2), :]
            for j, chip in enumerate(chips):
                cp = copy(t, j, k, c, (*chip, c), src=src)
                cp.start()
                started.append(cp)
        for t in range(n):
            for j, chip in enumerate(chips):
                kj = 2 * chip[0] + chip[1]
                copy(t, j, kj, c, sibling).wait_recv()
                cp = copy(t, 3 + j, kj, c, sibling)
                cp.start()
                started.append(cp)
        for t in range(n):
            for j, chip in enumerate(chips):
                kj = 2 * chip[0] + chip[1]
                copy(t, 3 + j, kj, 1 - c, sibling).wait_recv()
        for cp in started:
            cp.wait_send()
        for lc in local:
            lc.wait()

    out_shape = [jax.ShapeDtypeStruct((s.shape[0], N_CHIPS * s.shape[1], s.shape[2]), s.dtype) for s in shards]
    return pl.pallas_call(
        body, name=name, out_shape=out_shape, in_specs=_hbm_specs(n), out_specs=_hbm_specs(n),
        scratch_shapes=[pltpu.SemaphoreType.DMA((6 * n,)), pltpu.SemaphoreType.DMA((6 * n,)),
                        pltpu.SemaphoreType.DMA((n,))],
    )(*shards)


def _chunk_rows(h, w):
    best = 16
    for t in range(16, h + 1, 16):
        if h % t == 0 and t * w * 4 <= (2 << 20):
            best = t
    return best


def _pair_sum(part, pos, name):
    _, h, w = part.shape
    cr = _chunk_rows(h, w)
    nc = h // cr
    n = 4 * nc
    slots = 4

    def body(pos_ref, own_ref, send_ref, s_ref, sb_ref, rbuf, send_sems, recv_sems):
        x, y, c, _ = _mesh_pos()
        k = pl.program_id(0)

        def copy(slot):
            return pltpu.make_async_remote_copy(
                src_ref=send_ref, dst_ref=rbuf.at[slot], send_sem=send_sems.at[slot], recv_sem=recv_sems.at[slot],
                device_id=(x, y, 1 - c), device_id_type=MESH)

        @pl.when(k < n)
        def _():
            copy(k % slots).start()

        @pl.when(k > 0)
        def _():
            before = (k + slots - 1) % slots
            copy(before).wait_recv()
            s = own_ref[...] + rbuf[before]
            s_ref[...] = s
            sb_ref[...] = s.astype(BF16)

        @pl.when(k < n)
        def _():
            copy(k % slots).wait_send()

    def own(k, p):
        j = jnp.maximum(k - 1, 0)
        return ((2 * (j // nc) + p[1]) * nc + j % nc, 0)

    def send(k, p):
        j = jnp.minimum(k, n - 1)
        return ((2 * (j // nc) + 1 - p[1]) * nc + j % nc, 0)

    grid_spec = pltpu.PrefetchScalarGridSpec(
        num_scalar_prefetch=1, grid=(n + 1,),
        in_specs=[pl.BlockSpec((cr, w), own), pl.BlockSpec((cr, w), send)],
        out_specs=[pl.BlockSpec((cr, w), lambda k, p: (jnp.maximum(k - 1, 0), 0))] * 2,
        scratch_shapes=[pltpu.VMEM((slots, cr, w), F32), pltpu.SemaphoreType.DMA((slots,)),
                        pltpu.SemaphoreType.DMA((slots,))])
    part2 = part.reshape(8 * h, w)
    s, sb = pl.pallas_call(
        body, name=name, grid_spec=grid_spec, compiler_params=_params(),
        out_shape=[jax.ShapeDtypeStruct((4 * h, w), F32), jax.ShapeDtypeStruct((4 * h, w), BF16)],
    )(pos, part2, part2)
    return s.reshape(4, h, w), sb.reshape(4, h, w)


class _Ride:
    def __init__(self, ins, out_shape, sems, copies):
        self.ins, self.out_shape, self.sems, self.copies = list(ins), list(out_shape), list(sems), copies

    def start(self, ins, outs, sems):
        sends, _, _, local = self.copies(ins, outs, sems)
        for cp in local + sends:
            cp.start()

    def finish(self, ins, outs, sems):
        _, recvs, sends, local = self.copies(ins, outs, sems)
        for cp in recvs:
            cp.wait_recv()
        for cp in sends:
            cp.wait_send()
        for cp in local:
            cp.wait()


def _scatter_ride(sums_bf16):
    n = len(sums_bf16)

    def copies(ins, outs, sems):
        send_sems, recv_sems = sems
        x, y, c, chips = _mesh_pos()
        cps = [pltpu.make_async_remote_copy(
            src_ref=ins[t].at[2 * chip[0] + chip[1]], dst_ref=outs[t].at[j],
            send_sem=send_sems.at[3 * t + j], recv_sem=recv_sems.at[3 * t + j],
            device_id=(*chip, c), device_id_type=MESH) for t in range(n) for j, chip in enumerate(chips)]
        return cps, cps, cps, []

    return _Ride(sums_bf16, [jax.ShapeDtypeStruct((3,) + s.shape[1:], BF16) for s in sums_bf16],
                 [pltpu.SemaphoreType.DMA((3 * n,)), pltpu.SemaphoreType.DMA((3 * n,))], copies)


def _broadcast_ride(shards):
    n = len(shards)

    def copies(ins, outs, sems):
        send_sems, recv_sems, local_sems = sems
        x, y, c, chips = _mesh_pos()
        k = 2 * x + y
        sends, recvs, local = [], [], []
        for t in range(n):
            r = ins[t].shape[1]
            h = r // 2
            local.append(pltpu.make_async_copy(ins[t], outs[t].at[:, pl.ds(k * r, r), :], local_sems.at[t]))
            src = ins[t].at[:, pl.ds(c * h, h), :]
            mine = outs[t].at[:, pl.ds(k * r + c * h, h), :]
            for j, chip in enumerate(chips):
                kj = 2 * chip[0] + chip[1]
                for d in range(2):
                    sends.append(pltpu.make_async_remote_copy(
                        src_ref=src, dst_ref=mine, send_sem=send_sems.at[6 * t + 2 * j + d],
                        recv_sem=recv_sems.at[6 * t + 2 * j + c], device_id=(*chip, d), device_id_type=MESH))
                    theirs = outs[t].at[:, pl.ds(kj * r + d * h, h), :]
                    recvs.append(pltpu.make_async_remote_copy(
                        src_ref=theirs, dst_ref=theirs, send_sem=send_sems.at[6 * t + 2 * j + d],
                        recv_sem=recv_sems.at[6 * t + 2 * j + d], device_id=(*chip, d), device_id_type=MESH))
        return sends, recvs, sends, local

    return _Ride(shards, [jax.ShapeDtypeStruct((s.shape[0], N_CHIPS * s.shape[1], s.shape[2]), s.dtype) for s in shards],
                 [pltpu.SemaphoreType.DMA((6 * n,)), pltpu.SemaphoreType.DMA((6 * n,)), pltpu.SemaphoreType.DMA((n,))],
                 copies)


def _ride_call(body, ride, args, *, name, grid, in_specs, out_specs, out_shape, compiler_params=None):
    in_specs, out_specs, out_shape = list(in_specs), list(out_specs), list(out_shape)
    if ride is None:
        res = pl.pallas_call(body, name=name, grid=grid, in_specs=in_specs, out_specs=out_specs, out_shape=out_shape,
                             compiler_params=compiler_params)(*args)
        return list(res), []
    ni, no, ri, ro = len(in_specs), len(out_specs), len(ride.ins), len(ride.out_shape)

    def at_step(pick):
        hit = None
        for d, n in enumerate(grid):
            here = pl.program_id(d) == pick(n)
            hit = here if hit is None else hit & here
        return hit

    def carried(*refs):
        ins, rins = refs[:ni], refs[ni:ni + ri]
        outs, routs = refs[ni + ri:ni + ri + no], refs[ni + ri + no:ni + ri + no + ro]
        sems = refs[ni + ri + no + ro:]

        @pl.when(at_step(lambda n: 0))
        def _():
            ride.start(rins, routs, sems)

        body(*ins, *outs)

        @pl.when(at_step(lambda n: n - 1))
        def _():
            ride.finish(rins, routs, sems)

    res = pl.pallas_call(
        carried, name=name, grid=grid, in_specs=in_specs + _hbm_specs(ri), out_specs=out_specs + _hbm_specs(ro),
        out_shape=out_shape + ride.out_shape, scratch_shapes=ride.sems, compiler_params=compiler_params,
    )(*args, *ride.ins)
    return list(res[:no]), list(res[no:])


def _sum_and_join(sums, got, pos, name):
    _, h, w = sums.shape
    cr = _chunk_rows(h, w)

    def body(pos_ref, mine_ref, got_ref, o_ref, ebuf, rbuf, send_sems, recv_sems):
        x, y, c, _ = _mesh_pos()
        slot = pl.program_id(0) % 2
        e = mine_ref[...]
        for j in range(3):
            e = e + got_ref[j].astype(F32)
        ebuf[slot] = e
        cp = pltpu.make_async_remote_copy(
            src_ref=ebuf.at[slot], dst_ref=rbuf.at[slot], send_sem=send_sems.at[slot], recv_sem=recv_sems.at[slot],
            device_id=(x, y, 1 - c), device_id_type=MESH)
        cp.start()
        o_ref[pos_ref[1]] = e
        cp.wait_recv()
        o_ref[1 - pos_ref[1]] = rbuf[slot]
        cp.wait_send()

    grid_spec = pltpu.PrefetchScalarGridSpec(
        num_scalar_prefetch=1, grid=(h // cr,),
        in_specs=[pl.BlockSpec((None, cr, w), lambda i, p: (p[0], i, 0)),
                  pl.BlockSpec((3, cr, w), lambda i, p: (0, i, 0))],
        out_specs=pl.BlockSpec((2, cr, w), lambda i, p: (0, i, 0)),
        scratch_shapes=[pltpu.VMEM((2, cr, w), F32), pltpu.VMEM((2, cr, w), F32),
                        pltpu.SemaphoreType.DMA((2,)), pltpu.SemaphoreType.DMA((2,))])
    return pl.pallas_call(
        body, name=name, grid_spec=grid_spec, compiler_params=_params(),
        out_shape=jax.ShapeDtypeStruct((2, h, w), F32),
    )(pos, sums, got)


def _pair_sums(parts, pos, tag):
    pairs = [_pair_sum(p.reshape(8, p.shape[0] // 8, p.shape[1]), pos, f"rs_pair_{tag}_{t}")
             for t, p in enumerate(parts)]
    return [s for s, _ in pairs], [sb for _, sb in pairs]


def _joins(sums, got, pos, tag):
    out = []
    for t, (s, r) in enumerate(zip(sums, got)):
        full = _sum_and_join(s, r, pos, f"rs_join_{tag}_{t}")
        out.append(full.reshape(2 * full.shape[1], full.shape[2]))
    return out


def _ffn_fwd(x, prm, w13g, w2g, t, name, tm=512, ride=None):
    L, D = x.shape
    Fh = w13g.shape[-1]
    tm = min(tm, L)

    def body(x_ref, p_ref, w13_ref, w2_ref, xo_ref, sv_ref, g_ref, y_ref):
        xv = x_ref[...]
        hn, _, _, _ = _nm(xv, p_ref[3:4, :], p_ref[0:1, :], p_ref[1:2, :])
        hb = hn.astype(BF16)
        acc = jnp.zeros((tm, D), F32)
        for j in range(2):
            a = _dot(hb, w13_ref[j])
            b = _dot(hb, w13_ref[2 + j])
            sg = _sigmoid(a)
            sa = a * sg
            sv_ref[:, j * Fh:(j + 1) * Fh] = sa.astype(BF16)
            sv_ref[:, (2 + j) * Fh:(3 + j) * Fh] = (b * (sg * (1.0 + a * (1.0 - sg)))).astype(BF16)
            g = (sa * b).astype(BF16)
            g_ref[:, j * Fh:(j + 1) * Fh] = g
            acc = acc + _dot(g, w2_ref[j * Fh:(j + 1) * Fh, :])
        y_ref[...] = acc.astype(BF16)
        xo_ref[...] = xv + (0.5 * p_ref[2:3, :]) * acc

    res, carried = _ride_call(
        body, ride, (x, prm, w13g, w2g), name=name, grid=(L // tm,),
        in_specs=[_tile(tm, D), _full((8, D)),
                  _resident((None, 4, D, Fh), lambda i: (t, 0, 0, 0)),
                  _resident((None, 2 * Fh, D), lambda i: (t, 0, 0))],
        out_specs=[_tile(tm, D), _tile(tm, 4 * Fh), _tile(tm, 2 * Fh), _tile(tm, D)],
        out_shape=[jax.ShapeDtypeStruct((L, D), F32), jax.ShapeDtypeStruct((L, 4 * Fh), BF16),
                   jax.ShapeDtypeStruct((L, 2 * Fh), BF16), jax.ShapeDtypeStruct((L, D), BF16)],
        compiler_params=_params())
    xo, sv, g, y = res
    return xo, (sv, g), y, carried


def _ffn_bwd(dout, x, sv, y, prm, w13g, w2g, t, name, tm=256, ride=None):
    L, D = x.shape
    Fh = w13g.shape[-1]
    tm = min(tm, L)

    def body(do_ref, x_ref, sv_ref, y_ref, p_ref, w13_ref, w2_ref, dx_ref, dab_ref, dy_ref, hn_ref, acc_ref):
        i = pl.program_id(0)
        do = do_ref[...]
        gain, shift, scale, gate = p_ref[3:4, :], p_ref[0:1, :], p_ref[1:2, :], p_ref[2:3, :]
        hn, xhat, r, nrm = _nm(x_ref[...], gain, shift, scale)
        hn_ref[...] = hn.astype(BF16)
        dgate = 0.5 * _sum0(do * y_ref[...].astype(F32))
        dyb = ((0.5 * gate) * do).astype(BF16)
        dy_ref[...] = dyb
        dhn = jnp.zeros((tm, D), F32)
        for j in range(2):
            dg = _dot_nt(dyb, w2_ref[j * Fh:(j + 1) * Fh, :])
            da = (dg * sv_ref[:, (2 + j) * Fh:(3 + j) * Fh].astype(F32)).astype(BF16)
            db = (dg * sv_ref[:, j * Fh:(j + 1) * Fh].astype(F32)).astype(BF16)
            dab_ref[:, j * Fh:(j + 1) * Fh] = da
            dab_ref[:, (2 + j) * Fh:(3 + j) * Fh] = db
            dhn = dhn + _dot_nt(da, w13_ref[j]) + _dot_nt(db, w13_ref[2 + j])
        dx, dshift, dscale, dgn = _nm_bwd(dhn, xhat, r, nrm, gain, scale)
        dx_ref[...] = do + dx
        _acc_rows(acc_ref, i == 0, [dshift, dscale, dgate, dgn])

    res, carried = _ride_call(
        body, ride, (dout, x, sv, y, prm, w13g, w2g), name=name, grid=(L // tm,),
        in_specs=[_tile(tm, D), _tile(tm, D), _tile(tm, 4 * Fh), _tile(tm, D), _full((8, D)),
                  _resident((None, 4, D, Fh), lambda i: (t, 0, 0, 0)),
                  _resident((None, 2 * Fh, D), lambda i: (t, 0, 0))],
        out_specs=[_tile(tm, D), _tile(tm, 4 * Fh), _tile(tm, D), _tile(tm, D), _full((8, D))],
        out_shape=[jax.ShapeDtypeStruct((L, D), F32), jax.ShapeDtypeStruct((L, 4 * Fh), BF16),
                   jax.ShapeDtypeStruct((L, D), BF16), jax.ShapeDtypeStruct((L, D), BF16),
                   jax.ShapeDtypeStruct((8, D), F32)],
        compiler_params=_params())
    return (*res, carried)


def _mm_tn(a, b, slabs, a_slabbed, name, init=None, tl=1024, ride=None):
    L = a.shape[0]
    ka = a.shape[1] // slabs if a_slabbed else a.shape[1]
    nb = b.shape[1] if a_slabbed else b.shape[1] // slabs
    tl = min(tl, L)
    has_init = init is not None

    def body(a_ref, b_ref, *rest):
        o_ref = rest[-1]
        step = pl.program_id(1)

        @pl.when(step == 0)
        def _():
            o_ref[...] = rest[0][...] if has_init else jnp.zeros((ka, nb), F32)

        o_ref[...] += _dot_tn(a_ref[...], b_ref[...])

    in_specs = [pl.BlockSpec((tl, ka), (lambda s, l: (l, s)) if a_slabbed else (lambda s, l: (l, 0))),
                pl.BlockSpec((tl, nb), (lambda s, l: (l, 0)) if a_slabbed else (lambda s, l: (l, s)))]
    args = [a, b]
    if has_init:
        in_specs.append(pl.BlockSpec((ka, nb), lambda s, l: (s, 0)))
        args.append(init)
    res, carried = _ride_call(
        body, ride, args, name=name, grid=(slabs, L // tl), in_specs=in_specs,
        out_specs=[pl.BlockSpec((ka, nb), lambda s, l: (s, 0))],
        out_shape=[jax.ShapeDtypeStruct((slabs * ka, nb), F32)], compiler_params=_params())
    return res[0] if ride is None else (res[0], carried)


def _even_in_fwd(x, prm, wing, name, tm=512):
    L, D = x.shape
    W = wing.shape[-1]
    tm = min(tm, L)

    def body(x_ref, p_ref, w_ref, q_ref, k_ref, v_ref, u_ref, hn_ref):
        hn, _, _, _ = _nm(x_ref[...], p_ref[3:4, :], p_ref[0:1, :], p_ref[1:2, :])
        hb = hn.astype(BF16)
        hn_ref[...] = hb
        q_ref[...] = _dot(hb, w_ref[0]).astype(BF16)
        k_ref[...] = _dot(hb, w_ref[1]).astype(BF16)
        v_ref[...] = _dot(hb, w_ref[2]).astype(BF16)
        u_ref[...] = _dot(hb, w_ref[3])

    return pl.pallas_call(
        body, name=name, grid=(L // tm,),
        in_specs=[_tile(tm, D), _full((8, D)), _resident((None, 4, D, W), lambda i: (0, 0, 0, 0))],
        out_specs=[_tile(tm, W)] * 4 + [_tile(tm, D)],
        out_shape=[jax.ShapeDtypeStruct((L, W), BF16)] * 3 + [jax.ShapeDtypeStruct((L, W), F32),
                                                              jax.ShapeDtypeStruct((L, D), BF16)],
        compiler_params=_params())(x, prm, wing)


def _even_in_bwd(dout, x, dq, dk, dv, du, prm, wing, name, tm=512):
    L, D = x.shape
    W = wing.shape[-1]
    tm = min(tm, L)

    def body(do_ref, x_ref, dq_ref, dk_ref, dv_ref, du_ref, p_ref, w_ref, dx_ref, ds_ref, acc_ref):
        i = pl.program_id(0)
        gain, shift, scale = p_ref[3:4, :], p_ref[0:1, :], p_ref[1:2, :]
        _, xhat, r, nrm = _nm(x_ref[...], gain, shift, scale)
        dhn = jnp.zeros((tm, D), F32)
        for s, ref in enumerate((dq_ref, dk_ref, dv_ref, du_ref)):
            d = ref[...].astype(BF16)
            ds_ref[:, s * W:(s + 1) * W] = d
            dhn = dhn + _dot_nt(d, w_ref[s])
        dx, dshift, dscale, dgn = _nm_bwd(dhn, xhat, r, nrm, gain, scale)
        dx_ref[...] = do_ref[...] + dx
        _acc_rows(acc_ref, i == 0, [dshift, dscale, None, dgn])

    return pl.pallas_call(
        body, name=name, grid=(L // tm,),
        in_specs=[_tile(tm, D), _tile(tm, D)] + [_tile(tm, W)] * 4 +
                 [_full((8, D)), _resident((None, 4, D, W), lambda i: (0, 0, 0, 0))],
        out_specs=[_tile(tm, D), _tile(tm, 4 * W), _full((8, D))],
        out_shape=[jax.ShapeDtypeStruct((L, D), F32), jax.ShapeDtypeStruct((L, 4 * W), BF16),
                   jax.ShapeDtypeStruct((8, D), F32)],
        compiler_params=_params())(dout, x, dq, dk, dv, du, prm, wing)


def _even_out_fwd(x, att, pool, prm, woutg, name, tm=512):
    L, D = x.shape
    W = D // 2
    tm = min(tm, L)

    def body(x_ref, a_ref, p_ref, prm_ref, w_ref, xo_ref, y_ref):
        yv = _dot(a_ref[...], w_ref[0:W, :]) + _dot(p_ref[...], w_ref[W:2 * W, :])
        y_ref[...] = yv.astype(BF16)
        xo_ref[...] = x_ref[...] + prm_ref[2:3, :] * yv

    return pl.pallas_call(
        body, name=name, grid=(L // tm,),
        in_specs=[_tile(tm, D), _tile(tm, W), _tile(tm, W), _full((8, D)),
                  _resident((None, D, D), lambda i: (0, 0, 0))],
        out_specs=[_tile(tm, D), _tile(tm, D)],
        out_shape=[jax.ShapeDtypeStruct((L, D), F32), jax.ShapeDtypeStruct((L, D), BF16)],
        compiler_params=_params())(x, att, pool, prm, woutg)


def _even_out_bwd(dout, y, prm, woutg, name, tm=512):
    L, D = dout.shape
    W = D // 2
    tm = min(tm, L)

    def body(do_ref, y_ref, p_ref, w_ref, dy_ref, da_ref, dp_ref, acc_ref):
        i = pl.program_id(0)
        do = do_ref[...]
        dgate = _sum0(do * y_ref[...].astype(F32))
        dyb = (p_ref[2:3, :] * do).astype(BF16)
        dy_ref[...] = dyb
        da_ref[...] = _dot_nt(dyb, w_ref[0:W, :]).astype(BF16)
        dp_ref[...] = _dot_nt(dyb, w_ref[W:2 * W, :])
        _acc_rows(acc_ref, i == 0, [None, None, dgate])

    return pl.pallas_call(
        body, name=name, grid=(L // tm,),
        in_specs=[_tile(tm, D), _tile(tm, D), _full((8, D)), _resident((None, D, D), lambda i: (0, 0, 0))],
        out_specs=[_tile(tm, D), _tile(tm, W), _tile(tm, W), _full((8, D))],
        out_shape=[jax.ShapeDtypeStruct((L, D), BF16), jax.ShapeDtypeStruct((L, W), BF16),
                   jax.ShapeDtypeStruct((L, W), F32), jax.ShapeDtypeStruct((8, D), F32)],
        compiler_params=_params())(dout, y, prm, woutg)


def _group_ri(variant, qr, kr):
    first_key = (0, qr, GK - NA_KH)[variant]
    if not first_key <= kr < first_key + NA_KH:
        return None
    return kr - qr + (NA_KH - 1, NA_KH - 1 - NA_KH // 2, NA_KH - 1 - (GK - GQ))[variant]


def _bias_table(rpb, name):
    H = rpb.shape[0]
    nri, nci = 2 * NA_KH - 1, 2 * NA_KW - 1
    col = jnp.arange(GRID_W)
    rel = (col[None, :] - col[:, None] + (NA_KW - 1)).reshape(1, -1)
    onehot = (rel == jnp.arange(32)[:, None]).astype(F32)
    cs = jnp.clip(col - NA_KW // 2, 0, GRID_W - NA_KW)
    ok = ((col[None, :] >= cs[:, None]) & (col[None, :] < cs[:, None] + NA_KW)).astype(F32).reshape(1, -1)
    rpb2 = jnp.pad(rpb.reshape(H * nri, nci), ((0, 0), (0, 32 - nci)))

    def body(r_ref, e_ref, m_ref, o_ref):
        t = jnp.dot(r_ref[...], e_ref[...], preferred_element_type=F32, precision=lax.Precision.HIGHEST)
        o_ref[...] = jnp.where(m_ref[...] > 0.0, t, NEG_INF)

    tab = pl.pallas_call(body, name=name, out_shape=jax.ShapeDtypeStruct((H * nri, GRID_W * GRID_W), F32))(
        rpb2, onehot, ok)
    tab = tab.reshape(H, nri, GRID_W, GRID_W)
    outside = jnp.full((H, GRID_W, GRID_W), NEG_INF, F32)
    variants = []
    for variant in range(3):
        rows = []
        for qr in range(GQ):
            ris = [_group_ri(variant, qr, kr) for kr in range(GK)]
            rows.append(jnp.concatenate([outside if ri is None else tab[:, ri] for ri in ris], axis=2))
        variants.append(jnp.concatenate(rows, axis=1))
    return jnp.stack(variants, axis=1)


def _attn_probs(q, kw, kc, bias, scale):
    s_w = _dot_nt(q, kw) * scale + bias
    s_c = _dot_nt(q, kc) * scale
    m = jnp.maximum(jnp.max(s_w, axis=-1, keepdims=True), jnp.max(s_c, axis=-1, keepdims=True))
    e_w = jnp.exp(s_w - m)
    e_c = jnp.exp(s_c - m)
    inv = 1.0 / (jnp.sum(e_w, axis=-1, keepdims=True) + jnp.sum(e_c, axis=-1, keepdims=True))
    return e_w * inv, e_c * inv


def _group_place(g, R):
    G = R // GQ
    kb = jnp.clip(g * GQ - NA_KH // 2, 0, R - GK)
    variant = jnp.where(g == 0, 0, jnp.where(g == G - 1, 2, 1))
    return pl.multiple_of(g * (GQ * GRID_W), GQ * GRID_W), pl.multiple_of(kb * GRID_W, GRID_W), variant


def _lane_masks(width, dh):
    lane = lax.broadcasted_iota(jnp.int32, (1, width), 1)
    return [(lane >= h * dh) & (lane < (h + 1) * dh) for h in range(width // dh)]


def _only(mask, a):
    return jnp.where(mask, a, jnp.zeros_like(a))


def _stack_heads(bias, hpb):
    H, nv, nq, nk = bias.shape
    return bias.reshape(H // hpb, hpb, nv, nq, nk).transpose(0, 2, 1, 3, 4).reshape(H // hpb, nv, hpb * nq, nk)


def _unstack_heads(stacked, hpb):
    P, nv, rows, nk = stacked.shape
    return stacked.reshape(P, nv, hpb, rows // hpb, nk).transpose(0, 2, 1, 3, 4).reshape(P * hpb, nv, rows // hpb, nk)


def _attn_fwd(q, k, v, kc, vc, bias, name, ride=None):
    L, width = q.shape
    C = kc.shape[0]
    dh = NA_HEAD_DIM
    lanes = 128
    hpb = lanes // dh
    R = L // GRID_W
    nq, nk = GQ * GRID_W, GK * GRID_W
    scale = dh ** -0.5

    def body(q_ref, k_ref, v_ref, kc_ref, vc_ref, b_ref, o_ref):
        masks = _lane_masks(lanes, dh)
        kc2 = kc_ref[...]
        vcs = [_only(m, vc_ref[...]) for m in masks]

        def group(g, carry):
            q0, k0, variant = _group_place(g, R)
            q2 = q_ref[pl.ds(q0, nq), :]
            k2 = k_ref[pl.ds(k0, nk), :]
            v2 = v_ref[pl.ds(k0, nk), :]
            qs = jnp.concatenate([_only(m, q2) for m in masks], axis=0)
            p_w, p_c = _attn_probs(qs, k2, kc2, b_ref[variant], scale)
            p_w, p_c = p_w.astype(BF16), p_c.astype(BF16)
            o2 = jnp.zeros((nq, lanes), F32)
            for h, m in enumerate(masks):
                rows = slice(h * nq, (h + 1) * nq)
                o2 = o2 + _dot(p_w[rows], _only(m, v2)) + _dot(p_c[rows], vcs[h])
            o_ref[pl.ds(q0, nq), :] = o2.astype(BF16)
            return carry

        lax.fori_loop(0, R // GQ, group, 0)

    cols = lambda n: pl.BlockSpec((n, lanes), lambda p: (0, p))
    res, carried = _ride_call(
        body, ride, (q, k, v, kc, vc, _stack_heads(bias, hpb)), name=name, grid=(width // lanes,),
        in_specs=[cols(L), cols(L), cols(L), cols(C), cols(C),
                  pl.BlockSpec((None, 3, hpb * nq, nk), lambda p: (p, 0, 0, 0))],
        out_specs=[cols(L)], out_shape=[jax.ShapeDtypeStruct((L, width), BF16)],
        compiler_params=_params())
    return res[0], carried


def _attn_bwd(q, k, v, kc, vc, bias, do, name, ride=None):
    L, width = q.shape
    C = kc.shape[0]
    dh = NA_HEAD_DIM
    lanes = 128
    hpb = lanes // dh
    R = L // GRID_W
    nq, nk = GQ * GRID_W, GK * GRID_W
    scale = dh ** -0.5

    def body(q_ref, k_ref, v_ref, kc_ref, vc_ref, b_ref, do_ref, dq_ref, dk_ref, dv_ref, dkc_ref, dvc_ref, db_ref):
        masks = _lane_masks(lanes, dh)
        kc2 = kc_ref[...]
        vc2 = vc_ref[...]
        kcs = [_only(m, kc2) for m in masks]
        dk_ref[...] = jnp.zeros((L, lanes), F32)
        dv_ref[...] = jnp.zeros((L, lanes), F32)
        dkc_ref[...] = jnp.zeros((C, lanes), F32)
        dvc_ref[...] = jnp.zeros((C, lanes), F32)
        db_ref[...] = jnp.zeros((3, hpb * nq, nk), F32)

        def group(g, carry):
            q0, k0, variant = _group_place(g, R)
            q2 = q_ref[pl.ds(q0, nq), :]
            k2 = k_ref[pl.ds(k0, nk), :]
            v2 = v_ref[pl.ds(k0, nk), :]
            do2 = do_ref[pl.ds(q0, nq), :]
            qs = jnp.concatenate([_only(m, q2) for m in masks], axis=0)
            dos = jnp.concatenate([_only(m, do2) for m in masks], axis=0)
            p_w, p_c = _attn_probs(qs, k2, kc2, b_ref[variant], scale)
            dp_w = _dot_nt(dos, v2)
            dp_c = _dot_nt(dos, vc2)
            delta = jnp.sum(p_w * dp_w, axis=-1, keepdims=True) + jnp.sum(p_c * dp_c, axis=-1, keepdims=True)
            ds_w = p_w * (dp_w - delta)
            ds_c = p_c * (dp_c - delta)
            db_ref[variant] += ds_w
            dsw = (ds_w * scale).astype(BF16)
            dsc = (ds_c * scale).astype(BF16)
            dq2 = jnp.zeros((nq, lanes), F32)
            for h, m in enumerate(masks):
                rows = slice(h * nq, (h + 1) * nq)
                dq2 = dq2 + _dot(dsw[rows], _only(m, k2)) + _dot(dsc[rows], kcs[h])
            dq_ref[pl.ds(q0, nq), :] = dq2.astype(BF16)
            dk_ref[pl.ds(k0, nk), :] += _dot_tn(dsw, qs)
            dv_ref[pl.ds(k0, nk), :] += _dot_tn(p_w.astype(BF16), dos)
            dkc_ref[...] += _dot_tn(dsc, qs)
            dvc_ref[...] += _dot_tn(p_c.astype(BF16), dos)
            return carry

        lax.fori_loop(0, R // GQ, group, 0)

    cols = lambda n: _resident((n, lanes), lambda p: (0, p))
    bspec = _resident((None, 3, hpb * nq, nk), lambda p: (p, 0, 0, 0))
    res, carried = _ride_call(
        body, ride, (q, k, v, kc, vc, _stack_heads(bias, hpb), do), name=name, grid=(width // lanes,),
        in_specs=[cols(L), cols(L), cols(L), cols(C), cols(C), bspec, cols(L)],
        out_specs=[cols(L), cols(L), cols(L), cols(C), cols(C), bspec],
        out_shape=[jax.ShapeDtypeStruct((L, width), BF16)] + [jax.ShapeDtypeStruct((L, width), F32)] * 2 +
                  [jax.ShapeDtypeStruct((C, width), F32)] * 2 +
                  [jax.ShapeDtypeStruct((width // lanes, 3, hpb * nq, nk), F32)],
        compiler_params=_params())
    res[5] = _unstack_heads(res[5], hpb)
    return (*res, carried)


def _rpb_grad(dbias, name):
    H = dbias.shape[0]
    nri, nci = 2 * NA_KH - 1, 2 * NA_KW - 1
    d6 = dbias.reshape(H, 3, GQ, GRID_W, GK, GRID_W).transpose(0, 1, 2, 4, 3, 5)
    col = jnp.arange(GRID_W)
    onehot = (col[None, None, :] - col[None, :, None] + (NA_KW - 1) == jnp.arange(32)[:, None, None]).astype(F32)
    places = [(v, qr, kr) for v in range(3) for qr in range(GQ) for kr in range(GK)]

    def body(d_ref, m_ref, o_ref, t_ref):
        t_ref[...] = jnp.zeros((32, GRID_W), F32)
        o_ref[...] = jnp.zeros((16, 32, 128), F32)
        for ri in range(nri):
            a = None
            for place in places:
                if _group_ri(*place) == ri:
                    blk = d_ref[place]
                    a = blk if a is None else a + blk
            for ci in range(nci):
                t_ref[ci:ci + 1, :] = _sum0(a * m_ref[ci])
            o_ref[ri] = jnp.broadcast_to(jnp.sum(t_ref[...], axis=1, keepdims=True), (32, 128))

    out = pl.pallas_call(
        body, name=name, grid=(H,),
        in_specs=[pl.BlockSpec((None, 3, GQ, GK, GRID_W, GRID_W), lambda h: (h, 0, 0, 0, 0, 0)),
                  pl.BlockSpec((32, GRID_W, GRID_W), lambda h: (0, 0, 0))],
        out_specs=pl.BlockSpec((None, 16, 32, 128), lambda h: (h, 0, 0, 0)),
        out_shape=jax.ShapeDtypeStruct((H, 16, 32, 128), F32),
        scratch_shapes=[pltpu.VMEM((32, GRID_W), F32)])(d6, onehot)
    return out[:, :nri, :nci, 0]


def _window_count(t, w, L):
    lo = jnp.clip(t - w // 2, 0, L)
    hi = jnp.clip(t - w // 2 + w, 0, L)
    return jnp.maximum(hi - lo, 1).astype(F32)


def _running_sum(v, w):
    k = 1
    while k < w:
        v = v + _shift_rows(v, k)
        k *= 2
    return v


def _pool_fwd(u, poolw, pscale, name, tm=512):
    L, W = u.shape
    G = POOL_GROUP_DIM
    tm = min(tm, L)
    nt = L // tm

    def body(c_ref, p_ref, n_ref, w_ref, s_ref, o_ref, dm_ref):
        i = pl.program_id(0)
        ext = _ext(p_ref[...], c_ref[...], n_ref[...], i, nt)
        t = i * tm + lax.broadcasted_iota(jnp.int32, (tm, 1), 0)
        for g, w in enumerate(POOL_WINDOWS):
            e = ext[:, g * G:(g + 1) * G]
            win = _shift_rows(_running_sum(e, w), -(w // 2 - 1))[HALO:HALO + tm]
            dmx = (win / _window_count(t, w, L) - e[HALO:HALO + tm]).astype(BF16)
            dm_ref[:, g * G:(g + 1) * G] = dmx
            o_ref[:, g * G:(g + 1) * G] = (_dot(dmx, w_ref[g]) * s_ref[:, g * G:(g + 1) * G]).astype(BF16)

    return pl.pallas_call(
        body, name=name, grid=(nt,),
        in_specs=[_tile(tm, W), _halo_prev(tm, W), _halo_next(tm, W, L), _full((4, G, G)), _full((1, W))],
        out_specs=[_tile(tm, W), _tile(tm, W)],
        out_shape=[jax.ShapeDtypeStruct((L, W), BF16)] * 2, compiler_params=_params())(u, u, u, poolw, pscale)


def _pool_bwd(dpool, dmx, poolw, pscale, name, tm=512):
    L, W = dpool.shape
    G = POOL_GROUP_DIM
    tm = min(tm, L)
    nt = L // tm

    def body(c_ref, p_ref, n_ref, dm_ref, w_ref, s_ref, du_ref, dw_ref, acc_ref):
        i = pl.program_id(0)
        ext = _ext(p_ref[...], c_ref[...], n_ref[...], i, nt)
        te = i * tm - HALO + lax.broadcasted_iota(jnp.int32, (tm + 2 * HALO, 1), 0)

        @pl.when(i == 0)
        def _():
            dw_ref[...] = jnp.zeros((4 * G, G), F32)

        rows = []
        for g, w in enumerate(POOL_WINDOWS):
            sc = s_ref[:, g * G:(g + 1) * G]
            dpre = (ext[:, g * G:(g + 1) * G] * sc).astype(BF16)
            dd = _dot_nt(dpre, w_ref[g])
            spread = _shift_rows(_running_sum(dd / _window_count(te, w, L), w), -(w // 2))
            du_ref[:, g * G:(g + 1) * G] = (spread - dd)[HALO:HALO + tm]
            dmx_g = dm_ref[:, g * G:(g + 1) * G]
            rows.append(_sum0(c_ref[:, g * G:(g + 1) * G] * _dot(dmx_g, w_ref[g])))
            dw_ref[g * G:(g + 1) * G, :] += _dot_tn(dmx_g, dpre[HALO:HALO + tm])
        _acc_rows(acc_ref, i == 0, [jnp.concatenate(rows, axis=1)])

    return pl.pallas_call(
        body, name=name, grid=(nt,),
        in_specs=[_tile(tm, W), _halo_prev(tm, W), _halo_next(tm, W, L), _tile(tm, W), _full((4, G, G)),
                  _full((1, W))],
        out_specs=[_tile(tm, W), _full((4 * G, G)), _full((8, W))],
        out_shape=[jax.ShapeDtypeStruct((L, W), F32), jax.ShapeDtypeStruct((4 * G, G), F32),
                   jax.ShapeDtypeStruct((8, W), F32)],
        compiler_params=_params())(dpool, dpool, dpool, dmx, poolw, pscale)


def _conv3(z, cw):
    return _shift_rows(z, 1) * cw[0] + z * cw[1] + _shift_rows(z, -1) * cw[2]


def _conv_fwd(x, prm, wing, woutg, name, tm=512):
    L, D = x.shape
    Ws = wing.shape[-1]
    tm = min(tm, L)
    nt = L // tm
    te = tm + 2 * HALO

    def body(c_ref, p_ref, n_ref, prm_ref, wi_ref, wo_ref, xo_ref, y_ref, b_ref):
        i = pl.program_id(0)
        xe = jnp.concatenate([p_ref[...], c_ref[...], n_ref[...]], axis=0)
        hn, _, _, _ = _nm(xe, prm_ref[3:4, :], prm_ref[0:1, :], prm_ref[1:2, :])
        hb = hn.astype(BF16)
        proj = jnp.concatenate([_dot(hb, wi_ref[s]) for s in range(4)], axis=1)
        bg, cg, xin = proj[:, :D], proj[:, D:2 * D], proj[:, 2 * D:]
        tpos = i * tm - HALO + lax.broadcasted_iota(jnp.int32, (te, 1), 0)
        valid = ((tpos >= 0) & (tpos < L)).astype(F32)
        yc = _conv3(cg * xin * valid, [prm_ref[4 + k:5 + k, :] for k in range(3)])
        h2 = (bg * yc)[HALO:HALO + tm].astype(BF16)
        yv = _dot(h2, wo_ref[...])
        y_ref[...] = yv.astype(BF16)
        xo_ref[...] = c_ref[...] + prm_ref[2:3, :] * yv
        b_ref[...] = proj[HALO:HALO + tm].astype(BF16)

    return pl.pallas_call(
        body, name=name, grid=(nt,),
        in_specs=[_tile(tm, D), _halo_prev(tm, D), _halo_next(tm, D, L), _full((8, D)),
                  _resident((None, 4, D, Ws), lambda i: (0, 0, 0, 0)),
                  _resident((None, D, D), lambda i: (0, 0, 0))],
        out_specs=[_tile(tm, D), _tile(tm, D), _tile(tm, 3 * D)],
        out_shape=[jax.ShapeDtypeStruct((L, D), F32), jax.ShapeDtypeStruct((L, D), BF16),
                   jax.ShapeDtypeStruct((L, 3 * D), BF16)],
        compiler_params=_params())(x, x, x, prm, wing, woutg)


def _conv_bwd(dout, x, y, bcx, prm, wing, woutg, name, tm=256, ride=None):
    L, D = x.shape
    Ws = wing.shape[-1]
    tm = min(tm, L)
    nt = L // tm
    te = tm + 2 * HALO

    def body(dc_ref, dp_ref, dn_ref, x_ref, y_ref, bc_ref, bp_ref, bn_ref, prm_ref, wi_ref, wo_ref,
             dx_ref, dpr_ref, h2_ref, dy_ref, hn_ref, acc_ref):
        i = pl.program_id(0)
        gain, shift, scale, gate = prm_ref[3:4, :], prm_ref[0:1, :], prm_ref[1:2, :], prm_ref[2:3, :]
        taps = [prm_ref[4 + k:5 + k, :] for k in range(3)]
        do = dc_ref[...]
        doe = _ext(dp_ref[...], do, dn_ref[...], i, nt)
        dye = (gate * doe).astype(BF16)
        dy_ref[...] = dye[HALO:HALO + tm]
        dh2 = _dot_nt(dye, wo_ref[...])
        be = jnp.concatenate([bp_ref[...], bc_ref[...], bn_ref[...]], axis=0).astype(F32)
        bg, cg, xin = be[:, :D], be[:, D:2 * D], be[:, 2 * D:]
        tpos = i * tm - HALO + lax.broadcasted_iota(jnp.int32, (te, 1), 0)
        valid = ((tpos >= 0) & (tpos < L)).astype(F32)
        z = cg * xin * valid
        yc = _conv3(z, taps)
        dyc = dh2 * bg
        h2_ref[...] = (bg * yc)[HALO:HALO + tm].astype(BF16)
        dz = _conv3(dyc, taps[::-1]) * valid
        dproj = jnp.concatenate([dh2 * yc, dz * xin, dz * cg], axis=1)[HALO:HALO + tm].astype(BF16)
        dpr_ref[...] = dproj
        dhn = jnp.zeros((tm, D), F32)
        for s in range(4):
            dhn = dhn + _dot_nt(dproj[:, s * Ws:(s + 1) * Ws], wi_ref[s])
        hn, xhat, r, nrm = _nm(x_ref[...], gain, shift, scale)
        hn_ref[...] = hn.astype(BF16)
        dx, dshift, dscale, dgn = _nm_bwd(dhn, xhat, r, nrm, gain, scale)
        dx_ref[...] = do + dx
        dgate = _sum0(do * y_ref[...].astype(F32))
        dtaps = [_sum0((dyc * _shift_rows(z, 1 - k))[HALO:HALO + tm]) for k in range(3)]
        _acc_rows(acc_ref, i == 0, [dshift, dscale, dgate, dgn] + dtaps)

    res, carried = _ride_call(
        body, ride, (dout, dout, dout, x, y, bcx, bcx, bcx, prm, wing, woutg), name=name, grid=(nt,),
        in_specs=[_tile(tm, D), _halo_prev(tm, D), _halo_next(tm, D, L), _tile(tm, D), _tile(tm, D),
                  _tile(tm, 3 * D), _halo_prev(tm, 3 * D), _halo_next(tm, 3 * D, L), _full((8, D)),
                  _resident((None, 4, D, Ws), lambda i: (0, 0, 0, 0)),
                  _resident((None, D, D), lambda i: (0, 0, 0))],
        out_specs=[_tile(tm, D), _tile(tm, 3 * D), _tile(tm, D), _tile(tm, D), _tile(tm, D), _full((8, D))],
        out_shape=[jax.ShapeDtypeStruct((L, D), F32), jax.ShapeDtypeStruct((L, 3 * D), BF16),
                   jax.ShapeDtypeStruct((L, D), BF16), jax.ShapeDtypeStruct((L, D), BF16),
                   jax.ShapeDtypeStruct((L, D), BF16), jax.ShapeDtypeStruct((8, D), F32)],
        compiler_params=_params())
    return (*res, carried)


def _loss_head(x, tgt, fg, name, tm=512):
    L, D = x.shape
    tm = min(tm, L)

    def body(x_ref, t_ref, g_ref, dx_ref, acc_ref):
        i = pl.program_id(0)
        xv = x_ref[...]
        g = g_ref[...]
        r = lax.rsqrt(jnp.mean(xv * xv, axis=-1, keepdims=True) + RMS_EPS)
        xhat = xv * r
        err = xhat * g - t_ref[...]
        part = 0.5 * jnp.sum(jnp.mean(err * err, axis=-1, keepdims=True), axis=0, keepdims=True)
        dy = err * (1.0 / D)
        dxh = dy * g
        dx_ref[...] = r * (dxh - xhat * jnp.mean(dxh * xhat, axis=-1, keepdims=True))
        _acc_rows(acc_ref, i == 0, [_sum0(dy * xhat), jnp.broadcast_to(part, (1, D))])

    return pl.pallas_call(
        body, name=name, grid=(L // tm,), in_specs=[_tile(tm, D), _tile(tm, D), _full((1, D))],
        out_specs=[_tile(tm, D), _full((8, D))],
        out_shape=[jax.ShapeDtypeStruct((L, D), F32), jax.ShapeDtypeStruct((8, D), F32)],
        compiler_params=_params())(x, tgt, fg)


def _mod_fwd(cond, mod_w, mod_b, name, tn=768):
    nl, D, N = mod_w.shape
    tn = min(tn, N)

    def body(c_ref, w_ref, b_ref, o_ref):
        cv = c_ref[...]
        s = (cv * _sigmoid(cv)).astype(BF16)
        o_ref[...] = _dot(s, w_ref[...].astype(BF16)) + b_ref[...]

    return pl.pallas_call(
        body, name=name, grid=(nl, N // tn),
        in_specs=[pl.BlockSpec((16, D), lambda l, j: (0, 0)), pl.BlockSpec((None, D, tn), lambda l, j: (l, 0, j)),
                  pl.BlockSpec((None, 1, tn), lambda l, j: (l, 0, j))],
        out_specs=pl.BlockSpec((None, 16, tn), lambda l, j: (l, 0, j)),
        out_shape=jax.ShapeDtypeStruct((nl, 16, N), F32), compiler_params=_params())(cond, mod_w, mod_b)


def _mod_bwd(cond, dm, mod_w, name, tn=768):
    nl, D, N = mod_w.shape
    tn = min(tn, N)

    def body(c_ref, d_ref, w_ref, dw_ref, dc_ref):
        first = (pl.program_id(0) == 0) & (pl.program_id(1) == 0)
        cv = c_ref[...]
        s = (cv * _sigmoid(cv)).astype(BF16)
        d = d_ref[...].astype(BF16)
        dw_ref[...] = _dot_tn(s, d)

        @pl.when(first)
        def _():
            dc_ref[...] = jnp.zeros((16, D), F32)

        dc_ref[...] += _dot_nt(d, w_ref[...].astype(BF16))

    return pl.pallas_call(
        body, name=name, grid=(nl, N // tn),
        in_specs=[pl.BlockSpec((16, D), lambda l, j: (0, 0)), pl.BlockSpec((None, 16, tn), lambda l, j: (l, 0, j)),
                  pl.BlockSpec((None, D, tn), lambda l, j: (l, 0, j))],
        out_specs=[pl.BlockSpec((None, D, tn), lambda l, j: (l, 0, j)), pl.BlockSpec((16, D), lambda l, j: (0, 0))],
        out_shape=[jax.ShapeDtypeStruct((nl, D, N), F32), jax.ShapeDtypeStruct((16, D), F32)],
        compiler_params=_params())(cond, dm, mod_w)


def _mod_small_grads(dm_all, cond, dsilu_parts, name):
    nl, _, N = dm_all.shape
    D = cond.shape[1]

    def body(d_ref, c_ref, p_ref, db_ref, dc_ref):
        for l in range(nl):
            db_ref[l] = _sum0(d_ref[l])
        tot = p_ref[0, 8:9, :]
        for k in range(1, N_CHIPS):
            tot = tot + p_ref[2 * k, 8:9, :]
        cv = c_ref[8:9, :]
        sg = _sigmoid(cv)
        dc_ref[...] = tot * (sg * (1.0 + cv * (1.0 - sg)))

    return pl.pallas_call(
        body, name=name, out_shape=[jax.ShapeDtypeStruct((nl, 1, N), F32), jax.ShapeDtypeStruct((1, D), F32)],
    )(dm_all, cond, dsilu_parts)


def _prm(rows, D):
    rows = [r.reshape(1, D) for r in rows]
    return jnp.concatenate(rows + [jnp.zeros((8 - len(rows), D), F32)], axis=0)


def kernel(x, c, ctx, c_ctx, mod_w, mod_b, norm_g, ffn_w13, ffn_w2, even_w_in, even_w_out, na_rpb, pool_w, pool_scale, conv_w_in, conv_w, conv_w_out, final_g, loss_target, m_c_ctx, m_mod_w, m_mod_b, m_norm_g, m_ffn_w13, m_ffn_w2, m_even_w_in, m_even_w_out, m_na_rpb, m_pool_w, m_pool_scale, m_conv_w_in, m_conv_w, m_conv_w_out, m_final_g, v_c_ctx, v_mod_w, v_mod_b, v_norm_g, v_ffn_w13, v_ffn_w2, v_even_w_in, v_even_w_out, v_na_rpb, v_pool_w, v_pool_scale, v_conv_w_in, v_conv_w, v_conv_w_out, v_final_g):
    xi, yi, ci = lax.axis_index("x"), lax.axis_index("y"), lax.axis_index("c")
    chip = 2 * xi + yi
    dev = 4 * xi + 2 * yi + ci
    _, L, D = x.shape
    C = ctx.shape[1]
    Ds = D // N_CHIPS
    Nm = mod_w.shape[-1]
    Fh = ffn_w13.shape[-1]
    Fq = ffn_w2.shape[2]
    assert ffn_w13.shape[:2] == (2, 2) and Fh == 2 * Fq and L % (GQ * GRID_W) == 0 and L // GRID_W >= GK and GQ == NA_KH // 2
    x0, ctx0, tgt = x[0], ctx[0], loss_target[0]

    pad = lambda a: jnp.pad(a, ((0, 0), (0, D - a.shape[1])))
    pack1 = jnp.concatenate([c, pad(norm_g.reshape(6, Ds)), pad(conv_w.reshape(3, Ds)), jnp.zeros((6, D), F32)], axis=0)
    g1 = _small_all_gather(pack1, "ag_cond")
    cond = jnp.concatenate([g1[:, 0], c_ctx[None], jnp.zeros((7, D), F32)], axis=0)
    norm_full = jnp.concatenate([g1[2 * k, 1:7, :Ds] for k in range(N_CHIPS)], axis=1).reshape(2, 3, D)
    convw_full = jnp.concatenate([g1[2 * k, 7:10, :Ds] for k in range(N_CHIPS)], axis=1)

    mod_b_loc = lax.dynamic_slice_in_dim(mod_b, chip * Nm, Nm, axis=1).reshape(2, 1, Nm)
    m_loc = _mod_fwd(cond, mod_w, mod_b_loc, "mod_fwd")
    g2 = _small_all_gather(m_loc.reshape(32, Nm), "ag_mod")
    m_all = jnp.concatenate([g2[2 * k] for k in range(N_CHIPS)], axis=1).reshape(2, 16, N_MOD, D)
    m_lat = lax.dynamic_index_in_dim(m_all, dev, axis=1, keepdims=False)
    m_ctx = m_all[:, 8]

    def prm(mods, layer, base, gain_idx, extra=()):
        return _prm([mods[layer, base], mods[layer, base + 1], mods[layer, base + 2], norm_full[layer, gain_idx],
                     *extra], D)

    def shard_bf16(w, name):
        return _cast_bf16(w.reshape(-1, w.shape[-1]), name).reshape(-1, *w.shape[-2:])

    w13s, w2s = shard_bf16(ffn_w13, "cast_w13"), shard_bf16(ffn_w2, "cast_w2")
    eins, eouts = shard_bf16(even_w_in, "cast_ein"), shard_bf16(even_w_out, "cast_eout")
    cins, couts = shard_bf16(conv_w_in, "cast_cin"), shard_bf16(conv_w_out, "cast_cout")
    ffn_shards = [[w13s[t:t + 1], w2s[t:t + 1]] for t in range(4)]

    def ffn_weights(w13g, w2g):
        return w13g.reshape(1, 4, D, Fh), w2g

    wf = [ffn_weights(*_gather_shards(ffn_shards[0], "ag_ffn0")), None, None, None]
    pos = jnp.stack([chip, ci]).astype(jnp.int32)

    p_f1 = prm(m_lat, 0, 0, 0)
    p_mx = prm(m_lat, 0, 3, 1)
    p_f2 = prm(m_lat, 0, 6, 2)
    p_g1 = prm(m_lat, 1, 0, 0)
    p_cv = prm(m_lat, 1, 3, 1, extra=(convw_full[0], convw_full[1], convw_full[2]))
    p_g2 = prm(m_lat, 1, 6, 2)
    pc_f1 = prm(m_ctx, 0, 0, 0)
    pc_mx = prm(m_ctx, 0, 3, 1)

    x1, ab1, y1, (eing, eoutg) = _ffn_fwd(x0, p_f1, *wf[0], 0, "ffn_fwd_l0a", ride=_broadcast_ride([eins, eouts]))
    eing = eing.reshape(1, 4, D, NA_WIDTH)
    ctx1, abc, yc, _ = _ffn_fwd(ctx0, pc_f1, *wf[0], 0, "ffn_fwd_ctx")
    q, k, v, u, hn_mx = _even_in_fwd(x1, p_mx, eing, "even_in_fwd")
    _, k_c, v_c, _, hn_cx = _even_in_fwd(ctx1, pc_mx, eing, "even_in_ctx")
    bias = _bias_table(na_rpb[0], "bias_table")
    att, gathered = _attn_fwd(q, k, v, k_c, v_c, bias, "attn_fwd", ride=_broadcast_ride(ffn_shards[1]))
    wf[1] = ffn_weights(*gathered)
    pw_b = _cast_bf16(pool_w.reshape(-1, POOL_GROUP_DIM), "cast_poolw").reshape(4, POOL_GROUP_DIM, POOL_GROUP_DIM)
    pool, dmx = _pool_fwd(u, pw_b, pool_scale, "pool_fwd")
    x2, ymx = _even_out_fwd(x1, att, pool, p_mx, eoutg, "even_out_fwd")
    x3, ab2, y2, gathered = _ffn_fwd(x2, p_f2, *wf[1], 0, "ffn_fwd_l0b",
                                     ride=_broadcast_ride(ffn_shards[2] + [cins, couts]))
    wf[2] = ffn_weights(*gathered[:2])
    cing, coutg = gathered[2].reshape(1, 4, D, conv_w_in.shape[-1]), gathered[3]
    x4, ab3, y3, gathered = _ffn_fwd(x3, p_g1, *wf[2], 0, "ffn_fwd_l1a", ride=_broadcast_ride(ffn_shards[3]))
    wf[3] = ffn_weights(*gathered)
    x5, ycv, bcx = _conv_fwd(x4, p_cv, cing, coutg, "conv_fwd")
    x6, ab4, y4, _ = _ffn_fwd(x5, p_g2, *wf[3], 0, "ffn_fwd_l1b")
    dx6, acc_head = _loss_head(x6, tgt, final_g.reshape(1, D), "loss_head")
    loss = lax.psum(acc_head[1, 0], ("x", "y", "c"))

    def ffn_back(dout, xin, ab, yy, p, t, tag, init13=None, init2=None, ride=None):
        sv, gact = ab
        dx, dab, dy, hn, acc, carried = _ffn_bwd(dout, xin, sv, yy, p, *wf[t], 0, f"ffn_bwd_{tag}", ride=ride)
        dw13 = _mm_tn(hn, dab, 4, False, f"dw13_{tag}", init=init13)
        dw2 = _mm_tn(gact, dy, 2, True, f"dw2_{tag}", init=init2)
        return dx, acc, dw13, dw2, carried

    dx5, acc_g2, dw13_3, dw2_3, _ = ffn_back(dx6, x5, ab4, y4, p_g2, 3, "l1b")
    s_a, sb_a = _pair_sums([dw13_3, dw2_3], pos, "l1b")
    dx4, dproj, h2, dycv, hn_cv, acc_cv, got_a = _conv_bwd(dx5, x4, ycv, bcx, p_cv, cing, coutg, "conv_bwd",
                                                           ride=_scatter_ride(sb_a))
    dcin = _mm_tn(hn_cv, dproj, 4, False, "dw_cin")
    dcout = _mm_tn(h2, dycv, 1, False, "dw_cout")
    s_b, sb_b = _pair_sums([dcin, dcout], pos, "conv")
    dx3, acc_g1, dw13_2, dw2_2, got_b = ffn_back(dx4, x3, ab3, y3, p_g1, 2, "l1a", ride=_scatter_ride(sb_b))
    s_c, sb_c = _pair_sums([dw13_2, dw2_2], pos, "l1a")
    dx2, acc_f2, dw13_1, dw2_1, got_c = ffn_back(dx3, x2, ab2, y2, p_f2, 1, "l0b", ride=_scatter_ride(sb_c))

    dymx, datt, dpool, acc_mxo = _even_out_bwd(dx2, ymx, p_mx, eoutg, "even_out_bwd")
    deout = jnp.concatenate([_mm_tn(att, dymx, 1, False, "dw_eout_att"),
                             _mm_tn(pool, dymx, 1, False, "dw_eout_pool")], axis=0)
    s_d, sb_d = _pair_sums([dw13_1, dw2_1, deout], pos, "l0b")
    du, dpoolw, acc_pool = _pool_bwd(dpool, dmx, pw_b, pool_scale, "pool_bwd")
    dq, dk, dv, dkc, dvc, dbias, got_d = _attn_bwd(q, k, v, k_c, v_c, bias, datt, "attn_bwd",
                                                   ride=_scatter_ride(sb_d))
    drpb = _rpb_grad(dbias, "rpb_grad")
    dx1, dstack, acc_mxi = _even_in_bwd(dx2, x1, dq, dk, dv, du, p_mx, eing,
                                        "even_in_bwd")
    zc = jnp.zeros((C, NA_WIDTH), F32)
    dctx1, dstack_c, accc_mx = _even_in_bwd(jnp.zeros((C, D), F32), ctx1, zc, dkc, dvc, zc,
                                            pc_mx, eing, "even_in_bwd_ctx")
    dein_c = _mm_tn(hn_cx, dstack_c, 4, False, "dw_ein_ctx")
    dein = _mm_tn(hn_mx, dstack, 4, False, "dw_ein", init=dein_c)
    s_e, sb_e = _pair_sums([dein], pos, "ein")
    _, accc_f1, dw13_c, dw2_c, _ = ffn_back(dctx1, ctx0, abc, yc, pc_f1, 0, "ctx")
    sv1, gact1 = ab1
    dx0, dab, dy, hn, acc_f1, got_e = _ffn_bwd(dx1, x0, sv1, y1, p_f1, *wf[0], 0, "ffn_bwd_l0a",
                                               ride=_scatter_ride(sb_e))
    dw13_0 = _mm_tn(hn, dab, 4, False, "dw13_l0a", init=dw13_c)
    s_f13, sb_f13 = _pair_sums([dw13_0], pos, "l0a_w13")
    dw2_0, got_f13 = _mm_tn(gact1, dy, 2, True, "dw2_l0a", init=dw2_c, ride=_scatter_ride(sb_f13))
    s_f2, sb_f2 = _pair_sums([dw2_0], pos, "l0a_w2")

    z1 = jnp.zeros((1, D), F32)
    dm_lat = jnp.concatenate([acc_f1[0:3], acc_mxi[0:2], acc_mxo[2:3], acc_f2[0:3],
                              acc_g1[0:3], acc_cv[0:3], acc_g2[0:3]], axis=0)
    dm_ctx = jnp.concatenate([accc_f1[0:3], accc_mx[0:2]] + [z1] * 13, axis=0)
    dnorm = jnp.concatenate([acc_f1[3:4] + accc_f1[3:4], acc_mxi[3:4] + accc_mx[3:4], acc_f2[3:4],
                             acc_g1[3:4], acc_cv[3:4], acc_g2[3:4]], axis=0)
    rpb_flat = jnp.pad(drpb.reshape(-1), (0, 4 * D - drpb.size)).reshape(4, D)
    pack3 = jnp.concatenate([dm_lat, dm_ctx, dnorm, acc_cv[4:7], acc_head[0:1], pad(acc_pool[0:1]), z1,
                             dpoolw.reshape(-1, D), rpb_flat, jnp.zeros((4, D), F32)], axis=0)
    g3 = _small_all_gather(pack3, "ag_small")
    tot = _sum_devices(g3, "sum_small")
    dm_all = jnp.concatenate([g3[:, 0:18].reshape(8, 2, N_MOD * D).transpose(1, 0, 2),
                              tot[18:36].reshape(2, 1, N_MOD * D), jnp.zeros((2, 7, N_MOD * D), F32)], axis=1)
    dm_loc = lax.dynamic_slice_in_dim(dm_all, chip * Nm, Nm, axis=2)
    g_mod_w, dsilu = _mod_bwd(cond, dm_loc, mod_w, "mod_bwd")
    g4 = _small_all_gather(dsilu, "ag_dsilu")
    g_mod_b, g_c_ctx = _mod_small_grads(dm_all, cond, g4, "mod_small")
    g_mod_b = g_mod_b.reshape(2, N_MOD * D)
    g_c_ctx = g_c_ctx.reshape(D)
    g_norm_full = tot[36:42].reshape(2, 3, D)
    g_norm = lax.dynamic_slice_in_dim(g_norm_full, chip * Ds, Ds, axis=2)
    g_conv_w = lax.dynamic_slice_in_dim(tot[42:45], chip * Ds, Ds, axis=1).reshape(1, 3, Ds)
    g_final = tot[45]
    g_pscale = tot[46:47, :pool_scale.shape[1]]
    g_poolw = tot[48:112].reshape(pool_w.shape)
    g_rpb = tot[112:116].reshape(-1)[:na_rpb.size].reshape(na_rpb.shape)

    r13_3, r2_3 = _joins(s_a, got_a, pos, "l1b")
    r_cin, r_cout = _joins(s_b, got_b, pos, "conv")
    r13_2, r2_2 = _joins(s_c, got_c, pos, "l1a")
    r13_1, r2_1, r_eout = _joins(s_d, got_d, pos, "l0b")
    (r_ein,) = _joins(s_e, got_e, pos, "ein")
    adamw_mod_w, got_f2 = _adamw(mod_w, g_mod_w, m_mod_w, v_mod_w, "adamw_mod_w", ride=_scatter_ride(sb_f2))
    (r13_0,) = _joins(s_f13, got_f13, pos, "l0a_w13")
    (r2_0,) = _joins(s_f2, got_f2, pos, "l0a_w2")
    g_w13 = jnp.stack([r13_0, r13_1, r13_2, r13_3]).reshape(ffn_w13.shape)
    g_w2 = jnp.stack([r2_0, r2_1, r2_2, r2_3]).reshape(ffn_w2.shape)
    g_ein, g_eout, g_cin, g_cout = r_ein[None], r_eout[None], r_cin[None], r_cout[None]

    grads = [g_c_ctx, g_mod_w, g_mod_b, g_norm, g_w13, g_w2, g_ein, g_eout, g_rpb, g_poolw, g_pscale, g_cin,
             g_conv_w, g_cout, g_final]
    weights = [c_ctx, mod_w, mod_b, norm_g, ffn_w13, ffn_w2, even_w_in, even_w_out, na_rpb, pool_w, pool_scale,
               conv_w_in, conv_w, conv_w_out, final_g]
    ms = [m_c_ctx, m_mod_w, m_mod_b, m_norm_g, m_ffn_w13, m_ffn_w2, m_even_w_in, m_even_w_out, m_na_rpb, m_pool_w,
          m_pool_scale, m_conv_w_in, m_conv_w, m_conv_w_out, m_final_g]
    vs = [v_c_ctx, v_mod_w, v_mod_b, v_norm_g, v_ffn_w13, v_ffn_w2, v_even_w_in, v_even_w_out, v_na_rpb, v_pool_w,
          v_pool_scale, v_conv_w_in, v_conv_w, v_conv_w_out, v_final_g]
    names = ["c_ctx", "mod_w", "mod_b", "norm_g", "ffn_w13", "ffn_w2", "even_w_in", "even_w_out", "na_rpb", "pool_w",
             "pool_scale", "conv_w_in", "conv_w", "conv_w_out", "final_g"]
    deltas, new_m, new_v = [], [], []
    for n, w, g, m, vv in zip(names, weights, grads, ms, vs):
        g = g.reshape(w.shape)
        if n == "mod_w":
            d, mn, vn = adamw_mod_w
        elif w.ndim == 1:
            d, mn, vn = (t.reshape(w.shape) for t in _adamw(w[None], g[None], m[None], vv[None], f"adamw_{n}"))
        else:
            d, mn, vn = _adamw(w, g, m, vv, f"adamw_{n}")
        deltas.append(d)
        new_m.append(mn)
        new_v.append(vn)
    grads = [g.reshape(w.shape) for g, w in zip(grads, weights)]
    return (loss, dx0[None], *grads, *deltas, *new_m, *new_v)
```

```python
import jax
import jax.numpy as jnp
from jax import lax
from jax.experimental import pallas as pl
from jax.experimental.pallas import tpu as pltpu

F32 = jnp.float32
BF16 = jnp.bfloat16
MESH = pl.DeviceIdType.MESH

GRID_W = 64
NA_HEADS = 8
NA_HEAD_DIM = 64
NA_KH = 8
NA_KW = 16
GQ = 4
GK = GQ + NA_KH
NA_WIDTH = NA_HEADS * NA_HEAD_DIM
POOL_WINDOWS = (2, 4, 8, 16)
POOL_GROUP_DIM = 128
N_MOD = 9
RMS_EPS = 1e-6
NEG_INF = -1e30
ADAM_LR, ADAM_B1, ADAM_B2, ADAM_EPS, ADAM_WD, ADAM_STEP = 0.001, 0.9, 0.999, 1e-08, 0.01, 10

HALO = 16
VMEM_LIMIT = 56 * 1024 * 1024
N_CHIPS = 4
N_DEV = 8


def _dot(a, b):
    return jnp.dot(a, b, preferred_element_type=F32)


def _dot_nt(a, b):
    return lax.dot_general(a, b, (((1,), (1,)), ((), ())), preferred_element_type=F32)


def _dot_tn(a, b):
    return lax.dot_general(a, b, (((0,), (0,)), ((), ())), preferred_element_type=F32)


def _sigmoid(a):
    return 1.0 / (1.0 + jnp.exp(-a))


def _sum0(v):
    return jnp.sum(v, axis=0, keepdims=True)


def _nm(x, g, shift, scale):
    r = lax.rsqrt(jnp.mean(x * x, axis=-1, keepdims=True) + RMS_EPS)
    xhat = x * r
    nrm = xhat * g
    return nrm * (1.0 + scale) + shift, xhat, r, nrm


def _nm_bwd(dhn, xhat, r, nrm, g, scale):
    dshift = _sum0(dhn)
    dscale = _sum0(dhn * nrm)
    dnrm = dhn * (1.0 + scale)
    dgn = _sum0(dnrm * xhat)
    dxh = dnrm * g
    dx = r * (dxh - xhat * jnp.mean(dxh * xhat, axis=-1, keepdims=True))
    return dx, dshift, dscale, dgn


def _acc_rows(acc_ref, first, rows):
    @pl.when(first)
    def _():
        acc_ref[...] = jnp.zeros(acc_ref.shape, acc_ref.dtype)
    for k, row in enumerate(rows):
        if row is not None:
            acc_ref[k:k + 1, :] += row


def _shift_rows(v, k):
    n = v.shape[0]
    k = k % n
    return v if k == 0 else pltpu.roll(v, k, 0)


def _tile(tm, w):
    return pl.BlockSpec((tm, w), lambda i: (i, 0))


def _full(shape):
    nd = len(shape)
    return pl.BlockSpec(shape, lambda i: (0,) * nd)


def _resident(block, imap):
    return pl.BlockSpec(block, imap, pipeline_mode=pl.Buffered(1))


def _halo_prev(tm, w):
    return pl.BlockSpec((HALO, w), lambda i: (jnp.maximum(i * (tm // HALO) - 1, 0), 0))


def _halo_next(tm, w, L):
    return pl.BlockSpec((HALO, w), lambda i: (jnp.minimum((i + 1) * (tm // HALO), L // HALO - 1), 0))


def _params(vmem=VMEM_LIMIT):
    return pltpu.CompilerParams(vmem_limit_bytes=vmem)


def _pick_rows(rows, cols, itemsize=4, target=1 << 20):
    best = None
    for t in range(8, rows + 1, 8):
        if rows % t == 0 and t * cols * itemsize <= target:
            best = t
    return best if best is not None else rows


def _ext(prev, cur, nxt, i, nt):
    prev = jnp.where(i > 0, prev, jnp.zeros_like(prev))
    nxt = jnp.where(i < nt - 1, nxt, jnp.zeros_like(nxt))
    return jnp.concatenate([prev, cur, nxt], axis=0)


def _cast_bf16(a2d, name):
    rows, cols = a2d.shape
    tr = _pick_rows(rows, cols)

    def body(a_ref, o_ref):
        o_ref[...] = a_ref[...].astype(BF16)

    return pl.pallas_call(
        body, name=name, grid=(rows // tr,), in_specs=[_tile(tr, cols)], out_specs=_tile(tr, cols),
        out_shape=jax.ShapeDtypeStruct((rows, cols), BF16))(a2d)


def _sum_devices(g, name):
    n, rows, cols = g.shape
    tr = _pick_rows(rows, cols, target=1 << 18)

    def body(g_ref, o_ref):
        s = g_ref[0]
        for d in range(1, n):
            s = s + g_ref[d]
        o_ref[...] = s

    return pl.pallas_call(
        body, name=name, grid=(rows // tr,), in_specs=[pl.BlockSpec((n, tr, cols), lambda i: (0, i, 0))],
        out_specs=_tile(tr, cols), out_shape=jax.ShapeDtypeStruct((rows, cols), F32))(g)


def _adamw(w, g, m, v, name, ride=None):
    shape = w.shape
    cols = shape[-1]
    rows = w.size // cols
    w2, g2, m2, v2 = (t.reshape(rows, cols) for t in (w, g, m, v))
    tr = _pick_rows(rows, cols)
    c1 = 1.0 - ADAM_B1 ** ADAM_STEP
    c2 = 1.0 - ADAM_B2 ** ADAM_STEP

    def body(w_ref, g_ref, m_ref, v_ref, d_ref, mo_ref, vo_ref):
        gg = g_ref[...]
        mn = ADAM_B1 * m_ref[...] + (1.0 - ADAM_B1) * gg
        vn = ADAM_B2 * v_ref[...] + (1.0 - ADAM_B2) * (gg * gg)
        d_ref[...] = -ADAM_LR * ((mn / c1) / (jnp.sqrt(vn / c2) + ADAM_EPS) + ADAM_WD * w_ref[...])
        mo_ref[...] = mn
        vo_ref[...] = vn

    outs, carried = _ride_call(
        body, ride, (w2, g2, m2, v2), name=name, grid=(rows // tr,), in_specs=[_tile(tr, cols)] * 4,
        out_specs=[_tile(tr, cols)] * 3, out_shape=[jax.ShapeDtypeStruct((rows, cols), F32)] * 3)
    outs = tuple(o.reshape(shape) for o in outs)
    return outs if ride is None else (outs, carried)


def _mesh_pos():
    x, y, c = lax.axis_index("x"), lax.axis_index("y"), lax.axis_index("c")
    chips = [(1 - x, y), (x, 1 - y), (1 - x, 1 - y)]
    return x, y, c, chips


def _hbm_specs(n):
    return [pl.BlockSpec(memory_space=pltpu.HBM)] * n


def _small_all_gather(v, name):
    rows, w = v.shape

    def body(x_ref, out_ref, send_sems, recv_sems, local_sem):
        x, y, c, chips = _mesh_pos()
        me, sibling = (x, y, c), (x, y, 1 - c)

        def blk(px, py, pc):
            return out_ref.at[4 * px + 2 * py + pc]

        def copy(k, block, to, src=None):
            return pltpu.make_async_remote_copy(
                src_ref=blk(*block) if src is None else src, dst_ref=blk(*block),
                send_sem=send_sems.at[k], recv_sem=recv_sems.at[k], device_id=to, device_id_type=MESH)

        mine = pltpu.make_async_copy(x_ref, blk(*me), local_sem)
        mine.start()
        first = [copy(0, me, sibling, src=x_ref)]
        first += [copy(1 + j, me, (*chip, c), src=x_ref) for j, chip in enumerate(chips)]
        for cp in first:
            cp.start()
        passed = [copy(4 + j, (*chip, c), sibling) for j, chip in enumerate(chips)]
        for j, chip in enumerate(chips):
            copy(1 + j, (*chip, c), me).wait_recv()
            passed[j].start()
        copy(0, sibling, me).wait_recv()
        for j, chip in enumerate(chips):
            copy(4 + j, (*chip, 1 - c), me).wait_recv()
        for cp in first + passed:
            cp.wait_send()
        mine.wait()

    return pl.pallas_call(
        body, name=name, out_shape=jax.ShapeDtypeStruct((N_DEV, rows, w), v.dtype),
        in_specs=[pl.BlockSpec(memory_space=pltpu.VMEM)], out_specs=pl.BlockSpec(memory_space=pltpu.VMEM),
        scratch_shapes=[pltpu.SemaphoreType.DMA((7,)), pltpu.SemaphoreType.DMA((7,)), pltpu.SemaphoreType.DMA],
    )(v)


def _gather_shards(shards, name):
    n = len(shards)

    def body(*refs):
        ins, outs = refs[:n], refs[n:2 * n]
        send_sems, recv_sems, local_sems = refs[2 * n:]
        x, y, c, chips = _mesh_pos()
        k = 2 * x + y
        sibling = (x, y, 1 - c)

        def window(t, chip_k, half):
            r = ins[t].shape[1]
            return outs[t].at[:, pl.ds(chip_k * r + half * (r // 2), r // 2), :]

        def copy(t, j, chip_k, half, to, src=None):
            return pltpu.make_async_remote_copy(
                src_ref=window(t, chip_k, half) if src is None else src, dst_ref=window(t, chip_k, half),
                send_sem=send_sems.at[6 * t + j], recv_sem=recv_sems.at[6 * t + j], device_id=to, device_id_type=MESH)

        started, local = [], []
        for t in range(n):
            r = ins[t].shape[1]
            lc = pltpu.make_async_copy(ins[t], outs[t].at[:, pl.ds(k * r, r), :], local_sems.at[t])
            lc.start()
            local.append(lc)
            src = ins[t].at[:, pl.ds(c * (r // 2), r // 2), :]
            for j, chip in enumerate(chips):
                cp = copy(t, j, k, c, (*chip, c), src=src)
                cp.start()
                started.append(cp)
        for t in range(n):
            for j, chip in enumerate(chips):
                kj = 2 * chip[0] + chip[1]
                copy(t, j, kj, c, sibling).wait_recv()
                cp = copy(t, 3 + j, kj, c, sibling)
                cp.start()
                started.append(cp)
        for t in range(n):
            for j, chip in enumerate(chips):
                kj = 2 * chip[0] + chip[1]
                copy(t, 3 + j, kj, 1 - c, sibling).wait_recv()
        for cp in started:
            cp.wait_send()
        for lc in local:
            lc.wait()

    out_shape = [jax.ShapeDtypeStruct((s.shape[0], N_CHIPS * s.shape[1], s.shape[2]), s.dtype) for s in shards]
    return pl.pallas_call(
        body, name=name, out_shape=out_shape, in_specs=_hbm_specs(n), out_specs=_hbm_specs(n),
        scratch_shapes=[pltpu.SemaphoreType.DMA((6 * n,)), pltpu.SemaphoreType.DMA((6 * n,)),
                        pltpu.SemaphoreType.DMA((n,))],
    )(*shards)


def _chunk_rows(h, w):
    best = 16
    for t in range(16, h + 1, 16):
        if h % t == 0 and t * w * 4 <= (2 << 20):
            best = t
    return best


def _pair_sum(part, pos, name):
    _, h, w = part.shape
    cr = _chunk_rows(h, w)
    nc = h // cr
    n = 4 * nc
    slots = 4

    def body(pos_ref, own_ref, send_ref, s_ref, sb_ref, rbuf, send_sems, recv_sems):
        x, y, c, _ = _mesh_pos()
        k = pl.program_id(0)

        def copy(slot):
            return pltpu.make_async_remote_copy(
                src_ref=send_ref, dst_ref=rbuf.at[slot], send_sem=send_sems.at[slot], recv_sem=recv_sems.at[slot],
                device_id=(x, y, 1 - c), device_id_type=MESH)

        @pl.when(k < n)
        def _():
            copy(k % slots).start()

        @pl.when(k > 0)
        def _():
            before = (k + slots - 1) % slots
            copy(before).wait_recv()
            s = own_ref[...] + rbuf[before]
            s_ref[...] = s
            sb_ref[...] = s.astype(BF16)

        @pl.when(k < n)
        def _():
            copy(k % slots).wait_send()

    def own(k, p):
        j = jnp.maximum(k - 1, 0)
        return ((2 * (j // nc) + p[1]) * nc + j % nc, 0)

    def send(k, p):
        j = jnp.minimum(k, n - 1)
        return ((2 * (j // nc) + 1 - p[1]) * nc + j % nc, 0)

    grid_spec = pltpu.PrefetchScalarGridSpec(
        num_scalar_prefetch=1, grid=(n + 1,),
        in_specs=[pl.BlockSpec((cr, w), own), pl.BlockSpec((cr, w), send)],
        out_specs=[pl.BlockSpec((cr, w), lambda k, p: (jnp.maximum(k - 1, 0), 0))] * 2,
        scratch_shapes=[pltpu.VMEM((slots, cr, w), F32), pltpu.SemaphoreType.DMA((slots,)),
                        pltpu.SemaphoreType.DMA((slots,))])
    part2 = part.reshape(8 * h, w)
    s, sb = pl.pallas_call(
        body, name=name, grid_spec=grid_spec, compiler_params=_params(),
        out_shape=[jax.ShapeDtypeStruct((4 * h, w), F32), jax.ShapeDtypeStruct((4 * h, w), BF16)],
    )(pos, part2, part2)
    return s.reshape(4, h, w), sb.reshape(4, h, w)


class _Ride:
    def __init__(self, ins, out_shape, sems, copies):
        self.ins, self.out_shape, self.sems, self.copies = list(ins), list(out_shape), list(sems), copies

    def start(self, ins, outs, sems):
        sends, _, _, local = self.copies(ins, outs, sems)
        for cp in local + sends:
            cp.start()

    def finish(self, ins, outs, sems):
        _, recvs, sends, local = self.copies(ins, outs, sems)
        for cp in recvs:
            cp.wait_recv()
        for cp in sends:
            cp.wait_send()
        for cp in local:
            cp.wait()


def _scatter_ride(sums_bf16):
    n = len(sums_bf16)

    def copies(ins, outs, sems):
        send_sems, recv_sems = sems
        x, y, c, chips = _mesh_pos()
        cps = [pltpu.make_async_remote_copy(
            src_ref=ins[t].at[2 * chip[0] + chip[1]], dst_ref=outs[t].at[j],
            send_sem=send_sems.at[3 * t + j], recv_sem=recv_sems.at[3 * t + j],
            device_id=(*chip, c), device_id_type=MESH) for t in range(n) for j, chip in enumerate(chips)]
        return cps, cps, cps, []

    return _Ride(sums_bf16, [jax.ShapeDtypeStruct((3,) + s.shape[1:], BF16) for s in sums_bf16],
                 [pltpu.SemaphoreType.DMA((3 * n,)), pltpu.SemaphoreType.DMA((3 * n,))], copies)


def _broadcast_ride(shards):
    n = len(shards)

    def copies(ins, outs, sems):
        send_sems, recv_sems, local_sems = sems
        x, y, c, chips = _mesh_pos()
        k = 2 * x + y
        sends, recvs, local = [], [], []
        for t in range(n):
            r = ins[t].shape[1]
            h = r // 2
            local.append(pltpu.make_async_copy(ins[t], outs[t].at[:, pl.ds(k * r, r), :], local_sems.at[t]))
            src = ins[t].at[:, pl.ds(c * h, h), :]
            mine = outs[t].at[:, pl.ds(k * r + c * h, h), :]
            for j, chip in enumerate(chips):
                kj = 2 * chip[0] + chip[1]
                for d in range(2):
                    sends.append(pltpu.make_async_remote_copy(
                        src_ref=src, dst_ref=mine, send_sem=send_sems.at[6 * t + 2 * j + d],
                        recv_sem=recv_sems.at[6 * t + 2 * j + c], device_id=(*chip, d), device_id_type=MESH))
                    theirs = outs[t].at[:, pl.ds(kj * r + d * h, h), :]
                    recvs.append(pltpu.make_async_remote_copy(
                        src_ref=theirs, dst_ref=theirs, send_sem=send_sems.at[6 * t + 2 * j + d],
                        recv_sem=recv_sems.at[6 * t + 2 * j + d], device_id=(*chip, d), device_id_type=MESH))
        return sends, recvs, sends, local

    return _Ride(shards, [jax.ShapeDtypeStruct((s.shape[0], N_CHIPS * s.shape[1], s.shape[2]), s.dtype) for s in shards],
                 [pltpu.SemaphoreType.DMA((6 * n,)), pltpu.SemaphoreType.DMA((6 * n,)), pltpu.SemaphoreType.DMA((n,))],
                 copies)


def _ride_call(body, ride, args, *, name, grid, in_specs, out_specs, out_shape, compiler_params=None):
    in_specs, out_specs, out_shape = list(in_specs), list(out_specs), list(out_shape)
    if ride is None:
        res = pl.pallas_call(body, name=name, grid=grid, in_specs=in_specs, out_specs=out_specs, out_shape=out_shape,
                             compiler_params=compiler_params)(*args)
        return list(res), []
    ni, no, ri, ro = len(in_specs), len(out_specs), len(ride.ins), len(ride.out_shape)

    def at_step(pick):
        hit = None
        for d, n in enumerate(grid):
            here = pl.program_id(d) == pick(n)
            hit = here if hit is None else hit & here
        return hit

    def carried(*refs):
        ins, rins = refs[:ni], refs[ni:ni + ri]
        outs, routs = refs[ni + ri:ni + ri + no], refs[ni + ri + no:ni + ri + no + ro]
        sems = refs[ni + ri + no + ro:]

        @pl.when(at_step(lambda n: 0))
        def _():
            ride.start(rins, routs, sems)

        body(*ins, *outs)

        @pl.when(at_step(lambda n: n - 1))
        def _():
            ride.finish(rins, routs, sems)

    res = pl.pallas_call(
        carried, name=name, grid=grid, in_specs=in_specs + _hbm_specs(ri), out_specs=out_specs + _hbm_specs(ro),
        out_shape=out_shape + ride.out_shape, scratch_shapes=ride.sems, compiler_params=compiler_params,
    )(*args, *ride.ins)
    return list(res[:no]), list(res[no:])


def _sum_and_join(sums, got, pos, name):
    _, h, w = sums.shape
    cr = _chunk_rows(h, w)

    def body(pos_ref, mine_ref, got_ref, o_ref, ebuf, rbuf, send_sems, recv_sems):
        x, y, c, _ = _mesh_pos()
        slot = pl.program_id(0) % 2
        e = mine_ref[...]
        for j in range(3):
            e = e + got_ref[j].astype(F32)
        ebuf[slot] = e
        cp = pltpu.make_async_remote_copy(
            src_ref=ebuf.at[slot], dst_ref=rbuf.at[slot], send_sem=send_sems.at[slot], recv_sem=recv_sems.at[slot],
            device_id=(x, y, 1 - c), device_id_type=MESH)
        cp.start()
        o_ref[pos_ref[1]] = e
        cp.wait_recv()
        o_ref[1 - pos_ref[1]] = rbuf[slot]
        cp.wait_send()

    grid_spec = pltpu.PrefetchScalarGridSpec(
        num_scalar_prefetch=1, grid=(h // cr,),
        in_specs=[pl.BlockSpec((None, cr, w), lambda i, p: (p[0], i, 0)),
                  pl.BlockSpec((3, cr, w), lambda i, p: (0, i, 0))],
        out_specs=pl.BlockSpec((2, cr, w), lambda i, p: (0, i, 0)),
        scratch_shapes=[pltpu.VMEM((2, cr, w), F32), pltpu.VMEM((2, cr, w), F32),
                        pltpu.SemaphoreType.DMA((2,)), pltpu.SemaphoreType.DMA((2,))])
    return pl.pallas_call(
        body, name=name, grid_spec=grid_spec, compiler_params=_params(),
        out_shape=jax.ShapeDtypeStruct((2, h, w), F32),
    )(pos, sums, got)


def _pair_sums(parts, pos, tag):
    pairs = [_pair_sum(p.reshape(8, p.shape[0] // 8, p.shape[1]), pos, f"rs_pair_{tag}_{t}")
             for t, p in enumerate(parts)]
    return [s for s, _ in pairs], [sb for _, sb in pairs]


def _joins(sums, got, pos, tag):
    out = []
    for t, (s, r) in enumerate(zip(sums, got)):
        full = _sum_and_join(s, r, pos, f"rs_join_{tag}_{t}")
        out.append(full.reshape(2 * full.shape[1], full.shape[2]))
    return out


def _ffn_fwd(x, prm, w13g, w2g, t, name, tm=512, ride=None):
    L, D = x.shape
    Fh = w13g.shape[-1]
    tm = min(tm, L)

    def body(x_ref, p_ref, w13_ref, w2_ref, xo_ref, sv_ref, g_ref, y_ref):
        xv = x_ref[...]
        hn, _, _, _ = _nm(xv, p_ref[3:4, :], p_ref[0:1, :], p_ref[1:2, :])
        hb = hn.astype(BF16)
        acc = jnp.zeros((tm, D), F32)
        for j in range(2):
            a = _dot(hb, w13_ref[j])
            b = _dot(hb, w13_ref[2 + j])
            sg = _sigmoid(a)
            sa = a * sg
            sv_ref[:, j * Fh:(j + 1) * Fh] = sa.astype(BF16)
            sv_ref[:, (2 + j) * Fh:(3 + j) * Fh] = (b * (sg * (1.0 + a * (1.0 - sg)))).astype(BF16)
            g = (sa * b).astype(BF16)
            g_ref[:, j * Fh:(j + 1) * Fh] = g
            acc = acc + _dot(g, w2_ref[j * Fh:(j + 1) * Fh, :])
        y_ref[...] = acc.astype(BF16)
        xo_ref[...] = xv + (0.5 * p_ref[2:3, :]) * acc

    res, carried = _ride_call(
        body, ride, (x, prm, w13g, w2g), name=name, grid=(L // tm,),
        in_specs=[_tile(tm, D), _full((8, D)),
                  _resident((None, 4, D, Fh), lambda i: (t, 0, 0, 0)),
                  _resident((None, 2 * Fh, D), lambda i: (t, 0, 0))],
        out_specs=[_tile(tm, D), _tile(tm, 4 * Fh), _tile(tm, 2 * Fh), _tile(tm, D)],
        out_shape=[jax.ShapeDtypeStruct((L, D), F32), jax.ShapeDtypeStruct((L, 4 * Fh), BF16),
                   jax.ShapeDtypeStruct((L, 2 * Fh), BF16), jax.ShapeDtypeStruct((L, D), BF16)],
        compiler_params=_params())
    xo, sv, g, y = res
    return xo, (sv, g), y, carried


def _ffn_bwd(dout, x, sv, y, prm, w13g, w2g, t, name, tm=256, ride=None):
    L, D = x.shape
    Fh = w13g.shape[-1]
    tm = min(tm, L)

    def body(do_ref, x_ref, sv_ref, y_ref, p_ref, w13_ref, w2_ref, dx_ref, dab_ref, dy_ref, hn_ref, acc_ref):
        i = pl.program_id(0)
        do = do_ref[...]
        gain, shift, scale, gate = p_ref[3:4, :], p_ref[0:1, :], p_ref[1:2, :], p_ref[2:3, :]
        hn, xhat, r, nrm = _nm(x_ref[...], gain, shift, scale)
        hn_ref[...] = hn.astype(BF16)
        dgate = 0.5 * _sum0(do * y_ref[...].astype(F32))
        dyb = ((0.5 * gate) * do).astype(BF16)
        dy_ref[...] = dyb
        dhn = jnp.zeros((tm, D), F32)
        for j in range(2):
            dg = _dot_nt(dyb, w2_ref[j * Fh:(j + 1) * Fh, :])
            da = (dg * sv_ref[:, (2 + j) * Fh:(3 + j) * Fh].astype(F32)).astype(BF16)
            db = (dg * sv_ref[:, j * Fh:(j + 1) * Fh].astype(F32)).astype(BF16)
            dab_ref[:, j * Fh:(j + 1) * Fh] = da
            dab_ref[:, (2 + j) * Fh:(3 + j) * Fh] = db
            dhn = dhn + _dot_nt(da, w13_ref[j]) + _dot_nt(db, w13_ref[2 + j])
        dx, dshift, dscale, dgn = _nm_bwd(dhn, xhat, r, nrm, gain, scale)
        dx_ref[...] = do + dx
        _acc_rows(acc_ref, i == 0, [dshift, dscale, dgate, dgn])

    res, carried = _ride_call(
        body, ride, (dout, x, sv, y, prm, w13g, w2g), name=name, grid=(L // tm,),
        in_specs=[_tile(tm, D), _tile(tm, D), _tile(tm, 4 * Fh), _tile(tm, D), _full((8, D)),
                  _resident((None, 4, D, Fh), lambda i: (t, 0, 0, 0)),
                  _resident((None, 2 * Fh, D), lambda i: (t, 0, 0))],
        out_specs=[_tile(tm, D), _tile(tm, 4 * Fh), _tile(tm, D), _tile(tm, D), _full((8, D))],
        out_shape=[jax.ShapeDtypeStruct((L, D), F32), jax.ShapeDtypeStruct((L, 4 * Fh), BF16),
                   jax.ShapeDtypeStruct((L, D), BF16), jax.ShapeDtypeStruct((L, D), BF16),
                   jax.ShapeDtypeStruct((8, D), F32)],
        compiler_params=_params())
    return (*res, carried)


def _mm_tn(a, b, slabs, a_slabbed, name, init=None, tl=1024, ride=None):
    L = a.shape[0]
    ka = a.shape[1] // slabs if a_slabbed else a.shape[1]
    nb = b.shape[1] if a_slabbed else b.shape[1] // slabs
    tl = min(tl, L)
    has_init = init is not None

    def body(a_ref, b_ref, *rest):
        o_ref = rest[-1]
        step = pl.program_id(1)

        @pl.when(step == 0)
        def _():
            o_ref[...] = rest[0][...] if has_init else jnp.zeros((ka, nb), F32)

        o_ref[...] += _dot_tn(a_ref[...], b_ref[...])

    in_specs = [pl.BlockSpec((tl, ka), (lambda s, l: (l, s)) if a_slabbed else (lambda s, l: (l, 0))),
                pl.BlockSpec((tl, nb), (lambda s, l: (l, 0)) if a_slabbed else (lambda s, l: (l, s)))]
    args = [a, b]
    if has_init:
        in_specs.append(pl.BlockSpec((ka, nb), lambda s, l: (s, 0)))
        args.append(init)
    res, carried = _ride_call(
        body, ride, args, name=name, grid=(slabs, L // tl), in_specs=in_specs,
        out_specs=[pl.BlockSpec((ka, nb), lambda s, l: (s, 0))],
        out_shape=[jax.ShapeDtypeStruct((slabs * ka, nb), F32)], compiler_params=_params())
    return res[0] if ride is None else (res[0], carried)


def _even_in_fwd(x, prm, wing, name, tm=512):
    L, D = x.shape
    W = wing.shape[-1]
    tm = min(tm, L)

    def body(x_ref, p_ref, w_ref, q_ref, k_ref, v_ref, u_ref, hn_ref):
        hn, _, _, _ = _nm(x_ref[...], p_ref[3:4, :], p_ref[0:1, :], p_ref[1:2, :])
        hb = hn.astype(BF16)
        hn_ref[...] = hb
        q_ref[...] = _dot(hb, w_ref[0]).astype(BF16)
        k_ref[...] = _dot(hb, w_ref[1]).astype(BF16)
        v_ref[...] = _dot(hb, w_ref[2]).astype(BF16)
        u_ref[...] = _dot(hb, w_ref[3])

    return pl.pallas_call(
        body, name=name, grid=(L // tm,),
        in_specs=[_tile(tm, D), _full((8, D)), _resident((None, 4, D, W), lambda i: (0, 0, 0, 0))],
        out_specs=[_tile(tm, W)] * 4 + [_tile(tm, D)],
        out_shape=[jax.ShapeDtypeStruct((L, W), BF16)] * 3 + [jax.ShapeDtypeStruct((L, W), F32),
                                                              jax.ShapeDtypeStruct((L, D), BF16)],
        compiler_params=_params())(x, prm, wing)


def _even_in_bwd(dout, x, dq, dk, dv, du, prm, wing, name, tm=512):
    L, D = x.shape
    W = wing.shape[-1]
    tm = min(tm, L)

    def body(do_ref, x_ref, dq_ref, dk_ref, dv_ref, du_ref, p_ref, w_ref, dx_ref, ds_ref, acc_ref):
        i = pl.program_id(0)
        gain, shift, scale = p_ref[3:4, :], p_ref[0:1, :], p_ref[1:2, :]
        _, xhat, r, nrm = _nm(x_ref[...], gain, shift, scale)
        dhn = jnp.zeros((tm, D), F32)
        for s, ref in enumerate((dq_ref, dk_ref, dv_ref, du_ref)):
            d = ref[...].astype(BF16)
            ds_ref[:, s * W:(s + 1) * W] = d
            dhn = dhn + _dot_nt(d, w_ref[s])
        dx, dshift, dscale, dgn = _nm_bwd(dhn, xhat, r, nrm, gain, scale)
        dx_ref[...] = do_ref[...] + dx
        _acc_rows(acc_ref, i == 0, [dshift, dscale, None, dgn])

    return pl.pallas_call(
        body, name=name, grid=(L // tm,),
        in_specs=[_tile(tm, D), _tile(tm, D)] + [_tile(tm, W)] * 4 +
                 [_full((8, D)), _resident((None, 4, D, W), lambda i: (0, 0, 0, 0))],
        out_specs=[_tile(tm, D), _tile(tm, 4 * W), _full((8, D))],
        out_shape=[jax.ShapeDtypeStruct((L, D), F32), jax.ShapeDtypeStruct((L, 4 * W), BF16),
                   jax.ShapeDtypeStruct((8, D), F32)],
        compiler_params=_params())(dout, x, dq, dk, dv, du, prm, wing)


def _even_out_fwd(x, att, pool, prm, woutg, name, tm=512):
    L, D = x.shape
    W = D // 2
    tm = min(tm, L)

    def body(x_ref, a_ref, p_ref, prm_ref, w_ref, xo_ref, y_ref):
        yv = _dot(a_ref[...], w_ref[0:W, :]) + _dot(p_ref[...], w_ref[W:2 * W, :])
        y_ref[...] = yv.astype(BF16)
        xo_ref[...] = x_ref[...] + prm_ref[2:3, :] * yv

    return pl.pallas_call(
        body, name=name, grid=(L // tm,),
        in_specs=[_tile(tm, D), _tile(tm, W), _tile(tm, W), _full((8, D)),
                  _resident((None, D, D), lambda i: (0, 0, 0))],
        out_specs=[_tile(tm, D), _tile(tm, D)],
        out_shape=[jax.ShapeDtypeStruct((L, D), F32), jax.ShapeDtypeStruct((L, D), BF16)],
        compiler_params=_params())(x, att, pool, prm, woutg)


def _even_out_bwd(dout, y, prm, woutg, name, tm=512):
    L, D = dout.shape
    W = D // 2
    tm = min(tm, L)

    def body(do_ref, y_ref, p_ref, w_ref, dy_ref, da_ref, dp_ref, acc_ref):
        i = pl.program_id(0)
        do = do_ref[...]
        dgate = _sum0(do * y_ref[...].astype(F32))
        dyb = (p_ref[2:3, :] * do).astype(BF16)
        dy_ref[...] = dyb
        da_ref[...] = _dot_nt(dyb, w_ref[0:W, :]).astype(BF16)
        dp_ref[...] = _dot_nt(dyb, w_ref[W:2 * W, :])
        _acc_rows(acc_ref, i == 0, [None, None, dgate])

    return pl.pallas_call(
        body, name=name, grid=(L // tm,),
        in_specs=[_tile(tm, D), _tile(tm, D), _full((8, D)), _resident((None, D, D), lambda i: (0, 0, 0))],
        out_specs=[_tile(tm, D), _tile(tm, W), _tile(tm, W), _full((8, D))],
        out_shape=[jax.ShapeDtypeStruct((L, D), BF16), jax.ShapeDtypeStruct((L, W), BF16),
                   jax.ShapeDtypeStruct((L, W), F32), jax.ShapeDtypeStruct((8, D), F32)],
        compiler_params=_params())(dout, y, prm, woutg)


def _group_ri(variant, qr, kr):
    first_key = (0, qr, GK - NA_KH)[variant]
    if not first_key <= kr < first_key + NA_KH:
        return None
    return kr - qr + (NA_KH - 1, NA_KH - 1 - NA_KH // 2, NA_KH - 1 - (GK - GQ))[variant]


HEADS_PER_BLOCK = 128 // NA_HEAD_DIM


def _bias_table(rpb, name):
    H = rpb.shape[0]
    hpb = HEADS_PER_BLOCK
    nri, nci = 2 * NA_KH - 1, 2 * NA_KW - 1
    col = jnp.arange(GRID_W)
    rel = (col[None, :] - col[:, None] + (NA_KW - 1)).reshape(1, -1)
    onehot = (rel == jnp.arange(32)[:, None]).astype(F32)
    cs = jnp.clip(col - NA_KW // 2, 0, GRID_W - NA_KW)
    ok = ((col[None, :] >= cs[:, None]) & (col[None, :] < cs[:, None] + NA_KW)).astype(F32).reshape(1, -1)
    rpb2 = jnp.pad(rpb.reshape(H * nri, nci), ((0, 0), (0, 32 - nci)))

    def body(r_ref, e_ref, m_ref, o_ref):
        t = jnp.dot(r_ref[...], e_ref[...], preferred_element_type=F32, precision=lax.Precision.HIGHEST)
        o_ref[...] = jnp.where(m_ref[...] > 0.0, t, NEG_INF)

    tab = pl.pallas_call(body, name=name, out_shape=jax.ShapeDtypeStruct((H * nri, GRID_W * GRID_W), F32))(
        rpb2, onehot, ok)
    tab = tab.reshape(H, nri, GRID_W, GRID_W)
    outside = jnp.full((H // hpb, GRID_W, GRID_W), NEG_INF, F32)
    variants = []
    for variant in range(3):
        rows = []
        for h in range(hpb):
            for qr in range(GQ):
                ris = [_group_ri(variant, qr, kr) for kr in range(GK)]
                rows.append(jnp.concatenate([outside if ri is None else tab[h::hpb, ri] for ri in ris], axis=2))
        variants.append(jnp.concatenate(rows, axis=1))
    return jnp.stack(variants, axis=1)


def _attn_probs(q, kw, kc, bias, scale):
    s_w = _dot_nt(q, kw) * scale + bias
    s_c = _dot_nt(q, kc) * scale
    m = jnp.maximum(jnp.max(s_w, axis=-1, keepdims=True), jnp.max(s_c, axis=-1, keepdims=True))
    e_w = jnp.exp(s_w - m)
    e_c = jnp.exp(s_c - m)
    inv = 1.0 / (jnp.sum(e_w, axis=-1, keepdims=True) + jnp.sum(e_c, axis=-1, keepdims=True))
    return e_w * inv, e_c * inv


def _group_place(g, R):
    G = R // GQ
    kb = jnp.clip(g * GQ - NA_KH // 2, 0, R - GK)
    variant = jnp.where(g == 0, 0, jnp.where(g == G - 1, 2, 1))
    return pl.multiple_of(g * (GQ * GRID_W), GQ * GRID_W), pl.multiple_of(kb * GRID_W, GRID_W), variant


def _lane_masks(width, dh):
    lane = lax.broadcasted_iota(jnp.int32, (1, width), 1)
    return [(lane >= h * dh) & (lane < (h + 1) * dh) for h in range(width // dh)]


def _only(mask, a):
    return jnp.where(mask, a, jnp.zeros_like(a))


def _attn_fwd(q, k, v, kc, vc, bias, name, ride=None):
    L, width = q.shape
    C = kc.shape[0]
    dh = NA_HEAD_DIM
    lanes = 128
    hpb = lanes // dh
    R = L // GRID_W
    nq, nk = GQ * GRID_W, GK * GRID_W
    scale = dh ** -0.5

    def body(q_ref, k_ref, v_ref, kc_ref, vc_ref, b_ref, o_ref):
        masks = _lane_masks(lanes, dh)
        kc2 = kc_ref[...]
        vcs = [_only(m, vc_ref[...]) for m in masks]

        def group(g, carry):
            q0, k0, variant = _group_place(g, R)
            q2 = q_ref[pl.ds(q0, nq), :]
            k2 = k_ref[pl.ds(k0, nk), :]
            v2 = v_ref[pl.ds(k0, nk), :]
            qs = jnp.concatenate([_only(m, q2) for m in masks], axis=0)
            p_w, p_c = _attn_probs(qs, k2, kc2, b_ref[variant], scale)
            p_w, p_c = p_w.astype(BF16), p_c.astype(BF16)
            o2 = jnp.zeros((nq, lanes), F32)
            for h, m in enumerate(masks):
                rows = slice(h * nq, (h + 1) * nq)
                o2 = o2 + _dot(p_w[rows], _only(m, v2)) + _dot(p_c[rows], vcs[h])
            o_ref[pl.ds(q0, nq), :] = o2.astype(BF16)
            return carry

        lax.fori_loop(0, R // GQ, group, 0)

    cols = lambda n: pl.BlockSpec((n, lanes), lambda p: (0, p))
    res, carried = _ride_call(
        body, ride, (q, k, v, kc, vc, bias), name=name, grid=(width // lanes,),
        in_specs=[cols(L), cols(L), cols(L), cols(C), cols(C),
                  pl.BlockSpec((None, 3, hpb * nq, nk), lambda p: (p, 0, 0, 0))],
        out_specs=[cols(L)], out_shape=[jax.ShapeDtypeStruct((L, width), BF16)],
        compiler_params=_params())
    return res[0], carried


def _attn_bwd(q, k, v, kc, vc, bias, do, name, ride=None):
    L, width = q.shape
    C = kc.shape[0]
    dh = NA_HEAD_DIM
    lanes = 128
    hpb = lanes // dh
    R = L // GRID_W
    nq, nk = GQ * GRID_W, GK * GRID_W
    scale = dh ** -0.5

    def body(q_ref, k_ref, v_ref, kc_ref, vc_ref, b_ref, do_ref, dq_ref, dk_ref, dv_ref, dkc_ref, dvc_ref, db_ref):
        masks = _lane_masks(lanes, dh)
        kc2 = kc_ref[...]
        vc2 = vc_ref[...]
        kcs = [_only(m, kc2) for m in masks]
        dk_ref[...] = jnp.zeros((L, lanes), F32)
        dv_ref[...] = jnp.zeros((L, lanes), F32)
        dkc_ref[...] = jnp.zeros((C, lanes), F32)
        dvc_ref[...] = jnp.zeros((C, lanes), F32)
        db_ref[...] = jnp.zeros((3, hpb * nq, nk), F32)

        def group(g, carry):
            q0, k0, variant = _group_place(g, R)
            q2 = q_ref[pl.ds(q0, nq), :]
            k2 = k_ref[pl.ds(k0, nk), :]
            v2 = v_ref[pl.ds(k0, nk), :]
            do2 = do_ref[pl.ds(q0, nq), :]
            qs = jnp.concatenate([_only(m, q2) for m in masks], axis=0)
            dos = jnp.concatenate([_only(m, do2) for m in masks], axis=0)
            p_w, p_c = _attn_probs(qs, k2, kc2, b_ref[variant], scale)
            dp_w = _dot_nt(dos, v2)
            dp_c = _dot_nt(dos, vc2)
            delta = jnp.sum(p_w * dp_w, axis=-1, keepdims=True) + jnp.sum(p_c * dp_c, axis=-1, keepdims=True)
            ds_w = p_w * (dp_w - delta)
            ds_c = p_c * (dp_c - delta)
            db_ref[variant] += ds_w
            dsw = (ds_w * scale).astype(BF16)
            dsc = (ds_c * scale).astype(BF16)
            dq2 = jnp.zeros((nq, lanes), F32)
            for h, m in enumerate(masks):
                rows = slice(h * nq, (h + 1) * nq)
                dq2 = dq2 + _dot(dsw[rows], _only(m, k2)) + _dot(dsc[rows], kcs[h])
            dq_ref[pl.ds(q0, nq), :] = dq2.astype(BF16)
            dk_ref[pl.ds(k0, nk), :] += _dot_tn(dsw, qs)
            dv_ref[pl.ds(k0, nk), :] += _dot_tn(p_w.astype(BF16), dos)
            dkc_ref[...] += _dot_tn(dsc, qs)
            dvc_ref[...] += _dot_tn(p_c.astype(BF16), dos)
            return carry

        lax.fori_loop(0, R // GQ, group, 0)

    cols = lambda n: _resident((n, lanes), lambda p: (0, p))
    bspec = _resident((None, 3, hpb * nq, nk), lambda p: (p, 0, 0, 0))
    res, carried = _ride_call(
        body, ride, (q, k, v, kc, vc, bias, do), name=name, grid=(width // lanes,),
        in_specs=[cols(L), cols(L), cols(L), cols(C), cols(C), bspec, cols(L)],
        out_specs=[cols(L), cols(L), cols(L), cols(C), cols(C), bspec],
        out_shape=[jax.ShapeDtypeStruct((L, width), BF16)] + [jax.ShapeDtypeStruct((L, width), F32)] * 2 +
                  [jax.ShapeDtypeStruct((C, width), F32)] * 2 +
                  [jax.ShapeDtypeStruct((width // lanes, 3, hpb * nq, nk), F32)],
        compiler_params=_params())
    return (*res, carried)


def _rpb_grad(dbias, name):
    hpb = HEADS_PER_BLOCK
    H = dbias.shape[0] * hpb
    nri, nci = 2 * NA_KH - 1, 2 * NA_KW - 1
    d6 = dbias.reshape(H // hpb, 3, hpb, GQ, GRID_W, GK, GRID_W).transpose(0, 2, 1, 3, 5, 4, 6)
    d6 = d6.reshape(H, 3, GQ, GK, GRID_W, GRID_W)
    col = jnp.arange(GRID_W)
    onehot = (col[None, None, :] - col[None, :, None] + (NA_KW - 1) == jnp.arange(32)[:, None, None]).astype(F32)
    places = [(v, qr, kr) for v in range(3) for qr in range(GQ) for kr in range(GK)]

    def body(d_ref, m_ref, o_ref, t_ref):
        t_ref[...] = jnp.zeros((32, GRID_W), F32)
        o_ref[...] = jnp.zeros((16, 32, 128), F32)
        for ri in range(nri):
            a = None
            for place in places:
                if _group_ri(*place) == ri:
                    blk = d_ref[place]
                    a = blk if a is None else a + blk
            for ci in range(nci):
                t_ref[ci:ci + 1, :] = _sum0(a * m_ref[ci])
            o_ref[ri] = jnp.broadcast_to(jnp.sum(t_ref[...], axis=1, keepdims=True), (32, 128))

    out = pl.pallas_call(
        body, name=name, grid=(H,),
        in_specs=[pl.BlockSpec((None, 3, GQ, GK, GRID_W, GRID_W), lambda h: (h, 0, 0, 0, 0, 0)),
                  pl.BlockSpec((32, GRID_W, GRID_W), lambda h: (0, 0, 0))],
        out_specs=pl.BlockSpec((None, 16, 32, 128), lambda h: (h, 0, 0, 0)),
        out_shape=jax.ShapeDtypeStruct((H, 16, 32, 128), F32),
        scratch_shapes=[pltpu.VMEM((32, GRID_W), F32)])(d6, onehot)
    return out[:, :nri, :nci, 0]


def _window_count(t, w, L):
    lo = jnp.clip(t - w // 2, 0, L)
    hi = jnp.clip(t - w // 2 + w, 0, L)
    return jnp.maximum(hi - lo, 1).astype(F32)


def _running_sum(v, w):
    k = 1
    while k < w:
        v = v + _shift_rows(v, k)
        k *= 2
    return v


def _pool_fwd(u, poolw, pscale, name, tm=512):
    L, W = u.shape
    G = POOL_GROUP_DIM
    tm = min(tm, L)
    nt = L // tm

    def body(c_ref, p_ref, n_ref, w_ref, s_ref, o_ref, dm_ref):
        i = pl.program_id(0)
        ext = _ext(p_ref[...], c_ref[...], n_ref[...], i, nt)
        t = i * tm + lax.broadcasted_iota(jnp.int32, (tm, 1), 0)
        for g, w in enumerate(POOL_WINDOWS):
            e = ext[:, g * G:(g + 1) * G]
            win = _shift_rows(_running_sum(e, w), -(w // 2 - 1))[HALO:HALO + tm]
            dmx = (win / _window_count(t, w, L) - e[HALO:HALO + tm]).astype(BF16)
            dm_ref[:, g * G:(g + 1) * G] = dmx
            o_ref[:, g * G:(g + 1) * G] = (_dot(dmx, w_ref[g]) * s_ref[:, g * G:(g + 1) * G]).astype(BF16)

    return pl.pallas_call(
        body, name=name, grid=(nt,),
        in_specs=[_tile(tm, W), _halo_prev(tm, W), _halo_next(tm, W, L), _full((4, G, G)), _full((1, W))],
        out_specs=[_tile(tm, W), _tile(tm, W)],
        out_shape=[jax.ShapeDtypeStruct((L, W), BF16)] * 2, compiler_params=_params())(u, u, u, poolw, pscale)


def _pool_bwd(dpool, dmx, poolw, pscale, name, tm=512):
    L, W = dpool.shape
    G = POOL_GROUP_DIM
    tm = min(tm, L)
    nt = L // tm

    def body(c_ref, p_ref, n_ref, dm_ref, w_ref, s_ref, du_ref, dw_ref, acc_ref):
        i = pl.program_id(0)
        ext = _ext(p_ref[...], c_ref[...], n_ref[...], i, nt)
        te = i * tm - HALO + lax.broadcasted_iota(jnp.int32, (tm + 2 * HALO, 1), 0)

        @pl.when(i == 0)
        def _():
            dw_ref[...] = jnp.zeros((4 * G, G), F32)

        rows = []
        for g, w in enumerate(POOL_WINDOWS):
            sc = s_ref[:, g * G:(g + 1) * G]
            dpre = (ext[:, g * G:(g + 1) * G] * sc).astype(BF16)
            dd = _dot_nt(dpre, w_ref[g])
            spread = _shift_rows(_running_sum(dd / _window_count(te, w, L), w), -(w // 2))
            du_ref[:, g * G:(g + 1) * G] = (spread - dd)[HALO:HALO + tm]
            dmx_g = dm_ref[:, g * G:(g + 1) * G]
            rows.append(_sum0(c_ref[:, g * G:(g + 1) * G] * _dot(dmx_g, w_ref[g])))
            dw_ref[g * G:(g + 1) * G, :] += _dot_tn(dmx_g, dpre[HALO:HALO + tm])
        _acc_rows(acc_ref, i == 0, [jnp.concatenate(rows, axis=1)])

    return pl.pallas_call(
        body, name=name, grid=(nt,),
        in_specs=[_tile(tm, W), _halo_prev(tm, W), _halo_next(tm, W, L), _tile(tm, W), _full((4, G, G)),
                  _full((1, W))],
        out_specs=[_tile(tm, W), _full((4 * G, G)), _full((8, W))],
        out_shape=[jax.ShapeDtypeStruct((L, W), F32), jax.ShapeDtypeStruct((4 * G, G), F32),
                   jax.ShapeDtypeStruct((8, W), F32)],
        compiler_params=_params())(dpool, dpool, dpool, dmx, poolw, pscale)


def _conv3(z, cw):
    return _shift_rows(z, 1) * cw[0] + z * cw[1] + _shift_rows(z, -1) * cw[2]


def _conv_fwd(x, prm, wing, woutg, name, tm=512):
    L, D = x.shape
    Ws = wing.shape[-1]
    tm = min(tm, L)
    nt = L // tm
    te = tm + 2 * HALO

    def body(c_ref, p_ref, n_ref, prm_ref, wi_ref, wo_ref, xo_ref, y_ref, b_ref):
        i = pl.program_id(0)
        xe = jnp.concatenate([p_ref[...], c_ref[...], n_ref[...]], axis=0)
        hn, _, _, _ = _nm(xe, prm_ref[3:4, :], prm_ref[0:1, :], prm_ref[1:2, :])
        hb = hn.astype(BF16)
        proj = jnp.concatenate([_dot(hb, wi_ref[s]) for s in range(4)], axis=1)
        bg, cg, xin = proj[:, :D], proj[:, D:2 * D], proj[:, 2 * D:]
        tpos = i * tm - HALO + lax.broadcasted_iota(jnp.int32, (te, 1), 0)
        valid = ((tpos >= 0) & (tpos < L)).astype(F32)
        yc = _conv3(cg * xin * valid, [prm_ref[4 + k:5 + k, :] for k in range(3)])
        h2 = (bg * yc)[HALO:HALO + tm].astype(BF16)
        yv = _dot(h2, wo_ref[...])
        y_ref[...] = yv.astype(BF16)
        xo_ref[...] = c_ref[...] + prm_ref[2:3, :] * yv
        b_ref[...] = proj[HALO:HALO + tm].astype(BF16)

    return pl.pallas_call(
        body, name=name, grid=(nt,),
        in_specs=[_tile(tm, D), _halo_prev(tm, D), _halo_next(tm, D, L), _full((8, D)),
                  _resident((None, 4, D, Ws), lambda i: (0, 0, 0, 0)),
                  _resident((None, D, D), lambda i: (0, 0, 0))],
        out_specs=[_tile(tm, D), _tile(tm, D), _tile(tm, 3 * D)],
        out_shape=[jax.ShapeDtypeStruct((L, D), F32), jax.ShapeDtypeStruct((L, D), BF16),
                   jax.ShapeDtypeStruct((L, 3 * D), BF16)],
        compiler_params=_params())(x, x, x, prm, wing, woutg)


def _conv_bwd(dout, x, y, bcx, prm, wing, woutg, name, tm=256, ride=None):
    L, D = x.shape
    Ws = wing.shape[-1]
    tm = min(tm, L)
    nt = L // tm
    te = tm + 2 * HALO

    def body(dc_ref, dp_ref, dn_ref, x_ref, y_ref, bc_ref, bp_ref, bn_ref, prm_ref, wi_ref, wo_ref,
             dx_ref, dpr_ref, h2_ref, dy_ref, hn_ref, acc_ref):
        i = pl.program_id(0)
        gain, shift, scale, gate = prm_ref[3:4, :], prm_ref[0:1, :], prm_ref[1:2, :], prm_ref[2:3, :]
        taps = [prm_ref[4 + k:5 + k, :] for k in range(3)]
        do = dc_ref[...]
        doe = _ext(dp_ref[...], do, dn_ref[...], i, nt)
        dye = (gate * doe).astype(BF16)
        dy_ref[...] = dye[HALO:HALO + tm]
        dh2 = _dot_nt(dye, wo_ref[...])
        be = jnp.concatenate([bp_ref[...], bc_ref[...], bn_ref[...]], axis=0).astype(F32)
        bg, cg, xin = be[:, :D], be[:, D:2 * D], be[:, 2 * D:]
        tpos = i * tm - HALO + lax.broadcasted_iota(jnp.int32, (te, 1), 0)
        valid = ((tpos >= 0) & (tpos < L)).astype(F32)
        z = cg * xin * valid
        yc = _conv3(z, taps)
        dyc = dh2 * bg
        h2_ref[...] = (bg * yc)[HALO:HALO + tm].astype(BF16)
        dz = _conv3(dyc, taps[::-1]) * valid
        dproj = jnp.concatenate([dh2 * yc, dz * xin, dz * cg], axis=1)[HALO:HALO + tm].astype(BF16)
        dpr_ref[...] = dproj
        dhn = jnp.zeros((tm, D), F32)
        for s in range(4):
            dhn = dhn + _dot_nt(dproj[:, s * Ws:(s + 1) * Ws], wi_ref[s])
        hn, xhat, r, nrm = _nm(x_ref[...], gain, shift, scale)
        hn_ref[...] = hn.astype(BF16)
        dx, dshift, dscale, dgn = _nm_bwd(dhn, xhat, r, nrm, gain, scale)
        dx_ref[...] = do + dx
        dgate = _sum0(do * y_ref[...].astype(F32))
        dtaps = [_sum0((dyc * _shift_rows(z, 1 - k))[HALO:HALO + tm]) for k in range(3)]
        _acc_rows(acc_ref, i == 0, [dshift, dscale, dgate, dgn] + dtaps)

    res, carried = _ride_call(
        body, ride, (dout, dout, dout, x, y, bcx, bcx, bcx, prm, wing, woutg), name=name, grid=(nt,),
        in_specs=[_tile(tm, D), _halo_prev(tm, D), _halo_next(tm, D, L), _tile(tm, D), _tile(tm, D),
                  _tile(tm, 3 * D), _halo_prev(tm, 3 * D), _halo_next(tm, 3 * D, L), _full((8, D)),
                  _resident((None, 4, D, Ws), lambda i: (0, 0, 0, 0)),
                  _resident((None, D, D), lambda i: (0, 0, 0))],
        out_specs=[_tile(tm, D), _tile(tm, 3 * D), _tile(tm, D), _tile(tm, D), _tile(tm, D), _full((8, D))],
        out_shape=[jax.ShapeDtypeStruct((L, D), F32), jax.ShapeDtypeStruct((L, 3 * D), BF16),
                   jax.ShapeDtypeStruct((L, D), BF16), jax.ShapeDtypeStruct((L, D), BF16),
                   jax.ShapeDtypeStruct((L, D), BF16), jax.ShapeDtypeStruct((8, D), F32)],
        compiler_params=_params())
    return (*res, carried)


def _loss_head(x, tgt, fg, name, tm=512):
    L, D = x.shape
    tm = min(tm, L)

    def body(x_ref, t_ref, g_ref, dx_ref, acc_ref):
        i = pl.program_id(0)
        xv = x_ref[...]
        g = g_ref[...]
        r = lax.rsqrt(jnp.mean(xv * xv, axis=-1, keepdims=True) + RMS_EPS)
        xhat = xv * r
        err = xhat * g - t_ref[...]
        part = 0.5 * jnp.sum(jnp.mean(err * err, axis=-1, keepdims=True), axis=0, keepdims=True)
        dy = err * (1.0 / D)
        dxh = dy * g
        dx_ref[...] = r * (dxh - xhat * jnp.mean(dxh * xhat, axis=-1, keepdims=True))
        _acc_rows(acc_ref, i == 0, [_sum0(dy * xhat), jnp.broadcast_to(part, (1, D))])

    return pl.pallas_call(
        body, name=name, grid=(L // tm,), in_specs=[_tile(tm, D), _tile(tm, D), _full((1, D))],
        out_specs=[_tile(tm, D), _full((8, D))],
        out_shape=[jax.ShapeDtypeStruct((L, D), F32), jax.ShapeDtypeStruct((8, D), F32)],
        compiler_params=_params())(x, tgt, fg)


def _mod_fwd(cond, mod_w, mod_b, name, tn=768):
    nl, D, N = mod_w.shape
    tn = min(tn, N)

    def body(c_ref, w_ref, b_ref, o_ref):
        cv = c_ref[...]
        s = (cv * _sigmoid(cv)).astype(BF16)
        o_ref[...] = _dot(s, w_ref[...].astype(BF16)) + b_ref[...]

    return pl.pallas_call(
        body, name=name, grid=(nl, N // tn),
        in_specs=[pl.BlockSpec((16, D), lambda l, j: (0, 0)), pl.BlockSpec((None, D, tn), lambda l, j: (l, 0, j)),
                  pl.BlockSpec((None, 1, tn), lambda l, j: (l, 0, j))],
        out_specs=pl.BlockSpec((None, 16, tn), lambda l, j: (l, 0, j)),
        out_shape=jax.ShapeDtypeStruct((nl, 16, N), F32), compiler_params=_params())(cond, mod_w, mod_b)


def _mod_bwd(cond, dm, mod_w, name, tn=768):
    nl, D, N = mod_w.shape
    tn = min(tn, N)

    def body(c_ref, d_ref, w_ref, dw_ref, dc_ref):
        first = (pl.program_id(0) == 0) & (pl.program_id(1) == 0)
        cv = c_ref[...]
        s = (cv * _sigmoid(cv)).astype(BF16)
        d = d_ref[...].astype(BF16)
        dw_ref[...] = _dot_tn(s, d)

        @pl.when(first)
        def _():
            dc_ref[...] = jnp.zeros((16, D), F32)

        dc_ref[...] += _dot_nt(d, w_ref[...].astype(BF16))

    return pl.pallas_call(
        body, name=name, grid=(nl, N // tn),
        in_specs=[pl.BlockSpec((16, D), lambda l, j: (0, 0)), pl.BlockSpec((None, 16, tn), lambda l, j: (l, 0, j)),
                  pl.BlockSpec((None, D, tn), lambda l, j: (l, 0, j))],
        out_specs=[pl.BlockSpec((None, D, tn), lambda l, j: (l, 0, j)), pl.BlockSpec((16, D), lambda l, j: (0, 0))],
        out_shape=[jax.ShapeDtypeStruct((nl, D, N), F32), jax.ShapeDtypeStruct((16, D), F32)],
        compiler_params=_params())(cond, dm, mod_w)


def _mod_small_grads(dm_all, cond, dsilu_parts, name):
    nl, _, N = dm_all.shape
    D = cond.shape[1]

    def body(d_ref, c_ref, p_ref, db_ref, dc_ref):
        for l in range(nl):
            db_ref[l] = _sum0(d_ref[l])
        tot = p_ref[0, 8:9, :]
        for k in range(1, N_CHIPS):
            tot = tot + p_ref[2 * k, 8:9, :]
        cv = c_ref[8:9, :]
        sg = _sigmoid(cv)
        dc_ref[...] = tot * (sg * (1.0 + cv * (1.0 - sg)))

    return pl.pallas_call(
        body, name=name, out_shape=[jax.ShapeDtypeStruct((nl, 1, N), F32), jax.ShapeDtypeStruct((1, D), F32)],
    )(dm_all, cond, dsilu_parts)


def _prm(rows, D):
    rows = [r.reshape(1, D) for r in rows]
    return jnp.concatenate(rows + [jnp.zeros((8 - len(rows), D), F32)], axis=0)


def kernel(x, c, ctx, c_ctx, mod_w, mod_b, norm_g, ffn_w13, ffn_w2, even_w_in, even_w_out, na_rpb, pool_w, pool_scale, conv_w_in, conv_w, conv_w_out, final_g, loss_target, m_c_ctx, m_mod_w, m_mod_b, m_norm_g, m_ffn_w13, m_ffn_w2, m_even_w_in, m_even_w_out, m_na_rpb, m_pool_w, m_pool_scale, m_conv_w_in, m_conv_w, m_conv_w_out, m_final_g, v_c_ctx, v_mod_w, v_mod_b, v_norm_g, v_ffn_w13, v_ffn_w2, v_even_w_in, v_even_w_out, v_na_rpb, v_pool_w, v_pool_scale, v_conv_w_in, v_conv_w, v_conv_w_out, v_final_g):
    xi, yi, ci = lax.axis_index("x"), lax.axis_index("y"), lax.axis_index("c")
    chip = 2 * xi + yi
    dev = 4 * xi + 2 * yi + ci
    _, L, D = x.shape
    C = ctx.shape[1]
    Ds = D // N_CHIPS
    Nm = mod_w.shape[-1]
    Fh = ffn_w13.shape[-1]
    Fq = ffn_w2.shape[2]
    assert ffn_w13.shape[:2] == (2, 2) and Fh == 2 * Fq and L % (GQ * GRID_W) == 0 and L // GRID_W >= GK and GQ == NA_KH // 2
    x0, ctx0, tgt = x[0], ctx[0], loss_target[0]

    pad = lambda a: jnp.pad(a, ((0, 0), (0, D - a.shape[1])))
    pack1 = jnp.concatenate([c, pad(norm_g.reshape(6, Ds)), pad(conv_w.reshape(3, Ds)), jnp.zeros((6, D), F32)], axis=0)
    g1 = _small_all_gather(pack1, "ag_cond")
    cond = jnp.concatenate([g1[:, 0], c_ctx[None], jnp.zeros((7, D), F32)], axis=0)
    norm_full = jnp.concatenate([g1[2 * k, 1:7, :Ds] for k in range(N_CHIPS)], axis=1).reshape(2, 3, D)
    convw_full = jnp.concatenate([g1[2 * k, 7:10, :Ds] for k in range(N_CHIPS)], axis=1)

    mod_b_loc = lax.dynamic_slice_in_dim(mod_b, chip * Nm, Nm, axis=1).reshape(2, 1, Nm)
    m_loc = _mod_fwd(cond, mod_w, mod_b_loc, "mod_fwd")
    g2 = _small_all_gather(m_loc.reshape(32, Nm), "ag_mod")
    m_all = jnp.concatenate([g2[2 * k] for k in range(N_CHIPS)], axis=1).reshape(2, 16, N_MOD, D)
    m_lat = lax.dynamic_index_in_dim(m_all, dev, axis=1, keepdims=False)
    m_ctx = m_all[:, 8]

    def prm(mods, layer, base, gain_idx, extra=()):
        return _prm([mods[layer, base], mods[layer, base + 1], mods[layer, base + 2], norm_full[layer, gain_idx],
                     *extra], D)

    def shard_bf16(w, name):
        return _cast_bf16(w.reshape(-1, w.shape[-1]), name).reshape(-1, *w.shape[-2:])

    w13s, w2s = shard_bf16(ffn_w13, "cast_w13"), shard_bf16(ffn_w2, "cast_w2")
    eins, eouts = shard_bf16(even_w_in, "cast_ein"), shard_bf16(even_w_out, "cast_eout")
    cins, couts = shard_bf16(conv_w_in, "cast_cin"), shard_bf16(conv_w_out, "cast_cout")
    ffn_shards = [[w13s[t:t + 1], w2s[t:t + 1]] for t in range(4)]

    def ffn_weights(w13g, w2g):
        return w13g.reshape(1, 4, D, Fh), w2g

    wf = [ffn_weights(*_gather_shards(ffn_shards[0], "ag_ffn0")), None, None, None]
    pos = jnp.stack([chip, ci]).astype(jnp.int32)

    p_f1 = prm(m_lat, 0, 0, 0)
    p_mx = prm(m_lat, 0, 3, 1)
    p_f2 = prm(m_lat, 0, 6, 2)
    p_g1 = prm(m_lat, 1, 0, 0)
    p_cv = prm(m_lat, 1, 3, 1, extra=(convw_full[0], convw_full[1], convw_full[2]))
    p_g2 = prm(m_lat, 1, 6, 2)
    pc_f1 = prm(m_ctx, 0, 0, 0)
    pc_mx = prm(m_ctx, 0, 3, 1)

    x1, ab1, y1, (eing, eoutg) = _ffn_fwd(x0, p_f1, *wf[0], 0, "ffn_fwd_l0a", ride=_broadcast_ride([eins, eouts]))
    eing = eing.reshape(1, 4, D, NA_WIDTH)
    ctx1, abc, yc, _ = _ffn_fwd(ctx0, pc_f1, *wf[0], 0, "ffn_fwd_ctx")
    q, k, v, u, hn_mx = _even_in_fwd(x1, p_mx, eing, "even_in_fwd")
    _, k_c, v_c, _, hn_cx = _even_in_fwd(ctx1, pc_mx, eing, "even_in_ctx")
    bias = _bias_table(na_rpb[0], "bias_table")
    att, gathered = _attn_fwd(q, k, v, k_c, v_c, bias, "attn_fwd", ride=_broadcast_ride(ffn_shards[1]))
    wf[1] = ffn_weights(*gathered)
    pw_b = _cast_bf16(pool_w.reshape(-1, POOL_GROUP_DIM), "cast_poolw").reshape(4, POOL_GROUP_DIM, POOL_GROUP_DIM)
    pool, dmx = _pool_fwd(u, pw_b, pool_scale, "pool_fwd")
    x2, ymx = _even_out_fwd(x1, att, pool, p_mx, eoutg, "even_out_fwd")
    x3, ab2, y2, gathered = _ffn_fwd(x2, p_f2, *wf[1], 0, "ffn_fwd_l0b",
                                     ride=_broadcast_ride(ffn_shards[2] + [cins, couts]))
    wf[2] = ffn_weights(*gathered[:2])
    cing, coutg = gathered[2].reshape(1, 4, D, conv_w_in.shape[-1]), gathered[3]
    x4, ab3, y3, gathered = _ffn_fwd(x3, p_g1, *wf[2], 0, "ffn_fwd_l1a", ride=_broadcast_ride(ffn_shards[3]))
    wf[3] = ffn_weights(*gathered)
    x5, ycv, bcx = _conv_fwd(x4, p_cv, cing, coutg, "conv_fwd")
    x6, ab4, y4, _ = _ffn_fwd(x5, p_g2, *wf[3], 0, "ffn_fwd_l1b")
    dx6, acc_head = _loss_head(x6, tgt, final_g.reshape(1, D), "loss_head")
    loss = lax.psum(acc_head[1, 0], ("x", "y", "c"))

    def ffn_back(dout, xin, ab, yy, p, t, tag, init13=None, init2=None, ride=None):
        sv, gact = ab
        dx, dab, dy, hn, acc, carried = _ffn_bwd(dout, xin, sv, yy, p, *wf[t], 0, f"ffn_bwd_{tag}", ride=ride)
        dw13 = _mm_tn(hn, dab, 4, False, f"dw13_{tag}", init=init13)
        dw2 = _mm_tn(gact, dy, 2, True, f"dw2_{tag}", init=init2)
        return dx, acc, dw13, dw2, carried

    dx5, acc_g2, dw13_3, dw2_3, _ = ffn_back(dx6, x5, ab4, y4, p_g2, 3, "l1b")
    s_a, sb_a = _pair_sums([dw13_3, dw2_3], pos, "l1b")
    dx4, dproj, h2, dycv, hn_cv, acc_cv, got_a = _conv_bwd(dx5, x4, ycv, bcx, p_cv, cing, coutg, "conv_bwd",
                                                           ride=_scatter_ride(sb_a))
    dcin = _mm_tn(hn_cv, dproj, 4, False, "dw_cin")
    dcout = _mm_tn(h2, dycv, 1, False, "dw_cout")
    s_b, sb_b = _pair_sums([dcin, dcout], pos, "conv")
    dx3, acc_g1, dw13_2, dw2_2, got_b = ffn_back(dx4, x3, ab3, y3, p_g1, 2, "l1a", ride=_scatter_ride(sb_b))
    s_c, sb_c = _pair_sums([dw13_2, dw2_2], pos, "l1a")
    dx2, acc_f2, dw13_1, dw2_1, got_c = ffn_back(dx3, x2, ab2, y2, p_f2, 1, "l0b", ride=_scatter_ride(sb_c))

    dymx, datt, dpool, acc_mxo = _even_out_bwd(dx2, ymx, p_mx, eoutg, "even_out_bwd")
    deout = jnp.concatenate([_mm_tn(att, dymx, 1, False, "dw_eout_att"),
                             _mm_tn(pool, dymx, 1, False, "dw_eout_pool")], axis=0)
    s_d, sb_d = _pair_sums([dw13_1, dw2_1, deout], pos, "l0b")
    du, dpoolw, acc_pool = _pool_bwd(dpool, dmx, pw_b, pool_scale, "pool_bwd")
    dq, dk, dv, dkc, dvc, dbias, got_d = _attn_bwd(q, k, v, k_c, v_c, bias, datt, "attn_bwd",
                                                   ride=_scatter_ride(sb_d))
    drpb = _rpb_grad(dbias, "rpb_grad")
    dx1, dstack, acc_mxi = _even_in_bwd(dx2, x1, dq, dk, dv, du, p_mx, eing,
                                        "even_in_bwd")
    zc = jnp.zeros((C, NA_WIDTH), F32)
    dctx1, dstack_c, accc_mx = _even_in_bwd(jnp.zeros((C, D), F32), ctx1, zc, dkc, dvc, zc,
                                            pc_mx, eing, "even_in_bwd_ctx")
    dein_c = _mm_tn(hn_cx, dstack_c, 4, False, "dw_ein_ctx")
    dein = _mm_tn(hn_mx, dstack, 4, False, "dw_ein", init=dein_c)
    s_e, sb_e = _pair_sums([dein], pos, "ein")
    _, accc_f1, dw13_c, dw2_c, _ = ffn_back(dctx1, ctx0, abc, yc, pc_f1, 0, "ctx")
    sv1, gact1 = ab1
    dx0, dab, dy, hn, acc_f1, got_e = _ffn_bwd(dx1, x0, sv1, y1, p_f1, *wf[0], 0, "ffn_bwd_l0a",
                                               ride=_scatter_ride(sb_e))
    dw13_0 = _mm_tn(hn, dab, 4, False, "dw13_l0a", init=dw13_c)
    s_f13, sb_f13 = _pair_sums([dw13_0], pos, "l0a_w13")
    dw2_0, got_f13 = _mm_tn(gact1, dy, 2, True, "dw2_l0a", init=dw2_c, ride=_scatter_ride(sb_f13))
    s_f2, sb_f2 = _pair_sums([dw2_0], pos, "l0a_w2")

    z1 = jnp.zeros((1, D), F32)
    dm_lat = jnp.concatenate([acc_f1[0:3], acc_mxi[0:2], acc_mxo[2:3], acc_f2[0:3],
                              acc_g1[0:3], acc_cv[0:3], acc_g2[0:3]], axis=0)
    dm_ctx = jnp.concatenate([accc_f1[0:3], accc_mx[0:2]] + [z1] * 13, axis=0)
    dnorm = jnp.concatenate([acc_f1[3:4] + accc_f1[3:4], acc_mxi[3:4] + accc_mx[3:4], acc_f2[3:4],
                             acc_g1[3:4], acc_cv[3:4], acc_g2[3:4]], axis=0)
    rpb_flat = jnp.pad(drpb.reshape(-1), (0, 4 * D - drpb.size)).reshape(4, D)
    pack3 = jnp.concatenate([dm_lat, dm_ctx, dnorm, acc_cv[4:7], acc_head[0:1], pad(acc_pool[0:1]), z1,
                             dpoolw.reshape(-1, D), rpb_flat, jnp.zeros((4, D), F32)], axis=0)
    g3 = _small_all_gather(pack3, "ag_small")
    tot = _sum_devices(g3, "sum_small")
    dm_all = jnp.concatenate([g3[:, 0:18].reshape(8, 2, N_MOD * D).transpose(1, 0, 2),
                              tot[18:36].reshape(2, 1, N_MOD * D), jnp.zeros((2, 7, N_MOD * D), F32)], axis=1)
    dm_loc = lax.dynamic_slice_in_dim(dm_all, chip * Nm, Nm, axis=2)
    g_mod_w, dsilu = _mod_bwd(cond, dm_loc, mod_w, "mod_bwd")
    g4 = _small_all_gather(dsilu, "ag_dsilu")
    g_mod_b, g_c_ctx = _mod_small_grads(dm_all, cond, g4, "mod_small")
    g_mod_b = g_mod_b.reshape(2, N_MOD * D)
    g_c_ctx = g_c_ctx.reshape(D)
    g_norm_full = tot[36:42].reshape(2, 3, D)
    g_norm = lax.dynamic_slice_in_dim(g_norm_full, chip * Ds, Ds, axis=2)
    g_conv_w = lax.dynamic_slice_in_dim(tot[42:45], chip * Ds, Ds, axis=1).reshape(1, 3, Ds)
    g_final = tot[45]
    g_pscale = tot[46:47, :pool_scale.shape[1]]
    g_poolw = tot[48:112].reshape(pool_w.shape)
    g_rpb = tot[112:116].reshape(-1)[:na_rpb.size].reshape(na_rpb.shape)

    r13_3, r2_3 = _joins(s_a, got_a, pos, "l1b")
    r_cin, r_cout = _joins(s_b, got_b, pos, "conv")
    r13_2, r2_2 = _joins(s_c, got_c, pos, "l1a")
    r13_1, r2_1, r_eout = _joins(s_d, got_d, pos, "l0b")
    (r_ein,) = _joins(s_e, got_e, pos, "ein")
    adamw_mod_w, got_f2 = _adamw(mod_w, g_mod_w, m_mod_w, v_mod_w, "adamw_mod_w", ride=_scatter_ride(sb_f2))
    (r13_0,) = _joins(s_f13, got_f13, pos, "l0a_w13")
    (r2_0,) = _joins(s_f2, got_f2, pos, "l0a_w2")
    g_w13 = jnp.stack([r13_0, r13_1, r13_2, r13_3]).reshape(ffn_w13.shape)
    g_w2 = jnp.stack([r2_0, r2_1, r2_2, r2_3]).reshape(ffn_w2.shape)
    g_ein, g_eout, g_cin, g_cout = r_ein[None], r_eout[None], r_cin[None], r_cout[None]

    grads = [g_c_ctx, g_mod_w, g_mod_b, g_norm, g_w13, g_w2, g_ein, g_eout, g_rpb, g_poolw, g_pscale, g_cin,
             g_conv_w, g_cout, g_final]
    weights = [c_ctx, mod_w, mod_b, norm_g, ffn_w13, ffn_w2, even_w_in, even_w_out, na_rpb, pool_w, pool_scale,
               conv_w_in, conv_w, conv_w_out, final_g]
    ms = [m_c_ctx, m_mod_w, m_mod_b, m_norm_g, m_ffn_w13, m_ffn_w2, m_even_w_in, m_even_w_out, m_na_rpb, m_pool_w,
          m_pool_scale, m_conv_w_in, m_conv_w, m_conv_w_out, m_final_g]
    vs = [v_c_ctx, v_mod_w, v_mod_b, v_norm_g, v_ffn_w13, v_ffn_w2, v_even_w_in, v_even_w_out, v_na_rpb, v_pool_w,
          v_pool_scale, v_conv_w_in, v_conv_w, v_conv_w_out, v_final_g]
    names = ["c_ctx", "mod_w", "mod_b", "norm_g", "ffn_w13", "ffn_w2", "even_w_in", "even_w_out", "na_rpb", "pool_w",
             "pool_scale", "conv_w_in", "conv_w", "conv_w_out", "final_g"]
    deltas, new_m, new_v = [], [], []
    for n, w, g, m, vv in zip(names, weights, grads, ms, vs):
        g = g.reshape(w.shape)
        if n == "mod_w":
            d, mn, vn = adamw_mod_w
        elif w.ndim == 1:
            d, mn, vn = (t.reshape(w.shape) for t in _adamw(w[None], g[None], m[None], vv[None], f"adamw_{n}"))
        else:
            d, mn, vn = _adamw(w, g, m, vv, f"adamw_{n}")
        deltas.append(d)
        new_m.append(mn)
        new_v.append(vn)
    grads = [g.reshape(w.shape) for g, w in zip(grads, weights)]
    return (loss, dx0[None], *grads, *deltas, *new_m, *new_v)
```

```python
import jax
import jax.numpy as jnp
from jax import lax
from jax.experimental import pallas as pl
from jax.experimental.pallas import tpu as pltpu

F32 = jnp.float32
BF16 = jnp.bfloat16
MESH = pl.DeviceIdType.MESH

GRID_W = 64
NA_HEADS = 8
NA_HEAD_DIM = 64
NA_KH = 8
NA_KW = 16
GQ = 4
GK = GQ + NA_KH
NA_WIDTH = NA_HEADS * NA_HEAD_DIM
POOL_WINDOWS = (2, 4, 8, 16)
POOL_GROUP_DIM = 128
N_MOD = 9
RMS_EPS = 1e-6
NEG_INF = -1e30
ADAM_LR, ADAM_B1, ADAM_B2, ADAM_EPS, ADAM_WD, ADAM_STEP = 0.001, 0.9, 0.999, 1e-08, 0.01, 10

HALO = 16
VMEM_LIMIT = 56 * 1024 * 1024
N_CHIPS = 4
N_DEV = 8


def _dot(a, b):
    return jnp.dot(a, b, preferred_element_type=F32)


def _dot_nt(a, b):
    return lax.dot_general(a, b, (((1,), (1,)), ((), ())), preferred_element_type=F32)


def _dot_tn(a, b):
    return lax.dot_general(a, b, (((0,), (0,)), ((), ())), preferred_element_type=F32)


def _sigmoid(a):
    return 1.0 / (1.0 + jnp.exp(-a))


def _sum0(v):
    return jnp.sum(v, axis=0, keepdims=True)


def _nm(x, g, shift, scale):
    r = lax.rsqrt(jnp.mean(x * x, axis=-1, keepdims=True) + RMS_EPS)
    xhat = x * r
    nrm = xhat * g
    return nrm * (1.0 + scale) + shift, xhat, r, nrm


def _nm_bwd(dhn, xhat, r, nrm, g, scale):
    dshift = _sum0(dhn)
    dscale = _sum0(dhn * nrm)
    dnrm = dhn * (1.0 + scale)
    dgn = _sum0(dnrm * xhat)
    dxh = dnrm * g
    dx = r * (dxh - xhat * jnp.mean(dxh * xhat, axis=-1, keepdims=True))
    return dx, dshift, dscale, dgn


def _acc_rows(acc_ref, first, rows):
    @pl.when(first)
    def _():
        acc_ref[...] = jnp.zeros(acc_ref.shape, acc_ref.dtype)
    for k, row in enumerate(rows):
        if row is not None:
            acc_ref[k:k + 1, :] += row


def _shift_rows(v, k):
    n = v.shape[0]
    k = k % n
    return v if k == 0 else pltpu.roll(v, k, 0)


def _tile(tm, w):
    return pl.BlockSpec((tm, w), lambda i: (i, 0))


def _full(shape):
    nd = len(shape)
    return pl.BlockSpec(shape, lambda i: (0,) * nd)


def _resident(block, imap):
    return pl.BlockSpec(block, imap, pipeline_mode=pl.Buffered(1))


def _halo_prev(tm, w):
    return pl.BlockSpec((HALO, w), lambda i: (jnp.maximum(i * (tm // HALO) - 1, 0), 0))


def _halo_next(tm, w, L):
    return pl.BlockSpec((HALO, w), lambda i: (jnp.minimum((i + 1) * (tm // HALO), L // HALO - 1), 0))


def _params(vmem=VMEM_LIMIT):
    return pltpu.CompilerParams(vmem_limit_bytes=vmem)


def _pick_rows(rows, cols, itemsize=4, target=1 << 20):
    best = None
    for t in range(8, rows + 1, 8):
        if rows % t == 0 and t * cols * itemsize <= target:
            best = t
    return best if best is not None else rows


def _ext(prev, cur, nxt, i, nt):
    prev = jnp.where(i > 0, prev, jnp.zeros_like(prev))
    nxt = jnp.where(i < nt - 1, nxt, jnp.zeros_like(nxt))
    return jnp.concatenate([prev, cur, nxt], axis=0)


def _cast_bf16(a2d, name):
    rows, cols = a2d.shape
    tr = _pick_rows(rows, cols)

    def body(a_ref, o_ref):
        o_ref[...] = a_ref[...].astype(BF16)

    return pl.pallas_call(
        body, name=name, grid=(rows // tr,), in_specs=[_tile(tr, cols)], out_specs=_tile(tr, cols),
        out_shape=jax.ShapeDtypeStruct((rows, cols), BF16))(a2d)


def _sum_devices(g, name):
    n, rows, cols = g.shape
    tr = _pick_rows(rows, cols, target=1 << 18)

    def body(g_ref, o_ref):
        s = g_ref[0]
        for d in range(1, n):
            s = s + g_ref[d]
        o_ref[...] = s

    return pl.pallas_call(
        body, name=name, grid=(rows // tr,), in_specs=[pl.BlockSpec((n, tr, cols), lambda i: (0, i, 0))],
        out_specs=_tile(tr, cols), out_shape=jax.ShapeDtypeStruct((rows, cols), F32))(g)


def _adamw(w, g, m, v, name, ride=None, regrad=False):
    shape = w.shape
    cols = shape[-1]
    rows = w.size // cols
    w2, g2, m2, v2 = (t.reshape(rows, cols) for t in (w, g, m, v))
    tr = _pick_rows(rows, cols)
    c1 = 1.0 - ADAM_B1 ** ADAM_STEP
    c2 = 1.0 - ADAM_B2 ** ADAM_STEP
    n_out = 4 if regrad else 3

    def body(w_ref, g_ref, m_ref, v_ref, d_ref, mo_ref, vo_ref, *again):
        gg = g_ref[...]
        mn = ADAM_B1 * m_ref[...] + (1.0 - ADAM_B1) * gg
        vn = ADAM_B2 * v_ref[...] + (1.0 - ADAM_B2) * (gg * gg)
        d_ref[...] = -ADAM_LR * ((mn / c1) / (jnp.sqrt(vn / c2) + ADAM_EPS) + ADAM_WD * w_ref[...])
        mo_ref[...] = mn
        vo_ref[...] = vn
        for ref in again:
            ref[...] = gg

    outs, carried = _ride_call(
        body, ride, (w2, g2, m2, v2), name=name, grid=(rows // tr,), in_specs=[_tile(tr, cols)] * 4,
        out_specs=[_tile(tr, cols)] * n_out, out_shape=[jax.ShapeDtypeStruct((rows, cols), F32)] * n_out)
    outs = tuple(o.reshape(shape) for o in outs)
    return outs if ride is None else (outs, carried)


def _mesh_pos():
    x, y, c = lax.axis_index("x"), lax.axis_index("y"), lax.axis_index("c")
    chips = [(1 - x, y), (x, 1 - y), (1 - x, 1 - y)]
    return x, y, c, chips


def _hbm_specs(n):
    return [pl.BlockSpec(memory_space=pltpu.HBM)] * n


def _small_all_gather(v, name):
    rows, w = v.shape

    def body(x_ref, out_ref, send_sems, recv_sems, local_sem):
        x, y, c, chips = _mesh_pos()
        me, sibling = (x, y, c), (x, y, 1 - c)

        def blk(px, py, pc):
            return out_ref.at[4 * px + 2 * py + pc]

        def copy(k, block, to, src=None):
            return pltpu.make_async_remote_copy(
                src_ref=blk(*block) if src is None else src, dst_ref=blk(*block),
                send_sem=send_sems.at[k], recv_sem=recv_sems.at[k], device_id=to, device_id_type=MESH)

        mine = pltpu.make_async_copy(x_ref, blk(*me), local_sem)
        mine.start()
        first = [copy(0, me, sibling, src=x_ref)]
        first += [copy(1 + j, me, (*chip, c), src=x_ref) for j, chip in enumerate(chips)]
        for cp in first:
            cp.start()
        passed = [copy(4 + j, (*chip, c), sibling) for j, chip in enumerate(chips)]
        for j, chip in enumerate(chips):
            copy(1 + j, (*chip, c), me).wait_recv()
            passed[j].start()
        copy(0, sibling, me).wait_recv()
        for j, chip in enumerate(chips):
            copy(4 + j, (*chip, 1 - c), me).wait_recv()
        for cp in first + passed:
            cp.wait_send()
        mine.wait()

    return pl.pallas_call(
        body, name=name, out_shape=jax.ShapeDtypeStruct((N_DEV, rows, w), v.dtype),
        in_specs=[pl.BlockSpec(memory_space=pltpu.VMEM)], out_specs=pl.BlockSpec(memory_space=pltpu.VMEM),
        scratch_shapes=[pltpu.SemaphoreType.DMA((7,)), pltpu.SemaphoreType.DMA((7,)), pltpu.SemaphoreType.DMA],
    )(v)


def _gather_shards(shards, name):
    n = len(shards)

    def body(*refs):
        ins, outs = refs[:n], refs[n:2 * n]
        send_sems, recv_sems, local_sems = refs[2 * n:]
        x, y, c, chips = _mesh_pos()
        k = 2 * x + y
        sibling = (x, y, 1 - c)

        def window(t, chip_k, half):
            r = ins[t].shape[1]
            return outs[t].at[:, pl.ds(chip_k * r + half * (r // 2), r // 2), :]

        def copy(t, j, chip_k, half, to, src=None):
            return pltpu.make_async_remote_copy(
                src_ref=window(t, chip_k, half) if src is None else src, dst_ref=window(t, chip_k, half),
                send_sem=send_sems.at[6 * t + j], recv_sem=recv_sems.at[6 * t + j], device_id=to, device_id_type=MESH)

        started, local = [], []
        for t in range(n):
            r = ins[t].shape[1]
            lc = pltpu.make_async_copy(ins[t], outs[t].at[:, pl.ds(k * r, r), :], local_sems.at[t])
            lc.start()
            local.append(lc)
            src = ins[t].at[:, pl.ds(c * (r // 2), r // 2), :]
            for j, chip in enumerate(chips):
                cp = copy(t, j, k, c, (*chip, c), src=src)
                cp.start()
                started.append(cp)
        for t in range(n):
            for j, chip in enumerate(chips):
                kj = 2 * chip[0] + chip[1]
                copy(t, j, kj, c, sibling).wait_recv()
                cp = copy(t, 3 + j, kj, c, sibling)
                cp.start()
                started.append(cp)
        for t in range(n):
            for j, chip in enumerate(chips):
                kj = 2 * chip[0] + chip[1]
                copy(t, 3 + j, kj, 1 - c, sibling).wait_recv()
        for cp in started:
            cp.wait_send()
        for lc in local:
            lc.wait()

    out_shape = [jax.ShapeDtypeStruct((s.shape[0], N_CHIPS * s.shape[1], s.shape[2]), s.dtype) for s in shards]
    return pl.pallas_call(
        body, name=name, out_shape=out_shape, in_specs=_hbm_specs(n), out_specs=_hbm_specs(n),
        scratch_shapes=[pltpu.SemaphoreType.DMA((6 * n,)), pltpu.SemaphoreType.DMA((6 * n,)),
                        pltpu.SemaphoreType.DMA((n,))],
    )(*shards)


def _chunk_rows(h, w):
    best = 16
    for t in range(16, h + 1, 16):
        if h % t == 0 and t * w * 4 <= (2 << 20):
            best = t
    return best


def _pair_sum(part, pos, name):
    _, h, w = part.shape
    cr = _chunk_rows(h, w)
    nc = h // cr
    n = 4 * nc
    slots = 4

    def body(pos_ref, own_ref, send_ref, s_ref, sb_ref, rbuf, send_sems, recv_sems):
        x, y, c, _ = _mesh_pos()
        k = pl.program_id(0)

        def copy(slot):
            return pltpu.make_async_remote_copy(
                src_ref=send_ref, dst_ref=rbuf.at[slot], send_sem=send_sems.at[slot], recv_sem=recv_sems.at[slot],
                device_id=(x, y, 1 - c), device_id_type=MESH)

        @pl.when(k < n)
        def _():
            copy(k % slots).start()

        @pl.when(k > 0)
        def _():
            before = (k + slots - 1) % slots
            copy(before).wait_recv()
            s = own_ref[...] + rbuf[before]
            s_ref[...] = s
            sb_ref[...] = s.astype(BF16)

        @pl.when(k < n)
        def _():
            copy(k % slots).wait_send()

    def own(k, p):
        j = jnp.maximum(k - 1, 0)
        return ((2 * (j // nc) + p[1]) * nc + j % nc, 0)

    def send(k, p):
        j = jnp.minimum(k, n - 1)
        return ((2 * (j // nc) + 1 - p[1]) * nc + j % nc, 0)

    grid_spec = pltpu.PrefetchScalarGridSpec(
        num_scalar_prefetch=1, grid=(n + 1,),
        in_specs=[pl.BlockSpec((cr, w), own), pl.BlockSpec((cr, w), send)],
        out_specs=[pl.BlockSpec((cr, w), lambda k, p: (jnp.maximum(k - 1, 0), 0))] * 2,
        scratch_shapes=[pltpu.VMEM((slots, cr, w), F32), pltpu.SemaphoreType.DMA((slots,)),
                        pltpu.SemaphoreType.DMA((slots,))])
    part2 = part.reshape(8 * h, w)
    s, sb = pl.pallas_call(
        body, name=name, grid_spec=grid_spec, compiler_params=_params(),
        out_shape=[jax.ShapeDtypeStruct((4 * h, w), F32), jax.ShapeDtypeStruct((4 * h, w), BF16)],
    )(pos, part2, part2)
    return s.reshape(4, h, w), sb.reshape(4, h, w)


class _Ride:
    def __init__(self, ins, out_shape, sems, copies):
        self.ins, self.out_shape, self.sems, self.copies = list(ins), list(out_shape), list(sems), copies

    def start(self, ins, outs, sems):
        sends, _, _, local = self.copies(ins, outs, sems)
        for cp in local + sends:
            cp.start()

    def finish(self, ins, outs, sems):
        _, recvs, sends, local = self.copies(ins, outs, sems)
        for cp in recvs:
            cp.wait_recv()
        for cp in sends:
            cp.wait_send()
        for cp in local:
            cp.wait()


def _scatter_ride(sums_bf16):
    n = len(sums_bf16)

    def copies(ins, outs, sems):
        send_sems, recv_sems = sems
        x, y, c, chips = _mesh_pos()
        cps = [pltpu.make_async_remote_copy(
            src_ref=ins[t].at[2 * chip[0] + chip[1]], dst_ref=outs[t].at[j],
            send_sem=send_sems.at[3 * t + j], recv_sem=recv_sems.at[3 * t + j],
            device_id=(*chip, c), device_id_type=MESH) for t in range(n) for j, chip in enumerate(chips)]
        return cps, cps, cps, []

    return _Ride(sums_bf16, [jax.ShapeDtypeStruct((3,) + s.shape[1:], BF16) for s in sums_bf16],
                 [pltpu.SemaphoreType.DMA((3 * n,)), pltpu.SemaphoreType.DMA((3 * n,))], copies)


def _broadcast_ride(shards):
    n = len(shards)

    def copies(ins, outs, sems):
        send_sems, recv_sems, local_sems = sems
        x, y, c, chips = _mesh_pos()
        k = 2 * x + y
        sends, recvs, local = [], [], []
        for t in range(n):
            r = ins[t].shape[1]
            h = r // 2
            local.append(pltpu.make_async_copy(ins[t], outs[t].at[:, pl.ds(k * r, r), :], local_sems.at[t]))
            src = ins[t].at[:, pl.ds(c * h, h), :]
            mine = outs[t].at[:, pl.ds(k * r + c * h, h), :]
            for j, chip in enumerate(chips):
                kj = 2 * chip[0] + chip[1]
                for d in range(2):
                    sends.append(pltpu.make_async_remote_copy(
                        src_ref=src, dst_ref=mine, send_sem=send_sems.at[6 * t + 2 * j + d],
                        recv_sem=recv_sems.at[6 * t + 2 * j + c], device_id=(*chip, d), device_id_type=MESH))
                    theirs = outs[t].at[:, pl.ds(kj * r + d * h, h), :]
                    recvs.append(pltpu.make_async_remote_copy(
                        src_ref=theirs, dst_ref=theirs, send_sem=send_sems.at[6 * t + 2 * j + d],
                        recv_sem=recv_sems.at[6 * t + 2 * j + d], device_id=(*chip, d), device_id_type=MESH))
        return sends, recvs, sends, local

    return _Ride(shards, [jax.ShapeDtypeStruct((s.shape[0], N_CHIPS * s.shape[1], s.shape[2]), s.dtype) for s in shards],
                 [pltpu.SemaphoreType.DMA((6 * n,)), pltpu.SemaphoreType.DMA((6 * n,)), pltpu.SemaphoreType.DMA((n,))],
                 copies)


def _ride_call(body, ride, args, *, name, grid, in_specs, out_specs, out_shape, compiler_params=None):
    in_specs, out_specs, out_shape = list(in_specs), list(out_specs), list(out_shape)
    if ride is None:
        res = pl.pallas_call(body, name=name, grid=grid, in_specs=in_specs, out_specs=out_specs, out_shape=out_shape,
                             compiler_params=compiler_params)(*args)
        return list(res), []
    ni, no, ri, ro = len(in_specs), len(out_specs), len(ride.ins), len(ride.out_shape)

    def at_step(pick):
        hit = None
        for d, n in enumerate(grid):
            here = pl.program_id(d) == pick(n)
            hit = here if hit is None else hit & here
        return hit

    def carried(*refs):
        ins, rins = refs[:ni], refs[ni:ni + ri]
        outs, routs = refs[ni + ri:ni + ri + no], refs[ni + ri + no:ni + ri + no + ro]
        sems = refs[ni + ri + no + ro:]

        @pl.when(at_step(lambda n: 0))
        def _():
            ride.start(rins, routs, sems)

        body(*ins, *outs)

        @pl.when(at_step(lambda n: n - 1))
        def _():
            ride.finish(rins, routs, sems)

    res = pl.pallas_call(
        carried, name=name, grid=grid, in_specs=in_specs + _hbm_specs(ri), out_specs=out_specs + _hbm_specs(ro),
        out_shape=out_shape + ride.out_shape, scratch_shapes=ride.sems, compiler_params=compiler_params,
    )(*args, *ride.ins)
    return list(res[:no]), list(res[no:])


def _sum_and_join(sums, got, pos, name):
    _, h, w = sums.shape
    cr = _chunk_rows(h, w)

    def body(pos_ref, mine_ref, got_ref, o_ref, ebuf, rbuf, send_sems, recv_sems):
        x, y, c, _ = _mesh_pos()
        slot = pl.program_id(0) % 2
        e = mine_ref[...]
        for j in range(3):
            e = e + got_ref[j].astype(F32)
        ebuf[slot] = e
        cp = pltpu.make_async_remote_copy(
            src_ref=ebuf.at[slot], dst_ref=rbuf.at[slot], send_sem=send_sems.at[slot], recv_sem=recv_sems.at[slot],
            device_id=(x, y, 1 - c), device_id_type=MESH)
        cp.start()
        o_ref[pos_ref[1]] = e
        cp.wait_recv()
        o_ref[1 - pos_ref[1]] = rbuf[slot]
        cp.wait_send()

    grid_spec = pltpu.PrefetchScalarGridSpec(
        num_scalar_prefetch=1, grid=(h // cr,),
        in_specs=[pl.BlockSpec((None, cr, w), lambda i, p: (p[0], i, 0)),
                  pl.BlockSpec((3, cr, w), lambda i, p: (0, i, 0))],
        out_specs=pl.BlockSpec((2, cr, w), lambda i, p: (0, i, 0)),
        scratch_shapes=[pltpu.VMEM((2, cr, w), F32), pltpu.VMEM((2, cr, w), F32),
                        pltpu.SemaphoreType.DMA((2,)), pltpu.SemaphoreType.DMA((2,))])
    return pl.pallas_call(
        body, name=name, grid_spec=grid_spec, compiler_params=_params(),
        out_shape=jax.ShapeDtypeStruct((2, h, w), F32),
    )(pos, sums, got)


def _pair_sums(parts, pos, tag):
    pairs = [_pair_sum(p.reshape(8, p.shape[0] // 8, p.shape[1]), pos, f"rs_pair_{tag}_{t}")
             for t, p in enumerate(parts)]
    return [s for s, _ in pairs], [sb for _, sb in pairs]


def _joins(sums, got, pos, tag):
    out = []
    for t, (s, r) in enumerate(zip(sums, got)):
        full = _sum_and_join(s, r, pos, f"rs_join_{tag}_{t}")
        out.append(full.reshape(2 * full.shape[1], full.shape[2]))
    return out


def _ffn_fwd(x, prm, w13g, w2g, t, name, tm=512, ride=None):
    L, D = x.shape
    Fh = w13g.shape[-1]
    tm = min(tm, L)

    def body(x_ref, p_ref, w13_ref, w2_ref, xo_ref, sv_ref, g_ref, y_ref):
        xv = x_ref[...]
        hn, _, _, _ = _nm(xv, p_ref[3:4, :], p_ref[0:1, :], p_ref[1:2, :])
        hb = hn.astype(BF16)
        acc = jnp.zeros((tm, D), F32)
        for j in range(2):
            a = _dot(hb, w13_ref[j])
            b = _dot(hb, w13_ref[2 + j])
            sg = _sigmoid(a)
            sa = a * sg
            sv_ref[:, j * Fh:(j + 1) * Fh] = sa.astype(BF16)
            sv_ref[:, (2 + j) * Fh:(3 + j) * Fh] = (b * (sg * (1.0 + a * (1.0 - sg)))).astype(BF16)
            g = (sa * b).astype(BF16)
            g_ref[:, j * Fh:(j + 1) * Fh] = g
            acc = acc + _dot(g, w2_ref[j * Fh:(j + 1) * Fh, :])
        y_ref[...] = acc.astype(BF16)
        xo_ref[...] = xv + (0.5 * p_ref[2:3, :]) * acc

    res, carried = _ride_call(
        body, ride, (x, prm, w13g, w2g), name=name, grid=(L // tm,),
        in_specs=[_tile(tm, D), _full((8, D)),
                  _resident((None, 4, D, Fh), lambda i: (t, 0, 0, 0)),
                  _resident((None, 2 * Fh, D), lambda i: (t, 0, 0))],
        out_specs=[_tile(tm, D), _tile(tm, 4 * Fh), _tile(tm, 2 * Fh), _tile(tm, D)],
        out_shape=[jax.ShapeDtypeStruct((L, D), F32), jax.ShapeDtypeStruct((L, 4 * Fh), BF16),
                   jax.ShapeDtypeStruct((L, 2 * Fh), BF16), jax.ShapeDtypeStruct((L, D), BF16)],
        compiler_params=_params())
    xo, sv, g, y = res
    return xo, (sv, g), y, carried


def _head_grad(xo, tgt, fg):
    D = xo.shape[-1]
    r = lax.rsqrt(jnp.mean(xo * xo, axis=-1, keepdims=True) + RMS_EPS)
    xhat = xo * r
    err = xhat * fg - tgt
    loss = 0.5 * jnp.sum(jnp.mean(err * err, axis=-1, keepdims=True), axis=0, keepdims=True)
    dy = err * (1.0 / D)
    dxh = dy * fg
    return r * (dxh - xhat * jnp.mean(dxh * xhat, axis=-1, keepdims=True)), _sum0(dy * xhat), loss


def _ffn_bwd(dout, x, sv, y, prm, w13g, w2g, t, name, tm=256, ride=None, head=None):
    L, D = x.shape
    Fh = w13g.shape[-1]
    tm = min(tm, L)

    def body(do_ref, x_ref, sv_ref, y_ref, p_ref, w13_ref, w2_ref, *rest):
        dx_ref, dab_ref, dy_ref, hn_ref, acc_ref = rest[-5:]
        i = pl.program_id(0)
        head_rows = []
        if head is None:
            do = do_ref[...]
        else:
            do, dfg, loss = _head_grad(do_ref[...], rest[0][...], rest[1][...])
            head_rows = [dfg, jnp.broadcast_to(loss, (1, D))]
        gain, shift, scale, gate = p_ref[3:4, :], p_ref[0:1, :], p_ref[1:2, :], p_ref[2:3, :]
        hn, xhat, r, nrm = _nm(x_ref[...], gain, shift, scale)
        hn_ref[...] = hn.astype(BF16)
        dgate = 0.5 * _sum0(do * y_ref[...].astype(F32))
        dyb = ((0.5 * gate) * do).astype(BF16)
        dy_ref[...] = dyb
        dhn = jnp.zeros((tm, D), F32)
        for j in range(2):
            dg = _dot_nt(dyb, w2_ref[j * Fh:(j + 1) * Fh, :])
            da = (dg * sv_ref[:, (2 + j) * Fh:(3 + j) * Fh].astype(F32)).astype(BF16)
            db = (dg * sv_ref[:, j * Fh:(j + 1) * Fh].astype(F32)).astype(BF16)
            dab_ref[:, j * Fh:(j + 1) * Fh] = da
            dab_ref[:, (2 + j) * Fh:(3 + j) * Fh] = db
            dhn = dhn + _dot_nt(da, w13_ref[j]) + _dot_nt(db, w13_ref[2 + j])
        dx, dshift, dscale, dgn = _nm_bwd(dhn, xhat, r, nrm, gain, scale)
        dx_ref[...] = do + dx
        _acc_rows(acc_ref, i == 0, [dshift, dscale, dgate, dgn] + head_rows)

    head_args = () if head is None else head
    head_specs = [] if head is None else [_tile(tm, D), _full((1, D))]
    res, carried = _ride_call(
        body, ride, (dout, x, sv, y, prm, w13g, w2g, *head_args), name=name, grid=(L // tm,),
        in_specs=[_tile(tm, D), _tile(tm, D), _tile(tm, 4 * Fh), _tile(tm, D), _full((8, D)),
                  _resident((None, 4, D, Fh), lambda i: (t, 0, 0, 0)),
                  _resident((None, 2 * Fh, D), lambda i: (t, 0, 0))] + head_specs,
        out_specs=[_tile(tm, D), _tile(tm, 4 * Fh), _tile(tm, D), _tile(tm, D), _full((8, D))],
        out_shape=[jax.ShapeDtypeStruct((L, D), F32), jax.ShapeDtypeStruct((L, 4 * Fh), BF16),
                   jax.ShapeDtypeStruct((L, D), BF16), jax.ShapeDtypeStruct((L, D), BF16),
                   jax.ShapeDtypeStruct((8, D), F32)],
        compiler_params=_params())
    return (*res, carried)


def _mm_tn(a, b, slabs, a_slabbed, name, init=None, tl=1024, ride=None):
    L = a.shape[0]
    ka = a.shape[1] // slabs if a_slabbed else a.shape[1]
    nb = b.shape[1] if a_slabbed else b.shape[1] // slabs
    tl = min(tl, L)
    has_init = init is not None

    def body(a_ref, b_ref, *rest):
        o_ref = rest[-1]
        step = pl.program_id(1)

        @pl.when(step == 0)
        def _():
            o_ref[...] = rest[0][...] if has_init else jnp.zeros((ka, nb), F32)

        o_ref[...] += _dot_tn(a_ref[...], b_ref[...])

    in_specs = [pl.BlockSpec((tl, ka), (lambda s, l: (l, s)) if a_slabbed else (lambda s, l: (l, 0))),
                pl.BlockSpec((tl, nb), (lambda s, l: (l, 0)) if a_slabbed else (lambda s, l: (l, s)))]
    args = [a, b]
    if has_init:
        in_specs.append(pl.BlockSpec((ka, nb), lambda s, l: (s, 0)))
        args.append(init)
    res, carried = _ride_call(
        body, ride, args, name=name, grid=(slabs, L // tl), in_specs=in_specs,
        out_specs=[pl.BlockSpec((ka, nb), lambda s, l: (s, 0))],
        out_shape=[jax.ShapeDtypeStruct((slabs * ka, nb), F32)], compiler_params=_params())
    return res[0] if ride is None else (res[0], carried)


def _even_in_fwd(x, prm, wing, name, tm=512):
    L, D = x.shape
    W = wing.shape[-1]
    tm = min(tm, L)

    def body(x_ref, p_ref, w_ref, q_ref, k_ref, v_ref, u_ref, hn_ref):
        hn, _, _, _ = _nm(x_ref[...], p_ref[3:4, :], p_ref[0:1, :], p_ref[1:2, :])
        hb = hn.astype(BF16)
        hn_ref[...] = hb
        q_ref[...] = _dot(hb, w_ref[0]).astype(BF16)
        k_ref[...] = _dot(hb, w_ref[1]).astype(BF16)
        v_ref[...] = _dot(hb, w_ref[2]).astype(BF16)
        u_ref[...] = _dot(hb, w_ref[3])

    return pl.pallas_call(
        body, name=name, grid=(L // tm,),
        in_specs=[_tile(tm, D), _full((8, D)), _resident((None, 4, D, W), lambda i: (0, 0, 0, 0))],
        out_specs=[_tile(tm, W)] * 4 + [_tile(tm, D)],
        out_shape=[jax.ShapeDtypeStruct((L, W), BF16)] * 3 + [jax.ShapeDtypeStruct((L, W), F32),
                                                              jax.ShapeDtypeStruct((L, D), BF16)],
        compiler_params=_params())(x, prm, wing)


def _even_in_bwd(dout, x, dq, dk, dv, du, prm, wing, name, tm=512):
    L, D = x.shape
    W = wing.shape[-1]
    tm = min(tm, L)

    def body(do_ref, x_ref, dq_ref, dk_ref, dv_ref, du_ref, p_ref, w_ref, dx_ref, ds_ref, acc_ref):
        i = pl.program_id(0)
        gain, shift, scale = p_ref[3:4, :], p_ref[0:1, :], p_ref[1:2, :]
        _, xhat, r, nrm = _nm(x_ref[...], gain, shift, scale)
        dhn = jnp.zeros((tm, D), F32)
        for s, ref in enumerate((dq_ref, dk_ref, dv_ref, du_ref)):
            d = ref[...].astype(BF16)
            ds_ref[:, s * W:(s + 1) * W] = d
            dhn = dhn + _dot_nt(d, w_ref[s])
        dx, dshift, dscale, dgn = _nm_bwd(dhn, xhat, r, nrm, gain, scale)
        dx_ref[...] = do_ref[...] + dx
        _acc_rows(acc_ref, i == 0, [dshift, dscale, None, dgn])

    return pl.pallas_call(
        body, name=name, grid=(L // tm,),
        in_specs=[_tile(tm, D), _tile(tm, D)] + [_tile(tm, W)] * 4 +
                 [_full((8, D)), _resident((None, 4, D, W), lambda i: (0, 0, 0, 0))],
        out_specs=[_tile(tm, D), _tile(tm, 4 * W), _full((8, D))],
        out_shape=[jax.ShapeDtypeStruct((L, D), F32), jax.ShapeDtypeStruct((L, 4 * W), BF16),
                   jax.ShapeDtypeStruct((8, D), F32)],
        compiler_params=_params())(dout, x, dq, dk, dv, du, prm, wing)


def _even_out_fwd(x, att, pool, prm, woutg, name, tm=512):
    L, D = x.shape
    W = D // 2
    tm = min(tm, L)

    def body(x_ref, a_ref, p_ref, prm_ref, w_ref, xo_ref, y_ref):
        yv = _dot(a_ref[...], w_ref[0:W, :]) + _dot(p_ref[...], w_ref[W:2 * W, :])
        y_ref[...] = yv.astype(BF16)
        xo_ref[...] = x_ref[...] + prm_ref[2:3, :] * yv

    return pl.pallas_call(
        body, name=name, grid=(L // tm,),
        in_specs=[_tile(tm, D), _tile(tm, W), _tile(tm, W), _full((8, D)),
                  _resident((None, D, D), lambda i: (0, 0, 0))],
        out_specs=[_tile(tm, D), _tile(tm, D)],
        out_shape=[jax.ShapeDtypeStruct((L, D), F32), jax.ShapeDtypeStruct((L, D), BF16)],
        compiler_params=_params())(x, att, pool, prm, woutg)


def _even_out_bwd(dout, y, prm, woutg, name, tm=512):
    L, D = dout.shape
    W = D // 2
    tm = min(tm, L)

    def body(do_ref, y_ref, p_ref, w_ref, dy_ref, da_ref, dp_ref, acc_ref):
        i = pl.program_id(0)
        do = do_ref[...]
        dgate = _sum0(do * y_ref[...].astype(F32))
        dyb = (p_ref[2:3, :] * do).astype(BF16)
        dy_ref[...] = dyb
        da_ref[...] = _dot_nt(dyb, w_ref[0:W, :]).astype(BF16)
        dp_ref[...] = _dot_nt(dyb, w_ref[W:2 * W, :])
        _acc_rows(acc_ref, i == 0, [None, None, dgate])

    return pl.pallas_call(
        body, name=name, grid=(L // tm,),
        in_specs=[_tile(tm, D), _tile(tm, D), _full((8, D)), _resident((None, D, D), lambda i: (0, 0, 0))],
        out_specs=[_tile(tm, D), _tile(tm, W), _tile(tm, W), _full((8, D))],
        out_shape=[jax.ShapeDtypeStruct((L, D), BF16), jax.ShapeDtypeStruct((L, W), BF16),
                   jax.ShapeDtypeStruct((L, W), F32), jax.ShapeDtypeStruct((8, D), F32)],
        compiler_params=_params())(dout, y, prm, woutg)


def _group_ri(variant, qr, kr):
    first_key = (0, qr, GK - NA_KH)[variant]
    if not first_key <= kr < first_key + NA_KH:
        return None
    return kr - qr + (NA_KH - 1, NA_KH - 1 - NA_KH // 2, NA_KH - 1 - (GK - GQ))[variant]


HEADS_PER_BLOCK = 128 // NA_HEAD_DIM


def _bias_table(rpb, name):
    H = rpb.shape[0]
    hpb = HEADS_PER_BLOCK
    nri, nci = 2 * NA_KH - 1, 2 * NA_KW - 1
    col = jnp.arange(GRID_W)
    rel = (col[None, :] - col[:, None] + (NA_KW - 1)).reshape(1, -1)
    onehot = (rel == jnp.arange(32)[:, None]).astype(F32)
    cs = jnp.clip(col - NA_KW // 2, 0, GRID_W - NA_KW)
    ok = ((col[None, :] >= cs[:, None]) & (col[None, :] < cs[:, None] + NA_KW)).astype(F32).reshape(1, -1)
    by_lane_block = rpb.reshape(H // hpb, hpb, nri, nci).transpose(1, 0, 2, 3)
    rpb2 = jnp.pad(by_lane_block.reshape(H * nri, nci), ((0, 0), (0, 32 - nci)))

    def body(r_ref, e_ref, m_ref, o_ref):
        t = jnp.dot(r_ref[...], e_ref[...], preferred_element_type=F32, precision=lax.Precision.HIGHEST)
        o_ref[...] = jnp.where(m_ref[...] > 0.0, t, NEG_INF)

    tab = pl.pallas_call(body, name=name, out_shape=jax.ShapeDtypeStruct((H * nri, GRID_W * GRID_W), F32))(
        rpb2, onehot, ok)
    tab = tab.reshape(hpb, H // hpb, nri, GRID_W, GRID_W)
    outside = jnp.full((H // hpb, GRID_W, GRID_W), NEG_INF, F32)
    variants = []
    for variant in range(3):
        rows = []
        for h in range(hpb):
            for qr in range(GQ):
                ris = [_group_ri(variant, qr, kr) for kr in range(GK)]
                rows.append(jnp.concatenate([outside if ri is None else tab[h, :, ri] for ri in ris], axis=2))
        variants.append(jnp.concatenate(rows, axis=1))
    return jnp.stack(variants, axis=1)


def _attn_probs(q, kw, kc, bias, scale):
    s_w = _dot_nt(q, kw) * scale + bias
    s_c = _dot_nt(q, kc) * scale
    m = jnp.maximum(jnp.max(s_w, axis=-1, keepdims=True), jnp.max(s_c, axis=-1, keepdims=True))
    e_w = jnp.exp(s_w - m)
    e_c = jnp.exp(s_c - m)
    inv = 1.0 / (jnp.sum(e_w, axis=-1, keepdims=True) + jnp.sum(e_c, axis=-1, keepdims=True))
    return e_w * inv, e_c * inv


def _group_place(g, R):
    G = R // GQ
    kb = jnp.clip(g * GQ - NA_KH // 2, 0, R - GK)
    variant = jnp.where(g == 0, 0, jnp.where(g == G - 1, 2, 1))
    return pl.multiple_of(g * (GQ * GRID_W), GQ * GRID_W), pl.multiple_of(kb * GRID_W, GRID_W), variant


def _lane_masks(width, dh):
    lane = lax.broadcasted_iota(jnp.int32, (1, width), 1)
    return [(lane >= h * dh) & (lane < (h + 1) * dh) for h in range(width // dh)]


def _only(mask, a):
    return jnp.where(mask, a, jnp.zeros_like(a))


def _attn_fwd(q, k, v, kc, vc, bias, name, ride=None):
    L, width = q.shape
    C = kc.shape[0]
    dh = NA_HEAD_DIM
    lanes = 128
    hpb = lanes // dh
    R = L // GRID_W
    nq, nk = GQ * GRID_W, GK * GRID_W
    scale = dh ** -0.5

    def body(q_ref, k_ref, v_ref, kc_ref, vc_ref, b_ref, o_ref):
        masks = _lane_masks(lanes, dh)
        kc2 = kc_ref[...]
        vcs = [_only(m, vc_ref[...]) for m in masks]

        def group(g, carry):
            q0, k0, variant = _group_place(g, R)
            q2 = q_ref[pl.ds(q0, nq), :]
            k2 = k_ref[pl.ds(k0, nk), :]
            v2 = v_ref[pl.ds(k0, nk), :]
            qs = jnp.concatenate([_only(m, q2) for m in masks], axis=0)
            p_w, p_c = _attn_probs(qs, k2, kc2, b_ref[variant], scale)
            p_w, p_c = p_w.astype(BF16), p_c.astype(BF16)
            o2 = jnp.zeros((nq, lanes), F32)
            for h, m in enumerate(masks):
                rows = slice(h * nq, (h + 1) * nq)
                o2 = o2 + _dot(p_w[rows], _only(m, v2)) + _dot(p_c[rows], vcs[h])
            o_ref[pl.ds(q0, nq), :] = o2.astype(BF16)
            return carry

        lax.fori_loop(0, R // GQ, group, 0)

    cols = lambda n: pl.BlockSpec((n, lanes), lambda p: (0, p))
    res, carried = _ride_call(
        body, ride, (q, k, v, kc, vc, bias), name=name, grid=(width // lanes,),
        in_specs=[cols(L), cols(L), cols(L), cols(C), cols(C),
                  pl.BlockSpec((None, 3, hpb * nq, nk), lambda p: (p, 0, 0, 0))],
        out_specs=[cols(L)], out_shape=[jax.ShapeDtypeStruct((L, width), BF16)],
        compiler_params=_params())
    return res[0], carried


def _attn_bwd(q, k, v, kc, vc, bias, do, name, ride=None):
    L, width = q.shape
    C = kc.shape[0]
    dh = NA_HEAD_DIM
    lanes = 128
    hpb = lanes // dh
    R = L // GRID_W
    nq, nk = GQ * GRID_W, GK * GRID_W
    scale = dh ** -0.5

    def body(q_ref, k_ref, v_ref, kc_ref, vc_ref, b_ref, do_ref, dq_ref, dk_ref, dv_ref, dkc_ref, dvc_ref, db_ref):
        masks = _lane_masks(lanes, dh)
        kc2 = kc_ref[...]
        vc2 = vc_ref[...]
        kcs = [_only(m, kc2) for m in masks]
        dk_ref[...] = jnp.zeros((L, lanes), F32)
        dv_ref[...] = jnp.zeros((L, lanes), F32)
        dkc_ref[...] = jnp.zeros((C, lanes), F32)
        dvc_ref[...] = jnp.zeros((C, lanes), F32)
        db_ref[...] = jnp.zeros((3, hpb * nq, nk), F32)

        def group(g, carry):
            q0, k0, variant = _group_place(g, R)
            q2 = q_ref[pl.ds(q0, nq), :]
            k2 = k_ref[pl.ds(k0, nk), :]
            v2 = v_ref[pl.ds(k0, nk), :]
            do2 = do_ref[pl.ds(q0, nq), :]
            qs = jnp.concatenate([_only(m, q2) for m in masks], axis=0)
            dos = jnp.concatenate([_only(m, do2) for m in masks], axis=0)
            p_w, p_c = _attn_probs(qs, k2, kc2, b_ref[variant], scale)
            dp_w = _dot_nt(dos, v2)
            dp_c = _dot_nt(dos, vc2)
            delta = jnp.sum(p_w * dp_w, axis=-1, keepdims=True) + jnp.sum(p_c * dp_c, axis=-1, keepdims=True)
            ds_w = p_w * (dp_w - delta)
            ds_c = p_c * (dp_c - delta)
            db_ref[variant] += ds_w
            dsw = (ds_w * scale).astype(BF16)
            dsc = (ds_c * scale).astype(BF16)
            dq2 = jnp.zeros((nq, lanes), F32)
            for h, m in enumerate(masks):
                rows = slice(h * nq, (h + 1) * nq)
                dq2 = dq2 + _dot(dsw[rows], _only(m, k2)) + _dot(dsc[rows], kcs[h])
            dq_ref[pl.ds(q0, nq), :] = dq2.astype(BF16)
            dk_ref[pl.ds(k0, nk), :] += _dot_tn(dsw, qs)
            dv_ref[pl.ds(k0, nk), :] += _dot_tn(p_w.astype(BF16), dos)
            dkc_ref[...] += _dot_tn(dsc, qs)
            dvc_ref[...] += _dot_tn(p_c.astype(BF16), dos)
            return carry

        lax.fori_loop(0, R // GQ, group, 0)

    cols = lambda n: _resident((n, lanes), lambda p: (0, p))
    bspec = _resident((None, 3, hpb * nq, nk), lambda p: (p, 0, 0, 0))
    res, carried = _ride_call(
        body, ride, (q, k, v, kc, vc, bias, do), name=name, grid=(width // lanes,),
        in_specs=[cols(L), cols(L), cols(L), cols(C), cols(C), bspec, cols(L)],
        out_specs=[cols(L), cols(L), cols(L), cols(C), cols(C), bspec],
        out_shape=[jax.ShapeDtypeStruct((L, width), BF16)] + [jax.ShapeDtypeStruct((L, width), F32)] * 2 +
                  [jax.ShapeDtypeStruct((C, width), F32)] * 2 +
                  [jax.ShapeDtypeStruct((width // lanes, 3, hpb * nq, nk), F32)],
        compiler_params=_params())
    return (*res, carried)


def _rpb_grad(dbias, name):
    hpb = HEADS_PER_BLOCK
    H = dbias.shape[0] * hpb
    nri, nci = 2 * NA_KH - 1, 2 * NA_KW - 1
    d6 = dbias.reshape(H // hpb, 3, hpb, GQ, GRID_W, GK, GRID_W).transpose(0, 2, 1, 3, 5, 4, 6)
    d6 = d6.reshape(H, 3, GQ, GK, GRID_W, GRID_W)
    col = jnp.arange(GRID_W)
    onehot = (col[None, None, :] - col[None, :, None] + (NA_KW - 1) == jnp.arange(32)[:, None, None]).astype(F32)
    places = [(v, qr, kr) for v in range(3) for qr in range(GQ) for kr in range(GK)]

    def body(d_ref, m_ref, o_ref, t_ref):
        t_ref[...] = jnp.zeros((32, GRID_W), F32)
        o_ref[...] = jnp.zeros((16, 32, 128), F32)
        for ri in range(nri):
            a = None
            for place in places:
                if _group_ri(*place) == ri:
                    blk = d_ref[place]
                    a = blk if a is None else a + blk
            for ci in range(nci):
                t_ref[ci:ci + 1, :] = _sum0(a * m_ref[ci])
            o_ref[ri] = jnp.broadcast_to(jnp.sum(t_ref[...], axis=1, keepdims=True), (32, 128))

    out = pl.pallas_call(
        body, name=name, grid=(H,),
        in_specs=[pl.BlockSpec((None, 3, GQ, GK, GRID_W, GRID_W), lambda h: (h, 0, 0, 0, 0, 0)),
                  pl.BlockSpec((32, GRID_W, GRID_W), lambda h: (0, 0, 0))],
        out_specs=pl.BlockSpec((None, 16, 32, 128), lambda h: (h, 0, 0, 0)),
        out_shape=jax.ShapeDtypeStruct((H, 16, 32, 128), F32),
        scratch_shapes=[pltpu.VMEM((32, GRID_W), F32)])(d6, onehot)
    return out[:, :nri, :nci, 0]


def _window_count(t, w, L):
    lo = jnp.clip(t - w // 2, 0, L)
    hi = jnp.clip(t - w // 2 + w, 0, L)
    return jnp.maximum(hi - lo, 1).astype(F32)


def _running_sum(v, w):
    k = 1
    while k < w:
        v = v + _shift_rows(v, k)
        k *= 2
    return v


def _pool_fwd(u, poolw, pscale, name, tm=512):
    L, W = u.shape
    G = POOL_GROUP_DIM
    tm = min(tm, L)
    nt = L // tm

    def body(c_ref, p_ref, n_ref, w_ref, s_ref, o_ref, dm_ref):
        i = pl.program_id(0)
        ext = _ext(p_ref[...], c_ref[...], n_ref[...], i, nt)
        t = i * tm + lax.broadcasted_iota(jnp.int32, (tm, 1), 0)
        for g, w in enumerate(POOL_WINDOWS):
            e = ext[:, g * G:(g + 1) * G]
            win = _shift_rows(_running_sum(e, w), -(w // 2 - 1))[HALO:HALO + tm]
            dmx = (win / _window_count(t, w, L) - e[HALO:HALO + tm]).astype(BF16)
            dm_ref[:, g * G:(g + 1) * G] = dmx
            o_ref[:, g * G:(g + 1) * G] = (_dot(dmx, w_ref[g]) * s_ref[:, g * G:(g + 1) * G]).astype(BF16)

    return pl.pallas_call(
        body, name=name, grid=(nt,),
        in_specs=[_tile(tm, W), _halo_prev(tm, W), _halo_next(tm, W, L), _full((4, G, G)), _full((1, W))],
        out_specs=[_tile(tm, W), _tile(tm, W)],
        out_shape=[jax.ShapeDtypeStruct((L, W), BF16)] * 2, compiler_params=_params())(u, u, u, poolw, pscale)


def _pool_bwd(dpool, dmx, poolw, pscale, name, tm=512):
    L, W = dpool.shape
    G = POOL_GROUP_DIM
    tm = min(tm, L)
    nt = L // tm

    def body(c_ref, p_ref, n_ref, dm_ref, w_ref, s_ref, du_ref, dw_ref, acc_ref):
        i = pl.program_id(0)
        ext = _ext(p_ref[...], c_ref[...], n_ref[...], i, nt)
        te = i * tm - HALO + lax.broadcasted_iota(jnp.int32, (tm + 2 * HALO, 1), 0)

        @pl.when(i == 0)
        def _():
            dw_ref[...] = jnp.zeros((4 * G, G), F32)

        rows = []
        for g, w in enumerate(POOL_WINDOWS):
            sc = s_ref[:, g * G:(g + 1) * G]
            dpre = (ext[:, g * G:(g + 1) * G] * sc).astype(BF16)
            dd = _dot_nt(dpre, w_ref[g])
            spread = _shift_rows(_running_sum(dd / _window_count(te, w, L), w), -(w // 2))
            du_ref[:, g * G:(g + 1) * G] = (spread - dd)[HALO:HALO + tm]
            dmx_g = dm_ref[:, g * G:(g + 1) * G]
            rows.append(_sum0(c_ref[:, g * G:(g + 1) * G] * _dot(dmx_g, w_ref[g])))
            dw_ref[g * G:(g + 1) * G, :] += _dot_tn(dmx_g, dpre[HALO:HALO + tm])
        _acc_rows(acc_ref, i == 0, [jnp.concatenate(rows, axis=1)])

    return pl.pallas_call(
        body, name=name, grid=(nt,),
        in_specs=[_tile(tm, W), _halo_prev(tm, W), _halo_next(tm, W, L), _tile(tm, W), _full((4, G, G)),
                  _full((1, W))],
        out_specs=[_tile(tm, W), _full((4 * G, G)), _full((8, W))],
        out_shape=[jax.ShapeDtypeStruct((L, W), F32), jax.ShapeDtypeStruct((4 * G, G), F32),
                   jax.ShapeDtypeStruct((8, W), F32)],
        compiler_params=_params())(dpool, dpool, dpool, dmx, poolw, pscale)


def _conv3(z, cw):
    return _shift_rows(z, 1) * cw[0] + z * cw[1] + _shift_rows(z, -1) * cw[2]


def _conv_fwd(x, prm, wing, woutg, name, tm=512):
    L, D = x.shape
    Ws = wing.shape[-1]
    tm = min(tm, L)
    nt = L // tm
    te = tm + 2 * HALO

    def body(c_ref, p_ref, n_ref, prm_ref, wi_ref, wo_ref, xo_ref, y_ref, b_ref):
        i = pl.program_id(0)
        xe = jnp.concatenate([p_ref[...], c_ref[...], n_ref[...]], axis=0)
        hn, _, _, _ = _nm(xe, prm_ref[3:4, :], prm_ref[0:1, :], prm_ref[1:2, :])
        hb = hn.astype(BF16)
        proj = jnp.concatenate([_dot(hb, wi_ref[s]) for s in range(4)], axis=1)
        bg, cg, xin = proj[:, :D], proj[:, D:2 * D], proj[:, 2 * D:]
        tpos = i * tm - HALO + lax.broadcasted_iota(jnp.int32, (te, 1), 0)
        valid = ((tpos >= 0) & (tpos < L)).astype(F32)
        yc = _conv3(cg * xin * valid, [prm_ref[4 + k:5 + k, :] for k in range(3)])
        h2 = (bg * yc)[HALO:HALO + tm].astype(BF16)
        yv = _dot(h2, wo_ref[...])
        y_ref[...] = yv.astype(BF16)
        xo_ref[...] = c_ref[...] + prm_ref[2:3, :] * yv
        b_ref[...] = proj[HALO:HALO + tm].astype(BF16)

    return pl.pallas_call(
        body, name=name, grid=(nt,),
        in_specs=[_tile(tm, D), _halo_prev(tm, D), _halo_next(tm, D, L), _full((8, D)),
                  _resident((None, 4, D, Ws), lambda i: (0, 0, 0, 0)),
                  _resident((None, D, D), lambda i: (0, 0, 0))],
        out_specs=[_tile(tm, D), _tile(tm, D), _tile(tm, 3 * D)],
        out_shape=[jax.ShapeDtypeStruct((L, D), F32), jax.ShapeDtypeStruct((L, D), BF16),
                   jax.ShapeDtypeStruct((L, 3 * D), BF16)],
        compiler_params=_params())(x, x, x, prm, wing, woutg)


def _conv_bwd(dout, x, y, bcx, prm, wing, woutg, name, tm=256, ride=None):
    L, D = x.shape
    Ws = wing.shape[-1]
    tm = min(tm, L)
    nt = L // tm
    te = tm + 2 * HALO

    def body(dc_ref, dp_ref, dn_ref, x_ref, y_ref, bc_ref, bp_ref, bn_ref, prm_ref, wi_ref, wo_ref,
             dx_ref, dpr_ref, h2_ref, dy_ref, hn_ref, acc_ref):
        i = pl.program_id(0)
        gain, shift, scale, gate = prm_ref[3:4, :], prm_ref[0:1, :], prm_ref[1:2, :], prm_ref[2:3, :]
        taps = [prm_ref[4 + k:5 + k, :] for k in range(3)]
        do = dc_ref[...]
        doe = _ext(dp_ref[...], do, dn_ref[...], i, nt)
        dye = (gate * doe).astype(BF16)
        dy_ref[...] = dye[HALO:HALO + tm]
        dh2 = _dot_nt(dye, wo_ref[...])
        be = jnp.concatenate([bp_ref[...], bc_ref[...], bn_ref[...]], axis=0).astype(F32)
        bg, cg, xin = be[:, :D], be[:, D:2 * D], be[:, 2 * D:]
        tpos = i * tm - HALO + lax.broadcasted_iota(jnp.int32, (te, 1), 0)
        valid = ((tpos >= 0) & (tpos < L)).astype(F32)
        z = cg * xin * valid
        yc = _conv3(z, taps)
        dyc = dh2 * bg
        h2_ref[...] = (bg * yc)[HALO:HALO + tm].astype(BF16)
        dz = _conv3(dyc, taps[::-1]) * valid
        dproj = jnp.concatenate([dh2 * yc, dz * xin, dz * cg], axis=1)[HALO:HALO + tm].astype(BF16)
        dpr_ref[...] = dproj
        dhn = jnp.zeros((tm, D), F32)
        for s in range(4):
            dhn = dhn + _dot_nt(dproj[:, s * Ws:(s + 1) * Ws], wi_ref[s])
        hn, xhat, r, nrm = _nm(x_ref[...], gain, shift, scale)
        hn_ref[...] = hn.astype(BF16)
        dx, dshift, dscale, dgn = _nm_bwd(dhn, xhat, r, nrm, gain, scale)
        dx_ref[...] = do + dx
        dgate = _sum0(do * y_ref[...].astype(F32))
        dtaps = [_sum0((dyc * _shift_rows(z, 1 - k))[HALO:HALO + tm]) for k in range(3)]
        _acc_rows(acc_ref, i == 0, [dshift, dscale, dgate, dgn] + dtaps)

    res, carried = _ride_call(
        body, ride, (dout, dout, dout, x, y, bcx, bcx, bcx, prm, wing, woutg), name=name, grid=(nt,),
        in_specs=[_tile(tm, D), _halo_prev(tm, D), _halo_next(tm, D, L), _tile(tm, D), _tile(tm, D),
                  _tile(tm, 3 * D), _halo_prev(tm, 3 * D), _halo_next(tm, 3 * D, L), _full((8, D)),
                  _resident((None, 4, D, Ws), lambda i: (0, 0, 0, 0)),
                  _resident((None, D, D), lambda i: (0, 0, 0))],
        out_specs=[_tile(tm, D), _tile(tm, 3 * D), _tile(tm, D), _tile(tm, D), _tile(tm, D), _full((8, D))],
        out_shape=[jax.ShapeDtypeStruct((L, D), F32), jax.ShapeDtypeStruct((L, 3 * D), BF16),
                   jax.ShapeDtypeStruct((L, D), BF16), jax.ShapeDtypeStruct((L, D), BF16),
                   jax.ShapeDtypeStruct((L, D), BF16), jax.ShapeDtypeStruct((8, D), F32)],
        compiler_params=_params())
    return (*res, carried)


def _mod_fwd(cond, mod_w, mod_b, name, tn=768):
    nl, D, N = mod_w.shape
    tn = min(tn, N)

    def body(c_ref, w_ref, b_ref, o_ref):
        cv = c_ref[...]
        s = (cv * _sigmoid(cv)).astype(BF16)
        o_ref[...] = _dot(s, w_ref[...].astype(BF16)) + b_ref[...]

    return pl.pallas_call(
        body, name=name, grid=(nl, N // tn),
        in_specs=[pl.BlockSpec((16, D), lambda l, j: (0, 0)), pl.BlockSpec((None, D, tn), lambda l, j: (l, 0, j)),
                  pl.BlockSpec((None, 1, tn), lambda l, j: (l, 0, j))],
        out_specs=pl.BlockSpec((None, 16, tn), lambda l, j: (l, 0, j)),
        out_shape=jax.ShapeDtypeStruct((nl, 16, N), F32), compiler_params=_params())(cond, mod_w, mod_b)


def _mod_bwd(cond, dm, mod_w, name, tn=768):
    nl, D, N = mod_w.shape
    tn = min(tn, N)

    def body(c_ref, d_ref, w_ref, dw_ref, dc_ref):
        first = (pl.program_id(0) == 0) & (pl.program_id(1) == 0)
        cv = c_ref[...]
        s = (cv * _sigmoid(cv)).astype(BF16)
        d = d_ref[...].astype(BF16)
        dw_ref[...] = _dot_tn(s, d)

        @pl.when(first)
        def _():
            dc_ref[...] = jnp.zeros((16, D), F32)

        dc_ref[...] += _dot_nt(d, w_ref[...].astype(BF16))

    return pl.pallas_call(
        body, name=name, grid=(nl, N // tn),
        in_specs=[pl.BlockSpec((16, D), lambda l, j: (0, 0)), pl.BlockSpec((None, 16, tn), lambda l, j: (l, 0, j)),
                  pl.BlockSpec((None, D, tn), lambda l, j: (l, 0, j))],
        out_specs=[pl.BlockSpec((None, D, tn), lambda l, j: (l, 0, j)), pl.BlockSpec((16, D), lambda l, j: (0, 0))],
        out_shape=[jax.ShapeDtypeStruct((nl, D, N), F32), jax.ShapeDtypeStruct((16, D), F32)],
        compiler_params=_params())(cond, dm, mod_w)


def _mod_small_grads(dm_all, cond, dsilu_parts, name):
    nl, _, N = dm_all.shape
    D = cond.shape[1]

    def body(d_ref, c_ref, p_ref, db_ref, dc_ref):
        for l in range(nl):
            db_ref[l] = _sum0(d_ref[l])
        tot = p_ref[0, 8:9, :]
        for k in range(1, N_CHIPS):
            tot = tot + p_ref[2 * k, 8:9, :]
        cv = c_ref[8:9, :]
        sg = _sigmoid(cv)
        dc_ref[...] = tot * (sg * (1.0 + cv * (1.0 - sg)))

    return pl.pallas_call(
        body, name=name, out_shape=[jax.ShapeDtypeStruct((nl, 1, N), F32), jax.ShapeDtypeStruct((1, D), F32)],
    )(dm_all, cond, dsilu_parts)


def _prm(rows, D):
    rows = [r.reshape(1, D) for r in rows]
    return jnp.concatenate(rows + [jnp.zeros((8 - len(rows), D), F32)], axis=0)


def kernel(x, c, ctx, c_ctx, mod_w, mod_b, norm_g, ffn_w13, ffn_w2, even_w_in, even_w_out, na_rpb, pool_w, pool_scale, conv_w_in, conv_w, conv_w_out, final_g, loss_target, m_c_ctx, m_mod_w, m_mod_b, m_norm_g, m_ffn_w13, m_ffn_w2, m_even_w_in, m_even_w_out, m_na_rpb, m_pool_w, m_pool_scale, m_conv_w_in, m_conv_w, m_conv_w_out, m_final_g, v_c_ctx, v_mod_w, v_mod_b, v_norm_g, v_ffn_w13, v_ffn_w2, v_even_w_in, v_even_w_out, v_na_rpb, v_pool_w, v_pool_scale, v_conv_w_in, v_conv_w, v_conv_w_out, v_final_g):
    xi, yi, ci = lax.axis_index("x"), lax.axis_index("y"), lax.axis_index("c")
    chip = 2 * xi + yi
    dev = 4 * xi + 2 * yi + ci
    _, L, D = x.shape
    C = ctx.shape[1]
    Ds = D // N_CHIPS
    Nm = mod_w.shape[-1]
    Fh = ffn_w13.shape[-1]
    Fq = ffn_w2.shape[2]
    assert ffn_w13.shape[:2] == (2, 2) and Fh == 2 * Fq and L % (GQ * GRID_W) == 0 and L // GRID_W >= GK and GQ == NA_KH // 2
    x0, ctx0, tgt = x[0], ctx[0], loss_target[0]

    pad = lambda a: jnp.pad(a, ((0, 0), (0, D - a.shape[1])))
    pack1 = jnp.concatenate([c, pad(norm_g.reshape(6, Ds)), pad(conv_w.reshape(3, Ds)), jnp.zeros((6, D), F32)], axis=0)
    g1 = _small_all_gather(pack1, "ag_cond")
    cond = jnp.concatenate([g1[:, 0], c_ctx[None], jnp.zeros((7, D), F32)], axis=0)
    norm_full = jnp.concatenate([g1[2 * k, 1:7, :Ds] for k in range(N_CHIPS)], axis=1).reshape(2, 3, D)
    convw_full = jnp.concatenate([g1[2 * k, 7:10, :Ds] for k in range(N_CHIPS)], axis=1)

    mod_b_loc = lax.dynamic_slice_in_dim(mod_b, chip * Nm, Nm, axis=1).reshape(2, 1, Nm)
    m_loc = _mod_fwd(cond, mod_w, mod_b_loc, "mod_fwd")
    g2 = _small_all_gather(m_loc.reshape(32, Nm), "ag_mod")
    m_all = jnp.concatenate([g2[2 * k] for k in range(N_CHIPS)], axis=1).reshape(2, 16, N_MOD, D)
    m_lat = lax.dynamic_index_in_dim(m_all, dev, axis=1, keepdims=False)
    m_ctx = m_all[:, 8]

    def prm(mods, layer, base, gain_idx, extra=()):
        return _prm([mods[layer, base], mods[layer, base + 1], mods[layer, base + 2], norm_full[layer, gain_idx],
                     *extra], D)

    def shard_bf16(w, name):
        return _cast_bf16(w.reshape(-1, w.shape[-1]), name).reshape(-1, *w.shape[-2:])

    w13s, w2s = shard_bf16(ffn_w13, "cast_w13"), shard_bf16(ffn_w2, "cast_w2")
    eins, eouts = shard_bf16(even_w_in, "cast_ein"), shard_bf16(even_w_out, "cast_eout")
    cins, couts = shard_bf16(conv_w_in, "cast_cin"), shard_bf16(conv_w_out, "cast_cout")
    ffn_shards = [[w13s[t:t + 1], w2s[t:t + 1]] for t in range(4)]

    def ffn_weights(w13g, w2g):
        return w13g.reshape(1, 4, D, Fh), w2g

    wf = [ffn_weights(*_gather_shards(ffn_shards[0], "ag_ffn0")), None, None, None]
    pos = jnp.stack([chip, ci]).astype(jnp.int32)

    p_f1 = prm(m_lat, 0, 0, 0)
    p_mx = prm(m_lat, 0, 3, 1)
    p_f2 = prm(m_lat, 0, 6, 2)
    p_g1 = prm(m_lat, 1, 0, 0)
    p_cv = prm(m_lat, 1, 3, 1, extra=(convw_full[0], convw_full[1], convw_full[2]))
    p_g2 = prm(m_lat, 1, 6, 2)
    pc_f1 = prm(m_ctx, 0, 0, 0)
    pc_mx = prm(m_ctx, 0, 3, 1)

    x1, ab1, y1, (eing, eoutg) = _ffn_fwd(x0, p_f1, *wf[0], 0, "ffn_fwd_l0a", ride=_broadcast_ride([eins, eouts]))
    eing = eing.reshape(1, 4, D, NA_WIDTH)
    ctx1, abc, yc, _ = _ffn_fwd(ctx0, pc_f1, *wf[0], 0, "ffn_fwd_ctx")
    q, k, v, u, hn_mx = _even_in_fwd(x1, p_mx, eing, "even_in_fwd")
    _, k_c, v_c, _, hn_cx = _even_in_fwd(ctx1, pc_mx, eing, "even_in_ctx")
    bias = _bias_table(na_rpb[0], "bias_table")
    att, gathered = _attn_fwd(q, k, v, k_c, v_c, bias, "attn_fwd", ride=_broadcast_ride(ffn_shards[1]))
    wf[1] = ffn_weights(*gathered)
    pw_b = _cast_bf16(pool_w.reshape(-1, POOL_GROUP_DIM), "cast_poolw").reshape(4, POOL_GROUP_DIM, POOL_GROUP_DIM)
    pool, dmx = _pool_fwd(u, pw_b, pool_scale, "pool_fwd")
    x2, ymx = _even_out_fwd(x1, att, pool, p_mx, eoutg, "even_out_fwd")
    x3, ab2, y2, gathered = _ffn_fwd(x2, p_f2, *wf[1], 0, "ffn_fwd_l0b",
                                     ride=_broadcast_ride(ffn_shards[2] + [cins, couts]))
    wf[2] = ffn_weights(*gathered[:2])
    cing, coutg = gathered[2].reshape(1, 4, D, conv_w_in.shape[-1]), gathered[3]
    x4, ab3, y3, gathered = _ffn_fwd(x3, p_g1, *wf[2], 0, "ffn_fwd_l1a", ride=_broadcast_ride(ffn_shards[3]))
    wf[3] = ffn_weights(*gathered)
    x5, ycv, bcx = _conv_fwd(x4, p_cv, cing, coutg, "conv_fwd")
    x6, ab4, y4, _ = _ffn_fwd(x5, p_g2, *wf[3], 0, "ffn_fwd_l1b")

    def ffn_back(dout, xin, ab, yy, p, t, tag, init13=None, init2=None, ride=None, head=None):
        sv, gact = ab
        dx, dab, dy, hn, acc, carried = _ffn_bwd(dout, xin, sv, yy, p, *wf[t], 0, f"ffn_bwd_{tag}", ride=ride,
                                                 head=head)
        dw13 = _mm_tn(hn, dab, 4, False, f"dw13_{tag}", init=init13)
        dw2 = _mm_tn(gact, dy, 2, True, f"dw2_{tag}", init=init2)
        return dx, acc, dw13, dw2, carried

    dx5, acc_g2, dw13_3, dw2_3, _ = ffn_back(x6, x5, ab4, y4, p_g2, 3, "l1b", head=(tgt, final_g.reshape(1, D)))
    acc_head = acc_g2[4:6]
    loss = lax.psum(acc_head[1, 0], ("x", "y", "c"))
    s_a, sb_a = _pair_sums([dw13_3, dw2_3], pos, "l1b")
    dx4, dproj, h2, dycv, hn_cv, acc_cv, got_a = _conv_bwd(dx5, x4, ycv, bcx, p_cv, cing, coutg, "conv_bwd",
                                                           ride=_scatter_ride(sb_a))
    dcin = _mm_tn(hn_cv, dproj, 4, False, "dw_cin")
    dcout = _mm_tn(h2, dycv, 1, False, "dw_cout")
    s_b, sb_b = _pair_sums([dcin, dcout], pos, "conv")
    dx3, acc_g1, dw13_2, dw2_2, got_b = ffn_back(dx4, x3, ab3, y3, p_g1, 2, "l1a", ride=_scatter_ride(sb_b))
    s_c, sb_c = _pair_sums([dw13_2, dw2_2], pos, "l1a")
    dx2, acc_f2, dw13_1, dw2_1, got_c = ffn_back(dx3, x2, ab2, y2, p_f2, 1, "l0b", ride=_scatter_ride(sb_c))

    dymx, datt, dpool, acc_mxo = _even_out_bwd(dx2, ymx, p_mx, eoutg, "even_out_bwd")
    deout = jnp.concatenate([_mm_tn(att, dymx, 1, False, "dw_eout_att"),
                             _mm_tn(pool, dymx, 1, False, "dw_eout_pool")], axis=0)
    s_d, sb_d = _pair_sums([dw13_1, dw2_1, deout], pos, "l0b")
    du, dpoolw, acc_pool = _pool_bwd(dpool, dmx, pw_b, pool_scale, "pool_bwd")
    dq, dk, dv, dkc, dvc, dbias, got_d = _attn_bwd(q, k, v, k_c, v_c, bias, datt, "attn_bwd",
                                                   ride=_scatter_ride(sb_d))
    drpb = _rpb_grad(dbias, "rpb_grad")
    dx1, dstack, acc_mxi = _even_in_bwd(dx2, x1, dq, dk, dv, du, p_mx, eing,
                                        "even_in_bwd")
    zc = jnp.zeros((C, NA_WIDTH), F32)
    dctx1, dstack_c, accc_mx = _even_in_bwd(jnp.zeros((C, D), F32), ctx1, zc, dkc, dvc, zc,
                                            pc_mx, eing, "even_in_bwd_ctx")
    dein_c = _mm_tn(hn_cx, dstack_c, 4, False, "dw_ein_ctx")
    dein = _mm_tn(hn_mx, dstack, 4, False, "dw_ein", init=dein_c)
    s_e, sb_e = _pair_sums([dein], pos, "ein")
    _, accc_f1, dw13_c, dw2_c, _ = ffn_back(dctx1, ctx0, abc, yc, pc_f1, 0, "ctx")
    sv1, gact1 = ab1
    dx0, dab, dy, hn, acc_f1, got_e = _ffn_bwd(dx1, x0, sv1, y1, p_f1, *wf[0], 0, "ffn_bwd_l0a",
                                               ride=_scatter_ride(sb_e))
    dw13_0 = _mm_tn(hn, dab, 4, False, "dw13_l0a", init=dw13_c)
    s_f13, sb_f13 = _pair_sums([dw13_0], pos, "l0a_w13")
    dw2_0, got_f13 = _mm_tn(gact1, dy, 2, True, "dw2_l0a", init=dw2_c, ride=_scatter_ride(sb_f13))
    s_f2, sb_f2 = _pair_sums([dw2_0], pos, "l0a_w2")

    z1 = jnp.zeros((1, D), F32)
    dm_lat = jnp.concatenate([acc_f1[0:3], acc_mxi[0:2], acc_mxo[2:3], acc_f2[0:3],
                              acc_g1[0:3], acc_cv[0:3], acc_g2[0:3]], axis=0)
    dm_ctx = jnp.concatenate([accc_f1[0:3], accc_mx[0:2]] + [z1] * 13, axis=0)
    dnorm = jnp.concatenate([acc_f1[3:4] + accc_f1[3:4], acc_mxi[3:4] + accc_mx[3:4], acc_f2[3:4],
                             acc_g1[3:4], acc_cv[3:4], acc_g2[3:4]], axis=0)
    rpb_flat = jnp.pad(drpb.reshape(-1), (0, 4 * D - drpb.size)).reshape(4, D)
    pack3 = jnp.concatenate([dm_lat, dm_ctx, dnorm, acc_cv[4:7], acc_head[0:1], pad(acc_pool[0:1]), z1,
                             dpoolw.reshape(-1, D), rpb_flat, jnp.zeros((4, D), F32)], axis=0)
    g3 = _small_all_gather(pack3, "ag_small")
    tot = _sum_devices(g3, "sum_small")
    dm_all = jnp.concatenate([g3[:, 0:18].reshape(8, 2, N_MOD * D).transpose(1, 0, 2),
                              tot[18:36].reshape(2, 1, N_MOD * D), jnp.zeros((2, 7, N_MOD * D), F32)], axis=1)
    dm_loc = lax.dynamic_slice_in_dim(dm_all, chip * Nm, Nm, axis=2)
    g_mod_w, dsilu = _mod_bwd(cond, dm_loc, mod_w, "mod_bwd")
    g4 = _small_all_gather(dsilu, "ag_dsilu")
    g_mod_b, g_c_ctx = _mod_small_grads(dm_all, cond, g4, "mod_small")
    g_mod_b = g_mod_b.reshape(2, N_MOD * D)
    g_c_ctx = g_c_ctx.reshape(D)
    g_norm_full = tot[36:42].reshape(2, 3, D)
    g_norm = lax.dynamic_slice_in_dim(g_norm_full, chip * Ds, Ds, axis=2)
    g_conv_w = lax.dynamic_slice_in_dim(tot[42:45], chip * Ds, Ds, axis=1).reshape(1, 3, Ds)
    g_final = tot[45]
    g_pscale = tot[46:47, :pool_scale.shape[1]]
    g_poolw = tot[48:112].reshape(pool_w.shape)
    g_rpb = tot[112:116].reshape(-1)[:na_rpb.size].reshape(na_rpb.shape)

    r13_3, r2_3 = _joins(s_a, got_a, pos, "l1b")
    r_cin, r_cout = _joins(s_b, got_b, pos, "conv")
    r13_2, r2_2 = _joins(s_c, got_c, pos, "l1a")
    r13_1, r2_1, r_eout = _joins(s_d, got_d, pos, "l0b")
    (r_ein,) = _joins(s_e, got_e, pos, "ein")
    adamw_mod_w, got_f2 = _adamw(mod_w, g_mod_w, m_mod_w, v_mod_w, "adamw_mod_w", ride=_scatter_ride(sb_f2))
    (r13_0,) = _joins(s_f13, got_f13, pos, "l0a_w13")
    (r2_0,) = _joins(s_f2, got_f2, pos, "l0a_w2")
    g_w13 = jnp.stack([r13_0, r13_1, r13_2, r13_3]).reshape(ffn_w13.shape)
    g_w2 = jnp.stack([r2_0, r2_1, r2_2, r2_3]).reshape(ffn_w2.shape)
    g_ein, g_eout, g_cin, g_cout = r_ein[None], r_eout[None], r_cin[None], r_cout[None]

    grads = [g_c_ctx, g_mod_w, g_mod_b, g_norm, g_w13, g_w2, g_ein, g_eout, g_rpb, g_poolw, g_pscale, g_cin,
             g_conv_w, g_cout, g_final]
    weights = [c_ctx, mod_w, mod_b, norm_g, ffn_w13, ffn_w2, even_w_in, even_w_out, na_rpb, pool_w, pool_scale,
               conv_w_in, conv_w, conv_w_out, final_g]
    ms = [m_c_ctx, m_mod_w, m_mod_b, m_norm_g, m_ffn_w13, m_ffn_w2, m_even_w_in, m_even_w_out, m_na_rpb, m_pool_w,
          m_pool_scale, m_conv_w_in, m_conv_w, m_conv_w_out, m_final_g]
    vs = [v_c_ctx, v_mod_w, v_mod_b, v_norm_g, v_ffn_w13, v_ffn_w2, v_even_w_in, v_even_w_out, v_na_rpb, v_pool_w,
          v_pool_scale, v_conv_w_in, v_conv_w, v_conv_w_out, v_final_g]
    names = ["c_ctx", "mod_w", "mod_b", "norm_g", "ffn_w13", "ffn_w2", "even_w_in", "even_w_out", "na_rpb", "pool_w",
             "pool_scale", "conv_w_in", "conv_w", "conv_w_out", "final_g"]
    straight_from_join = ("even_w_in", "even_w_out", "conv_w_in", "conv_w_out")
    deltas, new_m, new_v = [], [], []
    for i, (n, w, g, m, vv) in enumerate(zip(names, weights, grads, ms, vs)):
        g = g.reshape(w.shape)
        if n == "mod_w":
            d, mn, vn = adamw_mod_w
        elif w.ndim == 1:
            d, mn, vn = (t.reshape(w.shape) for t in _adamw(w[None], g[None], m[None], vv[None], f"adamw_{n}"))
        elif n in straight_from_join:
            d, mn, vn, g = _adamw(w, g, m, vv, f"adamw_{n}", regrad=True)
        else:
            d, mn, vn = _adamw(w, g, m, vv, f"adamw_{n}")
        grads[i] = g
        deltas.append(d)
        new_m.append(mn)
        new_v.append(vn)
    return (loss, dx0[None], *grads, *deltas, *new_m, *new_v)
```

```python
import jax
import jax.numpy as jnp
from jax import lax
from jax.experimental import pallas as pl
from jax.experimental.pallas import tpu as pltpu

F32 = jnp.float32
BF16 = jnp.bfloat16
MESH = pl.DeviceIdType.MESH

GRID_W = 64
NA_HEADS = 8
NA_HEAD_DIM = 64
NA_KH = 8
NA_KW = 16
GQ = 4
GK = GQ + NA_KH
NA_WIDTH = NA_HEADS * NA_HEAD_DIM
POOL_WINDOWS = (2, 4, 8, 16)
POOL_GROUP_DIM = 128
N_MOD = 9
RMS_EPS = 1e-6
NEG_INF = -1e30
ADAM_LR, ADAM_B1, ADAM_B2, ADAM_EPS, ADAM_WD, ADAM_STEP = 0.001, 0.9, 0.999, 1e-08, 0.01, 10

HALO = 16
VMEM_LIMIT = 56 * 1024 * 1024
N_CHIPS = 4
N_DEV = 8


def _dot(a, b):
    return jnp.dot(a, b, preferred_element_type=F32)


def _dot_nt(a, b):
    return lax.dot_general(a, b, (((1,), (1,)), ((), ())), preferred_element_type=F32)


def _dot_tn(a, b):
    return lax.dot_general(a, b, (((0,), (0,)), ((), ())), preferred_element_type=F32)


def _sigmoid(a):
    return 1.0 / (1.0 + jnp.exp(-a))


def _sum0(v):
    return jnp.sum(v, axis=0, keepdims=True)


def _nm(x, g, shift, scale):
    r = lax.rsqrt(jnp.mean(x * x, axis=-1, keepdims=True) + RMS_EPS)
    xhat = x * r
    nrm = xhat * g
    return nrm * (1.0 + scale) + shift, xhat, r, nrm


def _nm_bwd(dhn, xhat, r, nrm, g, scale):
    dshift = _sum0(dhn)
    dscale = _sum0(dhn * nrm)
    dnrm = dhn * (1.0 + scale)
    dgn = _sum0(dnrm * xhat)
    dxh = dnrm * g
    dx = r * (dxh - xhat * jnp.mean(dxh * xhat, axis=-1, keepdims=True))
    return dx, dshift, dscale, dgn


def _acc_rows(acc_ref, first, rows):
    @pl.when(first)
    def _():
        acc_ref[...] = jnp.zeros(acc_ref.shape, acc_ref.dtype)
    for k, row in enumerate(rows):
        if row is not None:
            acc_ref[k:k + 1, :] += row


def _shift_rows(v, k):
    n = v.shape[0]
    k = k % n
    return v if k == 0 else pltpu.roll(v, k, 0)


def _tile(tm, w):
    return pl.BlockSpec((tm, w), lambda i: (i, 0))


def _full(shape):
    nd = len(shape)
    return pl.BlockSpec(shape, lambda i: (0,) * nd)


def _resident(block, imap):
    return pl.BlockSpec(block, imap, pipeline_mode=pl.Buffered(1))


def _halo_prev(tm, w):
    return pl.BlockSpec((HALO, w), lambda i: (jnp.maximum(i * (tm // HALO) - 1, 0), 0))


def _halo_next(tm, w, L):
    return pl.BlockSpec((HALO, w), lambda i: (jnp.minimum((i + 1) * (tm // HALO), L // HALO - 1), 0))


def _params(vmem=VMEM_LIMIT):
    return pltpu.CompilerParams(vmem_limit_bytes=vmem)


def _pick_rows(rows, cols, itemsize=4, target=1 << 20):
    best = None
    for t in range(8, rows + 1, 8):
        if rows % t == 0 and t * cols * itemsize <= target:
            best = t
    return best if best is not None else rows


def _ext(prev, cur, nxt, i, nt):
    prev = jnp.where(i > 0, prev, jnp.zeros_like(prev))
    nxt = jnp.where(i < nt - 1, nxt, jnp.zeros_like(nxt))
    return jnp.concatenate([prev, cur, nxt], axis=0)


def _cast_bf16(a2d, name):
    rows, cols = a2d.shape
    tr = _pick_rows(rows, cols)

    def body(a_ref, o_ref):
        o_ref[...] = a_ref[...].astype(BF16)

    return pl.pallas_call(
        body, name=name, grid=(rows // tr,), in_specs=[_tile(tr, cols)], out_specs=_tile(tr, cols),
        out_shape=jax.ShapeDtypeStruct((rows, cols), BF16))(a2d)


def _sum_devices(g, name):
    n, rows, cols = g.shape
    tr = _pick_rows(rows, cols, target=1 << 18)

    def body(g_ref, o_ref):
        s = g_ref[0]
        for d in range(1, n):
            s = s + g_ref[d]
        o_ref[...] = s

    return pl.pallas_call(
        body, name=name, grid=(rows // tr,), in_specs=[pl.BlockSpec((n, tr, cols), lambda i: (0, i, 0))],
        out_specs=_tile(tr, cols), out_shape=jax.ShapeDtypeStruct((rows, cols), F32))(g)


def _adamw(w, g, m, v, name, ride=None):
    shape = w.shape
    cols = shape[-1]
    rows = w.size // cols
    w2, g2, m2, v2 = (t.reshape(rows, cols) for t in (w, g, m, v))
    tr = _pick_rows(rows, cols)
    c1 = 1.0 - ADAM_B1 ** ADAM_STEP
    c2 = 1.0 - ADAM_B2 ** ADAM_STEP

    def body(w_ref, g_ref, m_ref, v_ref, d_ref, mo_ref, vo_ref):
        gg = g_ref[...]
        mn = ADAM_B1 * m_ref[...] + (1.0 - ADAM_B1) * gg
        vn = ADAM_B2 * v_ref[...] + (1.0 - ADAM_B2) * (gg * gg)
        d_ref[...] = -ADAM_LR * ((mn / c1) / (jnp.sqrt(vn / c2) + ADAM_EPS) + ADAM_WD * w_ref[...])
        mo_ref[...] = mn
        vo_ref[...] = vn

    outs, carried = _ride_call(
        body, ride, (w2, g2, m2, v2), name=name, grid=(rows // tr,), in_specs=[_tile(tr, cols)] * 4,
        out_specs=[_tile(tr, cols)] * 3, out_shape=[jax.ShapeDtypeStruct((rows, cols), F32)] * 3)
    outs = tuple(o.reshape(shape) for o in outs)
    return outs if ride is None else (outs, carried)


def _mesh_pos():
    x, y, c = lax.axis_index("x"), lax.axis_index("y"), lax.axis_index("c")
    chips = [(1 - x, y), (x, 1 - y), (1 - x, 1 - y)]
    return x, y, c, chips


def _hbm_specs(n):
    return [pl.BlockSpec(memory_space=pltpu.HBM)] * n


def _small_all_gather(v, name):
    rows, w = v.shape

    def body(x_ref, out_ref, send_sems, recv_sems, local_sem):
        x, y, c, chips = _mesh_pos()
        me, sibling = (x, y, c), (x, y, 1 - c)

        def blk(px, py, pc):
            return out_ref.at[4 * px + 2 * py + pc]

        def copy(k, block, to, src=None):
            return pltpu.make_async_remote_copy(
                src_ref=blk(*block) if src is None else src, dst_ref=blk(*block),
                send_sem=send_sems.at[k], recv_sem=recv_sems.at[k], device_id=to, device_id_type=MESH)

        mine = pltpu.make_async_copy(x_ref, blk(*me), local_sem)
        mine.start()
        first = [copy(0, me, sibling, src=x_ref)]
        first += [copy(1 + j, me, (*chip, c), src=x_ref) for j, chip in enumerate(chips)]
        for cp in first:
            cp.start()
        passed = [copy(4 + j, (*chip, c), sibling) for j, chip in enumerate(chips)]
        for j, chip in enumerate(chips):
            copy(1 + j, (*chip, c), me).wait_recv()
            passed[j].start()
        copy(0, sibling, me).wait_recv()
        for j, chip in enumerate(chips):
            copy(4 + j, (*chip, 1 - c), me).wait_recv()
        for cp in first + passed:
            cp.wait_send()
        mine.wait()

    return pl.pallas_call(
        body, name=name, out_shape=jax.ShapeDtypeStruct((N_DEV, rows, w), v.dtype),
        in_specs=[pl.BlockSpec(memory_space=pltpu.VMEM)], out_specs=pl.BlockSpec(memory_space=pltpu.VMEM),
        scratch_shapes=[pltpu.SemaphoreType.DMA((7,)), pltpu.SemaphoreType.DMA((7,)), pltpu.SemaphoreType.DMA],
    )(v)


def _gather_shards(shards, name):
    n = len(shards)

    def body(*refs):
        ins, outs = refs[:n], refs[n:2 * n]
        send_sems, recv_sems, local_sems = refs[2 * n:]
        x, y, c, chips = _mesh_pos()
        k = 2 * x + y
        sibling = (x, y, 1 - c)

        def window(t, chip_k, half):
            r = ins[t].shape[1]
            return outs[t].at[:, pl.ds(chip_k * r + half * (r // 2), r // 2), :]

        def copy(t, j, chip_k, half, to, src=None):
            return pltpu.make_async_remote_copy(
                src_ref=window(t, chip_k, half) if src is None else src, dst_ref=window(t, chip_k, half),
                send_sem=send_sems.at[6 * t + j], recv_sem=recv_sems.at[6 * t + j], device_id=to, device_id_type=MESH)

        started, local = [], []
        for t in range(n):
            r = ins[t].shape[1]
            lc = pltpu.make_async_copy(ins[t], outs[t].at[:, pl.ds(k * r, r), :], local_sems.at[t])
            lc.start()
            local.append(lc)
            src = ins[t].at[:, pl.ds(c * (r // 2), r // 2), :]
            for j, chip in enumerate(chips):
                cp = copy(t, j, k, c, (*chip, c), src=src)
                cp.start()
                started.append(cp)
        for t in range(n):
            for j, chip in enumerate(chips):
                kj = 2 * chip[0] + chip[1]
                copy(t, j, kj, c, sibling).wait_recv()
                cp = copy(t, 3 + j, kj, c, sibling)
                cp.start()
                started.append(cp)
        for t in range(n):
            for j, chip in enumerate(chips):
                kj = 2 * chip[0] + chip[1]
                copy(t, 3 + j, kj, 1 - c, sibling).wait_recv()
        for cp in started:
            cp.wait_send()
        for lc in local:
            lc.wait()

    out_shape = [jax.ShapeDtypeStruct((s.shape[0], N_CHIPS * s.shape[1], s.shape[2]), s.dtype) for s in shards]
    return pl.pallas_call(
        body, name=name, out_shape=out_shape, in_specs=_hbm_specs(n), out_specs=_hbm_specs(n),
        scratch_shapes=[pltpu.SemaphoreType.DMA((6 * n,)), pltpu.SemaphoreType.DMA((6 * n,)),
                        pltpu.SemaphoreType.DMA((n,))],
    )(*shards)


def _chunk_rows(h, w):
    best = 16
    for t in range(16, h + 1, 16):
        if h % t == 0 and t * w * 4 <= (2 << 20):
            best = t
    return best


def _pair_sum(part, pos, name):
    _, h, w = part.shape
    cr = _chunk_rows(h, w)
    nc = h // cr
    n = 4 * nc
    slots = 4

    def body(pos_ref, own_ref, send_ref, s_ref, sb_ref, rbuf, send_sems, recv_sems):
        x, y, c, _ = _mesh_pos()
        k = pl.program_id(0)

        def copy(slot):
            return pltpu.make_async_remote_copy(
                src_ref=send_ref, dst_ref=rbuf.at[slot], send_sem=send_sems.at[slot], recv_sem=recv_sems.at[slot],
                device_id=(x, y, 1 - c), device_id_type=MESH)

        @pl.when(k < n)
        def _():
            copy(k % slots).start()

        @pl.when(k > 0)
        def _():
            before = (k + slots - 1) % slots
            copy(before).wait_recv()
            s = own_ref[...] + rbuf[before]
            s_ref[...] = s
            sb_ref[...] = s.astype(BF16)

        @pl.when(k < n)
        def _():
            copy(k % slots).wait_send()

    def own(k, p):
        j = jnp.maximum(k - 1, 0)
        return ((2 * (j // nc) + p[1]) * nc + j % nc, 0)

    def send(k, p):
        j = jnp.minimum(k, n - 1)
        return ((2 * (j // nc) + 1 - p[1]) * nc + j % nc, 0)

    grid_spec = pltpu.PrefetchScalarGridSpec(
        num_scalar_prefetch=1, grid=(n + 1,),
        in_specs=[pl.BlockSpec((cr, w), own), pl.BlockSpec((cr, w), send)],
        out_specs=[pl.BlockSpec((cr, w), lambda k, p: (jnp.maximum(k - 1, 0), 0))] * 2,
        scratch_shapes=[pltpu.VMEM((slots, cr, w), F32), pltpu.SemaphoreType.DMA((slots,)),
                        pltpu.SemaphoreType.DMA((slots,))])
    part2 = part.reshape(8 * h, w)
    s, sb = pl.pallas_call(
        body, name=name, grid_spec=grid_spec, compiler_params=_params(),
        out_shape=[jax.ShapeDtypeStruct((4 * h, w), F32), jax.ShapeDtypeStruct((4 * h, w), BF16)],
    )(pos, part2, part2)
    return s.reshape(4, h, w), sb.reshape(4, h, w)


class _Ride:
    def __init__(self, ins, out_shape, sems, copies):
        self.ins, self.out_shape, self.sems, self.copies = list(ins), list(out_shape), list(sems), copies

    def start(self, ins, outs, sems):
        sends, _, _, local = self.copies(ins, outs, sems)
        for cp in local + sends:
            cp.start()

    def finish(self, ins, outs, sems):
        _, recvs, sends, local = self.copies(ins, outs, sems)
        for cp in recvs:
            cp.wait_recv()
        for cp in sends:
            cp.wait_send()
        for cp in local:
            cp.wait()


def _scatter_ride(sums_bf16):
    n = len(sums_bf16)

    def copies(ins, outs, sems):
        send_sems, recv_sems = sems
        x, y, c, chips = _mesh_pos()
        cps = [pltpu.make_async_remote_copy(
            src_ref=ins[t].at[2 * chip[0] + chip[1]], dst_ref=outs[t].at[j],
            send_sem=send_sems.at[3 * t + j], recv_sem=recv_sems.at[3 * t + j],
            device_id=(*chip, c), device_id_type=MESH) for t in range(n) for j, chip in enumerate(chips)]
        return cps, cps, cps, []

    return _Ride(sums_bf16, [jax.ShapeDtypeStruct((3,) + s.shape[1:], BF16) for s in sums_bf16],
                 [pltpu.SemaphoreType.DMA((3 * n,)), pltpu.SemaphoreType.DMA((3 * n,))], copies)


def _broadcast_ride(shards):
    n = len(shards)

    def copies(ins, outs, sems):
        send_sems, recv_sems, local_sems = sems
        x, y, c, chips = _mesh_pos()
        k = 2 * x + y
        sends, recvs, local = [], [], []
        for t in range(n):
            r = ins[t].shape[1]
            h = r // 2
            local.append(pltpu.make_async_copy(ins[t], outs[t].at[:, pl.ds(k * r, r), :], local_sems.at[t]))
            src = ins[t].at[:, pl.ds(c * h, h), :]
            mine = outs[t].at[:, pl.ds(k * r + c * h, h), :]
            for j, chip in enumerate(chips):
                kj = 2 * chip[0] + chip[1]
                for d in range(2):
                    sends.append(pltpu.make_async_remote_copy(
                        src_ref=src, dst_ref=mine, send_sem=send_sems.at[6 * t + 2 * j + d],
                        recv_sem=recv_sems.at[6 * t + 2 * j + c], device_id=(*chip, d), device_id_type=MESH))
                    theirs = outs[t].at[:, pl.ds(kj * r + d * h, h), :]
                    recvs.append(pltpu.make_async_remote_copy(
                        src_ref=theirs, dst_ref=theirs, send_sem=send_sems.at[6 * t + 2 * j + d],
                        recv_sem=recv_sems.at[6 * t + 2 * j + d], device_id=(*chip, d), device_id_type=MESH))
        return sends, recvs, sends, local

    return _Ride(shards, [jax.ShapeDtypeStruct((s.shape[0], N_CHIPS * s.shape[1], s.shape[2]), s.dtype) for s in shards],
                 [pltpu.SemaphoreType.DMA((6 * n,)), pltpu.SemaphoreType.DMA((6 * n,)), pltpu.SemaphoreType.DMA((n,))],
                 copies)


def _ride_call(body, ride, args, *, name, grid, in_specs, out_specs, out_shape, compiler_params=None):
    in_specs, out_specs, out_shape = list(in_specs), list(out_specs), list(out_shape)
    if ride is None:
        res = pl.pallas_call(body, name=name, grid=grid, in_specs=in_specs, out_specs=out_specs, out_shape=out_shape,
                             compiler_params=compiler_params)(*args)
        return list(res), []
    ni, no, ri, ro = len(in_specs), len(out_specs), len(ride.ins), len(ride.out_shape)

    def at_step(pick):
        hit = None
        for d, n in enumerate(grid):
            here = pl.program_id(d) == pick(n)
            hit = here if hit is None else hit & here
        return hit

    def carried(*refs):
        ins, rins = refs[:ni], refs[ni:ni + ri]
        outs, routs = refs[ni + ri:ni + ri + no], refs[ni + ri + no:ni + ri + no + ro]
        sems = refs[ni + ri + no + ro:]

        @pl.when(at_step(lambda n: 0))
        def _():
            ride.start(rins, routs, sems)

        body(*ins, *outs)

        @pl.when(at_step(lambda n: n - 1))
        def _():
            ride.finish(rins, routs, sems)

    res = pl.pallas_call(
        carried, name=name, grid=grid, in_specs=in_specs + _hbm_specs(ri), out_specs=out_specs + _hbm_specs(ro),
        out_shape=out_shape + ride.out_shape, scratch_shapes=ride.sems, compiler_params=compiler_params,
    )(*args, *ride.ins)
    return list(res[:no]), list(res[no:])


def _sum_and_join(sums, got, pos, name):
    _, h, w = sums.shape
    cr = _chunk_rows(h, w)

    def body(pos_ref, mine_ref, got_ref, o_ref, ebuf, rbuf, send_sems, recv_sems):
        x, y, c, _ = _mesh_pos()
        slot = pl.program_id(0) % 2
        e = mine_ref[...]
        for j in range(3):
            e = e + got_ref[j].astype(F32)
        ebuf[slot] = e
        cp = pltpu.make_async_remote_copy(
            src_ref=ebuf.at[slot], dst_ref=rbuf.at[slot], send_sem=send_sems.at[slot], recv_sem=recv_sems.at[slot],
            device_id=(x, y, 1 - c), device_id_type=MESH)
        cp.start()
        o_ref[pos_ref[1]] = e
        cp.wait_recv()
        o_ref[1 - pos_ref[1]] = rbuf[slot]
        cp.wait_send()

    grid_spec = pltpu.PrefetchScalarGridSpec(
        num_scalar_prefetch=1, grid=(h // cr,),
        in_specs=[pl.BlockSpec((None, cr, w), lambda i, p: (p[0], i, 0)),
                  pl.BlockSpec((3, cr, w), lambda i, p: (0, i, 0))],
        out_specs=pl.BlockSpec((2, cr, w), lambda i, p: (0, i, 0)),
        scratch_shapes=[pltpu.VMEM((2, cr, w), F32), pltpu.VMEM((2, cr, w), F32),
                        pltpu.SemaphoreType.DMA((2,)), pltpu.SemaphoreType.DMA((2,))])
    return pl.pallas_call(
        body, name=name, grid_spec=grid_spec, compiler_params=_params(),
        out_shape=jax.ShapeDtypeStruct((2, h, w), F32),
    )(pos, sums, got)


def _pair_sums(parts, pos, tag):
    pairs = [_pair_sum(p.reshape(8, p.shape[0] // 8, p.shape[1]), pos, f"rs_pair_{tag}_{t}")
             for t, p in enumerate(parts)]
    return [s for s, _ in pairs], [sb for _, sb in pairs]


def _joins(sums, got, pos, tag):
    out = []
    for t, (s, r) in enumerate(zip(sums, got)):
        full = _sum_and_join(s, r, pos, f"rs_join_{tag}_{t}")
        out.append(full.reshape(2 * full.shape[1], full.shape[2]))
    return out


def _ffn_fwd(x, prm, w13g, w2g, t, name, tm=512, ride=None):
    L, D = x.shape
    Fh = w13g.shape[-1]
    tm = min(tm, L)

    def body(x_ref, p_ref, w13_ref, w2_ref, xo_ref, sv_ref, g_ref, y_ref):
        xv = x_ref[...]
        hn, _, _, _ = _nm(xv, p_ref[3:4, :], p_ref[0:1, :], p_ref[1:2, :])
        hb = hn.astype(BF16)
        acc = jnp.zeros((tm, D), F32)
        for j in range(2):
            a = _dot(hb, w13_ref[j])
            b = _dot(hb, w13_ref[2 + j])
            sg = _sigmoid(a)
            sa = a * sg
            sv_ref[:, j * Fh:(j + 1) * Fh] = sa.astype(BF16)
            sv_ref[:, (2 + j) * Fh:(3 + j) * Fh] = (b * (sg * (1.0 + a * (1.0 - sg)))).astype(BF16)
            g = (sa * b).astype(BF16)
            g_ref[:, j * Fh:(j + 1) * Fh] = g
            acc = acc + _dot(g, w2_ref[j * Fh:(j + 1) * Fh, :])
        y_ref[...] = acc.astype(BF16)
        xo_ref[...] = xv + (0.5 * p_ref[2:3, :]) * acc

    res, carried = _ride_call(
        body, ride, (x, prm, w13g, w2g), name=name, grid=(L // tm,),
        in_specs=[_tile(tm, D), _full((8, D)),
                  _resident((None, 4, D, Fh), lambda i: (t, 0, 0, 0)),
                  _resident((None, 2 * Fh, D), lambda i: (t, 0, 0))],
        out_specs=[_tile(tm, D), _tile(tm, 4 * Fh), _tile(tm, 2 * Fh), _tile(tm, D)],
        out_shape=[jax.ShapeDtypeStruct((L, D), F32), jax.ShapeDtypeStruct((L, 4 * Fh), BF16),
                   jax.ShapeDtypeStruct((L, 2 * Fh), BF16), jax.ShapeDtypeStruct((L, D), BF16)],
        compiler_params=_params())
    xo, sv, g, y = res
    return xo, (sv, g), y, carried


def _head_grad(xo, tgt, fg):
    D = xo.shape[-1]
    r = lax.rsqrt(jnp.mean(xo * xo, axis=-1, keepdims=True) + RMS_EPS)
    xhat = xo * r
    err = xhat * fg - tgt
    loss = 0.5 * jnp.sum(jnp.mean(err * err, axis=-1, keepdims=True), axis=0, keepdims=True)
    dy = err * (1.0 / D)
    dxh = dy * fg
    return r * (dxh - xhat * jnp.mean(dxh * xhat, axis=-1, keepdims=True)), _sum0(dy * xhat), loss


def _ffn_bwd(dout, x, sv, y, prm, w13g, w2g, t, name, tm=256, ride=None, head=None):
    L, D = x.shape
    Fh = w13g.shape[-1]
    tm = min(tm, L)

    def body(do_ref, x_ref, sv_ref, y_ref, p_ref, w13_ref, w2_ref, *rest):
        dx_ref, dab_ref, dy_ref, hn_ref, acc_ref = rest[-5:]
        i = pl.program_id(0)
        head_rows = []
        if head is None:
            do = do_ref[...]
        else:
            do, dfg, loss = _head_grad(do_ref[...], rest[0][...], rest[1][...])
            head_rows = [dfg, jnp.broadcast_to(loss, (1, D))]
        gain, shift, scale, gate = p_ref[3:4, :], p_ref[0:1, :], p_ref[1:2, :], p_ref[2:3, :]
        hn, xhat, r, nrm = _nm(x_ref[...], gain, shift, scale)
        hn_ref[...] = hn.astype(BF16)
        dgate = 0.5 * _sum0(do * y_ref[...].astype(F32))
        dyb = ((0.5 * gate) * do).astype(BF16)
        dy_ref[...] = dyb
        dhn = jnp.zeros((tm, D), F32)
        for j in range(2):
            dg = _dot_nt(dyb, w2_ref[j * Fh:(j + 1) * Fh, :])
            da = (dg * sv_ref[:, (2 + j) * Fh:(3 + j) * Fh].astype(F32)).astype(BF16)
            db = (dg * sv_ref[:, j * Fh:(j + 1) * Fh].astype(F32)).astype(BF16)
            dab_ref[:, j * Fh:(j + 1) * Fh] = da
            dab_ref[:, (2 + j) * Fh:(3 + j) * Fh] = db
            dhn = dhn + _dot_nt(da, w13_ref[j]) + _dot_nt(db, w13_ref[2 + j])
        dx, dshift, dscale, dgn = _nm_bwd(dhn, xhat, r, nrm, gain, scale)
        dx_ref[...] = do + dx
        _acc_rows(acc_ref, i == 0, [dshift, dscale, dgate, dgn] + head_rows)

    head_args = () if head is None else head
    head_specs = [] if head is None else [_tile(tm, D), _full((1, D))]
    res, carried = _ride_call(
        body, ride, (dout, x, sv, y, prm, w13g, w2g, *head_args), name=name, grid=(L // tm,),
        in_specs=[_tile(tm, D), _tile(tm, D), _tile(tm, 4 * Fh), _tile(tm, D), _full((8, D)),
                  _resident((None, 4, D, Fh), lambda i: (t, 0, 0, 0)),
                  _resident((None, 2 * Fh, D), lambda i: (t, 0, 0))] + head_specs,
        out_specs=[_tile(tm, D), _tile(tm, 4 * Fh), _tile(tm, D), _tile(tm, D), _full((8, D))],
        out_shape=[jax.ShapeDtypeStruct((L, D), F32), jax.ShapeDtypeStruct((L, 4 * Fh), BF16),
                   jax.ShapeDtypeStruct((L, D), BF16), jax.ShapeDtypeStruct((L, D), BF16),
                   jax.ShapeDtypeStruct((8, D), F32)],
        compiler_params=_params())
    return (*res, carried)


def _mm_tn(a, b, slabs, a_slabbed, name, init=None, tl=1024, ride=None):
    L = a.shape[0]
    ka = a.shape[1] // slabs if a_slabbed else a.shape[1]
    nb = b.shape[1] if a_slabbed else b.shape[1] // slabs
    tl = min(tl, L)
    has_init = init is not None

    def body(a_ref, b_ref, *rest):
        o_ref = rest[-1]
        step = pl.program_id(1)

        @pl.when(step == 0)
        def _():
            o_ref[...] = rest[0][...] if has_init else jnp.zeros((ka, nb), F32)

        o_ref[...] += _dot_tn(a_ref[...], b_ref[...])

    in_specs = [pl.BlockSpec((tl, ka), (lambda s, l: (l, s)) if a_slabbed else (lambda s, l: (l, 0))),
                pl.BlockSpec((tl, nb), (lambda s, l: (l, 0)) if a_slabbed else (lambda s, l: (l, s)))]
    args = [a, b]
    if has_init:
        in_specs.append(pl.BlockSpec((ka, nb), lambda s, l: (s, 0)))
        args.append(init)
    res, carried = _ride_call(
        body, ride, args, name=name, grid=(slabs, L // tl), in_specs=in_specs,
        out_specs=[pl.BlockSpec((ka, nb), lambda s, l: (s, 0))],
        out_shape=[jax.ShapeDtypeStruct((slabs * ka, nb), F32)], compiler_params=_params())
    return res[0] if ride is None else (res[0], carried)


def _even_in_fwd(x, prm, wing, name, tm=512):
    L, D = x.shape
    W = wing.shape[-1]
    tm = min(tm, L)

    def body(x_ref, p_ref, w_ref, q_ref, k_ref, v_ref, u_ref, hn_ref):
        hn, _, _, _ = _nm(x_ref[...], p_ref[3:4, :], p_ref[0:1, :], p_ref[1:2, :])
        hb = hn.astype(BF16)
        hn_ref[...] = hb
        q_ref[...] = _dot(hb, w_ref[0]).astype(BF16)
        k_ref[...] = _dot(hb, w_ref[1]).astype(BF16)
        v_ref[...] = _dot(hb, w_ref[2]).astype(BF16)
        u_ref[...] = _dot(hb, w_ref[3])

    return pl.pallas_call(
        body, name=name, grid=(L // tm,),
        in_specs=[_tile(tm, D), _full((8, D)), _resident((None, 4, D, W), lambda i: (0, 0, 0, 0))],
        out_specs=[_tile(tm, W)] * 4 + [_tile(tm, D)],
        out_shape=[jax.ShapeDtypeStruct((L, W), BF16)] * 3 + [jax.ShapeDtypeStruct((L, W), F32),
                                                              jax.ShapeDtypeStruct((L, D), BF16)],
        compiler_params=_params())(x, prm, wing)


def _even_in_bwd(dout, x, dq, dk, dv, du, prm, wing, name, tm=512):
    L, D = x.shape
    W = wing.shape[-1]
    tm = min(tm, L)

    def body(do_ref, x_ref, dq_ref, dk_ref, dv_ref, du_ref, p_ref, w_ref, dx_ref, ds_ref, acc_ref):
        i = pl.program_id(0)
        gain, shift, scale = p_ref[3:4, :], p_ref[0:1, :], p_ref[1:2, :]
        _, xhat, r, nrm = _nm(x_ref[...], gain, shift, scale)
        dhn = jnp.zeros((tm, D), F32)
        for s, ref in enumerate((dq_ref, dk_ref, dv_ref, du_ref)):
            d = ref[...].astype(BF16)
            ds_ref[:, s * W:(s + 1) * W] = d
            dhn = dhn + _dot_nt(d, w_ref[s])
        dx, dshift, dscale, dgn = _nm_bwd(dhn, xhat, r, nrm, gain, scale)
        dx_ref[...] = do_ref[...] + dx
        _acc_rows(acc_ref, i == 0, [dshift, dscale, None, dgn])

    return pl.pallas_call(
        body, name=name, grid=(L // tm,),
        in_specs=[_tile(tm, D), _tile(tm, D)] + [_tile(tm, W)] * 4 +
                 [_full((8, D)), _resident((None, 4, D, W), lambda i: (0, 0, 0, 0))],
        out_specs=[_tile(tm, D), _tile(tm, 4 * W), _full((8, D))],
        out_shape=[jax.ShapeDtypeStruct((L, D), F32), jax.ShapeDtypeStruct((L, 4 * W), BF16),
                   jax.ShapeDtypeStruct((8, D), F32)],
        compiler_params=_params())(dout, x, dq, dk, dv, du, prm, wing)


def _even_out_fwd(x, att, pool, prm, woutg, name, tm=512):
    L, D = x.shape
    W = D // 2
    tm = min(tm, L)

    def body(x_ref, a_ref, p_ref, prm_ref, w_ref, xo_ref, y_ref):
        yv = _dot(a_ref[...], w_ref[0:W, :]) + _dot(p_ref[...], w_ref[W:2 * W, :])
        y_ref[...] = yv.astype(BF16)
        xo_ref[...] = x_ref[...] + prm_ref[2:3, :] * yv

    return pl.pallas_call(
        body, name=name, grid=(L // tm,),
        in_specs=[_tile(tm, D), _tile(tm, W), _tile(tm, W), _full((8, D)),
                  _resident((None, D, D), lambda i: (0, 0, 0))],
        out_specs=[_tile(tm, D), _tile(tm, D)],
        out_shape=[jax.ShapeDtypeStruct((L, D), F32), jax.ShapeDtypeStruct((L, D), BF16)],
        compiler_params=_params())(x, att, pool, prm, woutg)


def _even_out_bwd(dout, y, prm, woutg, name, tm=512):
    L, D = dout.shape
    W = D // 2
    tm = min(tm, L)

    def body(do_ref, y_ref, p_ref, w_ref, dy_ref, da_ref, dp_ref, acc_ref):
        i = pl.program_id(0)
        do = do_ref[...]
        dgate = _sum0(do * y_ref[...].astype(F32))
        dyb = (p_ref[2:3, :] * do).astype(BF16)
        dy_ref[...] = dyb
        da_ref[...] = _dot_nt(dyb, w_ref[0:W, :]).astype(BF16)
        dp_ref[...] = _dot_nt(dyb, w_ref[W:2 * W, :])
        _acc_rows(acc_ref, i == 0, [None, None, dgate])

    return pl.pallas_call(
        body, name=name, grid=(L // tm,),
        in_specs=[_tile(tm, D), _tile(tm, D), _full((8, D)), _resident((None, D, D), lambda i: (0, 0, 0))],
        out_specs=[_tile(tm, D), _tile(tm, W), _tile(tm, W), _full((8, D))],
        out_shape=[jax.ShapeDtypeStruct((L, D), BF16), jax.ShapeDtypeStruct((L, W), BF16),
                   jax.ShapeDtypeStruct((L, W), F32), jax.ShapeDtypeStruct((8, D), F32)],
        compiler_params=_params())(dout, y, prm, woutg)


def _group_ri(variant, qr, kr):
    first_key = (0, qr, GK - NA_KH)[variant]
    if not first_key <= kr < first_key + NA_KH:
        return None
    return kr - qr + (NA_KH - 1, NA_KH - 1 - NA_KH // 2, NA_KH - 1 - (GK - GQ))[variant]


HEADS_PER_BLOCK = 128 // NA_HEAD_DIM


def _bias_table(rpb, name):
    H = rpb.shape[0]
    hpb = HEADS_PER_BLOCK
    nri, nci = 2 * NA_KH - 1, 2 * NA_KW - 1
    col = jnp.arange(GRID_W)
    rel = (col[None, :] - col[:, None] + (NA_KW - 1)).reshape(1, -1)
    onehot = (rel == jnp.arange(32)[:, None]).astype(F32)
    cs = jnp.clip(col - NA_KW // 2, 0, GRID_W - NA_KW)
    ok = ((col[None, :] >= cs[:, None]) & (col[None, :] < cs[:, None] + NA_KW)).astype(F32).reshape(1, -1)
    by_lane_block = rpb.reshape(H // hpb, hpb, nri, nci).transpose(1, 0, 2, 3)
    rpb2 = jnp.pad(by_lane_block.reshape(H * nri, nci), ((0, 0), (0, 32 - nci)))

    def body(r_ref, e_ref, m_ref, o_ref):
        t = jnp.dot(r_ref[...], e_ref[...], preferred_element_type=F32, precision=lax.Precision.HIGHEST)
        o_ref[...] = jnp.where(m_ref[...] > 0.0, t, NEG_INF)

    tab = pl.pallas_call(body, name=name, out_shape=jax.ShapeDtypeStruct((H * nri, GRID_W * GRID_W), F32))(
        rpb2, onehot, ok)
    tab = tab.reshape(hpb, H // hpb, nri, GRID_W, GRID_W)
    outside = jnp.full((H // hpb, GRID_W, GRID_W), NEG_INF, F32)
    variants = []
    for variant in range(3):
        rows = []
        for h in range(hpb):
            for qr in range(GQ):
                ris = [_group_ri(variant, qr, kr) for kr in range(GK)]
                rows.append(jnp.concatenate([outside if ri is None else tab[h, :, ri] for ri in ris], axis=2))
        variants.append(jnp.concatenate(rows, axis=1))
    return jnp.stack(variants, axis=1)


def _attn_probs(q, kw, kc, bias, scale):
    s_w = _dot_nt(q, kw) * scale + bias
    s_c = _dot_nt(q, kc) * scale
    m = jnp.maximum(jnp.max(s_w, axis=-1, keepdims=True), jnp.max(s_c, axis=-1, keepdims=True))
    e_w = jnp.exp(s_w - m)
    e_c = jnp.exp(s_c - m)
    inv = 1.0 / (jnp.sum(e_w, axis=-1, keepdims=True) + jnp.sum(e_c, axis=-1, keepdims=True))
    return e_w * inv, e_c * inv


def _group_place(g, R):
    G = R // GQ
    kb = jnp.clip(g * GQ - NA_KH // 2, 0, R - GK)
    variant = jnp.where(g == 0, 0, jnp.where(g == G - 1, 2, 1))
    return pl.multiple_of(g * (GQ * GRID_W), GQ * GRID_W), pl.multiple_of(kb * GRID_W, GRID_W), variant


def _lane_masks(width, dh):
    lane = lax.broadcasted_iota(jnp.int32, (1, width), 1)
    return [(lane >= h * dh) & (lane < (h + 1) * dh) for h in range(width // dh)]


def _only(mask, a):
    return jnp.where(mask, a, jnp.zeros_like(a))


def _attn_fwd(q, k, v, kc, vc, bias, name, ride=None):
    L, width = q.shape
    C = kc.shape[0]
    dh = NA_HEAD_DIM
    lanes = 128
    hpb = lanes // dh
    R = L // GRID_W
    nq, nk = GQ * GRID_W, GK * GRID_W
    scale = dh ** -0.5

    def body(q_ref, k_ref, v_ref, kc_ref, vc_ref, b_ref, o_ref):
        masks = _lane_masks(lanes, dh)
        kc2 = kc_ref[...]
        vcs = [_only(m, vc_ref[...]) for m in masks]

        def group(g, carry):
            q0, k0, variant = _group_place(g, R)
            q2 = q_ref[pl.ds(q0, nq), :]
            k2 = k_ref[pl.ds(k0, nk), :]
            v2 = v_ref[pl.ds(k0, nk), :]
            qs = jnp.concatenate([_only(m, q2) for m in masks], axis=0)
            p_w, p_c = _attn_probs(qs, k2, kc2, b_ref[variant], scale)
            p_w, p_c = p_w.astype(BF16), p_c.astype(BF16)
            o2 = jnp.zeros((nq, lanes), F32)
            for h, m in enumerate(masks):
                rows = slice(h * nq, (h + 1) * nq)
                o2 = o2 + _dot(p_w[rows], _only(m, v2)) + _dot(p_c[rows], vcs[h])
            o_ref[pl.ds(q0, nq), :] = o2.astype(BF16)
            return carry

        lax.fori_loop(0, R // GQ, group, 0)

    cols = lambda n: pl.BlockSpec((n, lanes), lambda p: (0, p))
    res, carried = _ride_call(
        body, ride, (q, k, v, kc, vc, bias), name=name, grid=(width // lanes,),
        in_specs=[cols(L), cols(L), cols(L), cols(C), cols(C),
                  pl.BlockSpec((None, 3, hpb * nq, nk), lambda p: (p, 0, 0, 0))],
        out_specs=[cols(L)], out_shape=[jax.ShapeDtypeStruct((L, width), BF16)],
        compiler_params=_params())
    return res[0], carried


def _attn_bwd(q, k, v, kc, vc, bias, do, name, ride=None):
    L, width = q.shape
    C = kc.shape[0]
    dh = NA_HEAD_DIM
    lanes = 128
    hpb = lanes // dh
    R = L // GRID_W
    nq, nk = GQ * GRID_W, GK * GRID_W
    scale = dh ** -0.5

    def body(q_ref, k_ref, v_ref, kc_ref, vc_ref, b_ref, do_ref, dq_ref, dk_ref, dv_ref, dkc_ref, dvc_ref, db_ref):
        masks = _lane_masks(lanes, dh)
        kc2 = kc_ref[...]
        vc2 = vc_ref[...]
        kcs = [_only(m, kc2) for m in masks]
        dk_ref[...] = jnp.zeros((L, lanes), F32)
        dv_ref[...] = jnp.zeros((L, lanes), F32)
        dkc_ref[...] = jnp.zeros((C, lanes), F32)
        dvc_ref[...] = jnp.zeros((C, lanes), F32)
        db_ref[...] = jnp.zeros((3, hpb * nq, nk), F32)

        def group(g, carry):
            q0, k0, variant = _group_place(g, R)
            q2 = q_ref[pl.ds(q0, nq), :]
            k2 = k_ref[pl.ds(k0, nk), :]
            v2 = v_ref[pl.ds(k0, nk), :]
            do2 = do_ref[pl.ds(q0, nq), :]
            qs = jnp.concatenate([_only(m, q2) for m in masks], axis=0)
            dos = jnp.concatenate([_only(m, do2) for m in masks], axis=0)
            p_w, p_c = _attn_probs(qs, k2, kc2, b_ref[variant], scale)
            dp_w = _dot_nt(dos, v2)
            dp_c = _dot_nt(dos, vc2)
            delta = jnp.sum(p_w * dp_w, axis=-1, keepdims=True) + jnp.sum(p_c * dp_c, axis=-1, keepdims=True)
            ds_w = p_w * (dp_w - delta)
            ds_c = p_c * (dp_c - delta)
            db_ref[variant] += ds_w
            dsw = (ds_w * scale).astype(BF16)
            dsc = (ds_c * scale).astype(BF16)
            dq2 = jnp.zeros((nq, lanes), F32)
            for h, m in enumerate(masks):
                rows = slice(h * nq, (h + 1) * nq)
                dq2 = dq2 + _dot(dsw[rows], _only(m, k2)) + _dot(dsc[rows], kcs[h])
            dq_ref[pl.ds(q0, nq), :] = dq2.astype(BF16)
            dk_ref[pl.ds(k0, nk), :] += _dot_tn(dsw, qs)
            dv_ref[pl.ds(k0, nk), :] += _dot_tn(p_w.astype(BF16), dos)
            dkc_ref[...] += _dot_tn(dsc, qs)
            dvc_ref[...] += _dot_tn(p_c.astype(BF16), dos)
            return carry

        lax.fori_loop(0, R // GQ, group, 0)

    cols = lambda n: _resident((n, lanes), lambda p: (0, p))
    bspec = _resident((None, 3, hpb * nq, nk), lambda p: (p, 0, 0, 0))
    res, carried = _ride_call(
        body, ride, (q, k, v, kc, vc, bias, do), name=name, grid=(width // lanes,),
        in_specs=[cols(L), cols(L), cols(L), cols(C), cols(C), bspec, cols(L)],
        out_specs=[cols(L), cols(L), cols(L), cols(C), cols(C), bspec],
        out_shape=[jax.ShapeDtypeStruct((L, width), BF16)] + [jax.ShapeDtypeStruct((L, width), F32)] * 2 +
                  [jax.ShapeDtypeStruct((C, width), F32)] * 2 +
                  [jax.ShapeDtypeStruct((width // lanes, 3, hpb * nq, nk), F32)],
        compiler_params=_params())
    return (*res, carried)


def _rpb_grad(dbias, name):
    hpb = HEADS_PER_BLOCK
    H = dbias.shape[0] * hpb
    nri, nci = 2 * NA_KH - 1, 2 * NA_KW - 1
    d6 = dbias.reshape(H // hpb, 3, hpb, GQ, GRID_W, GK, GRID_W).transpose(0, 2, 1, 3, 5, 4, 6)
    d6 = d6.reshape(H, 3, GQ, GK, GRID_W, GRID_W)
    col = jnp.arange(GRID_W)
    onehot = (col[None, None, :] - col[None, :, None] + (NA_KW - 1) == jnp.arange(32)[:, None, None]).astype(F32)
    places = [(v, qr, kr) for v in range(3) for qr in range(GQ) for kr in range(GK)]

    def body(d_ref, m_ref, o_ref, t_ref):
        t_ref[...] = jnp.zeros((32, GRID_W), F32)
        o_ref[...] = jnp.zeros((16, 32, 128), F32)
        for ri in range(nri):
            a = None
            for place in places:
                if _group_ri(*place) == ri:
                    blk = d_ref[place]
                    a = blk if a is None else a + blk
            for ci in range(nci):
                t_ref[ci:ci + 1, :] = _sum0(a * m_ref[ci])
            o_ref[ri] = jnp.broadcast_to(jnp.sum(t_ref[...], axis=1, keepdims=True), (32, 128))

    out = pl.pallas_call(
        body, name=name, grid=(H,),
        in_specs=[pl.BlockSpec((None, 3, GQ, GK, GRID_W, GRID_W), lambda h: (h, 0, 0, 0, 0, 0)),
                  pl.BlockSpec((32, GRID_W, GRID_W), lambda h: (0, 0, 0))],
        out_specs=pl.BlockSpec((None, 16, 32, 128), lambda h: (h, 0, 0, 0)),
        out_shape=jax.ShapeDtypeStruct((H, 16, 32, 128), F32),
        scratch_shapes=[pltpu.VMEM((32, GRID_W), F32)])(d6, onehot)
    return out[:, :nri, :nci, 0]


def _window_count(t, w, L):
    lo = jnp.clip(t - w // 2, 0, L)
    hi = jnp.clip(t - w // 2 + w, 0, L)
    return jnp.maximum(hi - lo, 1).astype(F32)


def _running_sum(v, w):
    k = 1
    while k < w:
        v = v + _shift_rows(v, k)
        k *= 2
    return v


def _pool_fwd(u, poolw, pscale, name, tm=512):
    L, W = u.shape
    G = POOL_GROUP_DIM
    tm = min(tm, L)
    nt = L // tm

    def body(c_ref, p_ref, n_ref, w_ref, s_ref, o_ref, dm_ref):
        i = pl.program_id(0)
        ext = _ext(p_ref[...], c_ref[...], n_ref[...], i, nt)
        t = i * tm + lax.broadcasted_iota(jnp.int32, (tm, 1), 0)
        for g, w in enumerate(POOL_WINDOWS):
            e = ext[:, g * G:(g + 1) * G]
            win = _shift_rows(_running_sum(e, w), -(w // 2 - 1))[HALO:HALO + tm]
            dmx = (win / _window_count(t, w, L) - e[HALO:HALO + tm]).astype(BF16)
            dm_ref[:, g * G:(g + 1) * G] = dmx
            o_ref[:, g * G:(g + 1) * G] = (_dot(dmx, w_ref[g]) * s_ref[:, g * G:(g + 1) * G]).astype(BF16)

    return pl.pallas_call(
        body, name=name, grid=(nt,),
        in_specs=[_tile(tm, W), _halo_prev(tm, W), _halo_next(tm, W, L), _full((4, G, G)), _full((1, W))],
        out_specs=[_tile(tm, W), _tile(tm, W)],
        out_shape=[jax.ShapeDtypeStruct((L, W), BF16)] * 2, compiler_params=_params())(u, u, u, poolw, pscale)


def _pool_bwd(dpool, dmx, poolw, pscale, name, tm=512):
    L, W = dpool.shape
    G = POOL_GROUP_DIM
    tm = min(tm, L)
    nt = L // tm

    def body(c_ref, p_ref, n_ref, dm_ref, w_ref, s_ref, du_ref, dw_ref, acc_ref):
        i = pl.program_id(0)
        ext = _ext(p_ref[...], c_ref[...], n_ref[...], i, nt)
        te = i * tm - HALO + lax.broadcasted_iota(jnp.int32, (tm + 2 * HALO, 1), 0)

        @pl.when(i == 0)
        def _():
            dw_ref[...] = jnp.zeros((4 * G, G), F32)

        rows = []
        for g, w in enumerate(POOL_WINDOWS):
            sc = s_ref[:, g * G:(g + 1) * G]
            dpre = (ext[:, g * G:(g + 1) * G] * sc).astype(BF16)
            dd = _dot_nt(dpre, w_ref[g])
            spread = _shift_rows(_running_sum(dd / _window_count(te, w, L), w), -(w // 2))
            du_ref[:, g * G:(g + 1) * G] = (spread - dd)[HALO:HALO + tm]
            dmx_g = dm_ref[:, g * G:(g + 1) * G]
            rows.append(_sum0(c_ref[:, g * G:(g + 1) * G] * _dot(dmx_g, w_ref[g])))
            dw_ref[g * G:(g + 1) * G, :] += _dot_tn(dmx_g, dpre[HALO:HALO + tm])
        _acc_rows(acc_ref, i == 0, [jnp.concatenate(rows, axis=1)])

    return pl.pallas_call(
        body, name=name, grid=(nt,),
        in_specs=[_tile(tm, W), _halo_prev(tm, W), _halo_next(tm, W, L), _tile(tm, W), _full((4, G, G)),
                  _full((1, W))],
        out_specs=[_tile(tm, W), _full((4 * G, G)), _full((8, W))],
        out_shape=[jax.ShapeDtypeStruct((L, W), F32), jax.ShapeDtypeStruct((4 * G, G), F32),
                   jax.ShapeDtypeStruct((8, W), F32)],
        compiler_params=_params())(dpool, dpool, dpool, dmx, poolw, pscale)


def _conv3(z, cw):
    return _shift_rows(z, 1) * cw[0] + z * cw[1] + _shift_rows(z, -1) * cw[2]


def _conv_fwd(x, prm, wing, woutg, name, tm=512):
    L, D = x.shape
    Ws = wing.shape[-1]
    tm = min(tm, L)
    nt = L // tm
    te = tm + 2 * HALO

    def body(c_ref, p_ref, n_ref, prm_ref, wi_ref, wo_ref, xo_ref, y_ref, b_ref):
        i = pl.program_id(0)
        xe = jnp.concatenate([p_ref[...], c_ref[...], n_ref[...]], axis=0)
        hn, _, _, _ = _nm(xe, prm_ref[3:4, :], prm_ref[0:1, :], prm_ref[1:2, :])
        hb = hn.astype(BF16)
        proj = jnp.concatenate([_dot(hb, wi_ref[s]) for s in range(4)], axis=1)
        bg, cg, xin = proj[:, :D], proj[:, D:2 * D], proj[:, 2 * D:]
        tpos = i * tm - HALO + lax.broadcasted_iota(jnp.int32, (te, 1), 0)
        valid = ((tpos >= 0) & (tpos < L)).astype(F32)
        yc = _conv3(cg * xin * valid, [prm_ref[4 + k:5 + k, :] for k in range(3)])
        h2 = (bg * yc)[HALO:HALO + tm].astype(BF16)
        yv = _dot(h2, wo_ref[...])
        y_ref[...] = yv.astype(BF16)
        xo_ref[...] = c_ref[...] + prm_ref[2:3, :] * yv
        b_ref[...] = proj[HALO:HALO + tm].astype(BF16)

    return pl.pallas_call(
        body, name=name, grid=(nt,),
        in_specs=[_tile(tm, D), _halo_prev(tm, D), _halo_next(tm, D, L), _full((8, D)),
                  _resident((None, 4, D, Ws), lambda i: (0, 0, 0, 0)),
                  _resident((None, D, D), lambda i: (0, 0, 0))],
        out_specs=[_tile(tm, D), _tile(tm, D), _tile(tm, 3 * D)],
        out_shape=[jax.ShapeDtypeStruct((L, D), F32), jax.ShapeDtypeStruct((L, D), BF16),
                   jax.ShapeDtypeStruct((L, 3 * D), BF16)],
        compiler_params=_params())(x, x, x, prm, wing, woutg)


def _conv_bwd(dout, x, y, bcx, prm, wing, woutg, name, tm=256, ride=None):
    L, D = x.shape
    Ws = wing.shape[-1]
    tm = min(tm, L)
    nt = L // tm
    te = tm + 2 * HALO

    def body(dc_ref, dp_ref, dn_ref, x_ref, y_ref, bc_ref, bp_ref, bn_ref, prm_ref, wi_ref, wo_ref,
             dx_ref, dpr_ref, h2_ref, dy_ref, hn_ref, acc_ref):
        i = pl.program_id(0)
        gain, shift, scale, gate = prm_ref[3:4, :], prm_ref[0:1, :], prm_ref[1:2, :], prm_ref[2:3, :]
        taps = [prm_ref[4 + k:5 + k, :] for k in range(3)]
        do = dc_ref[...]
        doe = _ext(dp_ref[...], do, dn_ref[...], i, nt)
        dye = (gate * doe).astype(BF16)
        dy_ref[...] = dye[HALO:HALO + tm]
        dh2 = _dot_nt(dye, wo_ref[...])
        be = jnp.concatenate([bp_ref[...], bc_ref[...], bn_ref[...]], axis=0).astype(F32)
        bg, cg, xin = be[:, :D], be[:, D:2 * D], be[:, 2 * D:]
        tpos = i * tm - HALO + lax.broadcasted_iota(jnp.int32, (te, 1), 0)
        valid = ((tpos >= 0) & (tpos < L)).astype(F32)
        z = cg * xin * valid
        yc = _conv3(z, taps)
        dyc = dh2 * bg
        h2_ref[...] = (bg * yc)[HALO:HALO + tm].astype(BF16)
        dz = _conv3(dyc, taps[::-1]) * valid
        dproj = jnp.concatenate([dh2 * yc, dz * xin, dz * cg], axis=1)[HALO:HALO + tm].astype(BF16)
        dpr_ref[...] = dproj
        dhn = jnp.zeros((tm, D), F32)
        for s in range(4):
            dhn = dhn + _dot_nt(dproj[:, s * Ws:(s + 1) * Ws], wi_ref[s])
        hn, xhat, r, nrm = _nm(x_ref[...], gain, shift, scale)
        hn_ref[...] = hn.astype(BF16)
        dx, dshift, dscale, dgn = _nm_bwd(dhn, xhat, r, nrm, gain, scale)
        dx_ref[...] = do + dx
        dgate = _sum0(do * y_ref[...].astype(F32))
        dtaps = [_sum0((dyc * _shift_rows(z, 1 - k))[HALO:HALO + tm]) for k in range(3)]
        _acc_rows(acc_ref, i == 0, [dshift, dscale, dgate, dgn] + dtaps)

    res, carried = _ride_call(
        body, ride, (dout, dout, dout, x, y, bcx, bcx, bcx, prm, wing, woutg), name=name, grid=(nt,),
        in_specs=[_tile(tm, D), _halo_prev(tm, D), _halo_next(tm, D, L), _tile(tm, D), _tile(tm, D),
                  _tile(tm, 3 * D), _halo_prev(tm, 3 * D), _halo_next(tm, 3 * D, L), _full((8, D)),
                  _resident((None, 4, D, Ws), lambda i: (0, 0, 0, 0)),
                  _resident((None, D, D), lambda i: (0, 0, 0))],
        out_specs=[_tile(tm, D), _tile(tm, 3 * D), _tile(tm, D), _tile(tm, D), _tile(tm, D), _full((8, D))],
        out_shape=[jax.ShapeDtypeStruct((L, D), F32), jax.ShapeDtypeStruct((L, 3 * D), BF16),
                   jax.ShapeDtypeStruct((L, D), BF16), jax.ShapeDtypeStruct((L, D), BF16),
                   jax.ShapeDtypeStruct((L, D), BF16), jax.ShapeDtypeStruct((8, D), F32)],
        compiler_params=_params())
    return (*res, carried)


def _mod_fwd(cond, mod_w, mod_b, name, tn=768):
    nl, D, N = mod_w.shape
    tn = min(tn, N)

    def body(c_ref, w_ref, b_ref, o_ref):
        cv = c_ref[...]
        s = (cv * _sigmoid(cv)).astype(BF16)
        o_ref[...] = _dot(s, w_ref[...].astype(BF16)) + b_ref[...]

    return pl.pallas_call(
        body, name=name, grid=(nl, N // tn),
        in_specs=[pl.BlockSpec((16, D), lambda l, j: (0, 0)), pl.BlockSpec((None, D, tn), lambda l, j: (l, 0, j)),
                  pl.BlockSpec((None, 1, tn), lambda l, j: (l, 0, j))],
        out_specs=pl.BlockSpec((None, 16, tn), lambda l, j: (l, 0, j)),
        out_shape=jax.ShapeDtypeStruct((nl, 16, N), F32), compiler_params=_params())(cond, mod_w, mod_b)


def _mod_bwd(cond, dm, mod_w, name, tn=768):
    nl, D, N = mod_w.shape
    tn = min(tn, N)

    def body(c_ref, d_ref, w_ref, dw_ref, dc_ref):
        first = (pl.program_id(0) == 0) & (pl.program_id(1) == 0)
        cv = c_ref[...]
        s = (cv * _sigmoid(cv)).astype(BF16)
        d = d_ref[...].astype(BF16)
        dw_ref[...] = _dot_tn(s, d)

        @pl.when(first)
        def _():
            dc_ref[...] = jnp.zeros((16, D), F32)

        dc_ref[...] += _dot_nt(d, w_ref[...].astype(BF16))

    return pl.pallas_call(
        body, name=name, grid=(nl, N // tn),
        in_specs=[pl.BlockSpec((16, D), lambda l, j: (0, 0)), pl.BlockSpec((None, 16, tn), lambda l, j: (l, 0, j)),
                  pl.BlockSpec((None, D, tn), lambda l, j: (l, 0, j))],
        out_specs=[pl.BlockSpec((None, D, tn), lambda l, j: (l, 0, j)), pl.BlockSpec((16, D), lambda l, j: (0, 0))],
        out_shape=[jax.ShapeDtypeStruct((nl, D, N), F32), jax.ShapeDtypeStruct((16, D), F32)],
        compiler_params=_params())(cond, dm, mod_w)


def _mod_small_grads(dm_all, cond, dsilu_parts, name):
    nl, _, N = dm_all.shape
    D = cond.shape[1]

    def body(d_ref, c_ref, p_ref, db_ref, dc_ref):
        for l in range(nl):
            db_ref[l] = _sum0(d_ref[l])
        tot = p_ref[0, 8:9, :]
        for k in range(1, N_CHIPS):
            tot = tot + p_ref[2 * k, 8:9, :]
        cv = c_ref[8:9, :]
        sg = _sigmoid(cv)
        dc_ref[...] = tot * (sg * (1.0 + cv * (1.0 - sg)))

    return pl.pallas_call(
        body, name=name, out_shape=[jax.ShapeDtypeStruct((nl, 1, N), F32), jax.ShapeDtypeStruct((1, D), F32)],
    )(dm_all, cond, dsilu_parts)


def _prm(rows, D):
    rows = [r.reshape(1, D) for r in rows]
    return jnp.concatenate(rows + [jnp.zeros((8 - len(rows), D), F32)], axis=0)


def kernel(x, c, ctx, c_ctx, mod_w, mod_b, norm_g, ffn_w13, ffn_w2, even_w_in, even_w_out, na_rpb, pool_w, pool_scale, conv_w_in, conv_w, conv_w_out, final_g, loss_target, m_c_ctx, m_mod_w, m_mod_b, m_norm_g, m_ffn_w13, m_ffn_w2, m_even_w_in, m_even_w_out, m_na_rpb, m_pool_w, m_pool_scale, m_conv_w_in, m_conv_w, m_conv_w_out, m_final_g, v_c_ctx, v_mod_w, v_mod_b, v_norm_g, v_ffn_w13, v_ffn_w2, v_even_w_in, v_even_w_out, v_na_rpb, v_pool_w, v_pool_scale, v_conv_w_in, v_conv_w, v_conv_w_out, v_final_g):
    xi, yi, ci = lax.axis_index("x"), lax.axis_index("y"), lax.axis_index("c")
    chip = 2 * xi + yi
    dev = 4 * xi + 2 * yi + ci
    _, L, D = x.shape
    C = ctx.shape[1]
    Ds = D // N_CHIPS
    Nm = mod_w.shape[-1]
    Fh = ffn_w13.shape[-1]
    Fq = ffn_w2.shape[2]
    assert ffn_w13.shape[:2] == (2, 2) and Fh == 2 * Fq and L % (GQ * GRID_W) == 0 and L // GRID_W >= GK and GQ == NA_KH // 2
    x0, ctx0, tgt = x[0], ctx[0], loss_target[0]

    pad = lambda a: jnp.pad(a, ((0, 0), (0, D - a.shape[1])))
    pack1 = jnp.concatenate([c, pad(norm_g.reshape(6, Ds)), pad(conv_w.reshape(3, Ds)), jnp.zeros((6, D), F32)], axis=0)
    g1 = _small_all_gather(pack1, "ag_cond")
    cond = jnp.concatenate([g1[:, 0], c_ctx[None], jnp.zeros((7, D), F32)], axis=0)
    norm_full = jnp.concatenate([g1[2 * k, 1:7, :Ds] for k in range(N_CHIPS)], axis=1).reshape(2, 3, D)
    convw_full = jnp.concatenate([g1[2 * k, 7:10, :Ds] for k in range(N_CHIPS)], axis=1)

    mod_b_loc = lax.dynamic_slice_in_dim(mod_b, chip * Nm, Nm, axis=1).reshape(2, 1, Nm)
    m_loc = _mod_fwd(cond, mod_w, mod_b_loc, "mod_fwd")
    g2 = _small_all_gather(m_loc.reshape(32, Nm), "ag_mod")
    m_all = jnp.concatenate([g2[2 * k] for k in range(N_CHIPS)], axis=1).reshape(2, 16, N_MOD, D)
    m_lat = lax.dynamic_index_in_dim(m_all, dev, axis=1, keepdims=False)
    m_ctx = m_all[:, 8]

    def prm(mods, layer, base, gain_idx, extra=()):
        return _prm([mods[layer, base], mods[layer, base + 1], mods[layer, base + 2], norm_full[layer, gain_idx],
                     *extra], D)

    def shard_bf16(w, name):
        return _cast_bf16(w.reshape(-1, w.shape[-1]), name).reshape(-1, *w.shape[-2:])

    w13s, w2s = shard_bf16(ffn_w13, "cast_w13"), shard_bf16(ffn_w2, "cast_w2")
    eins, eouts = shard_bf16(even_w_in, "cast_ein"), shard_bf16(even_w_out, "cast_eout")
    cins, couts = shard_bf16(conv_w_in, "cast_cin"), shard_bf16(conv_w_out, "cast_cout")
    ffn_shards = [[w13s[t:t + 1], w2s[t:t + 1]] for t in range(4)]

    def ffn_weights(w13g, w2g):
        return w13g.reshape(1, 4, D, Fh), w2g

    wf = [ffn_weights(*_gather_shards(ffn_shards[0], "ag_ffn0")), None, None, None]
    pos = jnp.stack([chip, ci]).astype(jnp.int32)

    p_f1 = prm(m_lat, 0, 0, 0)
    p_mx = prm(m_lat, 0, 3, 1)
    p_f2 = prm(m_lat, 0, 6, 2)
    p_g1 = prm(m_lat, 1, 0, 0)
    p_cv = prm(m_lat, 1, 3, 1, extra=(convw_full[0], convw_full[1], convw_full[2]))
    p_g2 = prm(m_lat, 1, 6, 2)
    pc_f1 = prm(m_ctx, 0, 0, 0)
    pc_mx = prm(m_ctx, 0, 3, 1)

    x1, ab1, y1, (eing, eoutg) = _ffn_fwd(x0, p_f1, *wf[0], 0, "ffn_fwd_l0a", ride=_broadcast_ride([eins, eouts]))
    eing = eing.reshape(1, 4, D, NA_WIDTH)
    ctx1, abc, yc, _ = _ffn_fwd(ctx0, pc_f1, *wf[0], 0, "ffn_fwd_ctx")
    q, k, v, u, hn_mx = _even_in_fwd(x1, p_mx, eing, "even_in_fwd")
    _, k_c, v_c, _, hn_cx = _even_in_fwd(ctx1, pc_mx, eing, "even_in_ctx")
    bias = _bias_table(na_rpb[0], "bias_table")
    att, gathered = _attn_fwd(q, k, v, k_c, v_c, bias, "attn_fwd", ride=_broadcast_ride(ffn_shards[1]))
    wf[1] = ffn_weights(*gathered)
    pw_b = _cast_bf16(pool_w.reshape(-1, POOL_GROUP_DIM), "cast_poolw").reshape(4, POOL_GROUP_DIM, POOL_GROUP_DIM)
    pool, dmx = _pool_fwd(u, pw_b, pool_scale, "pool_fwd")
    x2, ymx = _even_out_fwd(x1, att, pool, p_mx, eoutg, "even_out_fwd")
    x3, ab2, y2, gathered = _ffn_fwd(x2, p_f2, *wf[1], 0, "ffn_fwd_l0b",
                                     ride=_broadcast_ride(ffn_shards[2] + [cins, couts]))
    wf[2] = ffn_weights(*gathered[:2])
    cing, coutg = gathered[2].reshape(1, 4, D, conv_w_in.shape[-1]), gathered[3]
    x4, ab3, y3, gathered = _ffn_fwd(x3, p_g1, *wf[2], 0, "ffn_fwd_l1a", ride=_broadcast_ride(ffn_shards[3]))
    wf[3] = ffn_weights(*gathered)
    x5, ycv, bcx = _conv_fwd(x4, p_cv, cing, coutg, "conv_fwd")
    x6, ab4, y4, _ = _ffn_fwd(x5, p_g2, *wf[3], 0, "ffn_fwd_l1b")

    def ffn_back(dout, xin, ab, yy, p, t, tag, init13=None, init2=None, ride=None, head=None):
        sv, gact = ab
        dx, dab, dy, hn, acc, carried = _ffn_bwd(dout, xin, sv, yy, p, *wf[t], 0, f"ffn_bwd_{tag}", ride=ride,
                                                 head=head)
        dw13 = _mm_tn(hn, dab, 4, False, f"dw13_{tag}", init=init13)
        dw2 = _mm_tn(gact, dy, 2, True, f"dw2_{tag}", init=init2)
        return dx, acc, dw13, dw2, carried

    dx5, acc_g2, dw13_3, dw2_3, _ = ffn_back(x6, x5, ab4, y4, p_g2, 3, "l1b", head=(tgt, final_g.reshape(1, D)))
    acc_head = acc_g2[4:6]
    loss = lax.psum(acc_head[1, 0], ("x", "y", "c"))
    s_a, sb_a = _pair_sums([dw13_3, dw2_3], pos, "l1b")
    dx4, dproj, h2, dycv, hn_cv, acc_cv, got_a = _conv_bwd(dx5, x4, ycv, bcx, p_cv, cing, coutg, "conv_bwd",
                                                           ride=_scatter_ride(sb_a))
    dcin = _mm_tn(hn_cv, dproj, 4, False, "dw_cin")
    dcout = _mm_tn(h2, dycv, 1, False, "dw_cout")
    s_b, sb_b = _pair_sums([dcin, dcout], pos, "conv")
    dx3, acc_g1, dw13_2, dw2_2, got_b = ffn_back(dx4, x3, ab3, y3, p_g1, 2, "l1a", ride=_scatter_ride(sb_b))
    s_c, sb_c = _pair_sums([dw13_2, dw2_2], pos, "l1a")
    dx2, acc_f2, dw13_1, dw2_1, got_c = ffn_back(dx3, x2, ab2, y2, p_f2, 1, "l0b", ride=_scatter_ride(sb_c))

    dymx, datt, dpool, acc_mxo = _even_out_bwd(dx2, ymx, p_mx, eoutg, "even_out_bwd")
    deout = jnp.concatenate([_mm_tn(att, dymx, 1, False, "dw_eout_att"),
                             _mm_tn(pool, dymx, 1, False, "dw_eout_pool")], axis=0)
    s_d, sb_d = _pair_sums([dw13_1, dw2_1, deout], pos, "l0b")
    du, dpoolw, acc_pool = _pool_bwd(dpool, dmx, pw_b, pool_scale, "pool_bwd")
    dq, dk, dv, dkc, dvc, dbias, got_d = _attn_bwd(q, k, v, k_c, v_c, bias, datt, "attn_bwd",
                                                   ride=_scatter_ride(sb_d))
    drpb = _rpb_grad(dbias, "rpb_grad")
    dx1, dstack, acc_mxi = _even_in_bwd(dx2, x1, dq, dk, dv, du, p_mx, eing,
                                        "even_in_bwd")
    zc = jnp.zeros((C, NA_WIDTH), F32)
    dctx1, dstack_c, accc_mx = _even_in_bwd(jnp.zeros((C, D), F32), ctx1, zc, dkc, dvc, zc,
                                            pc_mx, eing, "even_in_bwd_ctx")
    dein_c = _mm_tn(hn_cx, dstack_c, 4, False, "dw_ein_ctx")
    dein = _mm_tn(hn_mx, dstack, 4, False, "dw_ein", init=dein_c)
    s_e, sb_e = _pair_sums([dein], pos, "ein")
    _, accc_f1, dw13_c, dw2_c, _ = ffn_back(dctx1, ctx0, abc, yc, pc_f1, 0, "ctx")
    sv1, gact1 = ab1
    dx0, dab, dy, hn, acc_f1, _ = _ffn_bwd(dx1, x0, sv1, y1, p_f1, *wf[0], 0, "ffn_bwd_l0a")
    dw13_0, got_e = _mm_tn(hn, dab, 4, False, "dw13_l0a", init=dw13_c, ride=_scatter_ride(sb_e))
    s_f13, sb_f13 = _pair_sums([dw13_0], pos, "l0a_w13")
    dw2_0, got_f13 = _mm_tn(gact1, dy, 2, True, "dw2_l0a", init=dw2_c, ride=_scatter_ride(sb_f13))
    s_f2, sb_f2 = _pair_sums([dw2_0], pos, "l0a_w2")

    z1 = jnp.zeros((1, D), F32)
    dm_lat = jnp.concatenate([acc_f1[0:3], acc_mxi[0:2], acc_mxo[2:3], acc_f2[0:3],
                              acc_g1[0:3], acc_cv[0:3], acc_g2[0:3]], axis=0)
    dm_ctx = jnp.concatenate([accc_f1[0:3], accc_mx[0:2]] + [z1] * 13, axis=0)
    dnorm = jnp.concatenate([acc_f1[3:4] + accc_f1[3:4], acc_mxi[3:4] + accc_mx[3:4], acc_f2[3:4],
                             acc_g1[3:4], acc_cv[3:4], acc_g2[3:4]], axis=0)
    rpb_flat = jnp.pad(drpb.reshape(-1), (0, 4 * D - drpb.size)).reshape(4, D)
    pack3 = jnp.concatenate([dm_lat, dm_ctx, dnorm, acc_cv[4:7], acc_head[0:1], pad(acc_pool[0:1]), z1,
                             dpoolw.reshape(-1, D), rpb_flat, jnp.zeros((4, D), F32)], axis=0)
    g3 = _small_all_gather(pack3, "ag_small")
    tot = _sum_devices(g3, "sum_small")
    dm_all = jnp.concatenate([g3[:, 0:18].reshape(8, 2, N_MOD * D).transpose(1, 0, 2),
                              tot[18:36].reshape(2, 1, N_MOD * D), jnp.zeros((2, 7, N_MOD * D), F32)], axis=1)
    dm_loc = lax.dynamic_slice_in_dim(dm_all, chip * Nm, Nm, axis=2)
    g_mod_w, dsilu = _mod_bwd(cond, dm_loc, mod_w, "mod_bwd")
    g4 = _small_all_gather(dsilu, "ag_dsilu")
    g_mod_b, g_c_ctx = _mod_small_grads(dm_all, cond, g4, "mod_small")
    g_mod_b = g_mod_b.reshape(2, N_MOD * D)
    g_c_ctx = g_c_ctx.reshape(D)
    g_norm_full = tot[36:42].reshape(2, 3, D)
    g_norm = lax.dynamic_slice_in_dim(g_norm_full, chip * Ds, Ds, axis=2)
    g_conv_w = lax.dynamic_slice_in_dim(tot[42:45], chip * Ds, Ds, axis=1).reshape(1, 3, Ds)
    g_final = tot[45]
    g_pscale = tot[46:47, :pool_scale.shape[1]]
    g_poolw = tot[48:112].reshape(pool_w.shape)
    g_rpb = tot[112:116].reshape(-1)[:na_rpb.size].reshape(na_rpb.shape)

    r13_3, r2_3 = _joins(s_a, got_a, pos, "l1b")
    r_cin, r_cout = _joins(s_b, got_b, pos, "conv")
    r13_2, r2_2 = _joins(s_c, got_c, pos, "l1a")
    r13_1, r2_1, r_eout = _joins(s_d, got_d, pos, "l0b")
    (r_ein,) = _joins(s_e, got_e, pos, "ein")
    adamw_mod_w, got_f2 = _adamw(mod_w, g_mod_w, m_mod_w, v_mod_w, "adamw_mod_w", ride=_scatter_ride(sb_f2))
    (r13_0,) = _joins(s_f13, got_f13, pos, "l0a_w13")
    (r2_0,) = _joins(s_f2, got_f2, pos, "l0a_w2")
    g_w13 = jnp.stack([r13_0, r13_1, r13_2, r13_3]).reshape(ffn_w13.shape)
    g_w2 = jnp.stack([r2_0, r2_1, r2_2, r2_3]).reshape(ffn_w2.shape)
    g_ein, g_eout, g_cin, g_cout = r_ein[None], r_eout[None], r_cin[None], r_cout[None]

    grads = [g_c_ctx, g_mod_w, g_mod_b, g_norm, g_w13, g_w2, g_ein, g_eout, g_rpb, g_poolw, g_pscale, g_cin,
             g_conv_w, g_cout, g_final]
    weights = [c_ctx, mod_w, mod_b, norm_g, ffn_w13, ffn_w2, even_w_in, even_w_out, na_rpb, pool_w, pool_scale,
               conv_w_in, conv_w, conv_w_out, final_g]
    ms = [m_c_ctx, m_mod_w, m_mod_b, m_norm_g, m_ffn_w13, m_ffn_w2, m_even_w_in, m_even_w_out, m_na_rpb, m_pool_w,
          m_pool_scale, m_conv_w_in, m_conv_w, m_conv_w_out, m_final_g]
    vs = [v_c_ctx, v_mod_w, v_mod_b, v_norm_g, v_ffn_w13, v_ffn_w2, v_even_w_in, v_even_w_out, v_na_rpb, v_pool_w,
          v_pool_scale, v_conv_w_in, v_conv_w, v_conv_w_out, v_final_g]
    names = ["c_ctx", "mod_w", "mod_b", "norm_g", "ffn_w13", "ffn_w2", "even_w_in", "even_w_out", "na_rpb", "pool_w",
             "pool_scale", "conv_w_in", "conv_w", "conv_w_out", "final_g"]
    deltas, new_m, new_v = [], [], []
    for n, w, g, m, vv in zip(names, weights, grads, ms, vs):
        g = g.reshape(w.shape)
        if n == "mod_w":
            d, mn, vn = adamw_mod_w
        elif w.ndim == 1:
            d, mn, vn = (t.reshape(w.shape) for t in _adamw(w[None], g[None], m[None], vv[None], f"adamw_{n}"))
        else:
            d, mn, vn = _adamw(w, g, m, vv, f"adamw_{n}")
        deltas.append(d)
        new_m.append(mn)
        new_v.append(vn)
    grads = [g.reshape(w.shape) for g, w in zip(grads, weights)]
    return (loss, dx0[None], *grads, *deltas, *new_m, *new_v)
```

```python
import jax
import jax.numpy as jnp
from jax import lax
from jax.experimental import pallas as pl
from jax.experimental.pallas import tpu as pltpu

F32 = jnp.float32
BF16 = jnp.bfloat16
MESH = pl.DeviceIdType.MESH

GRID_W = 64
NA_HEADS = 8
NA_HEAD_DIM = 64
NA_KH = 8
NA_KW = 16
GQ = 4
GK = GQ + NA_KH
NA_WIDTH = NA_HEADS * NA_HEAD_DIM
POOL_WINDOWS = (2, 4, 8, 16)
POOL_GROUP_DIM = 128
N_MOD = 9
RMS_EPS = 1e-6
NEG_INF = -1e30
ADAM_LR, ADAM_B1, ADAM_B2, ADAM_EPS, ADAM_WD, ADAM_STEP = 0.001, 0.9, 0.999, 1e-08, 0.01, 10

HALO = 16
VMEM_LIMIT = 56 * 1024 * 1024
N_CHIPS = 4
N_DEV = 8


def _dot(a, b):
    return jnp.dot(a, b, preferred_element_type=F32)


def _dot_nt(a, b):
    return lax.dot_general(a, b, (((1,), (1,)), ((), ())), preferred_element_type=F32)


def _dot_tn(a, b):
    return lax.dot_general(a, b, (((0,), (0,)), ((), ())), preferred_element_type=F32)


def _sigmoid(a):
    return 1.0 / (1.0 + jnp.exp(-a))


def _sum0(v):
    return jnp.sum(v, axis=0, keepdims=True)


def _nm(x, g, shift, scale):
    r = lax.rsqrt(jnp.mean(x * x, axis=-1, keepdims=True) + RMS_EPS)
    xhat = x * r
    nrm = xhat * g
    return nrm * (1.0 + scale) + shift, xhat, r, nrm


def _nm_bwd(dhn, xhat, r, nrm, g, scale):
    dshift = _sum0(dhn)
    dscale = _sum0(dhn * nrm)
    dnrm = dhn * (1.0 + scale)
    dgn = _sum0(dnrm * xhat)
    dxh = dnrm * g
    dx = r * (dxh - xhat * jnp.mean(dxh * xhat, axis=-1, keepdims=True))
    return dx, dshift, dscale, dgn


def _acc_rows(acc_ref, first, rows):
    @pl.when(first)
    def _():
        acc_ref[...] = jnp.zeros(acc_ref.shape, acc_ref.dtype)
    for k, row in enumerate(rows):
        if row is not None:
            acc_ref[k:k + 1, :] += row


def _shift_rows(v, k):
    n = v.shape[0]
    k = k % n
    return v if k == 0 else pltpu.roll(v, k, 0)


def _tile(tm, w):
    return pl.BlockSpec((tm, w), lambda i: (i, 0))


def _full(shape):
    nd = len(shape)
    return pl.BlockSpec(shape, lambda i: (0,) * nd)


def _resident(block, imap):
    return pl.BlockSpec(block, imap, pipeline_mode=pl.Buffered(1))


def _halo_prev(tm, w):
    return pl.BlockSpec((HALO, w), lambda i: (jnp.maximum(i * (tm // HALO) - 1, 0), 0))


def _halo_next(tm, w, L):
    return pl.BlockSpec((HALO, w), lambda i: (jnp.minimum((i + 1) * (tm // HALO), L // HALO - 1), 0))


def _params(vmem=VMEM_LIMIT):
    return pltpu.CompilerParams(vmem_limit_bytes=vmem)


def _pick_rows(rows, cols, itemsize=4, target=1 << 20):
    best = None
    for t in range(8, rows + 1, 8):
        if rows % t == 0 and t * cols * itemsize <= target:
            best = t
    return best if best is not None else rows


def _ext(prev, cur, nxt, i, nt):
    prev = jnp.where(i > 0, prev, jnp.zeros_like(prev))
    nxt = jnp.where(i < nt - 1, nxt, jnp.zeros_like(nxt))
    return jnp.concatenate([prev, cur, nxt], axis=0)


def _cast_bf16(a2d, name):
    rows, cols = a2d.shape
    tr = _pick_rows(rows, cols)

    def body(a_ref, o_ref):
        o_ref[...] = a_ref[...].astype(BF16)

    return pl.pallas_call(
        body, name=name, grid=(rows // tr,), in_specs=[_tile(tr, cols)], out_specs=_tile(tr, cols),
        out_shape=jax.ShapeDtypeStruct((rows, cols), BF16))(a2d)


def _sum_devices(g, name):
    n, rows, cols = g.shape
    tr = _pick_rows(rows, cols, target=1 << 18)

    def body(g_ref, o_ref):
        s = g_ref[0]
        for d in range(1, n):
            s = s + g_ref[d]
        o_ref[...] = s

    return pl.pallas_call(
        body, name=name, grid=(rows // tr,), in_specs=[pl.BlockSpec((n, tr, cols), lambda i: (0, i, 0))],
        out_specs=_tile(tr, cols), out_shape=jax.ShapeDtypeStruct((rows, cols), F32))(g)


def _adamw(w, g, m, v, name, ride=None):
    shape = w.shape
    cols = shape[-1]
    rows = w.size // cols
    w2, g2, m2, v2 = (t.reshape(rows, cols) for t in (w, g, m, v))
    tr = _pick_rows(rows, cols)
    c1 = 1.0 - ADAM_B1 ** ADAM_STEP
    c2 = 1.0 - ADAM_B2 ** ADAM_STEP

    def body(w_ref, g_ref, m_ref, v_ref, d_ref, mo_ref, vo_ref):
        gg = g_ref[...]
        mn = ADAM_B1 * m_ref[...] + (1.0 - ADAM_B1) * gg
        vn = ADAM_B2 * v_ref[...] + (1.0 - ADAM_B2) * (gg * gg)
        d_ref[...] = -ADAM_LR * ((mn / c1) / (jnp.sqrt(vn / c2) + ADAM_EPS) + ADAM_WD * w_ref[...])
        mo_ref[...] = mn
        vo_ref[...] = vn

    outs, carried = _ride_call(
        body, ride, (w2, g2, m2, v2), name=name, grid=(rows // tr,), in_specs=[_tile(tr, cols)] * 4,
        out_specs=[_tile(tr, cols)] * 3, out_shape=[jax.ShapeDtypeStruct((rows, cols), F32)] * 3)
    outs = tuple(o.reshape(shape) for o in outs)
    return outs if ride is None else (outs, carried)


def _mesh_pos():
    x, y, c = lax.axis_index("x"), lax.axis_index("y"), lax.axis_index("c")
    chips = [(1 - x, y), (x, 1 - y), (1 - x, 1 - y)]
    return x, y, c, chips


def _hbm_specs(n):
    return [pl.BlockSpec(memory_space=pltpu.HBM)] * n


def _small_all_gather(v, name):
    rows, w = v.shape

    def body(x_ref, out_ref, send_sems, recv_sems, local_sem):
        x, y, c, chips = _mesh_pos()
        me, sibling = (x, y, c), (x, y, 1 - c)

        def blk(px, py, pc):
            return out_ref.at[4 * px + 2 * py + pc]

        def copy(k, block, to, src=None):
            return pltpu.make_async_remote_copy(
                src_ref=blk(*block) if src is None else src, dst_ref=blk(*block),
                send_sem=send_sems.at[k], recv_sem=recv_sems.at[k], device_id=to, device_id_type=MESH)

        mine = pltpu.make_async_copy(x_ref, blk(*me), local_sem)
        mine.start()
        first = [copy(0, me, sibling, src=x_ref)]
        first += [copy(1 + j, me, (*chip, c), src=x_ref) for j, chip in enumerate(chips)]
        for cp in first:
            cp.start()
        passed = [copy(4 + j, (*chip, c), sibling) for j, chip in enumerate(chips)]
        for j, chip in enumerate(chips):
            copy(1 + j, (*chip, c), me).wait_recv()
            passed[j].start()
        copy(0, sibling, me).wait_recv()
        for j, chip in enumerate(chips):
            copy(4 + j, (*chip, 1 - c), me).wait_recv()
        for cp in first + passed:
            cp.wait_send()
        mine.wait()

    return pl.pallas_call(
        body, name=name, out_shape=jax.ShapeDtypeStruct((N_DEV, rows, w), v.dtype),
        in_specs=[pl.BlockSpec(memory_space=pltpu.VMEM)], out_specs=pl.BlockSpec(memory_space=pltpu.VMEM),
        scratch_shapes=[pltpu.SemaphoreType.DMA((7,)), pltpu.SemaphoreType.DMA((7,)), pltpu.SemaphoreType.DMA],
    )(v)


def _gather_shards(shards, name):
    n = len(shards)

    def body(*refs):
        ins, outs = refs[:n], refs[n:2 * n]
        send_sems, recv_sems, local_sems = refs[2 * n:]
        x, y, c, chips = _mesh_pos()
        k = 2 * x + y
        sibling = (x, y, 1 - c)

        def window(t, chip_k, half):
            r = ins[t].shape[1]
            return outs[t].at[:, pl.ds(chip_k * r + half * (r // 2), r // 2), :]

        def copy(t, j, chip_k, half, to, src=None):
            return pltpu.make_async_remote_copy(
                src_ref=window(t, chip_k, half) if src is None else src, dst_ref=window(t, chip_k, half),
                send_sem=send_sems.at[6 * t + j], recv_sem=recv_sems.at[6 * t + j], device_id=to, device_id_type=MESH)

        started, local = [], []
        for t in range(n):
            r = ins[t].shape[1]
            lc = pltpu.make_async_copy(ins[t], outs[t].at[:, pl.ds(k * r, r), :], local_sems.at[t])
            lc.start()
            local.append(lc)
            src = ins[t].at[:, pl.ds(c * (r // 2), r // 2), :]
            for j, chip in enumerate(chips):
                cp = copy(t, j, k, c, (*chip, c), src=src)
                cp.start()
                started.append(cp)
        for t in range(n):
            for j, chip in enumerate(chips):
                kj = 2 * chip[0] + chip[1]
                copy(t, j, kj, c, sibling).wait_recv()
                cp = copy(t, 3 + j, kj, c, sibling)
                cp.start()
                started.append(cp)
        for t in range(n):
            for j, chip in enumerate(chips):
                kj = 2 * chip[0] + chip[1]
                copy(t, 3 + j, kj, 1 - c, sibling).wait_recv()
        for cp in started:
            cp.wait_send()
        for lc in local:
            lc.wait()

    out_shape = [jax.ShapeDtypeStruct((s.shape[0], N_CHIPS * s.shape[1], s.shape[2]), s.dtype) for s in shards]
    return pl.pallas_call(
        body, name=name, out_shape=out_shape, in_specs=_hbm_specs(n), out_specs=_hbm_specs(n),
        scratch_shapes=[pltpu.SemaphoreType.DMA((6 * n,)), pltpu.SemaphoreType.DMA((6 * n,)),
                        pltpu.SemaphoreType.DMA((n,))],
    )(*shards)


def _chunk_rows(h, w):
    best = 16
    for t in range(16, h + 1, 16):
        if h % t == 0 and t * w * 4 <= (2 << 20):
            best = t
    return best


def _pair_sum(part, pos, name):
    _, h, w = part.shape
    cr = _chunk_rows(h, w)
    nc = h // cr
    n = 4 * nc
    slots = 4

    def body(pos_ref, own_ref, send_ref, s_ref, sb_ref, stage, rbuf, send_sems, recv_sems):
        x, y, c, _ = _mesh_pos()
        k = pl.program_id(0)

        def copy(chunk):
            return pltpu.make_async_remote_copy(
                src_ref=stage.at[chunk % 2], dst_ref=rbuf.at[chunk % slots], send_sem=send_sems.at[chunk % 2],
                recv_sem=recv_sems.at[chunk % slots], device_id=(x, y, 1 - c), device_id_type=MESH)

        @pl.when(k >= 2)
        def _():
            copy(k - 2).wait_send()

        @pl.when(k < n)
        def _():
            stage[k % 2] = send_ref[...]
            copy(k).start()

        @pl.when(k > 0)
        def _():
            copy(k - 1).wait_recv()
            s = own_ref[...] + rbuf[(k - 1) % slots]
            s_ref[...] = s
            sb_ref[...] = s.astype(BF16)

        @pl.when(k == n)
        def _():
            copy(k - 1).wait_send()

    def own(k, p):
        j = jnp.maximum(k - 1, 0)
        return ((2 * (j // nc) + p[1]) * nc + j % nc, 0)

    def send(k, p):
        j = jnp.minimum(k, n - 1)
        return ((2 * (j // nc) + 1 - p[1]) * nc + j % nc, 0)

    grid_spec = pltpu.PrefetchScalarGridSpec(
        num_scalar_prefetch=1, grid=(n + 1,),
        in_specs=[pl.BlockSpec((cr, w), own), pl.BlockSpec((cr, w), send)],
        out_specs=[pl.BlockSpec((cr, w), lambda k, p: (jnp.maximum(k - 1, 0), 0))] * 2,
        scratch_shapes=[pltpu.VMEM((2, cr, w), F32), pltpu.VMEM((slots, cr, w), F32),
                        pltpu.SemaphoreType.DMA((2,)), pltpu.SemaphoreType.DMA((slots,))])
    part2 = part.reshape(8 * h, w)
    s, sb = pl.pallas_call(
        body, name=name, grid_spec=grid_spec, compiler_params=_params(),
        out_shape=[jax.ShapeDtypeStruct((4 * h, w), F32), jax.ShapeDtypeStruct((4 * h, w), BF16)],
    )(pos, part2, part2)
    return s.reshape(4, h, w), sb.reshape(4, h, w)


class _Ride:
    def __init__(self, ins, out_shape, sems, copies):
        self.ins, self.out_shape, self.sems, self.copies = list(ins), list(out_shape), list(sems), copies

    def start(self, ins, outs, sems):
        sends, _, _, local = self.copies(ins, outs, sems)
        for cp in local + sends:
            cp.start()

    def finish(self, ins, outs, sems):
        _, recvs, sends, local = self.copies(ins, outs, sems)
        for cp in recvs:
            cp.wait_recv()
        for cp in sends:
            cp.wait_send()
        for cp in local:
            cp.wait()


def _scatter_ride(sums_bf16):
    n = len(sums_bf16)

    def copies(ins, outs, sems):
        send_sems, recv_sems = sems
        x, y, c, chips = _mesh_pos()
        cps = [pltpu.make_async_remote_copy(
            src_ref=ins[t].at[2 * chip[0] + chip[1]], dst_ref=outs[t].at[j],
            send_sem=send_sems.at[3 * t + j], recv_sem=recv_sems.at[3 * t + j],
            device_id=(*chip, c), device_id_type=MESH) for t in range(n) for j, chip in enumerate(chips)]
        return cps, cps, cps, []

    return _Ride(sums_bf16, [jax.ShapeDtypeStruct((3,) + s.shape[1:], BF16) for s in sums_bf16],
                 [pltpu.SemaphoreType.DMA((3 * n,)), pltpu.SemaphoreType.DMA((3 * n,))], copies)


def _broadcast_ride(shards):
    n = len(shards)

    def copies(ins, outs, sems):
        send_sems, recv_sems, local_sems = sems
        x, y, c, chips = _mesh_pos()
        k = 2 * x + y
        sends, recvs, local = [], [], []
        for t in range(n):
            r = ins[t].shape[1]
            h = r // 2
            local.append(pltpu.make_async_copy(ins[t], outs[t].at[:, pl.ds(k * r, r), :], local_sems.at[t]))
            src = ins[t].at[:, pl.ds(c * h, h), :]
            mine = outs[t].at[:, pl.ds(k * r + c * h, h), :]
            for j, chip in enumerate(chips):
                kj = 2 * chip[0] + chip[1]
                for d in range(2):
                    sends.append(pltpu.make_async_remote_copy(
                        src_ref=src, dst_ref=mine, send_sem=send_sems.at[6 * t + 2 * j + d],
                        recv_sem=recv_sems.at[6 * t + 2 * j + c], device_id=(*chip, d), device_id_type=MESH))
                    theirs = outs[t].at[:, pl.ds(kj * r + d * h, h), :]
                    recvs.append(pltpu.make_async_remote_copy(
                        src_ref=theirs, dst_ref=theirs, send_sem=send_sems.at[6 * t + 2 * j + d],
                        recv_sem=recv_sems.at[6 * t + 2 * j + d], device_id=(*chip, d), device_id_type=MESH))
        return sends, recvs, sends, local

    return _Ride(shards, [jax.ShapeDtypeStruct((s.shape[0], N_CHIPS * s.shape[1], s.shape[2]), s.dtype) for s in shards],
                 [pltpu.SemaphoreType.DMA((6 * n,)), pltpu.SemaphoreType.DMA((6 * n,)), pltpu.SemaphoreType.DMA((n,))],
                 copies)


def _ride_call(body, ride, args, *, name, grid, in_specs, out_specs, out_shape, compiler_params=None):
    in_specs, out_specs, out_shape = list(in_specs), list(out_specs), list(out_shape)
    if ride is None:
        res = pl.pallas_call(body, name=name, grid=grid, in_specs=in_specs, out_specs=out_specs, out_shape=out_shape,
                             compiler_params=compiler_params)(*args)
        return list(res), []
    ni, no, ri, ro = len(in_specs), len(out_specs), len(ride.ins), len(ride.out_shape)

    def at_step(pick):
        hit = None
        for d, n in enumerate(grid):
            here = pl.program_id(d) == pick(n)
            hit = here if hit is None else hit & here
        return hit

    def carried(*refs):
        ins, rins = refs[:ni], refs[ni:ni + ri]
        outs, routs = refs[ni + ri:ni + ri + no], refs[ni + ri + no:ni + ri + no + ro]
        sems = refs[ni + ri + no + ro:]

        @pl.when(at_step(lambda n: 0))
        def _():
            ride.start(rins, routs, sems)

        body(*ins, *outs)

        @pl.when(at_step(lambda n: n - 1))
        def _():
            ride.finish(rins, routs, sems)

    res = pl.pallas_call(
        carried, name=name, grid=grid, in_specs=in_specs + _hbm_specs(ri), out_specs=out_specs + _hbm_specs(ro),
        out_shape=out_shape + ride.out_shape, scratch_shapes=ride.sems, compiler_params=compiler_params,
    )(*args, *ride.ins)
    return list(res[:no]), list(res[no:])


def _sum_and_join(sums, got, pos, name):
    _, h, w = sums.shape
    cr = _chunk_rows(h, w)

    def body(pos_ref, mine_ref, got_ref, o_ref, ebuf, rbuf, send_sems, recv_sems):
        x, y, c, _ = _mesh_pos()
        slot = pl.program_id(0) % 2
        e = mine_ref[...]
        for j in range(3):
            e = e + got_ref[j].astype(F32)
        ebuf[slot] = e
        cp = pltpu.make_async_remote_copy(
            src_ref=ebuf.at[slot], dst_ref=rbuf.at[slot], send_sem=send_sems.at[slot], recv_sem=recv_sems.at[slot],
            device_id=(x, y, 1 - c), device_id_type=MESH)
        cp.start()
        o_ref[pos_ref[1]] = e
        cp.wait_recv()
        o_ref[1 - pos_ref[1]] = rbuf[slot]
        cp.wait_send()

    grid_spec = pltpu.PrefetchScalarGridSpec(
        num_scalar_prefetch=1, grid=(h // cr,),
        in_specs=[pl.BlockSpec((None, cr, w), lambda i, p: (p[0], i, 0)),
                  pl.BlockSpec((3, cr, w), lambda i, p: (0, i, 0))],
        out_specs=pl.BlockSpec((2, cr, w), lambda i, p: (0, i, 0)),
        scratch_shapes=[pltpu.VMEM((2, cr, w), F32), pltpu.VMEM((2, cr, w), F32),
                        pltpu.SemaphoreType.DMA((2,)), pltpu.SemaphoreType.DMA((2,))])
    return pl.pallas_call(
        body, name=name, grid_spec=grid_spec, compiler_params=_params(),
        out_shape=jax.ShapeDtypeStruct((2, h, w), F32),
    )(pos, sums, got)


def _pair_sums(parts, pos, tag):
    pairs = [_pair_sum(p.reshape(8, p.shape[0] // 8, p.shape[1]), pos, f"rs_pair_{tag}_{t}")
             for t, p in enumerate(parts)]
    return [s for s, _ in pairs], [sb for _, sb in pairs]


def _joins(sums, got, pos, tag):
    out = []
    for t, (s, r) in enumerate(zip(sums, got)):
        full = _sum_and_join(s, r, pos, f"rs_join_{tag}_{t}")
        out.append(full.reshape(2 * full.shape[1], full.shape[2]))
    return out


def _ffn_fwd(x, prm, w13g, w2g, t, name, tm=512, ride=None):
    L, D = x.shape
    Fh = w13g.shape[-1]
    tm = min(tm, L)

    def body(x_ref, p_ref, w13_ref, w2_ref, xo_ref, sv_ref, g_ref, y_ref):
        xv = x_ref[...]
        hn, _, _, _ = _nm(xv, p_ref[3:4, :], p_ref[0:1, :], p_ref[1:2, :])
        hb = hn.astype(BF16)
        acc = jnp.zeros((tm, D), F32)
        for j in range(2):
            a = _dot(hb, w13_ref[j])
            b = _dot(hb, w13_ref[2 + j])
            sg = _sigmoid(a)
            sa = a * sg
            sv_ref[:, j * Fh:(j + 1) * Fh] = sa.astype(BF16)
            sv_ref[:, (2 + j) * Fh:(3 + j) * Fh] = (b * (sg * (1.0 + a * (1.0 - sg)))).astype(BF16)
            g = (sa * b).astype(BF16)
            g_ref[:, j * Fh:(j + 1) * Fh] = g
            acc = acc + _dot(g, w2_ref[j * Fh:(j + 1) * Fh, :])
        y_ref[...] = acc.astype(BF16)
        xo_ref[...] = xv + (0.5 * p_ref[2:3, :]) * acc

    res, carried = _ride_call(
        body, ride, (x, prm, w13g, w2g), name=name, grid=(L // tm,),
        in_specs=[_tile(tm, D), _full((8, D)),
                  _resident((None, 4, D, Fh), lambda i: (t, 0, 0, 0)),
                  _resident((None, 2 * Fh, D), lambda i: (t, 0, 0))],
        out_specs=[_tile(tm, D), _tile(tm, 4 * Fh), _tile(tm, 2 * Fh), _tile(tm, D)],
        out_shape=[jax.ShapeDtypeStruct((L, D), F32), jax.ShapeDtypeStruct((L, 4 * Fh), BF16),
                   jax.ShapeDtypeStruct((L, 2 * Fh), BF16), jax.ShapeDtypeStruct((L, D), BF16)],
        compiler_params=_params())
    xo, sv, g, y = res
    return xo, (sv, g), y, carried


def _head_grad(xo, tgt, fg):
    D = xo.shape[-1]
    r = lax.rsqrt(jnp.mean(xo * xo, axis=-1, keepdims=True) + RMS_EPS)
    xhat = xo * r
    err = xhat * fg - tgt
    loss = 0.5 * jnp.sum(jnp.mean(err * err, axis=-1, keepdims=True), axis=0, keepdims=True)
    dy = err * (1.0 / D)
    dxh = dy * fg
    return r * (dxh - xhat * jnp.mean(dxh * xhat, axis=-1, keepdims=True)), _sum0(dy * xhat), loss


def _ffn_bwd(dout, x, sv, y, prm, w13g, w2g, t, name, tm=256, ride=None, head=None):
    L, D = x.shape
    Fh = w13g.shape[-1]
    tm = min(tm, L)

    def body(do_ref, x_ref, sv_ref, y_ref, p_ref, w13_ref, w2_ref, *rest):
        dx_ref, dab_ref, dy_ref, hn_ref, acc_ref = rest[-5:]
        i = pl.program_id(0)
        head_rows = []
        if head is None:
            do = do_ref[...]
        else:
            do, dfg, loss = _head_grad(do_ref[...], rest[0][...], rest[1][...])
            head_rows = [dfg, jnp.broadcast_to(loss, (1, D))]
        gain, shift, scale, gate = p_ref[3:4, :], p_ref[0:1, :], p_ref[1:2, :], p_ref[2:3, :]
        hn, xhat, r, nrm = _nm(x_ref[...], gain, shift, scale)
        hn_ref[...] = hn.astype(BF16)
        dgate = 0.5 * _sum0(do * y_ref[...].astype(F32))
        dyb = ((0.5 * gate) * do).astype(BF16)
        dy_ref[...] = dyb
        dhn = jnp.zeros((tm, D), F32)
        for j in range(2):
            dg = _dot_nt(dyb, w2_ref[j * Fh:(j + 1) * Fh, :])
            da = (dg * sv_ref[:, (2 + j) * Fh:(3 + j) * Fh].astype(F32)).astype(BF16)
            db = (dg * sv_ref[:, j * Fh:(j + 1) * Fh].astype(F32)).astype(BF16)
            dab_ref[:, j * Fh:(j + 1) * Fh] = da
            dab_ref[:, (2 + j) * Fh:(3 + j) * Fh] = db
            dhn = dhn + _dot_nt(da, w13_ref[j]) + _dot_nt(db, w13_ref[2 + j])
        dx, dshift, dscale, dgn = _nm_bwd(dhn, xhat, r, nrm, gain, scale)
        dx_ref[...] = do + dx
        _acc_rows(acc_ref, i == 0, [dshift, dscale, dgate, dgn] + head_rows)

    head_args = () if head is None else head
    head_specs = [] if head is None else [_tile(tm, D), _full((1, D))]
    res, carried = _ride_call(
        body, ride, (dout, x, sv, y, prm, w13g, w2g, *head_args), name=name, grid=(L // tm,),
        in_specs=[_tile(tm, D), _tile(tm, D), _tile(tm, 4 * Fh), _tile(tm, D), _full((8, D)),
                  _resident((None, 4, D, Fh), lambda i: (t, 0, 0, 0)),
                  _resident((None, 2 * Fh, D), lambda i: (t, 0, 0))] + head_specs,
        out_specs=[_tile(tm, D), _tile(tm, 4 * Fh), _tile(tm, D), _tile(tm, D), _full((8, D))],
        out_shape=[jax.ShapeDtypeStruct((L, D), F32), jax.ShapeDtypeStruct((L, 4 * Fh), BF16),
                   jax.ShapeDtypeStruct((L, D), BF16), jax.ShapeDtypeStruct((L, D), BF16),
                   jax.ShapeDtypeStruct((8, D), F32)],
        compiler_params=_params())
    return (*res, carried)


def _mm_tn(a, b, slabs, a_slabbed, name, init=None, tl=1024, ride=None):
    L = a.shape[0]
    ka = a.shape[1] // slabs if a_slabbed else a.shape[1]
    nb = b.shape[1] if a_slabbed else b.shape[1] // slabs
    tl = min(tl, L)
    has_init = init is not None

    def body(a_ref, b_ref, *rest):
        o_ref = rest[-1]
        step = pl.program_id(1)

        @pl.when(step == 0)
        def _():
            o_ref[...] = rest[0][...] if has_init else jnp.zeros((ka, nb), F32)

        o_ref[...] += _dot_tn(a_ref[...], b_ref[...])

    in_specs = [pl.BlockSpec((tl, ka), (lambda s, l: (l, s)) if a_slabbed else (lambda s, l: (l, 0))),
                pl.BlockSpec((tl, nb), (lambda s, l: (l, 0)) if a_slabbed else (lambda s, l: (l, s)))]
    args = [a, b]
    if has_init:
        in_specs.append(pl.BlockSpec((ka, nb), lambda s, l: (s, 0)))
        args.append(init)
    res, carried = _ride_call(
        body, ride, args, name=name, grid=(slabs, L // tl), in_specs=in_specs,
        out_specs=[pl.BlockSpec((ka, nb), lambda s, l: (s, 0))],
        out_shape=[jax.ShapeDtypeStruct((slabs * ka, nb), F32)], compiler_params=_params())
    return res[0] if ride is None else (res[0], carried)


def _even_in_fwd(x, prm, wing, name, tm=512):
    L, D = x.shape
    W = wing.shape[-1]
    tm = min(tm, L)

    def body(x_ref, p_ref, w_ref, q_ref, k_ref, v_ref, u_ref, hn_ref):
        hn, _, _, _ = _nm(x_ref[...], p_ref[3:4, :], p_ref[0:1, :], p_ref[1:2, :])
        hb = hn.astype(BF16)
        hn_ref[...] = hb
        q_ref[...] = _dot(hb, w_ref[0]).astype(BF16)
        k_ref[...] = _dot(hb, w_ref[1]).astype(BF16)
        v_ref[...] = _dot(hb, w_ref[2]).astype(BF16)
        u_ref[...] = _dot(hb, w_ref[3])

    return pl.pallas_call(
        body, name=name, grid=(L // tm,),
        in_specs=[_tile(tm, D), _full((8, D)), _resident((None, 4, D, W), lambda i: (0, 0, 0, 0))],
        out_specs=[_tile(tm, W)] * 4 + [_tile(tm, D)],
        out_shape=[jax.ShapeDtypeStruct((L, W), BF16)] * 3 + [jax.ShapeDtypeStruct((L, W), F32),
                                                              jax.ShapeDtypeStruct((L, D), BF16)],
        compiler_params=_params())(x, prm, wing)


def _even_in_bwd(dout, x, dq, dk, dv, du, prm, wing, name, tm=512):
    L, D = x.shape
    W = wing.shape[-1]
    tm = min(tm, L)

    def body(do_ref, x_ref, dq_ref, dk_ref, dv_ref, du_ref, p_ref, w_ref, dx_ref, ds_ref, acc_ref):
        i = pl.program_id(0)
        gain, shift, scale = p_ref[3:4, :], p_ref[0:1, :], p_ref[1:2, :]
        _, xhat, r, nrm = _nm(x_ref[...], gain, shift, scale)
        dhn = jnp.zeros((tm, D), F32)
        for s, ref in enumerate((dq_ref, dk_ref, dv_ref, du_ref)):
            d = ref[...].astype(BF16)
            ds_ref[:, s * W:(s + 1) * W] = d
            dhn = dhn + _dot_nt(d, w_ref[s])
        dx, dshift, dscale, dgn = _nm_bwd(dhn, xhat, r, nrm, gain, scale)
        dx_ref[...] = do_ref[...] + dx
        _acc_rows(acc_ref, i == 0, [dshift, dscale, None, dgn])

    return pl.pallas_call(
        body, name=name, grid=(L // tm,),
        in_specs=[_tile(tm, D), _tile(tm, D)] + [_tile(tm, W)] * 4 +
                 [_full((8, D)), _resident((None, 4, D, W), lambda i: (0, 0, 0, 0))],
        out_specs=[_tile(tm, D), _tile(tm, 4 * W), _full((8, D))],
        out_shape=[jax.ShapeDtypeStruct((L, D), F32), jax.ShapeDtypeStruct((L, 4 * W), BF16),
                   jax.ShapeDtypeStruct((8, D), F32)],
        compiler_params=_params())(dout, x, dq, dk, dv, du, prm, wing)


def _even_out_fwd(x, att, pool, prm, woutg, name, tm=512):
    L, D = x.shape
    W = D // 2
    tm = min(tm, L)

    def body(x_ref, a_ref, p_ref, prm_ref, w_ref, xo_ref, y_ref):
        yv = _dot(a_ref[...], w_ref[0:W, :]) + _dot(p_ref[...], w_ref[W:2 * W, :])
        y_ref[...] = yv.astype(BF16)
        xo_ref[...] = x_ref[...] + prm_ref[2:3, :] * yv

    return pl.pallas_call(
        body, name=name, grid=(L // tm,),
        in_specs=[_tile(tm, D), _tile(tm, W), _tile(tm, W), _full((8, D)),
                  _resident((None, D, D), lambda i: (0, 0, 0))],
        out_specs=[_tile(tm, D), _tile(tm, D)],
        out_shape=[jax.ShapeDtypeStruct((L, D), F32), jax.ShapeDtypeStruct((L, D), BF16)],
        compiler_params=_params())(x, att, pool, prm, woutg)


def _even_out_bwd(dout, y, prm, woutg, name, tm=512):
    L, D = dout.shape
    W = D // 2
    tm = min(tm, L)

    def body(do_ref, y_ref, p_ref, w_ref, dy_ref, da_ref, dp_ref, acc_ref):
        i = pl.program_id(0)
        do = do_ref[...]
        dgate = _sum0(do * y_ref[...].astype(F32))
        dyb = (p_ref[2:3, :] * do).astype(BF16)
        dy_ref[...] = dyb
        da_ref[...] = _dot_nt(dyb, w_ref[0:W, :]).astype(BF16)
        dp_ref[...] = _dot_nt(dyb, w_ref[W:2 * W, :])
        _acc_rows(acc_ref, i == 0, [None, None, dgate])

    return pl.pallas_call(
        body, name=name, grid=(L // tm,),
        in_specs=[_tile(tm, D), _tile(tm, D), _full((8, D)), _resident((None, D, D), lambda i: (0, 0, 0))],
        out_specs=[_tile(tm, D), _tile(tm, W), _tile(tm, W), _full((8, D))],
        out_shape=[jax.ShapeDtypeStruct((L, D), BF16), jax.ShapeDtypeStruct((L, W), BF16),
                   jax.ShapeDtypeStruct((L, W), F32), jax.ShapeDtypeStruct((8, D), F32)],
        compiler_params=_params())(dout, y, prm, woutg)


def _group_ri(variant, qr, kr):
    first_key = (0, qr, GK - NA_KH)[variant]
    if not first_key <= kr < first_key + NA_KH:
        return None
    return kr - qr + (NA_KH - 1, NA_KH - 1 - NA_KH // 2, NA_KH - 1 - (GK - GQ))[variant]


HEADS_PER_BLOCK = 128 // NA_HEAD_DIM


def _bias_table(rpb, name):
    H = rpb.shape[0]
    hpb = HEADS_PER_BLOCK
    nri, nci = 2 * NA_KH - 1, 2 * NA_KW - 1
    col = jnp.arange(GRID_W)
    rel = (col[None, :] - col[:, None] + (NA_KW - 1)).reshape(1, -1)
    onehot = (rel == jnp.arange(32)[:, None]).astype(F32)
    cs = jnp.clip(col - NA_KW // 2, 0, GRID_W - NA_KW)
    ok = ((col[None, :] >= cs[:, None]) & (col[None, :] < cs[:, None] + NA_KW)).astype(F32).reshape(1, -1)
    by_lane_block = rpb.reshape(H // hpb, hpb, nri, nci).transpose(1, 0, 2, 3)
    rpb2 = jnp.pad(by_lane_block.reshape(H * nri, nci), ((0, 0), (0, 32 - nci)))

    def body(r_ref, e_ref, m_ref, o_ref):
        t = jnp.dot(r_ref[...], e_ref[...], preferred_element_type=F32, precision=lax.Precision.HIGHEST)
        o_ref[...] = jnp.where(m_ref[...] > 0.0, t, NEG_INF)

    tab = pl.pallas_call(body, name=name, out_shape=jax.ShapeDtypeStruct((H * nri, GRID_W * GRID_W), F32))(
        rpb2, onehot, ok)
    tab = tab.reshape(hpb, H // hpb, nri, GRID_W, GRID_W)
    outside = jnp.full((H // hpb, GRID_W, GRID_W), NEG_INF, F32)
    variants = []
    for variant in range(3):
        rows = []
        for h in range(hpb):
            for qr in range(GQ):
                ris = [_group_ri(variant, qr, kr) for kr in range(GK)]
                rows.append(jnp.concatenate([outside if ri is None else tab[h, :, ri] for ri in ris], axis=2))
        variants.append(jnp.concatenate(rows, axis=1))
    return jnp.stack(variants, axis=1)


def _attn_probs(q, kw, kc, bias, scale):
    s_w = _dot_nt(q, kw) * scale + bias
    s_c = _dot_nt(q, kc) * scale
    m = jnp.maximum(jnp.max(s_w, axis=-1, keepdims=True), jnp.max(s_c, axis=-1, keepdims=True))
    e_w = jnp.exp(s_w - m)
    e_c = jnp.exp(s_c - m)
    inv = 1.0 / (jnp.sum(e_w, axis=-1, keepdims=True) + jnp.sum(e_c, axis=-1, keepdims=True))
    return e_w * inv, e_c * inv


def _group_place(g, R):
    G = R // GQ
    kb = jnp.clip(g * GQ - NA_KH // 2, 0, R - GK)
    variant = jnp.where(g == 0, 0, jnp.where(g == G - 1, 2, 1))
    return pl.multiple_of(g * (GQ * GRID_W), GQ * GRID_W), pl.multiple_of(kb * GRID_W, GRID_W), variant


def _lane_masks(width, dh):
    lane = lax.broadcasted_iota(jnp.int32, (1, width), 1)
    return [(lane >= h * dh) & (lane < (h + 1) * dh) for h in range(width // dh)]


def _only(mask, a):
    return jnp.where(mask, a, jnp.zeros_like(a))


def _attn_fwd(q, k, v, kc, vc, bias, name, ride=None):
    L, width = q.shape
    C = kc.shape[0]
    dh = NA_HEAD_DIM
    lanes = 128
    hpb = lanes // dh
    R = L // GRID_W
    nq, nk = GQ * GRID_W, GK * GRID_W
    scale = dh ** -0.5

    def body(q_ref, k_ref, v_ref, kc_ref, vc_ref, b_ref, o_ref):
        masks = _lane_masks(lanes, dh)
        kc2 = kc_ref[...]
        vcs = [_only(m, vc_ref[...]) for m in masks]

        def group(g, carry):
            q0, k0, variant = _group_place(g, R)
            q2 = q_ref[pl.ds(q0, nq), :]
            k2 = k_ref[pl.ds(k0, nk), :]
            v2 = v_ref[pl.ds(k0, nk), :]
            qs = jnp.concatenate([_only(m, q2) for m in masks], axis=0)
            p_w, p_c = _attn_probs(qs, k2, kc2, b_ref[variant], scale)
            p_w, p_c = p_w.astype(BF16), p_c.astype(BF16)
            o2 = jnp.zeros((nq, lanes), F32)
            for h, m in enumerate(masks):
                rows = slice(h * nq, (h + 1) * nq)
                o2 = o2 + _dot(p_w[rows], _only(m, v2)) + _dot(p_c[rows], vcs[h])
            o_ref[pl.ds(q0, nq), :] = o2.astype(BF16)
            return carry

        lax.fori_loop(0, R // GQ, group, 0)

    cols = lambda n: pl.BlockSpec((n, lanes), lambda p: (0, p))
    res, carried = _ride_call(
        body, ride, (q, k, v, kc, vc, bias), name=name, grid=(width // lanes,),
        in_specs=[cols(L), cols(L), cols(L), cols(C), cols(C),
                  pl.BlockSpec((None, 3, hpb * nq, nk), lambda p: (p, 0, 0, 0))],
        out_specs=[cols(L)], out_shape=[jax.ShapeDtypeStruct((L, width), BF16)],
        compiler_params=_params())
    return res[0], carried


def _attn_bwd(q, k, v, kc, vc, bias, do, name, ride=None):
    L, width = q.shape
    C = kc.shape[0]
    dh = NA_HEAD_DIM
    lanes = 128
    hpb = lanes // dh
    R = L // GRID_W
    nq, nk = GQ * GRID_W, GK * GRID_W
    scale = dh ** -0.5

    def body(q_ref, k_ref, v_ref, kc_ref, vc_ref, b_ref, do_ref, dq_ref, dk_ref, dv_ref, dkc_ref, dvc_ref, db_ref):
        masks = _lane_masks(lanes, dh)
        kc2 = kc_ref[...]
        vc2 = vc_ref[...]
        kcs = [_only(m, kc2) for m in masks]
        dk_ref[...] = jnp.zeros((L, lanes), F32)
        dv_ref[...] = jnp.zeros((L, lanes), F32)
        dkc_ref[...] = jnp.zeros((C, lanes), F32)
        dvc_ref[...] = jnp.zeros((C, lanes), F32)
        db_ref[...] = jnp.zeros((3, hpb * nq, nk), F32)

        def group(g, carry):
            q0, k0, variant = _group_place(g, R)
            q2 = q_ref[pl.ds(q0, nq), :]
            k2 = k_ref[pl.ds(k0, nk), :]
            v2 = v_ref[pl.ds(k0, nk), :]
            do2 = do_ref[pl.ds(q0, nq), :]
            qs = jnp.concatenate([_only(m, q2) for m in masks], axis=0)
            dos = jnp.concatenate([_only(m, do2) for m in masks], axis=0)
            p_w, p_c = _attn_probs(qs, k2, kc2, b_ref[variant], scale)
            dp_w = _dot_nt(dos, v2)
            dp_c = _dot_nt(dos, vc2)
            delta = jnp.sum(p_w * dp_w, axis=-1, keepdims=True) + jnp.sum(p_c * dp_c, axis=-1, keepdims=True)
            ds_w = p_w * (dp_w - delta)
            ds_c = p_c * (dp_c - delta)
            db_ref[variant] += ds_w
            dsw = (ds_w * scale).astype(BF16)
            dsc = (ds_c * scale).astype(BF16)
            dq2 = jnp.zeros((nq, lanes), F32)
            for h, m in enumerate(masks):
                rows = slice(h * nq, (h + 1) * nq)
                dq2 = dq2 + _dot(dsw[rows], _only(m, k2)) + _dot(dsc[rows], kcs[h])
            dq_ref[pl.ds(q0, nq), :] = dq2.astype(BF16)
            dk_ref[pl.ds(k0, nk), :] += _dot_tn(dsw, qs)
            dv_ref[pl.ds(k0, nk), :] += _dot_tn(p_w.astype(BF16), dos)
            dkc_ref[...] += _dot_tn(dsc, qs)
            dvc_ref[...] += _dot_tn(p_c.astype(BF16), dos)
            return carry

        lax.fori_loop(0, R // GQ, group, 0)

    cols = lambda n: _resident((n, lanes), lambda p: (0, p))
    bspec = _resident((None, 3, hpb * nq, nk), lambda p: (p, 0, 0, 0))
    res, carried = _ride_call(
        body, ride, (q, k, v, kc, vc, bias, do), name=name, grid=(width // lanes,),
        in_specs=[cols(L), cols(L), cols(L), cols(C), cols(C), bspec, cols(L)],
        out_specs=[cols(L), cols(L), cols(L), cols(C), cols(C), bspec],
        out_shape=[jax.ShapeDtypeStruct((L, width), BF16)] + [jax.ShapeDtypeStruct((L, width), F32)] * 2 +
                  [jax.ShapeDtypeStruct((C, width), F32)] * 2 +
                  [jax.ShapeDtypeStruct((width // lanes, 3, hpb * nq, nk), F32)],
        compiler_params=_params())
    return (*res, carried)


def _rpb_grad(dbias, name):
    hpb = HEADS_PER_BLOCK
    H = dbias.shape[0] * hpb
    nri, nci = 2 * NA_KH - 1, 2 * NA_KW - 1
    d6 = dbias.reshape(H // hpb, 3, hpb, GQ, GRID_W, GK, GRID_W).transpose(0, 2, 1, 3, 5, 4, 6)
    d6 = d6.reshape(H, 3, GQ, GK, GRID_W, GRID_W)
    col = jnp.arange(GRID_W)
    onehot = (col[None, None, :] - col[None, :, None] + (NA_KW - 1) == jnp.arange(32)[:, None, None]).astype(F32)
    places = [(v, qr, kr) for v in range(3) for qr in range(GQ) for kr in range(GK)]

    def body(d_ref, m_ref, o_ref, t_ref):
        t_ref[...] = jnp.zeros((32, GRID_W), F32)
        o_ref[...] = jnp.zeros((16, 32, 128), F32)
        for ri in range(nri):
            a = None
            for place in places:
                if _group_ri(*place) == ri:
                    blk = d_ref[place]
                    a = blk if a is None else a + blk
            for ci in range(nci):
                t_ref[ci:ci + 1, :] = _sum0(a * m_ref[ci])
            o_ref[ri] = jnp.broadcast_to(jnp.sum(t_ref[...], axis=1, keepdims=True), (32, 128))

    out = pl.pallas_call(
        body, name=name, grid=(H,),
        in_specs=[pl.BlockSpec((None, 3, GQ, GK, GRID_W, GRID_W), lambda h: (h, 0, 0, 0, 0, 0)),
                  pl.BlockSpec((32, GRID_W, GRID_W), lambda h: (0, 0, 0))],
        out_specs=pl.BlockSpec((None, 16, 32, 128), lambda h: (h, 0, 0, 0)),
        out_shape=jax.ShapeDtypeStruct((H, 16, 32, 128), F32),
        scratch_shapes=[pltpu.VMEM((32, GRID_W), F32)])(d6, onehot)
    return out[:, :nri, :nci, 0]


def _window_count(t, w, L):
    lo = jnp.clip(t - w // 2, 0, L)
    hi = jnp.clip(t - w // 2 + w, 0, L)
    return jnp.maximum(hi - lo, 1).astype(F32)


def _running_sum(v, w):
    k = 1
    while k < w:
        v = v + _shift_rows(v, k)
        k *= 2
    return v


def _pool_fwd(u, poolw, pscale, name, tm=512):
    L, W = u.shape
    G = POOL_GROUP_DIM
    tm = min(tm, L)
    nt = L // tm

    def body(c_ref, p_ref, n_ref, w_ref, s_ref, o_ref, dm_ref):
        i = pl.program_id(0)
        ext = _ext(p_ref[...], c_ref[...], n_ref[...], i, nt)
        t = i * tm + lax.broadcasted_iota(jnp.int32, (tm, 1), 0)
        for g, w in enumerate(POOL_WINDOWS):
            e = ext[:, g * G:(g + 1) * G]
            win = _shift_rows(_running_sum(e, w), -(w // 2 - 1))[HALO:HALO + tm]
            dmx = (win / _window_count(t, w, L) - e[HALO:HALO + tm]).astype(BF16)
            dm_ref[:, g * G:(g + 1) * G] = dmx
            o_ref[:, g * G:(g + 1) * G] = (_dot(dmx, w_ref[g]) * s_ref[:, g * G:(g + 1) * G]).astype(BF16)

    return pl.pallas_call(
        body, name=name, grid=(nt,),
        in_specs=[_tile(tm, W), _halo_prev(tm, W), _halo_next(tm, W, L), _full((4, G, G)), _full((1, W))],
        out_specs=[_tile(tm, W), _tile(tm, W)],
        out_shape=[jax.ShapeDtypeStruct((L, W), BF16)] * 2, compiler_params=_params())(u, u, u, poolw, pscale)


def _pool_bwd(dpool, dmx, poolw, pscale, name, tm=512):
    L, W = dpool.shape
    G = POOL_GROUP_DIM
    tm = min(tm, L)
    nt = L // tm

    def body(c_ref, p_ref, n_ref, dm_ref, w_ref, s_ref, du_ref, dw_ref, acc_ref):
        i = pl.program_id(0)
        ext = _ext(p_ref[...], c_ref[...], n_ref[...], i, nt)
        te = i * tm - HALO + lax.broadcasted_iota(jnp.int32, (tm + 2 * HALO, 1), 0)

        @pl.when(i == 0)
        def _():
            dw_ref[...] = jnp.zeros((4 * G, G), F32)

        rows = []
        for g, w in enumerate(POOL_WINDOWS):
            sc = s_ref[:, g * G:(g + 1) * G]
            dpre = (ext[:, g * G:(g + 1) * G] * sc).astype(BF16)
            dd = _dot_nt(dpre, w_ref[g])
            spread = _shift_rows(_running_sum(dd / _window_count(te, w, L), w), -(w // 2))
            du_ref[:, g * G:(g + 1) * G] = (spread - dd)[HALO:HALO + tm]
            dmx_g = dm_ref[:, g * G:(g + 1) * G]
            rows.append(_sum0(c_ref[:, g * G:(g + 1) * G] * _dot(dmx_g, w_ref[g])))
            dw_ref[g * G:(g + 1) * G, :] += _dot_tn(dmx_g, dpre[HALO:HALO + tm])
        _acc_rows(acc_ref, i == 0, [jnp.concatenate(rows, axis=1)])

    return pl.pallas_call(
        body, name=name, grid=(nt,),
        in_specs=[_tile(tm, W), _halo_prev(tm, W), _halo_next(tm, W, L), _tile(tm, W), _full((4, G, G)),
                  _full((1, W))],
        out_specs=[_tile(tm, W), _full((4 * G, G)), _full((8, W))],
        out_shape=[jax.ShapeDtypeStruct((L, W), F32), jax.ShapeDtypeStruct((4 * G, G), F32),
                   jax.ShapeDtypeStruct((8, W), F32)],
        compiler_params=_params())(dpool, dpool, dpool, dmx, poolw, pscale)


def _conv3(z, cw):
    return _shift_rows(z, 1) * cw[0] + z * cw[1] + _shift_rows(z, -1) * cw[2]


def _conv_fwd(x, prm, wing, woutg, name, tm=512):
    L, D = x.shape
    Ws = wing.shape[-1]
    tm = min(tm, L)
    nt = L // tm
    te = tm + 2 * HALO

    def body(c_ref, p_ref, n_ref, prm_ref, wi_ref, wo_ref, xo_ref, y_ref, b_ref):
        i = pl.program_id(0)
        xe = jnp.concatenate([p_ref[...], c_ref[...], n_ref[...]], axis=0)
        hn, _, _, _ = _nm(xe, prm_ref[3:4, :], prm_ref[0:1, :], prm_ref[1:2, :])
        hb = hn.astype(BF16)
        proj = jnp.concatenate([_dot(hb, wi_ref[s]) for s in range(4)], axis=1)
        bg, cg, xin = proj[:, :D], proj[:, D:2 * D], proj[:, 2 * D:]
        tpos = i * tm - HALO + lax.broadcasted_iota(jnp.int32, (te, 1), 0)
        valid = ((tpos >= 0) & (tpos < L)).astype(F32)
        yc = _conv3(cg * xin * valid, [prm_ref[4 + k:5 + k, :] for k in range(3)])
        h2 = (bg * yc)[HALO:HALO + tm].astype(BF16)
        yv = _dot(h2, wo_ref[...])
        y_ref[...] = yv.astype(BF16)
        xo_ref[...] = c_ref[...] + prm_ref[2:3, :] * yv
        b_ref[...] = proj[HALO:HALO + tm].astype(BF16)

    return pl.pallas_call(
        body, name=name, grid=(nt,),
        in_specs=[_tile(tm, D), _halo_prev(tm, D), _halo_next(tm, D, L), _full((8, D)),
                  _resident((None, 4, D, Ws), lambda i: (0, 0, 0, 0)),
                  _resident((None, D, D), lambda i: (0, 0, 0))],
        out_specs=[_tile(tm, D), _tile(tm, D), _tile(tm, 3 * D)],
        out_shape=[jax.ShapeDtypeStruct((L, D), F32), jax.ShapeDtypeStruct((L, D), BF16),
                   jax.ShapeDtypeStruct((L, 3 * D), BF16)],
        compiler_params=_params())(x, x, x, prm, wing, woutg)


def _conv_bwd(dout, x, y, bcx, prm, wing, woutg, name, tm=256, ride=None):
    L, D = x.shape
    Ws = wing.shape[-1]
    tm = min(tm, L)
    nt = L // tm
    te = tm + 2 * HALO

    def body(dc_ref, dp_ref, dn_ref, x_ref, y_ref, bc_ref, bp_ref, bn_ref, prm_ref, wi_ref, wo_ref,
             dx_ref, dpr_ref, h2_ref, dy_ref, hn_ref, acc_ref):
        i = pl.program_id(0)
        gain, shift, scale, gate = prm_ref[3:4, :], prm_ref[0:1, :], prm_ref[1:2, :], prm_ref[2:3, :]
        taps = [prm_ref[4 + k:5 + k, :] for k in range(3)]
        do = dc_ref[...]
        doe = _ext(dp_ref[...], do, dn_ref[...], i, nt)
        dye = (gate * doe).astype(BF16)
        dy_ref[...] = dye[HALO:HALO + tm]
        dh2 = _dot_nt(dye, wo_ref[...])
        be = jnp.concatenate([bp_ref[...], bc_ref[...], bn_ref[...]], axis=0).astype(F32)
        bg, cg, xin = be[:, :D], be[:, D:2 * D], be[:, 2 * D:]
        tpos = i * tm - HALO + lax.broadcasted_iota(jnp.int32, (te, 1), 0)
        valid = ((tpos >= 0) & (tpos < L)).astype(F32)
        z = cg * xin * valid
        yc = _conv3(z, taps)
        dyc = dh2 * bg
        h2_ref[...] = (bg * yc)[HALO:HALO + tm].astype(BF16)
        dz = _conv3(dyc, taps[::-1]) * valid
        dproj = jnp.concatenate([dh2 * yc, dz * xin, dz * cg], axis=1)[HALO:HALO + tm].astype(BF16)
        dpr_ref[...] = dproj
        dhn = jnp.zeros((tm, D), F32)
        for s in range(4):
            dhn = dhn + _dot_nt(dproj[:, s * Ws:(s + 1) * Ws], wi_ref[s])
        hn, xhat, r, nrm = _nm(x_ref[...], gain, shift, scale)
        hn_ref[...] = hn.astype(BF16)
        dx, dshift, dscale, dgn = _nm_bwd(dhn, xhat, r, nrm, gain, scale)
        dx_ref[...] = do + dx
        dgate = _sum0(do * y_ref[...].astype(F32))
        dtaps = [_sum0((dyc * _shift_rows(z, 1 - k))[HALO:HALO + tm]) for k in range(3)]
        _acc_rows(acc_ref, i == 0, [dshift, dscale, dgate, dgn] + dtaps)

    res, carried = _ride_call(
        body, ride, (dout, dout, dout, x, y, bcx, bcx, bcx, prm, wing, woutg), name=name, grid=(nt,),
        in_specs=[_tile(tm, D), _halo_prev(tm, D), _halo_next(tm, D, L), _tile(tm, D), _tile(tm, D),
                  _tile(tm, 3 * D), _halo_prev(tm, 3 * D), _halo_next(tm, 3 * D, L), _full((8, D)),
                  _resident((None, 4, D, Ws), lambda i: (0, 0, 0, 0)),
                  _resident((None, D, D), lambda i: (0, 0, 0))],
        out_specs=[_tile(tm, D), _tile(tm, 3 * D), _tile(tm, D), _tile(tm, D), _tile(tm, D), _full((8, D))],
        out_shape=[jax.ShapeDtypeStruct((L, D), F32), jax.ShapeDtypeStruct((L, 3 * D), BF16),
                   jax.ShapeDtypeStruct((L, D), BF16), jax.ShapeDtypeStruct((L, D), BF16),
                   jax.ShapeDtypeStruct((L, D), BF16), jax.ShapeDtypeStruct((8, D), F32)],
        compiler_params=_params())
    return (*res, carried)


def _mod_fwd(cond, mod_w, mod_b, name, tn=768):
    nl, D, N = mod_w.shape
    tn = min(tn, N)

    def body(c_ref, w_ref, b_ref, o_ref):
        cv = c_ref[...]
        s = (cv * _sigmoid(cv)).astype(BF16)
        o_ref[...] = _dot(s, w_ref[...].astype(BF16)) + b_ref[...]

    return pl.pallas_call(
        body, name=name, grid=(nl, N // tn),
        in_specs=[pl.BlockSpec((16, D), lambda l, j: (0, 0)), pl.BlockSpec((None, D, tn), lambda l, j: (l, 0, j)),
                  pl.BlockSpec((None, 1, tn), lambda l, j: (l, 0, j))],
        out_specs=pl.BlockSpec((None, 16, tn), lambda l, j: (l, 0, j)),
        out_shape=jax.ShapeDtypeStruct((nl, 16, N), F32), compiler_params=_params())(cond, mod_w, mod_b)


def _mod_bwd(cond, dm, mod_w, name, tn=768):
    nl, D, N = mod_w.shape
    tn = min(tn, N)

    def body(c_ref, d_ref, w_ref, dw_ref, dc_ref):
        first = (pl.program_id(0) == 0) & (pl.program_id(1) == 0)
        cv = c_ref[...]
        s = (cv * _sigmoid(cv)).astype(BF16)
        d = d_ref[...].astype(BF16)
        dw_ref[...] = _dot_tn(s, d)

        @pl.when(first)
        def _():
            dc_ref[...] = jnp.zeros((16, D), F32)

        dc_ref[...] += _dot_nt(d, w_ref[...].astype(BF16))

    return pl.pallas_call(
        body, name=name, grid=(nl, N // tn),
        in_specs=[pl.BlockSpec((16, D), lambda l, j: (0, 0)), pl.BlockSpec((None, 16, tn), lambda l, j: (l, 0, j)),
                  pl.BlockSpec((None, D, tn), lambda l, j: (l, 0, j))],
        out_specs=[pl.BlockSpec((None, D, tn), lambda l, j: (l, 0, j)), pl.BlockSpec((16, D), lambda l, j: (0, 0))],
        out_shape=[jax.ShapeDtypeStruct((nl, D, N), F32), jax.ShapeDtypeStruct((16, D), F32)],
        compiler_params=_params())(cond, dm, mod_w)


def _mod_small_grads(dm_all, cond, dsilu_parts, name):
    nl, _, N = dm_all.shape
    D = cond.shape[1]

    def body(d_ref, c_ref, p_ref, db_ref, dc_ref):
        for l in range(nl):
            db_ref[l] = _sum0(d_ref[l])
        tot = p_ref[0, 8:9, :]
        for k in range(1, N_CHIPS):
            tot = tot + p_ref[2 * k, 8:9, :]
        cv = c_ref[8:9, :]
        sg = _sigmoid(cv)
        dc_ref[...] = tot * (sg * (1.0 + cv * (1.0 - sg)))

    return pl.pallas_call(
        body, name=name, out_shape=[jax.ShapeDtypeStruct((nl, 1, N), F32), jax.ShapeDtypeStruct((1, D), F32)],
    )(dm_all, cond, dsilu_parts)


def _prm(rows, D):
    rows = [r.reshape(1, D) for r in rows]
    return jnp.concatenate(rows + [jnp.zeros((8 - len(rows), D), F32)], axis=0)


def kernel(x, c, ctx, c_ctx, mod_w, mod_b, norm_g, ffn_w13, ffn_w2, even_w_in, even_w_out, na_rpb, pool_w, pool_scale, conv_w_in, conv_w, conv_w_out, final_g, loss_target, m_c_ctx, m_mod_w, m_mod_b, m_norm_g, m_ffn_w13, m_ffn_w2, m_even_w_in, m_even_w_out, m_na_rpb, m_pool_w, m_pool_scale, m_conv_w_in, m_conv_w, m_conv_w_out, m_final_g, v_c_ctx, v_mod_w, v_mod_b, v_norm_g, v_ffn_w13, v_ffn_w2, v_even_w_in, v_even_w_out, v_na_rpb, v_pool_w, v_pool_scale, v_conv_w_in, v_conv_w, v_conv_w_out, v_final_g):
    xi, yi, ci = lax.axis_index("x"), lax.axis_index("y"), lax.axis_index("c")
    chip = 2 * xi + yi
    dev = 4 * xi + 2 * yi + ci
    _, L, D = x.shape
    C = ctx.shape[1]
    Ds = D // N_CHIPS
    Nm = mod_w.shape[-1]
    Fh = ffn_w13.shape[-1]
    Fq = ffn_w2.shape[2]
    assert ffn_w13.shape[:2] == (2, 2) and Fh == 2 * Fq and L % (GQ * GRID_W) == 0 and L // GRID_W >= GK and GQ == NA_KH // 2
    x0, ctx0, tgt = x[0], ctx[0], loss_target[0]

    pad = lambda a: jnp.pad(a, ((0, 0), (0, D - a.shape[1])))
    pack1 = jnp.concatenate([c, pad(norm_g.reshape(6, Ds)), pad(conv_w.reshape(3, Ds)), jnp.zeros((6, D), F32)], axis=0)
    g1 = _small_all_gather(pack1, "ag_cond")
    cond = jnp.concatenate([g1[:, 0], c_ctx[None], jnp.zeros((7, D), F32)], axis=0)
    norm_full = jnp.concatenate([g1[2 * k, 1:7, :Ds] for k in range(N_CHIPS)], axis=1).reshape(2, 3, D)
    convw_full = jnp.concatenate([g1[2 * k, 7:10, :Ds] for k in range(N_CHIPS)], axis=1)

    mod_b_loc = lax.dynamic_slice_in_dim(mod_b, chip * Nm, Nm, axis=1).reshape(2, 1, Nm)
    m_loc = _mod_fwd(cond, mod_w, mod_b_loc, "mod_fwd")
    g2 = _small_all_gather(m_loc.reshape(32, Nm), "ag_mod")
    m_all = jnp.concatenate([g2[2 * k] for k in range(N_CHIPS)], axis=1).reshape(2, 16, N_MOD, D)
    m_lat = lax.dynamic_index_in_dim(m_all, dev, axis=1, keepdims=False)
    m_ctx = m_all[:, 8]

    def prm(mods, layer, base, gain_idx, extra=()):
        return _prm([mods[layer, base], mods[layer, base + 1], mods[layer, base + 2], norm_full[layer, gain_idx],
                     *extra], D)

    def shard_bf16(w, name):
        return _cast_bf16(w.reshape(-1, w.shape[-1]), name).reshape(-1, *w.shape[-2:])

    w13s, w2s = shard_bf16(ffn_w13, "cast_w13"), shard_bf16(ffn_w2, "cast_w2")
    eins, eouts = shard_bf16(even_w_in, "cast_ein"), shard_bf16(even_w_out, "cast_eout")
    cins, couts = shard_bf16(conv_w_in, "cast_cin"), shard_bf16(conv_w_out, "cast_cout")
    ffn_shards = [[w13s[t:t + 1], w2s[t:t + 1]] for t in range(4)]

    def ffn_weights(w13g, w2g):
        return w13g.reshape(1, 4, D, Fh), w2g

    wf = [ffn_weights(*_gather_shards(ffn_shards[0], "ag_ffn0")), None, None, None]
    pos = jnp.stack([chip, ci]).astype(jnp.int32)

    p_f1 = prm(m_lat, 0, 0, 0)
    p_mx = prm(m_lat, 0, 3, 1)
    p_f2 = prm(m_lat, 0, 6, 2)
    p_g1 = prm(m_lat, 1, 0, 0)
    p_cv = prm(m_lat, 1, 3, 1, extra=(convw_full[0], convw_full[1], convw_full[2]))
    p_g2 = prm(m_lat, 1, 6, 2)
    pc_f1 = prm(m_ctx, 0, 0, 0)
    pc_mx = prm(m_ctx, 0, 3, 1)

    x1, ab1, y1, (eing, eoutg) = _ffn_fwd(x0, p_f1, *wf[0], 0, "ffn_fwd_l0a", ride=_broadcast_ride([eins, eouts]))
    eing = eing.reshape(1, 4, D, NA_WIDTH)
    ctx1, abc, yc, _ = _ffn_fwd(ctx0, pc_f1, *wf[0], 0, "ffn_fwd_ctx")
    q, k, v, u, hn_mx = _even_in_fwd(x1, p_mx, eing, "even_in_fwd")
    _, k_c, v_c, _, hn_cx = _even_in_fwd(ctx1, pc_mx, eing, "even_in_ctx")
    bias = _bias_table(na_rpb[0], "bias_table")
    att, gathered = _attn_fwd(q, k, v, k_c, v_c, bias, "attn_fwd", ride=_broadcast_ride(ffn_shards[1]))
    wf[1] = ffn_weights(*gathered)
    pw_b = _cast_bf16(pool_w.reshape(-1, POOL_GROUP_DIM), "cast_poolw").reshape(4, POOL_GROUP_DIM, POOL_GROUP_DIM)
    pool, dmx = _pool_fwd(u, pw_b, pool_scale, "pool_fwd")
    x2, ymx = _even_out_fwd(x1, att, pool, p_mx, eoutg, "even_out_fwd")
    x3, ab2, y2, gathered = _ffn_fwd(x2, p_f2, *wf[1], 0, "ffn_fwd_l0b",
                                     ride=_broadcast_ride(ffn_shards[2] + [cins, couts]))
    wf[2] = ffn_weights(*gathered[:2])
    cing, coutg = gathered[2].reshape(1, 4, D, conv_w_in.shape[-1]), gathered[3]
    x4, ab3, y3, gathered = _ffn_fwd(x3, p_g1, *wf[2], 0, "ffn_fwd_l1a", ride=_broadcast_ride(ffn_shards[3]))
    wf[3] = ffn_weights(*gathered)
    x5, ycv, bcx = _conv_fwd(x4, p_cv, cing, coutg, "conv_fwd")
    x6, ab4, y4, _ = _ffn_fwd(x5, p_g2, *wf[3], 0, "ffn_fwd_l1b")

    def ffn_back(dout, xin, ab, yy, p, t, tag, init13=None, init2=None, ride=None, head=None):
        sv, gact = ab
        dx, dab, dy, hn, acc, carried = _ffn_bwd(dout, xin, sv, yy, p, *wf[t], 0, f"ffn_bwd_{tag}", ride=ride,
                                                 head=head)
        dw13 = _mm_tn(hn, dab, 4, False, f"dw13_{tag}", init=init13)
        dw2 = _mm_tn(gact, dy, 2, True, f"dw2_{tag}", init=init2)
        return dx, acc, dw13, dw2, carried

    dx5, acc_g2, dw13_3, dw2_3, _ = ffn_back(x6, x5, ab4, y4, p_g2, 3, "l1b", head=(tgt, final_g.reshape(1, D)))
    acc_head = acc_g2[4:6]
    loss = lax.psum(acc_head[1, 0], ("x", "y", "c"))
    s_a, sb_a = _pair_sums([dw13_3, dw2_3], pos, "l1b")
    dx4, dproj, h2, dycv, hn_cv, acc_cv, got_a = _conv_bwd(dx5, x4, ycv, bcx, p_cv, cing, coutg, "conv_bwd",
                                                           ride=_scatter_ride(sb_a))
    dcin = _mm_tn(hn_cv, dproj, 4, False, "dw_cin")
    dcout = _mm_tn(h2, dycv, 1, False, "dw_cout")
    s_b, sb_b = _pair_sums([dcin, dcout], pos, "conv")
    dx3, acc_g1, dw13_2, dw2_2, got_b = ffn_back(dx4, x3, ab3, y3, p_g1, 2, "l1a", ride=_scatter_ride(sb_b))
    s_c, sb_c = _pair_sums([dw13_2, dw2_2], pos, "l1a")
    dx2, acc_f2, dw13_1, dw2_1, got_c = ffn_back(dx3, x2, ab2, y2, p_f2, 1, "l0b", ride=_scatter_ride(sb_c))

    dymx, datt, dpool, acc_mxo = _even_out_bwd(dx2, ymx, p_mx, eoutg, "even_out_bwd")
    deout = jnp.concatenate([_mm_tn(att, dymx, 1, False, "dw_eout_att"),
                             _mm_tn(pool, dymx, 1, False, "dw_eout_pool")], axis=0)
    s_d, sb_d = _pair_sums([dw13_1, dw2_1, deout], pos, "l0b")
    du, dpoolw, acc_pool = _pool_bwd(dpool, dmx, pw_b, pool_scale, "pool_bwd")
    dq, dk, dv, dkc, dvc, dbias, got_d = _attn_bwd(q, k, v, k_c, v_c, bias, datt, "attn_bwd",
                                                   ride=_scatter_ride(sb_d))
    drpb = _rpb_grad(dbias, "rpb_grad")
    dx1, dstack, acc_mxi = _even_in_bwd(dx2, x1, dq, dk, dv, du, p_mx, eing,
                                        "even_in_bwd")
    zc = jnp.zeros((C, NA_WIDTH), F32)
    dctx1, dstack_c, accc_mx = _even_in_bwd(jnp.zeros((C, D), F32), ctx1, zc, dkc, dvc, zc,
                                            pc_mx, eing, "even_in_bwd_ctx")
    dein_c = _mm_tn(hn_cx, dstack_c, 4, False, "dw_ein_ctx")
    dein = _mm_tn(hn_mx, dstack, 4, False, "dw_ein", init=dein_c)
    s_e, sb_e = _pair_sums([dein], pos, "ein")
    _, accc_f1, dw13_c, dw2_c, _ = ffn_back(dctx1, ctx0, abc, yc, pc_f1, 0, "ctx")
    sv1, gact1 = ab1
    dx0, dab, dy, hn, acc_f1, _ = _ffn_bwd(dx1, x0, sv1, y1, p_f1, *wf[0], 0, "ffn_bwd_l0a")
    dw13_0, got_e = _mm_tn(hn, dab, 4, False, "dw13_l0a", init=dw13_c, ride=_scatter_ride(sb_e))
    s_f13, sb_f13 = _pair_sums([dw13_0], pos, "l0a_w13")
    dw2_0, got_f13 = _mm_tn(gact1, dy, 2, True, "dw2_l0a", init=dw2_c, ride=_scatter_ride(sb_f13))
    s_f2, sb_f2 = _pair_sums([dw2_0], pos, "l0a_w2")

    z1 = jnp.zeros((1, D), F32)
    dm_lat = jnp.concatenate([acc_f1[0:3], acc_mxi[0:2], acc_mxo[2:3], acc_f2[0:3],
                              acc_g1[0:3], acc_cv[0:3], acc_g2[0:3]], axis=0)
    dm_ctx = jnp.concatenate([accc_f1[0:3], accc_mx[0:2]] + [z1] * 13, axis=0)
    dnorm = jnp.concatenate([acc_f1[3:4] + accc_f1[3:4], acc_mxi[3:4] + accc_mx[3:4], acc_f2[3:4],
                             acc_g1[3:4], acc_cv[3:4], acc_g2[3:4]], axis=0)
    rpb_flat = jnp.pad(drpb.reshape(-1), (0, 4 * D - drpb.size)).reshape(4, D)
    pack3 = jnp.concatenate([dm_lat, dm_ctx, dnorm, acc_cv[4:7], acc_head[0:1], pad(acc_pool[0:1]), z1,
                             dpoolw.reshape(-1, D), rpb_flat, jnp.zeros((4, D), F32)], axis=0)
    g3 = _small_all_gather(pack3, "ag_small")
    tot = _sum_devices(g3, "sum_small")
    dm_all = jnp.concatenate([g3[:, 0:18].reshape(8, 2, N_MOD * D).transpose(1, 0, 2),
                              tot[18:36].reshape(2, 1, N_MOD * D), jnp.zeros((2, 7, N_MOD * D), F32)], axis=1)
    dm_loc = lax.dynamic_slice_in_dim(dm_all, chip * Nm, Nm, axis=2)
    g_mod_w, dsilu = _mod_bwd(cond, dm_loc, mod_w, "mod_bwd")
    g4 = _small_all_gather(dsilu, "ag_dsilu")
    g_mod_b, g_c_ctx = _mod_small_grads(dm_all, cond, g4, "mod_small")
    g_mod_b = g_mod_b.reshape(2, N_MOD * D)
    g_c_ctx = g_c_ctx.reshape(D)
    g_norm_full = tot[36:42].reshape(2, 3, D)
    g_norm = lax.dynamic_slice_in_dim(g_norm_full, chip * Ds, Ds, axis=2)
    g_conv_w = lax.dynamic_slice_in_dim(tot[42:45], chip * Ds, Ds, axis=1).reshape(1, 3, Ds)
    g_final = tot[45]
    g_pscale = tot[46:47, :pool_scale.shape[1]]
    g_poolw = tot[48:112].reshape(pool_w.shape)
    g_rpb = tot[112:116].reshape(-1)[:na_rpb.size].reshape(na_rpb.shape)

    r13_3, r2_3 = _joins(s_a, got_a, pos, "l1b")
    r_cin, r_cout = _joins(s_b, got_b, pos, "conv")
    r13_2, r2_2 = _joins(s_c, got_c, pos, "l1a")
    r13_1, r2_1, r_eout = _joins(s_d, got_d, pos, "l0b")
    (r_ein,) = _joins(s_e, got_e, pos, "ein")
    adamw_mod_w, got_f2 = _adamw(mod_w, g_mod_w, m_mod_w, v_mod_w, "adamw_mod_w", ride=_scatter_ride(sb_f2))
    (r13_0,) = _joins(s_f13, got_f13, pos, "l0a_w13")
    (r2_0,) = _joins(s_f2, got_f2, pos, "l0a_w2")
    g_w13 = jnp.stack([r13_0, r13_1, r13_2, r13_3]).reshape(ffn_w13.shape)
    g_w2 = jnp.stack([r2_0, r2_1, r2_2, r2_3]).reshape(ffn_w2.shape)
    g_ein, g_eout, g_cin, g_cout = r_ein[None], r_eout[None], r_cin[None], r_cout[None]

    grads = [g_c_ctx, g_mod_w, g_mod_b, g_norm, g_w13, g_w2, g_ein, g_eout, g_rpb, g_poolw, g_pscale, g_cin,
             g_conv_w, g_cout, g_final]
    weights = [c_ctx, mod_w, mod_b, norm_g, ffn_w13, ffn_w2, even_w_in, even_w_out, na_rpb, pool_w, pool_scale,
               conv_w_in, conv_w, conv_w_out, final_g]
    ms = [m_c_ctx, m_mod_w, m_mod_b, m_norm_g, m_ffn_w13, m_ffn_w2, m_even_w_in, m_even_w_out, m_na_rpb, m_pool_w,
          m_pool_scale, m_conv_w_in, m_conv_w, m_conv_w_out, m_final_g]
    vs = [v_c_ctx, v_mod_w, v_mod_b, v_norm_g, v_ffn_w13, v_ffn_w2, v_even_w_in, v_even_w_out, v_na_rpb, v_pool_w,
          v_pool_scale, v_conv_w_in, v_conv_w, v_conv_w_out, v_final_g]
    names = ["c_ctx", "mod_w", "mod_b", "norm_g", "ffn_w13", "ffn_w2", "even_w_in", "even_w_out", "na_rpb", "pool_w",
             "pool_scale", "conv_w_in", "conv_w", "conv_w_out", "final_g"]
    deltas, new_m, new_v = [], [], []
    for n, w, g, m, vv in zip(names, weights, grads, ms, vs):
        g = g.reshape(w.shape)
        if n == "mod_w":
            d, mn, vn = adamw_mod_w
        elif w.ndim == 1:
            d, mn, vn = (t.reshape(w.shape) for t in _adamw(w[None], g[None], m[None], vv[None], f"adamw_{n}"))
        else:
            d, mn, vn = _adamw(w, g, m, vv, f"adamw_{n}")
        deltas.append(d)
        new_m.append(mn)
        new_v.append(vn)
    grads = [g.reshape(w.shape) for g, w in zip(grads, weights)]
    return (loss, dx0[None], *grads, *deltas, *new_m, *new_v)
```

```python
import jax
import jax.numpy as jnp
from jax import lax
from jax.experimental import pallas as pl
from jax.experimental.pallas import tpu as pltpu

F32 = jnp.float32
BF16 = jnp.bfloat16
MESH = pl.DeviceIdType.MESH

GRID_W = 64
NA_HEADS = 8
NA_HEAD_DIM = 64
NA_KH = 8
NA_KW = 16
GQ = 4
GK = GQ + NA_KH
NA_WIDTH = NA_HEADS * NA_HEAD_DIM
POOL_WINDOWS = (2, 4, 8, 16)
POOL_GROUP_DIM = 128
N_MOD = 9
RMS_EPS = 1e-6
NEG_INF = -1e30
ADAM_LR, ADAM_B1, ADAM_B2, ADAM_EPS, ADAM_WD, ADAM_STEP = 0.001, 0.9, 0.999, 1e-08, 0.01, 10

HALO = 16
VMEM_LIMIT = 56 * 1024 * 1024
N_CHIPS = 4
N_DEV = 8


def _dot(a, b):
    return jnp.dot(a, b, preferred_element_type=F32)


def _dot_nt(a, b):
    return lax.dot_general(a, b, (((1,), (1,)), ((), ())), preferred_element_type=F32)


def _dot_tn(a, b):
    return lax.dot_general(a, b, (((0,), (0,)), ((), ())), preferred_element_type=F32)


def _sigmoid(a):
    return 1.0 / (1.0 + jnp.exp(-a))


def _sum0(v):
    return jnp.sum(v, axis=0, keepdims=True)


def _nm(x, g, shift, scale):
    r = lax.rsqrt(jnp.mean(x * x, axis=-1, keepdims=True) + RMS_EPS)
    xhat = x * r
    nrm = xhat * g
    return nrm * (1.0 + scale) + shift, xhat, r, nrm


def _nm_bwd(dhn, xhat, r, nrm, g, scale):
    dshift = _sum0(dhn)
    dscale = _sum0(dhn * nrm)
    dnrm = dhn * (1.0 + scale)
    dgn = _sum0(dnrm * xhat)
    dxh = dnrm * g
    dx = r * (dxh - xhat * jnp.mean(dxh * xhat, axis=-1, keepdims=True))
    return dx, dshift, dscale, dgn


def _acc_rows(acc_ref, first, rows):
    @pl.when(first)
    def _():
        acc_ref[...] = jnp.zeros(acc_ref.shape, acc_ref.dtype)
    for k, row in enumerate(rows):
        if row is not None:
            acc_ref[k:k + 1, :] += row


def _shift_rows(v, k):
    n = v.shape[0]
    k = k % n
    return v if k == 0 else pltpu.roll(v, k, 0)


def _tile(tm, w):
    return pl.BlockSpec((tm, w), lambda i: (i, 0))


def _full(shape):
    nd = len(shape)
    return pl.BlockSpec(shape, lambda i: (0,) * nd)


def _resident(block, imap):
    return pl.BlockSpec(block, imap, pipeline_mode=pl.Buffered(1))


def _halo_prev(tm, w):
    return pl.BlockSpec((HALO, w), lambda i: (jnp.maximum(i * (tm // HALO) - 1, 0), 0))


def _halo_next(tm, w, L):
    return pl.BlockSpec((HALO, w), lambda i: (jnp.minimum((i + 1) * (tm // HALO), L // HALO - 1), 0))


def _params(vmem=VMEM_LIMIT):
    return pltpu.CompilerParams(vmem_limit_bytes=vmem)


def _pick_rows(rows, cols, itemsize=4, target=1 << 20):
    best = None
    for t in range(8, rows + 1, 8):
        if rows % t == 0 and t * cols * itemsize <= target:
            best = t
    return best if best is not None else rows


def _ext(prev, cur, nxt, i, nt):
    prev = jnp.where(i > 0, prev, jnp.zeros_like(prev))
    nxt = jnp.where(i < nt - 1, nxt, jnp.zeros_like(nxt))
    return jnp.concatenate([prev, cur, nxt], axis=0)


def _cast_bf16(a2d, name):
    rows, cols = a2d.shape
    tr = _pick_rows(rows, cols)

    def body(a_ref, o_ref):
        o_ref[...] = a_ref[...].astype(BF16)

    return pl.pallas_call(
        body, name=name, grid=(rows // tr,), in_specs=[_tile(tr, cols)], out_specs=_tile(tr, cols),
        out_shape=jax.ShapeDtypeStruct((rows, cols), BF16))(a2d)


def _sum_devices(g, name):
    n, rows, cols = g.shape
    tr = _pick_rows(rows, cols, target=1 << 18)

    def body(g_ref, o_ref):
        s = g_ref[0]
        for d in range(1, n):
            s = s + g_ref[d]
        o_ref[...] = s

    return pl.pallas_call(
        body, name=name, grid=(rows // tr,), in_specs=[pl.BlockSpec((n, tr, cols), lambda i: (0, i, 0))],
        out_specs=_tile(tr, cols), out_shape=jax.ShapeDtypeStruct((rows, cols), F32))(g)


def _adamw(w, g, m, v, name, ride=None):
    shape = w.shape
    cols = shape[-1]
    rows = w.size // cols
    w2, g2, m2, v2 = (t.reshape(rows, cols) for t in (w, g, m, v))
    tr = _pick_rows(rows, cols)
    c1 = 1.0 - ADAM_B1 ** ADAM_STEP
    c2 = 1.0 - ADAM_B2 ** ADAM_STEP

    def body(w_ref, g_ref, m_ref, v_ref, d_ref, mo_ref, vo_ref):
        gg = g_ref[...]
        mn = ADAM_B1 * m_ref[...] + (1.0 - ADAM_B1) * gg
        vn = ADAM_B2 * v_ref[...] + (1.0 - ADAM_B2) * (gg * gg)
        d_ref[...] = -ADAM_LR * ((mn / c1) / (jnp.sqrt(vn / c2) + ADAM_EPS) + ADAM_WD * w_ref[...])
        mo_ref[...] = mn
        vo_ref[...] = vn

    outs, carried = _ride_call(
        body, ride, (w2, g2, m2, v2), name=name, grid=(rows // tr,), in_specs=[_tile(tr, cols)] * 4,
        out_specs=[_tile(tr, cols)] * 3, out_shape=[jax.ShapeDtypeStruct((rows, cols), F32)] * 3)
    outs = tuple(o.reshape(shape) for o in outs)
    return outs if ride is None else (outs, carried)


def _mesh_pos():
    x, y, c = lax.axis_index("x"), lax.axis_index("y"), lax.axis_index("c")
    chips = [(1 - x, y), (x, 1 - y), (1 - x, 1 - y)]
    return x, y, c, chips


def _hbm_specs(n):
    return [pl.BlockSpec(memory_space=pltpu.HBM)] * n


def _small_all_gather(v, name):
    rows, w = v.shape

    def body(x_ref, out_ref, send_sems, recv_sems, local_sem):
        x, y, c, chips = _mesh_pos()
        me, sibling = (x, y, c), (x, y, 1 - c)

        def blk(px, py, pc):
            return out_ref.at[4 * px + 2 * py + pc]

        def copy(k, block, to, src=None):
            return pltpu.make_async_remote_copy(
                src_ref=blk(*block) if src is None else src, dst_ref=blk(*block),
                send_sem=send_sems.at[k], recv_sem=recv_sems.at[k], device_id=to, device_id_type=MESH)

        mine = pltpu.make_async_copy(x_ref, blk(*me), local_sem)
        mine.start()
        first = [copy(0, me, sibling, src=x_ref)]
        first += [copy(1 + j, me, (*chip, c), src=x_ref) for j, chip in enumerate(chips)]
        for cp in first:
            cp.start()
        passed = [copy(4 + j, (*chip, c), sibling) for j, chip in enumerate(chips)]
        for j, chip in enumerate(chips):
            copy(1 + j, (*chip, c), me).wait_recv()
            passed[j].start()
        copy(0, sibling, me).wait_recv()
        for j, chip in enumerate(chips):
            copy(4 + j, (*chip, 1 - c), me).wait_recv()
        for cp in first + passed:
            cp.wait_send()
        mine.wait()

    return pl.pallas_call(
        body, name=name, out_shape=jax.ShapeDtypeStruct((N_DEV, rows, w), v.dtype),
        in_specs=[pl.BlockSpec(memory_space=pltpu.VMEM)], out_specs=pl.BlockSpec(memory_space=pltpu.VMEM),
        scratch_shapes=[pltpu.SemaphoreType.DMA((7,)), pltpu.SemaphoreType.DMA((7,)), pltpu.SemaphoreType.DMA],
    )(v)


def _gather_shards(shards, name):
    n = len(shards)

    def body(*refs):
        ins, outs = refs[:n], refs[n:2 * n]
        send_sems, recv_sems, local_sems = refs[2 * n:]
        x, y, c, chips = _mesh_pos()
        k = 2 * x + y
        sibling = (x, y, 1 - c)

        def window(t, chip_k, half):
            r = ins[t].shape[1]
            return outs[t].at[:, pl.ds(chip_k * r + half * (r // 2), r // 2), :]

        def copy(t, j, chip_k, half, to, src=None):
            return pltpu.make_async_remote_copy(
                src_ref=window(t, chip_k, half) if src is None else src, dst_ref=window(t, chip_k, half),
                send_sem=send_sems.at[6 * t + j], recv_sem=recv_sems.at[6 * t + j], device_id=to, device_id_type=MESH)

        started, local = [], []
        for t in range(n):
            r = ins[t].shape[1]
            lc = pltpu.make_async_copy(ins[t], outs[t].at[:, pl.ds(k * r, r), :], local_sems.at[t])
            lc.start()
            local.append(lc)
            src = ins[t].at[:, pl.ds(c * (r // 2), r // 2), :]
            for j, chip in enumerate(chips):
                cp = copy(t, j, k, c, (*chip, c), src=src)
                cp.start()
                started.append(cp)
        for t in range(n):
            for j, chip in enumerate(chips):
                kj = 2 * chip[0] + chip[1]
                copy(t, j, kj, c, sibling).wait_recv()
                cp = copy(t, 3 + j, kj, c, sibling)
                cp.start()
                started.append(cp)
        for t in range(n):
            for j, chip in enumerate(chips):
                kj = 2 * chip[0] + chip[1]
                copy(t, 3 + j, kj, 1 - c, sibling).wait_recv()
        for cp in started:
            cp.wait_send()
        for lc in local:
            lc.wait()

    out_shape = [jax.ShapeDtypeStruct((s.shape[0], N_CHIPS * s.shape[1], s.shape[2]), s.dtype) for s in shards]
    return pl.pallas_call(
        body, name=name, out_shape=out_shape, in_specs=_hbm_specs(n), out_specs=_hbm_specs(n),
        scratch_shapes=[pltpu.SemaphoreType.DMA((6 * n,)), pltpu.SemaphoreType.DMA((6 * n,)),
                        pltpu.SemaphoreType.DMA((n,))],
    )(*shards)


def _chunk_rows(h, w, limit=2 << 20):
    best = 16
    for t in range(16, h + 1, 16):
        if h % t == 0 and t * w * 4 <= limit:
            best = t
    return best


def _pair_sum(part, pos, name):
    _, h, w = part.shape
    cr = _chunk_rows(h, w)
    nc = h // cr
    n = 4 * nc
    slots = 4

    def body(pos_ref, own_ref, send_ref, s_ref, sb_ref, stage, rbuf, send_sems, recv_sems):
        x, y, c, _ = _mesh_pos()
        k = pl.program_id(0)

        def copy(chunk):
            return pltpu.make_async_remote_copy(
                src_ref=stage.at[chunk % 2], dst_ref=rbuf.at[chunk % slots], send_sem=send_sems.at[chunk % 2],
                recv_sem=recv_sems.at[chunk % slots], device_id=(x, y, 1 - c), device_id_type=MESH)

        @pl.when(k >= 2)
        def _():
            copy(k - 2).wait_send()

        @pl.when(k < n)
        def _():
            stage[k % 2] = send_ref[...]
            copy(k).start()

        @pl.when(k > 0)
        def _():
            copy(k - 1).wait_recv()
            s = own_ref[...] + rbuf[(k - 1) % slots]
            s_ref[...] = s
            sb_ref[...] = s.astype(BF16)

        @pl.when(k == n)
        def _():
            copy(k - 1).wait_send()

    def own(k, p):
        j = jnp.maximum(k - 1, 0)
        return ((2 * (j // nc) + p[1]) * nc + j % nc, 0)

    def send(k, p):
        j = jnp.minimum(k, n - 1)
        return ((2 * (j // nc) + 1 - p[1]) * nc + j % nc, 0)

    grid_spec = pltpu.PrefetchScalarGridSpec(
        num_scalar_prefetch=1, grid=(n + 1,),
        in_specs=[pl.BlockSpec((cr, w), own), pl.BlockSpec((cr, w), send)],
        out_specs=[pl.BlockSpec((cr, w), lambda k, p: (jnp.maximum(k - 1, 0), 0))] * 2,
        scratch_shapes=[pltpu.VMEM((2, cr, w), F32), pltpu.VMEM((slots, cr, w), F32),
                        pltpu.SemaphoreType.DMA((2,)), pltpu.SemaphoreType.DMA((slots,))])
    part2 = part.reshape(8 * h, w)
    s, sb = pl.pallas_call(
        body, name=name, grid_spec=grid_spec, compiler_params=_params(),
        out_shape=[jax.ShapeDtypeStruct((4 * h, w), F32), jax.ShapeDtypeStruct((4 * h, w), BF16)],
    )(pos, part2, part2)
    return s.reshape(4, h, w), sb.reshape(4, h, w)


class _Ride:
    def __init__(self, ins, out_shape, sems, copies):
        self.ins, self.out_shape, self.sems, self.copies = list(ins), list(out_shape), list(sems), copies

    def start(self, ins, outs, sems):
        sends, _, _, local = self.copies(ins, outs, sems)
        for cp in local + sends:
            cp.start()

    def finish(self, ins, outs, sems):
        _, recvs, sends, local = self.copies(ins, outs, sems)
        for cp in recvs:
            cp.wait_recv()
        for cp in sends:
            cp.wait_send()
        for cp in local:
            cp.wait()


def _scatter_ride(sums_bf16):
    n = len(sums_bf16)

    def copies(ins, outs, sems):
        send_sems, recv_sems = sems
        x, y, c, chips = _mesh_pos()
        cps = [pltpu.make_async_remote_copy(
            src_ref=ins[t].at[2 * chip[0] + chip[1]], dst_ref=outs[t].at[j],
            send_sem=send_sems.at[3 * t + j], recv_sem=recv_sems.at[3 * t + j],
            device_id=(*chip, c), device_id_type=MESH) for t in range(n) for j, chip in enumerate(chips)]
        return cps, cps, cps, []

    return _Ride(sums_bf16, [jax.ShapeDtypeStruct((3,) + s.shape[1:], BF16) for s in sums_bf16],
                 [pltpu.SemaphoreType.DMA((3 * n,)), pltpu.SemaphoreType.DMA((3 * n,))], copies)


def _broadcast_ride(shards):
    n = len(shards)

    def copies(ins, outs, sems):
        send_sems, recv_sems, local_sems = sems
        x, y, c, chips = _mesh_pos()
        k = 2 * x + y
        sends, recvs, local = [], [], []
        for t in range(n):
            r = ins[t].shape[1]
            h = r // 2
            local.append(pltpu.make_async_copy(ins[t], outs[t].at[:, pl.ds(k * r, r), :], local_sems.at[t]))
            src = ins[t].at[:, pl.ds(c * h, h), :]
            mine = outs[t].at[:, pl.ds(k * r + c * h, h), :]
            for j, chip in enumerate(chips):
                kj = 2 * chip[0] + chip[1]
                for d in range(2):
                    sends.append(pltpu.make_async_remote_copy(
                        src_ref=src, dst_ref=mine, send_sem=send_sems.at[6 * t + 2 * j + d],
                        recv_sem=recv_sems.at[6 * t + 2 * j + c], device_id=(*chip, d), device_id_type=MESH))
                    theirs = outs[t].at[:, pl.ds(kj * r + d * h, h), :]
                    recvs.append(pltpu.make_async_remote_copy(
                        src_ref=theirs, dst_ref=theirs, send_sem=send_sems.at[6 * t + 2 * j + d],
                        recv_sem=recv_sems.at[6 * t + 2 * j + d], device_id=(*chip, d), device_id_type=MESH))
        return sends, recvs, sends, local

    return _Ride(shards, [jax.ShapeDtypeStruct((s.shape[0], N_CHIPS * s.shape[1], s.shape[2]), s.dtype) for s in shards],
                 [pltpu.SemaphoreType.DMA((6 * n,)), pltpu.SemaphoreType.DMA((6 * n,)), pltpu.SemaphoreType.DMA((n,))],
                 copies)


def _ride_call(body, ride, args, *, name, grid, in_specs, out_specs, out_shape, compiler_params=None):
    in_specs, out_specs, out_shape = list(in_specs), list(out_specs), list(out_shape)
    if ride is None:
        res = pl.pallas_call(body, name=name, grid=grid, in_specs=in_specs, out_specs=out_specs, out_shape=out_shape,
                             compiler_params=compiler_params)(*args)
        return list(res), []
    ni, no, ri, ro = len(in_specs), len(out_specs), len(ride.ins), len(ride.out_shape)

    def at_step(pick):
        hit = None
        for d, n in enumerate(grid):
            here = pl.program_id(d) == pick(n)
            hit = here if hit is None else hit & here
        return hit

    def carried(*refs):
        ins, rins = refs[:ni], refs[ni:ni + ri]
        outs, routs = refs[ni + ri:ni + ri + no], refs[ni + ri + no:ni + ri + no + ro]
        sems = refs[ni + ri + no + ro:]

        @pl.when(at_step(lambda n: 0))
        def _():
            ride.start(rins, routs, sems)

        body(*ins, *outs)

        @pl.when(at_step(lambda n: n - 1))
        def _():
            ride.finish(rins, routs, sems)

    res = pl.pallas_call(
        carried, name=name, grid=grid, in_specs=in_specs + _hbm_specs(ri), out_specs=out_specs + _hbm_specs(ro),
        out_shape=out_shape + ride.out_shape, scratch_shapes=ride.sems, compiler_params=compiler_params,
    )(*args, *ride.ins)
    return list(res[:no]), list(res[no:])


def _sum_and_join(sums, got, pos, name):
    n = len(sums)
    _, h, w = sums[0].shape
    cr = _chunk_rows(h, w, limit=(2 << 20) // n)

    def body(pos_ref, *refs):
        mine, theirs, outs = refs[:n], refs[n:2 * n], refs[2 * n:3 * n]
        ebuf, rbuf, send_sems, recv_sems = refs[3 * n:]
        x, y, c, _ = _mesh_pos()
        slot = pl.program_id(0) % 2
        cps = []
        for t in range(n):
            e = mine[t][...]
            for j in range(3):
                e = e + theirs[t][j].astype(F32)
            ebuf[t, slot] = e
            cp = pltpu.make_async_remote_copy(
                src_ref=ebuf.at[t, slot], dst_ref=rbuf.at[t, slot], send_sem=send_sems.at[2 * t + slot],
                recv_sem=recv_sems.at[2 * t + slot], device_id=(x, y, 1 - c), device_id_type=MESH)
            cp.start()
            outs[t][pos_ref[1]] = e
            cps.append(cp)
        for t, cp in enumerate(cps):
            cp.wait_recv()
            outs[t][1 - pos_ref[1]] = rbuf[t, slot]
        for cp in cps:
            cp.wait_send()

    grid_spec = pltpu.PrefetchScalarGridSpec(
        num_scalar_prefetch=1, grid=(h // cr,),
        in_specs=[pl.BlockSpec((None, cr, w), lambda i, p: (p[0], i, 0))] * n +
                 [pl.BlockSpec((3, cr, w), lambda i, p: (0, i, 0))] * n,
        out_specs=[pl.BlockSpec((2, cr, w), lambda i, p: (0, i, 0))] * n,
        scratch_shapes=[pltpu.VMEM((n, 2, cr, w), F32), pltpu.VMEM((n, 2, cr, w), F32),
                        pltpu.SemaphoreType.DMA((2 * n,)), pltpu.SemaphoreType.DMA((2 * n,))])
    return pl.pallas_call(
        body, name=name, grid_spec=grid_spec, compiler_params=_params(),
        out_shape=[jax.ShapeDtypeStruct((2, h, w), F32)] * n,
    )(pos, *sums, *got)


def _pair_sums(parts, pos, tag):
    pairs = [_pair_sum(p.reshape(8, p.shape[0] // 8, p.shape[1]), pos, f"rs_pair_{tag}_{t}")
             for t, p in enumerate(parts)]
    return [s for s, _ in pairs], [sb for _, sb in pairs]


def _joins(sums, got, pos, tag):
    full = _sum_and_join(sums, got, pos, f"rs_join_{tag}")
    return [f.reshape(2 * f.shape[1], f.shape[2]) for f in full]


def _ffn_fwd(x, prm, w13g, w2g, t, name, tm=512, ride=None):
    L, D = x.shape
    Fh = w13g.shape[-1]
    tm = min(tm, L)

    def body(x_ref, p_ref, w13_ref, w2_ref, xo_ref, sv_ref, g_ref, y_ref):
        xv = x_ref[...]
        hn, _, _, _ = _nm(xv, p_ref[3:4, :], p_ref[0:1, :], p_ref[1:2, :])
        hb = hn.astype(BF16)
        acc = jnp.zeros((tm, D), F32)
        for j in range(2):
            a = _dot(hb, w13_ref[j])
            b = _dot(hb, w13_ref[2 + j])
            sg = _sigmoid(a)
            sa = a * sg
            sv_ref[:, j * Fh:(j + 1) * Fh] = sa.astype(BF16)
            sv_ref[:, (2 + j) * Fh:(3 + j) * Fh] = (b * (sg * (1.0 + a * (1.0 - sg)))).astype(BF16)
            g = (sa * b).astype(BF16)
            g_ref[:, j * Fh:(j + 1) * Fh] = g
            acc = acc + _dot(g, w2_ref[j * Fh:(j + 1) * Fh, :])
        y_ref[...] = acc.astype(BF16)
        xo_ref[...] = xv + (0.5 * p_ref[2:3, :]) * acc

    res, carried = _ride_call(
        body, ride, (x, prm, w13g, w2g), name=name, grid=(L // tm,),
        in_specs=[_tile(tm, D), _full((8, D)),
                  _resident((None, 4, D, Fh), lambda i: (t, 0, 0, 0)),
                  _resident((None, 2 * Fh, D), lambda i: (t, 0, 0))],
        out_specs=[_tile(tm, D), _tile(tm, 4 * Fh), _tile(tm, 2 * Fh), _tile(tm, D)],
        out_shape=[jax.ShapeDtypeStruct((L, D), F32), jax.ShapeDtypeStruct((L, 4 * Fh), BF16),
                   jax.ShapeDtypeStruct((L, 2 * Fh), BF16), jax.ShapeDtypeStruct((L, D), BF16)],
        compiler_params=_params())
    xo, sv, g, y = res
    return xo, (sv, g), y, carried


def _head_grad(xo, tgt, fg):
    D = xo.shape[-1]
    r = lax.rsqrt(jnp.mean(xo * xo, axis=-1, keepdims=True) + RMS_EPS)
    xhat = xo * r
    err = xhat * fg - tgt
    loss = 0.5 * jnp.sum(jnp.mean(err * err, axis=-1, keepdims=True), axis=0, keepdims=True)
    dy = err * (1.0 / D)
    dxh = dy * fg
    return r * (dxh - xhat * jnp.mean(dxh * xhat, axis=-1, keepdims=True)), _sum0(dy * xhat), loss


def _ffn_bwd(dout, x, sv, y, prm, w13g, w2g, t, name, tm=256, ride=None, head=None):
    L, D = x.shape
    Fh = w13g.shape[-1]
    tm = min(tm, L)

    def body(do_ref, x_ref, sv_ref, y_ref, p_ref, w13_ref, w2_ref, *rest):
        dx_ref, dab_ref, dy_ref, hn_ref, acc_ref = rest[-5:]
        i = pl.program_id(0)
        head_rows = []
        if head is None:
            do = do_ref[...]
        else:
            do, dfg, loss = _head_grad(do_ref[...], rest[0][...], rest[1][...])
            head_rows = [dfg, jnp.broadcast_to(loss, (1, D))]
        gain, shift, scale, gate = p_ref[3:4, :], p_ref[0:1, :], p_ref[1:2, :], p_ref[2:3, :]
        hn, xhat, r, nrm = _nm(x_ref[...], gain, shift, scale)
        hn_ref[...] = hn.astype(BF16)
        dgate = 0.5 * _sum0(do * y_ref[...].astype(F32))
        dyb = ((0.5 * gate) * do).astype(BF16)
        dy_ref[...] = dyb
        dhn = jnp.zeros((tm, D), F32)
        for j in range(2):
            dg = _dot_nt(dyb, w2_ref[j * Fh:(j + 1) * Fh, :])
            da = (dg * sv_ref[:, (2 + j) * Fh:(3 + j) * Fh].astype(F32)).astype(BF16)
            db = (dg * sv_ref[:, j * Fh:(j + 1) * Fh].astype(F32)).astype(BF16)
            dab_ref[:, j * Fh:(j + 1) * Fh] = da
            dab_ref[:, (2 + j) * Fh:(3 + j) * Fh] = db
            dhn = dhn + _dot_nt(da, w13_ref[j]) + _dot_nt(db, w13_ref[2 + j])
        dx, dshift, dscale, dgn = _nm_bwd(dhn, xhat, r, nrm, gain, scale)
        dx_ref[...] = do + dx
        _acc_rows(acc_ref, i == 0, [dshift, dscale, dgate, dgn] + head_rows)

    head_args = () if head is None else head
    head_specs = [] if head is None else [_tile(tm, D), _full((1, D))]
    res, carried = _ride_call(
        body, ride, (dout, x, sv, y, prm, w13g, w2g, *head_args), name=name, grid=(L // tm,),
        in_specs=[_tile(tm, D), _tile(tm, D), _tile(tm, 4 * Fh), _tile(tm, D), _full((8, D)),
                  _resident((None, 4, D, Fh), lambda i: (t, 0, 0, 0)),
                  _resident((None, 2 * Fh, D), lambda i: (t, 0, 0))] + head_specs,
        out_specs=[_tile(tm, D), _tile(tm, 4 * Fh), _tile(tm, D), _tile(tm, D), _full((8, D))],
        out_shape=[jax.ShapeDtypeStruct((L, D), F32), jax.ShapeDtypeStruct((L, 4 * Fh), BF16),
                   jax.ShapeDtypeStruct((L, D), BF16), jax.ShapeDtypeStruct((L, D), BF16),
                   jax.ShapeDtypeStruct((8, D), F32)],
        compiler_params=_params())
    return (*res, carried)


def _mm_tn(a, b, slabs, a_slabbed, name, init=None, tl=1024, ride=None):
    L = a.shape[0]
    ka = a.shape[1] // slabs if a_slabbed else a.shape[1]
    nb = b.shape[1] if a_slabbed else b.shape[1] // slabs
    tl = min(tl, L)
    has_init = init is not None

    def body(a_ref, b_ref, *rest):
        o_ref = rest[-1]
        step = pl.program_id(1)

        @pl.when(step == 0)
        def _():
            o_ref[...] = rest[0][...] if has_init else jnp.zeros((ka, nb), F32)

        o_ref[...] += _dot_tn(a_ref[...], b_ref[...])

    in_specs = [pl.BlockSpec((tl, ka), (lambda s, l: (l, s)) if a_slabbed else (lambda s, l: (l, 0))),
                pl.BlockSpec((tl, nb), (lambda s, l: (l, 0)) if a_slabbed else (lambda s, l: (l, s)))]
    args = [a, b]
    if has_init:
        in_specs.append(pl.BlockSpec((ka, nb), lambda s, l: (s, 0)))
        args.append(init)
    res, carried = _ride_call(
        body, ride, args, name=name, grid=(slabs, L // tl), in_specs=in_specs,
        out_specs=[pl.BlockSpec((ka, nb), lambda s, l: (s, 0))],
        out_shape=[jax.ShapeDtypeStruct((slabs * ka, nb), F32)], compiler_params=_params())
    return res[0] if ride is None else (res[0], carried)


def _even_in_fwd(x, prm, wing, name, tm=512):
    L, D = x.shape
    W = wing.shape[-1]
    tm = min(tm, L)

    def body(x_ref, p_ref, w_ref, q_ref, k_ref, v_ref, u_ref, hn_ref):
        hn, _, _, _ = _nm(x_ref[...], p_ref[3:4, :], p_ref[0:1, :], p_ref[1:2, :])
        hb = hn.astype(BF16)
        hn_ref[...] = hb
        q_ref[...] = _dot(hb, w_ref[0]).astype(BF16)
        k_ref[...] = _dot(hb, w_ref[1]).astype(BF16)
        v_ref[...] = _dot(hb, w_ref[2]).astype(BF16)
        u_ref[...] = _dot(hb, w_ref[3])

    return pl.pallas_call(
        body, name=name, grid=(L // tm,),
        in_specs=[_tile(tm, D), _full((8, D)), _resident((None, 4, D, W), lambda i: (0, 0, 0, 0))],
        out_specs=[_tile(tm, W)] * 4 + [_tile(tm, D)],
        out_shape=[jax.ShapeDtypeStruct((L, W), BF16)] * 3 + [jax.ShapeDtypeStruct((L, W), F32),
                                                              jax.ShapeDtypeStruct((L, D), BF16)],
        compiler_params=_params())(x, prm, wing)


def _even_in_bwd(dout, x, dq, dk, dv, du, prm, wing, name, tm=512):
    L, D = x.shape
    W = wing.shape[-1]
    tm = min(tm, L)

    def body(do_ref, x_ref, dq_ref, dk_ref, dv_ref, du_ref, p_ref, w_ref, dx_ref, ds_ref, acc_ref):
        i = pl.program_id(0)
        gain, shift, scale = p_ref[3:4, :], p_ref[0:1, :], p_ref[1:2, :]
        _, xhat, r, nrm = _nm(x_ref[...], gain, shift, scale)
        dhn = jnp.zeros((tm, D), F32)
        for s, ref in enumerate((dq_ref, dk_ref, dv_ref, du_ref)):
            d = ref[...].astype(BF16)
            ds_ref[:, s * W:(s + 1) * W] = d
            dhn = dhn + _dot_nt(d, w_ref[s])
        dx, dshift, dscale, dgn = _nm_bwd(dhn, xhat, r, nrm, gain, scale)
        dx_ref[...] = do_ref[...] + dx
        _acc_rows(acc_ref, i == 0, [dshift, dscale, None, dgn])

    return pl.pallas_call(
        body, name=name, grid=(L // tm,),
        in_specs=[_tile(tm, D), _tile(tm, D)] + [_tile(tm, W)] * 4 +
                 [_full((8, D)), _resident((None, 4, D, W), lambda i: (0, 0, 0, 0))],
        out_specs=[_tile(tm, D), _tile(tm, 4 * W), _full((8, D))],
        out_shape=[jax.ShapeDtypeStruct((L, D), F32), jax.ShapeDtypeStruct((L, 4 * W), BF16),
                   jax.ShapeDtypeStruct((8, D), F32)],
        compiler_params=_params())(dout, x, dq, dk, dv, du, prm, wing)


def _even_out_fwd(x, att, pool, prm, woutg, name, tm=512):
    L, D = x.shape
    W = D // 2
    tm = min(tm, L)

    def body(x_ref, a_ref, p_ref, prm_ref, w_ref, xo_ref, y_ref):
        yv = _dot(a_ref[...], w_ref[0:W, :]) + _dot(p_ref[...], w_ref[W:2 * W, :])
        y_ref[...] = yv.astype(BF16)
        xo_ref[...] = x_ref[...] + prm_ref[2:3, :] * yv

    return pl.pallas_call(
        body, name=name, grid=(L // tm,),
        in_specs=[_tile(tm, D), _tile(tm, W), _tile(tm, W), _full((8, D)),
                  _resident((None, D, D), lambda i: (0, 0, 0))],
        out_specs=[_tile(tm, D), _tile(tm, D)],
        out_shape=[jax.ShapeDtypeStruct((L, D), F32), jax.ShapeDtypeStruct((L, D), BF16)],
        compiler_params=_params())(x, att, pool, prm, woutg)


def _even_out_bwd(dout, y, prm, woutg, name, tm=512):
    L, D = dout.shape
    W = D // 2
    tm = min(tm, L)

    def body(do_ref, y_ref, p_ref, w_ref, dy_ref, da_ref, dp_ref, acc_ref):
        i = pl.program_id(0)
        do = do_ref[...]
        dgate = _sum0(do * y_ref[...].astype(F32))
        dyb = (p_ref[2:3, :] * do).astype(BF16)
        dy_ref[...] = dyb
        da_ref[...] = _dot_nt(dyb, w_ref[0:W, :]).astype(BF16)
        dp_ref[...] = _dot_nt(dyb, w_ref[W:2 * W, :])
        _acc_rows(acc_ref, i == 0, [None, None, dgate])

    return pl.pallas_call(
        body, name=name, grid=(L // tm,),
        in_specs=[_tile(tm, D), _tile(tm, D), _full((8, D)), _resident((None, D, D), lambda i: (0, 0, 0))],
        out_specs=[_tile(tm, D), _tile(tm, W), _tile(tm, W), _full((8, D))],
        out_shape=[jax.ShapeDtypeStruct((L, D), BF16), jax.ShapeDtypeStruct((L, W), BF16),
                   jax.ShapeDtypeStruct((L, W), F32), jax.ShapeDtypeStruct((8, D), F32)],
        compiler_params=_params())(dout, y, prm, woutg)


def _group_ri(variant, qr, kr):
    first_key = (0, qr, GK - NA_KH)[variant]
    if not first_key <= kr < first_key + NA_KH:
        return None
    return kr - qr + (NA_KH - 1, NA_KH - 1 - NA_KH // 2, NA_KH - 1 - (GK - GQ))[variant]


HEADS_PER_BLOCK = 128 // NA_HEAD_DIM


def _bias_table(rpb, name):
    H = rpb.shape[0]
    hpb = HEADS_PER_BLOCK
    nri, nci = 2 * NA_KH - 1, 2 * NA_KW - 1
    col = jnp.arange(GRID_W)
    rel = (col[None, :] - col[:, None] + (NA_KW - 1)).reshape(1, -1)
    onehot = (rel == jnp.arange(32)[:, None]).astype(F32)
    cs = jnp.clip(col - NA_KW // 2, 0, GRID_W - NA_KW)
    ok = ((col[None, :] >= cs[:, None]) & (col[None, :] < cs[:, None] + NA_KW)).astype(F32).reshape(1, -1)
    by_lane_block = rpb.reshape(H // hpb, hpb, nri, nci).transpose(1, 0, 2, 3)
    rpb2 = jnp.pad(by_lane_block.reshape(H * nri, nci), ((0, 0), (0, 32 - nci)))

    def body(r_ref, e_ref, m_ref, o_ref):
        t = jnp.dot(r_ref[...], e_ref[...], preferred_element_type=F32, precision=lax.Precision.HIGHEST)
        o_ref[...] = jnp.where(m_ref[...] > 0.0, t, NEG_INF)

    tab = pl.pallas_call(body, name=name, out_shape=jax.ShapeDtypeStruct((H * nri, GRID_W * GRID_W), F32))(
        rpb2, onehot, ok)
    tab = tab.reshape(hpb, H // hpb, nri, GRID_W, GRID_W)
    outside = jnp.full((H // hpb, GRID_W, GRID_W), NEG_INF, F32)
    variants = []
    for variant in range(3):
        rows = []
        for h in range(hpb):
            for qr in range(GQ):
                ris = [_group_ri(variant, qr, kr) for kr in range(GK)]
                rows.append(jnp.concatenate([outside if ri is None else tab[h, :, ri] for ri in ris], axis=2))
        variants.append(jnp.concatenate(rows, axis=1))
    return jnp.stack(variants, axis=1)


def _attn_probs(q, kw, kc, bias, scale):
    s_w = _dot_nt(q, kw) * scale + bias
    s_c = _dot_nt(q, kc) * scale
    m = jnp.maximum(jnp.max(s_w, axis=-1, keepdims=True), jnp.max(s_c, axis=-1, keepdims=True))
    e_w = jnp.exp(s_w - m)
    e_c = jnp.exp(s_c - m)
    inv = 1.0 / (jnp.sum(e_w, axis=-1, keepdims=True) + jnp.sum(e_c, axis=-1, keepdims=True))
    return e_w * inv, e_c * inv


def _group_place(g, R):
    G = R // GQ
    kb = jnp.clip(g * GQ - NA_KH // 2, 0, R - GK)
    variant = jnp.where(g == 0, 0, jnp.where(g == G - 1, 2, 1))
    return pl.multiple_of(g * (GQ * GRID_W), GQ * GRID_W), pl.multiple_of(kb * GRID_W, GRID_W), variant


def _lane_masks(width, dh):
    lane = lax.broadcasted_iota(jnp.int32, (1, width), 1)
    return [(lane >= h * dh) & (lane < (h + 1) * dh) for h in range(width // dh)]


def _only(mask, a):
    return jnp.where(mask, a, jnp.zeros_like(a))


def _attn_fwd(q, k, v, kc, vc, bias, name, ride=None):
    L, width = q.shape
    C = kc.shape[0]
    dh = NA_HEAD_DIM
    lanes = 128
    hpb = lanes // dh
    R = L // GRID_W
    nq, nk = GQ * GRID_W, GK * GRID_W
    scale = dh ** -0.5

    def body(q_ref, k_ref, v_ref, kc_ref, vc_ref, b_ref, o_ref):
        masks = _lane_masks(lanes, dh)
        kc2 = kc_ref[...]
        vcs = [_only(m, vc_ref[...]) for m in masks]

        def group(g, carry):
            q0, k0, variant = _group_place(g, R)
            q2 = q_ref[pl.ds(q0, nq), :]
            k2 = k_ref[pl.ds(k0, nk), :]
            v2 = v_ref[pl.ds(k0, nk), :]
            qs = jnp.concatenate([_only(m, q2) for m in masks], axis=0)
            p_w, p_c = _attn_probs(qs, k2, kc2, b_ref[variant], scale)
            p_w, p_c = p_w.astype(BF16), p_c.astype(BF16)
            o2 = jnp.zeros((nq, lanes), F32)
            for h, m in enumerate(masks):
                rows = slice(h * nq, (h + 1) * nq)
                o2 = o2 + _dot(p_w[rows], _only(m, v2)) + _dot(p_c[rows], vcs[h])
            o_ref[pl.ds(q0, nq), :] = o2.astype(BF16)
            return carry

        lax.fori_loop(0, R // GQ, group, 0)

    cols = lambda n: pl.BlockSpec((n, lanes), lambda p: (0, p))
    res, carried = _ride_call(
        body, ride, (q, k, v, kc, vc, bias), name=name, grid=(width // lanes,),
        in_specs=[cols(L), cols(L), cols(L), cols(C), cols(C),
                  pl.BlockSpec((None, 3, hpb * nq, nk), lambda p: (p, 0, 0, 0))],
        out_specs=[cols(L)], out_shape=[jax.ShapeDtypeStruct((L, width), BF16)],
        compiler_params=_params())
    return res[0], carried


def _attn_bwd(q, k, v, kc, vc, bias, do, name, ride=None):
    L, width = q.shape
    C = kc.shape[0]
    dh = NA_HEAD_DIM
    lanes = 128
    hpb = lanes // dh
    R = L // GRID_W
    nq, nk = GQ * GRID_W, GK * GRID_W
    scale = dh ** -0.5

    def body(q_ref, k_ref, v_ref, kc_ref, vc_ref, b_ref, do_ref, dq_ref, dk_ref, dv_ref, dkc_ref, dvc_ref, db_ref):
        masks = _lane_masks(lanes, dh)
        kc2 = kc_ref[...]
        vc2 = vc_ref[...]
        kcs = [_only(m, kc2) for m in masks]
        dk_ref[...] = jnp.zeros((L, lanes), F32)
        dv_ref[...] = jnp.zeros((L, lanes), F32)
        dkc_ref[...] = jnp.zeros((C, lanes), F32)
        dvc_ref[...] = jnp.zeros((C, lanes), F32)
        db_ref[...] = jnp.zeros((3, hpb * nq, nk), F32)

        def group(g, carry):
            q0, k0, variant = _group_place(g, R)
            q2 = q_ref[pl.ds(q0, nq), :]
            k2 = k_ref[pl.ds(k0, nk), :]
            v2 = v_ref[pl.ds(k0, nk), :]
            do2 = do_ref[pl.ds(q0, nq), :]
            qs = jnp.concatenate([_only(m, q2) for m in masks], axis=0)
            dos = jnp.concatenate([_only(m, do2) for m in masks], axis=0)
            p_w, p_c = _attn_probs(qs, k2, kc2, b_ref[variant], scale)
            dp_w = _dot_nt(dos, v2)
            dp_c = _dot_nt(dos, vc2)
            delta = jnp.sum(p_w * dp_w, axis=-1, keepdims=True) + jnp.sum(p_c * dp_c, axis=-1, keepdims=True)
            ds_w = p_w * (dp_w - delta)
            ds_c = p_c * (dp_c - delta)
            db_ref[variant] += ds_w
            dsw = (ds_w * scale).astype(BF16)
            dsc = (ds_c * scale).astype(BF16)
            dq2 = jnp.zeros((nq, lanes), F32)
            for h, m in enumerate(masks):
                rows = slice(h * nq, (h + 1) * nq)
                dq2 = dq2 + _dot(dsw[rows], _only(m, k2)) + _dot(dsc[rows], kcs[h])
            dq_ref[pl.ds(q0, nq), :] = dq2.astype(BF16)
            dk_ref[pl.ds(k0, nk), :] += _dot_tn(dsw, qs)
            dv_ref[pl.ds(k0, nk), :] += _dot_tn(p_w.astype(BF16), dos)
            dkc_ref[...] += _dot_tn(dsc, qs)
            dvc_ref[...] += _dot_tn(p_c.astype(BF16), dos)
            return carry

        lax.fori_loop(0, R // GQ, group, 0)

    cols = lambda n: _resident((n, lanes), lambda p: (0, p))
    bspec = _resident((None, 3, hpb * nq, nk), lambda p: (p, 0, 0, 0))
    res, carried = _ride_call(
        body, ride, (q, k, v, kc, vc, bias, do), name=name, grid=(width // lanes,),
        in_specs=[cols(L), cols(L), cols(L), cols(C), cols(C), bspec, cols(L)],
        out_specs=[cols(L), cols(L), cols(L), cols(C), cols(C), bspec],
        out_shape=[jax.ShapeDtypeStruct((L, width), BF16)] + [jax.ShapeDtypeStruct((L, width), F32)] * 2 +
                  [jax.ShapeDtypeStruct((C, width), F32)] * 2 +
                  [jax.ShapeDtypeStruct((width // lanes, 3, hpb * nq, nk), F32)],
        compiler_params=_params())
    return (*res, carried)


def _rpb_grad(dbias, name):
    hpb = HEADS_PER_BLOCK
    H = dbias.shape[0] * hpb
    nri, nci = 2 * NA_KH - 1, 2 * NA_KW - 1
    d6 = dbias.reshape(H // hpb, 3, hpb, GQ, GRID_W, GK, GRID_W).transpose(0, 2, 1, 3, 5, 4, 6)
    d6 = d6.reshape(H, 3, GQ, GK, GRID_W, GRID_W)
    col = jnp.arange(GRID_W)
    onehot = (col[None, None, :] - col[None, :, None] + (NA_KW - 1) == jnp.arange(32)[:, None, None]).astype(F32)
    places = [(v, qr, kr) for v in range(3) for qr in range(GQ) for kr in range(GK)]

    def body(d_ref, m_ref, o_ref, t_ref):
        t_ref[...] = jnp.zeros((32, GRID_W), F32)
        o_ref[...] = jnp.zeros((16, 32, 128), F32)
        for ri in range(nri):
            a = None
            for place in places:
                if _group_ri(*place) == ri:
                    blk = d_ref[place]
                    a = blk if a is None else a + blk
            for ci in range(nci):
                t_ref[ci:ci + 1, :] = _sum0(a * m_ref[ci])
            o_ref[ri] = jnp.broadcast_to(jnp.sum(t_ref[...], axis=1, keepdims=True), (32, 128))

    out = pl.pallas_call(
        body, name=name, grid=(H,),
        in_specs=[pl.BlockSpec((None, 3, GQ, GK, GRID_W, GRID_W), lambda h: (h, 0, 0, 0, 0, 0)),
                  pl.BlockSpec((32, GRID_W, GRID_W), lambda h: (0, 0, 0))],
        out_specs=pl.BlockSpec((None, 16, 32, 128), lambda h: (h, 0, 0, 0)),
        out_shape=jax.ShapeDtypeStruct((H, 16, 32, 128), F32),
        scratch_shapes=[pltpu.VMEM((32, GRID_W), F32)])(d6, onehot)
    return out[:, :nri, :nci, 0]


def _window_count(t, w, L):
    lo = jnp.clip(t - w // 2, 0, L)
    hi = jnp.clip(t - w // 2 + w, 0, L)
    return jnp.maximum(hi - lo, 1).astype(F32)


def _running_sum(v, w):
    k = 1
    while k < w:
        v = v + _shift_rows(v, k)
        k *= 2
    return v


def _pool_fwd(u, poolw, pscale, name, tm=512):
    L, W = u.shape
    G = POOL_GROUP_DIM
    tm = min(tm, L)
    nt = L // tm

    def body(c_ref, p_ref, n_ref, w_ref, s_ref, o_ref, dm_ref):
        i = pl.program_id(0)
        ext = _ext(p_ref[...], c_ref[...], n_ref[...], i, nt)
        t = i * tm + lax.broadcasted_iota(jnp.int32, (tm, 1), 0)
        for g, w in enumerate(POOL_WINDOWS):
            e = ext[:, g * G:(g + 1) * G]
            win = _shift_rows(_running_sum(e, w), -(w // 2 - 1))[HALO:HALO + tm]
            dmx = (win / _window_count(t, w, L) - e[HALO:HALO + tm]).astype(BF16)
            dm_ref[:, g * G:(g + 1) * G] = dmx
            o_ref[:, g * G:(g + 1) * G] = (_dot(dmx, w_ref[g]) * s_ref[:, g * G:(g + 1) * G]).astype(BF16)

    return pl.pallas_call(
        body, name=name, grid=(nt,),
        in_specs=[_tile(tm, W), _halo_prev(tm, W), _halo_next(tm, W, L), _full((4, G, G)), _full((1, W))],
        out_specs=[_tile(tm, W), _tile(tm, W)],
        out_shape=[jax.ShapeDtypeStruct((L, W), BF16)] * 2, compiler_params=_params())(u, u, u, poolw, pscale)


def _pool_bwd(dpool, dmx, poolw, pscale, name, tm=512):
    L, W = dpool.shape
    G = POOL_GROUP_DIM
    tm = min(tm, L)
    nt = L // tm

    def body(c_ref, p_ref, n_ref, dm_ref, w_ref, s_ref, du_ref, dw_ref, acc_ref):
        i = pl.program_id(0)
        ext = _ext(p_ref[...], c_ref[...], n_ref[...], i, nt)
        te = i * tm - HALO + lax.broadcasted_iota(jnp.int32, (tm + 2 * HALO, 1), 0)

        @pl.when(i == 0)
        def _():
            dw_ref[...] = jnp.zeros((4 * G, G), F32)

        rows = []
        for g, w in enumerate(POOL_WINDOWS):
            sc = s_ref[:, g * G:(g + 1) * G]
            dpre = (ext[:, g * G:(g + 1) * G] * sc).astype(BF16)
            dd = _dot_nt(dpre, w_ref[g])
            spread = _shift_rows(_running_sum(dd / _window_count(te, w, L), w), -(w // 2))
            du_ref[:, g * G:(g + 1) * G] = (spread - dd)[HALO:HALO + tm]
            dmx_g = dm_ref[:, g * G:(g + 1) * G]
            rows.append(_sum0(c_ref[:, g * G:(g + 1) * G] * _dot(dmx_g, w_ref[g])))
            dw_ref[g * G:(g + 1) * G, :] += _dot_tn(dmx_g, dpre[HALO:HALO + tm])
        _acc_rows(acc_ref, i == 0, [jnp.concatenate(rows, axis=1)])

    return pl.pallas_call(
        body, name=name, grid=(nt,),
        in_specs=[_tile(tm, W), _halo_prev(tm, W), _halo_next(tm, W, L), _tile(tm, W), _full((4, G, G)),
                  _full((1, W))],
        out_specs=[_tile(tm, W), _full((4 * G, G)), _full((8, W))],
        out_shape=[jax.ShapeDtypeStruct((L, W), F32), jax.ShapeDtypeStruct((4 * G, G), F32),
                   jax.ShapeDtypeStruct((8, W), F32)],
        compiler_params=_params())(dpool, dpool, dpool, dmx, poolw, pscale)


def _conv3(z, cw):
    return _shift_rows(z, 1) * cw[0] + z * cw[1] + _shift_rows(z, -1) * cw[2]


def _conv_fwd(x, prm, wing, woutg, name, tm=512):
    L, D = x.shape
    Ws = wing.shape[-1]
    tm = min(tm, L)
    nt = L // tm
    te = tm + 2 * HALO

    def body(c_ref, p_ref, n_ref, prm_ref, wi_ref, wo_ref, xo_ref, y_ref, b_ref):
        i = pl.program_id(0)
        xe = jnp.concatenate([p_ref[...], c_ref[...], n_ref[...]], axis=0)
        hn, _, _, _ = _nm(xe, prm_ref[3:4, :], prm_ref[0:1, :], prm_ref[1:2, :])
        hb = hn.astype(BF16)
        proj = jnp.concatenate([_dot(hb, wi_ref[s]) for s in range(4)], axis=1)
        bg, cg, xin = proj[:, :D], proj[:, D:2 * D], proj[:, 2 * D:]
        tpos = i * tm - HALO + lax.broadcasted_iota(jnp.int32, (te, 1), 0)
        valid = ((tpos >= 0) & (tpos < L)).astype(F32)
        yc = _conv3(cg * xin * valid, [prm_ref[4 + k:5 + k, :] for k in range(3)])
        h2 = (bg * yc)[HALO:HALO + tm].astype(BF16)
        yv = _dot(h2, wo_ref[...])
        y_ref[...] = yv.astype(BF16)
        xo_ref[...] = c_ref[...] + prm_ref[2:3, :] * yv
        b_ref[...] = proj[HALO:HALO + tm].astype(BF16)

    return pl.pallas_call(
        body, name=name, grid=(nt,),
        in_specs=[_tile(tm, D), _halo_prev(tm, D), _halo_next(tm, D, L), _full((8, D)),
                  _resident((None, 4, D, Ws), lambda i: (0, 0, 0, 0)),
                  _resident((None, D, D), lambda i: (0, 0, 0))],
        out_specs=[_tile(tm, D), _tile(tm, D), _tile(tm, 3 * D)],
        out_shape=[jax.ShapeDtypeStruct((L, D), F32), jax.ShapeDtypeStruct((L, D), BF16),
                   jax.ShapeDtypeStruct((L, 3 * D), BF16)],
        compiler_params=_params())(x, x, x, prm, wing, woutg)


def _conv_bwd(dout, x, y, bcx, prm, wing, woutg, name, tm=256, ride=None):
    L, D = x.shape
    Ws = wing.shape[-1]
    tm = min(tm, L)
    nt = L // tm
    te = tm + 2 * HALO

    def body(dc_ref, dp_ref, dn_ref, x_ref, y_ref, bc_ref, bp_ref, bn_ref, prm_ref, wi_ref, wo_ref,
             dx_ref, dpr_ref, h2_ref, dy_ref, hn_ref, acc_ref):
        i = pl.program_id(0)
        gain, shift, scale, gate = prm_ref[3:4, :], prm_ref[0:1, :], prm_ref[1:2, :], prm_ref[2:3, :]
        taps = [prm_ref[4 + k:5 + k, :] for k in range(3)]
        do = dc_ref[...]
        doe = _ext(dp_ref[...], do, dn_ref[...], i, nt)
        dye = (gate * doe).astype(BF16)
        dy_ref[...] = dye[HALO:HALO + tm]
        dh2 = _dot_nt(dye, wo_ref[...])
        be = jnp.concatenate([bp_ref[...], bc_ref[...], bn_ref[...]], axis=0).astype(F32)
        bg, cg, xin = be[:, :D], be[:, D:2 * D], be[:, 2 * D:]
        tpos = i * tm - HALO + lax.broadcasted_iota(jnp.int32, (te, 1), 0)
        valid = ((tpos >= 0) & (tpos < L)).astype(F32)
        z = cg * xin * valid
        yc = _conv3(z, taps)
        dyc = dh2 * bg
        h2_ref[...] = (bg * yc)[HALO:HALO + tm].astype(BF16)
        dz = _conv3(dyc, taps[::-1]) * valid
        dproj = jnp.concatenate([dh2 * yc, dz * xin, dz * cg], axis=1)[HALO:HALO + tm].astype(BF16)
        dpr_ref[...] = dproj
        dhn = jnp.zeros((tm, D), F32)
        for s in range(4):
            dhn = dhn + _dot_nt(dproj[:, s * Ws:(s + 1) * Ws], wi_ref[s])
        hn, xhat, r, nrm = _nm(x_ref[...], gain, shift, scale)
        hn_ref[...] = hn.astype(BF16)
        dx, dshift, dscale, dgn = _nm_bwd(dhn, xhat, r, nrm, gain, scale)
        dx_ref[...] = do + dx
        dgate = _sum0(do * y_ref[...].astype(F32))
        dtaps = [_sum0((dyc * _shift_rows(z, 1 - k))[HALO:HALO + tm]) for k in range(3)]
        _acc_rows(acc_ref, i == 0, [dshift, dscale, dgate, dgn] + dtaps)

    res, carried = _ride_call(
        body, ride, (dout, dout, dout, x, y, bcx, bcx, bcx, prm, wing, woutg), name=name, grid=(nt,),
        in_specs=[_tile(tm, D), _halo_prev(tm, D), _halo_next(tm, D, L), _tile(tm, D), _tile(tm, D),
                  _tile(tm, 3 * D), _halo_prev(tm, 3 * D), _halo_next(tm, 3 * D, L), _full((8, D)),
                  _resident((None, 4, D, Ws), lambda i: (0, 0, 0, 0)),
                  _resident((None, D, D), lambda i: (0, 0, 0))],
        out_specs=[_tile(tm, D), _tile(tm, 3 * D), _tile(tm, D), _tile(tm, D), _tile(tm, D), _full((8, D))],
        out_shape=[jax.ShapeDtypeStruct((L, D), F32), jax.ShapeDtypeStruct((L, 3 * D), BF16),
                   jax.ShapeDtypeStruct((L, D), BF16), jax.ShapeDtypeStruct((L, D), BF16),
                   jax.ShapeDtypeStruct((L, D), BF16), jax.ShapeDtypeStruct((8, D), F32)],
        compiler_params=_params())
    return (*res, carried)


def _mod_fwd(cond, mod_w, mod_b, name, tn=768):
    nl, D, N = mod_w.shape
    tn = min(tn, N)

    def body(c_ref, w_ref, b_ref, o_ref):
        cv = c_ref[...]
        s = (cv * _sigmoid(cv)).astype(BF16)
        o_ref[...] = _dot(s, w_ref[...].astype(BF16)) + b_ref[...]

    return pl.pallas_call(
        body, name=name, grid=(nl, N // tn),
        in_specs=[pl.BlockSpec((16, D), lambda l, j: (0, 0)), pl.BlockSpec((None, D, tn), lambda l, j: (l, 0, j)),
                  pl.BlockSpec((None, 1, tn), lambda l, j: (l, 0, j))],
        out_specs=pl.BlockSpec((None, 16, tn), lambda l, j: (l, 0, j)),
        out_shape=jax.ShapeDtypeStruct((nl, 16, N), F32), compiler_params=_params())(cond, mod_w, mod_b)


def _mod_bwd(cond, dm, mod_w, name, tn=768):
    nl, D, N = mod_w.shape
    tn = min(tn, N)

    def body(c_ref, d_ref, w_ref, dw_ref, dc_ref):
        first = (pl.program_id(0) == 0) & (pl.program_id(1) == 0)
        cv = c_ref[...]
        s = (cv * _sigmoid(cv)).astype(BF16)
        d = d_ref[...].astype(BF16)
        dw_ref[...] = _dot_tn(s, d)

        @pl.when(first)
        def _():
            dc_ref[...] = jnp.zeros((16, D), F32)

        dc_ref[...] += _dot_nt(d, w_ref[...].astype(BF16))

    return pl.pallas_call(
        body, name=name, grid=(nl, N // tn),
        in_specs=[pl.BlockSpec((16, D), lambda l, j: (0, 0)), pl.BlockSpec((None, 16, tn), lambda l, j: (l, 0, j)),
                  pl.BlockSpec((None, D, tn), lambda l, j: (l, 0, j))],
        out_specs=[pl.BlockSpec((None, D, tn), lambda l, j: (l, 0, j)), pl.BlockSpec((16, D), lambda l, j: (0, 0))],
        out_shape=[jax.ShapeDtypeStruct((nl, D, N), F32), jax.ShapeDtypeStruct((16, D), F32)],
        compiler_params=_params())(cond, dm, mod_w)


def _mod_small_grads(dm_all, cond, dsilu_parts, name):
    nl, _, N = dm_all.shape
    D = cond.shape[1]

    def body(d_ref, c_ref, p_ref, db_ref, dc_ref):
        for l in range(nl):
            db_ref[l] = _sum0(d_ref[l])
        tot = p_ref[0, 8:9, :]
        for k in range(1, N_CHIPS):
            tot = tot + p_ref[2 * k, 8:9, :]
        cv = c_ref[8:9, :]
        sg = _sigmoid(cv)
        dc_ref[...] = tot * (sg * (1.0 + cv * (1.0 - sg)))

    return pl.pallas_call(
        body, name=name, out_shape=[jax.ShapeDtypeStruct((nl, 1, N), F32), jax.ShapeDtypeStruct((1, D), F32)],
    )(dm_all, cond, dsilu_parts)


def _prm(rows, D):
    rows = [r.reshape(1, D) for r in rows]
    return jnp.concatenate(rows + [jnp.zeros((8 - len(rows), D), F32)], axis=0)


def kernel(x, c, ctx, c_ctx, mod_w, mod_b, norm_g, ffn_w13, ffn_w2, even_w_in, even_w_out, na_rpb, pool_w, pool_scale, conv_w_in, conv_w, conv_w_out, final_g, loss_target, m_c_ctx, m_mod_w, m_mod_b, m_norm_g, m_ffn_w13, m_ffn_w2, m_even_w_in, m_even_w_out, m_na_rpb, m_pool_w, m_pool_scale, m_conv_w_in, m_conv_w, m_conv_w_out, m_final_g, v_c_ctx, v_mod_w, v_mod_b, v_norm_g, v_ffn_w13, v_ffn_w2, v_even_w_in, v_even_w_out, v_na_rpb, v_pool_w, v_pool_scale, v_conv_w_in, v_conv_w, v_conv_w_out, v_final_g):
    xi, yi, ci = lax.axis_index("x"), lax.axis_index("y"), lax.axis_index("c")
    chip = 2 * xi + yi
    dev = 4 * xi + 2 * yi + ci
    _, L, D = x.shape
    C = ctx.shape[1]
    Ds = D // N_CHIPS
    Nm = mod_w.shape[-1]
    Fh = ffn_w13.shape[-1]
    Fq = ffn_w2.shape[2]
    assert ffn_w13.shape[:2] == (2, 2) and Fh == 2 * Fq and L % (GQ * GRID_W) == 0 and L // GRID_W >= GK and GQ == NA_KH // 2
    x0, ctx0, tgt = x[0], ctx[0], loss_target[0]

    pad = lambda a: jnp.pad(a, ((0, 0), (0, D - a.shape[1])))
    pack1 = jnp.concatenate([c, pad(norm_g.reshape(6, Ds)), pad(conv_w.reshape(3, Ds)), jnp.zeros((6, D), F32)], axis=0)
    g1 = _small_all_gather(pack1, "ag_cond")
    cond = jnp.concatenate([g1[:, 0], c_ctx[None], jnp.zeros((7, D), F32)], axis=0)
    norm_full = jnp.concatenate([g1[2 * k, 1:7, :Ds] for k in range(N_CHIPS)], axis=1).reshape(2, 3, D)
    convw_full = jnp.concatenate([g1[2 * k, 7:10, :Ds] for k in range(N_CHIPS)], axis=1)

    mod_b_loc = lax.dynamic_slice_in_dim(mod_b, chip * Nm, Nm, axis=1).reshape(2, 1, Nm)
    m_loc = _mod_fwd(cond, mod_w, mod_b_loc, "mod_fwd")
    g2 = _small_all_gather(m_loc.reshape(32, Nm), "ag_mod")
    m_all = jnp.concatenate([g2[2 * k] for k in range(N_CHIPS)], axis=1).reshape(2, 16, N_MOD, D)
    m_lat = lax.dynamic_index_in_dim(m_all, dev, axis=1, keepdims=False)
    m_ctx = m_all[:, 8]

    def prm(mods, layer, base, gain_idx, extra=()):
        return _prm([mods[layer, base], mods[layer, base + 1], mods[layer, base + 2], norm_full[layer, gain_idx],
                     *extra], D)

    def shard_bf16(w, name):
        return _cast_bf16(w.reshape(-1, w.shape[-1]), name).reshape(-1, *w.shape[-2:])

    w13s, w2s = shard_bf16(ffn_w13, "cast_w13"), shard_bf16(ffn_w2, "cast_w2")
    eins, eouts = shard_bf16(even_w_in, "cast_ein"), shard_bf16(even_w_out, "cast_eout")
    cins, couts = shard_bf16(conv_w_in, "cast_cin"), shard_bf16(conv_w_out, "cast_cout")
    ffn_shards = [[w13s[t:t + 1], w2s[t:t + 1]] for t in range(4)]

    def ffn_weights(w13g, w2g):
        return w13g.reshape(1, 4, D, Fh), w2g

    wf = [ffn_weights(*_gather_shards(ffn_shards[0], "ag_ffn0")), None, None, None]
    pos = jnp.stack([chip, ci]).astype(jnp.int32)

    p_f1 = prm(m_lat, 0, 0, 0)
    p_mx = prm(m_lat, 0, 3, 1)
    p_f2 = prm(m_lat, 0, 6, 2)
    p_g1 = prm(m_lat, 1, 0, 0)
    p_cv = prm(m_lat, 1, 3, 1, extra=(convw_full[0], convw_full[1], convw_full[2]))
    p_g2 = prm(m_lat, 1, 6, 2)
    pc_f1 = prm(m_ctx, 0, 0, 0)
    pc_mx = prm(m_ctx, 0, 3, 1)

    x1, ab1, y1, (eing, eoutg) = _ffn_fwd(x0, p_f1, *wf[0], 0, "ffn_fwd_l0a", ride=_broadcast_ride([eins, eouts]))
    eing = eing.reshape(1, 4, D, NA_WIDTH)
    ctx1, abc, yc, _ = _ffn_fwd(ctx0, pc_f1, *wf[0], 0, "ffn_fwd_ctx")
    q, k, v, u, hn_mx = _even_in_fwd(x1, p_mx, eing, "even_in_fwd")
    _, k_c, v_c, _, hn_cx = _even_in_fwd(ctx1, pc_mx, eing, "even_in_ctx")
    bias = _bias_table(na_rpb[0], "bias_table")
    att, gathered = _attn_fwd(q, k, v, k_c, v_c, bias, "attn_fwd", ride=_broadcast_ride(ffn_shards[1]))
    wf[1] = ffn_weights(*gathered)
    pw_b = _cast_bf16(pool_w.reshape(-1, POOL_GROUP_DIM), "cast_poolw").reshape(4, POOL_GROUP_DIM, POOL_GROUP_DIM)
    pool, dmx = _pool_fwd(u, pw_b, pool_scale, "pool_fwd")
    x2, ymx = _even_out_fwd(x1, att, pool, p_mx, eoutg, "even_out_fwd")
    x3, ab2, y2, gathered = _ffn_fwd(x2, p_f2, *wf[1], 0, "ffn_fwd_l0b",
                                     ride=_broadcast_ride(ffn_shards[2] + [cins, couts]))
    wf[2] = ffn_weights(*gathered[:2])
    cing, coutg = gathered[2].reshape(1, 4, D, conv_w_in.shape[-1]), gathered[3]
    x4, ab3, y3, gathered = _ffn_fwd(x3, p_g1, *wf[2], 0, "ffn_fwd_l1a", ride=_broadcast_ride(ffn_shards[3]))
    wf[3] = ffn_weights(*gathered)
    x5, ycv, bcx = _conv_fwd(x4, p_cv, cing, coutg, "conv_fwd")
    x6, ab4, y4, _ = _ffn_fwd(x5, p_g2, *wf[3], 0, "ffn_fwd_l1b")

    def ffn_back(dout, xin, ab, yy, p, t, tag, init13=None, init2=None, ride=None, head=None):
        sv, gact = ab
        dx, dab, dy, hn, acc, carried = _ffn_bwd(dout, xin, sv, yy, p, *wf[t], 0, f"ffn_bwd_{tag}", ride=ride,
                                                 head=head)
        dw13 = _mm_tn(hn, dab, 4, False, f"dw13_{tag}", init=init13)
        dw2 = _mm_tn(gact, dy, 2, True, f"dw2_{tag}", init=init2)
        return dx, acc, dw13, dw2, carried

    dx5, acc_g2, dw13_3, dw2_3, _ = ffn_back(x6, x5, ab4, y4, p_g2, 3, "l1b", head=(tgt, final_g.reshape(1, D)))
    acc_head = acc_g2[4:6]
    loss = lax.psum(acc_head[1, 0], ("x", "y", "c"))
    s_a, sb_a = _pair_sums([dw13_3, dw2_3], pos, "l1b")
    dx4, dproj, h2, dycv, hn_cv, acc_cv, got_a = _conv_bwd(dx5, x4, ycv, bcx, p_cv, cing, coutg, "conv_bwd",
                                                           ride=_scatter_ride(sb_a))
    dcin = _mm_tn(hn_cv, dproj, 4, False, "dw_cin")
    dcout = _mm_tn(h2, dycv, 1, False, "dw_cout")
    s_b, sb_b = _pair_sums([dcin, dcout], pos, "conv")
    dx3, acc_g1, dw13_2, dw2_2, got_b = ffn_back(dx4, x3, ab3, y3, p_g1, 2, "l1a", ride=_scatter_ride(sb_b))
    s_c, sb_c = _pair_sums([dw13_2, dw2_2], pos, "l1a")
    dx2, acc_f2, dw13_1, dw2_1, got_c = ffn_back(dx3, x2, ab2, y2, p_f2, 1, "l0b", ride=_scatter_ride(sb_c))

    dymx, datt, dpool, acc_mxo = _even_out_bwd(dx2, ymx, p_mx, eoutg, "even_out_bwd")
    deout = jnp.concatenate([_mm_tn(att, dymx, 1, False, "dw_eout_att"),
                             _mm_tn(pool, dymx, 1, False, "dw_eout_pool")], axis=0)
    s_d, sb_d = _pair_sums([dw13_1, dw2_1, deout], pos, "l0b")
    du, dpoolw, acc_pool = _pool_bwd(dpool, dmx, pw_b, pool_scale, "pool_bwd")
    dq, dk, dv, dkc, dvc, dbias, got_d = _attn_bwd(q, k, v, k_c, v_c, bias, datt, "attn_bwd",
                                                   ride=_scatter_ride(sb_d))
    drpb = _rpb_grad(dbias, "rpb_grad")
    dx1, dstack, acc_mxi = _even_in_bwd(dx2, x1, dq, dk, dv, du, p_mx, eing,
                                        "even_in_bwd")
    zc = jnp.zeros((C, NA_WIDTH), F32)
    dctx1, dstack_c, accc_mx = _even_in_bwd(jnp.zeros((C, D), F32), ctx1, zc, dkc, dvc, zc,
                                            pc_mx, eing, "even_in_bwd_ctx")
    dein_c = _mm_tn(hn_cx, dstack_c, 4, False, "dw_ein_ctx")
    dein = _mm_tn(hn_mx, dstack, 4, False, "dw_ein", init=dein_c)
    s_e, sb_e = _pair_sums([dein], pos, "ein")
    _, accc_f1, dw13_c, dw2_c, _ = ffn_back(dctx1, ctx0, abc, yc, pc_f1, 0, "ctx")
    sv1, gact1 = ab1
    dx0, dab, dy, hn, acc_f1, _ = _ffn_bwd(dx1, x0, sv1, y1, p_f1, *wf[0], 0, "ffn_bwd_l0a")
    dw13_0, got_e = _mm_tn(hn, dab, 4, False, "dw13_l0a", init=dw13_c, ride=_scatter_ride(sb_e))
    s_f13, sb_f13 = _pair_sums([dw13_0], pos, "l0a_w13")
    dw2_0, got_f13 = _mm_tn(gact1, dy, 2, True, "dw2_l0a", init=dw2_c, ride=_scatter_ride(sb_f13))
    s_f2, sb_f2 = _pair_sums([dw2_0], pos, "l0a_w2")

    z1 = jnp.zeros((1, D), F32)
    dm_lat = jnp.concatenate([acc_f1[0:3], acc_mxi[0:2], acc_mxo[2:3], acc_f2[0:3],
                              acc_g1[0:3], acc_cv[0:3], acc_g2[0:3]], axis=0)
    dm_ctx = jnp.concatenate([accc_f1[0:3], accc_mx[0:2]] + [z1] * 13, axis=0)
    dnorm = jnp.concatenate([acc_f1[3:4] + accc_f1[3:4], acc_mxi[3:4] + accc_mx[3:4], acc_f2[3:4],
                             acc_g1[3:4], acc_cv[3:4], acc_g2[3:4]], axis=0)
    rpb_flat = jnp.pad(drpb.reshape(-1), (0, 4 * D - drpb.size)).reshape(4, D)
    pack3 = jnp.concatenate([dm_lat, dm_ctx, dnorm, acc_cv[4:7], acc_head[0:1], pad(acc_pool[0:1]), z1,
                             dpoolw.reshape(-1, D), rpb_flat, jnp.zeros((4, D), F32)], axis=0)
    g3 = _small_all_gather(pack3, "ag_small")
    tot = _sum_devices(g3, "sum_small")
    dm_all = jnp.concatenate([g3[:, 0:18].reshape(8, 2, N_MOD * D).transpose(1, 0, 2),
                              tot[18:36].reshape(2, 1, N_MOD * D), jnp.zeros((2, 7, N_MOD * D), F32)], axis=1)
    dm_loc = lax.dynamic_slice_in_dim(dm_all, chip * Nm, Nm, axis=2)
    g_mod_w, dsilu = _mod_bwd(cond, dm_loc, mod_w, "mod_bwd")
    g4 = _small_all_gather(dsilu, "ag_dsilu")
    g_mod_b, g_c_ctx = _mod_small_grads(dm_all, cond, g4, "mod_small")
    g_mod_b = g_mod_b.reshape(2, N_MOD * D)
    g_c_ctx = g_c_ctx.reshape(D)
    g_norm_full = tot[36:42].reshape(2, 3, D)
    g_norm = lax.dynamic_slice_in_dim(g_norm_full, chip * Ds, Ds, axis=2)
    g_conv_w = lax.dynamic_slice_in_dim(tot[42:45], chip * Ds, Ds, axis=1).reshape(1, 3, Ds)
    g_final = tot[45]
    g_pscale = tot[46:47, :pool_scale.shape[1]]
    g_poolw = tot[48:112].reshape(pool_w.shape)
    g_rpb = tot[112:116].reshape(-1)[:na_rpb.size].reshape(na_rpb.shape)

    adamw_mod_w, got_f2 = _adamw(mod_w, g_mod_w, m_mod_w, v_mod_w, "adamw_mod_w", ride=_scatter_ride(sb_f2))
    r13 = _joins([s_f13[0], s_d[0], s_c[0], s_a[0]], [got_f13[0], got_d[0], got_c[0], got_a[0]], pos, "w13")
    r2 = _joins([s_f2[0], s_d[1], s_c[1], s_a[1]], [got_f2[0], got_d[1], got_c[1], got_a[1]], pos, "w2")
    r_eout, r_cout = _joins([s_d[2], s_b[1]], [got_d[2], got_b[1]], pos, "out")
    (r_ein,) = _joins(s_e, got_e, pos, "ein")
    (r_cin,) = _joins(s_b[:1], got_b[:1], pos, "cin")
    g_w13 = jnp.stack(r13).reshape(ffn_w13.shape)
    g_w2 = jnp.stack(r2).reshape(ffn_w2.shape)
    g_ein, g_eout, g_cin, g_cout = r_ein[None], r_eout[None], r_cin[None], r_cout[None]

    grads = [g_c_ctx, g_mod_w, g_mod_b, g_norm, g_w13, g_w2, g_ein, g_eout, g_rpb, g_poolw, g_pscale, g_cin,
             g_conv_w, g_cout, g_final]
    weights = [c_ctx, mod_w, mod_b, norm_g, ffn_w13, ffn_w2, even_w_in, even_w_out, na_rpb, pool_w, pool_scale,
               conv_w_in, conv_w, conv_w_out, final_g]
    ms = [m_c_ctx, m_mod_w, m_mod_b, m_norm_g, m_ffn_w13, m_ffn_w2, m_even_w_in, m_even_w_out, m_na_rpb, m_pool_w,
          m_pool_scale, m_conv_w_in, m_conv_w, m_conv_w_out, m_final_g]
    vs = [v_c_ctx, v_mod_w, v_mod_b, v_norm_g, v_ffn_w13, v_ffn_w2, v_even_w_in, v_even_w_out, v_na_rpb, v_pool_w,
          v_pool_scale, v_conv_w_in, v_conv_w, v_conv_w_out, v_final_g]
    names = ["c_ctx", "mod_w", "mod_b", "norm_g", "ffn_w13", "ffn_w2", "even_w_in", "even_w_out", "na_rpb", "pool_w",
             "pool_scale", "conv_w_in", "conv_w", "conv_w_out", "final_g"]
    deltas, new_m, new_v = [], [], []
    for n, w, g, m, vv in zip(names, weights, grads, ms, vs):
        g = g.reshape(w.shape)
        if n == "mod_w":
            d, mn, vn = adamw_mod_w
        elif w.ndim == 1:
            d, mn, vn = (t.reshape(w.shape) for t in _adamw(w[None], g[None], m[None], vv[None], f"adamw_{n}"))
        else:
            d, mn, vn = _adamw(w, g, m, vv, f"adamw_{n}")
        deltas.append(d)
        new_m.append(mn)
        new_v.append(vn)
    grads = [g.reshape(w.shape) for g, w in zip(grads, weights)]
    return (loss, dx0[None], *grads, *deltas, *new_m, *new_v)
```

```python
import jax
import jax.numpy as jnp
from jax import lax
from jax.experimental import pallas as pl
from jax.experimental.pallas import tpu as pltpu

F32 = jnp.float32
BF16 = jnp.bfloat16
MESH = pl.DeviceIdType.MESH

GRID_W = 64
NA_HEADS = 8
NA_HEAD_DIM = 64
NA_KH = 8
NA_KW = 16
GQ = 4
GK = GQ + NA_KH
NA_WIDTH = NA_HEADS * NA_HEAD_DIM
POOL_WINDOWS = (2, 4, 8, 16)
POOL_GROUP_DIM = 128
N_MOD = 9
RMS_EPS = 1e-6
NEG_INF = -1e30
ADAM_LR, ADAM_B1, ADAM_B2, ADAM_EPS, ADAM_WD, ADAM_STEP = 0.001, 0.9, 0.999, 1e-08, 0.01, 10

HALO = 16
VMEM_LIMIT = 56 * 1024 * 1024
N_CHIPS = 4
N_DEV = 8


def _dot(a, b):
    return jnp.dot(a, b, preferred_element_type=F32)


def _dot_nt(a, b):
    return lax.dot_general(a, b, (((1,), (1,)), ((), ())), preferred_element_type=F32)


def _dot_tn(a, b):
    return lax.dot_general(a, b, (((0,), (0,)), ((), ())), preferred_element_type=F32)


def _sigmoid(a):
    return 1.0 / (1.0 + jnp.exp(-a))


def _sum0(v):
    return jnp.sum(v, axis=0, keepdims=True)


def _nm(x, g, shift, scale):
    r = lax.rsqrt(jnp.mean(x * x, axis=-1, keepdims=True) + RMS_EPS)
    xhat = x * r
    nrm = xhat * g
    return nrm * (1.0 + scale) + shift, xhat, r, nrm


def _nm_bwd(dhn, xhat, r, nrm, g, scale):
    dshift = _sum0(dhn)
    dscale = _sum0(dhn * nrm)
    dnrm = dhn * (1.0 + scale)
    dgn = _sum0(dnrm * xhat)
    dxh = dnrm * g
    dx = r * (dxh - xhat * jnp.mean(dxh * xhat, axis=-1, keepdims=True))
    return dx, dshift, dscale, dgn


def _acc_rows(acc_ref, first, rows):
    @pl.when(first)
    def _():
        acc_ref[...] = jnp.zeros(acc_ref.shape, acc_ref.dtype)
    for k, row in enumerate(rows):
        if row is not None:
            acc_ref[k:k + 1, :] += row


def _shift_rows(v, k):
    n = v.shape[0]
    k = k % n
    return v if k == 0 else pltpu.roll(v, k, 0)


def _tile(tm, w):
    return pl.BlockSpec((tm, w), lambda i: (i, 0))


def _full(shape):
    nd = len(shape)
    return pl.BlockSpec(shape, lambda i: (0,) * nd)


def _resident(block, imap):
    return pl.BlockSpec(block, imap, pipeline_mode=pl.Buffered(1))


def _halo_prev(tm, w):
    return pl.BlockSpec((HALO, w), lambda i: (jnp.maximum(i * (tm // HALO) - 1, 0), 0))


def _halo_next(tm, w, L):
    return pl.BlockSpec((HALO, w), lambda i: (jnp.minimum((i + 1) * (tm // HALO), L // HALO - 1), 0))


def _params(vmem=VMEM_LIMIT):
    return pltpu.CompilerParams(vmem_limit_bytes=vmem)


def _pick_rows(rows, cols, itemsize=4, target=1 << 20):
    best = None
    for t in range(8, rows + 1, 8):
        if rows % t == 0 and t * cols * itemsize <= target:
            best = t
    return best if best is not None else rows


def _ext(prev, cur, nxt, i, nt):
    prev = jnp.where(i > 0, prev, jnp.zeros_like(prev))
    nxt = jnp.where(i < nt - 1, nxt, jnp.zeros_like(nxt))
    return jnp.concatenate([prev, cur, nxt], axis=0)


def _cast_bf16(a2d, name):
    rows, cols = a2d.shape
    tr = _pick_rows(rows, cols)

    def body(a_ref, o_ref):
        o_ref[...] = a_ref[...].astype(BF16)

    return pl.pallas_call(
        body, name=name, grid=(rows // tr,), in_specs=[_tile(tr, cols)], out_specs=_tile(tr, cols),
        out_shape=jax.ShapeDtypeStruct((rows, cols), BF16))(a2d)


def _sum_devices(g, name):
    n, rows, cols = g.shape
    tr = _pick_rows(rows, cols, target=1 << 18)

    def body(g_ref, o_ref):
        s = g_ref[0]
        for d in range(1, n):
            s = s + g_ref[d]
        o_ref[...] = s

    return pl.pallas_call(
        body, name=name, grid=(rows // tr,), in_specs=[pl.BlockSpec((n, tr, cols), lambda i: (0, i, 0))],
        out_specs=_tile(tr, cols), out_shape=jax.ShapeDtypeStruct((rows, cols), F32))(g)


def _adamw(w, g, m, v, name, ride=None):
    shape = w.shape
    cols = shape[-1]
    rows = w.size // cols
    w2, g2, m2, v2 = (t.reshape(rows, cols) for t in (w, g, m, v))
    tr = _pick_rows(rows, cols)
    c1 = 1.0 - ADAM_B1 ** ADAM_STEP
    c2 = 1.0 - ADAM_B2 ** ADAM_STEP

    def body(w_ref, g_ref, m_ref, v_ref, d_ref, mo_ref, vo_ref):
        gg = g_ref[...]
        mn = ADAM_B1 * m_ref[...] + (1.0 - ADAM_B1) * gg
        vn = ADAM_B2 * v_ref[...] + (1.0 - ADAM_B2) * (gg * gg)
        d_ref[...] = -ADAM_LR * ((mn / c1) / (jnp.sqrt(vn / c2) + ADAM_EPS) + ADAM_WD * w_ref[...])
        mo_ref[...] = mn
        vo_ref[...] = vn

    outs, carried = _ride_call(
        body, ride, (w2, g2, m2, v2), name=name, grid=(rows // tr,), in_specs=[_tile(tr, cols)] * 4,
        out_specs=[_tile(tr, cols)] * 3, out_shape=[jax.ShapeDtypeStruct((rows, cols), F32)] * 3)
    outs = tuple(o.reshape(shape) for o in outs)
    return outs if ride is None else (outs, carried)


def _mesh_pos():
    x, y, c = lax.axis_index("x"), lax.axis_index("y"), lax.axis_index("c")
    chips = [(1 - x, y), (x, 1 - y), (1 - x, 1 - y)]
    return x, y, c, chips


def _hbm_specs(n):
    return [pl.BlockSpec(memory_space=pltpu.HBM)] * n


def _small_all_gather(v, name):
    rows, w = v.shape

    def body(x_ref, out_ref, send_sems, recv_sems, local_sem):
        x, y, c, chips = _mesh_pos()
        me, sibling = (x, y, c), (x, y, 1 - c)

        def blk(px, py, pc):
            return out_ref.at[4 * px + 2 * py + pc]

        def copy(k, block, to, src=None):
            return pltpu.make_async_remote_copy(
                src_ref=blk(*block) if src is None else src, dst_ref=blk(*block),
                send_sem=send_sems.at[k], recv_sem=recv_sems.at[k], device_id=to, device_id_type=MESH)

        mine = pltpu.make_async_copy(x_ref, blk(*me), local_sem)
        mine.start()
        first = [copy(0, me, sibling, src=x_ref)]
        first += [copy(1 + j, me, (*chip, c), src=x_ref) for j, chip in enumerate(chips)]
        for cp in first:
            cp.start()
        passed = [copy(4 + j, (*chip, c), sibling) for j, chip in enumerate(chips)]
        for j, chip in enumerate(chips):
            copy(1 + j, (*chip, c), me).wait_recv()
            passed[j].start()
        copy(0, sibling, me).wait_recv()
        for j, chip in enumerate(chips):
            copy(4 + j, (*chip, 1 - c), me).wait_recv()
        for cp in first + passed:
            cp.wait_send()
        mine.wait()

    return pl.pallas_call(
        body, name=name, out_shape=jax.ShapeDtypeStruct((N_DEV, rows, w), v.dtype),
        in_specs=[pl.BlockSpec(memory_space=pltpu.VMEM)], out_specs=pl.BlockSpec(memory_space=pltpu.VMEM),
        scratch_shapes=[pltpu.SemaphoreType.DMA((7,)), pltpu.SemaphoreType.DMA((7,)), pltpu.SemaphoreType.DMA],
    )(v)


def _gather_shards(shards, name):
    n = len(shards)

    def body(*refs):
        ins, outs = refs[:n], refs[n:2 * n]
        send_sems, recv_sems, local_sems = refs[2 * n:]
        x, y, c, chips = _mesh_pos()
        k = 2 * x + y
        sibling = (x, y, 1 - c)

        def window(t, chip_k, half):
            r = ins[t].shape[1]
            return outs[t].at[:, pl.ds(chip_k * r + half * (r // 2), r // 2), :]

        def copy(t, j, chip_k, half, to, src=None):
            return pltpu.make_async_remote_copy(
                src_ref=window(t, chip_k, half) if src is None else src, dst_ref=window(t, chip_k, half),
                send_sem=send_sems.at[6 * t + j], recv_sem=recv_sems.at[6 * t + j], device_id=to, device_id_type=MESH)

        started, local = [], []
        for t in range(n):
            r = ins[t].shape[1]
            lc = pltpu.make_async_copy(ins[t], outs[t].at[:, pl.ds(k * r, r), :], local_sems.at[t])
            lc.start()
            local.append(lc)
            src = ins[t].at[:, pl.ds(c * (r // 2), r // 2), :]
            for j, chip in enumerate(chips):
                cp = copy(t, j, k, c, (*chip, c), src=src)
                cp.start()
                started.append(cp)
        for t in range(n):
            for j, chip in enumerate(chips):
                kj = 2 * chip[0] + chip[1]
                copy(t, j, kj, c, sibling).wait_recv()
                cp = copy(t, 3 + j, kj, c, sibling)
                cp.start()
                started.append(cp)
        for t in range(n):
            for j, chip in enumerate(chips):
                kj = 2 * chip[0] + chip[1]
                copy(t, 3 + j, kj, 1 - c, sibling).wait_recv()
        for cp in started:
            cp.wait_send()
        for lc in local:
            lc.wait()

    out_shape = [jax.ShapeDtypeStruct((s.shape[0], N_CHIPS * s.shape[1], s.shape[2]), s.dtype) for s in shards]
    return pl.pallas_call(
        body, name=name, out_shape=out_shape, in_specs=_hbm_specs(n), out_specs=_hbm_specs(n),
        scratch_shapes=[pltpu.SemaphoreType.DMA((6 * n,)), pltpu.SemaphoreType.DMA((6 * n,)),
                        pltpu.SemaphoreType.DMA((n,))],
    )(*shards)


def _chunk_rows(h, w, limit=2 << 20):
    best = 16
    for t in range(16, h + 1, 16):
        if h % t == 0 and t * w * 4 <= limit:
            best = t
    return best


def _pair_sum(part, pos, name):
    _, h, w = part.shape
    cr = _chunk_rows(h, w)
    nc = h // cr
    n = 4 * nc
    slots = 4

    def body(pos_ref, own_ref, send_ref, s_ref, sb_ref, stage, rbuf, send_sems, recv_sems):
        x, y, c, _ = _mesh_pos()
        k = pl.program_id(0)

        def copy(chunk):
            return pltpu.make_async_remote_copy(
                src_ref=stage.at[chunk % 2], dst_ref=rbuf.at[chunk % slots], send_sem=send_sems.at[chunk % 2],
                recv_sem=recv_sems.at[chunk % slots], device_id=(x, y, 1 - c), device_id_type=MESH)

        @pl.when(k >= 2)
        def _():
            copy(k - 2).wait_send()

        @pl.when(k < n)
        def _():
            stage[k % 2] = send_ref[...]
            copy(k).start()

        @pl.when(k > 0)
        def _():
            copy(k - 1).wait_recv()
            s = own_ref[...] + rbuf[(k - 1) % slots]
            s_ref[...] = s
            sb_ref[...] = s.astype(BF16)

        @pl.when(k == n)
        def _():
            copy(k - 1).wait_send()

    def own(k, p):
        j = jnp.maximum(k - 1, 0)
        return ((2 * (j // nc) + p[1]) * nc + j % nc, 0)

    def send(k, p):
        j = jnp.minimum(k, n - 1)
        return ((2 * (j // nc) + 1 - p[1]) * nc + j % nc, 0)

    grid_spec = pltpu.PrefetchScalarGridSpec(
        num_scalar_prefetch=1, grid=(n + 1,),
        in_specs=[pl.BlockSpec((cr, w), own), pl.BlockSpec((cr, w), send)],
        out_specs=[pl.BlockSpec((cr, w), lambda k, p: (jnp.maximum(k - 1, 0), 0))] * 2,
        scratch_shapes=[pltpu.VMEM((2, cr, w), F32), pltpu.VMEM((slots, cr, w), F32),
                        pltpu.SemaphoreType.DMA((2,)), pltpu.SemaphoreType.DMA((slots,))])
    part2 = part.reshape(8 * h, w)
    s, sb = pl.pallas_call(
        body, name=name, grid_spec=grid_spec, compiler_params=_params(),
        out_shape=[jax.ShapeDtypeStruct((4 * h, w), F32), jax.ShapeDtypeStruct((4 * h, w), BF16)],
    )(pos, part2, part2)
    return s.reshape(4, h, w), sb.reshape(4, h, w)


class _Ride:
    def __init__(self, ins, out_shape, sems, copies):
        self.ins, self.out_shape, self.sems, self.copies = list(ins), list(out_shape), list(sems), copies

    def start(self, ins, outs, sems):
        sends, _, _, local = self.copies(ins, outs, sems)
        for cp in local + sends:
            cp.start()

    def finish(self, ins, outs, sems):
        _, recvs, sends, local = self.copies(ins, outs, sems)
        for cp in recvs:
            cp.wait_recv()
        for cp in sends:
            cp.wait_send()
        for cp in local:
            cp.wait()


def _scatter_ride(sums_bf16):
    n = len(sums_bf16)

    def copies(ins, outs, sems):
        send_sems, recv_sems = sems
        x, y, c, chips = _mesh_pos()
        cps = [pltpu.make_async_remote_copy(
            src_ref=ins[t].at[2 * chip[0] + chip[1]], dst_ref=outs[t].at[j],
            send_sem=send_sems.at[3 * t + j], recv_sem=recv_sems.at[3 * t + j],
            device_id=(*chip, c), device_id_type=MESH) for t in range(n) for j, chip in enumerate(chips)]
        return cps, cps, cps, []

    return _Ride(sums_bf16, [jax.ShapeDtypeStruct((3,) + s.shape[1:], BF16) for s in sums_bf16],
                 [pltpu.SemaphoreType.DMA((3 * n,)), pltpu.SemaphoreType.DMA((3 * n,))], copies)


def _broadcast_ride(shards):
    n = len(shards)

    def copies(ins, outs, sems):
        send_sems, recv_sems, local_sems = sems
        x, y, c, chips = _mesh_pos()
        k = 2 * x + y
        sends, recvs, local = [], [], []
        for t in range(n):
            r = ins[t].shape[1]
            h = r // 2
            local.append(pltpu.make_async_copy(ins[t], outs[t].at[:, pl.ds(k * r, r), :], local_sems.at[t]))
            src = ins[t].at[:, pl.ds(c * h, h), :]
            mine = outs[t].at[:, pl.ds(k * r + c * h, h), :]
            for j, chip in enumerate(chips):
                kj = 2 * chip[0] + chip[1]
                for d in range(2):
                    sends.append(pltpu.make_async_remote_copy(
                        src_ref=src, dst_ref=mine, send_sem=send_sems.at[6 * t + 2 * j + d],
                        recv_sem=recv_sems.at[6 * t + 2 * j + c], device_id=(*chip, d), device_id_type=MESH))
                    theirs = outs[t].at[:, pl.ds(kj * r + d * h, h), :]
                    recvs.append(pltpu.make_async_remote_copy(
                        src_ref=theirs, dst_ref=theirs, send_sem=send_sems.at[6 * t + 2 * j + d],
                        recv_sem=recv_sems.at[6 * t + 2 * j + d], device_id=(*chip, d), device_id_type=MESH))
        return sends, recvs, sends, local

    return _Ride(shards, [jax.ShapeDtypeStruct((s.shape[0], N_CHIPS * s.shape[1], s.shape[2]), s.dtype) for s in shards],
                 [pltpu.SemaphoreType.DMA((6 * n,)), pltpu.SemaphoreType.DMA((6 * n,)), pltpu.SemaphoreType.DMA((n,))],
                 copies)


def _ride_call(body, ride, args, *, name, grid, in_specs, out_specs, out_shape, compiler_params=None):
    in_specs, out_specs, out_shape = list(in_specs), list(out_specs), list(out_shape)
    if ride is None:
        res = pl.pallas_call(body, name=name, grid=grid, in_specs=in_specs, out_specs=out_specs, out_shape=out_shape,
                             compiler_params=compiler_params)(*args)
        return list(res), []
    ni, no, ri, ro = len(in_specs), len(out_specs), len(ride.ins), len(ride.out_shape)

    def at_step(pick):
        hit = None
        for d, n in enumerate(grid):
            here = pl.program_id(d) == pick(n)
            hit = here if hit is None else hit & here
        return hit

    def carried(*refs):
        ins, rins = refs[:ni], refs[ni:ni + ri]
        outs, routs = refs[ni + ri:ni + ri + no], refs[ni + ri + no:ni + ri + no + ro]
        sems = refs[ni + ri + no + ro:]

        @pl.when(at_step(lambda n: 0))
        def _():
            ride.start(rins, routs, sems)

        body(*ins, *outs)

        @pl.when(at_step(lambda n: n - 1))
        def _():
            ride.finish(rins, routs, sems)

    res = pl.pallas_call(
        carried, name=name, grid=grid, in_specs=in_specs + _hbm_specs(ri), out_specs=out_specs + _hbm_specs(ro),
        out_shape=out_shape + ride.out_shape, scratch_shapes=ride.sems, compiler_params=compiler_params,
    )(*args, *ride.ins)
    return list(res[:no]), list(res[no:])


def _sum_and_join(sums, got, pos, name):
    n = len(sums)
    _, h, w = sums[0].shape
    cr = _chunk_rows(h, w, limit=(4 << 20) // n)

    def body(pos_ref, *refs):
        mine, theirs, outs = refs[:n], refs[n:2 * n], refs[2 * n:3 * n]
        ebuf, rbuf, send_sems, recv_sems = refs[3 * n:]
        x, y, c, _ = _mesh_pos()
        slot = pl.program_id(0) % 2
        cps = []
        for t in range(n):
            e = mine[t][...]
            for j in range(3):
                e = e + theirs[t][j].astype(F32)
            ebuf[t, slot] = e
            cp = pltpu.make_async_remote_copy(
                src_ref=ebuf.at[t, slot], dst_ref=rbuf.at[t, slot], send_sem=send_sems.at[2 * t + slot],
                recv_sem=recv_sems.at[2 * t + slot], device_id=(x, y, 1 - c), device_id_type=MESH)
            cp.start()
            outs[t][pos_ref[1]] = e
            cps.append(cp)
        for t, cp in enumerate(cps):
            cp.wait_recv()
            outs[t][1 - pos_ref[1]] = rbuf[t, slot]
        for cp in cps:
            cp.wait_send()

    grid_spec = pltpu.PrefetchScalarGridSpec(
        num_scalar_prefetch=1, grid=(h // cr,),
        in_specs=[pl.BlockSpec((None, cr, w), lambda i, p: (p[0], i, 0))] * n +
                 [pl.BlockSpec((3, cr, w), lambda i, p: (0, i, 0))] * n,
        out_specs=[pl.BlockSpec((2, cr, w), lambda i, p: (0, i, 0))] * n,
        scratch_shapes=[pltpu.VMEM((n, 2, cr, w), F32), pltpu.VMEM((n, 2, cr, w), F32),
                        pltpu.SemaphoreType.DMA((2 * n,)), pltpu.SemaphoreType.DMA((2 * n,))])
    return pl.pallas_call(
        body, name=name, grid_spec=grid_spec, compiler_params=_params(),
        out_shape=[jax.ShapeDtypeStruct((2, h, w), F32)] * n,
    )(pos, *sums, *got)


def _pair_sums(parts, pos, tag):
    pairs = [_pair_sum(p.reshape(8, p.shape[0] // 8, p.shape[1]), pos, f"rs_pair_{tag}_{t}")
             for t, p in enumerate(parts)]
    return [s for s, _ in pairs], [sb for _, sb in pairs]


def _joins(sums, got, pos, tag):
    full = _sum_and_join(sums, got, pos, f"rs_join_{tag}")
    return [f.reshape(2 * f.shape[1], f.shape[2]) for f in full]


def _ffn_fwd(x, prm, w13g, w2g, t, name, tm=512, ride=None):
    L, D = x.shape
    Fh = w13g.shape[-1]
    tm = min(tm, L)

    def body(x_ref, p_ref, w13_ref, w2_ref, xo_ref, sv_ref, g_ref, y_ref):
        xv = x_ref[...]
        hn, _, _, _ = _nm(xv, p_ref[3:4, :], p_ref[0:1, :], p_ref[1:2, :])
        hb = hn.astype(BF16)
        acc = jnp.zeros((tm, D), F32)
        for j in range(2):
            a = _dot(hb, w13_ref[j])
            b = _dot(hb, w13_ref[2 + j])
            sg = _sigmoid(a)
            sa = a * sg
            sv_ref[:, j * Fh:(j + 1) * Fh] = sa.astype(BF16)
            sv_ref[:, (2 + j) * Fh:(3 + j) * Fh] = (b * (sg * (1.0 + a * (1.0 - sg)))).astype(BF16)
            g = (sa * b).astype(BF16)
            g_ref[:, j * Fh:(j + 1) * Fh] = g
            acc = acc + _dot(g, w2_ref[j * Fh:(j + 1) * Fh, :])
        y_ref[...] = acc.astype(BF16)
        xo_ref[...] = xv + (0.5 * p_ref[2:3, :]) * acc

    res, carried = _ride_call(
        body, ride, (x, prm, w13g, w2g), name=name, grid=(L // tm,),
        in_specs=[_tile(tm, D), _full((8, D)),
                  _resident((None, 4, D, Fh), lambda i: (t, 0, 0, 0)),
                  _resident((None, 2 * Fh, D), lambda i: (t, 0, 0))],
        out_specs=[_tile(tm, D), _tile(tm, 4 * Fh), _tile(tm, 2 * Fh), _tile(tm, D)],
        out_shape=[jax.ShapeDtypeStruct((L, D), F32), jax.ShapeDtypeStruct((L, 4 * Fh), BF16),
                   jax.ShapeDtypeStruct((L, 2 * Fh), BF16), jax.ShapeDtypeStruct((L, D), BF16)],
        compiler_params=_params())
    xo, sv, g, y = res
    return xo, (sv, g), y, carried


def _head_grad(xo, tgt, fg):
    D = xo.shape[-1]
    r = lax.rsqrt(jnp.mean(xo * xo, axis=-1, keepdims=True) + RMS_EPS)
    xhat = xo * r
    err = xhat * fg - tgt
    loss = 0.5 * jnp.sum(jnp.mean(err * err, axis=-1, keepdims=True), axis=0, keepdims=True)
    dy = err * (1.0 / D)
    dxh = dy * fg
    return r * (dxh - xhat * jnp.mean(dxh * xhat, axis=-1, keepdims=True)), _sum0(dy * xhat), loss


def _ffn_bwd(dout, x, sv, y, prm, w13g, w2g, t, name, tm=256, ride=None, head=None):
    L, D = x.shape
    Fh = w13g.shape[-1]
    tm = min(tm, L)

    def body(do_ref, x_ref, sv_ref, y_ref, p_ref, w13_ref, w2_ref, *rest):
        dx_ref, dab_ref, dy_ref, hn_ref, acc_ref = rest[-5:]
        i = pl.program_id(0)
        head_rows = []
        if head is None:
            do = do_ref[...]
        else:
            do, dfg, loss = _head_grad(do_ref[...], rest[0][...], rest[1][...])
            head_rows = [dfg, jnp.broadcast_to(loss, (1, D))]
        gain, shift, scale, gate = p_ref[3:4, :], p_ref[0:1, :], p_ref[1:2, :], p_ref[2:3, :]
        hn, xhat, r, nrm = _nm(x_ref[...], gain, shift, scale)
        hn_ref[...] = hn.astype(BF16)
        dgate = 0.5 * _sum0(do * y_ref[...].astype(F32))
        dyb = ((0.5 * gate) * do).astype(BF16)
        dy_ref[...] = dyb
        dhn = jnp.zeros((tm, D), F32)
        for j in range(2):
            dg = _dot_nt(dyb, w2_ref[j * Fh:(j + 1) * Fh, :])
            da = (dg * sv_ref[:, (2 + j) * Fh:(3 + j) * Fh].astype(F32)).astype(BF16)
            db = (dg * sv_ref[:, j * Fh:(j + 1) * Fh].astype(F32)).astype(BF16)
            dab_ref[:, j * Fh:(j + 1) * Fh] = da
            dab_ref[:, (2 + j) * Fh:(3 + j) * Fh] = db
            dhn = dhn + _dot_nt(da, w13_ref[j]) + _dot_nt(db, w13_ref[2 + j])
        dx, dshift, dscale, dgn = _nm_bwd(dhn, xhat, r, nrm, gain, scale)
        dx_ref[...] = do + dx
        _acc_rows(acc_ref, i == 0, [dshift, dscale, dgate, dgn] + head_rows)

    head_args = () if head is None else head
    head_specs = [] if head is None else [_tile(tm, D), _full((1, D))]
    res, carried = _ride_call(
        body, ride, (dout, x, sv, y, prm, w13g, w2g, *head_args), name=name, grid=(L // tm,),
        in_specs=[_tile(tm, D), _tile(tm, D), _tile(tm, 4 * Fh), _tile(tm, D), _full((8, D)),
                  _resident((None, 4, D, Fh), lambda i: (t, 0, 0, 0)),
                  _resident((None, 2 * Fh, D), lambda i: (t, 0, 0))] + head_specs,
        out_specs=[_tile(tm, D), _tile(tm, 4 * Fh), _tile(tm, D), _tile(tm, D), _full((8, D))],
        out_shape=[jax.ShapeDtypeStruct((L, D), F32), jax.ShapeDtypeStruct((L, 4 * Fh), BF16),
                   jax.ShapeDtypeStruct((L, D), BF16), jax.ShapeDtypeStruct((L, D), BF16),
                   jax.ShapeDtypeStruct((8, D), F32)],
        compiler_params=_params())
    return (*res, carried)


def _mm_tn(a, b, slabs, a_slabbed, name, init=None, tl=1024, ride=None):
    L = a.shape[0]
    ka = a.shape[1] // slabs if a_slabbed else a.shape[1]
    nb = b.shape[1] if a_slabbed else b.shape[1] // slabs
    tl = min(tl, L)
    has_init = init is not None

    def body(a_ref, b_ref, *rest):
        o_ref = rest[-1]
        step = pl.program_id(1)

        @pl.when(step == 0)
        def _():
            o_ref[...] = rest[0][...] if has_init else jnp.zeros((ka, nb), F32)

        o_ref[...] += _dot_tn(a_ref[...], b_ref[...])

    in_specs = [pl.BlockSpec((tl, ka), (lambda s, l: (l, s)) if a_slabbed else (lambda s, l: (l, 0))),
                pl.BlockSpec((tl, nb), (lambda s, l: (l, 0)) if a_slabbed else (lambda s, l: (l, s)))]
    args = [a, b]
    if has_init:
        in_specs.append(pl.BlockSpec((ka, nb), lambda s, l: (s, 0)))
        args.append(init)
    res, carried = _ride_call(
        body, ride, args, name=name, grid=(slabs, L // tl), in_specs=in_specs,
        out_specs=[pl.BlockSpec((ka, nb), lambda s, l: (s, 0))],
        out_shape=[jax.ShapeDtypeStruct((slabs * ka, nb), F32)], compiler_params=_params())
    return res[0] if ride is None else (res[0], carried)


def _even_in_fwd(x, prm, wing, name, tm=512):
    L, D = x.shape
    W = wing.shape[-1]
    tm = min(tm, L)

    def body(x_ref, p_ref, w_ref, q_ref, k_ref, v_ref, u_ref, hn_ref):
        hn, _, _, _ = _nm(x_ref[...], p_ref[3:4, :], p_ref[0:1, :], p_ref[1:2, :])
        hb = hn.astype(BF16)
        hn_ref[...] = hb
        q_ref[...] = _dot(hb, w_ref[0]).astype(BF16)
        k_ref[...] = _dot(hb, w_ref[1]).astype(BF16)
        v_ref[...] = _dot(hb, w_ref[2]).astype(BF16)
        u_ref[...] = _dot(hb, w_ref[3])

    return pl.pallas_call(
        body, name=name, grid=(L // tm,),
        in_specs=[_tile(tm, D), _full((8, D)), _resident((None, 4, D, W), lambda i: (0, 0, 0, 0))],
        out_specs=[_tile(tm, W)] * 4 + [_tile(tm, D)],
        out_shape=[jax.ShapeDtypeStruct((L, W), BF16)] * 3 + [jax.ShapeDtypeStruct((L, W), F32),
                                                              jax.ShapeDtypeStruct((L, D), BF16)],
        compiler_params=_params())(x, prm, wing)


def _even_in_bwd(dout, x, dq, dk, dv, du, prm, wing, name, tm=512):
    L, D = x.shape
    W = wing.shape[-1]
    tm = min(tm, L)

    def body(do_ref, x_ref, dq_ref, dk_ref, dv_ref, du_ref, p_ref, w_ref, dx_ref, ds_ref, acc_ref):
        i = pl.program_id(0)
        gain, shift, scale = p_ref[3:4, :], p_ref[0:1, :], p_ref[1:2, :]
        _, xhat, r, nrm = _nm(x_ref[...], gain, shift, scale)
        dhn = jnp.zeros((tm, D), F32)
        for s, ref in enumerate((dq_ref, dk_ref, dv_ref, du_ref)):
            d = ref[...].astype(BF16)
            ds_ref[:, s * W:(s + 1) * W] = d
            dhn = dhn + _dot_nt(d, w_ref[s])
        dx, dshift, dscale, dgn = _nm_bwd(dhn, xhat, r, nrm, gain, scale)
        dx_ref[...] = do_ref[...] + dx
        _acc_rows(acc_ref, i == 0, [dshift, dscale, None, dgn])

    return pl.pallas_call(
        body, name=name, grid=(L // tm,),
        in_specs=[_tile(tm, D), _tile(tm, D)] + [_tile(tm, W)] * 4 +
                 [_full((8, D)), _resident((None, 4, D, W), lambda i: (0, 0, 0, 0))],
        out_specs=[_tile(tm, D), _tile(tm, 4 * W), _full((8, D))],
        out_shape=[jax.ShapeDtypeStruct((L, D), F32), jax.ShapeDtypeStruct((L, 4 * W), BF16),
                   jax.ShapeDtypeStruct((8, D), F32)],
        compiler_params=_params())(dout, x, dq, dk, dv, du, prm, wing)


def _even_out_fwd(x, att, pool, prm, woutg, name, tm=512):
    L, D = x.shape
    W = D // 2
    tm = min(tm, L)

    def body(x_ref, a_ref, p_ref, prm_ref, w_ref, xo_ref, y_ref):
        yv = _dot(a_ref[...], w_ref[0:W, :]) + _dot(p_ref[...], w_ref[W:2 * W, :])
        y_ref[...] = yv.astype(BF16)
        xo_ref[...] = x_ref[...] + prm_ref[2:3, :] * yv

    return pl.pallas_call(
        body, name=name, grid=(L // tm,),
        in_specs=[_tile(tm, D), _tile(tm, W), _tile(tm, W), _full((8, D)),
                  _resident((None, D, D), lambda i: (0, 0, 0))],
        out_specs=[_tile(tm, D), _tile(tm, D)],
        out_shape=[jax.ShapeDtypeStruct((L, D), F32), jax.ShapeDtypeStruct((L, D), BF16)],
        compiler_params=_params())(x, att, pool, prm, woutg)


def _even_out_bwd(dout, y, prm, woutg, name, tm=512):
    L, D = dout.shape
    W = D // 2
    tm = min(tm, L)

    def body(do_ref, y_ref, p_ref, w_ref, dy_ref, da_ref, dp_ref, acc_ref):
        i = pl.program_id(0)
        do = do_ref[...]
        dgate = _sum0(do * y_ref[...].astype(F32))
        dyb = (p_ref[2:3, :] * do).astype(BF16)
        dy_ref[...] = dyb
        da_ref[...] = _dot_nt(dyb, w_ref[0:W, :]).astype(BF16)
        dp_ref[...] = _dot_nt(dyb, w_ref[W:2 * W, :])
        _acc_rows(acc_ref, i == 0, [None, None, dgate])

    return pl.pallas_call(
        body, name=name, grid=(L // tm,),
        in_specs=[_tile(tm, D), _tile(tm, D), _full((8, D)), _resident((None, D, D), lambda i: (0, 0, 0))],
        out_specs=[_tile(tm, D), _tile(tm, W), _tile(tm, W), _full((8, D))],
        out_shape=[jax.ShapeDtypeStruct((L, D), BF16), jax.ShapeDtypeStruct((L, W), BF16),
                   jax.ShapeDtypeStruct((L, W), F32), jax.ShapeDtypeStruct((8, D), F32)],
        compiler_params=_params())(dout, y, prm, woutg)


def _group_ri(variant, qr, kr):
    first_key = (0, qr, GK - NA_KH)[variant]
    if not first_key <= kr < first_key + NA_KH:
        return None
    return kr - qr + (NA_KH - 1, NA_KH - 1 - NA_KH // 2, NA_KH - 1 - (GK - GQ))[variant]


HEADS_PER_BLOCK = 128 // NA_HEAD_DIM


def _bias_table(rpb, name):
    H = rpb.shape[0]
    hpb = HEADS_PER_BLOCK
    nri, nci = 2 * NA_KH - 1, 2 * NA_KW - 1
    col = jnp.arange(GRID_W)
    rel = (col[None, :] - col[:, None] + (NA_KW - 1)).reshape(1, -1)
    onehot = (rel == jnp.arange(32)[:, None]).astype(F32)
    cs = jnp.clip(col - NA_KW // 2, 0, GRID_W - NA_KW)
    ok = ((col[None, :] >= cs[:, None]) & (col[None, :] < cs[:, None] + NA_KW)).astype(F32).reshape(1, -1)
    by_lane_block = rpb.reshape(H // hpb, hpb, nri, nci).transpose(1, 0, 2, 3)
    rpb2 = jnp.pad(by_lane_block.reshape(H * nri, nci), ((0, 0), (0, 32 - nci)))

    def body(r_ref, e_ref, m_ref, o_ref):
        t = jnp.dot(r_ref[...], e_ref[...], preferred_element_type=F32, precision=lax.Precision.HIGHEST)
        o_ref[...] = jnp.where(m_ref[...] > 0.0, t, NEG_INF)

    tab = pl.pallas_call(body, name=name, out_shape=jax.ShapeDtypeStruct((H * nri, GRID_W * GRID_W), F32))(
        rpb2, onehot, ok)
    tab = tab.reshape(hpb, H // hpb, nri, GRID_W, GRID_W)
    outside = jnp.full((H // hpb, GRID_W, GRID_W), NEG_INF, F32)
    variants = []
    for variant in range(3):
        rows = []
        for h in range(hpb):
            for qr in range(GQ):
                ris = [_group_ri(variant, qr, kr) for kr in range(GK)]
                rows.append(jnp.concatenate([outside if ri is None else tab[h, :, ri] for ri in ris], axis=2))
        variants.append(jnp.concatenate(rows, axis=1))
    return jnp.stack(variants, axis=1)


def _attn_probs(q, kw, kc, bias, scale):
    s_w = _dot_nt(q, kw) * scale + bias
    s_c = _dot_nt(q, kc) * scale
    m = jnp.maximum(jnp.max(s_w, axis=-1, keepdims=True), jnp.max(s_c, axis=-1, keepdims=True))
    e_w = jnp.exp(s_w - m)
    e_c = jnp.exp(s_c - m)
    inv = 1.0 / (jnp.sum(e_w, axis=-1, keepdims=True) + jnp.sum(e_c, axis=-1, keepdims=True))
    return e_w * inv, e_c * inv


def _group_place(g, R):
    G = R // GQ
    kb = jnp.clip(g * GQ - NA_KH // 2, 0, R - GK)
    variant = jnp.where(g == 0, 0, jnp.where(g == G - 1, 2, 1))
    return pl.multiple_of(g * (GQ * GRID_W), GQ * GRID_W), pl.multiple_of(kb * GRID_W, GRID_W), variant


def _lane_masks(width, dh):
    lane = lax.broadcasted_iota(jnp.int32, (1, width), 1)
    return [(lane >= h * dh) & (lane < (h + 1) * dh) for h in range(width // dh)]


def _only(mask, a):
    return jnp.where(mask, a, jnp.zeros_like(a))


def _attn_fwd(q, k, v, kc, vc, bias, name, ride=None):
    L, width = q.shape
    C = kc.shape[0]
    dh = NA_HEAD_DIM
    lanes = 128
    hpb = lanes // dh
    R = L // GRID_W
    nq, nk = GQ * GRID_W, GK * GRID_W
    scale = dh ** -0.5

    def body(q_ref, k_ref, v_ref, kc_ref, vc_ref, b_ref, o_ref):
        masks = _lane_masks(lanes, dh)
        kc2 = kc_ref[...]
        vcs = [_only(m, vc_ref[...]) for m in masks]

        def group(g, carry):
            q0, k0, variant = _group_place(g, R)
            q2 = q_ref[pl.ds(q0, nq), :]
            k2 = k_ref[pl.ds(k0, nk), :]
            v2 = v_ref[pl.ds(k0, nk), :]
            qs = jnp.concatenate([_only(m, q2) for m in masks], axis=0)
            p_w, p_c = _attn_probs(qs, k2, kc2, b_ref[variant], scale)
            p_w, p_c = p_w.astype(BF16), p_c.astype(BF16)
            o2 = jnp.zeros((nq, lanes), F32)
            for h, m in enumerate(masks):
                rows = slice(h * nq, (h + 1) * nq)
                o2 = o2 + _dot(p_w[rows], _only(m, v2)) + _dot(p_c[rows], vcs[h])
            o_ref[pl.ds(q0, nq), :] = o2.astype(BF16)
            return carry

        lax.fori_loop(0, R // GQ, group, 0)

    cols = lambda n: pl.BlockSpec((n, lanes), lambda p: (0, p))
    res, carried = _ride_call(
        body, ride, (q, k, v, kc, vc, bias), name=name, grid=(width // lanes,),
        in_specs=[cols(L), cols(L), cols(L), cols(C), cols(C),
                  pl.BlockSpec((None, 3, hpb * nq, nk), lambda p: (p, 0, 0, 0))],
        out_specs=[cols(L)], out_shape=[jax.ShapeDtypeStruct((L, width), BF16)],
        compiler_params=_params())
    return res[0], carried


def _attn_bwd(q, k, v, kc, vc, bias, do, name, ride=None):
    L, width = q.shape
    C = kc.shape[0]
    dh = NA_HEAD_DIM
    lanes = 128
    hpb = lanes // dh
    R = L // GRID_W
    nq, nk = GQ * GRID_W, GK * GRID_W
    scale = dh ** -0.5

    def body(q_ref, k_ref, v_ref, kc_ref, vc_ref, b_ref, do_ref, dq_ref, dk_ref, dv_ref, dkc_ref, dvc_ref, db_ref):
        masks = _lane_masks(lanes, dh)
        kc2 = kc_ref[...]
        vc2 = vc_ref[...]
        kcs = [_only(m, kc2) for m in masks]
        dk_ref[...] = jnp.zeros((L, lanes), F32)
        dv_ref[...] = jnp.zeros((L, lanes), F32)
        dkc_ref[...] = jnp.zeros((C, lanes), F32)
        dvc_ref[...] = jnp.zeros((C, lanes), F32)
        db_ref[...] = jnp.zeros((3, hpb * nq, nk), F32)

        def group(g, carry):
            q0, k0, variant = _group_place(g, R)
            q2 = q_ref[pl.ds(q0, nq), :]
            k2 = k_ref[pl.ds(k0, nk), :]
            v2 = v_ref[pl.ds(k0, nk), :]
            do2 = do_ref[pl.ds(q0, nq), :]
            qs = jnp.concatenate([_only(m, q2) for m in masks], axis=0)
            dos = jnp.concatenate([_only(m, do2) for m in masks], axis=0)
            p_w, p_c = _attn_probs(qs, k2, kc2, b_ref[variant], scale)
            dp_w = _dot_nt(dos, v2)
            dp_c = _dot_nt(dos, vc2)
            delta = jnp.sum(p_w * dp_w, axis=-1, keepdims=True) + jnp.sum(p_c * dp_c, axis=-1, keepdims=True)
            ds_w = p_w * (dp_w - delta)
            ds_c = p_c * (dp_c - delta)
            db_ref[variant] += ds_w
            dsw = (ds_w * scale).astype(BF16)
            dsc = (ds_c * scale).astype(BF16)
            dq2 = jnp.zeros((nq, lanes), F32)
            for h, m in enumerate(masks):
                rows = slice(h * nq, (h + 1) * nq)
                dq2 = dq2 + _dot(dsw[rows], _only(m, k2)) + _dot(dsc[rows], kcs[h])
            dq_ref[pl.ds(q0, nq), :] = dq2.astype(BF16)
            dk_ref[pl.ds(k0, nk), :] += _dot_tn(dsw, qs)
            dv_ref[pl.ds(k0, nk), :] += _dot_tn(p_w.astype(BF16), dos)
            dkc_ref[...] += _dot_tn(dsc, qs)
            dvc_ref[...] += _dot_tn(p_c.astype(BF16), dos)
            return carry

        lax.fori_loop(0, R // GQ, group, 0)

    cols = lambda n: _resident((n, lanes), lambda p: (0, p))
    bspec = _resident((None, 3, hpb * nq, nk), lambda p: (p, 0, 0, 0))
    res, carried = _ride_call(
        body, ride, (q, k, v, kc, vc, bias, do), name=name, grid=(width // lanes,),
        in_specs=[cols(L), cols(L), cols(L), cols(C), cols(C), bspec, cols(L)],
        out_specs=[cols(L), cols(L), cols(L), cols(C), cols(C), bspec],
        out_shape=[jax.ShapeDtypeStruct((L, width), BF16)] + [jax.ShapeDtypeStruct((L, width), F32)] * 2 +
                  [jax.ShapeDtypeStruct((C, width), F32)] * 2 +
                  [jax.ShapeDtypeStruct((width // lanes, 3, hpb * nq, nk), F32)],
        compiler_params=_params())
    return (*res, carried)


def _rpb_grad(dbias, name):
    hpb = HEADS_PER_BLOCK
    H = dbias.shape[0] * hpb
    nri, nci = 2 * NA_KH - 1, 2 * NA_KW - 1
    d6 = dbias.reshape(H // hpb, 3, hpb, GQ, GRID_W, GK, GRID_W).transpose(0, 2, 1, 3, 5, 4, 6)
    d6 = d6.reshape(H, 3, GQ, GK, GRID_W, GRID_W)
    col = jnp.arange(GRID_W)
    onehot = (col[None, None, :] - col[None, :, None] + (NA_KW - 1) == jnp.arange(32)[:, None, None]).astype(F32)
    places = [(v, qr, kr) for v in range(3) for qr in range(GQ) for kr in range(GK)]

    def body(d_ref, m_ref, o_ref, t_ref):
        t_ref[...] = jnp.zeros((32, GRID_W), F32)
        o_ref[...] = jnp.zeros((16, 32, 128), F32)
        for ri in range(nri):
            a = None
            for place in places:
                if _group_ri(*place) == ri:
                    blk = d_ref[place]
                    a = blk if a is None else a + blk
            for ci in range(nci):
                t_ref[ci:ci + 1, :] = _sum0(a * m_ref[ci])
            o_ref[ri] = jnp.broadcast_to(jnp.sum(t_ref[...], axis=1, keepdims=True), (32, 128))

    out = pl.pallas_call(
        body, name=name, grid=(H,),
        in_specs=[pl.BlockSpec((None, 3, GQ, GK, GRID_W, GRID_W), lambda h: (h, 0, 0, 0, 0, 0)),
                  pl.BlockSpec((32, GRID_W, GRID_W), lambda h: (0, 0, 0))],
        out_specs=pl.BlockSpec((None, 16, 32, 128), lambda h: (h, 0, 0, 0)),
        out_shape=jax.ShapeDtypeStruct((H, 16, 32, 128), F32),
        scratch_shapes=[pltpu.VMEM((32, GRID_W), F32)])(d6, onehot)
    return out[:, :nri, :nci, 0]


def _window_count(t, w, L):
    lo = jnp.clip(t - w // 2, 0, L)
    hi = jnp.clip(t - w // 2 + w, 0, L)
    return jnp.maximum(hi - lo, 1).astype(F32)


def _running_sum(v, w):
    k = 1
    while k < w:
        v = v + _shift_rows(v, k)
        k *= 2
    return v


def _pool_fwd(u, poolw, pscale, name, tm=512):
    L, W = u.shape
    G = POOL_GROUP_DIM
    tm = min(tm, L)
    nt = L // tm

    def body(c_ref, p_ref, n_ref, w_ref, s_ref, o_ref, dm_ref):
        i = pl.program_id(0)
        ext = _ext(p_ref[...], c_ref[...], n_ref[...], i, nt)
        t = i * tm + lax.broadcasted_iota(jnp.int32, (tm, 1), 0)
        for g, w in enumerate(POOL_WINDOWS):
            e = ext[:, g * G:(g + 1) * G]
            win = _shift_rows(_running_sum(e, w), -(w // 2 - 1))[HALO:HALO + tm]
            dmx = (win / _window_count(t, w, L) - e[HALO:HALO + tm]).astype(BF16)
            dm_ref[:, g * G:(g + 1) * G] = dmx
            o_ref[:, g * G:(g + 1) * G] = (_dot(dmx, w_ref[g]) * s_ref[:, g * G:(g + 1) * G]).astype(BF16)

    return pl.pallas_call(
        body, name=name, grid=(nt,),
        in_specs=[_tile(tm, W), _halo_prev(tm, W), _halo_next(tm, W, L), _full((4, G, G)), _full((1, W))],
        out_specs=[_tile(tm, W), _tile(tm, W)],
        out_shape=[jax.ShapeDtypeStruct((L, W), BF16)] * 2, compiler_params=_params())(u, u, u, poolw, pscale)


def _pool_bwd(dpool, dmx, poolw, pscale, name, tm=512):
    L, W = dpool.shape
    G = POOL_GROUP_DIM
    tm = min(tm, L)
    nt = L // tm

    def body(c_ref, p_ref, n_ref, dm_ref, w_ref, s_ref, du_ref, dw_ref, acc_ref):
        i = pl.program_id(0)
        ext = _ext(p_ref[...], c_ref[...], n_ref[...], i, nt)
        te = i * tm - HALO + lax.broadcasted_iota(jnp.int32, (tm + 2 * HALO, 1), 0)

        @pl.when(i == 0)
        def _():
            dw_ref[...] = jnp.zeros((4 * G, G), F32)

        rows = []
        for g, w in enumerate(POOL_WINDOWS):
            sc = s_ref[:, g * G:(g + 1) * G]
            dpre = (ext[:, g * G:(g + 1) * G] * sc).astype(BF16)
            dd = _dot_nt(dpre, w_ref[g])
            spread = _shift_rows(_running_sum(dd / _window_count(te, w, L), w), -(w // 2))
            du_ref[:, g * G:(g + 1) * G] = (spread - dd)[HALO:HALO + tm]
            dmx_g = dm_ref[:, g * G:(g + 1) * G]
            rows.append(_sum0(c_ref[:, g * G:(g + 1) * G] * _dot(dmx_g, w_ref[g])))
            dw_ref[g * G:(g + 1) * G, :] += _dot_tn(dmx_g, dpre[HALO:HALO + tm])
        _acc_rows(acc_ref, i == 0, [jnp.concatenate(rows, axis=1)])

    return pl.pallas_call(
        body, name=name, grid=(nt,),
        in_specs=[_tile(tm, W), _halo_prev(tm, W), _halo_next(tm, W, L), _tile(tm, W), _full((4, G, G)),
                  _full((1, W))],
        out_specs=[_tile(tm, W), _full((4 * G, G)), _full((8, W))],
        out_shape=[jax.ShapeDtypeStruct((L, W), F32), jax.ShapeDtypeStruct((4 * G, G), F32),
                   jax.ShapeDtypeStruct((8, W), F32)],
        compiler_params=_params())(dpool, dpool, dpool, dmx, poolw, pscale)


def _conv3(z, cw):
    return _shift_rows(z, 1) * cw[0] + z * cw[1] + _shift_rows(z, -1) * cw[2]


def _conv_fwd(x, prm, wing, woutg, name, tm=512):
    L, D = x.shape
    Ws = wing.shape[-1]
    tm = min(tm, L)
    nt = L // tm
    te = tm + 2 * HALO

    def body(c_ref, p_ref, n_ref, prm_ref, wi_ref, wo_ref, xo_ref, y_ref, b_ref):
        i = pl.program_id(0)
        xe = jnp.concatenate([p_ref[...], c_ref[...], n_ref[...]], axis=0)
        hn, _, _, _ = _nm(xe, prm_ref[3:4, :], prm_ref[0:1, :], prm_ref[1:2, :])
        hb = hn.astype(BF16)
        proj = jnp.concatenate([_dot(hb, wi_ref[s]) for s in range(4)], axis=1)
        bg, cg, xin = proj[:, :D], proj[:, D:2 * D], proj[:, 2 * D:]
        tpos = i * tm - HALO + lax.broadcasted_iota(jnp.int32, (te, 1), 0)
        valid = ((tpos >= 0) & (tpos < L)).astype(F32)
        yc = _conv3(cg * xin * valid, [prm_ref[4 + k:5 + k, :] for k in range(3)])
        h2 = (bg * yc)[HALO:HALO + tm].astype(BF16)
        yv = _dot(h2, wo_ref[...])
        y_ref[...] = yv.astype(BF16)
        xo_ref[...] = c_ref[...] + prm_ref[2:3, :] * yv
        b_ref[...] = proj[HALO:HALO + tm].astype(BF16)

    return pl.pallas_call(
        body, name=name, grid=(nt,),
        in_specs=[_tile(tm, D), _halo_prev(tm, D), _halo_next(tm, D, L), _full((8, D)),
                  _resident((None, 4, D, Ws), lambda i: (0, 0, 0, 0)),
                  _resident((None, D, D), lambda i: (0, 0, 0))],
        out_specs=[_tile(tm, D), _tile(tm, D), _tile(tm, 3 * D)],
        out_shape=[jax.ShapeDtypeStruct((L, D), F32), jax.ShapeDtypeStruct((L, D), BF16),
                   jax.ShapeDtypeStruct((L, 3 * D), BF16)],
        compiler_params=_params())(x, x, x, prm, wing, woutg)


def _conv_bwd(dout, x, y, bcx, prm, wing, woutg, name, tm=256, ride=None):
    L, D = x.shape
    Ws = wing.shape[-1]
    tm = min(tm, L)
    nt = L // tm
    te = tm + 2 * HALO

    def body(dc_ref, dp_ref, dn_ref, x_ref, y_ref, bc_ref, bp_ref, bn_ref, prm_ref, wi_ref, wo_ref,
             dx_ref, dpr_ref, h2_ref, dy_ref, hn_ref, acc_ref):
        i = pl.program_id(0)
        gain, shift, scale, gate = prm_ref[3:4, :], prm_ref[0:1, :], prm_ref[1:2, :], prm_ref[2:3, :]
        taps = [prm_ref[4 + k:5 + k, :] for k in range(3)]
        do = dc_ref[...]
        doe = _ext(dp_ref[...], do, dn_ref[...], i, nt)
        dye = (gate * doe).astype(BF16)
        dy_ref[...] = dye[HALO:HALO + tm]
        dh2 = _dot_nt(dye, wo_ref[...])
        be = jnp.concatenate([bp_ref[...], bc_ref[...], bn_ref[...]], axis=0).astype(F32)
        bg, cg, xin = be[:, :D], be[:, D:2 * D], be[:, 2 * D:]
        tpos = i * tm - HALO + lax.broadcasted_iota(jnp.int32, (te, 1), 0)
        valid = ((tpos >= 0) & (tpos < L)).astype(F32)
        z = cg * xin * valid
        yc = _conv3(z, taps)
        dyc = dh2 * bg
        h2_ref[...] = (bg * yc)[HALO:HALO + tm].astype(BF16)
        dz = _conv3(dyc, taps[::-1]) * valid
        dproj = jnp.concatenate([dh2 * yc, dz * xin, dz * cg], axis=1)[HALO:HALO + tm].astype(BF16)
        dpr_ref[...] = dproj
        dhn = jnp.zeros((tm, D), F32)
        for s in range(4):
            dhn = dhn + _dot_nt(dproj[:, s * Ws:(s + 1) * Ws], wi_ref[s])
        hn, xhat, r, nrm = _nm(x_ref[...], gain, shift, scale)
        hn_ref[...] = hn.astype(BF16)
        dx, dshift, dscale, dgn = _nm_bwd(dhn, xhat, r, nrm, gain, scale)
        dx_ref[...] = do + dx
        dgate = _sum0(do * y_ref[...].astype(F32))
        dtaps = [_sum0((dyc * _shift_rows(z, 1 - k))[HALO:HALO + tm]) for k in range(3)]
        _acc_rows(acc_ref, i == 0, [dshift, dscale, dgate, dgn] + dtaps)

    res, carried = _ride_call(
        body, ride, (dout, dout, dout, x, y, bcx, bcx, bcx, prm, wing, woutg), name=name, grid=(nt,),
        in_specs=[_tile(tm, D), _halo_prev(tm, D), _halo_next(tm, D, L), _tile(tm, D), _tile(tm, D),
                  _tile(tm, 3 * D), _halo_prev(tm, 3 * D), _halo_next(tm, 3 * D, L), _full((8, D)),
                  _resident((None, 4, D, Ws), lambda i: (0, 0, 0, 0)),
                  _resident((None, D, D), lambda i: (0, 0, 0))],
        out_specs=[_tile(tm, D), _tile(tm, 3 * D), _tile(tm, D), _tile(tm, D), _tile(tm, D), _full((8, D))],
        out_shape=[jax.ShapeDtypeStruct((L, D), F32), jax.ShapeDtypeStruct((L, 3 * D), BF16),
                   jax.ShapeDtypeStruct((L, D), BF16), jax.ShapeDtypeStruct((L, D), BF16),
                   jax.ShapeDtypeStruct((L, D), BF16), jax.ShapeDtypeStruct((8, D), F32)],
        compiler_params=_params())
    return (*res, carried)


def _mod_fwd(cond, mod_w, mod_b, name, tn=768):
    nl, D, N = mod_w.shape
    tn = min(tn, N)

    def body(c_ref, w_ref, b_ref, o_ref):
        cv = c_ref[...]
        s = (cv * _sigmoid(cv)).astype(BF16)
        o_ref[...] = _dot(s, w_ref[...].astype(BF16)) + b_ref[...]

    return pl.pallas_call(
        body, name=name, grid=(nl, N // tn),
        in_specs=[pl.BlockSpec((16, D), lambda l, j: (0, 0)), pl.BlockSpec((None, D, tn), lambda l, j: (l, 0, j)),
                  pl.BlockSpec((None, 1, tn), lambda l, j: (l, 0, j))],
        out_specs=pl.BlockSpec((None, 16, tn), lambda l, j: (l, 0, j)),
        out_shape=jax.ShapeDtypeStruct((nl, 16, N), F32), compiler_params=_params())(cond, mod_w, mod_b)


def _mod_bwd(cond, dm, mod_w, name, tn=768):
    nl, D, N = mod_w.shape
    tn = min(tn, N)

    def body(c_ref, d_ref, w_ref, dw_ref, dc_ref):
        first = (pl.program_id(0) == 0) & (pl.program_id(1) == 0)
        cv = c_ref[...]
        s = (cv * _sigmoid(cv)).astype(BF16)
        d = d_ref[...].astype(BF16)
        dw_ref[...] = _dot_tn(s, d)

        @pl.when(first)
        def _():
            dc_ref[...] = jnp.zeros((16, D), F32)

        dc_ref[...] += _dot_nt(d, w_ref[...].astype(BF16))

    return pl.pallas_call(
        body, name=name, grid=(nl, N // tn),
        in_specs=[pl.BlockSpec((16, D), lambda l, j: (0, 0)), pl.BlockSpec((None, 16, tn), lambda l, j: (l, 0, j)),
                  pl.BlockSpec((None, D, tn), lambda l, j: (l, 0, j))],
        out_specs=[pl.BlockSpec((None, D, tn), lambda l, j: (l, 0, j)), pl.BlockSpec((16, D), lambda l, j: (0, 0))],
        out_shape=[jax.ShapeDtypeStruct((nl, D, N), F32), jax.ShapeDtypeStruct((16, D), F32)],
        compiler_params=_params())(cond, dm, mod_w)


def _mod_small_grads(dm_all, cond, dsilu_parts, name):
    nl, _, N = dm_all.shape
    D = cond.shape[1]

    def body(d_ref, c_ref, p_ref, db_ref, dc_ref):
        for l in range(nl):
            db_ref[l] = _sum0(d_ref[l])
        tot = p_ref[0, 8:9, :]
        for k in range(1, N_CHIPS):
            tot = tot + p_ref[2 * k, 8:9, :]
        cv = c_ref[8:9, :]
        sg = _sigmoid(cv)
        dc_ref[...] = tot * (sg * (1.0 + cv * (1.0 - sg)))

    return pl.pallas_call(
        body, name=name, out_shape=[jax.ShapeDtypeStruct((nl, 1, N), F32), jax.ShapeDtypeStruct((1, D), F32)],
    )(dm_all, cond, dsilu_parts)


def _prm(rows, D):
    rows = [r.reshape(1, D) for r in rows]
    return jnp.concatenate(rows + [jnp.zeros((8 - len(rows), D), F32)], axis=0)


def kernel(x, c, ctx, c_ctx, mod_w, mod_b, norm_g, ffn_w13, ffn_w2, even_w_in, even_w_out, na_rpb, pool_w, pool_scale, conv_w_in, conv_w, conv_w_out, final_g, loss_target, m_c_ctx, m_mod_w, m_mod_b, m_norm_g, m_ffn_w13, m_ffn_w2, m_even_w_in, m_even_w_out, m_na_rpb, m_pool_w, m_pool_scale, m_conv_w_in, m_conv_w, m_conv_w_out, m_final_g, v_c_ctx, v_mod_w, v_mod_b, v_norm_g, v_ffn_w13, v_ffn_w2, v_even_w_in, v_even_w_out, v_na_rpb, v_pool_w, v_pool_scale, v_conv_w_in, v_conv_w, v_conv_w_out, v_final_g):
    xi, yi, ci = lax.axis_index("x"), lax.axis_index("y"), lax.axis_index("c")
    chip = 2 * xi + yi
    dev = 4 * xi + 2 * yi + ci
    _, L, D = x.shape
    C = ctx.shape[1]
    Ds = D // N_CHIPS
    Nm = mod_w.shape[-1]
    Fh = ffn_w13.shape[-1]
    Fq = ffn_w2.shape[2]
    assert ffn_w13.shape[:2] == (2, 2) and Fh == 2 * Fq and L % (GQ * GRID_W) == 0 and L // GRID_W >= GK and GQ == NA_KH // 2
    x0, ctx0, tgt = x[0], ctx[0], loss_target[0]

    pad = lambda a: jnp.pad(a, ((0, 0), (0, D - a.shape[1])))
    pack1 = jnp.concatenate([c, pad(norm_g.reshape(6, Ds)), pad(conv_w.reshape(3, Ds)), jnp.zeros((6, D), F32)], axis=0)
    g1 = _small_all_gather(pack1, "ag_cond")
    cond = jnp.concatenate([g1[:, 0], c_ctx[None], jnp.zeros((7, D), F32)], axis=0)
    norm_full = jnp.concatenate([g1[2 * k, 1:7, :Ds] for k in range(N_CHIPS)], axis=1).reshape(2, 3, D)
    convw_full = jnp.concatenate([g1[2 * k, 7:10, :Ds] for k in range(N_CHIPS)], axis=1)

    mod_b_loc = lax.dynamic_slice_in_dim(mod_b, chip * Nm, Nm, axis=1).reshape(2, 1, Nm)
    m_loc = _mod_fwd(cond, mod_w, mod_b_loc, "mod_fwd")
    g2 = _small_all_gather(m_loc.reshape(32, Nm), "ag_mod")
    m_all = jnp.concatenate([g2[2 * k] for k in range(N_CHIPS)], axis=1).reshape(2, 16, N_MOD, D)
    m_lat = lax.dynamic_index_in_dim(m_all, dev, axis=1, keepdims=False)
    m_ctx = m_all[:, 8]

    def prm(mods, layer, base, gain_idx, extra=()):
        return _prm([mods[layer, base], mods[layer, base + 1], mods[layer, base + 2], norm_full[layer, gain_idx],
                     *extra], D)

    def shard_bf16(w, name):
        return _cast_bf16(w.reshape(-1, w.shape[-1]), name).reshape(-1, *w.shape[-2:])

    w13s, w2s = shard_bf16(ffn_w13, "cast_w13"), shard_bf16(ffn_w2, "cast_w2")
    eins, eouts = shard_bf16(even_w_in, "cast_ein"), shard_bf16(even_w_out, "cast_eout")
    cins, couts = shard_bf16(conv_w_in, "cast_cin"), shard_bf16(conv_w_out, "cast_cout")
    ffn_shards = [[w13s[t:t + 1], w2s[t:t + 1]] for t in range(4)]

    def ffn_weights(w13g, w2g):
        return w13g.reshape(1, 4, D, Fh), w2g

    wf = [ffn_weights(*_gather_shards(ffn_shards[0], "ag_ffn0")), None, None, None]
    pos = jnp.stack([chip, ci]).astype(jnp.int32)

    p_f1 = prm(m_lat, 0, 0, 0)
    p_mx = prm(m_lat, 0, 3, 1)
    p_f2 = prm(m_lat, 0, 6, 2)
    p_g1 = prm(m_lat, 1, 0, 0)
    p_cv = prm(m_lat, 1, 3, 1, extra=(convw_full[0], convw_full[1], convw_full[2]))
    p_g2 = prm(m_lat, 1, 6, 2)
    pc_f1 = prm(m_ctx, 0, 0, 0)
    pc_mx = prm(m_ctx, 0, 3, 1)

    x1, ab1, y1, (eing, eoutg) = _ffn_fwd(x0, p_f1, *wf[0], 0, "ffn_fwd_l0a", ride=_broadcast_ride([eins, eouts]))
    eing = eing.reshape(1, 4, D, NA_WIDTH)
    ctx1, abc, yc, _ = _ffn_fwd(ctx0, pc_f1, *wf[0], 0, "ffn_fwd_ctx")
    q, k, v, u, hn_mx = _even_in_fwd(x1, p_mx, eing, "even_in_fwd")
    _, k_c, v_c, _, hn_cx = _even_in_fwd(ctx1, pc_mx, eing, "even_in_ctx")
    bias = _bias_table(na_rpb[0], "bias_table")
    att, gathered = _attn_fwd(q, k, v, k_c, v_c, bias, "attn_fwd", ride=_broadcast_ride(ffn_shards[1]))
    wf[1] = ffn_weights(*gathered)
    pw_b = _cast_bf16(pool_w.reshape(-1, POOL_GROUP_DIM), "cast_poolw").reshape(4, POOL_GROUP_DIM, POOL_GROUP_DIM)
    pool, dmx = _pool_fwd(u, pw_b, pool_scale, "pool_fwd")
    x2, ymx = _even_out_fwd(x1, att, pool, p_mx, eoutg, "even_out_fwd")
    x3, ab2, y2, gathered = _ffn_fwd(x2, p_f2, *wf[1], 0, "ffn_fwd_l0b",
                                     ride=_broadcast_ride(ffn_shards[2] + [cins, couts]))
    wf[2] = ffn_weights(*gathered[:2])
    cing, coutg = gathered[2].reshape(1, 4, D, conv_w_in.shape[-1]), gathered[3]
    x4, ab3, y3, gathered = _ffn_fwd(x3, p_g1, *wf[2], 0, "ffn_fwd_l1a", ride=_broadcast_ride(ffn_shards[3]))
    wf[3] = ffn_weights(*gathered)
    x5, ycv, bcx = _conv_fwd(x4, p_cv, cing, coutg, "conv_fwd")
    x6, ab4, y4, _ = _ffn_fwd(x5, p_g2, *wf[3], 0, "ffn_fwd_l1b")

    def ffn_back(dout, xin, ab, yy, p, t, tag, init13=None, init2=None, ride=None, head=None):
        sv, gact = ab
        dx, dab, dy, hn, acc, carried = _ffn_bwd(dout, xin, sv, yy, p, *wf[t], 0, f"ffn_bwd_{tag}", ride=ride,
                                                 head=head)
        dw13 = _mm_tn(hn, dab, 4, False, f"dw13_{tag}", init=init13)
        dw2 = _mm_tn(gact, dy, 2, True, f"dw2_{tag}", init=init2)
        return dx, acc, dw13, dw2, carried

    dx5, acc_g2, dw13_3, dw2_3, _ = ffn_back(x6, x5, ab4, y4, p_g2, 3, "l1b", head=(tgt, final_g.reshape(1, D)))
    acc_head = acc_g2[4:6]
    loss = lax.psum(acc_head[1, 0], ("x", "y", "c"))
    s_a, sb_a = _pair_sums([dw13_3, dw2_3], pos, "l1b")
    dx4, dproj, h2, dycv, hn_cv, acc_cv, got_a = _conv_bwd(dx5, x4, ycv, bcx, p_cv, cing, coutg, "conv_bwd",
                                                           ride=_scatter_ride(sb_a))
    dcin = _mm_tn(hn_cv, dproj, 4, False, "dw_cin")
    dcout = _mm_tn(h2, dycv, 1, False, "dw_cout")
    s_b, sb_b = _pair_sums([dcin, dcout], pos, "conv")
    dx3, acc_g1, dw13_2, dw2_2, got_b = ffn_back(dx4, x3, ab3, y3, p_g1, 2, "l1a", ride=_scatter_ride(sb_b))
    s_c, sb_c = _pair_sums([dw13_2, dw2_2], pos, "l1a")
    dx2, acc_f2, dw13_1, dw2_1, got_c = ffn_back(dx3, x2, ab2, y2, p_f2, 1, "l0b", ride=_scatter_ride(sb_c))

    dymx, datt, dpool, acc_mxo = _even_out_bwd(dx2, ymx, p_mx, eoutg, "even_out_bwd")
    deout = jnp.concatenate([_mm_tn(att, dymx, 1, False, "dw_eout_att"),
                             _mm_tn(pool, dymx, 1, False, "dw_eout_pool")], axis=0)
    s_d, sb_d = _pair_sums([dw13_1, dw2_1, deout], pos, "l0b")
    du, dpoolw, acc_pool = _pool_bwd(dpool, dmx, pw_b, pool_scale, "pool_bwd")
    dq, dk, dv, dkc, dvc, dbias, got_d = _attn_bwd(q, k, v, k_c, v_c, bias, datt, "attn_bwd",
                                                   ride=_scatter_ride(sb_d))
    drpb = _rpb_grad(dbias, "rpb_grad")
    dx1, dstack, acc_mxi = _even_in_bwd(dx2, x1, dq, dk, dv, du, p_mx, eing,
                                        "even_in_bwd")
    zc = jnp.zeros((C, NA_WIDTH), F32)
    dctx1, dstack_c, accc_mx = _even_in_bwd(jnp.zeros((C, D), F32), ctx1, zc, dkc, dvc, zc,
                                            pc_mx, eing, "even_in_bwd_ctx")
    dein_c = _mm_tn(hn_cx, dstack_c, 4, False, "dw_ein_ctx")
    dein = _mm_tn(hn_mx, dstack, 4, False, "dw_ein", init=dein_c)
    s_e, sb_e = _pair_sums([dein], pos, "ein")
    _, accc_f1, dw13_c, dw2_c, _ = ffn_back(dctx1, ctx0, abc, yc, pc_f1, 0, "ctx")
    sv1, gact1 = ab1
    dx0, dab, dy, hn, acc_f1, _ = _ffn_bwd(dx1, x0, sv1, y1, p_f1, *wf[0], 0, "ffn_bwd_l0a")
    dw13_0, got_e = _mm_tn(hn, dab, 4, False, "dw13_l0a", init=dw13_c, ride=_scatter_ride(sb_e))
    s_f13, sb_f13 = _pair_sums([dw13_0], pos, "l0a_w13")
    dw2_0, got_f13 = _mm_tn(gact1, dy, 2, True, "dw2_l0a", init=dw2_c, ride=_scatter_ride(sb_f13))
    s_f2, sb_f2 = _pair_sums([dw2_0], pos, "l0a_w2")

    z1 = jnp.zeros((1, D), F32)
    dm_lat = jnp.concatenate([acc_f1[0:3], acc_mxi[0:2], acc_mxo[2:3], acc_f2[0:3],
                              acc_g1[0:3], acc_cv[0:3], acc_g2[0:3]], axis=0)
    dm_ctx = jnp.concatenate([accc_f1[0:3], accc_mx[0:2]] + [z1] * 13, axis=0)
    dnorm = jnp.concatenate([acc_f1[3:4] + accc_f1[3:4], acc_mxi[3:4] + accc_mx[3:4], acc_f2[3:4],
                             acc_g1[3:4], acc_cv[3:4], acc_g2[3:4]], axis=0)
    rpb_flat = jnp.pad(drpb.reshape(-1), (0, 4 * D - drpb.size)).reshape(4, D)
    pack3 = jnp.concatenate([dm_lat, dm_ctx, dnorm, acc_cv[4:7], acc_head[0:1], pad(acc_pool[0:1]), z1,
                             dpoolw.reshape(-1, D), rpb_flat, jnp.zeros((4, D), F32)], axis=0)
    g3 = _small_all_gather(pack3, "ag_small")
    tot = _sum_devices(g3, "sum_small")
    dm_all = jnp.concatenate([g3[:, 0:18].reshape(8, 2, N_MOD * D).transpose(1, 0, 2),
                              tot[18:36].reshape(2, 1, N_MOD * D), jnp.zeros((2, 7, N_MOD * D), F32)], axis=1)
    dm_loc = lax.dynamic_slice_in_dim(dm_all, chip * Nm, Nm, axis=2)
    g_mod_w, dsilu = _mod_bwd(cond, dm_loc, mod_w, "mod_bwd")
    g4 = _small_all_gather(dsilu, "ag_dsilu")
    g_mod_b, g_c_ctx = _mod_small_grads(dm_all, cond, g4, "mod_small")
    g_mod_b = g_mod_b.reshape(2, N_MOD * D)
    g_c_ctx = g_c_ctx.reshape(D)
    g_norm_full = tot[36:42].reshape(2, 3, D)
    g_norm = lax.dynamic_slice_in_dim(g_norm_full, chip * Ds, Ds, axis=2)
    g_conv_w = lax.dynamic_slice_in_dim(tot[42:45], chip * Ds, Ds, axis=1).reshape(1, 3, Ds)
    g_final = tot[45]
    g_pscale = tot[46:47, :pool_scale.shape[1]]
    g_poolw = tot[48:112].reshape(pool_w.shape)
    g_rpb = tot[112:116].reshape(-1)[:na_rpb.size].reshape(na_rpb.shape)

    adamw_mod_w, got_f2 = _adamw(mod_w, g_mod_w, m_mod_w, v_mod_w, "adamw_mod_w", ride=_scatter_ride(sb_f2))
    r13 = _joins([s_f13[0], s_d[0], s_c[0], s_a[0]], [got_f13[0], got_d[0], got_c[0], got_a[0]], pos, "w13")
    r2 = _joins([s_f2[0], s_d[1], s_c[1], s_a[1]], [got_f2[0], got_d[1], got_c[1], got_a[1]], pos, "w2")
    r_eout, r_cout = _joins([s_d[2], s_b[1]], [got_d[2], got_b[1]], pos, "out")
    (r_ein,) = _joins(s_e, got_e, pos, "ein")
    (r_cin,) = _joins(s_b[:1], got_b[:1], pos, "cin")
    g_w13 = jnp.stack(r13).reshape(ffn_w13.shape)
    g_w2 = jnp.stack(r2).reshape(ffn_w2.shape)
    g_ein, g_eout, g_cin, g_cout = r_ein[None], r_eout[None], r_cin[None], r_cout[None]

    grads = [g_c_ctx, g_mod_w, g_mod_b, g_norm, g_w13, g_w2, g_ein, g_eout, g_rpb, g_poolw, g_pscale, g_cin,
             g_conv_w, g_cout, g_final]
    weights = [c_ctx, mod_w, mod_b, norm_g, ffn_w13, ffn_w2, even_w_in, even_w_out, na_rpb, pool_w, pool_scale,
               conv_w_in, conv_w, conv_w_out, final_g]
    ms = [m_c_ctx, m_mod_w, m_mod_b, m_norm_g, m_ffn_w13, m_ffn_w2, m_even_w_in, m_even_w_out, m_na_rpb, m_pool_w,
          m_pool_scale, m_conv_w_in, m_conv_w, m_conv_w_out, m_final_g]
    vs = [v_c_ctx, v_mod_w, v_mod_b, v_norm_g, v_ffn_w13, v_ffn_w2, v_even_w_in, v_even_w_out, v_na_rpb, v_pool_w,
          v_pool_scale, v_conv_w_in, v_conv_w, v_conv_w_out, v_final_g]
    names = ["c_ctx", "mod_w", "mod_b", "norm_g", "ffn_w13", "ffn_w2", "even_w_in", "even_w_out", "na_rpb", "pool_w",
             "pool_scale", "conv_w_in", "conv_w", "conv_w_out", "final_g"]
    deltas, new_m, new_v = [], [], []
    for n, w, g, m, vv in zip(names, weights, grads, ms, vs):
        g = g.reshape(w.shape)
        if n == "mod_w":
            d, mn, vn = adamw_mod_w
        elif w.ndim == 1:
            d, mn, vn = (t.reshape(w.shape) for t in _adamw(w[None], g[None], m[None], vv[None], f"adamw_{n}"))
        else:
            d, mn, vn = _adamw(w, g, m, vv, f"adamw_{n}")
        deltas.append(d)
        new_m.append(mn)
        new_v.append(vn)
    grads = [g.reshape(w.shape) for g, w in zip(grads, weights)]
    return (loss, dx0[None], *grads, *deltas, *new_m, *new_v)
```

```python
import jax
import jax.numpy as jnp
from jax import lax
from jax.experimental import pallas as pl
from jax.experimental.pallas import tpu as pltpu

F32 = jnp.float32
BF16 = jnp.bfloat16
MESH = pl.DeviceIdType.MESH

GRID_W = 64
NA_HEADS = 8
NA_HEAD_DIM = 64
NA_KH = 8
NA_KW = 16
GQ = 4
GK = GQ + NA_KH
NA_WIDTH = NA_HEADS * NA_HEAD_DIM
POOL_WINDOWS = (2, 4, 8, 16)
POOL_GROUP_DIM = 128
N_MOD = 9
RMS_EPS = 1e-6
NEG_INF = -1e30
ADAM_LR, ADAM_B1, ADAM_B2, ADAM_EPS, ADAM_WD, ADAM_STEP = 0.001, 0.9, 0.999, 1e-08, 0.01, 10

HALO = 16
VMEM_LIMIT = 56 * 1024 * 1024
N_CHIPS = 4
N_DEV = 8


def _dot(a, b):
    return jnp.dot(a, b, preferred_element_type=F32)


def _dot_nt(a, b):
    return lax.dot_general(a, b, (((1,), (1,)), ((), ())), preferred_element_type=F32)


def _dot_tn(a, b):
    return lax.dot_general(a, b, (((0,), (0,)), ((), ())), preferred_element_type=F32)


def _sigmoid(a):
    return 1.0 / (1.0 + jnp.exp(-a))


def _sum0(v):
    return jnp.sum(v, axis=0, keepdims=True)


def _nm(x, g, shift, scale):
    r = lax.rsqrt(jnp.mean(x * x, axis=-1, keepdims=True) + RMS_EPS)
    xhat = x * r
    nrm = xhat * g
    return nrm * (1.0 + scale) + shift, xhat, r, nrm


def _nm_bwd(dhn, xhat, r, nrm, g, scale):
    dshift = _sum0(dhn)
    dscale = _sum0(dhn * nrm)
    dnrm = dhn * (1.0 + scale)
    dgn = _sum0(dnrm * xhat)
    dxh = dnrm * g
    dx = r * (dxh - xhat * jnp.mean(dxh * xhat, axis=-1, keepdims=True))
    return dx, dshift, dscale, dgn


def _acc_rows(acc_ref, first, rows):
    @pl.when(first)
    def _():
        acc_ref[...] = jnp.zeros(acc_ref.shape, acc_ref.dtype)
    for k, row in enumerate(rows):
        if row is not None:
            acc_ref[k:k + 1, :] += row


def _shift_rows(v, k):
    n = v.shape[0]
    k = k % n
    return v if k == 0 else pltpu.roll(v, k, 0)


def _tile(tm, w):
    return pl.BlockSpec((tm, w), lambda i: (i, 0))


def _full(shape):
    nd = len(shape)
    return pl.BlockSpec(shape, lambda i: (0,) * nd)


def _resident(block, imap):
    return pl.BlockSpec(block, imap, pipeline_mode=pl.Buffered(1))


def _halo_prev(tm, w):
    return pl.BlockSpec((HALO, w), lambda i: (jnp.maximum(i * (tm // HALO) - 1, 0), 0))


def _halo_next(tm, w, L):
    return pl.BlockSpec((HALO, w), lambda i: (jnp.minimum((i + 1) * (tm // HALO), L // HALO - 1), 0))


def _params(vmem=VMEM_LIMIT):
    return pltpu.CompilerParams(vmem_limit_bytes=vmem)


def _pick_rows(rows, cols, itemsize=4, target=1 << 20):
    best = None
    for t in range(8, rows + 1, 8):
        if rows % t == 0 and t * cols * itemsize <= target:
            best = t
    return best if best is not None else rows


def _ext(prev, cur, nxt, i, nt):
    prev = jnp.where(i > 0, prev, jnp.zeros_like(prev))
    nxt = jnp.where(i < nt - 1, nxt, jnp.zeros_like(nxt))
    return jnp.concatenate([prev, cur, nxt], axis=0)


def _cast_bf16(a2d, name):
    rows, cols = a2d.shape
    tr = _pick_rows(rows, cols)

    def body(a_ref, o_ref):
        o_ref[...] = a_ref[...].astype(BF16)

    return pl.pallas_call(
        body, name=name, grid=(rows // tr,), in_specs=[_tile(tr, cols)], out_specs=_tile(tr, cols),
        out_shape=jax.ShapeDtypeStruct((rows, cols), BF16))(a2d)


def _sum_devices(g, name):
    n, rows, cols = g.shape
    tr = _pick_rows(rows, cols, target=1 << 18)

    def body(g_ref, o_ref):
        s = g_ref[0]
        for d in range(1, n):
            s = s + g_ref[d]
        o_ref[...] = s

    return pl.pallas_call(
        body, name=name, grid=(rows // tr,), in_specs=[pl.BlockSpec((n, tr, cols), lambda i: (0, i, 0))],
        out_specs=_tile(tr, cols), out_shape=jax.ShapeDtypeStruct((rows, cols), F32))(g)


def _adamw(w, g, m, v, name, ride=None):
    shape = w.shape
    cols = shape[-1]
    rows = w.size // cols
    w2, g2, m2, v2 = (t.reshape(rows, cols) for t in (w, g, m, v))
    tr = _pick_rows(rows, cols)
    c1 = 1.0 - ADAM_B1 ** ADAM_STEP
    c2 = 1.0 - ADAM_B2 ** ADAM_STEP

    def body(w_ref, g_ref, m_ref, v_ref, d_ref, mo_ref, vo_ref):
        gg = g_ref[...]
        mn = ADAM_B1 * m_ref[...] + (1.0 - ADAM_B1) * gg
        vn = ADAM_B2 * v_ref[...] + (1.0 - ADAM_B2) * (gg * gg)
        d_ref[...] = -ADAM_LR * ((mn / c1) / (jnp.sqrt(vn / c2) + ADAM_EPS) + ADAM_WD * w_ref[...])
        mo_ref[...] = mn
        vo_ref[...] = vn

    outs, carried = _ride_call(
        body, ride, (w2, g2, m2, v2), name=name, grid=(rows // tr,), in_specs=[_tile(tr, cols)] * 4,
        out_specs=[_tile(tr, cols)] * 3, out_shape=[jax.ShapeDtypeStruct((rows, cols), F32)] * 3)
    outs = tuple(o.reshape(shape) for o in outs)
    return outs if ride is None else (outs, carried)


def _mesh_pos():
    x, y, c = lax.axis_index("x"), lax.axis_index("y"), lax.axis_index("c")
    chips = [(1 - x, y), (x, 1 - y), (1 - x, 1 - y)]
    return x, y, c, chips


def _hbm_specs(n):
    return [pl.BlockSpec(memory_space=pltpu.HBM)] * n


def _small_all_gather(v, name):
    rows, w = v.shape

    def body(x_ref, out_ref, send_sems, recv_sems, local_sem):
        x, y, c, chips = _mesh_pos()
        me, sibling = (x, y, c), (x, y, 1 - c)

        def blk(px, py, pc):
            return out_ref.at[4 * px + 2 * py + pc]

        def copy(k, block, to, src=None):
            return pltpu.make_async_remote_copy(
                src_ref=blk(*block) if src is None else src, dst_ref=blk(*block),
                send_sem=send_sems.at[k], recv_sem=recv_sems.at[k], device_id=to, device_id_type=MESH)

        mine = pltpu.make_async_copy(x_ref, blk(*me), local_sem)
        mine.start()
        first = [copy(0, me, sibling, src=x_ref)]
        first += [copy(1 + j, me, (*chip, c), src=x_ref) for j, chip in enumerate(chips)]
        for cp in first:
            cp.start()
        passed = [copy(4 + j, (*chip, c), sibling) for j, chip in enumerate(chips)]
        for j, chip in enumerate(chips):
            copy(1 + j, (*chip, c), me).wait_recv()
            passed[j].start()
        copy(0, sibling, me).wait_recv()
        for j, chip in enumerate(chips):
            copy(4 + j, (*chip, 1 - c), me).wait_recv()
        for cp in first + passed:
            cp.wait_send()
        mine.wait()

    return pl.pallas_call(
        body, name=name, out_shape=jax.ShapeDtypeStruct((N_DEV, rows, w), v.dtype),
        in_specs=[pl.BlockSpec(memory_space=pltpu.VMEM)], out_specs=pl.BlockSpec(memory_space=pltpu.VMEM),
        scratch_shapes=[pltpu.SemaphoreType.DMA((7,)), pltpu.SemaphoreType.DMA((7,)), pltpu.SemaphoreType.DMA],
    )(v)


def _gather_shards(shards, name):
    n = len(shards)

    def body(*refs):
        ins, outs = refs[:n], refs[n:2 * n]
        send_sems, recv_sems, local_sems = refs[2 * n:]
        x, y, c, chips = _mesh_pos()
        k = 2 * x + y
        sibling = (x, y, 1 - c)

        def window(t, chip_k, half):
            r = ins[t].shape[1]
            return outs[t].at[:, pl.ds(chip_k * r + half * (r // 2), r // 2), :]

        def copy(t, j, chip_k, half, to, src=None):
            return pltpu.make_async_remote_copy(
                src_ref=window(t, chip_k, half) if src is None else src, dst_ref=window(t, chip_k, half),
                send_sem=send_sems.at[6 * t + j], recv_sem=recv_sems.at[6 * t + j], device_id=to, device_id_type=MESH)

        started, local = [], []
        for t in range(n):
            r = ins[t].shape[1]
            lc = pltpu.make_async_copy(ins[t], outs[t].at[:, pl.ds(k * r, r), :], local_sems.at[t])
            lc.start()
            local.append(lc)
            src = ins[t].at[:, pl.ds(c * (r // 2), r // 2), :]
            for j, chip in enumerate(chips):
                cp = copy(t, j, k, c, (*chip, c), src=src)
                cp.start()
                started.append(cp)
        for t in range(n):
            for j, chip in enumerate(chips):
                kj = 2 * chip[0] + chip[1]
                copy(t, j, kj, c, sibling).wait_recv()
                cp = copy(t, 3 + j, kj, c, sibling)
                cp.start()
                started.append(cp)
        for t in range(n):
            for j, chip in enumerate(chips):
                kj = 2 * chip[0] + chip[1]
                copy(t, 3 + j, kj, 1 - c, sibling).wait_recv()
        for cp in started:
            cp.wait_send()
        for lc in local:
            lc.wait()

    out_shape = [jax.ShapeDtypeStruct((s.shape[0], N_CHIPS * s.shape[1], s.shape[2]), s.dtype) for s in shards]
    return pl.pallas_call(
        body, name=name, out_shape=out_shape, in_specs=_hbm_specs(n), out_specs=_hbm_specs(n),
        scratch_shapes=[pltpu.SemaphoreType.DMA((6 * n,)), pltpu.SemaphoreType.DMA((6 * n,)),
                        pltpu.SemaphoreType.DMA((n,))],
    )(*shards)


def _chunk_rows(h, w, limit=2 << 20):
    best = 16
    for t in range(16, h + 1, 16):
        if h % t == 0 and t * w * 4 <= limit:
            best = t
    return best


def _pair_sum(part, pos, name):
    _, h, w = part.shape
    cr = _chunk_rows(h, w)
    nc = h // cr
    n = 4 * nc
    slots = 4

    def body(pos_ref, own_ref, send_ref, s_ref, sb_ref, stage, rbuf, send_sems, recv_sems):
        x, y, c, _ = _mesh_pos()
        k = pl.program_id(0)

        def copy(chunk):
            return pltpu.make_async_remote_copy(
                src_ref=stage.at[chunk % 2], dst_ref=rbuf.at[chunk % slots], send_sem=send_sems.at[chunk % 2],
                recv_sem=recv_sems.at[chunk % slots], device_id=(x, y, 1 - c), device_id_type=MESH)

        @pl.when(k >= 2)
        def _():
            copy(k - 2).wait_send()

        @pl.when(k < n)
        def _():
            stage[k % 2] = send_ref[...]
            copy(k).start()

        @pl.when(k > 0)
        def _():
            copy(k - 1).wait_recv()
            s = own_ref[...] + rbuf[(k - 1) % slots]
            s_ref[...] = s
            sb_ref[...] = s.astype(BF16)

        @pl.when(k == n)
        def _():
            copy(k - 1).wait_send()

    def own(k, p):
        j = jnp.maximum(k - 1, 0)
        return ((2 * (j // nc) + p[1]) * nc + j % nc, 0)

    def send(k, p):
        j = jnp.minimum(k, n - 1)
        return ((2 * (j // nc) + 1 - p[1]) * nc + j % nc, 0)

    grid_spec = pltpu.PrefetchScalarGridSpec(
        num_scalar_prefetch=1, grid=(n + 1,),
        in_specs=[pl.BlockSpec((cr, w), own), pl.BlockSpec((cr, w), send)],
        out_specs=[pl.BlockSpec((cr, w), lambda k, p: (jnp.maximum(k - 1, 0), 0))] * 2,
        scratch_shapes=[pltpu.VMEM((2, cr, w), F32), pltpu.VMEM((slots, cr, w), F32),
                        pltpu.SemaphoreType.DMA((2,)), pltpu.SemaphoreType.DMA((slots,))])
    part2 = part.reshape(8 * h, w)
    s, sb = pl.pallas_call(
        body, name=name, grid_spec=grid_spec, compiler_params=_params(),
        out_shape=[jax.ShapeDtypeStruct((4 * h, w), F32), jax.ShapeDtypeStruct((4 * h, w), BF16)],
    )(pos, part2, part2)
    return s.reshape(4, h, w), sb.reshape(4, h, w)


class _Ride:
    def __init__(self, ins, out_shape, sems, copies):
        self.ins, self.out_shape, self.sems, self.copies = list(ins), list(out_shape), list(sems), copies

    def start(self, ins, outs, sems):
        sends, _, _, local = self.copies(ins, outs, sems)
        for cp in local + sends:
            cp.start()

    def finish(self, ins, outs, sems):
        _, recvs, sends, local = self.copies(ins, outs, sems)
        for cp in recvs:
            cp.wait_recv()
        for cp in sends:
            cp.wait_send()
        for cp in local:
            cp.wait()


def _scatter_ride(sums_bf16):
    n = len(sums_bf16)

    def copies(ins, outs, sems):
        send_sems, recv_sems = sems
        x, y, c, chips = _mesh_pos()
        cps = [pltpu.make_async_remote_copy(
            src_ref=ins[t].at[2 * chip[0] + chip[1]], dst_ref=outs[t].at[j],
            send_sem=send_sems.at[3 * t + j], recv_sem=recv_sems.at[3 * t + j],
            device_id=(*chip, c), device_id_type=MESH) for t in range(n) for j, chip in enumerate(chips)]
        return cps, cps, cps, []

    return _Ride(sums_bf16, [jax.ShapeDtypeStruct((3,) + s.shape[1:], BF16) for s in sums_bf16],
                 [pltpu.SemaphoreType.DMA((3 * n,)), pltpu.SemaphoreType.DMA((3 * n,))], copies)


def _broadcast_ride(shards):
    n = len(shards)

    def copies(ins, outs, sems):
        send_sems, recv_sems, local_sems = sems
        x, y, c, chips = _mesh_pos()
        k = 2 * x + y
        sends, recvs, local = [], [], []
        for t in range(n):
            r = ins[t].shape[1]
            h = r // 2
            local.append(pltpu.make_async_copy(ins[t], outs[t].at[:, pl.ds(k * r, r), :], local_sems.at[t]))
            src = ins[t].at[:, pl.ds(c * h, h), :]
            mine = outs[t].at[:, pl.ds(k * r + c * h, h), :]
            for j, chip in enumerate(chips):
                kj = 2 * chip[0] + chip[1]
                for d in range(2):
                    sends.append(pltpu.make_async_remote_copy(
                        src_ref=src, dst_ref=mine, send_sem=send_sems.at[6 * t + 2 * j + d],
                        recv_sem=recv_sems.at[6 * t + 2 * j + c], device_id=(*chip, d), device_id_type=MESH))
                    theirs = outs[t].at[:, pl.ds(kj * r + d * h, h), :]
                    recvs.append(pltpu.make_async_remote_copy(
                        src_ref=theirs, dst_ref=theirs, send_sem=send_sems.at[6 * t + 2 * j + d],
                        recv_sem=recv_sems.at[6 * t + 2 * j + d], device_id=(*chip, d), device_id_type=MESH))
        return sends, recvs, sends, local

    return _Ride(shards, [jax.ShapeDtypeStruct((s.shape[0], N_CHIPS * s.shape[1], s.shape[2]), s.dtype) for s in shards],
                 [pltpu.SemaphoreType.DMA((6 * n,)), pltpu.SemaphoreType.DMA((6 * n,)), pltpu.SemaphoreType.DMA((n,))],
                 copies)


def _ride_call(body, ride, args, *, name, grid, in_specs, out_specs, out_shape, compiler_params=None):
    in_specs, out_specs, out_shape = list(in_specs), list(out_specs), list(out_shape)
    if ride is None:
        res = pl.pallas_call(body, name=name, grid=grid, in_specs=in_specs, out_specs=out_specs, out_shape=out_shape,
                             compiler_params=compiler_params)(*args)
        return list(res), []
    ni, no, ri, ro = len(in_specs), len(out_specs), len(ride.ins), len(ride.out_shape)

    def at_step(pick):
        hit = None
        for d, n in enumerate(grid):
            here = pl.program_id(d) == pick(n)
            hit = here if hit is None else hit & here
        return hit

    def carried(*refs):
        ins, rins = refs[:ni], refs[ni:ni + ri]
        outs, routs = refs[ni + ri:ni + ri + no], refs[ni + ri + no:ni + ri + no + ro]
        sems = refs[ni + ri + no + ro:]

        @pl.when(at_step(lambda n: 0))
        def _():
            ride.start(rins, routs, sems)

        body(*ins, *outs)

        @pl.when(at_step(lambda n: n - 1))
        def _():
            ride.finish(rins, routs, sems)

    res = pl.pallas_call(
        carried, name=name, grid=grid, in_specs=in_specs + _hbm_specs(ri), out_specs=out_specs + _hbm_specs(ro),
        out_shape=out_shape + ride.out_shape, scratch_shapes=ride.sems, compiler_params=compiler_params,
    )(*args, *ride.ins)
    return list(res[:no]), list(res[no:])


def _sum_and_join(sums, got, pos, name):
    n = len(sums)
    _, h, w = sums[0].shape
    cr = _chunk_rows(h, w, limit=(2 << 20) // n)

    def body(pos_ref, *refs):
        mine, theirs, outs = refs[:n], refs[n:2 * n], refs[2 * n:3 * n]
        ebuf, rbuf, send_sems, recv_sems = refs[3 * n:]
        x, y, c, _ = _mesh_pos()
        slot = pl.program_id(0) % 2
        cps = []
        for t in range(n):
            e = mine[t][...]
            for j in range(3):
                e = e + theirs[t][j].astype(F32)
            ebuf[t, slot] = e
            cp = pltpu.make_async_remote_copy(
                src_ref=ebuf.at[t, slot], dst_ref=rbuf.at[t, slot], send_sem=send_sems.at[2 * t + slot],
                recv_sem=recv_sems.at[2 * t + slot], device_id=(x, y, 1 - c), device_id_type=MESH)
            cp.start()
            outs[t][pos_ref[1]] = e
            cps.append(cp)
        for t, cp in enumerate(cps):
            cp.wait_recv()
            outs[t][1 - pos_ref[1]] = rbuf[t, slot]
        for cp in cps:
            cp.wait_send()

    grid_spec = pltpu.PrefetchScalarGridSpec(
        num_scalar_prefetch=1, grid=(h // cr,),
        in_specs=[pl.BlockSpec((None, cr, w), lambda i, p: (p[0], i, 0))] * n +
                 [pl.BlockSpec((3, cr, w), lambda i, p: (0, i, 0))] * n,
        out_specs=[pl.BlockSpec((2, cr, w), lambda i, p: (0, i, 0))] * n,
        scratch_shapes=[pltpu.VMEM((n, 2, cr, w), F32), pltpu.VMEM((n, 2, cr, w), F32),
                        pltpu.SemaphoreType.DMA((2 * n,)), pltpu.SemaphoreType.DMA((2 * n,))])
    return pl.pallas_call(
        body, name=name, grid_spec=grid_spec, compiler_params=_params(),
        out_shape=[jax.ShapeDtypeStruct((2, h, w), F32)] * n,
    )(pos, *sums, *got)


def _pair_sums(parts, pos, tag):
    pairs = [_pair_sum(p.reshape(8, p.shape[0] // 8, p.shape[1]), pos, f"rs_pair_{tag}_{t}")
             for t, p in enumerate(parts)]
    return [s for s, _ in pairs], [sb for _, sb in pairs]


def _joins(sums, got, pos, tag):
    full = _sum_and_join(sums, got, pos, f"rs_join_{tag}")
    return [f.reshape(2 * f.shape[1], f.shape[2]) for f in full]


def _ffn_fwd(x, prm, w13g, w2g, t, name, tm=512, ride=None):
    L, D = x.shape
    Fh = w13g.shape[-1]
    tm = min(tm, L)

    def body(x_ref, p_ref, w13_ref, w2_ref, xo_ref, sv_ref, g_ref, y_ref):
        xv = x_ref[...]
        hn, _, _, _ = _nm(xv, p_ref[3:4, :], p_ref[0:1, :], p_ref[1:2, :])
        hb = hn.astype(BF16)
        acc = jnp.zeros((tm, D), F32)
        for j in range(2):
            a = _dot(hb, w13_ref[j])
            b = _dot(hb, w13_ref[2 + j])
            sg = _sigmoid(a)
            sa = a * sg
            sv_ref[:, j * Fh:(j + 1) * Fh] = sa.astype(BF16)
            sv_ref[:, (2 + j) * Fh:(3 + j) * Fh] = (b * (sg * (1.0 + a * (1.0 - sg)))).astype(BF16)
            g = (sa * b).astype(BF16)
            g_ref[:, j * Fh:(j + 1) * Fh] = g
            acc = acc + _dot(g, w2_ref[j * Fh:(j + 1) * Fh, :])
        y_ref[...] = acc.astype(BF16)
        xo_ref[...] = xv + (0.5 * p_ref[2:3, :]) * acc

    res, carried = _ride_call(
        body, ride, (x, prm, w13g, w2g), name=name, grid=(L // tm,),
        in_specs=[_tile(tm, D), _full((8, D)),
                  _resident((None, 4, D, Fh), lambda i: (t, 0, 0, 0)),
                  _resident((None, 2 * Fh, D), lambda i: (t, 0, 0))],
        out_specs=[_tile(tm, D), _tile(tm, 4 * Fh), _tile(tm, 2 * Fh), _tile(tm, D)],
        out_shape=[jax.ShapeDtypeStruct((L, D), F32), jax.ShapeDtypeStruct((L, 4 * Fh), BF16),
                   jax.ShapeDtypeStruct((L, 2 * Fh), BF16), jax.ShapeDtypeStruct((L, D), BF16)],
        compiler_params=_params())
    xo, sv, g, y = res
    return xo, (sv, g), y, carried


def _head_grad(xo, tgt, fg):
    D = xo.shape[-1]
    r = lax.rsqrt(jnp.mean(xo * xo, axis=-1, keepdims=True) + RMS_EPS)
    xhat = xo * r
    err = xhat * fg - tgt
    loss = 0.5 * jnp.sum(jnp.mean(err * err, axis=-1, keepdims=True), axis=0, keepdims=True)
    dy = err * (1.0 / D)
    dxh = dy * fg
    return r * (dxh - xhat * jnp.mean(dxh * xhat, axis=-1, keepdims=True)), _sum0(dy * xhat), loss


def _ffn_bwd(dout, x, sv, y, prm, w13g, w2g, t, name, tm=256, ride=None, head=None):
    L, D = x.shape
    Fh = w13g.shape[-1]
    tm = min(tm, L)

    def body(do_ref, x_ref, sv_ref, y_ref, p_ref, w13_ref, w2_ref, *rest):
        dx_ref, dab_ref, dy_ref, hn_ref, acc_ref = rest[-5:]
        i = pl.program_id(0)
        head_rows = []
        if head is None:
            do = do_ref[...]
        else:
            do, dfg, loss = _head_grad(do_ref[...], rest[0][...], rest[1][...])
            head_rows = [dfg, jnp.broadcast_to(loss, (1, D))]
        gain, shift, scale, gate = p_ref[3:4, :], p_ref[0:1, :], p_ref[1:2, :], p_ref[2:3, :]
        hn, xhat, r, nrm = _nm(x_ref[...], gain, shift, scale)
        hn_ref[...] = hn.astype(BF16)
        dgate = 0.5 * _sum0(do * y_ref[...].astype(F32))
        dyb = ((0.5 * gate) * do).astype(BF16)
        dy_ref[...] = dyb
        dhn = jnp.zeros((tm, D), F32)
        for j in range(2):
            dg = _dot_nt(dyb, w2_ref[j * Fh:(j + 1) * Fh, :])
            da = (dg * sv_ref[:, (2 + j) * Fh:(3 + j) * Fh].astype(F32)).astype(BF16)
            db = (dg * sv_ref[:, j * Fh:(j + 1) * Fh].astype(F32)).astype(BF16)
            dab_ref[:, j * Fh:(j + 1) * Fh] = da
            dab_ref[:, (2 + j) * Fh:(3 + j) * Fh] = db
            dhn = dhn + _dot_nt(da, w13_ref[j]) + _dot_nt(db, w13_ref[2 + j])
        dx, dshift, dscale, dgn = _nm_bwd(dhn, xhat, r, nrm, gain, scale)
        dx_ref[...] = do + dx
        _acc_rows(acc_ref, i == 0, [dshift, dscale, dgate, dgn] + head_rows)

    head_args = () if head is None else head
    head_specs = [] if head is None else [_tile(tm, D), _full((1, D))]
    res, carried = _ride_call(
        body, ride, (dout, x, sv, y, prm, w13g, w2g, *head_args), name=name, grid=(L // tm,),
        in_specs=[_tile(tm, D), _tile(tm, D), _tile(tm, 4 * Fh), _tile(tm, D), _full((8, D)),
                  _resident((None, 4, D, Fh), lambda i: (t, 0, 0, 0)),
                  _resident((None, 2 * Fh, D), lambda i: (t, 0, 0))] + head_specs,
        out_specs=[_tile(tm, D), _tile(tm, 4 * Fh), _tile(tm, D), _tile(tm, D), _full((8, D))],
        out_shape=[jax.ShapeDtypeStruct((L, D), F32), jax.ShapeDtypeStruct((L, 4 * Fh), BF16),
                   jax.ShapeDtypeStruct((L, D), BF16), jax.ShapeDtypeStruct((L, D), BF16),
                   jax.ShapeDtypeStruct((8, D), F32)],
        compiler_params=_params())
    return (*res, carried)


def _mm_tn(a, b, slabs, a_slabbed, name, init=None, tl=1024, ride=None):
    L = a.shape[0]
    ka = a.shape[1] // slabs if a_slabbed else a.shape[1]
    nb = b.shape[1] if a_slabbed else b.shape[1] // slabs
    tl = min(tl, L)
    has_init = init is not None

    def body(a_ref, b_ref, *rest):
        o_ref = rest[-1]
        step = pl.program_id(1)

        @pl.when(step == 0)
        def _():
            o_ref[...] = rest[0][...] if has_init else jnp.zeros((ka, nb), F32)

        o_ref[...] += _dot_tn(a_ref[...], b_ref[...])

    in_specs = [pl.BlockSpec((tl, ka), (lambda s, l: (l, s)) if a_slabbed else (lambda s, l: (l, 0))),
                pl.BlockSpec((tl, nb), (lambda s, l: (l, 0)) if a_slabbed else (lambda s, l: (l, s)))]
    args = [a, b]
    if has_init:
        in_specs.append(pl.BlockSpec((ka, nb), lambda s, l: (s, 0)))
        args.append(init)
    res, carried = _ride_call(
        body, ride, args, name=name, grid=(slabs, L // tl), in_specs=in_specs,
        out_specs=[pl.BlockSpec((ka, nb), lambda s, l: (s, 0))],
        out_shape=[jax.ShapeDtypeStruct((slabs * ka, nb), F32)], compiler_params=_params())
    return res[0] if ride is None else (res[0], carried)


def _even_in_fwd(x, prm, wing, name, tm=512):
    L, D = x.shape
    W = wing.shape[-1]
    tm = min(tm, L)

    def body(x_ref, p_ref, w_ref, q_ref, k_ref, v_ref, u_ref, hn_ref):
        hn, _, _, _ = _nm(x_ref[...], p_ref[3:4, :], p_ref[0:1, :], p_ref[1:2, :])
        hb = hn.astype(BF16)
        hn_ref[...] = hb
        q_ref[...] = _dot(hb, w_ref[0]).astype(BF16)
        k_ref[...] = _dot(hb, w_ref[1]).astype(BF16)
        v_ref[...] = _dot(hb, w_ref[2]).astype(BF16)
        u_ref[...] = _dot(hb, w_ref[3])

    return pl.pallas_call(
        body, name=name, grid=(L // tm,),
        in_specs=[_tile(tm, D), _full((8, D)), _resident((None, 4, D, W), lambda i: (0, 0, 0, 0))],
        out_specs=[_tile(tm, W)] * 4 + [_tile(tm, D)],
        out_shape=[jax.ShapeDtypeStruct((L, W), BF16)] * 3 + [jax.ShapeDtypeStruct((L, W), F32),
                                                              jax.ShapeDtypeStruct((L, D), BF16)],
        compiler_params=_params())(x, prm, wing)


def _even_in_bwd(dout, x, dq, dk, dv, du, prm, wing, name, tm=512):
    L, D = x.shape
    W = wing.shape[-1]
    tm = min(tm, L)

    def body(do_ref, x_ref, dq_ref, dk_ref, dv_ref, du_ref, p_ref, w_ref, dx_ref, ds_ref, acc_ref):
        i = pl.program_id(0)
        gain, shift, scale = p_ref[3:4, :], p_ref[0:1, :], p_ref[1:2, :]
        _, xhat, r, nrm = _nm(x_ref[...], gain, shift, scale)
        dhn = jnp.zeros((tm, D), F32)
        for s, ref in enumerate((dq_ref, dk_ref, dv_ref, du_ref)):
            d = ref[...].astype(BF16)
            ds_ref[:, s * W:(s + 1) * W] = d
            dhn = dhn + _dot_nt(d, w_ref[s])
        dx, dshift, dscale, dgn = _nm_bwd(dhn, xhat, r, nrm, gain, scale)
        dx_ref[...] = do_ref[...] + dx
        _acc_rows(acc_ref, i == 0, [dshift, dscale, None, dgn])

    return pl.pallas_call(
        body, name=name, grid=(L // tm,),
        in_specs=[_tile(tm, D), _tile(tm, D)] + [_tile(tm, W)] * 4 +
                 [_full((8, D)), _resident((None, 4, D, W), lambda i: (0, 0, 0, 0))],
        out_specs=[_tile(tm, D), _tile(tm, 4 * W), _full((8, D))],
        out_shape=[jax.ShapeDtypeStruct((L, D), F32), jax.ShapeDtypeStruct((L, 4 * W), BF16),
                   jax.ShapeDtypeStruct((8, D), F32)],
        compiler_params=_params())(dout, x, dq, dk, dv, du, prm, wing)


def _even_out_fwd(x, att, pool, prm, woutg, name, tm=512):
    L, D = x.shape
    W = D // 2
    tm = min(tm, L)

    def body(x_ref, a_ref, p_ref, prm_ref, w_ref, xo_ref, y_ref):
        yv = _dot(a_ref[...], w_ref[0:W, :]) + _dot(p_ref[...], w_ref[W:2 * W, :])
        y_ref[...] = yv.astype(BF16)
        xo_ref[...] = x_ref[...] + prm_ref[2:3, :] * yv

    return pl.pallas_call(
        body, name=name, grid=(L // tm,),
        in_specs=[_tile(tm, D), _tile(tm, W), _tile(tm, W), _full((8, D)),
                  _resident((None, D, D), lambda i: (0, 0, 0))],
        out_specs=[_tile(tm, D), _tile(tm, D)],
        out_shape=[jax.ShapeDtypeStruct((L, D), F32), jax.ShapeDtypeStruct((L, D), BF16)],
        compiler_params=_params())(x, att, pool, prm, woutg)


def _even_out_bwd(dout, y, prm, woutg, name, tm=512):
    L, D = dout.shape
    W = D // 2
    tm = min(tm, L)

    def body(do_ref, y_ref, p_ref, w_ref, dy_ref, da_ref, dp_ref, acc_ref):
        i = pl.program_id(0)
        do = do_ref[...]
        dgate = _sum0(do * y_ref[...].astype(F32))
        dyb = (p_ref[2:3, :] * do).astype(BF16)
        dy_ref[...] = dyb
        da_ref[...] = _dot_nt(dyb, w_ref[0:W, :]).astype(BF16)
        dp_ref[...] = _dot_nt(dyb, w_ref[W:2 * W, :])
        _acc_rows(acc_ref, i == 0, [None, None, dgate])

    return pl.pallas_call(
        body, name=name, grid=(L // tm,),
        in_specs=[_tile(tm, D), _tile(tm, D), _full((8, D)), _resident((None, D, D), lambda i: (0, 0, 0))],
        out_specs=[_tile(tm, D), _tile(tm, W), _tile(tm, W), _full((8, D))],
        out_shape=[jax.ShapeDtypeStruct((L, D), BF16), jax.ShapeDtypeStruct((L, W), BF16),
                   jax.ShapeDtypeStruct((L, W), F32), jax.ShapeDtypeStruct((8, D), F32)],
        compiler_params=_params())(dout, y, prm, woutg)


def _group_ri(variant, qr, kr):
    first_key = (0, qr, GK - NA_KH)[variant]
    if not first_key <= kr < first_key + NA_KH:
        return None
    return kr - qr + (NA_KH - 1, NA_KH - 1 - NA_KH // 2, NA_KH - 1 - (GK - GQ))[variant]


HEADS_PER_BLOCK = 128 // NA_HEAD_DIM


def _bias_table(rpb, name):
    H = rpb.shape[0]
    hpb = HEADS_PER_BLOCK
    nri, nci = 2 * NA_KH - 1, 2 * NA_KW - 1
    col = jnp.arange(GRID_W)
    rel = (col[None, :] - col[:, None] + (NA_KW - 1)).reshape(1, -1)
    onehot = (rel == jnp.arange(32)[:, None]).astype(F32)
    cs = jnp.clip(col - NA_KW // 2, 0, GRID_W - NA_KW)
    ok = ((col[None, :] >= cs[:, None]) & (col[None, :] < cs[:, None] + NA_KW)).astype(F32).reshape(1, -1)
    by_lane_block = rpb.reshape(H // hpb, hpb, nri, nci).transpose(1, 0, 2, 3)
    rpb2 = jnp.pad(by_lane_block.reshape(H * nri, nci), ((0, 0), (0, 32 - nci)))

    def body(r_ref, e_ref, m_ref, o_ref):
        t = jnp.dot(r_ref[...], e_ref[...], preferred_element_type=F32, precision=lax.Precision.HIGHEST)
        o_ref[...] = jnp.where(m_ref[...] > 0.0, t, NEG_INF)

    tab = pl.pallas_call(body, name=name, out_shape=jax.ShapeDtypeStruct((H * nri, GRID_W * GRID_W), F32))(
        rpb2, onehot, ok)
    tab = tab.reshape(hpb, H // hpb, nri, GRID_W, GRID_W)
    outside = jnp.full((H // hpb, GRID_W, GRID_W), NEG_INF, F32)
    variants = []
    for variant in range(3):
        rows = []
        for h in range(hpb):
            for qr in range(GQ):
                ris = [_group_ri(variant, qr, kr) for kr in range(GK)]
                rows.append(jnp.concatenate([outside if ri is None else tab[h, :, ri] for ri in ris], axis=2))
        variants.append(jnp.concatenate(rows, axis=1))
    return jnp.stack(variants, axis=1)


ATTN_SCALE = NA_HEAD_DIM ** -0.5
assert ATTN_SCALE == 0.125


def _attn_probs(q, kw, kc, bias):
    s_w = _dot_nt(q, kw) + bias
    s_c = _dot_nt(q, kc)
    m = jnp.maximum(jnp.max(s_w, axis=-1, keepdims=True), jnp.max(s_c, axis=-1, keepdims=True))
    e_w = jnp.exp(s_w - m)
    e_c = jnp.exp(s_c - m)
    inv = 1.0 / (jnp.sum(e_w, axis=-1, keepdims=True) + jnp.sum(e_c, axis=-1, keepdims=True))
    return e_w * inv, e_c * inv


def _group_place(g, R):
    G = R // GQ
    kb = jnp.clip(g * GQ - NA_KH // 2, 0, R - GK)
    variant = jnp.where(g == 0, 0, jnp.where(g == G - 1, 2, 1))
    return pl.multiple_of(g * (GQ * GRID_W), GQ * GRID_W), pl.multiple_of(kb * GRID_W, GRID_W), variant


def _lane_masks(width, dh):
    lane = lax.broadcasted_iota(jnp.int32, (1, width), 1)
    return [(lane >= h * dh) & (lane < (h + 1) * dh) for h in range(width // dh)]


def _only(mask, a):
    return jnp.where(mask, a, jnp.zeros_like(a))


def _attn_fwd(q, k, v, kc, vc, bias, name, ride=None):
    L, width = q.shape
    C = kc.shape[0]
    dh = NA_HEAD_DIM
    lanes = 128
    hpb = lanes // dh
    R = L // GRID_W
    nq, nk = GQ * GRID_W, GK * GRID_W
    scale = dh ** -0.5

    def body(q_ref, k_ref, v_ref, kc_ref, vc_ref, b_ref, o_ref):
        masks = _lane_masks(lanes, dh)
        kc2 = kc_ref[...]
        vcs = [_only(m, vc_ref[...]) for m in masks]

        def group(g, carry):
            q0, k0, variant = _group_place(g, R)
            q2 = q_ref[pl.ds(q0, nq), :]
            k2 = k_ref[pl.ds(k0, nk), :]
            v2 = v_ref[pl.ds(k0, nk), :]
            qs = jnp.concatenate([_only(m, q2) for m in masks], axis=0) * scale
            p_w, p_c = _attn_probs(qs, k2, kc2, b_ref[variant])
            p_w, p_c = p_w.astype(BF16), p_c.astype(BF16)
            o2 = jnp.zeros((nq, lanes), F32)
            for h, m in enumerate(masks):
                rows = slice(h * nq, (h + 1) * nq)
                o2 = o2 + _dot(p_w[rows], _only(m, v2)) + _dot(p_c[rows], vcs[h])
            o_ref[pl.ds(q0, nq), :] = o2.astype(BF16)
            return carry

        lax.fori_loop(0, R // GQ, group, 0)

    cols = lambda n: pl.BlockSpec((n, lanes), lambda p: (0, p))
    res, carried = _ride_call(
        body, ride, (q, k, v, kc, vc, bias), name=name, grid=(width // lanes,),
        in_specs=[cols(L), cols(L), cols(L), cols(C), cols(C),
                  pl.BlockSpec((None, 3, hpb * nq, nk), lambda p: (p, 0, 0, 0))],
        out_specs=[cols(L)], out_shape=[jax.ShapeDtypeStruct((L, width), BF16)],
        compiler_params=_params())
    return res[0], carried


def _attn_bwd(q, k, v, kc, vc, bias, do, name, ride=None):
    L, width = q.shape
    C = kc.shape[0]
    dh = NA_HEAD_DIM
    lanes = 128
    hpb = lanes // dh
    R = L // GRID_W
    nq, nk = GQ * GRID_W, GK * GRID_W
    scale = dh ** -0.5

    def body(q_ref, k_ref, v_ref, kc_ref, vc_ref, b_ref, do_ref, dq_ref, dk_ref, dv_ref, dkc_ref, dvc_ref, db_ref):
        masks = _lane_masks(lanes, dh)
        kc2 = kc_ref[...]
        vc2 = vc_ref[...]
        kcs = [_only(m, kc2) * scale for m in masks]
        dk_ref[...] = jnp.zeros((L, lanes), F32)
        dv_ref[...] = jnp.zeros((L, lanes), F32)
        dkc_ref[...] = jnp.zeros((C, lanes), F32)
        dvc_ref[...] = jnp.zeros((C, lanes), F32)
        db_ref[...] = jnp.zeros((3, hpb * nq, nk), F32)

        def group(g, carry):
            q0, k0, variant = _group_place(g, R)
            q2 = q_ref[pl.ds(q0, nq), :]
            k2 = k_ref[pl.ds(k0, nk), :]
            v2 = v_ref[pl.ds(k0, nk), :]
            do2 = do_ref[pl.ds(q0, nq), :]
            qs = jnp.concatenate([_only(m, q2) for m in masks], axis=0) * scale
            dos = jnp.concatenate([_only(m, do2) for m in masks], axis=0)
            p_w, p_c = _attn_probs(qs, k2, kc2, b_ref[variant])
            dp_w = _dot_nt(dos, v2)
            dp_c = _dot_nt(dos, vc2)
            delta = jnp.sum(p_w * dp_w, axis=-1, keepdims=True) + jnp.sum(p_c * dp_c, axis=-1, keepdims=True)
            ds_w = p_w * (dp_w - delta)
            ds_c = p_c * (dp_c - delta)
            db_ref[variant] += ds_w
            dsw = ds_w.astype(BF16)
            dsc = ds_c.astype(BF16)
            dq2 = jnp.zeros((nq, lanes), F32)
            for h, m in enumerate(masks):
                rows = slice(h * nq, (h + 1) * nq)
                dq2 = dq2 + _dot(dsw[rows], _only(m, k2) * scale) + _dot(dsc[rows], kcs[h])
            dq_ref[pl.ds(q0, nq), :] = dq2.astype(BF16)
            dk_ref[pl.ds(k0, nk), :] += _dot_tn(dsw, qs)
            dv_ref[pl.ds(k0, nk), :] += _dot_tn(p_w.astype(BF16), dos)
            dkc_ref[...] += _dot_tn(dsc, qs)
            dvc_ref[...] += _dot_tn(p_c.astype(BF16), dos)
            return carry

        lax.fori_loop(0, R // GQ, group, 0)

    cols = lambda n: _resident((n, lanes), lambda p: (0, p))
    bspec = _resident((None, 3, hpb * nq, nk), lambda p: (p, 0, 0, 0))
    res, carried = _ride_call(
        body, ride, (q, k, v, kc, vc, bias, do), name=name, grid=(width // lanes,),
        in_specs=[cols(L), cols(L), cols(L), cols(C), cols(C), bspec, cols(L)],
        out_specs=[cols(L), cols(L), cols(L), cols(C), cols(C), bspec],
        out_shape=[jax.ShapeDtypeStruct((L, width), BF16)] + [jax.ShapeDtypeStruct((L, width), F32)] * 2 +
                  [jax.ShapeDtypeStruct((C, width), F32)] * 2 +
                  [jax.ShapeDtypeStruct((width // lanes, 3, hpb * nq, nk), F32)],
        compiler_params=_params())
    return (*res, carried)


def _rpb_grad(dbias, name):
    hpb = HEADS_PER_BLOCK
    H = dbias.shape[0] * hpb
    nri, nci = 2 * NA_KH - 1, 2 * NA_KW - 1
    d6 = dbias.reshape(H // hpb, 3, hpb, GQ, GRID_W, GK, GRID_W).transpose(0, 2, 1, 3, 5, 4, 6)
    d6 = d6.reshape(H, 3, GQ, GK, GRID_W, GRID_W)
    col = jnp.arange(GRID_W)
    onehot = (col[None, None, :] - col[None, :, None] + (NA_KW - 1) == jnp.arange(32)[:, None, None]).astype(F32)
    places = [(v, qr, kr) for v in range(3) for qr in range(GQ) for kr in range(GK)]

    def body(d_ref, m_ref, o_ref, t_ref):
        t_ref[...] = jnp.zeros((32, GRID_W), F32)
        o_ref[...] = jnp.zeros((16, 32, 128), F32)
        for ri in range(nri):
            a = None
            for place in places:
                if _group_ri(*place) == ri:
                    blk = d_ref[place]
                    a = blk if a is None else a + blk
            for ci in range(nci):
                t_ref[ci:ci + 1, :] = _sum0(a * m_ref[ci])
            o_ref[ri] = jnp.broadcast_to(jnp.sum(t_ref[...], axis=1, keepdims=True), (32, 128))

    out = pl.pallas_call(
        body, name=name, grid=(H,),
        in_specs=[pl.BlockSpec((None, 3, GQ, GK, GRID_W, GRID_W), lambda h: (h, 0, 0, 0, 0, 0)),
                  pl.BlockSpec((32, GRID_W, GRID_W), lambda h: (0, 0, 0))],
        out_specs=pl.BlockSpec((None, 16, 32, 128), lambda h: (h, 0, 0, 0)),
        out_shape=jax.ShapeDtypeStruct((H, 16, 32, 128), F32),
        scratch_shapes=[pltpu.VMEM((32, GRID_W), F32)])(d6, onehot)
    return out[:, :nri, :nci, 0]


def _window_count(t, w, L):
    lo = jnp.clip(t - w // 2, 0, L)
    hi = jnp.clip(t - w // 2 + w, 0, L)
    return jnp.maximum(hi - lo, 1).astype(F32)


def _running_sum(v, w):
    k = 1
    while k < w:
        v = v + _shift_rows(v, k)
        k *= 2
    return v


def _pool_fwd(u, poolw, pscale, name, tm=512):
    L, W = u.shape
    G = POOL_GROUP_DIM
    tm = min(tm, L)
    nt = L // tm

    def body(c_ref, p_ref, n_ref, w_ref, s_ref, o_ref, dm_ref):
        i = pl.program_id(0)
        ext = _ext(p_ref[...], c_ref[...], n_ref[...], i, nt)
        t = i * tm + lax.broadcasted_iota(jnp.int32, (tm, 1), 0)
        for g, w in enumerate(POOL_WINDOWS):
            e = ext[:, g * G:(g + 1) * G]
            win = _shift_rows(_running_sum(e, w), -(w // 2 - 1))[HALO:HALO + tm]
            dmx = (win / _window_count(t, w, L) - e[HALO:HALO + tm]).astype(BF16)
            dm_ref[:, g * G:(g + 1) * G] = dmx
            o_ref[:, g * G:(g + 1) * G] = (_dot(dmx, w_ref[g]) * s_ref[:, g * G:(g + 1) * G]).astype(BF16)

    return pl.pallas_call(
        body, name=name, grid=(nt,),
        in_specs=[_tile(tm, W), _halo_prev(tm, W), _halo_next(tm, W, L), _full((4, G, G)), _full((1, W))],
        out_specs=[_tile(tm, W), _tile(tm, W)],
        out_shape=[jax.ShapeDtypeStruct((L, W), BF16)] * 2, compiler_params=_params())(u, u, u, poolw, pscale)


def _pool_bwd(dpool, dmx, poolw, pscale, name, tm=512):
    L, W = dpool.shape
    G = POOL_GROUP_DIM
    tm = min(tm, L)
    nt = L // tm

    def body(c_ref, p_ref, n_ref, dm_ref, w_ref, s_ref, du_ref, dw_ref, acc_ref):
        i = pl.program_id(0)
        ext = _ext(p_ref[...], c_ref[...], n_ref[...], i, nt)
        te = i * tm - HALO + lax.broadcasted_iota(jnp.int32, (tm + 2 * HALO, 1), 0)

        @pl.when(i == 0)
        def _():
            dw_ref[...] = jnp.zeros((4 * G, G), F32)

        rows = []
        for g, w in enumerate(POOL_WINDOWS):
            sc = s_ref[:, g * G:(g + 1) * G]
            dpre = (ext[:, g * G:(g + 1) * G] * sc).astype(BF16)
            dd = _dot_nt(dpre, w_ref[g])
            spread = _shift_rows(_running_sum(dd / _window_count(te, w, L), w), -(w // 2))
            du_ref[:, g * G:(g + 1) * G] = (spread - dd)[HALO:HALO + tm]
            dmx_g = dm_ref[:, g * G:(g + 1) * G]
            rows.append(_sum0(c_ref[:, g * G:(g + 1) * G] * _dot(dmx_g, w_ref[g])))
            dw_ref[g * G:(g + 1) * G, :] += _dot_tn(dmx_g, dpre[HALO:HALO + tm])
        _acc_rows(acc_ref, i == 0, [jnp.concatenate(rows, axis=1)])

    return pl.pallas_call(
        body, name=name, grid=(nt,),
        in_specs=[_tile(tm, W), _halo_prev(tm, W), _halo_next(tm, W, L), _tile(tm, W), _full((4, G, G)),
                  _full((1, W))],
        out_specs=[_tile(tm, W), _full((4 * G, G)), _full((8, W))],
        out_shape=[jax.ShapeDtypeStruct((L, W), F32), jax.ShapeDtypeStruct((4 * G, G), F32),
                   jax.ShapeDtypeStruct((8, W), F32)],
        compiler_params=_params())(dpool, dpool, dpool, dmx, poolw, pscale)


def _conv3(z, cw):
    return _shift_rows(z, 1) * cw[0] + z * cw[1] + _shift_rows(z, -1) * cw[2]


def _conv_fwd(x, prm, wing, woutg, name, tm=512):
    L, D = x.shape
    Ws = wing.shape[-1]
    tm = min(tm, L)
    nt = L // tm
    te = tm + 2 * HALO

    def body(c_ref, p_ref, n_ref, prm_ref, wi_ref, wo_ref, xo_ref, y_ref, b_ref):
        i = pl.program_id(0)
        xe = jnp.concatenate([p_ref[...], c_ref[...], n_ref[...]], axis=0)
        hn, _, _, _ = _nm(xe, prm_ref[3:4, :], prm_ref[0:1, :], prm_ref[1:2, :])
        hb = hn.astype(BF16)
        proj = jnp.concatenate([_dot(hb, wi_ref[s]) for s in range(4)], axis=1)
        bg, cg, xin = proj[:, :D], proj[:, D:2 * D], proj[:, 2 * D:]
        tpos = i * tm - HALO + lax.broadcasted_iota(jnp.int32, (te, 1), 0)
        valid = ((tpos >= 0) & (tpos < L)).astype(F32)
        yc = _conv3(cg * xin * valid, [prm_ref[4 + k:5 + k, :] for k in range(3)])
        h2 = (bg * yc)[HALO:HALO + tm].astype(BF16)
        yv = _dot(h2, wo_ref[...])
        y_ref[...] = yv.astype(BF16)
        xo_ref[...] = c_ref[...] + prm_ref[2:3, :] * yv
        b_ref[...] = proj[HALO:HALO + tm].astype(BF16)

    return pl.pallas_call(
        body, name=name, grid=(nt,),
        in_specs=[_tile(tm, D), _halo_prev(tm, D), _halo_next(tm, D, L), _full((8, D)),
                  _resident((None, 4, D, Ws), lambda i: (0, 0, 0, 0)),
                  _resident((None, D, D), lambda i: (0, 0, 0))],
        out_specs=[_tile(tm, D), _tile(tm, D), _tile(tm, 3 * D)],
        out_shape=[jax.ShapeDtypeStruct((L, D), F32), jax.ShapeDtypeStruct((L, D), BF16),
                   jax.ShapeDtypeStruct((L, 3 * D), BF16)],
        compiler_params=_params())(x, x, x, prm, wing, woutg)


def _conv_bwd(dout, x, y, bcx, prm, wing, woutg, name, tm=256, ride=None):
    L, D = x.shape
    Ws = wing.shape[-1]
    tm = min(tm, L)
    nt = L // tm
    te = tm + 2 * HALO

    def body(dc_ref, dp_ref, dn_ref, x_ref, y_ref, bc_ref, bp_ref, bn_ref, prm_ref, wi_ref, wo_ref,
             dx_ref, dpr_ref, h2_ref, dy_ref, hn_ref, acc_ref):
        i = pl.program_id(0)
        gain, shift, scale, gate = prm_ref[3:4, :], prm_ref[0:1, :], prm_ref[1:2, :], prm_ref[2:3, :]
        taps = [prm_ref[4 + k:5 + k, :] for k in range(3)]
        do = dc_ref[...]
        doe = _ext(dp_ref[...], do, dn_ref[...], i, nt)
        dye = (gate * doe).astype(BF16)
        dy_ref[...] = dye[HALO:HALO + tm]
        dh2 = _dot_nt(dye, wo_ref[...])
        be = jnp.concatenate([bp_ref[...], bc_ref[...], bn_ref[...]], axis=0).astype(F32)
        bg, cg, xin = be[:, :D], be[:, D:2 * D], be[:, 2 * D:]
        tpos = i * tm - HALO + lax.broadcasted_iota(jnp.int32, (te, 1), 0)
        valid = ((tpos >= 0) & (tpos < L)).astype(F32)
        z = cg * xin * valid
        yc = _conv3(z, taps)
        dyc = dh2 * bg
        h2_ref[...] = (bg * yc)[HALO:HALO + tm].astype(BF16)
        dz = _conv3(dyc, taps[::-1]) * valid
        dproj = jnp.concatenate([dh2 * yc, dz * xin, dz * cg], axis=1)[HALO:HALO + tm].astype(BF16)
        dpr_ref[...] = dproj
        dhn = jnp.zeros((tm, D), F32)
        for s in range(4):
            dhn = dhn + _dot_nt(dproj[:, s * Ws:(s + 1) * Ws], wi_ref[s])
        hn, xhat, r, nrm = _nm(x_ref[...], gain, shift, scale)
        hn_ref[...] = hn.astype(BF16)
        dx, dshift, dscale, dgn = _nm_bwd(dhn, xhat, r, nrm, gain, scale)
        dx_ref[...] = do + dx
        dgate = _sum0(do * y_ref[...].astype(F32))
        dtaps = [_sum0((dyc * _shift_rows(z, 1 - k))[HALO:HALO + tm]) for k in range(3)]
        _acc_rows(acc_ref, i == 0, [dshift, dscale, dgate, dgn] + dtaps)

    res, carried = _ride_call(
        body, ride, (dout, dout, dout, x, y, bcx, bcx, bcx, prm, wing, woutg), name=name, grid=(nt,),
        in_specs=[_tile(tm, D), _halo_prev(tm, D), _halo_next(tm, D, L), _tile(tm, D), _tile(tm, D),
                  _tile(tm, 3 * D), _halo_prev(tm, 3 * D), _halo_next(tm, 3 * D, L), _full((8, D)),
                  _resident((None, 4, D, Ws), lambda i: (0, 0, 0, 0)),
                  _resident((None, D, D), lambda i: (0, 0, 0))],
        out_specs=[_tile(tm, D), _tile(tm, 3 * D), _tile(tm, D), _tile(tm, D), _tile(tm, D), _full((8, D))],
        out_shape=[jax.ShapeDtypeStruct((L, D), F32), jax.ShapeDtypeStruct((L, 3 * D), BF16),
                   jax.ShapeDtypeStruct((L, D), BF16), jax.ShapeDtypeStruct((L, D), BF16),
                   jax.ShapeDtypeStruct((L, D), BF16), jax.ShapeDtypeStruct((8, D), F32)],
        compiler_params=_params())
    return (*res, carried)


def _mod_fwd(cond, mod_w, mod_b, name, tn=768):
    nl, D, N = mod_w.shape
    tn = min(tn, N)

    def body(c_ref, w_ref, b_ref, o_ref):
        cv = c_ref[...]
        s = (cv * _sigmoid(cv)).astype(BF16)
        o_ref[...] = _dot(s, w_ref[...].astype(BF16)) + b_ref[...]

    return pl.pallas_call(
        body, name=name, grid=(nl, N // tn),
        in_specs=[pl.BlockSpec((16, D), lambda l, j: (0, 0)), pl.BlockSpec((None, D, tn), lambda l, j: (l, 0, j)),
                  pl.BlockSpec((None, 1, tn), lambda l, j: (l, 0, j))],
        out_specs=pl.BlockSpec((None, 16, tn), lambda l, j: (l, 0, j)),
        out_shape=jax.ShapeDtypeStruct((nl, 16, N), F32), compiler_params=_params())(cond, mod_w, mod_b)


def _mod_bwd(cond, dm, mod_w, name, tn=768):
    nl, D, N = mod_w.shape
    tn = min(tn, N)

    def body(c_ref, d_ref, w_ref, dw_ref, dc_ref):
        first = (pl.program_id(0) == 0) & (pl.program_id(1) == 0)
        cv = c_ref[...]
        s = (cv * _sigmoid(cv)).astype(BF16)
        d = d_ref[...].astype(BF16)
        dw_ref[...] = _dot_tn(s, d)

        @pl.when(first)
        def _():
            dc_ref[...] = jnp.zeros((16, D), F32)

        dc_ref[...] += _dot_nt(d, w_ref[...].astype(BF16))

    return pl.pallas_call(
        body, name=name, grid=(nl, N // tn),
        in_specs=[pl.BlockSpec((16, D), lambda l, j: (0, 0)), pl.BlockSpec((None, 16, tn), lambda l, j: (l, 0, j)),
                  pl.BlockSpec((None, D, tn), lambda l, j: (l, 0, j))],
        out_specs=[pl.BlockSpec((None, D, tn), lambda l, j: (l, 0, j)), pl.BlockSpec((16, D), lambda l, j: (0, 0))],
        out_shape=[jax.ShapeDtypeStruct((nl, D, N), F32), jax.ShapeDtypeStruct((16, D), F32)],
        compiler_params=_params())(cond, dm, mod_w)


def _mod_small_grads(dm_all, cond, dsilu_parts, name):
    nl, _, N = dm_all.shape
    D = cond.shape[1]

    def body(d_ref, c_ref, p_ref, db_ref, dc_ref):
        for l in range(nl):
            db_ref[l] = _sum0(d_ref[l])
        tot = p_ref[0, 8:9, :]
        for k in range(1, N_CHIPS):
            tot = tot + p_ref[2 * k, 8:9, :]
        cv = c_ref[8:9, :]
        sg = _sigmoid(cv)
        dc_ref[...] = tot * (sg * (1.0 + cv * (1.0 - sg)))

    return pl.pallas_call(
        body, name=name, out_shape=[jax.ShapeDtypeStruct((nl, 1, N), F32), jax.ShapeDtypeStruct((1, D), F32)],
    )(dm_all, cond, dsilu_parts)


def _prm(rows, D):
    rows = [r.reshape(1, D) for r in rows]
    return jnp.concatenate(rows + [jnp.zeros((8 - len(rows), D), F32)], axis=0)


def kernel(x, c, ctx, c_ctx, mod_w, mod_b, norm_g, ffn_w13, ffn_w2, even_w_in, even_w_out, na_rpb, pool_w, pool_scale, conv_w_in, conv_w, conv_w_out, final_g, loss_target, m_c_ctx, m_mod_w, m_mod_b, m_norm_g, m_ffn_w13, m_ffn_w2, m_even_w_in, m_even_w_out, m_na_rpb, m_pool_w, m_pool_scale, m_conv_w_in, m_conv_w, m_conv_w_out, m_final_g, v_c_ctx, v_mod_w, v_mod_b, v_norm_g, v_ffn_w13, v_ffn_w2, v_even_w_in, v_even_w_out, v_na_rpb, v_pool_w, v_pool_scale, v_conv_w_in, v_conv_w, v_conv_w_out, v_final_g):
    xi, yi, ci = lax.axis_index("x"), lax.axis_index("y"), lax.axis_index("c")
    chip = 2 * xi + yi
    dev = 4 * xi + 2 * yi + ci
    _, L, D = x.shape
    C = ctx.shape[1]
    Ds = D // N_CHIPS
    Nm = mod_w.shape[-1]
    Fh = ffn_w13.shape[-1]
    Fq = ffn_w2.shape[2]
    assert ffn_w13.shape[:2] == (2, 2) and Fh == 2 * Fq and L % (GQ * GRID_W) == 0 and L // GRID_W >= GK and GQ == NA_KH // 2
    x0, ctx0, tgt = x[0], ctx[0], loss_target[0]

    pad = lambda a: jnp.pad(a, ((0, 0), (0, D - a.shape[1])))
    pack1 = jnp.concatenate([c, pad(norm_g.reshape(6, Ds)), pad(conv_w.reshape(3, Ds)), jnp.zeros((6, D), F32)], axis=0)
    g1 = _small_all_gather(pack1, "ag_cond")
    cond = jnp.concatenate([g1[:, 0], c_ctx[None], jnp.zeros((7, D), F32)], axis=0)
    norm_full = jnp.concatenate([g1[2 * k, 1:7, :Ds] for k in range(N_CHIPS)], axis=1).reshape(2, 3, D)
    convw_full = jnp.concatenate([g1[2 * k, 7:10, :Ds] for k in range(N_CHIPS)], axis=1)

    mod_b_loc = lax.dynamic_slice_in_dim(mod_b, chip * Nm, Nm, axis=1).reshape(2, 1, Nm)
    m_loc = _mod_fwd(cond, mod_w, mod_b_loc, "mod_fwd")
    g2 = _small_all_gather(m_loc.reshape(32, Nm), "ag_mod")
    m_all = jnp.concatenate([g2[2 * k] for k in range(N_CHIPS)], axis=1).reshape(2, 16, N_MOD, D)
    m_lat = lax.dynamic_index_in_dim(m_all, dev, axis=1, keepdims=False)
    m_ctx = m_all[:, 8]

    def prm(mods, layer, base, gain_idx, extra=()):
        return _prm([mods[layer, base], mods[layer, base + 1], mods[layer, base + 2], norm_full[layer, gain_idx],
                     *extra], D)

    def shard_bf16(w, name):
        return _cast_bf16(w.reshape(-1, w.shape[-1]), name).reshape(-1, *w.shape[-2:])

    w13s, w2s = shard_bf16(ffn_w13, "cast_w13"), shard_bf16(ffn_w2, "cast_w2")
    eins, eouts = shard_bf16(even_w_in, "cast_ein"), shard_bf16(even_w_out, "cast_eout")
    cins, couts = shard_bf16(conv_w_in, "cast_cin"), shard_bf16(conv_w_out, "cast_cout")
    ffn_shards = [[w13s[t:t + 1], w2s[t:t + 1]] for t in range(4)]

    def ffn_weights(w13g, w2g):
        return w13g.reshape(1, 4, D, Fh), w2g

    wf = [ffn_weights(*_gather_shards(ffn_shards[0], "ag_ffn0")), None, None, None]
    pos = jnp.stack([chip, ci]).astype(jnp.int32)

    p_f1 = prm(m_lat, 0, 0, 0)
    p_mx = prm(m_lat, 0, 3, 1)
    p_f2 = prm(m_lat, 0, 6, 2)
    p_g1 = prm(m_lat, 1, 0, 0)
    p_cv = prm(m_lat, 1, 3, 1, extra=(convw_full[0], convw_full[1], convw_full[2]))
    p_g2 = prm(m_lat, 1, 6, 2)
    pc_f1 = prm(m_ctx, 0, 0, 0)
    pc_mx = prm(m_ctx, 0, 3, 1)

    x1, ab1, y1, (eing, eoutg) = _ffn_fwd(x0, p_f1, *wf[0], 0, "ffn_fwd_l0a", ride=_broadcast_ride([eins, eouts]))
    eing = eing.reshape(1, 4, D, NA_WIDTH)
    ctx1, abc, yc, _ = _ffn_fwd(ctx0, pc_f1, *wf[0], 0, "ffn_fwd_ctx")
    q, k, v, u, hn_mx = _even_in_fwd(x1, p_mx, eing, "even_in_fwd")
    _, k_c, v_c, _, hn_cx = _even_in_fwd(ctx1, pc_mx, eing, "even_in_ctx")
    bias = _bias_table(na_rpb[0], "bias_table")
    att, gathered = _attn_fwd(q, k, v, k_c, v_c, bias, "attn_fwd", ride=_broadcast_ride(ffn_shards[1]))
    wf[1] = ffn_weights(*gathered)
    pw_b = _cast_bf16(pool_w.reshape(-1, POOL_GROUP_DIM), "cast_poolw").reshape(4, POOL_GROUP_DIM, POOL_GROUP_DIM)
    pool, dmx = _pool_fwd(u, pw_b, pool_scale, "pool_fwd")
    x2, ymx = _even_out_fwd(x1, att, pool, p_mx, eoutg, "even_out_fwd")
    x3, ab2, y2, gathered = _ffn_fwd(x2, p_f2, *wf[1], 0, "ffn_fwd_l0b",
                                     ride=_broadcast_ride(ffn_shards[2] + [cins, couts]))
    wf[2] = ffn_weights(*gathered[:2])
    cing, coutg = gathered[2].reshape(1, 4, D, conv_w_in.shape[-1]), gathered[3]
    x4, ab3, y3, gathered = _ffn_fwd(x3, p_g1, *wf[2], 0, "ffn_fwd_l1a", ride=_broadcast_ride(ffn_shards[3]))
    wf[3] = ffn_weights(*gathered)
    x5, ycv, bcx = _conv_fwd(x4, p_cv, cing, coutg, "conv_fwd")
    x6, ab4, y4, _ = _ffn_fwd(x5, p_g2, *wf[3], 0, "ffn_fwd_l1b")

    def ffn_back(dout, xin, ab, yy, p, t, tag, init13=None, init2=None, ride=None, head=None):
        sv, gact = ab
        dx, dab, dy, hn, acc, carried = _ffn_bwd(dout, xin, sv, yy, p, *wf[t], 0, f"ffn_bwd_{tag}", ride=ride,
                                                 head=head)
        dw13 = _mm_tn(hn, dab, 4, False, f"dw13_{tag}", init=init13)
        dw2 = _mm_tn(gact, dy, 2, True, f"dw2_{tag}", init=init2)
        return dx, acc, dw13, dw2, carried

    dx5, acc_g2, dw13_3, dw2_3, _ = ffn_back(x6, x5, ab4, y4, p_g2, 3, "l1b", head=(tgt, final_g.reshape(1, D)))
    acc_head = acc_g2[4:6]
    loss = lax.psum(acc_head[1, 0], ("x", "y", "c"))
    s_a, sb_a = _pair_sums([dw13_3, dw2_3], pos, "l1b")
    dx4, dproj, h2, dycv, hn_cv, acc_cv, got_a = _conv_bwd(dx5, x4, ycv, bcx, p_cv, cing, coutg, "conv_bwd",
                                                           ride=_scatter_ride(sb_a))
    dcin = _mm_tn(hn_cv, dproj, 4, False, "dw_cin")
    dcout = _mm_tn(h2, dycv, 1, False, "dw_cout")
    s_b, sb_b = _pair_sums([dcin, dcout], pos, "conv")
    dx3, acc_g1, dw13_2, dw2_2, got_b = ffn_back(dx4, x3, ab3, y3, p_g1, 2, "l1a", ride=_scatter_ride(sb_b))
    s_c, sb_c = _pair_sums([dw13_2, dw2_2], pos, "l1a")
    dx2, acc_f2, dw13_1, dw2_1, got_c = ffn_back(dx3, x2, ab2, y2, p_f2, 1, "l0b", ride=_scatter_ride(sb_c))

    dymx, datt, dpool, acc_mxo = _even_out_bwd(dx2, ymx, p_mx, eoutg, "even_out_bwd")
    deout = jnp.concatenate([_mm_tn(att, dymx, 1, False, "dw_eout_att"),
                             _mm_tn(pool, dymx, 1, False, "dw_eout_pool")], axis=0)
    s_d, sb_d = _pair_sums([dw13_1, dw2_1, deout], pos, "l0b")
    du, dpoolw, acc_pool = _pool_bwd(dpool, dmx, pw_b, pool_scale, "pool_bwd")
    dq, dk, dv, dkc, dvc, dbias, got_d = _attn_bwd(q, k, v, k_c, v_c, bias, datt, "attn_bwd",
                                                   ride=_scatter_ride(sb_d))
    drpb = _rpb_grad(dbias, "rpb_grad")
    dx1, dstack, acc_mxi = _even_in_bwd(dx2, x1, dq, dk, dv, du, p_mx, eing,
                                        "even_in_bwd")
    zc = jnp.zeros((C, NA_WIDTH), F32)
    dctx1, dstack_c, accc_mx = _even_in_bwd(jnp.zeros((C, D), F32), ctx1, zc, dkc, dvc, zc,
                                            pc_mx, eing, "even_in_bwd_ctx")
    dein_c = _mm_tn(hn_cx, dstack_c, 4, False, "dw_ein_ctx")
    dein = _mm_tn(hn_mx, dstack, 4, False, "dw_ein", init=dein_c)
    s_e, sb_e = _pair_sums([dein], pos, "ein")
    _, accc_f1, dw13_c, dw2_c, _ = ffn_back(dctx1, ctx0, abc, yc, pc_f1, 0, "ctx")
    sv1, gact1 = ab1
    dx0, dab, dy, hn, acc_f1, _ = _ffn_bwd(dx1, x0, sv1, y1, p_f1, *wf[0], 0, "ffn_bwd_l0a")
    dw13_0, got_e = _mm_tn(hn, dab, 4, False, "dw13_l0a", init=dw13_c, ride=_scatter_ride(sb_e))
    s_f13, sb_f13 = _pair_sums([dw13_0], pos, "l0a_w13")
    dw2_0, got_f13 = _mm_tn(gact1, dy, 2, True, "dw2_l0a", init=dw2_c, ride=_scatter_ride(sb_f13))
    s_f2, sb_f2 = _pair_sums([dw2_0], pos, "l0a_w2")

    z1 = jnp.zeros((1, D), F32)
    dm_lat = jnp.concatenate([acc_f1[0:3], acc_mxi[0:2], acc_mxo[2:3], acc_f2[0:3],
                              acc_g1[0:3], acc_cv[0:3], acc_g2[0:3]], axis=0)
    dm_ctx = jnp.concatenate([accc_f1[0:3], accc_mx[0:2]] + [z1] * 13, axis=0)
    dnorm = jnp.concatenate([acc_f1[3:4] + accc_f1[3:4], acc_mxi[3:4] + accc_mx[3:4], acc_f2[3:4],
                             acc_g1[3:4], acc_cv[3:4], acc_g2[3:4]], axis=0)
    rpb_flat = jnp.pad(drpb.reshape(-1), (0, 4 * D - drpb.size)).reshape(4, D)
    pack3 = jnp.concatenate([dm_lat, dm_ctx, dnorm, acc_cv[4:7], acc_head[0:1], pad(acc_pool[0:1]), z1,
                             dpoolw.reshape(-1, D), rpb_flat, jnp.zeros((4, D), F32)], axis=0)
    g3 = _small_all_gather(pack3, "ag_small")
    tot = _sum_devices(g3, "sum_small")
    dm_all = jnp.concatenate([g3[:, 0:18].reshape(8, 2, N_MOD * D).transpose(1, 0, 2),
                              tot[18:36].reshape(2, 1, N_MOD * D), jnp.zeros((2, 7, N_MOD * D), F32)], axis=1)
    dm_loc = lax.dynamic_slice_in_dim(dm_all, chip * Nm, Nm, axis=2)
    g_mod_w, dsilu = _mod_bwd(cond, dm_loc, mod_w, "mod_bwd")
    g4 = _small_all_gather(dsilu, "ag_dsilu")
    g_mod_b, g_c_ctx = _mod_small_grads(dm_all, cond, g4, "mod_small")
    g_mod_b = g_mod_b.reshape(2, N_MOD * D)
    g_c_ctx = g_c_ctx.reshape(D)
    g_norm_full = tot[36:42].reshape(2, 3, D)
    g_norm = lax.dynamic_slice_in_dim(g_norm_full, chip * Ds, Ds, axis=2)
    g_conv_w = lax.dynamic_slice_in_dim(tot[42:45], chip * Ds, Ds, axis=1).reshape(1, 3, Ds)
    g_final = tot[45]
    g_pscale = tot[46:47, :pool_scale.shape[1]]
    g_poolw = tot[48:112].reshape(pool_w.shape)
    g_rpb = tot[112:116].reshape(-1)[:na_rpb.size].reshape(na_rpb.shape)

    adamw_mod_w, got_f2 = _adamw(mod_w, g_mod_w, m_mod_w, v_mod_w, "adamw_mod_w", ride=_scatter_ride(sb_f2))
    r13 = _joins([s_f13[0], s_d[0], s_c[0], s_a[0]], [got_f13[0], got_d[0], got_c[0], got_a[0]], pos, "w13")
    r2 = _joins([s_f2[0], s_d[1], s_c[1], s_a[1]], [got_f2[0], got_d[1], got_c[1], got_a[1]], pos, "w2")
    r_eout, r_cout = _joins([s_d[2], s_b[1]], [got_d[2], got_b[1]], pos, "out")
    (r_ein,) = _joins(s_e, got_e, pos, "ein")
    (r_cin,) = _joins(s_b[:1], got_b[:1], pos, "cin")
    g_w13 = jnp.stack(r13).reshape(ffn_w13.shape)
    g_w2 = jnp.stack(r2).reshape(ffn_w2.shape)
    g_ein, g_eout, g_cin, g_cout = r_ein[None], r_eout[None], r_cin[None], r_cout[None]

    grads = [g_c_ctx, g_mod_w, g_mod_b, g_norm, g_w13, g_w2, g_ein, g_eout, g_rpb, g_poolw, g_pscale, g_cin,
             g_conv_w, g_cout, g_final]
    weights = [c_ctx, mod_w, mod_b, norm_g, ffn_w13, ffn_w2, even_w_in, even_w_out, na_rpb, pool_w, pool_scale,
               conv_w_in, conv_w, conv_w_out, final_g]
    ms = [m_c_ctx, m_mod_w, m_mod_b, m_norm_g, m_ffn_w13, m_ffn_w2, m_even_w_in, m_even_w_out, m_na_rpb, m_pool_w,
          m_pool_scale, m_conv_w_in, m_conv_w, m_conv_w_out, m_final_g]
    vs = [v_c_ctx, v_mod_w, v_mod_b, v_norm_g, v_ffn_w13, v_ffn_w2, v_even_w_in, v_even_w_out, v_na_rpb, v_pool_w,
          v_pool_scale, v_conv_w_in, v_conv_w, v_conv_w_out, v_final_g]
    names = ["c_ctx", "mod_w", "mod_b", "norm_g", "ffn_w13", "ffn_w2", "even_w_in", "even_w_out", "na_rpb", "pool_w",
             "pool_scale", "conv_w_in", "conv_w", "conv_w_out", "final_g"]
    deltas, new_m, new_v = [], [], []
    for n, w, g, m, vv in zip(names, weights, grads, ms, vs):
        g = g.reshape(w.shape)
        if n == "mod_w":
            d, mn, vn = adamw_mod_w
        elif w.ndim == 1:
            d, mn, vn = (t.reshape(w.shape) for t in _adamw(w[None], g[None], m[None], vv[None], f"adamw_{n}"))
        else:
            d, mn, vn = _adamw(w, g, m, vv, f"adamw_{n}")
        deltas.append(d)
        new_m.append(mn)
        new_v.append(vn)
    grads = [g.reshape(w.shape) for g, w in zip(grads, weights)]
    return (loss, dx0[None], *grads, *deltas, *new_m, *new_v)
```

```python
import jax
import jax.numpy as jnp
from jax import lax
from jax.experimental import pallas as pl
from jax.experimental.pallas import tpu as pltpu

F32 = jnp.float32
BF16 = jnp.bfloat16
MESH = pl.DeviceIdType.MESH

GRID_W = 64
NA_HEADS = 8
NA_HEAD_DIM = 64
NA_KH = 8
NA_KW = 16
GQ = 4
GK = GQ + NA_KH
NA_WIDTH = NA_HEADS * NA_HEAD_DIM
POOL_WINDOWS = (2, 4, 8, 16)
POOL_GROUP_DIM = 128
N_MOD = 9
RMS_EPS = 1e-6
NEG_INF = -1e30
ADAM_LR, ADAM_B1, ADAM_B2, ADAM_EPS, ADAM_WD, ADAM_STEP = 0.001, 0.9, 0.999, 1e-08, 0.01, 10

HALO = 16
VMEM_LIMIT = 56 * 1024 * 1024
N_CHIPS = 4
N_DEV = 8


def _dot(a, b):
    return jnp.dot(a, b, preferred_element_type=F32)


def _dot_nt(a, b):
    return lax.dot_general(a, b, (((1,), (1,)), ((), ())), preferred_element_type=F32)


def _dot_tn(a, b):
    return lax.dot_general(a, b, (((0,), (0,)), ((), ())), preferred_element_type=F32)


def _sigmoid(a):
    return 1.0 / (1.0 + jnp.exp(-a))


def _sum0(v):
    return jnp.sum(v, axis=0, keepdims=True)


def _nm(x, g, shift, scale):
    r = lax.rsqrt(jnp.mean(x * x, axis=-1, keepdims=True) + RMS_EPS)
    xhat = x * r
    nrm = xhat * g
    return nrm * (1.0 + scale) + shift, xhat, r, nrm


def _nm_bwd(dhn, xhat, r, nrm, g, scale):
    dshift = _sum0(dhn)
    dscale = _sum0(dhn * nrm)
    dnrm = dhn * (1.0 + scale)
    dgn = _sum0(dnrm * xhat)
    dxh = dnrm * g
    dx = r * (dxh - xhat * jnp.mean(dxh * xhat, axis=-1, keepdims=True))
    return dx, dshift, dscale, dgn


def _acc_rows(acc_ref, first, rows):
    @pl.when(first)
    def _():
        acc_ref[...] = jnp.zeros(acc_ref.shape, acc_ref.dtype)
    for k, row in enumerate(rows):
        if row is not None:
            acc_ref[k:k + 1, :] += row


def _shift_rows(v, k):
    n = v.shape[0]
    k = k % n
    return v if k == 0 else pltpu.roll(v, k, 0)


def _tile(tm, w):
    return pl.BlockSpec((tm, w), lambda i: (i, 0))


def _full(shape):
    nd = len(shape)
    return pl.BlockSpec(shape, lambda i: (0,) * nd)


def _resident(block, imap):
    return pl.BlockSpec(block, imap, pipeline_mode=pl.Buffered(1))


def _halo_prev(tm, w):
    return pl.BlockSpec((HALO, w), lambda i: (jnp.maximum(i * (tm // HALO) - 1, 0), 0))


def _halo_next(tm, w, L):
    return pl.BlockSpec((HALO, w), lambda i: (jnp.minimum((i + 1) * (tm // HALO), L // HALO - 1), 0))


def _params(vmem=VMEM_LIMIT):
    return pltpu.CompilerParams(vmem_limit_bytes=vmem)


def _pick_rows(rows, cols, itemsize=4, target=1 << 20):
    best = None
    for t in range(8, rows + 1, 8):
        if rows % t == 0 and t * cols * itemsize <= target:
            best = t
    return best if best is not None else rows


def _ext(prev, cur, nxt, i, nt):
    prev = jnp.where(i > 0, prev, jnp.zeros_like(prev))
    nxt = jnp.where(i < nt - 1, nxt, jnp.zeros_like(nxt))
    return jnp.concatenate([prev, cur, nxt], axis=0)


def _cast_bf16(a2d, name):
    rows, cols = a2d.shape
    tr = _pick_rows(rows, cols)

    def body(a_ref, o_ref):
        o_ref[...] = a_ref[...].astype(BF16)

    return pl.pallas_call(
        body, name=name, grid=(rows // tr,), in_specs=[_tile(tr, cols)], out_specs=_tile(tr, cols),
        out_shape=jax.ShapeDtypeStruct((rows, cols), BF16))(a2d)


def _sum_devices(g, name):
    n, rows, cols = g.shape
    tr = _pick_rows(rows, cols, target=1 << 18)

    def body(g_ref, o_ref):
        s = g_ref[0]
        for d in range(1, n):
            s = s + g_ref[d]
        o_ref[...] = s

    return pl.pallas_call(
        body, name=name, grid=(rows // tr,), in_specs=[pl.BlockSpec((n, tr, cols), lambda i: (0, i, 0))],
        out_specs=_tile(tr, cols), out_shape=jax.ShapeDtypeStruct((rows, cols), F32))(g)


def _adamw(w, g, m, v, name, ride=None):
    shape = w.shape
    cols = shape[-1]
    rows = w.size // cols
    w2, g2, m2, v2 = (t.reshape(rows, cols) for t in (w, g, m, v))
    tr = _pick_rows(rows, cols)
    c1 = 1.0 - ADAM_B1 ** ADAM_STEP
    c2 = 1.0 - ADAM_B2 ** ADAM_STEP

    def body(w_ref, g_ref, m_ref, v_ref, d_ref, mo_ref, vo_ref):
        gg = g_ref[...]
        mn = ADAM_B1 * m_ref[...] + (1.0 - ADAM_B1) * gg
        vn = ADAM_B2 * v_ref[...] + (1.0 - ADAM_B2) * (gg * gg)
        d_ref[...] = -ADAM_LR * ((mn / c1) / (jnp.sqrt(vn / c2) + ADAM_EPS) + ADAM_WD * w_ref[...])
        mo_ref[...] = mn
        vo_ref[...] = vn

    outs, carried = _ride_call(
        body, ride, (w2, g2, m2, v2), name=name, grid=(rows // tr,), in_specs=[_tile(tr, cols)] * 4,
        out_specs=[_tile(tr, cols)] * 3, out_shape=[jax.ShapeDtypeStruct((rows, cols), F32)] * 3)
    outs = tuple(o.reshape(shape) for o in outs)
    return outs if ride is None else (outs, carried)


def _mesh_pos():
    x, y, c = lax.axis_index("x"), lax.axis_index("y"), lax.axis_index("c")
    chips = [(1 - x, y), (x, 1 - y), (1 - x, 1 - y)]
    return x, y, c, chips


def _hbm_specs(n):
    return [pl.BlockSpec(memory_space=pltpu.HBM)] * n


def _small_all_gather(v, name):
    rows, w = v.shape

    def body(x_ref, out_ref, send_sems, recv_sems, local_sem):
        x, y, c, chips = _mesh_pos()
        me, sibling = (x, y, c), (x, y, 1 - c)

        def blk(px, py, pc):
            return out_ref.at[4 * px + 2 * py + pc]

        def copy(k, block, to, src=None):
            return pltpu.make_async_remote_copy(
                src_ref=blk(*block) if src is None else src, dst_ref=blk(*block),
                send_sem=send_sems.at[k], recv_sem=recv_sems.at[k], device_id=to, device_id_type=MESH)

        mine = pltpu.make_async_copy(x_ref, blk(*me), local_sem)
        mine.start()
        first = [copy(0, me, sibling, src=x_ref)]
        first += [copy(1 + j, me, (*chip, c), src=x_ref) for j, chip in enumerate(chips)]
        for cp in first:
            cp.start()
        passed = [copy(4 + j, (*chip, c), sibling) for j, chip in enumerate(chips)]
        for j, chip in enumerate(chips):
            copy(1 + j, (*chip, c), me).wait_recv()
            passed[j].start()
        copy(0, sibling, me).wait_recv()
        for j, chip in enumerate(chips):
            copy(4 + j, (*chip, 1 - c), me).wait_recv()
        for cp in first + passed:
            cp.wait_send()
        mine.wait()

    return pl.pallas_call(
        body, name=name, out_shape=jax.ShapeDtypeStruct((N_DEV, rows, w), v.dtype),
        in_specs=[pl.BlockSpec(memory_space=pltpu.VMEM)], out_specs=pl.BlockSpec(memory_space=pltpu.VMEM),
        scratch_shapes=[pltpu.SemaphoreType.DMA((7,)), pltpu.SemaphoreType.DMA((7,)), pltpu.SemaphoreType.DMA],
    )(v)


def _gather_shards(shards, name):
    n = len(shards)

    def body(*refs):
        ins, outs = refs[:n], refs[n:2 * n]
        send_sems, recv_sems, local_sems = refs[2 * n:]
        x, y, c, chips = _mesh_pos()
        k = 2 * x + y
        sibling = (x, y, 1 - c)

        def window(t, chip_k, half):
            r = ins[t].shape[1]
            return outs[t].at[:, pl.ds(chip_k * r + half * (r // 2), r // 2), :]

        def copy(t, j, chip_k, half, to, src=None):
            return pltpu.make_async_remote_copy(
                src_ref=window(t, chip_k, half) if src is None else src, dst_ref=window(t, chip_k, half),
                send_sem=send_sems.at[6 * t + j], recv_sem=recv_sems.at[6 * t + j], device_id=to, device_id_type=MESH)

        started, local = [], []
        for t in range(n):
            r = ins[t].shape[1]
            lc = pltpu.make_async_copy(ins[t], outs[t].at[:, pl.ds(k * r, r), :], local_sems.at[t])
            lc.start()
            local.append(lc)
            src = ins[t].at[:, pl.ds(c * (r // 2), r // 2), :]
            for j, chip in enumerate(chips):
                cp = copy(t, j, k, c, (*chip, c), src=src)
                cp.start()
                started.append(cp)
        for t in range(n):
            for j, chip in enumerate(chips):
                kj = 2 * chip[0] + chip[1]
                copy(t, j, kj, c, sibling).wait_recv()
                cp = copy(t, 3 + j, kj, c, sibling)
                cp.start()
                started.append(cp)
        for t in range(n):
            for j, chip in enumerate(chips):
                kj = 2 * chip[0] + chip[1]
                copy(t, 3 + j, kj, 1 - c, sibling).wait_recv()
        for cp in started:
            cp.wait_send()
        for lc in local:
            lc.wait()

    out_shape = [jax.ShapeDtypeStruct((s.shape[0], N_CHIPS * s.shape[1], s.shape[2]), s.dtype) for s in shards]
    return pl.pallas_call(
        body, name=name, out_shape=out_shape, in_specs=_hbm_specs(n), out_specs=_hbm_specs(n),
        scratch_shapes=[pltpu.SemaphoreType.DMA((6 * n,)), pltpu.SemaphoreType.DMA((6 * n,)),
                        pltpu.SemaphoreType.DMA((n,))],
    )(*shards)


def _chunk_rows(h, w, limit=2 << 20):
    best = 16
    for t in range(16, h + 1, 16):
        if h % t == 0 and t * w * 4 <= limit:
            best = t
    return best


def _pair_sum(part, pos, name):
    _, h, w = part.shape
    cr = _chunk_rows(h, w)
    nc = h // cr
    n = 4 * nc
    slots = 4

    def body(pos_ref, own_ref, send_ref, s_ref, sb_ref, stage, rbuf, send_sems, recv_sems):
        x, y, c, _ = _mesh_pos()
        k = pl.program_id(0)

        def copy(chunk):
            return pltpu.make_async_remote_copy(
                src_ref=stage.at[chunk % 2], dst_ref=rbuf.at[chunk % slots], send_sem=send_sems.at[chunk % 2],
                recv_sem=recv_sems.at[chunk % slots], device_id=(x, y, 1 - c), device_id_type=MESH)

        @pl.when(k >= 2)
        def _():
            copy(k - 2).wait_send()

        @pl.when(k < n)
        def _():
            stage[k % 2] = send_ref[...]
            copy(k).start()

        @pl.when(k > 0)
        def _():
            copy(k - 1).wait_recv()
            s = own_ref[...] + rbuf[(k - 1) % slots]
            s_ref[...] = s
            sb_ref[...] = s.astype(BF16)

        @pl.when(k == n)
        def _():
            copy(k - 1).wait_send()

    def own(k, p):
        j = jnp.maximum(k - 1, 0)
        return ((2 * (j // nc) + p[1]) * nc + j % nc, 0)

    def send(k, p):
        j = jnp.minimum(k, n - 1)
        return ((2 * (j // nc) + 1 - p[1]) * nc + j % nc, 0)

    grid_spec = pltpu.PrefetchScalarGridSpec(
        num_scalar_prefetch=1, grid=(n + 1,),
        in_specs=[pl.BlockSpec((cr, w), own), pl.BlockSpec((cr, w), send)],
        out_specs=[pl.BlockSpec((cr, w), lambda k, p: (jnp.maximum(k - 1, 0), 0))] * 2,
        scratch_shapes=[pltpu.VMEM((2, cr, w), F32), pltpu.VMEM((slots, cr, w), F32),
                        pltpu.SemaphoreType.DMA((2,)), pltpu.SemaphoreType.DMA((slots,))])
    part2 = part.reshape(8 * h, w)
    s, sb = pl.pallas_call(
        body, name=name, grid_spec=grid_spec, compiler_params=_params(),
        out_shape=[jax.ShapeDtypeStruct((4 * h, w), F32), jax.ShapeDtypeStruct((4 * h, w), BF16)],
    )(pos, part2, part2)
    return s.reshape(4, h, w), sb.reshape(4, h, w)


class _Ride:
    def __init__(self, ins, out_shape, sems, copies):
        self.ins, self.out_shape, self.sems, self.copies = list(ins), list(out_shape), list(sems), copies

    def start(self, ins, outs, sems):
        sends, _, _, local = self.copies(ins, outs, sems)
        for cp in local + sends:
            cp.start()

    def finish(self, ins, outs, sems):
        _, recvs, sends, local = self.copies(ins, outs, sems)
        for cp in recvs:
            cp.wait_recv()
        for cp in sends:
            cp.wait_send()
        for cp in local:
            cp.wait()


def _scatter_ride(sums_bf16):
    n = len(sums_bf16)

    def copies(ins, outs, sems):
        send_sems, recv_sems = sems
        x, y, c, chips = _mesh_pos()
        cps = [pltpu.make_async_remote_copy(
            src_ref=ins[t].at[2 * chip[0] + chip[1]], dst_ref=outs[t].at[j],
            send_sem=send_sems.at[3 * t + j], recv_sem=recv_sems.at[3 * t + j],
            device_id=(*chip, c), device_id_type=MESH) for t in range(n) for j, chip in enumerate(chips)]
        return cps, cps, cps, []

    return _Ride(sums_bf16, [jax.ShapeDtypeStruct((3,) + s.shape[1:], BF16) for s in sums_bf16],
                 [pltpu.SemaphoreType.DMA((3 * n,)), pltpu.SemaphoreType.DMA((3 * n,))], copies)


def _broadcast_ride(shards):
    n = len(shards)

    def copies(ins, outs, sems):
        send_sems, recv_sems, local_sems = sems
        x, y, c, chips = _mesh_pos()
        k = 2 * x + y
        sends, recvs, local = [], [], []
        for t in range(n):
            r = ins[t].shape[1]
            h = r // 2
            local.append(pltpu.make_async_copy(ins[t], outs[t].at[:, pl.ds(k * r, r), :], local_sems.at[t]))
            src = ins[t].at[:, pl.ds(c * h, h), :]
            mine = outs[t].at[:, pl.ds(k * r + c * h, h), :]
            for j, chip in enumerate(chips):
                kj = 2 * chip[0] + chip[1]
                for d in range(2):
                    sends.append(pltpu.make_async_remote_copy(
                        src_ref=src, dst_ref=mine, send_sem=send_sems.at[6 * t + 2 * j + d],
                        recv_sem=recv_sems.at[6 * t + 2 * j + c], device_id=(*chip, d), device_id_type=MESH))
                    theirs = outs[t].at[:, pl.ds(kj * r + d * h, h), :]
                    recvs.append(pltpu.make_async_remote_copy(
                        src_ref=theirs, dst_ref=theirs, send_sem=send_sems.at[6 * t + 2 * j + d],
                        recv_sem=recv_sems.at[6 * t + 2 * j + d], device_id=(*chip, d), device_id_type=MESH))
        return sends, recvs, sends, local

    return _Ride(shards, [jax.ShapeDtypeStruct((s.shape[0], N_CHIPS * s.shape[1], s.shape[2]), s.dtype) for s in shards],
                 [pltpu.SemaphoreType.DMA((6 * n,)), pltpu.SemaphoreType.DMA((6 * n,)), pltpu.SemaphoreType.DMA((n,))],
                 copies)


def _ride_call(body, ride, args, *, name, grid, in_specs, out_specs, out_shape, compiler_params=None):
    in_specs, out_specs, out_shape = list(in_specs), list(out_specs), list(out_shape)
    if ride is None:
        res = pl.pallas_call(body, name=name, grid=grid, in_specs=in_specs, out_specs=out_specs, out_shape=out_shape,
                             compiler_params=compiler_params)(*args)
        return list(res), []
    ni, no, ri, ro = len(in_specs), len(out_specs), len(ride.ins), len(ride.out_shape)

    def at_step(pick):
        hit = None
        for d, n in enumerate(grid):
            here = pl.program_id(d) == pick(n)
            hit = here if hit is None else hit & here
        return hit

    def carried(*refs):
        ins, rins = refs[:ni], refs[ni:ni + ri]
        outs, routs = refs[ni + ri:ni + ri + no], refs[ni + ri + no:ni + ri + no + ro]
        sems = refs[ni + ri + no + ro:]

        @pl.when(at_step(lambda n: 0))
        def _():
            ride.start(rins, routs, sems)

        body(*ins, *outs)

        @pl.when(at_step(lambda n: n - 1))
        def _():
            ride.finish(rins, routs, sems)

    res = pl.pallas_call(
        carried, name=name, grid=grid, in_specs=in_specs + _hbm_specs(ri), out_specs=out_specs + _hbm_specs(ro),
        out_shape=out_shape + ride.out_shape, scratch_shapes=ride.sems, compiler_params=compiler_params,
    )(*args, *ride.ins)
    return list(res[:no]), list(res[no:])


def _sum_and_join(sums, got, pos, name):
    n = len(sums)
    _, h, w = sums[0].shape
    cr = _chunk_rows(h, w, limit=(2 << 20) // n)

    def body(pos_ref, *refs):
        mine, theirs, outs = refs[:n], refs[n:2 * n], refs[2 * n:3 * n]
        ebuf, rbuf, send_sems, recv_sems = refs[3 * n:]
        x, y, c, _ = _mesh_pos()
        slot = pl.program_id(0) % 2
        cps = []
        for t in range(n):
            e = mine[t][...]
            for j in range(3):
                e = e + theirs[t][j].astype(F32)
            ebuf[t, slot] = e
            cp = pltpu.make_async_remote_copy(
                src_ref=ebuf.at[t, slot], dst_ref=rbuf.at[t, slot], send_sem=send_sems.at[2 * t + slot],
                recv_sem=recv_sems.at[2 * t + slot], device_id=(x, y, 1 - c), device_id_type=MESH)
            cp.start()
            outs[t][pos_ref[1]] = e
            cps.append(cp)
        for t, cp in enumerate(cps):
            cp.wait_recv()
            outs[t][1 - pos_ref[1]] = rbuf[t, slot]
        for cp in cps:
            cp.wait_send()

    grid_spec = pltpu.PrefetchScalarGridSpec(
        num_scalar_prefetch=1, grid=(h // cr,),
        in_specs=[pl.BlockSpec((None, cr, w), lambda i, p: (p[0], i, 0))] * n +
                 [pl.BlockSpec((3, cr, w), lambda i, p: (0, i, 0))] * n,
        out_specs=[pl.BlockSpec((2, cr, w), lambda i, p: (0, i, 0))] * n,
        scratch_shapes=[pltpu.VMEM((n, 2, cr, w), F32), pltpu.VMEM((n, 2, cr, w), F32),
                        pltpu.SemaphoreType.DMA((2 * n,)), pltpu.SemaphoreType.DMA((2 * n,))])
    return pl.pallas_call(
        body, name=name, grid_spec=grid_spec, compiler_params=_params(),
        out_shape=[jax.ShapeDtypeStruct((2, h, w), F32)] * n,
    )(pos, *sums, *got)


def _pair_sums(parts, pos, tag):
    pairs = [_pair_sum(p.reshape(8, p.shape[0] // 8, p.shape[1]), pos, f"rs_pair_{tag}_{t}")
             for t, p in enumerate(parts)]
    return [s for s, _ in pairs], [sb for _, sb in pairs]


def _joins(sums, got, pos, tag):
    full = _sum_and_join(sums, got, pos, f"rs_join_{tag}")
    return [f.reshape(2 * f.shape[1], f.shape[2]) for f in full]


def _ffn_fwd(x, prm, w13g, w2g, t, name, tm=512, ride=None):
    L, D = x.shape
    Fh = w13g.shape[-1]
    tm = min(tm, L)

    def body(x_ref, p_ref, w13_ref, w2_ref, xo_ref, sv_ref, g_ref, y_ref):
        xv = x_ref[...]
        hn, _, _, _ = _nm(xv, p_ref[3:4, :], p_ref[0:1, :], p_ref[1:2, :])
        hb = hn.astype(BF16)
        acc = jnp.zeros((tm, D), F32)
        for j in range(2):
            a = _dot(hb, w13_ref[j])
            b = _dot(hb, w13_ref[2 + j])
            sg = _sigmoid(a)
            sa = a * sg
            sv_ref[:, j * Fh:(j + 1) * Fh] = sa.astype(BF16)
            sv_ref[:, (2 + j) * Fh:(3 + j) * Fh] = (b * (sg * (1.0 + a * (1.0 - sg)))).astype(BF16)
            g = (sa * b).astype(BF16)
            g_ref[:, j * Fh:(j + 1) * Fh] = g
            acc = acc + _dot(g, w2_ref[j * Fh:(j + 1) * Fh, :])
        y_ref[...] = acc.astype(BF16)
        xo_ref[...] = xv + (0.5 * p_ref[2:3, :]) * acc

    res, carried = _ride_call(
        body, ride, (x, prm, w13g, w2g), name=name, grid=(L // tm,),
        in_specs=[_tile(tm, D), _full((8, D)),
                  _resident((None, 4, D, Fh), lambda i: (t, 0, 0, 0)),
                  _resident((None, 2 * Fh, D), lambda i: (t, 0, 0))],
        out_specs=[_tile(tm, D), _tile(tm, 4 * Fh), _tile(tm, 2 * Fh), _tile(tm, D)],
        out_shape=[jax.ShapeDtypeStruct((L, D), F32), jax.ShapeDtypeStruct((L, 4 * Fh), BF16),
                   jax.ShapeDtypeStruct((L, 2 * Fh), BF16), jax.ShapeDtypeStruct((L, D), BF16)],
        compiler_params=_params())
    xo, sv, g, y = res
    return xo, (sv, g), y, carried


def _head_grad(xo, tgt, fg):
    D = xo.shape[-1]
    r = lax.rsqrt(jnp.mean(xo * xo, axis=-1, keepdims=True) + RMS_EPS)
    xhat = xo * r
    err = xhat * fg - tgt
    loss = 0.5 * jnp.sum(jnp.mean(err * err, axis=-1, keepdims=True), axis=0, keepdims=True)
    dy = err * (1.0 / D)
    dxh = dy * fg
    return r * (dxh - xhat * jnp.mean(dxh * xhat, axis=-1, keepdims=True)), _sum0(dy * xhat), loss


def _ffn_bwd(dout, x, sv, y, prm, w13g, w2g, t, name, tm=256, ride=None, head=None):
    L, D = x.shape
    Fh = w13g.shape[-1]
    tm = min(tm, L)

    def body(do_ref, x_ref, sv_ref, y_ref, p_ref, w13_ref, w2_ref, *rest):
        dx_ref, dab_ref, dy_ref, hn_ref, acc_ref = rest[-5:]
        i = pl.program_id(0)
        head_rows = []
        if head is None:
            do = do_ref[...]
        else:
            do, dfg, loss = _head_grad(do_ref[...], rest[0][...], rest[1][...])
            head_rows = [dfg, jnp.broadcast_to(loss, (1, D))]
        gain, shift, scale, gate = p_ref[3:4, :], p_ref[0:1, :], p_ref[1:2, :], p_ref[2:3, :]
        hn, xhat, r, nrm = _nm(x_ref[...], gain, shift, scale)
        hn_ref[...] = hn.astype(BF16)
        dgate = 0.5 * _sum0(do * y_ref[...].astype(F32))
        dyb = ((0.5 * gate) * do).astype(BF16)
        dy_ref[...] = dyb
        dhn = jnp.zeros((tm, D), F32)
        for j in range(2):
            dg = _dot_nt(dyb, w2_ref[j * Fh:(j + 1) * Fh, :])
            da = (dg * sv_ref[:, (2 + j) * Fh:(3 + j) * Fh].astype(F32)).astype(BF16)
            db = (dg * sv_ref[:, j * Fh:(j + 1) * Fh].astype(F32)).astype(BF16)
            dab_ref[:, j * Fh:(j + 1) * Fh] = da
            dab_ref[:, (2 + j) * Fh:(3 + j) * Fh] = db
            dhn = dhn + _dot_nt(da, w13_ref[j]) + _dot_nt(db, w13_ref[2 + j])
        dx, dshift, dscale, dgn = _nm_bwd(dhn, xhat, r, nrm, gain, scale)
        dx_ref[...] = do + dx
        _acc_rows(acc_ref, i == 0, [dshift, dscale, dgate, dgn] + head_rows)

    head_args = () if head is None else head
    head_specs = [] if head is None else [_tile(tm, D), _full((1, D))]
    res, carried = _ride_call(
        body, ride, (dout, x, sv, y, prm, w13g, w2g, *head_args), name=name, grid=(L // tm,),
        in_specs=[_tile(tm, D), _tile(tm, D), _tile(tm, 4 * Fh), _tile(tm, D), _full((8, D)),
                  _resident((None, 4, D, Fh), lambda i: (t, 0, 0, 0)),
                  _resident((None, 2 * Fh, D), lambda i: (t, 0, 0))] + head_specs,
        out_specs=[_tile(tm, D), _tile(tm, 4 * Fh), _tile(tm, D), _tile(tm, D), _full((8, D))],
        out_shape=[jax.ShapeDtypeStruct((L, D), F32), jax.ShapeDtypeStruct((L, 4 * Fh), BF16),
                   jax.ShapeDtypeStruct((L, D), BF16), jax.ShapeDtypeStruct((L, D), BF16),
                   jax.ShapeDtypeStruct((8, D), F32)],
        compiler_params=_params())
    return (*res, carried)


def _mm_tn(a, b, slabs, a_slabbed, name, init=None, tl=1024, ride=None):
    L = a.shape[0]
    ka = a.shape[1] // slabs if a_slabbed else a.shape[1]
    nb = b.shape[1] if a_slabbed else b.shape[1] // slabs
    tl = min(tl, L)
    has_init = init is not None

    def body(a_ref, b_ref, *rest):
        o_ref = rest[-1]
        step = pl.program_id(1)

        @pl.when(step == 0)
        def _():
            o_ref[...] = rest[0][...] if has_init else jnp.zeros((ka, nb), F32)

        o_ref[...] += _dot_tn(a_ref[...], b_ref[...])

    in_specs = [pl.BlockSpec((tl, ka), (lambda s, l: (l, s)) if a_slabbed else (lambda s, l: (l, 0))),
                pl.BlockSpec((tl, nb), (lambda s, l: (l, 0)) if a_slabbed else (lambda s, l: (l, s)))]
    args = [a, b]
    if has_init:
        in_specs.append(pl.BlockSpec((ka, nb), lambda s, l: (s, 0)))
        args.append(init)
    res, carried = _ride_call(
        body, ride, args, name=name, grid=(slabs, L // tl), in_specs=in_specs,
        out_specs=[pl.BlockSpec((ka, nb), lambda s, l: (s, 0))],
        out_shape=[jax.ShapeDtypeStruct((slabs * ka, nb), F32)], compiler_params=_params())
    return res[0] if ride is None else (res[0], carried)


def _even_in_fwd(x, prm, wing, name, tm=512):
    L, D = x.shape
    W = wing.shape[-1]
    tm = min(tm, L)

    def body(x_ref, p_ref, w_ref, q_ref, k_ref, v_ref, u_ref, hn_ref):
        hn, _, _, _ = _nm(x_ref[...], p_ref[3:4, :], p_ref[0:1, :], p_ref[1:2, :])
        hb = hn.astype(BF16)
        hn_ref[...] = hb
        q_ref[...] = _dot(hb, w_ref[0]).astype(BF16)
        k_ref[...] = _dot(hb, w_ref[1]).astype(BF16)
        v_ref[...] = _dot(hb, w_ref[2]).astype(BF16)
        u_ref[...] = _dot(hb, w_ref[3])

    return pl.pallas_call(
        body, name=name, grid=(L // tm,),
        in_specs=[_tile(tm, D), _full((8, D)), _resident((None, 4, D, W), lambda i: (0, 0, 0, 0))],
        out_specs=[_tile(tm, W)] * 4 + [_tile(tm, D)],
        out_shape=[jax.ShapeDtypeStruct((L, W), BF16)] * 3 + [jax.ShapeDtypeStruct((L, W), F32),
                                                              jax.ShapeDtypeStruct((L, D), BF16)],
        compiler_params=_params())(x, prm, wing)


def _even_in_bwd(dout, x, dq, dk, dv, du, prm, wing, name, tm=512):
    L, D = x.shape
    W = wing.shape[-1]
    tm = min(tm, L)

    def body(do_ref, x_ref, dq_ref, dk_ref, dv_ref, du_ref, p_ref, w_ref, dx_ref, ds_ref, acc_ref):
        i = pl.program_id(0)
        gain, shift, scale = p_ref[3:4, :], p_ref[0:1, :], p_ref[1:2, :]
        _, xhat, r, nrm = _nm(x_ref[...], gain, shift, scale)
        dhn = jnp.zeros((tm, D), F32)
        for s, ref in enumerate((dq_ref, dk_ref, dv_ref, du_ref)):
            d = ref[...].astype(BF16)
            ds_ref[:, s * W:(s + 1) * W] = d
            dhn = dhn + _dot_nt(d, w_ref[s])
        dx, dshift, dscale, dgn = _nm_bwd(dhn, xhat, r, nrm, gain, scale)
        dx_ref[...] = do_ref[...] + dx
        _acc_rows(acc_ref, i == 0, [dshift, dscale, None, dgn])

    return pl.pallas_call(
        body, name=name, grid=(L // tm,),
        in_specs=[_tile(tm, D), _tile(tm, D)] + [_tile(tm, W)] * 4 +
                 [_full((8, D)), _resident((None, 4, D, W), lambda i: (0, 0, 0, 0))],
        out_specs=[_tile(tm, D), _tile(tm, 4 * W), _full((8, D))],
        out_shape=[jax.ShapeDtypeStruct((L, D), F32), jax.ShapeDtypeStruct((L, 4 * W), BF16),
                   jax.ShapeDtypeStruct((8, D), F32)],
        compiler_params=_params())(dout, x, dq, dk, dv, du, prm, wing)


def _even_out_fwd(x, att, pool, prm, woutg, name, tm=512):
    L, D = x.shape
    W = D // 2
    tm = min(tm, L)

    def body(x_ref, a_ref, p_ref, prm_ref, w_ref, xo_ref, y_ref):
        yv = _dot(a_ref[...], w_ref[0:W, :]) + _dot(p_ref[...], w_ref[W:2 * W, :])
        y_ref[...] = yv.astype(BF16)
        xo_ref[...] = x_ref[...] + prm_ref[2:3, :] * yv

    return pl.pallas_call(
        body, name=name, grid=(L // tm,),
        in_specs=[_tile(tm, D), _tile(tm, W), _tile(tm, W), _full((8, D)),
                  _resident((None, D, D), lambda i: (0, 0, 0))],
        out_specs=[_tile(tm, D), _tile(tm, D)],
        out_shape=[jax.ShapeDtypeStruct((L, D), F32), jax.ShapeDtypeStruct((L, D), BF16)],
        compiler_params=_params())(x, att, pool, prm, woutg)


def _even_out_bwd(dout, y, prm, woutg, name, tm=512):
    L, D = dout.shape
    W = D // 2
    tm = min(tm, L)

    def body(do_ref, y_ref, p_ref, w_ref, dy_ref, da_ref, dp_ref, acc_ref):
        i = pl.program_id(0)
        do = do_ref[...]
        dgate = _sum0(do * y_ref[...].astype(F32))
        dyb = (p_ref[2:3, :] * do).astype(BF16)
        dy_ref[...] = dyb
        da_ref[...] = _dot_nt(dyb, w_ref[0:W, :]).astype(BF16)
        dp_ref[...] = _dot_nt(dyb, w_ref[W:2 * W, :])
        _acc_rows(acc_ref, i == 0, [None, None, dgate])

    return pl.pallas_call(
        body, name=name, grid=(L // tm,),
        in_specs=[_tile(tm, D), _tile(tm, D), _full((8, D)), _resident((None, D, D), lambda i: (0, 0, 0))],
        out_specs=[_tile(tm, D), _tile(tm, W), _tile(tm, W), _full((8, D))],
        out_shape=[jax.ShapeDtypeStruct((L, D), BF16), jax.ShapeDtypeStruct((L, W), BF16),
                   jax.ShapeDtypeStruct((L, W), F32), jax.ShapeDtypeStruct((8, D), F32)],
        compiler_params=_params())(dout, y, prm, woutg)


def _group_ri(variant, qr, kr):
    first_key = (0, qr, GK - NA_KH)[variant]
    if not first_key <= kr < first_key + NA_KH:
        return None
    return kr - qr + (NA_KH - 1, NA_KH - 1 - NA_KH // 2, NA_KH - 1 - (GK - GQ))[variant]


HEADS_PER_BLOCK = 128 // NA_HEAD_DIM


def _bias_table(rpb, name):
    H = rpb.shape[0]
    hpb = HEADS_PER_BLOCK
    nri, nci = 2 * NA_KH - 1, 2 * NA_KW - 1
    col = jnp.arange(GRID_W)
    rel = (col[None, :] - col[:, None] + (NA_KW - 1)).reshape(1, -1)
    onehot = (rel == jnp.arange(32)[:, None]).astype(F32)
    cs = jnp.clip(col - NA_KW // 2, 0, GRID_W - NA_KW)
    ok = ((col[None, :] >= cs[:, None]) & (col[None, :] < cs[:, None] + NA_KW)).astype(F32).reshape(1, -1)
    by_lane_block = rpb.reshape(H // hpb, hpb, nri, nci).transpose(1, 0, 2, 3)
    rpb2 = jnp.pad(by_lane_block.reshape(H * nri, nci), ((0, 0), (0, 32 - nci)))

    def body(r_ref, e_ref, m_ref, o_ref):
        t = jnp.dot(r_ref[...], e_ref[...], preferred_element_type=F32, precision=lax.Precision.HIGHEST)
        o_ref[...] = jnp.where(m_ref[...] > 0.0, t, NEG_INF)

    tab = pl.pallas_call(body, name=name, out_shape=jax.ShapeDtypeStruct((H * nri, GRID_W * GRID_W), F32))(
        rpb2, onehot, ok)
    tab = tab.reshape(hpb, H // hpb, nri, GRID_W, GRID_W)
    outside = jnp.full((H // hpb, GRID_W, GRID_W), NEG_INF, F32)
    variants = []
    for variant in range(3):
        rows = []
        for h in range(hpb):
            for qr in range(GQ):
                ris = [_group_ri(variant, qr, kr) for kr in range(GK)]
                rows.append(jnp.concatenate([outside if ri is None else tab[h, :, ri] for ri in ris], axis=2))
        variants.append(jnp.concatenate(rows, axis=1))
    return jnp.stack(variants, axis=1)


ATTN_SCALE = NA_HEAD_DIM ** -0.5
assert ATTN_SCALE == 0.125


def _attn_probs(q, kw, kc, bias):
    s_w = _dot_nt(q, kw) + bias
    s_c = _dot_nt(q, kc)
    m = jnp.maximum(jnp.max(s_w, axis=-1, keepdims=True), jnp.max(s_c, axis=-1, keepdims=True))
    e_w = jnp.exp(s_w - m)
    e_c = jnp.exp(s_c - m)
    inv = 1.0 / (jnp.sum(e_w, axis=-1, keepdims=True) + jnp.sum(e_c, axis=-1, keepdims=True))
    return e_w * inv, e_c * inv


def _group_place(g, R):
    G = R // GQ
    kb = jnp.clip(g * GQ - NA_KH // 2, 0, R - GK)
    variant = jnp.where(g == 0, 0, jnp.where(g == G - 1, 2, 1))
    return pl.multiple_of(g * (GQ * GRID_W), GQ * GRID_W), pl.multiple_of(kb * GRID_W, GRID_W), variant


def _lane_masks(width, dh):
    lane = lax.broadcasted_iota(jnp.int32, (1, width), 1)
    return [(lane >= h * dh) & (lane < (h + 1) * dh) for h in range(width // dh)]


def _only(mask, a):
    return jnp.where(mask, a, jnp.zeros_like(a))


def _attn_fwd(q, k, v, kc, vc, bias, name, ride=None):
    L, width = q.shape
    C = kc.shape[0]
    dh = NA_HEAD_DIM
    lanes = 128
    hpb = lanes // dh
    R = L // GRID_W
    nq, nk = GQ * GRID_W, GK * GRID_W
    scale = dh ** -0.5

    def body(q_ref, k_ref, v_ref, kc_ref, vc_ref, b_ref, o_ref):
        masks = _lane_masks(lanes, dh)
        kc2 = kc_ref[...]
        vcs = [_only(m, vc_ref[...]) for m in masks]

        def group(g, carry):
            q0, k0, variant = _group_place(g, R)
            q2 = q_ref[pl.ds(q0, nq), :]
            k2 = k_ref[pl.ds(k0, nk), :]
            v2 = v_ref[pl.ds(k0, nk), :]
            qs = jnp.concatenate([_only(m, q2) for m in masks], axis=0) * scale
            p_w, p_c = _attn_probs(qs, k2, kc2, b_ref[variant])
            p_w, p_c = p_w.astype(BF16), p_c.astype(BF16)
            o2 = jnp.zeros((nq, lanes), F32)
            for h, m in enumerate(masks):
                rows = slice(h * nq, (h + 1) * nq)
                o2 = o2 + _dot(p_w[rows], _only(m, v2)) + _dot(p_c[rows], vcs[h])
            o_ref[pl.ds(q0, nq), :] = o2.astype(BF16)
            return carry

        lax.fori_loop(0, R // GQ, group, 0)

    cols = lambda n: pl.BlockSpec((n, lanes), lambda p: (0, p))
    res, carried = _ride_call(
        body, ride, (q, k, v, kc, vc, bias), name=name, grid=(width // lanes,),
        in_specs=[cols(L), cols(L), cols(L), cols(C), cols(C),
                  pl.BlockSpec((None, 3, hpb * nq, nk), lambda p: (p, 0, 0, 0))],
        out_specs=[cols(L)], out_shape=[jax.ShapeDtypeStruct((L, width), BF16)],
        compiler_params=_params())
    return res[0], carried


def _attn_bwd(q, k, v, kc, vc, bias, do, name, ride=None):
    L, width = q.shape
    C = kc.shape[0]
    dh = NA_HEAD_DIM
    lanes = 128
    hpb = lanes // dh
    R = L // GRID_W
    nq, nk = GQ * GRID_W, GK * GRID_W
    scale = dh ** -0.5

    def body(q_ref, k_ref, v_ref, kc_ref, vc_ref, b_ref, do_ref, dq_ref, dk_ref, dv_ref, dkc_ref, dvc_ref, db_ref):
        masks = _lane_masks(lanes, dh)
        kc2 = kc_ref[...]
        vc2 = vc_ref[...]
        kcs = [_only(m, kc2) * scale for m in masks]
        dk_ref[...] = jnp.zeros((L, lanes), F32)
        dv_ref[...] = jnp.zeros((L, lanes), F32)
        dkc_ref[...] = jnp.zeros((C, lanes), F32)
        dvc_ref[...] = jnp.zeros((C, lanes), F32)
        db_ref[...] = jnp.zeros((3, hpb * nq, nk), F32)

        def group(g, carry):
            q0, k0, variant = _group_place(g, R)
            q2 = q_ref[pl.ds(q0, nq), :]
            k2 = k_ref[pl.ds(k0, nk), :]
            v2 = v_ref[pl.ds(k0, nk), :]
            do2 = do_ref[pl.ds(q0, nq), :]
            qs = jnp.concatenate([_only(m, q2) for m in masks], axis=0) * scale
            dos = jnp.concatenate([_only(m, do2) for m in masks], axis=0)
            p_w, p_c = _attn_probs(qs, k2, kc2, b_ref[variant])
            dp_w = _dot_nt(dos, v2)
            dp_c = _dot_nt(dos, vc2)
            delta = jnp.sum(p_w * dp_w, axis=-1, keepdims=True) + jnp.sum(p_c * dp_c, axis=-1, keepdims=True)
            ds_w = p_w * (dp_w - delta)
            ds_c = p_c * (dp_c - delta)
            db_ref[variant] += ds_w
            dsw = ds_w.astype(BF16)
            dsc = ds_c.astype(BF16)
            dq2 = jnp.zeros((nq, lanes), F32)
            for h, m in enumerate(masks):
                rows = slice(h * nq, (h + 1) * nq)
                dq2 = dq2 + _dot(dsw[rows], _only(m, k2) * scale) + _dot(dsc[rows], kcs[h])
            dq_ref[pl.ds(q0, nq), :] = dq2.astype(BF16)
            dk_ref[pl.ds(k0, nk), :] += _dot_tn(dsw, qs)
            dv_ref[pl.ds(k0, nk), :] += _dot_tn(p_w.astype(BF16), dos)
            dkc_ref[...] += _dot_tn(dsc, qs)
            dvc_ref[...] += _dot_tn(p_c.astype(BF16), dos)
            return carry

        lax.fori_loop(0, R // GQ, group, 0)

    cols = lambda n: _resident((n, lanes), lambda p: (0, p))
    bspec = _resident((None, 3, hpb * nq, nk), lambda p: (p, 0, 0, 0))
    res, carried = _ride_call(
        body, ride, (q, k, v, kc, vc, bias, do), name=name, grid=(width // lanes,),
        in_specs=[cols(L), cols(L), cols(L), cols(C), cols(C), bspec, cols(L)],
        out_specs=[cols(L), cols(L), cols(L), cols(C), cols(C), bspec],
        out_shape=[jax.ShapeDtypeStruct((L, width), BF16)] + [jax.ShapeDtypeStruct((L, width), F32)] * 2 +
                  [jax.ShapeDtypeStruct((C, width), F32)] * 2 +
                  [jax.ShapeDtypeStruct((width // lanes, 3, hpb * nq, nk), F32)],
        compiler_params=_params())
    return (*res, carried)


def _rpb_grad(dbias, name):
    hpb = HEADS_PER_BLOCK
    H = dbias.shape[0] * hpb
    nri, nci = 2 * NA_KH - 1, 2 * NA_KW - 1
    d6 = dbias.reshape(H // hpb, 3, hpb, GQ, GRID_W, GK, GRID_W).transpose(0, 2, 1, 3, 5, 4, 6)
    d6 = d6.reshape(H, 3, GQ, GK, GRID_W, GRID_W)
    col = jnp.arange(GRID_W)
    onehot = (col[None, None, :] - col[None, :, None] + (NA_KW - 1) == jnp.arange(32)[:, None, None]).astype(F32)
    places = [(v, qr, kr) for v in range(3) for qr in range(GQ) for kr in range(GK)]

    def body(d_ref, m_ref, o_ref, t_ref):
        t_ref[...] = jnp.zeros((32, GRID_W), F32)
        o_ref[...] = jnp.zeros((16, 32, 128), F32)
        for ri in range(nri):
            a = None
            for place in places:
                if _group_ri(*place) == ri:
                    blk = d_ref[place]
                    a = blk if a is None else a + blk
            for ci in range(nci):
                t_ref[ci:ci + 1, :] = _sum0(a * m_ref[ci])
            o_ref[ri] = jnp.broadcast_to(jnp.sum(t_ref[...], axis=1, keepdims=True), (32, 128))

    out = pl.pallas_call(
        body, name=name, grid=(H,),
        in_specs=[pl.BlockSpec((None, 3, GQ, GK, GRID_W, GRID_W), lambda h: (h, 0, 0, 0, 0, 0)),
                  pl.BlockSpec((32, GRID_W, GRID_W), lambda h: (0, 0, 0))],
        out_specs=pl.BlockSpec((None, 16, 32, 128), lambda h: (h, 0, 0, 0)),
        out_shape=jax.ShapeDtypeStruct((H, 16, 32, 128), F32),
        scratch_shapes=[pltpu.VMEM((32, GRID_W), F32)])(d6, onehot)
    return out[:, :nri, :nci, 0]


def _window_count(t, w, L):
    lo = jnp.clip(t - w // 2, 0, L)
    hi = jnp.clip(t - w // 2 + w, 0, L)
    return jnp.maximum(hi - lo, 1).astype(F32)


def _running_sum(v, w):
    k = 1
    while k < w:
        v = v + _shift_rows(v, k)
        k *= 2
    return v


def _pool_fwd(u, poolw, pscale, name, tm=512):
    L, W = u.shape
    G = POOL_GROUP_DIM
    tm = min(tm, L)
    nt = L // tm

    def body(c_ref, p_ref, n_ref, w_ref, s_ref, o_ref, dm_ref):
        i = pl.program_id(0)
        ext = _ext(p_ref[...], c_ref[...], n_ref[...], i, nt)
        t = i * tm + lax.broadcasted_iota(jnp.int32, (tm, 1), 0)
        for g, w in enumerate(POOL_WINDOWS):
            e = ext[:, g * G:(g + 1) * G]
            win = _shift_rows(_running_sum(e, w), -(w // 2 - 1))[HALO:HALO + tm]
            dmx = (win / _window_count(t, w, L) - e[HALO:HALO + tm]).astype(BF16)
            dm_ref[:, g * G:(g + 1) * G] = dmx
            o_ref[:, g * G:(g + 1) * G] = (_dot(dmx, w_ref[g]) * s_ref[:, g * G:(g + 1) * G]).astype(BF16)

    return pl.pallas_call(
        body, name=name, grid=(nt,),
        in_specs=[_tile(tm, W), _halo_prev(tm, W), _halo_next(tm, W, L), _full((4, G, G)), _full((1, W))],
        out_specs=[_tile(tm, W), _tile(tm, W)],
        out_shape=[jax.ShapeDtypeStruct((L, W), BF16)] * 2, compiler_params=_params())(u, u, u, poolw, pscale)


def _pool_bwd(dpool, dmx, poolw, pscale, name, tm=512):
    L, W = dpool.shape
    G = POOL_GROUP_DIM
    tm = min(tm, L)
    nt = L // tm

    def body(c_ref, p_ref, n_ref, dm_ref, w_ref, s_ref, du_ref, dw_ref, acc_ref):
        i = pl.program_id(0)
        ext = _ext(p_ref[...], c_ref[...], n_ref[...], i, nt)
        te = i * tm - HALO + lax.broadcasted_iota(jnp.int32, (tm + 2 * HALO, 1), 0)

        @pl.when(i == 0)
        def _():
            dw_ref[...] = jnp.zeros((4 * G, G), F32)

        rows = []
        for g, w in enumerate(POOL_WINDOWS):
            sc = s_ref[:, g * G:(g + 1) * G]
            dpre = (ext[:, g * G:(g + 1) * G] * sc).astype(BF16)
            dd = _dot_nt(dpre, w_ref[g])
            spread = _shift_rows(_running_sum(dd / _window_count(te, w, L), w), -(w // 2))
            du_ref[:, g * G:(g + 1) * G] = (spread - dd)[HALO:HALO + tm]
            dmx_g = dm_ref[:, g * G:(g + 1) * G]
            rows.append(_sum0(c_ref[:, g * G:(g + 1) * G] * _dot(dmx_g, w_ref[g])))
            dw_ref[g * G:(g + 1) * G, :] += _dot_tn(dmx_g, dpre[HALO:HALO + tm])
        _acc_rows(acc_ref, i == 0, [jnp.concatenate(rows, axis=1)])

    return pl.pallas_call(
        body, name=name, grid=(nt,),
        in_specs=[_tile(tm, W), _halo_prev(tm, W), _halo_next(tm, W, L), _tile(tm, W), _full((4, G, G)),
                  _full((1, W))],
        out_specs=[_tile(tm, W), _full((4 * G, G)), _full((8, W))],
        out_shape=[jax.ShapeDtypeStruct((L, W), F32), jax.ShapeDtypeStruct((4 * G, G), F32),
                   jax.ShapeDtypeStruct((8, W), F32)],
        compiler_params=_params())(dpool, dpool, dpool, dmx, poolw, pscale)


def _conv3(z, cw):
    return _shift_rows(z, 1) * cw[0] + z * cw[1] + _shift_rows(z, -1) * cw[2]


def _conv_fwd(x, prm, wing, woutg, name, tm=512):
    L, D = x.shape
    Ws = wing.shape[-1]
    tm = min(tm, L)
    nt = L // tm
    te = tm + 2 * HALO

    def body(c_ref, p_ref, n_ref, prm_ref, wi_ref, wo_ref, xo_ref, y_ref, b_ref):
        i = pl.program_id(0)
        xe = jnp.concatenate([p_ref[...], c_ref[...], n_ref[...]], axis=0)
        hn, _, _, _ = _nm(xe, prm_ref[3:4, :], prm_ref[0:1, :], prm_ref[1:2, :])
        hb = hn.astype(BF16)
        proj = jnp.concatenate([_dot(hb, wi_ref[s]) for s in range(4)], axis=1)
        bg, cg, xin = proj[:, :D], proj[:, D:2 * D], proj[:, 2 * D:]
        tpos = i * tm - HALO + lax.broadcasted_iota(jnp.int32, (te, 1), 0)
        valid = ((tpos >= 0) & (tpos < L)).astype(F32)
        yc = _conv3(cg * xin * valid, [prm_ref[4 + k:5 + k, :] for k in range(3)])
        h2 = (bg * yc)[HALO:HALO + tm].astype(BF16)
        yv = _dot(h2, wo_ref[...])
        y_ref[...] = yv.astype(BF16)
        xo_ref[...] = c_ref[...] + prm_ref[2:3, :] * yv
        b_ref[...] = proj[HALO:HALO + tm].astype(BF16)

    return pl.pallas_call(
        body, name=name, grid=(nt,),
        in_specs=[_tile(tm, D), _halo_prev(tm, D), _halo_next(tm, D, L), _full((8, D)),
                  _resident((None, 4, D, Ws), lambda i: (0, 0, 0, 0)),
                  _resident((None, D, D), lambda i: (0, 0, 0))],
        out_specs=[_tile(tm, D), _tile(tm, D), _tile(tm, 3 * D)],
        out_shape=[jax.ShapeDtypeStruct((L, D), F32), jax.ShapeDtypeStruct((L, D), BF16),
                   jax.ShapeDtypeStruct((L, 3 * D), BF16)],
        compiler_params=_params())(x, x, x, prm, wing, woutg)


def _conv_bwd(dout, x, y, bcx, prm, wing, woutg, name, tm=256, ride=None):
    L, D = x.shape
    Ws = wing.shape[-1]
    tm = min(tm, L)
    nt = L // tm
    te = tm + 2 * HALO

    def body(dc_ref, dp_ref, dn_ref, x_ref, y_ref, bc_ref, bp_ref, bn_ref, prm_ref, wi_ref, wo_ref,
             dx_ref, dpr_ref, h2_ref, dy_ref, hn_ref, acc_ref):
        i = pl.program_id(0)
        gain, shift, scale, gate = prm_ref[3:4, :], prm_ref[0:1, :], prm_ref[1:2, :], prm_ref[2:3, :]
        taps = [prm_ref[4 + k:5 + k, :] for k in range(3)]
        do = dc_ref[...]
        doe = _ext(dp_ref[...], do, dn_ref[...], i, nt)
        dye = (gate * doe).astype(BF16)
        dy_ref[...] = dye[HALO:HALO + tm]
        dh2 = _dot_nt(dye, wo_ref[...])
        be = jnp.concatenate([bp_ref[...], bc_ref[...], bn_ref[...]], axis=0).astype(F32)
        bg, cg, xin = be[:, :D], be[:, D:2 * D], be[:, 2 * D:]
        tpos = i * tm - HALO + lax.broadcasted_iota(jnp.int32, (te, 1), 0)
        valid = ((tpos >= 0) & (tpos < L)).astype(F32)
        z = cg * xin * valid
        yc = _conv3(z, taps)
        dyc = dh2 * bg
        h2_ref[...] = (bg * yc)[HALO:HALO + tm].astype(BF16)
        dz = _conv3(dyc, taps[::-1])
        dproj = jnp.concatenate([dh2 * yc, dz * xin, dz * cg], axis=1)[HALO:HALO + tm].astype(BF16)
        dpr_ref[...] = dproj
        dhn = jnp.zeros((tm, D), F32)
        for s in range(4):
            dhn = dhn + _dot_nt(dproj[:, s * Ws:(s + 1) * Ws], wi_ref[s])
        hn, xhat, r, nrm = _nm(x_ref[...], gain, shift, scale)
        hn_ref[...] = hn.astype(BF16)
        dx, dshift, dscale, dgn = _nm_bwd(dhn, xhat, r, nrm, gain, scale)
        dx_ref[...] = do + dx
        dgate = _sum0(do * y_ref[...].astype(F32))
        dtaps = [_sum0((dyc * _shift_rows(z, 1 - k))[HALO:HALO + tm]) for k in range(3)]
        _acc_rows(acc_ref, i == 0, [dshift, dscale, dgate, dgn] + dtaps)

    res, carried = _ride_call(
        body, ride, (dout, dout, dout, x, y, bcx, bcx, bcx, prm, wing, woutg), name=name, grid=(nt,),
        in_specs=[_tile(tm, D), _halo_prev(tm, D), _halo_next(tm, D, L), _tile(tm, D), _tile(tm, D),
                  _tile(tm, 3 * D), _halo_prev(tm, 3 * D), _halo_next(tm, 3 * D, L), _full((8, D)),
                  _resident((None, 4, D, Ws), lambda i: (0, 0, 0, 0)),
                  _resident((None, D, D), lambda i: (0, 0, 0))],
        out_specs=[_tile(tm, D), _tile(tm, 3 * D), _tile(tm, D), _tile(tm, D), _tile(tm, D), _full((8, D))],
        out_shape=[jax.ShapeDtypeStruct((L, D), F32), jax.ShapeDtypeStruct((L, 3 * D), BF16),
                   jax.ShapeDtypeStruct((L, D), BF16), jax.ShapeDtypeStruct((L, D), BF16),
                   jax.ShapeDtypeStruct((L, D), BF16), jax.ShapeDtypeStruct((8, D), F32)],
        compiler_params=_params())
    return (*res, carried)


def _mod_fwd(cond, mod_w, mod_b, name, tn=768):
    nl, D, N = mod_w.shape
    tn = min(tn, N)

    def body(c_ref, w_ref, b_ref, o_ref):
        cv = c_ref[...]
        s = (cv * _sigmoid(cv)).astype(BF16)
        o_ref[...] = _dot(s, w_ref[...].astype(BF16)) + b_ref[...]

    return pl.pallas_call(
        body, name=name, grid=(nl, N // tn),
        in_specs=[pl.BlockSpec((16, D), lambda l, j: (0, 0)), pl.BlockSpec((None, D, tn), lambda l, j: (l, 0, j)),
                  pl.BlockSpec((None, 1, tn), lambda l, j: (l, 0, j))],
        out_specs=pl.BlockSpec((None, 16, tn), lambda l, j: (l, 0, j)),
        out_shape=jax.ShapeDtypeStruct((nl, 16, N), F32), compiler_params=_params())(cond, mod_w, mod_b)


def _mod_bwd(cond, dm, mod_w, name, tn=768):
    nl, D, N = mod_w.shape
    tn = min(tn, N)

    def body(c_ref, d_ref, w_ref, dw_ref, dc_ref):
        first = (pl.program_id(0) == 0) & (pl.program_id(1) == 0)
        cv = c_ref[...]
        s = (cv * _sigmoid(cv)).astype(BF16)
        d = d_ref[...].astype(BF16)
        dw_ref[...] = _dot_tn(s, d)

        @pl.when(first)
        def _():
            dc_ref[...] = jnp.zeros((16, D), F32)

        dc_ref[...] += _dot_nt(d, w_ref[...].astype(BF16))

    return pl.pallas_call(
        body, name=name, grid=(nl, N // tn),
        in_specs=[pl.BlockSpec((16, D), lambda l, j: (0, 0)), pl.BlockSpec((None, 16, tn), lambda l, j: (l, 0, j)),
                  pl.BlockSpec((None, D, tn), lambda l, j: (l, 0, j))],
        out_specs=[pl.BlockSpec((None, D, tn), lambda l, j: (l, 0, j)), pl.BlockSpec((16, D), lambda l, j: (0, 0))],
        out_shape=[jax.ShapeDtypeStruct((nl, D, N), F32), jax.ShapeDtypeStruct((16, D), F32)],
        compiler_params=_params())(cond, dm, mod_w)


def _mod_small_grads(dm_all, cond, dsilu_parts, name):
    nl, _, N = dm_all.shape
    D = cond.shape[1]

    def body(d_ref, c_ref, p_ref, db_ref, dc_ref):
        for l in range(nl):
            db_ref[l] = _sum0(d_ref[l])
        tot = p_ref[0, 8:9, :]
        for k in range(1, N_CHIPS):
            tot = tot + p_ref[2 * k, 8:9, :]
        cv = c_ref[8:9, :]
        sg = _sigmoid(cv)
        dc_ref[...] = tot * (sg * (1.0 + cv * (1.0 - sg)))

    return pl.pallas_call(
        body, name=name, out_shape=[jax.ShapeDtypeStruct((nl, 1, N), F32), jax.ShapeDtypeStruct((1, D), F32)],
    )(dm_all, cond, dsilu_parts)


def _prm(rows, D):
    rows = [r.reshape(1, D) for r in rows]
    return jnp.concatenate(rows + [jnp.zeros((8 - len(rows), D), F32)], axis=0)


def kernel(x, c, ctx, c_ctx, mod_w, mod_b, norm_g, ffn_w13, ffn_w2, even_w_in, even_w_out, na_rpb, pool_w, pool_scale, conv_w_in, conv_w, conv_w_out, final_g, loss_target, m_c_ctx, m_mod_w, m_mod_b, m_norm_g, m_ffn_w13, m_ffn_w2, m_even_w_in, m_even_w_out, m_na_rpb, m_pool_w, m_pool_scale, m_conv_w_in, m_conv_w, m_conv_w_out, m_final_g, v_c_ctx, v_mod_w, v_mod_b, v_norm_g, v_ffn_w13, v_ffn_w2, v_even_w_in, v_even_w_out, v_na_rpb, v_pool_w, v_pool_scale, v_conv_w_in, v_conv_w, v_conv_w_out, v_final_g):
    xi, yi, ci = lax.axis_index("x"), lax.axis_index("y"), lax.axis_index("c")
    chip = 2 * xi + yi
    dev = 4 * xi + 2 * yi + ci
    _, L, D = x.shape
    C = ctx.shape[1]
    Ds = D // N_CHIPS
    Nm = mod_w.shape[-1]
    Fh = ffn_w13.shape[-1]
    Fq = ffn_w2.shape[2]
    assert ffn_w13.shape[:2] == (2, 2) and Fh == 2 * Fq and L % (GQ * GRID_W) == 0 and L // GRID_W >= GK and GQ == NA_KH // 2
    x0, ctx0, tgt = x[0], ctx[0], loss_target[0]

    pad = lambda a: jnp.pad(a, ((0, 0), (0, D - a.shape[1])))
    pack1 = jnp.concatenate([c, pad(norm_g.reshape(6, Ds)), pad(conv_w.reshape(3, Ds)), jnp.zeros((6, D), F32)], axis=0)
    g1 = _small_all_gather(pack1, "ag_cond")
    cond = jnp.concatenate([g1[:, 0], c_ctx[None], jnp.zeros((7, D), F32)], axis=0)
    norm_full = jnp.concatenate([g1[2 * k, 1:7, :Ds] for k in range(N_CHIPS)], axis=1).reshape(2, 3, D)
    convw_full = jnp.concatenate([g1[2 * k, 7:10, :Ds] for k in range(N_CHIPS)], axis=1)

    mod_b_loc = lax.dynamic_slice_in_dim(mod_b, chip * Nm, Nm, axis=1).reshape(2, 1, Nm)
    m_loc = _mod_fwd(cond, mod_w, mod_b_loc, "mod_fwd")
    g2 = _small_all_gather(m_loc.reshape(32, Nm), "ag_mod")
    m_all = jnp.concatenate([g2[2 * k] for k in range(N_CHIPS)], axis=1).reshape(2, 16, N_MOD, D)
    m_lat = lax.dynamic_index_in_dim(m_all, dev, axis=1, keepdims=False)
    m_ctx = m_all[:, 8]

    def prm(mods, layer, base, gain_idx, extra=()):
        return _prm([mods[layer, base], mods[layer, base + 1], mods[layer, base + 2], norm_full[layer, gain_idx],
                     *extra], D)

    def shard_bf16(w, name):
        return _cast_bf16(w.reshape(-1, w.shape[-1]), name).reshape(-1, *w.shape[-2:])

    w13s, w2s = shard_bf16(ffn_w13, "cast_w13"), shard_bf16(ffn_w2, "cast_w2")
    eins, eouts = shard_bf16(even_w_in, "cast_ein"), shard_bf16(even_w_out, "cast_eout")
    cins, couts = shard_bf16(conv_w_in, "cast_cin"), shard_bf16(conv_w_out, "cast_cout")
    ffn_shards = [[w13s[t:t + 1], w2s[t:t + 1]] for t in range(4)]

    def ffn_weights(w13g, w2g):
        return w13g.reshape(1, 4, D, Fh), w2g

    wf = [ffn_weights(*_gather_shards(ffn_shards[0], "ag_ffn0")), None, None, None]
    pos = jnp.stack([chip, ci]).astype(jnp.int32)

    p_f1 = prm(m_lat, 0, 0, 0)
    p_mx = prm(m_lat, 0, 3, 1)
    p_f2 = prm(m_lat, 0, 6, 2)
    p_g1 = prm(m_lat, 1, 0, 0)
    p_cv = prm(m_lat, 1, 3, 1, extra=(convw_full[0], convw_full[1], convw_full[2]))
    p_g2 = prm(m_lat, 1, 6, 2)
    pc_f1 = prm(m_ctx, 0, 0, 0)
    pc_mx = prm(m_ctx, 0, 3, 1)

    x1, ab1, y1, (eing, eoutg) = _ffn_fwd(x0, p_f1, *wf[0], 0, "ffn_fwd_l0a", ride=_broadcast_ride([eins, eouts]))
    eing = eing.reshape(1, 4, D, NA_WIDTH)
    ctx1, abc, yc, _ = _ffn_fwd(ctx0, pc_f1, *wf[0], 0, "ffn_fwd_ctx")
    q, k, v, u, hn_mx = _even_in_fwd(x1, p_mx, eing, "even_in_fwd")
    _, k_c, v_c, _, hn_cx = _even_in_fwd(ctx1, pc_mx, eing, "even_in_ctx")
    bias = _bias_table(na_rpb[0], "bias_table")
    att, gathered = _attn_fwd(q, k, v, k_c, v_c, bias, "attn_fwd", ride=_broadcast_ride(ffn_shards[1]))
    wf[1] = ffn_weights(*gathered)
    pw_b = _cast_bf16(pool_w.reshape(-1, POOL_GROUP_DIM), "cast_poolw").reshape(4, POOL_GROUP_DIM, POOL_GROUP_DIM)
    pool, dmx = _pool_fwd(u, pw_b, pool_scale, "pool_fwd")
    x2, ymx = _even_out_fwd(x1, att, pool, p_mx, eoutg, "even_out_fwd")
    x3, ab2, y2, gathered = _ffn_fwd(x2, p_f2, *wf[1], 0, "ffn_fwd_l0b",
                                     ride=_broadcast_ride(ffn_shards[2] + [cins, couts]))
    wf[2] = ffn_weights(*gathered[:2])
    cing, coutg = gathered[2].reshape(1, 4, D, conv_w_in.shape[-1]), gathered[3]
    x4, ab3, y3, gathered = _ffn_fwd(x3, p_g1, *wf[2], 0, "ffn_fwd_l1a", ride=_broadcast_ride(ffn_shards[3]))
    wf[3] = ffn_weights(*gathered)
    x5, ycv, bcx = _conv_fwd(x4, p_cv, cing, coutg, "conv_fwd")
    x6, ab4, y4, _ = _ffn_fwd(x5, p_g2, *wf[3], 0, "ffn_fwd_l1b")

    def ffn_back(dout, xin, ab, yy, p, t, tag, init13=None, init2=None, ride=None, head=None):
        sv, gact = ab
        dx, dab, dy, hn, acc, carried = _ffn_bwd(dout, xin, sv, yy, p, *wf[t], 0, f"ffn_bwd_{tag}", ride=ride,
                                                 head=head)
        dw13 = _mm_tn(hn, dab, 4, False, f"dw13_{tag}", init=init13)
        dw2 = _mm_tn(gact, dy, 2, True, f"dw2_{tag}", init=init2)
        return dx, acc, dw13, dw2, carried

    dx5, acc_g2, dw13_3, dw2_3, _ = ffn_back(x6, x5, ab4, y4, p_g2, 3, "l1b", head=(tgt, final_g.reshape(1, D)))
    acc_head = acc_g2[4:6]
    loss = lax.psum(acc_head[1, 0], ("x", "y", "c"))
    s_a, sb_a = _pair_sums([dw13_3, dw2_3], pos, "l1b")
    dx4, dproj, h2, dycv, hn_cv, acc_cv, got_a = _conv_bwd(dx5, x4, ycv, bcx, p_cv, cing, coutg, "conv_bwd",
                                                           ride=_scatter_ride(sb_a))
    dcin = _mm_tn(hn_cv, dproj, 4, False, "dw_cin")
    dcout = _mm_tn(h2, dycv, 1, False, "dw_cout")
    s_b, sb_b = _pair_sums([dcin, dcout], pos, "conv")
    dx3, acc_g1, dw13_2, dw2_2, got_b = ffn_back(dx4, x3, ab3, y3, p_g1, 2, "l1a", ride=_scatter_ride(sb_b))
    s_c, sb_c = _pair_sums([dw13_2, dw2_2], pos, "l1a")
    dx2, acc_f2, dw13_1, dw2_1, got_c = ffn_back(dx3, x2, ab2, y2, p_f2, 1, "l0b", ride=_scatter_ride(sb_c))

    dymx, datt, dpool, acc_mxo = _even_out_bwd(dx2, ymx, p_mx, eoutg, "even_out_bwd")
    deout = jnp.concatenate([_mm_tn(att, dymx, 1, False, "dw_eout_att"),
                             _mm_tn(pool, dymx, 1, False, "dw_eout_pool")], axis=0)
    s_d, sb_d = _pair_sums([dw13_1, dw2_1, deout], pos, "l0b")
    du, dpoolw, acc_pool = _pool_bwd(dpool, dmx, pw_b, pool_scale, "pool_bwd")
    dq, dk, dv, dkc, dvc, dbias, got_d = _attn_bwd(q, k, v, k_c, v_c, bias, datt, "attn_bwd",
                                                   ride=_scatter_ride(sb_d))
    drpb = _rpb_grad(dbias, "rpb_grad")
    dx1, dstack, acc_mxi = _even_in_bwd(dx2, x1, dq, dk, dv, du, p_mx, eing,
                                        "even_in_bwd")
    zc = jnp.zeros((C, NA_WIDTH), F32)
    dctx1, dstack_c, accc_mx = _even_in_bwd(jnp.zeros((C, D), F32), ctx1, zc, dkc, dvc, zc,
                                            pc_mx, eing, "even_in_bwd_ctx")
    dein_c = _mm_tn(hn_cx, dstack_c, 4, False, "dw_ein_ctx")
    dein = _mm_tn(hn_mx, dstack, 4, False, "dw_ein", init=dein_c)
    s_e, sb_e = _pair_sums([dein], pos, "ein")
    _, accc_f1, dw13_c, dw2_c, _ = ffn_back(dctx1, ctx0, abc, yc, pc_f1, 0, "ctx")
    sv1, gact1 = ab1
    dx0, dab, dy, hn, acc_f1, _ = _ffn_bwd(dx1, x0, sv1, y1, p_f1, *wf[0], 0, "ffn_bwd_l0a")
    dw13_0, got_e = _mm_tn(hn, dab, 4, False, "dw13_l0a", init=dw13_c, ride=_scatter_ride(sb_e))
    s_f13, sb_f13 = _pair_sums([dw13_0], pos, "l0a_w13")
    dw2_0, got_f13 = _mm_tn(gact1, dy, 2, True, "dw2_l0a", init=dw2_c, ride=_scatter_ride(sb_f13))
    s_f2, sb_f2 = _pair_sums([dw2_0], pos, "l0a_w2")

    z1 = jnp.zeros((1, D), F32)
    dm_lat = jnp.concatenate([acc_f1[0:3], acc_mxi[0:2], acc_mxo[2:3], acc_f2[0:3],
                              acc_g1[0:3], acc_cv[0:3], acc_g2[0:3]], axis=0)
    dm_ctx = jnp.concatenate([accc_f1[0:3], accc_mx[0:2]] + [z1] * 13, axis=0)
    dnorm = jnp.concatenate([acc_f1[3:4] + accc_f1[3:4], acc_mxi[3:4] + accc_mx[3:4], acc_f2[3:4],
                             acc_g1[3:4], acc_cv[3:4], acc_g2[3:4]], axis=0)
    rpb_flat = jnp.pad(drpb.reshape(-1), (0, 4 * D - drpb.size)).reshape(4, D)
    pack3 = jnp.concatenate([dm_lat, dm_ctx, dnorm, acc_cv[4:7], acc_head[0:1], pad(acc_pool[0:1]), z1,
                             dpoolw.reshape(-1, D), rpb_flat, jnp.zeros((4, D), F32)], axis=0)
    g3 = _small_all_gather(pack3, "ag_small")
    tot = _sum_devices(g3, "sum_small")
    dm_all = jnp.concatenate([g3[:, 0:18].reshape(8, 2, N_MOD * D).transpose(1, 0, 2),
                              tot[18:36].reshape(2, 1, N_MOD * D), jnp.zeros((2, 7, N_MOD * D), F32)], axis=1)
    dm_loc = lax.dynamic_slice_in_dim(dm_all, chip * Nm, Nm, axis=2)
    g_mod_w, dsilu = _mod_bwd(cond, dm_loc, mod_w, "mod_bwd")
    g4 = _small_all_gather(dsilu, "ag_dsilu")
    g_mod_b, g_c_ctx = _mod_small_grads(dm_all, cond, g4, "mod_small")
    g_mod_b = g_mod_b.reshape(2, N_MOD * D)
    g_c_ctx = g_c_ctx.reshape(D)
    g_norm_full = tot[36:42].reshape(2, 3, D)
    g_norm = lax.dynamic_slice_in_dim(g_norm_full, chip * Ds, Ds, axis=2)
    g_conv_w = lax.dynamic_slice_in_dim(tot[42:45], chip * Ds, Ds, axis=1).reshape(1, 3, Ds)
    g_final = tot[45]
    g_pscale = tot[46:47, :pool_scale.shape[1]]
    g_poolw = tot[48:112].reshape(pool_w.shape)
    g_rpb = tot[112:116].reshape(-1)[:na_rpb.size].reshape(na_rpb.shape)

    adamw_mod_w, got_f2 = _adamw(mod_w, g_mod_w, m_mod_w, v_mod_w, "adamw_mod_w", ride=_scatter_ride(sb_f2))
    r13 = _joins([s_f13[0], s_d[0], s_c[0], s_a[0]], [got_f13[0], got_d[0], got_c[0], got_a[0]], pos, "w13")
    r2 = _joins([s_f2[0], s_d[1], s_c[1], s_a[1]], [got_f2[0], got_d[1], got_c[1], got_a[1]], pos, "w2")
    r_eout, r_cout = _joins([s_d[2], s_b[1]], [got_d[2], got_b[1]], pos, "out")
    (r_ein,) = _joins(s_e, got_e, pos, "ein")
    (r_cin,) = _joins(s_b[:1], got_b[:1], pos, "cin")
    g_w13 = jnp.stack(r13).reshape(ffn_w13.shape)
    g_w2 = jnp.stack(r2).reshape(ffn_w2.shape)
    g_ein, g_eout, g_cin, g_cout = r_ein[None], r_eout[None], r_cin[None], r_cout[None]

    grads = [g_c_ctx, g_mod_w, g_mod_b, g_norm, g_w13, g_w2, g_ein, g_eout, g_rpb, g_poolw, g_pscale, g_cin,
             g_conv_w, g_cout, g_final]
    weights = [c_ctx, mod_w, mod_b, norm_g, ffn_w13, ffn_w2, even_w_in, even_w_out, na_rpb, pool_w, pool_scale,
               conv_w_in, conv_w, conv_w_out, final_g]
    ms = [m_c_ctx, m_mod_w, m_mod_b, m_norm_g, m_ffn_w13, m_ffn_w2, m_even_w_in, m_even_w_out, m_na_rpb, m_pool_w,
          m_pool_scale, m_conv_w_in, m_conv_w, m_conv_w_out, m_final_g]
    vs = [v_c_ctx, v_mod_w, v_mod_b, v_norm_g, v_ffn_w13, v_ffn_w2, v_even_w_in, v_even_w_out, v_na_rpb, v_pool_w,
          v_pool_scale, v_conv_w_in, v_conv_w, v_conv_w_out, v_final_g]
    names = ["c_ctx", "mod_w", "mod_b", "norm_g", "ffn_w13", "ffn_w2", "even_w_in", "even_w_out", "na_rpb", "pool_w",
             "pool_scale", "conv_w_in", "conv_w", "conv_w_out", "final_g"]
    deltas, new_m, new_v = [], [], []
    for n, w, g, m, vv in zip(names, weights, grads, ms, vs):
        g = g.reshape(w.shape)
        if n == "mod_w":
            d, mn, vn = adamw_mod_w
        elif w.ndim == 1:
            d, mn, vn = (t.reshape(w.shape) for t in _adamw(w[None], g[None], m[None], vv[None], f"adamw_{n}"))
        else:
            d, mn, vn = _adamw(w, g, m, vv, f"adamw_{n}")
        deltas.append(d)
        new_m.append(mn)
        new_v.append(vn)
    grads = [g.reshape(w.shape) for g, w in zip(grads, weights)]
    return (loss, dx0[None], *grads, *deltas, *new_m, *new_v)
```
